```python
import jax, jax.numpy as jnp
from jax import lax
import numpy as np

D_MODEL = 1024
BATCH = 8
SEQ = 2048
DEPTH = 2
DEC_BATCH = 128
DEC_SEQ = 4
PAST_LEN = 16384
PAGE_SIZE = 128

CHUNK = 128
D_A = D_MODEL
H_A = 4
DH_A = D_A // H_A
D_RNN = D_MODEL
H_B = 8
BLK_B = D_RNN // H_B
CONV_W = 4
LRU_C = 8.0
N_GROUPS = 4
E_PER_GROUP = 8
N_EXPERTS = N_GROUPS * E_PER_GROUP
D_EXPERT = 256
TOP_K_INNER = 2
EPS = 1e-6
D_IN = 2 * D_A + D_RNN + 2 * D_MODEL
SPLITS = (D_A, 2 * D_A, 2 * D_A + D_RNN, 2 * D_A + D_RNN + D_MODEL)

kernel_name = 'hybrid_gmlp_rglru_hmoe_step'


def rms_norm(x, g):
    xf = x.astype(jnp.float32)
    y = xf * lax.rsqrt(jnp.mean(xf * xf, axis=-1, keepdims=True) + EPS)
    return (y * g.astype(jnp.float32)).astype(x.dtype)


def layer_norm(x, g, b):
    xf = x.astype(jnp.float32)
    xc = xf - jnp.mean(xf, axis=-1, keepdims=True)
    var = jnp.mean(xc * xc, axis=-1, keepdims=True)
    return (xc * lax.rsqrt(var + EPS) * g.astype(jnp.float32) + b.astype(jnp.float32)).astype(x.dtype)


def chunk_spatial_gating(u, v, w_s, b_s):
    n, s, _ = v.shape
    n_chunks = -(-s // CHUNK)
    pad = n_chunks * CHUNK - s
    vc = jnp.pad(v, ((0, 0), (0, pad), (0, 0))).reshape(n, n_chunks, CHUNK, H_A, DH_A)
    causal = jnp.tril(jnp.ones((CHUNK, CHUNK), dtype=bool))
    w = jnp.where(causal, w_s, jnp.zeros_like(w_s)).astype(v.dtype)
    z = jnp.einsum('hts,bcshk->bcthk', w, vc) + jnp.transpose(b_s).astype(v.dtype)[None, None, :, :, None]
    z = z.reshape(n, n_chunks * CHUNK, D_A)[:, :s]
    return u * z


def causal_depthwise_conv(x, buf, w, b):
    s = x.shape[1]
    xp = jnp.concatenate([buf.astype(x.dtype), x], axis=1)
    y = b.astype(x.dtype) + xp[:, 0:s] * w[0]
    for k in range(1, CONV_W):
        y = y + xp[:, k:k + s] * w[k]
    return y, xp[:, s:]


def block_diag_linear(x, w, b):
    xb = x.reshape(x.shape[:-1] + (H_B, BLK_B))
    y = jnp.einsum('bshi,hij->bshj', xb, w) + b
    return y.reshape(x.shape)


def rg_lru(x, h0, lam, w_a, b_a, w_x, b_x):
    r = jax.nn.sigmoid(block_diag_linear(x, w_a, b_a)).astype(jnp.float32)
    i = jax.nn.sigmoid(block_diag_linear(x, w_x, b_x))
    log_a = -LRU_C * r * jax.nn.softplus(-lam.astype(jnp.float32))
    a = jnp.exp(log_a)
    mult = jnp.sqrt(-jnp.expm1(2.0 * log_a))
    if h0 is None:
        mult = mult.at[:, 0].set(1.0)
    bt = mult * (i * x).astype(jnp.float32)
    if h0 is not None:
        bt = bt.at[:, 0].add(a[:, 0] * h0.astype(jnp.float32))

    def combine(left, right):
        a1, b1 = left
        a2, b2 = right
        return a1 * a2, a2 * b1 + b2

    _, h = lax.associative_scan(combine, (a, bt), axis=1)
    return h.astype(x.dtype), h[:, -1].astype(x.dtype)


def mixer_sublayer(x, h0, conv_buf, norm_g, w_in, ln_g, ln_b, w_s, b_s, conv_w, conv_b,
                   lam, w_a, b_a, w_x, b_x, w_out):
    s = x.shape[1]
    xn = rms_norm(x, norm_g)
    proj = xn @ w_in
    u, v, xr, g_a, g_b = jnp.split(proj, SPLITS, axis=-1)
    u = jax.nn.gelu(u)
    v = layer_norm(jax.nn.gelu(v), ln_g, ln_b)
    y_a = chunk_spatial_gating(u, v, w_s, b_s)
    if conv_buf is None:
        conv_buf = jnp.zeros((x.shape[0], CONV_W - 1, D_RNN), x.dtype)
    xc, new_buf = causal_depthwise_conv(xr, conv_buf, conv_w, conv_b)
    y_b, h_last = rg_lru(xc, h0, lam, w_a, b_a, w_x, b_x)
    merged = jax.nn.sigmoid(g_a) * y_a + jax.nn.sigmoid(g_b) * y_b
    v_rows = v[:, ((s - 1) // CHUNK) * CHUNK:]
    return x + merged @ w_out, h_last, new_buf, v_rows


def hier_moe(x, w_rg, b_rg, w_re, b_re, w_gate, w_up, w_down):
    shp = x.shape
    xf = x.reshape(-1, D_MODEL)
    pg = jax.nn.softmax((xf @ w_rg + b_rg).astype(jnp.float32), axis=-1)
    pg_top, g_idx = lax.top_k(pg, 1)
    e_logits = (xf @ w_re + b_re).astype(jnp.float32).reshape(-1, N_GROUPS, E_PER_GROUP)
    e_in = jnp.take_along_axis(e_logits, g_idx[:, :, None], axis=1)[:, 0]
    pe = jax.nn.softmax(e_in, axis=-1)
    pe_top, e_idx = lax.top_k(pe, TOP_K_INNER)
    w = pg_top * pe_top / jnp.sum(pe_top, axis=-1, keepdims=True)
    expert = g_idx * E_PER_GROUP + e_idx
    gates = jnp.sum(jax.nn.one_hot(expert, N_EXPERTS, dtype=jnp.float32) * w[..., None], axis=1)
    h = jax.nn.silu(jnp.einsum('td,edf->tef', xf, w_gate)) * jnp.einsum('td,edf->tef', xf, w_up)
    h = h * gates.astype(h.dtype)[..., None]
    y = jnp.einsum('tef,efd->td', h, w_down)
    return y.reshape(shp)


def setup_inputs(seed: int = 0) -> dict:
    key = jax.random.key(seed)
    ks = jax.random.split(key, 32)
    f32 = jnp.float32
    nrm = lambda k, shape, scale: jax.random.normal(k, shape, f32) * scale
    a_c = jax.random.uniform(ks[10], (DEPTH, D_RNN), f32, 0.9, 0.999)
    a0 = a_c ** (1.0 / LRU_C)
    return {
        'x_prompt': nrm(ks[0], (BATCH, SEQ, D_MODEL), 1.0),
        'x_sample': nrm(ks[1], (DEC_BATCH, DEC_SEQ, D_MODEL), 1.0),
        'state_lru_h': nrm(ks[2], (DEPTH, DEC_BATCH, D_RNN), 0.5),
        'state_conv': nrm(ks[3], (DEPTH, DEC_BATCH, CONV_W - 1, D_RNN), 1.0),
        'norm1': 1.0 + nrm(ks[4], (DEPTH, D_MODEL), 0.05),
        'w_in': nrm(ks[5], (DEPTH, D_MODEL, D_IN), D_MODEL ** -0.5),
        'ln_g': 1.0 + nrm(ks[6], (DEPTH, D_A), 0.05),
        'ln_b': nrm(ks[7], (DEPTH, D_A), 0.05),
        'w_s': nrm(ks[8], (DEPTH, H_A, CHUNK, CHUNK), CHUNK ** -0.5),
        'b_s': 1.0 + nrm(ks[9], (DEPTH, H_A, CHUNK), 0.1),
        'conv_w': nrm(ks[11], (DEPTH, CONV_W, D_RNN), CONV_W ** -0.5),
        'conv_b': nrm(ks[12], (DEPTH, D_RNN), 0.02),
        'lru_lambda': jnp.log(a0) - jnp.log1p(-a0),
        'w_rg_a': nrm(ks[13], (DEPTH, H_B, BLK_B, BLK_B), BLK_B ** -0.5),
        'b_rg_a': nrm(ks[14], (DEPTH, H_B, BLK_B), 0.02),
        'w_rg_x': nrm(ks[15], (DEPTH, H_B, BLK_B, BLK_B), BLK_B ** -0.5),
        'b_rg_x': nrm(ks[16], (DEPTH, H_B, BLK_B), 0.02),
        'w_out': nrm(ks[17], (DEPTH, D_MODEL, D_MODEL), D_MODEL ** -0.5),
        'norm2': 1.0 + nrm(ks[18], (DEPTH, D_MODEL), 0.05),
        'w_route_group': nrm(ks[19], (DEPTH, D_MODEL, N_GROUPS), D_MODEL ** -0.5),
        'b_route_group': nrm(ks[20], (DEPTH, N_GROUPS), 0.01),
        'w_route_expert': nrm(ks[21], (DEPTH, D_MODEL, N_EXPERTS), D_MODEL ** -0.5),
        'b_route_expert': nrm(ks[22], (DEPTH, N_EXPERTS), 0.01),
        'w_gate': nrm(ks[23], (DEPTH, N_EXPERTS, D_MODEL, D_EXPERT), D_MODEL ** -0.5),
        'w_up': nrm(ks[24], (DEPTH, N_EXPERTS, D_MODEL, D_EXPERT), D_MODEL ** -0.5),
        'w_down': nrm(ks[25], (DEPTH, N_EXPERTS, D_EXPERT, D_MODEL), D_EXPERT ** -0.5),
        'norm_f': 1.0 + nrm(ks[26], (D_MODEL,), 0.05),
    }


def reference(x_prompt, x_sample, state_lru_h, state_conv, norm1, w_in, ln_g, ln_b, w_s, b_s,
              conv_w, conv_b, lru_lambda, w_rg_a, b_rg_a, w_rg_x, b_rg_x, w_out, norm2,
              w_route_group, b_route_group, w_route_expert, b_route_expert,
              w_gate, w_up, w_down, norm_f):
    yp, ys = x_prompt, x_sample
    hp, cp, vp, hs, cs, vs = [], [], [], [], [], []
    for l in range(DEPTH):
        mix_w = (norm1[l], w_in[l], ln_g[l], ln_b[l], w_s[l], b_s[l], conv_w[l], conv_b[l],
                 lru_lambda[l], w_rg_a[l], b_rg_a[l], w_rg_x[l], b_rg_x[l], w_out[l])
        moe_w = (w_route_group[l], b_route_group[l], w_route_expert[l], b_route_expert[l],
                 w_gate[l], w_up[l], w_down[l])
        yp, h, c, v = mixer_sublayer(yp, None, None, *mix_w)
        hp.append(h)
        cp.append(c)
        vp.append(v)
        ys, h, c, v = mixer_sublayer(ys, state_lru_h[l], state_conv[l], *mix_w)
        hs.append(h)
        cs.append(c)
        vs.append(v)
        yp = yp + hier_moe(rms_norm(yp, norm2[l]), *moe_w)
        ys = ys + hier_moe(rms_norm(ys, norm2[l]), *moe_w)
    yp = rms_norm(yp, norm_f)
    ys = rms_norm(ys, norm_f)
    return (yp, ys, jnp.stack(hp), jnp.stack(cp), jnp.stack(vp), jnp.stack(hs), jnp.stack(cs), jnp.stack(vs))
```

```python
import functools

import jax
import jax.numpy as jnp
from jax import lax
from jax.experimental import pallas as pl
from jax.experimental.pallas import tpu as pltpu

F32 = jnp.float32
BF16 = jnp.bfloat16

CHUNK = 128
H_A = 4
H_B = 8
CONV_W = 4
LRU_C = 8.0
N_GROUPS = 4
E_PER_GROUP = 8
N_EXPERTS = N_GROUPS * E_PER_GROUP
EPS = 1e-6

LANES = 128
SUBLANES = 8
SCAN_PITCH = CHUNK + SUBLANES
ROUTE_LANES = 128
VMEM_LIMIT = 60 * 1024 * 1024


def _dot(a, b):
    return jnp.dot(a, b, preferred_element_type=F32)


def _sigmoid(x):
    return 0.5 * (jnp.tanh(0.5 * x) + 1.0)


def _gelu_tanh(x):
    c = 0.7978845608028654
    return 0.5 * x * (1.0 + jnp.tanh(c * (x + 0.044715 * (x * x * x))))


def _rms_norm(x, g):
    return x * lax.rsqrt(jnp.mean(x * x, axis=-1, keepdims=True) + EPS) * g


def _layer_norm(x, g, b):
    xc = x - jnp.mean(x, axis=-1, keepdims=True)
    var = jnp.mean(xc * xc, axis=-1, keepdims=True)
    return xc * lax.rsqrt(var + EPS) * g + b


def _softplus(x):
    return jnp.maximum(x, 0.0) + jnp.log(1.0 + jnp.exp(-jnp.abs(x)))


def _lru_gates(xc, wbd_ref, b_a, b_x, lam_c):
    d = xc.shape[-1]
    xcb = xc.astype(BF16)
    r_parts, i_parts = [], []
    for q in range(d // 256):
        ri = _dot(xcb[:, q * 256:(q + 1) * 256], wbd_ref[q])
        r_parts.append(ri[:, :256])
        i_parts.append(ri[:, 256:])
    r = _sigmoid(jnp.concatenate(r_parts, axis=-1) + b_a)
    i = _sigmoid(jnp.concatenate(i_parts, axis=-1) + b_x)
    a = jnp.exp(lam_c * r)
    mult = jnp.sqrt(1.0 - a * a)
    return a, mult, i * xc


def _mixer_prompt_kernel(x_ref, norm_ref, win_ref, lng_ref, lnb_ref, ws_ref, bs_ref,
                         cw_ref, cb_ref, lam_ref, wbd_ref, ba_ref, bx_ref, wout_ref,
                         y_ref, h_ref, conv_ref, v_ref,
                         xn_s, xr_s, a_s, b_s, ya_s, hc_s):
    i = pl.program_id(0)
    last = pl.num_programs(0) - 1
    nseq, _, d = x_ref.shape
    nlt = d // LANES
    pair = 2 * CHUNK

    @pl.when(i == 0)
    def _():
        xr_s[:, 0:SUBLANES, :] = jnp.zeros((nseq, SUBLANES, d), F32)
        hc_s[...] = jnp.zeros_like(hc_s)

    g1 = norm_ref[...]

    def norm_body(s, c):
        row = pl.multiple_of(s * CHUNK, CHUNK)
        xn_s[pl.ds(row, CHUNK), :] = _rms_norm(x_ref[s], g1).astype(BF16)
        return c

    lax.fori_loop(0, nseq, norm_body, 0)

    lam_c = -LRU_C * _softplus(-lam_ref[...])
    is_first = i == 0

    def branch_body(p, c):
        row = pl.multiple_of(p * pair, pair)
        xnb = xn_s[pl.ds(row, pair), :]
        v = _layer_norm(_gelu_tanh(_dot(xnb, win_ref[:, d:2 * d])), lng_ref[...], lnb_ref[...])

        @pl.when(i == last)
        def _():
            v_ref[2 * p] = v[:CHUNK]
            v_ref[2 * p + 1] = v[CHUNK:]

        vb = v.astype(BF16)
        dh = d // H_A
        z_rows = []
        for k in range(2):
            zs = [_dot(ws_ref[h], vb[k * CHUNK:(k + 1) * CHUNK, h * dh:(h + 1) * dh])
                  for h in range(H_A)]
            z_rows.append(jnp.concatenate(zs, axis=-1) + bs_ref[...])
        z = jnp.concatenate(z_rows, axis=0)
        ya = _gelu_tanh(_dot(xnb, win_ref[:, 0:d])) * z
        ga = _sigmoid(_dot(xnb, win_ref[:, 3 * d:4 * d]))
        ya_s[pl.ds(row, pair), :] = (ga * ya).astype(BF16)
        xr = _dot(xnb, win_ref[:, 2 * d:3 * d])
        xcs = []
        for k in range(2):
            s = 2 * p + k
            xr_s[s, SUBLANES:SUBLANES + CHUNK, :] = xr[k * CHUNK:(k + 1) * CHUNK]
            xc = cb_ref[...] + xr[k * CHUNK:(k + 1) * CHUNK] * cw_ref[CONV_W - 1:CONV_W, :]
            for j in range(1, CONV_W):
                xc = xc + xr_s[s, SUBLANES - j:SUBLANES - j + CHUNK, :] * cw_ref[CONV_W - 1 - j:CONV_W - j, :]
            xcs.append(xc)
        xc = jnp.concatenate(xcs, axis=0)
        a, mult, ix = _lru_gates(xc, wbd_ref, ba_ref[...], bx_ref[...], lam_c)
        rows = lax.broadcasted_iota(jnp.int32, (pair, 1), 0)
        reset = jnp.logical_and(is_first, (rows % CHUNK) == 0)
        bt = jnp.where(reset, 1.0, mult) * ix
        for k in range(2):
            srow = pl.multiple_of((2 * p + k) * SCAN_PITCH, SUBLANES)
            for j in range(nlt):
                a_s[j, pl.ds(srow, CHUNK), :] = a[k * CHUNK:(k + 1) * CHUNK, j * LANES:(j + 1) * LANES]
                b_s[j, pl.ds(srow, CHUNK), :] = bt[k * CHUNK:(k + 1) * CHUNK, j * LANES:(j + 1) * LANES]
        return c

    lax.fori_loop(0, nseq // 2, branch_body, 0)

    def scan_body(t, hs):
        out = []
        for j in range(nlt):
            a_t = a_s[j, pl.ds(t, nseq, stride=SCAN_PITCH), :]
            b_t = b_s[j, pl.ds(t, nseq, stride=SCAN_PITCH), :]
            h = a_t * hs[j] + b_t
            b_s[j, pl.ds(t, nseq, stride=SCAN_PITCH), :] = h
            out.append(h)
        return tuple(out)

    h0 = tuple(hc_s[:, j * LANES:(j + 1) * LANES] for j in range(nlt))
    hs = lax.fori_loop(0, CHUNK, scan_body, h0)
    hfin = jnp.concatenate(hs, axis=-1)
    hc_s[...] = hfin
    h_ref[...] = hfin

    def out_body(p, c):
        row = pl.multiple_of(p * pair, pair)
        xnb = xn_s[pl.ds(row, pair), :]
        gb = _sigmoid(_dot(xnb, win_ref[:, 4 * d:5 * d]))
        hrows = []
        for k in range(2):
            srow = pl.multiple_of((2 * p + k) * SCAN_PITCH, SUBLANES)
            hrows.append(jnp.concatenate([b_s[j, pl.ds(srow, CHUNK), :] for j in range(nlt)], axis=-1))
        h = jnp.concatenate(hrows, axis=0)
        merged = (ya_s[pl.ds(row, pair), :].astype(F32) + gb * h).astype(BF16)
        o = _dot(merged, wout_ref[...])
        y_ref[2 * p] = x_ref[2 * p] + o[:CHUNK]
        y_ref[2 * p + 1] = x_ref[2 * p + 1] + o[CHUNK:]
        return c

    lax.fori_loop(0, nseq // 2, out_body, 0)

    @pl.when(i == last)
    def _():
        conv_ref[...] = xr_s[:, SUBLANES + CHUNK - (CONV_W - 1):SUBLANES + CHUNK, :]

    @pl.when(i != last)
    def _():
        xr_s[:, 0:SUBLANES, :] = xr_s[:, CHUNK:CHUNK + SUBLANES, :]


def _const_spec(shape):
    nd = len(shape)
    return pl.BlockSpec(shape, lambda i, _n=nd: (0,) * _n, pipeline_mode=pl.Buffered(1))


def _mixer_prompt(x, wts):
    nseq, seq, d = x.shape
    assert seq % CHUNK == 0 and nseq == SUBLANES and d % 256 == 0
    n_chunks = seq // CHUNK
    nlt = d // LANES
    weights = (wts['norm1'], wts['w_in'], wts['ln_g'], wts['ln_b'], wts['ws'], wts['bs'],
               wts['conv_w'], wts['conv_b'], wts['lam'], wts['wbd'], wts['b_a'], wts['b_x'],
               wts['w_out'])
    x_spec = pl.BlockSpec((nseq, CHUNK, d), lambda i: (0, i, 0))
    out_shape = (
        jax.ShapeDtypeStruct((nseq, seq, d), F32),
        jax.ShapeDtypeStruct((nseq, d), F32),
        jax.ShapeDtypeStruct((nseq, CONV_W - 1, d), F32),
        jax.ShapeDtypeStruct((nseq, CHUNK, d), F32),
    )
    out_specs = (
        x_spec,
        _const_spec((nseq, d)),
        _const_spec((nseq, CONV_W - 1, d)),
        _const_spec((nseq, CHUNK, d)),
    )
    scratch = [
        pltpu.VMEM((nseq * CHUNK, d), BF16),
        pltpu.VMEM((nseq, CHUNK + SUBLANES, d), F32),
        pltpu.VMEM((nlt, nseq * SCAN_PITCH, LANES), F32),
        pltpu.VMEM((nlt, nseq * SCAN_PITCH, LANES), F32),
        pltpu.VMEM((nseq * CHUNK, d), BF16),
        pltpu.VMEM((nseq, d), F32),
    ]
    return pl.pallas_call(
        _mixer_prompt_kernel,
        grid=(n_chunks,),
        in_specs=[x_spec] + [_const_spec(w.shape) for w in weights],
        out_specs=out_specs,
        out_shape=out_shape,
        scratch_shapes=scratch,
        compiler_params=pltpu.CompilerParams(
            dimension_semantics=("arbitrary",), vmem_limit_bytes=VMEM_LIMIT),
        name="mixer_prompt",
    )(x, *weights)


def _mixer_sample_kernel(x_ref, h0_ref, cst_ref, norm_ref, win_ref, lng_ref, lnb_ref, wsc_ref,
                         bsc_ref, cw_ref, cb_ref, lam_ref, wbd_ref, ba_ref, bx_ref, wout_ref,
                         y_ref, h_ref, conv_ref, v_ref):
    nt, ns, d = x_ref.shape
    lam_c = -LRU_C * _softplus(-lam_ref[...])
    g1 = norm_ref[...]
    h = h0_ref[...]
    xp = [cst_ref[k] for k in range(CONV_W - 1)]
    for t in range(nt):
        x = x_ref[t]
        xnb = _rms_norm(x, g1).astype(BF16)
        v = _layer_norm(_gelu_tanh(_dot(xnb, win_ref[:, d:2 * d])), lng_ref[...], lnb_ref[...])
        v_ref[t] = v
        z = bsc_ref[t:t + 1, :]
        for s in range(t + 1):
            z = z + wsc_ref[t, s:s + 1, :] * v_ref[s]
        ya = _gelu_tanh(_dot(xnb, win_ref[:, 0:d])) * z
        ga = _sigmoid(_dot(xnb, win_ref[:, 3 * d:4 * d]))
        xr = _dot(xnb, win_ref[:, 2 * d:3 * d])
        xp.append(xr)
        xc = cb_ref[...]
        for k in range(CONV_W):
            xc = xc + xp[t + k] * cw_ref[k:k + 1, :]
        a, mult, ix = _lru_gates(xc, wbd_ref, ba_ref[...], bx_ref[...], lam_c)
        h = a * h + mult * ix
        gb = _sigmoid(_dot(xnb, win_ref[:, 4 * d:5 * d]))
        merged = (ga * ya + gb * h).astype(BF16)
        y_ref[t] = x + _dot(merged, wout_ref[...])
    h_ref[...] = h
    for k in range(CONV_W - 1):
        conv_ref[k] = xp[nt + k]


def _mixer_sample(x, h0, conv_state, wts, wsc, bsc):
    nt, ns, d = x.shape
    args = (x, h0, conv_state, wts['norm1'], wts['w_in'], wts['ln_g'], wts['ln_b'], wsc, bsc,
            wts['conv_w'], wts['conv_b'], wts['lam'], wts['wbd'], wts['b_a'], wts['b_x'],
            wts['w_out'])
    out_shape = (
        jax.ShapeDtypeStruct((nt, ns, d), F32),
        jax.ShapeDtypeStruct((ns, d), F32),
        jax.ShapeDtypeStruct((CONV_W - 1, ns, d), F32),
        jax.ShapeDtypeStruct((nt, ns, d), F32),
    )
    return pl.pallas_call(
        _mixer_sample_kernel,
        out_shape=out_shape,
        compiler_params=pltpu.CompilerParams(vmem_limit_bytes=VMEM_LIMIT),
        name="mixer_sample",
    )(*args)


def _route(logits):
    lane = lax.broadcasted_iota(jnp.int32, logits.shape, 1)
    neg = jnp.float32(-jnp.inf)
    big = jnp.int32(ROUTE_LANES)

    def argmax_first(vals):
        m = jnp.max(vals, axis=-1, keepdims=True)
        idx = jnp.min(jnp.where(vals == m, lane, big), axis=-1, keepdims=True)
        return m, idx

    gl = jnp.where(lane < N_GROUPS, logits, neg)
    gm, gi = argmax_first(gl)
    pg_top = 1.0 / jnp.sum(jnp.exp(gl - gm), axis=-1, keepdims=True)
    lo = N_GROUPS + gi * E_PER_GROUP
    el = jnp.where(jnp.logical_and(lane >= lo, lane < lo + E_PER_GROUP), logits, neg)
    m1, i1 = argmax_first(el)
    m2, i2 = argmax_first(jnp.where(lane == i1, neg, el))
    e2 = jnp.exp(m2 - m1)
    w1 = pg_top / (1.0 + e2)
    w2 = pg_top * e2 / (1.0 + e2)
    return jnp.where(lane == i1, w1, 0.0) + jnp.where(lane == i2, w2, 0.0)


def _moe_dense_kernel(y_ref, norm_ref, wr_ref, br_ref, wg_ref, wu_ref, wd_ref, nf_ref,
                      o_ref, xn_s, gates_s, acc_s, *, final_norm):
    e = pl.program_id(1)

    @pl.when(e == 0)
    def _():
        xn = _rms_norm(y_ref[...], norm_ref[...])
        xnb = xn.astype(BF16)
        xn_s[...] = xnb
        gates_s[...] = _route(_dot(xnb, wr_ref[...]) + br_ref[...])
        acc_s[...] = jnp.zeros_like(acc_s)

    xnb = xn_s[...]
    lane = lax.broadcasted_iota(jnp.int32, gates_s.shape, 1)
    gate = jnp.sum(jnp.where(lane == N_GROUPS + e, gates_s[...], 0.0), axis=-1, keepdims=True)
    hg = _dot(xnb, wg_ref[0].astype(BF16))
    hu = _dot(xnb, wu_ref[0].astype(BF16))
    h = (hg * _sigmoid(hg) * hu * gate).astype(BF16)
    acc_s[...] += _dot(h, wd_ref[0].astype(BF16))

    @pl.when(e == pl.num_programs(1) - 1)
    def _():
        out = y_ref[...] + acc_s[...]
        if final_norm:
            out = _rms_norm(out, nf_ref[...])
        o_ref[...] = out


def _moe_dense(y, wts, norm_f, final_norm, tile):
    n, d = y.shape
    assert n % tile == 0
    f = wts['w_gate'].shape[-1]
    tok = pl.BlockSpec((tile, d), lambda i, e: (i, 0))
    cst = lambda shape: pl.BlockSpec(shape, lambda i, e: (0,) * len(shape))
    return pl.pallas_call(
        functools.partial(_moe_dense_kernel, final_norm=final_norm),
        grid=(n // tile, N_EXPERTS),
        in_specs=[tok, cst((1, d)), cst((d, ROUTE_LANES)), cst((1, ROUTE_LANES)),
                  pl.BlockSpec((1, d, f), lambda i, e: (e, 0, 0)),
                  pl.BlockSpec((1, d, f), lambda i, e: (e, 0, 0)),
                  pl.BlockSpec((1, f, d), lambda i, e: (e, 0, 0)),
                  cst((1, d))],
        out_specs=tok,
        out_shape=jax.ShapeDtypeStruct((n, d), F32),
        scratch_shapes=[pltpu.VMEM((tile, d), BF16), pltpu.VMEM((tile, ROUTE_LANES), F32),
                        pltpu.VMEM((tile, d), F32)],
        compiler_params=pltpu.CompilerParams(
            dimension_semantics=("arbitrary", "arbitrary"), vmem_limit_bytes=VMEM_LIMIT),
        name="moe_dense",
    )(y, wts['norm2'], wts['w_route'], wts['b_route'], wts['w_gate'], wts['w_up'],
      wts['w_down'], norm_f)


def _prep_layer(l, norm1, w_in, ln_g, ln_b, w_s, b_s, conv_w, conv_b, lru_lambda, w_rg_a, b_rg_a,
                w_rg_x, b_rg_x, w_out, norm2, w_route_group, b_route_group, w_route_expert,
                b_route_expert, w_gate, w_up, w_down, n_sample_t):
    d = w_in.shape[1]
    dh = d // H_A
    row = lambda a: a.reshape(1, -1).astype(F32)
    causal = jnp.tril(jnp.ones((CHUNK, CHUNK), dtype=bool))
    ws = jnp.where(causal, w_s[l], 0.0)
    blk = d // H_B

    def pair_bd(w):
        z = jnp.zeros((blk, blk), w.dtype)
        return jnp.stack([jnp.block([[w[2 * q], z], [z, w[2 * q + 1]]]) for q in range(H_B // 2)])

    wbd = jnp.concatenate([pair_bd(w_rg_a[l]), pair_bd(w_rg_x[l])], axis=-1).astype(BF16)
    w_route = jnp.zeros((d, ROUTE_LANES), F32)
    w_route = w_route.at[:, :N_GROUPS].set(w_route_group[l])
    w_route = w_route.at[:, N_GROUPS:N_GROUPS + N_EXPERTS].set(w_route_expert[l])
    b_route = jnp.zeros((1, ROUTE_LANES), F32)
    b_route = b_route.at[0, :N_GROUPS].set(b_route_group[l])
    b_route = b_route.at[0, N_GROUPS:N_GROUPS + N_EXPERTS].set(b_route_expert[l])
    wts = dict(
        norm1=row(norm1[l]), w_in=w_in[l].astype(BF16), ln_g=row(ln_g[l]), ln_b=row(ln_b[l]),
        ws=ws.astype(BF16), bs=jnp.repeat(b_s[l].T, dh, axis=1).astype(F32),
        conv_w=conv_w[l].astype(F32), conv_b=row(conv_b[l]), lam=row(lru_lambda[l]),
        wbd=wbd, b_a=row(b_rg_a[l]), b_x=row(b_rg_x[l]), w_out=w_out[l].astype(BF16),
        norm2=row(norm2[l]), w_route=w_route.astype(BF16), b_route=b_route,
        w_gate=w_gate[l], w_up=w_up[l], w_down=w_down[l],
    )
    nt = n_sample_t
    wsc = jnp.repeat(jnp.transpose(ws[:, :nt, :nt], (1, 2, 0)), dh, axis=2).astype(F32)
    bsc = jnp.repeat(b_s[l].T[:nt], dh, axis=1).astype(F32)
    return wts, wsc, bsc


def kernel(x_prompt, x_sample, state_lru_h, state_conv, norm1, w_in, ln_g, ln_b, w_s, b_s, conv_w, conv_b, lru_lambda, w_rg_a, b_rg_a, w_rg_x, b_rg_x, w_out, norm2, w_route_group, b_route_group, w_route_expert, b_route_expert, w_gate, w_up, w_down, norm_f):
    depth = w_in.shape[0]
    nb, seq, d = x_prompt.shape
    ns, nt, _ = x_sample.shape
    assert nt <= CHUNK
    nf = norm_f.reshape(1, d).astype(F32)
    yp = x_prompt
    ys = jnp.transpose(x_sample, (1, 0, 2))
    hp, cp, vp, hs, cs, vs = [], [], [], [], [], []
    for l in range(depth):
        wts, wsc, bsc = _prep_layer(
            l, norm1, w_in, ln_g, ln_b, w_s, b_s, conv_w, conv_b, lru_lambda, w_rg_a, b_rg_a,
            w_rg_x, b_rg_x, w_out, norm2, w_route_group, b_route_group, w_route_expert,
            b_route_expert, w_gate, w_up, w_down, nt)
        final = l == depth - 1
        yp, h, c, v = _mixer_prompt(yp, wts)
        hp.append(h)
        cp.append(c)
        vp.append(v)
        ys, h, c, v = _mixer_sample(ys, state_lru_h[l], jnp.transpose(state_conv[l], (1, 0, 2)),
                                    wts, wsc, bsc)
        hs.append(h)
        cs.append(jnp.transpose(c, (1, 0, 2)))
        vs.append(jnp.transpose(v, (1, 0, 2)))
        yp = _moe_dense(yp.reshape(nb * seq, d), wts, nf, final, 1024).reshape(nb, seq, d)
        ys = _moe_dense(ys.reshape(nt * ns, d), wts, nf, final, nt * ns).reshape(nt, ns, d)
    ys = jnp.transpose(ys, (1, 0, 2))
    return (yp, ys, jnp.stack(hp), jnp.stack(cp), jnp.stack(vp), jnp.stack(hs), jnp.stack(cs),
            jnp.stack(vs))
```

```python
import functools

import jax
import jax.numpy as jnp
from jax import lax
from jax.experimental import pallas as pl
from jax.experimental.pallas import tpu as pltpu

F32 = jnp.float32
BF16 = jnp.bfloat16

CHUNK = 128
H_A = 4
H_B = 8
CONV_W = 4
LRU_C = 8.0
N_GROUPS = 4
E_PER_GROUP = 8
N_EXPERTS = N_GROUPS * E_PER_GROUP
EPS = 1e-6

LANES = 128
SUBLANES = 8
SCAN_PITCH = CHUNK + SUBLANES
ROUTE_LANES = 128
MOE_TILE = 1024
MOE_BLOCK = 256
VMEM_LIMIT = 60 * 1024 * 1024


def _dot(a, b):
    return jnp.dot(a, b, preferred_element_type=F32)


def _sigmoid(x):
    return 0.5 * (jnp.tanh(0.5 * x) + 1.0)


def _gelu_tanh(x):
    c = 0.7978845608028654
    return 0.5 * x * (1.0 + jnp.tanh(c * (x + 0.044715 * (x * x * x))))


def _rms_norm(x, g):
    return x * lax.rsqrt(jnp.mean(x * x, axis=-1, keepdims=True) + EPS) * g


def _layer_norm(x, g, b):
    xc = x - jnp.mean(x, axis=-1, keepdims=True)
    var = jnp.mean(xc * xc, axis=-1, keepdims=True)
    return xc * lax.rsqrt(var + EPS) * g + b


def _softplus(x):
    return jnp.maximum(x, 0.0) + jnp.log(1.0 + jnp.exp(-jnp.abs(x)))


def _lru_gates(xc, wbd_ref, b_a, b_x, lam_c):
    d = xc.shape[-1]
    xcb = xc.astype(BF16)
    r_parts, i_parts = [], []
    for q in range(d // 256):
        ri = _dot(xcb[:, q * 256:(q + 1) * 256], wbd_ref[q])
        r_parts.append(ri[:, :256])
        i_parts.append(ri[:, 256:])
    r = _sigmoid(jnp.concatenate(r_parts, axis=-1) + b_a)
    i = _sigmoid(jnp.concatenate(i_parts, axis=-1) + b_x)
    a = jnp.exp(lam_c * r)
    mult = jnp.sqrt(1.0 - a * a)
    return a, mult, i * xc


def _mixer_prompt_kernel(x_ref, norm_ref, win_ref, lng_ref, lnb_ref, ws_ref, bs_ref,
                         cw_ref, cb_ref, lam_ref, wbd_ref, ba_ref, bx_ref, wout_ref,
                         y_ref, h_ref, conv_ref, v_ref,
                         xn_s, xr_s, a_s, b_s, ya_s, hc_s):
    i = pl.program_id(0)
    last = pl.num_programs(0) - 1
    nseq, _, d = x_ref.shape
    nlt = d // LANES
    pair = 2 * CHUNK

    @pl.when(i == 0)
    def _():
        xr_s[:, 0:SUBLANES, :] = jnp.zeros((nseq, SUBLANES, d), F32)
        hc_s[...] = jnp.zeros_like(hc_s)

    g1 = norm_ref[...]

    def norm_body(s, c):
        row = pl.multiple_of(s * CHUNK, CHUNK)
        xn_s[pl.ds(row, CHUNK), :] = _rms_norm(x_ref[s], g1).astype(BF16)
        return c

    lax.fori_loop(0, nseq, norm_body, 0)

    lam_c = -LRU_C * _softplus(-lam_ref[...])
    is_first = i == 0

    def branch_body(p, c):
        row = pl.multiple_of(p * pair, pair)
        xnb = xn_s[pl.ds(row, pair), :]
        v = _layer_norm(_gelu_tanh(_dot(xnb, win_ref[:, d:2 * d])), lng_ref[...], lnb_ref[...])

        @pl.when(i == last)
        def _():
            v_ref[2 * p] = v[:CHUNK]
            v_ref[2 * p + 1] = v[CHUNK:]

        vb = v.astype(BF16)
        dh = d // H_A
        z_rows = []
        for k in range(2):
            zs = [_dot(ws_ref[h], vb[k * CHUNK:(k + 1) * CHUNK, h * dh:(h + 1) * dh])
                  for h in range(H_A)]
            z_rows.append(jnp.concatenate(zs, axis=-1) + bs_ref[...])
        z = jnp.concatenate(z_rows, axis=0)
        ya = _gelu_tanh(_dot(xnb, win_ref[:, 0:d])) * z
        ga = _sigmoid(_dot(xnb, win_ref[:, 3 * d:4 * d]))
        ya_s[pl.ds(row, pair), :] = (ga * ya).astype(BF16)
        xr = _dot(xnb, win_ref[:, 2 * d:3 * d])
        xcs = []
        for k in range(2):
            s = 2 * p + k
            xr_s[s, SUBLANES:SUBLANES + CHUNK, :] = xr[k * CHUNK:(k + 1) * CHUNK]
            xc = cb_ref[...] + xr[k * CHUNK:(k + 1) * CHUNK] * cw_ref[CONV_W - 1:CONV_W, :]
            for j in range(1, CONV_W):
                xc = xc + xr_s[s, SUBLANES - j:SUBLANES - j + CHUNK, :] * cw_ref[CONV_W - 1 - j:CONV_W - j, :]
            xcs.append(xc)
        xc = jnp.concatenate(xcs, axis=0)
        a, mult, ix = _lru_gates(xc, wbd_ref, ba_ref[...], bx_ref[...], lam_c)
        rows = lax.broadcasted_iota(jnp.int32, (pair, 1), 0)
        reset = jnp.logical_and(is_first, (rows % CHUNK) == 0)
        bt = jnp.where(reset, 1.0, mult) * ix
        for k in range(2):
            srow = pl.multiple_of((2 * p + k) * SCAN_PITCH, SUBLANES)
            for j in range(nlt):
                a_s[j, pl.ds(srow, CHUNK), :] = a[k * CHUNK:(k + 1) * CHUNK, j * LANES:(j + 1) * LANES]
                b_s[j, pl.ds(srow, CHUNK), :] = bt[k * CHUNK:(k + 1) * CHUNK, j * LANES:(j + 1) * LANES]
        return c

    lax.fori_loop(0, nseq // 2, branch_body, 0)

    def scan_body(t, hs):
        out = []
        for j in range(nlt):
            a_t = a_s[j, pl.ds(t, nseq, stride=SCAN_PITCH), :]
            b_t = b_s[j, pl.ds(t, nseq, stride=SCAN_PITCH), :]
            h = a_t * hs[j] + b_t
            b_s[j, pl.ds(t, nseq, stride=SCAN_PITCH), :] = h
            out.append(h)
        return tuple(out)

    h0 = tuple(hc_s[:, j * LANES:(j + 1) * LANES] for j in range(nlt))
    hs = lax.fori_loop(0, CHUNK, scan_body, h0)
    hfin = jnp.concatenate(hs, axis=-1)
    hc_s[...] = hfin
    h_ref[...] = hfin

    def out_body(p, c):
        row = pl.multiple_of(p * pair, pair)
        xnb = xn_s[pl.ds(row, pair), :]
        gb = _sigmoid(_dot(xnb, win_ref[:, 4 * d:5 * d]))
        hrows = []
        for k in range(2):
            srow = pl.multiple_of((2 * p + k) * SCAN_PITCH, SUBLANES)
            hrows.append(jnp.concatenate([b_s[j, pl.ds(srow, CHUNK), :] for j in range(nlt)], axis=-1))
        h = jnp.concatenate(hrows, axis=0)
        merged = (ya_s[pl.ds(row, pair), :].astype(F32) + gb * h).astype(BF16)
        o = _dot(merged, wout_ref[...])
        y_ref[2 * p] = x_ref[2 * p] + o[:CHUNK]
        y_ref[2 * p + 1] = x_ref[2 * p + 1] + o[CHUNK:]
        return c

    lax.fori_loop(0, nseq // 2, out_body, 0)

    @pl.when(i == last)
    def _():
        conv_ref[...] = xr_s[:, SUBLANES + CHUNK - (CONV_W - 1):SUBLANES + CHUNK, :]

    @pl.when(i != last)
    def _():
        xr_s[:, 0:SUBLANES, :] = xr_s[:, CHUNK:CHUNK + SUBLANES, :]


def _const_spec(shape):
    nd = len(shape)
    return pl.BlockSpec(shape, lambda i, _n=nd: (0,) * _n, pipeline_mode=pl.Buffered(1))


def _mixer_prompt(x, wts):
    nseq, seq, d = x.shape
    assert seq % CHUNK == 0 and nseq == SUBLANES and d % 256 == 0
    n_chunks = seq // CHUNK
    nlt = d // LANES
    weights = (wts['norm1'], wts['w_in'], wts['ln_g'], wts['ln_b'], wts['ws'], wts['bs'],
               wts['conv_w'], wts['conv_b'], wts['lam'], wts['wbd'], wts['b_a'], wts['b_x'],
               wts['w_out'])
    x_spec = pl.BlockSpec((nseq, CHUNK, d), lambda i: (0, i, 0))
    out_shape = (
        jax.ShapeDtypeStruct((nseq, seq, d), F32),
        jax.ShapeDtypeStruct((nseq, d), F32),
        jax.ShapeDtypeStruct((nseq, CONV_W - 1, d), F32),
        jax.ShapeDtypeStruct((nseq, CHUNK, d), F32),
    )
    out_specs = (
        x_spec,
        _const_spec((nseq, d)),
        _const_spec((nseq, CONV_W - 1, d)),
        _const_spec((nseq, CHUNK, d)),
    )
    scratch = [
        pltpu.VMEM((nseq * CHUNK, d), BF16),
        pltpu.VMEM((nseq, CHUNK + SUBLANES, d), F32),
        pltpu.VMEM((nlt, nseq * SCAN_PITCH, LANES), F32),
        pltpu.VMEM((nlt, nseq * SCAN_PITCH, LANES), F32),
        pltpu.VMEM((nseq * CHUNK, d), BF16),
        pltpu.VMEM((nseq, d), F32),
    ]
    return pl.pallas_call(
        _mixer_prompt_kernel,
        grid=(n_chunks,),
        in_specs=[x_spec] + [_const_spec(w.shape) for w in weights],
        out_specs=out_specs,
        out_shape=out_shape,
        scratch_shapes=scratch,
        compiler_params=pltpu.CompilerParams(
            dimension_semantics=("arbitrary",), vmem_limit_bytes=VMEM_LIMIT),
        name="mixer_prompt",
    )(x, *weights)


def _mixer_sample_kernel(x_ref, h0_ref, cst_ref, norm_ref, win_ref, lng_ref, lnb_ref, wsc_ref,
                         bsc_ref, cw_ref, cb_ref, lam_ref, wbd_ref, ba_ref, bx_ref, wout_ref,
                         y_ref, h_ref, conv_ref, v_ref):
    nt, ns, d = x_ref.shape
    lam_c = -LRU_C * _softplus(-lam_ref[...])
    g1 = norm_ref[...]
    h = h0_ref[...]
    xp = [cst_ref[k] for k in range(CONV_W - 1)]
    for t in range(nt):
        x = x_ref[t]
        xnb = _rms_norm(x, g1).astype(BF16)
        v = _layer_norm(_gelu_tanh(_dot(xnb, win_ref[:, d:2 * d])), lng_ref[...], lnb_ref[...])
        v_ref[t] = v
        z = bsc_ref[t:t + 1, :]
        for s in range(t + 1):
            z = z + wsc_ref[t, s:s + 1, :] * v_ref[s]
        ya = _gelu_tanh(_dot(xnb, win_ref[:, 0:d])) * z
        ga = _sigmoid(_dot(xnb, win_ref[:, 3 * d:4 * d]))
        xr = _dot(xnb, win_ref[:, 2 * d:3 * d])
        xp.append(xr)
        xc = cb_ref[...]
        for k in range(CONV_W):
            xc = xc + xp[t + k] * cw_ref[k:k + 1, :]
        a, mult, ix = _lru_gates(xc, wbd_ref, ba_ref[...], bx_ref[...], lam_c)
        h = a * h + mult * ix
        gb = _sigmoid(_dot(xnb, win_ref[:, 4 * d:5 * d]))
        merged = (ga * ya + gb * h).astype(BF16)
        y_ref[t] = x + _dot(merged, wout_ref[...])
    h_ref[...] = h
    for k in range(CONV_W - 1):
        conv_ref[k] = xp[nt + k]


def _mixer_sample(x, h0, conv_state, wts, wsc, bsc):
    nt, ns, d = x.shape
    args = (x, h0, conv_state, wts['norm1'], wts['w_in'], wts['ln_g'], wts['ln_b'], wsc, bsc,
            wts['conv_w'], wts['conv_b'], wts['lam'], wts['wbd'], wts['b_a'], wts['b_x'],
            wts['w_out'])
    out_shape = (
        jax.ShapeDtypeStruct((nt, ns, d), F32),
        jax.ShapeDtypeStruct((ns, d), F32),
        jax.ShapeDtypeStruct((CONV_W - 1, ns, d), F32),
        jax.ShapeDtypeStruct((nt, ns, d), F32),
    )
    return pl.pallas_call(
        _mixer_sample_kernel,
        out_shape=out_shape,
        compiler_params=pltpu.CompilerParams(vmem_limit_bytes=VMEM_LIMIT),
        name="mixer_sample",
    )(*args)


def _route(logits):
    lane = lax.broadcasted_iota(jnp.int32, logits.shape, 1).astype(F32)
    neg = jnp.float32(-jnp.inf)
    big = jnp.float32(ROUTE_LANES)

    def argmax_first(vals):
        m = jnp.max(vals, axis=-1, keepdims=True)
        idx = jnp.min(jnp.where(vals == m, lane, big), axis=-1, keepdims=True)
        return m, idx

    gl = jnp.where(lane < N_GROUPS, logits, neg)
    gm, gi = argmax_first(gl)
    pg_top = 1.0 / jnp.sum(jnp.exp(gl - gm), axis=-1, keepdims=True)
    lo = N_GROUPS + gi * E_PER_GROUP
    el = jnp.where(jnp.logical_and(lane >= lo, lane < lo + E_PER_GROUP), logits, neg)
    m1, i1 = argmax_first(el)
    m2, i2 = argmax_first(jnp.where(lane == i1, neg, el))
    e2 = jnp.exp(m2 - m1)
    w1 = pg_top / (1.0 + e2)
    w2 = pg_top * e2 / (1.0 + e2)
    return i1, i2, w1, w2


def _lane_cumsum_exclusive(x):
    lane = lax.broadcasted_iota(jnp.int32, x.shape, 1)
    acc = x
    shift = 1
    while shift < LANES:
        acc = acc + jnp.where(lane >= shift, pltpu.roll(acc, shift, axis=1), 0.0)
        shift *= 2
    return acc - x


def _to_slab(ref, val):
    t, d = val.shape
    nlt = d // LANES
    for j in range(nlt):
        ref[pl.ds(j, t, stride=nlt), :] = val[:, j * LANES:(j + 1) * LANES]


def _from_slab(ref, t, nlt):
    return jnp.concatenate([ref[pl.ds(j, t, stride=nlt), :] for j in range(nlt)], axis=-1)


def _router_kernel(y_ref, norm_ref, wr_ref, br_ref, tri_ref, xn_ref, sel_ref, cnt_ref):
    xn = _rms_norm(y_ref[...], norm_ref[...])
    _to_slab(xn_ref, xn)
    logits = _dot(xn.astype(BF16), wr_ref[...]) + br_ref[...]
    i1, i2, w1, w2 = _route(logits)
    lane = lax.broadcasted_iota(jnp.int32, logits.shape, 1).astype(F32)
    hot1 = lane == i1
    hot2 = lane == i2
    onehot = jnp.where(jnp.logical_or(hot1, hot2), 1.0, 0.0)
    before = _dot(tri_ref[...], onehot.astype(BF16))
    cnt = jnp.sum(onehot, axis=0, keepdims=True)
    cnt8 = jnp.broadcast_to(cnt, (SUBLANES, ROUTE_LANES))
    seg_lo = _lane_cumsum_exclusive(cnt8)[0:1, :]
    slot = before + seg_lo
    p1 = jnp.sum(jnp.where(hot1, slot, 0.0), axis=-1, keepdims=True)
    p2 = jnp.sum(jnp.where(hot2, slot, 0.0), axis=-1, keepdims=True)
    cols = (i1 - N_GROUPS, i2 - N_GROUPS, w1, w2, p1, p2)
    sel = jnp.zeros_like(logits)
    for k, c in enumerate(cols):
        sel = jnp.where(lane == k, c, sel)
    sel_ref[...] = sel
    cnt_ref[0] = cnt8


def _router(y, wts, tile):
    n, d = y.shape
    nlt = d // LANES
    n_tiles = n // tile
    tri = jnp.tril(jnp.ones((tile, tile), BF16), -1)
    cst = lambda shape: pl.BlockSpec(shape, lambda i: (0,) * len(shape))
    return pl.pallas_call(
        _router_kernel,
        grid=(n_tiles,),
        in_specs=[pl.BlockSpec((tile, d), lambda i: (i, 0)), cst((1, d)), cst((d, ROUTE_LANES)),
                  cst((1, ROUTE_LANES)), cst((tile, tile))],
        out_specs=(pl.BlockSpec((tile * nlt, LANES), lambda i: (i, 0)),
                   pl.BlockSpec((tile, ROUTE_LANES), lambda i: (i, 0)),
                   pl.BlockSpec((1, SUBLANES, ROUTE_LANES), lambda i: (i, 0, 0))),
        out_shape=(jax.ShapeDtypeStruct((n * nlt, LANES), F32),
                   jax.ShapeDtypeStruct((n, ROUTE_LANES), F32),
                   jax.ShapeDtypeStruct((n_tiles, SUBLANES, ROUTE_LANES), F32)),
        compiler_params=pltpu.CompilerParams(
            dimension_semantics=("arbitrary",), vmem_limit_bytes=VMEM_LIMIT),
        name="moe_router",
    )(y, wts['norm2'], wts['w_route'], wts['b_route'], tri)


def _run_copy(tile_idx, e, lo_ref, cnt_ref, dst_ref, local, remote, sem, nlt, to_remote):
    k = tile_idx * N_EXPERTS + e
    n = cnt_ref[k] * nlt
    loc = local.at[pl.ds(pl.multiple_of(lo_ref[k] * nlt, nlt), n)]
    rem = remote.at[pl.ds(pl.multiple_of(dst_ref[k] * nlt, nlt), n)]
    return pltpu.make_async_copy(loc, rem, sem) if to_remote else pltpu.make_async_copy(rem, loc, sem)


def _for_each_run(tile_idx, cnt_ref, fn):
    def body(e, c):
        @pl.when(cnt_ref[tile_idx * N_EXPERTS + e] > 0)
        def _():
            fn(e)
        return c
    lax.fori_loop(0, N_EXPERTS, body, 0)


def _dispatch_kernel(lo_ref, cnt_ref, dst_ref, padst_ref, padn_ref,
                     slot_ref, x_ref, xs_ref, loc, zeros, sem, *, nlt, unroll):
    i = pl.program_id(0)
    tile = x_ref.shape[0] // nlt

    def body(tt, c):
        for u in range(unroll):
            t = tt * unroll + u
            row = x_ref[pl.ds(pl.multiple_of(t * nlt, nlt), nlt), :]
            for k in range(2):
                s = slot_ref[2 * t + k]
                loc[pl.ds(pl.multiple_of(s * nlt, nlt), nlt), :] = row
        return c

    lax.fori_loop(0, tile // unroll, body, 0)
    copy = lambda e: _run_copy(i, e, lo_ref, cnt_ref, dst_ref, loc, xs_ref, sem, nlt, True)
    _for_each_run(i, cnt_ref, lambda e: copy(e).start())
    _for_each_run(i, cnt_ref, lambda e: copy(e).wait())

    @pl.when(i == pl.num_programs(0) - 1)
    def _():
        zeros[...] = jnp.zeros_like(zeros)

        def pad_copy(e):
            n = padn_ref[e] * nlt
            dst = xs_ref.at[pl.ds(pl.multiple_of(padst_ref[e] * nlt, nlt), n)]
            return pltpu.make_async_copy(zeros.at[pl.ds(0, n)], dst, sem)

        def each_pad(fn):
            def b(e, c):
                @pl.when(padn_ref[e] > 0)
                def _():
                    fn(e)
                return c
            lax.fori_loop(0, padn_ref.shape[0], b, 0)

        each_pad(lambda e: pad_copy(e).start())
        each_pad(lambda e: pad_copy(e).wait())


def _dispatch(xn_slab, slots, tables, tile, n_rows, nlt):
    n = xn_slab.shape[0] // nlt
    n_tiles = n // tile
    lo, cnt, dst, padst, padn = tables
    return pl.pallas_call(
        functools.partial(_dispatch_kernel, nlt=nlt, unroll=8),
        grid_spec=pltpu.PrefetchScalarGridSpec(
            num_scalar_prefetch=5,
            grid=(n_tiles,),
            in_specs=[pl.BlockSpec((2 * tile,), lambda i, *_: (i,), memory_space=pltpu.SMEM),
                      pl.BlockSpec((tile * nlt, LANES), lambda i, *_: (i, 0))],
            out_specs=pl.BlockSpec(memory_space=pl.ANY),
            scratch_shapes=[pltpu.VMEM((2 * tile * nlt, LANES), F32),
                            pltpu.VMEM((MOE_BLOCK * nlt, LANES), F32),
                            pltpu.SemaphoreType.DMA]),
        out_shape=jax.ShapeDtypeStruct((n_rows * nlt, LANES), F32),
        compiler_params=pltpu.CompilerParams(
            dimension_semantics=("arbitrary",), vmem_limit_bytes=VMEM_LIMIT),
        name="moe_dispatch",
    )(lo, cnt, dst, padst, padn, slots, xn_slab)


def _experts_kernel(blk_ref, nused_ref, xs_ref, wg_ref, wu_ref, wd_ref, ys_ref, *, nlt):
    b = pl.program_id(0)
    rows = xs_ref.shape[0] // nlt

    @pl.when(b < nused_ref[0])
    def _():
        x = _from_slab(xs_ref, rows, nlt).astype(BF16)
        hg = _dot(x, wg_ref[0].astype(BF16))
        hu = _dot(x, wu_ref[0].astype(BF16))
        h = (hg * _sigmoid(hg) * hu).astype(BF16)
        _to_slab(ys_ref, _dot(h, wd_ref[0].astype(BF16)))

    @pl.when(b >= nused_ref[0])
    def _():
        ys_ref[...] = jnp.zeros_like(ys_ref)


def _experts(xs, blk_expert, n_used, wts, nlt):
    n_blocks = xs.shape[0] // (MOE_BLOCK * nlt)
    d = nlt * LANES
    f = wts['w_gate'].shape[-1]
    rows = pl.BlockSpec((MOE_BLOCK * nlt, LANES), lambda b, blk, nu: (b, 0))
    return pl.pallas_call(
        functools.partial(_experts_kernel, nlt=nlt),
        grid_spec=pltpu.PrefetchScalarGridSpec(
            num_scalar_prefetch=2,
            grid=(n_blocks,),
            in_specs=[rows,
                      pl.BlockSpec((1, d, f), lambda b, blk, nu: (blk[b], 0, 0)),
                      pl.BlockSpec((1, d, f), lambda b, blk, nu: (blk[b], 0, 0)),
                      pl.BlockSpec((1, f, d), lambda b, blk, nu: (blk[b], 0, 0))],
            out_specs=rows),
        out_shape=jax.ShapeDtypeStruct(xs.shape, F32),
        compiler_params=pltpu.CompilerParams(
            dimension_semantics=("arbitrary",), vmem_limit_bytes=VMEM_LIMIT),
        name="moe_experts",
    )(blk_expert, n_used, xs, wts['w_gate'], wts['w_up'], wts['w_down'])


def _combine_kernel(lo_ref, cnt_ref, dst_ref, slot_ref, w_ref, y_ref, nf_ref, ys_ref, o_ref,
                    loc, acc, sem, *, nlt, unroll, final_norm):
    i = pl.program_id(0)
    tile = y_ref.shape[0]
    copy = lambda e: _run_copy(i, e, lo_ref, cnt_ref, dst_ref, loc, ys_ref, sem, nlt, False)
    _for_each_run(i, cnt_ref, lambda e: copy(e).start())
    _for_each_run(i, cnt_ref, lambda e: copy(e).wait())

    def body(tt, c):
        for u in range(unroll):
            t = tt * unroll + u
            s1 = slot_ref[2 * t]
            s2 = slot_ref[2 * t + 1]
            r1 = loc[pl.ds(pl.multiple_of(s1 * nlt, nlt), nlt), :]
            r2 = loc[pl.ds(pl.multiple_of(s2 * nlt, nlt), nlt), :]
            acc[pl.ds(pl.multiple_of(t * nlt, nlt), nlt), :] = w_ref[2 * t] * r1 + w_ref[2 * t + 1] * r2
        return c

    lax.fori_loop(0, tile // unroll, body, 0)
    out = y_ref[...] + _from_slab(acc, tile, nlt)
    if final_norm:
        out = _rms_norm(out, nf_ref[...])
    o_ref[...] = out


def _combine(y, ys, slots, weights, tables, norm_f, final_norm, tile, nlt):
    n, d = y.shape
    lo, cnt, dst = tables
    smem = lambda: pl.BlockSpec((2 * tile,), lambda i, *_: (i,), memory_space=pltpu.SMEM)
    tok = pl.BlockSpec((tile, d), lambda i, *_: (i, 0))
    return pl.pallas_call(
        functools.partial(_combine_kernel, nlt=nlt, unroll=8, final_norm=final_norm),
        grid_spec=pltpu.PrefetchScalarGridSpec(
            num_scalar_prefetch=3,
            grid=(n // tile,),
            in_specs=[smem(), smem(), tok, pl.BlockSpec((1, d), lambda i, *_: (0, 0)),
                      pl.BlockSpec(memory_space=pl.ANY)],
            out_specs=tok,
            scratch_shapes=[pltpu.VMEM((2 * tile * nlt, LANES), F32),
                            pltpu.VMEM((tile * nlt, LANES), F32),
                            pltpu.SemaphoreType.DMA]),
        out_shape=jax.ShapeDtypeStruct((n, d), F32),
        compiler_params=pltpu.CompilerParams(
            dimension_semantics=("arbitrary",), vmem_limit_bytes=VMEM_LIMIT),
        name="moe_combine",
    )(lo, cnt, dst, slots, weights, y, norm_f, ys)


def _moe(y, wts, norm_f, final_norm, tile):
    n, d = y.shape
    assert n % tile == 0 and d % LANES == 0
    nlt = d // LANES
    n_tiles = n // tile
    xn_slab, sel, cnt = _router(y, wts, tile)
    cnt = cnt[:, 0, N_GROUPS:N_GROUPS + N_EXPERTS].astype(jnp.int32)
    lo = jnp.cumsum(cnt, axis=1) - cnt
    seg = jnp.sum(cnt, axis=0)
    seg_pad = (seg + MOE_BLOCK - 1) // MOE_BLOCK * MOE_BLOCK
    seg_start = jnp.cumsum(seg_pad) - seg_pad
    dst = seg_start[None, :] + jnp.cumsum(cnt, axis=0) - cnt
    n_blocks = (2 * n + N_EXPERTS * (MOE_BLOCK - 1)) // MOE_BLOCK
    blk_end = jnp.cumsum(seg_pad) // MOE_BLOCK
    blk_expert = jnp.minimum(
        jnp.searchsorted(blk_end, jnp.arange(n_blocks, dtype=jnp.int32), side='right'),
        N_EXPERTS - 1).astype(jnp.int32)
    n_used = blk_end[-1:].astype(jnp.int32)
    flat = lambda a: a.reshape(-1).astype(jnp.int32)
    slots = sel[:, 4:6].astype(jnp.int32).reshape(-1)
    weights = sel[:, 2:4].reshape(-1)
    blk_ids = jnp.arange(n_blocks, dtype=jnp.int32)
    pad_start = jnp.concatenate([seg_start + seg, blk_ids * MOE_BLOCK])
    pad_rows = jnp.concatenate([seg_pad - seg, jnp.where(blk_ids >= n_used[0], MOE_BLOCK, 0)])
    xs = _dispatch(xn_slab, slots, (flat(lo), flat(cnt), flat(dst), flat(pad_start),
                                    flat(pad_rows)), tile, n_blocks * MOE_BLOCK, nlt)
    ys = _experts(xs, blk_expert, n_used, wts, nlt)
    return _combine(y, ys, slots, weights, (flat(lo), flat(cnt), flat(dst)), norm_f, final_norm,
                    tile, nlt)


def _prep_layer(l, norm1, w_in, ln_g, ln_b, w_s, b_s, conv_w, conv_b, lru_lambda, w_rg_a, b_rg_a,
                w_rg_x, b_rg_x, w_out, norm2, w_route_group, b_route_group, w_route_expert,
                b_route_expert, w_gate, w_up, w_down, n_sample_t):
    d = w_in.shape[1]
    dh = d // H_A
    row = lambda a: a.reshape(1, -1).astype(F32)
    causal = jnp.tril(jnp.ones((CHUNK, CHUNK), dtype=bool))
    ws = jnp.where(causal, w_s[l], 0.0)
    blk = d // H_B

    def pair_bd(w):
        z = jnp.zeros((blk, blk), w.dtype)
        return jnp.stack([jnp.block([[w[2 * q], z], [z, w[2 * q + 1]]]) for q in range(H_B // 2)])

    wbd = jnp.concatenate([pair_bd(w_rg_a[l]), pair_bd(w_rg_x[l])], axis=-1).astype(BF16)
    w_route = jnp.zeros((d, ROUTE_LANES), F32)
    w_route = w_route.at[:, :N_GROUPS].set(w_route_group[l])
    w_route = w_route.at[:, N_GROUPS:N_GROUPS + N_EXPERTS].set(w_route_expert[l])
    b_route = jnp.zeros((1, ROUTE_LANES), F32)
    b_route = b_route.at[0, :N_GROUPS].set(b_route_group[l])
    b_route = b_route.at[0, N_GROUPS:N_GROUPS + N_EXPERTS].set(b_route_expert[l])
    wts = dict(
        norm1=row(norm1[l]), w_in=w_in[l].astype(BF16), ln_g=row(ln_g[l]), ln_b=row(ln_b[l]),
        ws=ws.astype(BF16), bs=jnp.repeat(b_s[l].T, dh, axis=1).astype(F32),
        conv_w=conv_w[l].astype(F32), conv_b=row(conv_b[l]), lam=row(lru_lambda[l]),
        wbd=wbd, b_a=row(b_rg_a[l]), b_x=row(b_rg_x[l]), w_out=w_out[l].astype(BF16),
        norm2=row(norm2[l]), w_route=w_route.astype(BF16), b_route=b_route,
        w_gate=w_gate[l], w_up=w_up[l], w_down=w_down[l],
    )
    nt = n_sample_t
    wsc = jnp.repeat(jnp.transpose(ws[:, :nt, :nt], (1, 2, 0)), dh, axis=2).astype(F32)
    bsc = jnp.repeat(b_s[l].T[:nt], dh, axis=1).astype(F32)
    return wts, wsc, bsc


def kernel(x_prompt, x_sample, state_lru_h, state_conv, norm1, w_in, ln_g, ln_b, w_s, b_s, conv_w, conv_b, lru_lambda, w_rg_a, b_rg_a, w_rg_x, b_rg_x, w_out, norm2, w_route_group, b_route_group, w_route_expert, b_route_expert, w_gate, w_up, w_down, norm_f):
    depth = w_in.shape[0]
    nb, seq, d = x_prompt.shape
    ns, nt, _ = x_sample.shape
    assert nt <= CHUNK
    nf = norm_f.reshape(1, d).astype(F32)
    yp = x_prompt
    ys = jnp.transpose(x_sample, (1, 0, 2))
    hp, cp, vp, hs, cs, vs = [], [], [], [], [], []
    for l in range(depth):
        wts, wsc, bsc = _prep_layer(
            l, norm1, w_in, ln_g, ln_b, w_s, b_s, conv_w, conv_b, lru_lambda, w_rg_a, b_rg_a,
            w_rg_x, b_rg_x, w_out, norm2, w_route_group, b_route_group, w_route_expert,
            b_route_expert, w_gate, w_up, w_down, nt)
        final = l == depth - 1
        yp, h, c, v = _mixer_prompt(yp, wts)
        hp.append(h)
        cp.append(c)
        vp.append(v)
        ys, h, c, v = _mixer_sample(ys, state_lru_h[l], jnp.transpose(state_conv[l], (1, 0, 2)),
                                    wts, wsc, bsc)
        hs.append(h)
        cs.append(jnp.transpose(c, (1, 0, 2)))
        vs.append(jnp.transpose(v, (1, 0, 2)))
        yp = _moe(yp.reshape(nb * seq, d), wts, nf, final, MOE_TILE).reshape(nb, seq, d)
        ys = _moe(ys.reshape(nt * ns, d), wts, nf, final, nt * ns).reshape(nt, ns, d)
    ys = jnp.transpose(ys, (1, 0, 2))
    return (yp, ys, jnp.stack(hp), jnp.stack(cp), jnp.stack(vp), jnp.stack(hs), jnp.stack(cs),
            jnp.stack(vs))
```

```python
import functools

import jax
import jax.numpy as jnp
from jax import lax
from jax.experimental import pallas as pl
from jax.experimental.pallas import tpu as pltpu

F32 = jnp.float32
BF16 = jnp.bfloat16

CHUNK = 128
H_A = 4
H_B = 8
CONV_W = 4
LRU_C = 8.0
N_GROUPS = 4
E_PER_GROUP = 8
N_EXPERTS = N_GROUPS * E_PER_GROUP
EPS = 1e-6

LANES = 128
SUBLANES = 8
SCAN_PITCH = CHUNK + SUBLANES
ROUTE_LANES = 128
MOE_TILE = 512
MOE_BLOCK = 256
VMEM_LIMIT = 60 * 1024 * 1024


def _dot(a, b):
    return jnp.dot(a, b, preferred_element_type=F32)


def _sigmoid(x):
    return 0.5 * (jnp.tanh(0.5 * x) + 1.0)


def _gelu_tanh(x):
    c = 0.7978845608028654
    return 0.5 * x * (1.0 + jnp.tanh(c * (x + 0.044715 * (x * x * x))))


def _rms_norm(x, g):
    return x * lax.rsqrt(jnp.mean(x * x, axis=-1, keepdims=True) + EPS) * g


def _layer_norm(x, g, b):
    xc = x - jnp.mean(x, axis=-1, keepdims=True)
    var = jnp.mean(xc * xc, axis=-1, keepdims=True)
    return xc * lax.rsqrt(var + EPS) * g + b


def _softplus(x):
    return jnp.maximum(x, 0.0) + jnp.log(1.0 + jnp.exp(-jnp.abs(x)))


def _lru_gates(xc, wbd_ref, b_a, b_x, lam_c):
    d = xc.shape[-1]
    xcb = xc.astype(BF16)
    r_parts, i_parts = [], []
    for q in range(d // 256):
        ri = _dot(xcb[:, q * 256:(q + 1) * 256], wbd_ref[q])
        r_parts.append(ri[:, :256])
        i_parts.append(ri[:, 256:])
    r = _sigmoid(jnp.concatenate(r_parts, axis=-1) + b_a)
    i = _sigmoid(jnp.concatenate(i_parts, axis=-1) + b_x)
    a = jnp.exp(lam_c * r)
    mult = jnp.sqrt(1.0 - a * a)
    return a, mult, i * xc


def _mixer_prompt_kernel(x_ref, norm_ref, win_ref, lng_ref, lnb_ref, ws_ref, bs_ref,
                         cw_ref, cb_ref, lam_ref, wbd_ref, ba_ref, bx_ref, wout_ref,
                         y_ref, h_ref, conv_ref, v_ref,
                         xn_s, xr_s, a_s, b_s, ya_s, hc_s):
    i = pl.program_id(0)
    last = pl.num_programs(0) - 1
    nseq, _, d = x_ref.shape
    nlt = d // LANES
    pair = 2 * CHUNK

    @pl.when(i == 0)
    def _():
        xr_s[:, 0:SUBLANES, :] = jnp.zeros((nseq, SUBLANES, d), F32)
        hc_s[...] = jnp.zeros_like(hc_s)

    g1 = norm_ref[...]

    def norm_body(s, c):
        row = pl.multiple_of(s * CHUNK, CHUNK)
        xn_s[pl.ds(row, CHUNK), :] = _rms_norm(x_ref[s], g1).astype(BF16)
        return c

    lax.fori_loop(0, nseq, norm_body, 0)

    lam_c = -LRU_C * _softplus(-lam_ref[...])
    is_first = i == 0

    def branch_body(p, c):
        row = pl.multiple_of(p * pair, pair)
        xnb = xn_s[pl.ds(row, pair), :]
        v = _layer_norm(_gelu_tanh(_dot(xnb, win_ref[:, d:2 * d])), lng_ref[...], lnb_ref[...])

        @pl.when(i == last)
        def _():
            v_ref[2 * p] = v[:CHUNK]
            v_ref[2 * p + 1] = v[CHUNK:]

        vb = v.astype(BF16)
        dh = d // H_A
        z_rows = []
        for k in range(2):
            zs = [_dot(ws_ref[h], vb[k * CHUNK:(k + 1) * CHUNK, h * dh:(h + 1) * dh])
                  for h in range(H_A)]
            z_rows.append(jnp.concatenate(zs, axis=-1) + bs_ref[...])
        z = jnp.concatenate(z_rows, axis=0)
        ya = _gelu_tanh(_dot(xnb, win_ref[:, 0:d])) * z
        ga = _sigmoid(_dot(xnb, win_ref[:, 3 * d:4 * d]))
        ya_s[pl.ds(row, pair), :] = (ga * ya).astype(BF16)
        xr = _dot(xnb, win_ref[:, 2 * d:3 * d])
        xcs = []
        for k in range(2):
            s = 2 * p + k
            xr_s[s, SUBLANES:SUBLANES + CHUNK, :] = xr[k * CHUNK:(k + 1) * CHUNK]
            xc = cb_ref[...] + xr[k * CHUNK:(k + 1) * CHUNK] * cw_ref[CONV_W - 1:CONV_W, :]
            for j in range(1, CONV_W):
                xc = xc + xr_s[s, SUBLANES - j:SUBLANES - j + CHUNK, :] * cw_ref[CONV_W - 1 - j:CONV_W - j, :]
            xcs.append(xc)
        xc = jnp.concatenate(xcs, axis=0)
        a, mult, ix = _lru_gates(xc, wbd_ref, ba_ref[...], bx_ref[...], lam_c)
        rows = lax.broadcasted_iota(jnp.int32, (pair, 1), 0)
        reset = jnp.logical_and(is_first, (rows % CHUNK) == 0)
        bt = jnp.where(reset, 1.0, mult) * ix
        for k in range(2):
            srow = pl.multiple_of((2 * p + k) * SCAN_PITCH, SUBLANES)
            for j in range(nlt):
                a_s[j, pl.ds(srow, CHUNK), :] = a[k * CHUNK:(k + 1) * CHUNK, j * LANES:(j + 1) * LANES]
                b_s[j, pl.ds(srow, CHUNK), :] = bt[k * CHUNK:(k + 1) * CHUNK, j * LANES:(j + 1) * LANES]
        return c

    lax.fori_loop(0, nseq // 2, branch_body, 0)

    def scan_body(t, hs):
        out = []
        for j in range(nlt):
            a_t = a_s[j, pl.ds(t, nseq, stride=SCAN_PITCH), :]
            b_t = b_s[j, pl.ds(t, nseq, stride=SCAN_PITCH), :]
            h = a_t * hs[j] + b_t
            b_s[j, pl.ds(t, nseq, stride=SCAN_PITCH), :] = h
            out.append(h)
        return tuple(out)

    h0 = tuple(hc_s[:, j * LANES:(j + 1) * LANES] for j in range(nlt))
    hs = lax.fori_loop(0, CHUNK, scan_body, h0)
    hfin = jnp.concatenate(hs, axis=-1)
    hc_s[...] = hfin
    h_ref[...] = hfin

    def out_body(p, c):
        row = pl.multiple_of(p * pair, pair)
        xnb = xn_s[pl.ds(row, pair), :]
        gb = _sigmoid(_dot(xnb, win_ref[:, 4 * d:5 * d]))
        hrows = []
        for k in range(2):
            srow = pl.multiple_of((2 * p + k) * SCAN_PITCH, SUBLANES)
            hrows.append(jnp.concatenate([b_s[j, pl.ds(srow, CHUNK), :] for j in range(nlt)], axis=-1))
        h = jnp.concatenate(hrows, axis=0)
        merged = (ya_s[pl.ds(row, pair), :].astype(F32) + gb * h).astype(BF16)
        o = _dot(merged, wout_ref[...])
        y_ref[2 * p] = x_ref[2 * p] + o[:CHUNK]
        y_ref[2 * p + 1] = x_ref[2 * p + 1] + o[CHUNK:]
        return c

    lax.fori_loop(0, nseq // 2, out_body, 0)

    @pl.when(i == last)
    def _():
        conv_ref[...] = xr_s[:, SUBLANES + CHUNK - (CONV_W - 1):SUBLANES + CHUNK, :]

    @pl.when(i != last)
    def _():
        xr_s[:, 0:SUBLANES, :] = xr_s[:, CHUNK:CHUNK + SUBLANES, :]


def _const_spec(shape):
    nd = len(shape)
    return pl.BlockSpec(shape, lambda i, *_, _n=nd: (0,) * _n, pipeline_mode=pl.Buffered(1))


def _layer_spec(stacked, layer):
    shape = stacked.shape[1:]
    nd = len(shape)
    return pl.BlockSpec((None,) + tuple(shape), lambda i, *_, _n=nd: (layer,) + (0,) * _n,
                        pipeline_mode=pl.Buffered(1))


MIXER_WEIGHTS = ('norm1', 'w_in', 'ln_g', 'ln_b', 'ws', 'bs', 'conv_w', 'conv_b', 'lam', 'wbd',
                 'b_a', 'b_x', 'w_out')


def _mixer_prompt(x, wts, layer):
    nseq, seq, d = x.shape
    assert seq % CHUNK == 0 and nseq == SUBLANES and d % 256 == 0
    n_chunks = seq // CHUNK
    nlt = d // LANES
    weights = tuple(wts[k] for k in MIXER_WEIGHTS)
    x_spec = pl.BlockSpec((nseq, CHUNK, d), lambda i: (0, i, 0))
    out_shape = (
        jax.ShapeDtypeStruct((nseq, seq, d), F32),
        jax.ShapeDtypeStruct((nseq, d), F32),
        jax.ShapeDtypeStruct((nseq, CONV_W - 1, d), F32),
        jax.ShapeDtypeStruct((nseq, CHUNK, d), F32),
    )
    out_specs = (
        x_spec,
        _const_spec((nseq, d)),
        _const_spec((nseq, CONV_W - 1, d)),
        _const_spec((nseq, CHUNK, d)),
    )
    scratch = [
        pltpu.VMEM((nseq * CHUNK, d), BF16),
        pltpu.VMEM((nseq, CHUNK + SUBLANES, d), F32),
        pltpu.VMEM((nlt, nseq * SCAN_PITCH, LANES), F32),
        pltpu.VMEM((nlt, nseq * SCAN_PITCH, LANES), F32),
        pltpu.VMEM((nseq * CHUNK, d), BF16),
        pltpu.VMEM((nseq, d), F32),
    ]
    return pl.pallas_call(
        _mixer_prompt_kernel,
        grid=(n_chunks,),
        in_specs=[x_spec] + [_layer_spec(w, layer) for w in weights],
        out_specs=out_specs,
        out_shape=out_shape,
        scratch_shapes=scratch,
        compiler_params=pltpu.CompilerParams(
            dimension_semantics=("arbitrary",), vmem_limit_bytes=VMEM_LIMIT),
        name="mixer_prompt",
    )(x, *weights)


def _mixer_sample_kernel(x_ref, h0_ref, cst_ref, norm_ref, win_ref, lng_ref, lnb_ref, wsc_ref,
                         bsc_ref, cw_ref, cb_ref, lam_ref, wbd_ref, ba_ref, bx_ref, wout_ref,
                         y_ref, h_ref, conv_ref, v_ref):
    nt, ns, d = x_ref.shape
    lam_c = -LRU_C * _softplus(-lam_ref[...])
    g1 = norm_ref[...]
    h = h0_ref[...]
    xp = [cst_ref[k] for k in range(CONV_W - 1)]
    for t in range(nt):
        x = x_ref[t]
        xnb = _rms_norm(x, g1).astype(BF16)
        v = _layer_norm(_gelu_tanh(_dot(xnb, win_ref[:, d:2 * d])), lng_ref[...], lnb_ref[...])
        v_ref[t] = v
        z = bsc_ref[t:t + 1, :]
        for s in range(t + 1):
            z = z + wsc_ref[t, s:s + 1, :] * v_ref[s]
        ya = _gelu_tanh(_dot(xnb, win_ref[:, 0:d])) * z
        ga = _sigmoid(_dot(xnb, win_ref[:, 3 * d:4 * d]))
        xr = _dot(xnb, win_ref[:, 2 * d:3 * d])
        xp.append(xr)
        xc = cb_ref[...]
        for k in range(CONV_W):
            xc = xc + xp[t + k] * cw_ref[k:k + 1, :]
        a, mult, ix = _lru_gates(xc, wbd_ref, ba_ref[...], bx_ref[...], lam_c)
        h = a * h + mult * ix
        gb = _sigmoid(_dot(xnb, win_ref[:, 4 * d:5 * d]))
        merged = (ga * ya + gb * h).astype(BF16)
        y_ref[t] = x + _dot(merged, wout_ref[...])
    h_ref[...] = h
    for k in range(CONV_W - 1):
        conv_ref[k] = xp[nt + k]


def _mixer_sample(x, h0, conv_state, wts, layer):
    nt, ns, d = x.shape
    names = MIXER_WEIGHTS[:4] + ('wsc', 'bsc') + MIXER_WEIGHTS[6:]
    stacked = (h0, conv_state) + tuple(wts[k] for k in names)
    out_shape = (
        jax.ShapeDtypeStruct((nt, ns, d), F32),
        jax.ShapeDtypeStruct((ns, d), F32),
        jax.ShapeDtypeStruct((CONV_W - 1, ns, d), F32),
        jax.ShapeDtypeStruct((nt, ns, d), F32),
    )
    return pl.pallas_call(
        _mixer_sample_kernel,
        grid=(1,),
        in_specs=[_const_spec(x.shape)] + [_layer_spec(w, layer) for w in stacked],
        out_specs=tuple(_const_spec(s.shape) for s in out_shape),
        out_shape=out_shape,
        compiler_params=pltpu.CompilerParams(
            dimension_semantics=("arbitrary",), vmem_limit_bytes=VMEM_LIMIT),
        name="mixer_sample",
    )(x, *stacked)


def _route(logits):
    lane = lax.broadcasted_iota(jnp.int32, logits.shape, 1).astype(F32)
    neg = jnp.float32(-jnp.inf)
    big = jnp.float32(ROUTE_LANES)

    def argmax_first(vals):
        m = jnp.max(vals, axis=-1, keepdims=True)
        idx = jnp.min(jnp.where(vals == m, lane, big), axis=-1, keepdims=True)
        return m, idx

    gl = jnp.where(lane < N_GROUPS, logits, neg)
    gm, gi = argmax_first(gl)
    pg_top = 1.0 / jnp.sum(jnp.exp(gl - gm), axis=-1, keepdims=True)
    lo = N_GROUPS + gi * E_PER_GROUP
    el = jnp.where(jnp.logical_and(lane >= lo, lane < lo + E_PER_GROUP), logits, neg)
    m1, i1 = argmax_first(el)
    m2, i2 = argmax_first(jnp.where(lane == i1, neg, el))
    e2 = jnp.exp(m2 - m1)
    w1 = pg_top / (1.0 + e2)
    w2 = pg_top * e2 / (1.0 + e2)
    return i1, i2, w1, w2


def _lane_cumsum_exclusive(x):
    lane = lax.broadcasted_iota(jnp.int32, x.shape, 1)
    acc = x
    shift = 1
    while shift < LANES:
        acc = acc + jnp.where(lane >= shift, pltpu.roll(acc, shift, axis=1), 0.0)
        shift *= 2
    return acc - x


def _to_slab(ref, val):
    t, d = val.shape
    nlt = d // LANES
    for j in range(nlt):
        ref[pl.ds(j, t, stride=nlt), :] = val[:, j * LANES:(j + 1) * LANES]


def _from_slab(ref, t, nlt):
    return jnp.concatenate([ref[pl.ds(j, t, stride=nlt), :] for j in range(nlt)], axis=-1)


def _router_kernel(yp_ref, ys_ref, norm_ref, wr_ref, br_ref, tri_ref, xn_ref, mf_ref, mi_ref,
                   cnt_ref, *, n_prompt_tiles):
    y = jnp.where(pl.program_id(0) < n_prompt_tiles, yp_ref[...], ys_ref[...])
    xn = _rms_norm(y, norm_ref[...])
    _to_slab(xn_ref, xn)
    logits = _dot(xn.astype(BF16), wr_ref[...]) + br_ref[...]
    i1, i2, w1, w2 = _route(logits)
    lane = lax.broadcasted_iota(jnp.int32, logits.shape, 1).astype(F32)
    hot1 = lane == i1
    hot2 = lane == i2
    onehot = jnp.where(jnp.logical_or(hot1, hot2), 1.0, 0.0)
    before = _dot(tri_ref[...], onehot.astype(BF16))
    cnt = jnp.sum(onehot, axis=0, keepdims=True)
    cnt8 = jnp.broadcast_to(cnt, (SUBLANES, ROUTE_LANES))
    seg_lo = _lane_cumsum_exclusive(cnt8)[0:1, :]
    slot = before + seg_lo
    p1 = jnp.sum(jnp.where(hot1, slot, 0.0), axis=-1, keepdims=True)
    p2 = jnp.sum(jnp.where(hot2, slot, 0.0), axis=-1, keepdims=True)
    cols = (i1 - N_GROUPS, i2 - N_GROUPS, w1, w2, p1, p2)
    sel = jnp.zeros_like(logits)
    for k, c in enumerate(cols):
        sel = jnp.where(lane == k, c, sel)
    meta = sel.T[:SUBLANES]
    mf_ref[0] = meta
    mi_ref[0] = meta.astype(jnp.int32)
    cnt_ref[0] = cnt8


def _token_specs(tile, d, n_prompt_tiles):
    last = n_prompt_tiles - 1
    return (pl.BlockSpec((tile, d), lambda i, *_: (jnp.minimum(i, last), 0)),
            pl.BlockSpec((tile, d), lambda i, *_: (jnp.maximum(i - n_prompt_tiles, 0), 0)))


def _router(yp, ys, wts, layer, tile):
    d = yp.shape[1]
    nlt = d // LANES
    n_prompt_tiles = yp.shape[0] // tile
    n_tiles = n_prompt_tiles + ys.shape[0] // tile
    tri = jnp.tril(jnp.ones((tile, tile), BF16), -1)
    meta_spec = pl.BlockSpec((1, SUBLANES, tile), lambda i: (i, 0, 0))
    return pl.pallas_call(
        functools.partial(_router_kernel, n_prompt_tiles=n_prompt_tiles),
        grid=(n_tiles,),
        in_specs=list(_token_specs(tile, d, n_prompt_tiles)) + [
            _layer_spec(wts['norm2'], layer), _layer_spec(wts['w_route'], layer),
            _layer_spec(wts['b_route'], layer), _const_spec((tile, tile))],
        out_specs=(pl.BlockSpec((tile * nlt, LANES), lambda i: (i, 0)), meta_spec, meta_spec,
                   pl.BlockSpec((1, SUBLANES, ROUTE_LANES), lambda i: (i, 0, 0))),
        out_shape=(jax.ShapeDtypeStruct((n_tiles * tile * nlt, LANES), F32),
                   jax.ShapeDtypeStruct((n_tiles, SUBLANES, tile), F32),
                   jax.ShapeDtypeStruct((n_tiles, SUBLANES, tile), jnp.int32),
                   jax.ShapeDtypeStruct((n_tiles, SUBLANES, ROUTE_LANES), F32)),
        compiler_params=pltpu.CompilerParams(
            dimension_semantics=("arbitrary",), vmem_limit_bytes=VMEM_LIMIT),
        name="moe_router",
    )(yp, ys, wts['norm2'], wts['w_route'], wts['b_route'], tri)


def _run_copy(tile_idx, e, lo_ref, cnt_ref, dst_ref, local, remote, sem, nlt, to_remote):
    k = tile_idx * N_EXPERTS + e
    n = cnt_ref[k] * nlt
    loc = local.at[pl.ds(pl.multiple_of(lo_ref[k] * nlt, nlt), n)]
    rem = remote.at[pl.ds(pl.multiple_of(dst_ref[k] * nlt, nlt), n)]
    return pltpu.make_async_copy(loc, rem, sem) if to_remote else pltpu.make_async_copy(rem, loc, sem)


def _start_runs(tile_idx, lo_ref, cnt_ref, dst_ref, local, remote, sem, nlt, to_remote):
    def body(e, c):
        @pl.when(cnt_ref[tile_idx * N_EXPERTS + e] > 0)
        def _():
            _run_copy(tile_idx, e, lo_ref, cnt_ref, dst_ref, local, remote, sem, nlt,
                      to_remote).start()
        return c
    lax.fori_loop(0, N_EXPERTS, body, 0)


def _wait_runs(local, sem):
    pltpu.make_async_copy(local, local, sem).wait()


def _dispatch_kernel(lo_ref, cnt_ref, dst_ref, padst_ref, padn_ref,
                     s1_ref, s2_ref, x_ref, xs_ref, loc, zeros, sem, *, nlt, unroll):
    i = pl.program_id(0)
    last = pl.num_programs(0) - 1
    tile = x_ref.shape[0] // nlt
    slot = i % 2
    buf = loc.at[slot]

    @pl.when(i >= 2)
    def _():
        _wait_runs(buf, sem.at[slot])

    def body(tt, c):
        for u in range(unroll):
            t = tt * unroll + u
            row = x_ref[pl.ds(pl.multiple_of(t * nlt, nlt), nlt), :]
            buf[pl.ds(pl.multiple_of(s1_ref[t] * nlt, nlt), nlt), :] = row
            buf[pl.ds(pl.multiple_of(s2_ref[t] * nlt, nlt), nlt), :] = row
        return c

    lax.fori_loop(0, tile // unroll, body, 0)
    _start_runs(i, lo_ref, cnt_ref, dst_ref, buf, xs_ref, sem.at[slot], nlt, True)

    @pl.when(i == last)
    def _():
        zeros[...] = jnp.zeros_like(zeros)

        def pad_copy(e):
            n = padn_ref[e] * nlt
            dst = xs_ref.at[pl.ds(pl.multiple_of(padst_ref[e] * nlt, nlt), n)]
            return pltpu.make_async_copy(zeros.at[pl.ds(0, n)], dst, sem.at[2])

        def each_pad(fn):
            def b(e, c):
                @pl.when(padn_ref[e] > 0)
                def _():
                    fn(e)
                return c
            lax.fori_loop(0, padn_ref.shape[0], b, 0)

        each_pad(lambda e: pad_copy(e).start())
        _wait_runs(buf, sem.at[slot])

        @pl.when(i >= 1)
        def _():
            _wait_runs(loc.at[1 - slot], sem.at[1 - slot])

        each_pad(lambda e: pad_copy(e).wait())


def _smem_tile(tile):
    return pl.BlockSpec((tile,), lambda i, *_: (i,), memory_space=pltpu.SMEM)


def _dispatch(xn_slab, s1, s2, tables, tile, n_rows, nlt):
    n_tiles = xn_slab.shape[0] // (tile * nlt)
    return pl.pallas_call(
        functools.partial(_dispatch_kernel, nlt=nlt, unroll=8),
        grid_spec=pltpu.PrefetchScalarGridSpec(
            num_scalar_prefetch=5,
            grid=(n_tiles,),
            in_specs=[_smem_tile(tile), _smem_tile(tile),
                      pl.BlockSpec((tile * nlt, LANES), lambda i, *_: (i, 0))],
            out_specs=pl.BlockSpec(memory_space=pl.ANY),
            scratch_shapes=[pltpu.VMEM((2, 2 * tile * nlt, LANES), F32),
                            pltpu.VMEM((MOE_BLOCK * nlt, LANES), F32),
                            pltpu.SemaphoreType.DMA((3,))]),
        out_shape=jax.ShapeDtypeStruct((n_rows * nlt, LANES), F32),
        compiler_params=pltpu.CompilerParams(
            dimension_semantics=("arbitrary",), vmem_limit_bytes=VMEM_LIMIT),
        name="moe_dispatch",
    )(*tables, s1, s2, xn_slab)


def _experts_kernel(blk_ref, nused_ref, xs_ref, wg_ref, wu_ref, wd_ref, ys_ref,
                    wg_s, wu_s, wd_s, *, nlt):
    b = pl.program_id(0)
    rows = xs_ref.shape[0] // nlt

    @pl.when(jnp.logical_or(b == 0, blk_ref[b] != blk_ref[jnp.maximum(b - 1, 0)]))
    def _():
        wg_s[...] = wg_ref[...].astype(BF16)
        wu_s[...] = wu_ref[...].astype(BF16)
        wd_s[...] = wd_ref[...].astype(BF16)

    @pl.when(b < nused_ref[0])
    def _():
        x = _from_slab(xs_ref, rows, nlt).astype(BF16)
        hg = _dot(x, wg_s[...])
        hu = _dot(x, wu_s[...])
        h = (hg * _sigmoid(hg) * hu).astype(BF16)
        _to_slab(ys_ref, _dot(h, wd_s[...]))

    @pl.when(b >= nused_ref[0])
    def _():
        ys_ref[...] = jnp.zeros_like(ys_ref)


def _experts(xs, blk_expert, n_used, wts, layer, nlt):
    n_blocks = xs.shape[0] // (MOE_BLOCK * nlt)
    d = nlt * LANES
    f = wts['w_gate'].shape[-1]
    rows = pl.BlockSpec((MOE_BLOCK * nlt, LANES), lambda b, blk, nu: (b, 0))
    w_spec = lambda r, c: pl.BlockSpec((None, None, r, c), lambda b, blk, nu: (layer, blk[b], 0, 0))
    return pl.pallas_call(
        functools.partial(_experts_kernel, nlt=nlt),
        grid_spec=pltpu.PrefetchScalarGridSpec(
            num_scalar_prefetch=2,
            grid=(n_blocks,),
            in_specs=[rows, w_spec(d, f), w_spec(d, f), w_spec(f, d)],
            out_specs=rows,
            scratch_shapes=[pltpu.VMEM((d, f), BF16), pltpu.VMEM((d, f), BF16),
                            pltpu.VMEM((f, d), BF16)]),
        out_shape=jax.ShapeDtypeStruct(xs.shape, F32),
        compiler_params=pltpu.CompilerParams(
            dimension_semantics=("arbitrary",), vmem_limit_bytes=VMEM_LIMIT),
        name="moe_experts",
    )(blk_expert, n_used, xs, wts['w_gate'], wts['w_up'], wts['w_down'])


def _combine_kernel(lo_ref, cnt_ref, dst_ref, s1_ref, s2_ref, w1_ref, w2_ref, yp_ref, ys_ref,
                    nf_ref, rows_ref, op_ref, os_ref, loc, acc, sem,
                    *, nlt, unroll, final_norm, n_prompt_tiles):
    i = pl.program_id(0)
    tile = yp_ref.shape[0]
    slot = i % 2
    buf = loc.at[slot]
    fetch = lambda t, s: _start_runs(t, lo_ref, cnt_ref, dst_ref, loc.at[s], rows_ref, sem.at[s],
                                     nlt, False)

    @pl.when(i == 0)
    def _():
        fetch(0, 0)

    @pl.when(i + 1 < pl.num_programs(0))
    def _():
        fetch(i + 1, 1 - slot)

    _wait_runs(buf, sem.at[slot])

    def body(tt, c):
        for u in range(unroll):
            t = tt * unroll + u
            r1 = buf[pl.ds(pl.multiple_of(s1_ref[t] * nlt, nlt), nlt), :]
            r2 = buf[pl.ds(pl.multiple_of(s2_ref[t] * nlt, nlt), nlt), :]
            acc[pl.ds(pl.multiple_of(t * nlt, nlt), nlt), :] = w1_ref[t] * r1 + w2_ref[t] * r2
        return c

    lax.fori_loop(0, tile // unroll, body, 0)
    is_prompt = i < n_prompt_tiles
    out = jnp.where(is_prompt, yp_ref[...], ys_ref[...]) + _from_slab(acc, tile, nlt)
    if final_norm:
        out = _rms_norm(out, nf_ref[...])

    @pl.when(is_prompt)
    def _():
        op_ref[...] = out

    @pl.when(jnp.logical_not(is_prompt))
    def _():
        os_ref[...] = out


def _combine(yp, ys, rows, meta, tables, norm_f, final_norm, tile, nlt):
    d = yp.shape[1]
    n_prompt_tiles = yp.shape[0] // tile
    n_tiles = n_prompt_tiles + ys.shape[0] // tile
    tok_p, tok_s = _token_specs(tile, d, n_prompt_tiles)
    return pl.pallas_call(
        functools.partial(_combine_kernel, nlt=nlt, unroll=8, final_norm=final_norm,
                          n_prompt_tiles=n_prompt_tiles),
        grid_spec=pltpu.PrefetchScalarGridSpec(
            num_scalar_prefetch=3,
            grid=(n_tiles,),
            in_specs=[_smem_tile(tile)] * 4 + [tok_p, tok_s, _const_spec((1, d)),
                                               pl.BlockSpec(memory_space=pl.ANY)],
            out_specs=(tok_p, tok_s),
            scratch_shapes=[pltpu.VMEM((2, 2 * tile * nlt, LANES), F32),
                            pltpu.VMEM((tile * nlt, LANES), F32),
                            pltpu.SemaphoreType.DMA((2,))]),
        out_shape=(jax.ShapeDtypeStruct(yp.shape, F32), jax.ShapeDtypeStruct(ys.shape, F32)),
        compiler_params=pltpu.CompilerParams(
            dimension_semantics=("arbitrary",), vmem_limit_bytes=VMEM_LIMIT),
        name="moe_combine",
    )(*tables, *meta, yp, ys, norm_f, rows)


def _moe(yp, ys, wts, layer, norm_f, final_norm, tile):
    d = yp.shape[1]
    n = yp.shape[0] + ys.shape[0]
    assert yp.shape[0] % tile == 0 and ys.shape[0] % tile == 0 and d % LANES == 0
    nlt = d // LANES
    xn_slab, meta_f, meta_i, cnt = _router(yp, ys, wts, layer, tile)
    cnt = cnt[:, 0, N_GROUPS:N_GROUPS + N_EXPERTS].astype(jnp.int32)
    lo = jnp.cumsum(cnt, axis=1) - cnt
    seg = jnp.sum(cnt, axis=0)
    seg_pad = (seg + MOE_BLOCK - 1) // MOE_BLOCK * MOE_BLOCK
    seg_end = jnp.cumsum(seg_pad)
    seg_start = seg_end - seg_pad
    dst = seg_start[None, :] + jnp.cumsum(cnt, axis=0) - cnt
    n_blocks = (2 * n + N_EXPERTS * (MOE_BLOCK - 1)) // MOE_BLOCK
    blk_ids = jnp.arange(n_blocks, dtype=jnp.int32)
    blk_expert = jnp.minimum(
        jnp.sum((seg_end[None, :] <= blk_ids[:, None] * MOE_BLOCK).astype(jnp.int32), axis=1),
        N_EXPERTS - 1)
    n_used = seg_end[-1:] // MOE_BLOCK
    pad_start = jnp.concatenate([seg_start + seg, blk_ids * MOE_BLOCK])
    pad_rows = jnp.concatenate([seg_pad - seg, jnp.where(blk_ids >= n_used[0], MOE_BLOCK, 0)])
    flat = lambda a: a.reshape(-1)
    runs = (flat(lo), flat(cnt), flat(dst))
    s1, s2 = flat(meta_i[:, 4]), flat(meta_i[:, 5])
    w1, w2 = flat(meta_f[:, 2]), flat(meta_f[:, 3])
    xs = _dispatch(xn_slab, s1, s2, runs + (pad_start, pad_rows), tile, n_blocks * MOE_BLOCK, nlt)
    rows = _experts(xs, blk_expert, n_used, wts, layer, nlt)
    return _combine(yp, ys, rows, (s1, s2, w1, w2), runs, norm_f, final_norm, tile, nlt)


def _prep_weights(norm1, w_in, ln_g, ln_b, w_s, b_s, conv_w, conv_b, lru_lambda, w_rg_a, b_rg_a,
                  w_rg_x, b_rg_x, w_out, norm2, w_route_group, b_route_group, w_route_expert,
                  b_route_expert, w_gate, w_up, w_down, n_sample_t):
    depth, d = w_in.shape[0], w_in.shape[1]
    dh = d // H_A
    row = lambda a: a.reshape(depth, 1, -1).astype(F32)
    causal = jnp.tril(jnp.ones((CHUNK, CHUNK), dtype=bool))
    ws = jnp.where(causal, w_s, 0.0)
    bs = jnp.repeat(jnp.swapaxes(b_s, 1, 2), dh, axis=2)

    def pair_bd(w):
        w = w.reshape(depth, H_B // 2, 2, w.shape[-2], w.shape[-1])
        z = jnp.zeros_like(w[:, :, 0])
        return jnp.concatenate([jnp.concatenate([w[:, :, 0], z], axis=-1),
                                jnp.concatenate([z, w[:, :, 1]], axis=-1)], axis=-2)

    pad = ROUTE_LANES - N_GROUPS - N_EXPERTS
    w_route = jnp.concatenate(
        [w_route_group, w_route_expert, jnp.zeros((depth, d, pad), F32)], axis=-1)
    b_route = jnp.concatenate(
        [b_route_group, b_route_expert, jnp.zeros((depth, pad), F32)], axis=-1)
    nt = n_sample_t
    return dict(
        norm1=row(norm1), w_in=w_in.astype(BF16), ln_g=row(ln_g), ln_b=row(ln_b),
        ws=ws.astype(BF16), bs=bs.astype(F32), conv_w=conv_w.astype(F32), conv_b=row(conv_b),
        lam=row(lru_lambda),
        wbd=jnp.concatenate([pair_bd(w_rg_a), pair_bd(w_rg_x)], axis=-1).astype(BF16),
        b_a=row(b_rg_a), b_x=row(b_rg_x), w_out=w_out.astype(BF16),
        norm2=row(norm2), w_route=w_route.astype(BF16), b_route=row(b_route),
        w_gate=w_gate, w_up=w_up, w_down=w_down,
        wsc=jnp.repeat(jnp.transpose(ws[:, :, :nt, :nt], (0, 2, 3, 1)), dh, axis=3).astype(F32),
        bsc=bs[:, :nt].astype(F32),
    )


def kernel(x_prompt, x_sample, state_lru_h, state_conv, norm1, w_in, ln_g, ln_b, w_s, b_s, conv_w, conv_b, lru_lambda, w_rg_a, b_rg_a, w_rg_x, b_rg_x, w_out, norm2, w_route_group, b_route_group, w_route_expert, b_route_expert, w_gate, w_up, w_down, norm_f):
    depth = w_in.shape[0]
    nb, seq, d = x_prompt.shape
    ns, nt, _ = x_sample.shape
    assert nt <= CHUNK
    nf = norm_f.reshape(1, d).astype(F32)
    wts = _prep_weights(norm1, w_in, ln_g, ln_b, w_s, b_s, conv_w, conv_b, lru_lambda, w_rg_a,
                        b_rg_a, w_rg_x, b_rg_x, w_out, norm2, w_route_group, b_route_group,
                        w_route_expert, b_route_expert, w_gate, w_up, w_down, nt)
    conv_state = jnp.transpose(state_conv, (0, 2, 1, 3))
    tile = min(MOE_TILE, nt * ns)
    yp = x_prompt
    ys = jnp.transpose(x_sample, (1, 0, 2))
    hp, cp, vp, hs, cs, vs = [], [], [], [], [], []
    for l in range(depth):
        yp, h, c, v = _mixer_prompt(yp, wts, l)
        hp.append(h)
        cp.append(c)
        vp.append(v)
        ys, h, c, v = _mixer_sample(ys, state_lru_h, conv_state, wts, l)
        hs.append(h)
        cs.append(c)
        vs.append(v)
        yp, ys = _moe(yp.reshape(nb * seq, d), ys.reshape(nt * ns, d), wts, l, nf,
                      l == depth - 1, tile)
        yp = yp.reshape(nb, seq, d)
        ys = ys.reshape(nt, ns, d)
    to_seq_major = lambda a: jnp.transpose(jnp.stack(a), (0, 2, 1, 3))
    return (yp, jnp.transpose(ys, (1, 0, 2)), jnp.stack(hp), jnp.stack(cp), jnp.stack(vp),
            jnp.stack(hs), to_seq_major(cs), to_seq_major(vs))
```

```python
import functools

import jax
import jax.numpy as jnp
from jax import lax
from jax.experimental import pallas as pl
from jax.experimental.pallas import tpu as pltpu

F32 = jnp.float32
BF16 = jnp.bfloat16

CHUNK = 128
H_A = 4
H_B = 8
CONV_W = 4
LRU_C = 8.0
N_GROUPS = 4
E_PER_GROUP = 8
N_EXPERTS = N_GROUPS * E_PER_GROUP
EPS = 1e-6

LANES = 128
SUBLANES = 8
SCAN_PITCH = CHUNK + SUBLANES
ROUTE_LANES = 128
MOE_TILE = 512
MOE_BLOCK = 256
VMEM_LIMIT = 60 * 1024 * 1024


def _dot(a, b):
    return jnp.dot(a, b, preferred_element_type=F32)


def _sigmoid(x):
    return 0.5 * (jnp.tanh(0.5 * x) + 1.0)


def _gelu2(x):
    c = 0.7978845608028654
    return x * (1.0 + jnp.tanh(x * (c + (c * 0.044715) * (x * x))))


def _rms_norm(x, g):
    return x * lax.rsqrt(jnp.mean(x * x, axis=-1, keepdims=True) + EPS) * g


def _layer_norm(x, g, b, eps):
    xc = x - jnp.mean(x, axis=-1, keepdims=True)
    var = jnp.mean(xc * xc, axis=-1, keepdims=True)
    return xc * lax.rsqrt(var + eps) * g + b


def _softplus(x):
    return jnp.maximum(x, 0.0) + jnp.log(1.0 + jnp.exp(-jnp.abs(x)))


def _lru_gates(xc, wbd_ref, b_a, b_x, lam_half):
    d = xc.shape[-1]
    xcb = xc.astype(BF16)
    r_parts, i_parts = [], []
    for q in range(d // 256):
        ri = _dot(xcb[:, q * 256:(q + 1) * 256], wbd_ref[q])
        r_parts.append(ri[:, :256])
        i_parts.append(ri[:, 256:])
    r2 = 1.0 + jnp.tanh(jnp.concatenate(r_parts, axis=-1) + b_a)
    i2 = 1.0 + jnp.tanh(jnp.concatenate(i_parts, axis=-1) + b_x)
    a = jnp.exp(lam_half * r2)
    mult_half = jnp.sqrt(0.25 - 0.25 * (a * a))
    return a, mult_half, i2 * xc


def _shift_rows(x, hist, j):
    sh = pltpu.roll(x, j, axis=0)
    top = jnp.where(lax.broadcasted_iota(jnp.int32, (SUBLANES, 1), 0) < j,
                    pltpu.roll(hist, j, axis=0), sh[:SUBLANES])
    return jnp.concatenate([top, sh[SUBLANES:]], axis=0)


def _mixer_prompt_kernel(x_ref, norm_ref, win_ref, lng_ref, lnb_ref, ws_ref, bs_ref,
                         cw_ref, cb_ref, lam_ref, wbd_ref, ba_ref, bx_ref, wout_ref,
                         y_ref, h_ref, conv_ref, v_ref,
                         xn_s, hist_s, a_s, b_s, ya_s, gb_s, hc_s):
    i = pl.program_id(0)
    last = pl.num_programs(0) - 1
    nseq, _, d = x_ref.shape
    nlt = d // LANES
    pair = 2 * CHUNK

    @pl.when(i == 0)
    def _():
        hist_s[...] = jnp.zeros_like(hist_s)
        hc_s[...] = jnp.zeros_like(hc_s)

    g1 = norm_ref[...]

    def norm_body(s, c):
        row = pl.multiple_of(s * CHUNK, CHUNK)
        xn_s[pl.ds(row, CHUNK), :] = _rms_norm(x_ref[s], g1).astype(BF16)
        return c

    lax.fori_loop(0, nseq, norm_body, 0)

    lam_half = (-0.5 * LRU_C) * _softplus(-lam_ref[...])
    is_first = i == 0

    def branch_body(p, c):
        row = pl.multiple_of(p * pair, pair)
        xnb = xn_s[pl.ds(row, pair), :]
        v = _layer_norm(_gelu2(_dot(xnb, win_ref[:, d:2 * d])), lng_ref[...], lnb_ref[...],
                        4.0 * EPS)

        @pl.when(i == last)
        def _():
            v_ref[2 * p] = v[:CHUNK]
            v_ref[2 * p + 1] = v[CHUNK:]

        vb = v.astype(BF16)
        dh = d // H_A
        z_rows = []
        for k in range(2):
            zs = [_dot(ws_ref[h], vb[k * CHUNK:(k + 1) * CHUNK, h * dh:(h + 1) * dh])
                  for h in range(H_A)]
            z_rows.append(jnp.concatenate(zs, axis=-1) + bs_ref[...])
        z = jnp.concatenate(z_rows, axis=0)
        ya = _gelu2(_dot(xnb, win_ref[:, 0:d])) * z
        ga2 = 1.0 + jnp.tanh(_dot(xnb, win_ref[:, 3 * d:4 * d]))
        ya_s[pl.ds(row, pair), :] = (ga2 * ya).astype(BF16)
        gb_s[pl.ds(row, pair), :] = (1.0 + jnp.tanh(_dot(xnb, win_ref[:, 4 * d:5 * d]))).astype(BF16)
        xr = _dot(xnb, win_ref[:, 2 * d:3 * d])
        xcs = []
        for k in range(2):
            s = 2 * p + k
            xk = xr[k * CHUNK:(k + 1) * CHUNK]
            hist = hist_s[s]
            xc = cb_ref[...] + xk * cw_ref[CONV_W - 1:CONV_W, :]
            for j in range(1, CONV_W):
                xc = xc + _shift_rows(xk, hist, j) * cw_ref[CONV_W - 1 - j:CONV_W - j, :]
            hist_s[s] = xk[CHUNK - SUBLANES:]
            xcs.append(xc)
        xc = jnp.concatenate(xcs, axis=0)
        a, mult_half, ix2 = _lru_gates(xc, wbd_ref, ba_ref[...], bx_ref[...], lam_half)
        bt = mult_half * ix2
        for k in range(2):
            srow = pl.multiple_of((2 * p + k) * SCAN_PITCH, SUBLANES)
            for j in range(nlt):
                a_s[j, pl.ds(srow, CHUNK), :] = a[k * CHUNK:(k + 1) * CHUNK, j * LANES:(j + 1) * LANES]
                b_s[j, pl.ds(srow, CHUNK), :] = bt[k * CHUNK:(k + 1) * CHUNK, j * LANES:(j + 1) * LANES]

        @pl.when(is_first)
        def _():
            first = lax.broadcasted_iota(jnp.int32, (SUBLANES, 1), 0) == 0
            for k in range(2):
                srow = pl.multiple_of((2 * p + k) * SCAN_PITCH, SUBLANES)
                top = slice(k * CHUNK, k * CHUNK + SUBLANES)
                fixed = jnp.where(first, 0.5 * ix2[top], bt[top])
                for j in range(nlt):
                    b_s[j, pl.ds(srow, SUBLANES), :] = fixed[:, j * LANES:(j + 1) * LANES]

        return c

    lax.fori_loop(0, nseq // 2, branch_body, 0)

    def scan_body(t, hs):
        out = []
        for j in range(nlt):
            a_t = a_s[j, pl.ds(t, nseq, stride=SCAN_PITCH), :]
            b_t = b_s[j, pl.ds(t, nseq, stride=SCAN_PITCH), :]
            h = a_t * hs[j] + b_t
            b_s[j, pl.ds(t, nseq, stride=SCAN_PITCH), :] = h
            out.append(h)
        return tuple(out)

    h0 = tuple(hc_s[:, j * LANES:(j + 1) * LANES] for j in range(nlt))
    hs = lax.fori_loop(0, CHUNK, scan_body, h0)
    hfin = jnp.concatenate(hs, axis=-1)
    hc_s[...] = hfin
    h_ref[...] = hfin

    def out_body(p, c):
        row = pl.multiple_of(p * pair, pair)
        hrows = []
        for k in range(2):
            srow = pl.multiple_of((2 * p + k) * SCAN_PITCH, SUBLANES)
            hrows.append(jnp.concatenate([b_s[j, pl.ds(srow, CHUNK), :] for j in range(nlt)], axis=-1))
        h = jnp.concatenate(hrows, axis=0)
        merged = (ya_s[pl.ds(row, pair), :].astype(F32)
                  + gb_s[pl.ds(row, pair), :].astype(F32) * h).astype(BF16)
        o = _dot(merged, wout_ref[...])
        y_ref[2 * p] = x_ref[2 * p] + o[:CHUNK]
        y_ref[2 * p + 1] = x_ref[2 * p + 1] + o[CHUNK:]
        return c

    lax.fori_loop(0, nseq // 2, out_body, 0)

    @pl.when(i == last)
    def _():
        conv_ref[...] = hist_s[:, SUBLANES - (CONV_W - 1):, :]


def _const_spec(shape):
    nd = len(shape)
    return pl.BlockSpec(shape, lambda i, *_, _n=nd: (0,) * _n, pipeline_mode=pl.Buffered(1))


def _layer_spec(stacked, layer):
    shape = stacked.shape[1:]
    nd = len(shape)
    return pl.BlockSpec((None,) + tuple(shape), lambda i, *_, _n=nd: (layer,) + (0,) * _n,
                        pipeline_mode=pl.Buffered(1))


MIXER_WEIGHTS = ('norm1', 'w_in', 'ln_g', 'ln_b', 'ws', 'bs', 'conv_w', 'conv_b', 'lam', 'wbd',
                 'b_a', 'b_x', 'w_out')


def _mixer_prompt(x, wts, layer):
    nseq, seq, d = x.shape
    assert seq % CHUNK == 0 and nseq == SUBLANES and d % 256 == 0
    n_chunks = seq // CHUNK
    nlt = d // LANES
    weights = tuple(wts[k] for k in MIXER_WEIGHTS)
    x_spec = pl.BlockSpec((nseq, CHUNK, d), lambda i: (0, i, 0))
    out_shape = (
        jax.ShapeDtypeStruct((nseq, seq, d), F32),
        jax.ShapeDtypeStruct((nseq, d), F32),
        jax.ShapeDtypeStruct((nseq, CONV_W - 1, d), F32),
        jax.ShapeDtypeStruct((nseq, CHUNK, d), F32),
    )
    out_specs = (
        x_spec,
        _const_spec((nseq, d)),
        _const_spec((nseq, CONV_W - 1, d)),
        _const_spec((nseq, CHUNK, d)),
    )
    scratch = [
        pltpu.VMEM((nseq * CHUNK, d), BF16),
        pltpu.VMEM((nseq, SUBLANES, d), F32),
        pltpu.VMEM((nlt, nseq * SCAN_PITCH, LANES), F32),
        pltpu.VMEM((nlt, nseq * SCAN_PITCH, LANES), F32),
        pltpu.VMEM((nseq * CHUNK, d), BF16),
        pltpu.VMEM((nseq * CHUNK, d), BF16),
        pltpu.VMEM((nseq, d), F32),
    ]
    return pl.pallas_call(
        _mixer_prompt_kernel,
        grid=(n_chunks,),
        in_specs=[x_spec] + [_layer_spec(w, layer) for w in weights],
        out_specs=out_specs,
        out_shape=out_shape,
        scratch_shapes=scratch,
        compiler_params=pltpu.CompilerParams(
            dimension_semantics=("arbitrary",), vmem_limit_bytes=VMEM_LIMIT),
        name="mixer_prompt",
    )(x, *weights)


def _mixer_sample_kernel(x_ref, h0_ref, cst_ref, norm_ref, win_ref, lng_ref, lnb_ref, wsc_ref,
                         bsc_ref, cw_ref, cb_ref, lam_ref, wbd_ref, ba_ref, bx_ref, wout_ref,
                         y_ref, h_ref, conv_ref, v_ref):
    nt, ns, d = x_ref.shape
    lam_half = (-0.5 * LRU_C) * _softplus(-lam_ref[...])
    g1 = norm_ref[...]
    h = h0_ref[...]
    xp = [cst_ref[k] for k in range(CONV_W - 1)]
    for t in range(nt):
        x = x_ref[t]
        xnb = _rms_norm(x, g1).astype(BF16)
        v = _layer_norm(_gelu2(_dot(xnb, win_ref[:, d:2 * d])), lng_ref[...], lnb_ref[...],
                        4.0 * EPS)
        v_ref[t] = v
        z = bsc_ref[t:t + 1, :]
        for s in range(t + 1):
            z = z + wsc_ref[t, s:s + 1, :] * v_ref[s]
        ya = _gelu2(_dot(xnb, win_ref[:, 0:d])) * z
        ga2 = 1.0 + jnp.tanh(_dot(xnb, win_ref[:, 3 * d:4 * d]))
        xr = _dot(xnb, win_ref[:, 2 * d:3 * d])
        xp.append(xr)
        xc = cb_ref[...]
        for k in range(CONV_W):
            xc = xc + xp[t + k] * cw_ref[k:k + 1, :]
        a, mult_half, ix2 = _lru_gates(xc, wbd_ref, ba_ref[...], bx_ref[...], lam_half)
        h = a * h + mult_half * ix2
        gb2 = 1.0 + jnp.tanh(_dot(xnb, win_ref[:, 4 * d:5 * d]))
        merged = (ga2 * ya + gb2 * h).astype(BF16)
        y_ref[t] = x + _dot(merged, wout_ref[...])
    h_ref[...] = h
    for k in range(CONV_W - 1):
        conv_ref[k] = xp[nt + k]


def _mixer_sample(x, h0, conv_state, wts, layer):
    nt, ns, d = x.shape
    names = MIXER_WEIGHTS[:4] + ('wsc', 'bsc') + MIXER_WEIGHTS[6:]
    stacked = (h0, conv_state) + tuple(wts[k] for k in names)
    out_shape = (
        jax.ShapeDtypeStruct((nt, ns, d), F32),
        jax.ShapeDtypeStruct((ns, d), F32),
        jax.ShapeDtypeStruct((CONV_W - 1, ns, d), F32),
        jax.ShapeDtypeStruct((nt, ns, d), F32),
    )
    return pl.pallas_call(
        _mixer_sample_kernel,
        grid=(1,),
        in_specs=[_const_spec(x.shape)] + [_layer_spec(w, layer) for w in stacked],
        out_specs=tuple(_const_spec(s.shape) for s in out_shape),
        out_shape=out_shape,
        compiler_params=pltpu.CompilerParams(
            dimension_semantics=("arbitrary",), vmem_limit_bytes=VMEM_LIMIT),
        name="mixer_sample",
    )(x, *stacked)


def _route(logits):
    lane = lax.broadcasted_iota(jnp.int32, logits.shape, 1).astype(F32)
    neg = jnp.float32(-jnp.inf)
    big = jnp.float32(ROUTE_LANES)

    def argmax_first(vals):
        m = jnp.max(vals, axis=-1, keepdims=True)
        idx = jnp.min(jnp.where(vals == m, lane, big), axis=-1, keepdims=True)
        return m, idx

    gl = jnp.where(lane < N_GROUPS, logits, neg)
    gm, gi = argmax_first(gl)
    pg_top = 1.0 / jnp.sum(jnp.exp(gl - gm), axis=-1, keepdims=True)
    lo = N_GROUPS + gi * E_PER_GROUP
    el = jnp.where(jnp.logical_and(lane >= lo, lane < lo + E_PER_GROUP), logits, neg)
    m1, i1 = argmax_first(el)
    m2, i2 = argmax_first(jnp.where(lane == i1, neg, el))
    e2 = jnp.exp(m2 - m1)
    w1 = pg_top / (1.0 + e2)
    w2 = pg_top * e2 / (1.0 + e2)
    return i1, i2, w1, w2


def _lane_cumsum_exclusive(x):
    lane = lax.broadcasted_iota(jnp.int32, x.shape, 1)
    acc = x
    shift = 1
    while shift < LANES:
        acc = acc + jnp.where(lane >= shift, pltpu.roll(acc, shift, axis=1), 0.0)
        shift *= 2
    return acc - x


def _to_slab(ref, val):
    t, d = val.shape
    nlt = d // LANES
    for j in range(nlt):
        ref[pl.ds(j, t, stride=nlt), :] = val[:, j * LANES:(j + 1) * LANES]


def _from_slab(ref, t, nlt):
    return jnp.concatenate([ref[pl.ds(j, t, stride=nlt), :] for j in range(nlt)], axis=-1)


def _router_kernel(yp_ref, ys_ref, norm_ref, wr_ref, br_ref, tri_ref, xn_ref, mf_ref, mi_ref,
                   cnt_ref, *, n_prompt_tiles):
    y = jnp.where(pl.program_id(0) < n_prompt_tiles, yp_ref[...], ys_ref[...])
    xn = _rms_norm(y, norm_ref[...])
    _to_slab(xn_ref, xn)
    logits = _dot(xn.astype(BF16), wr_ref[...]) + br_ref[...]
    i1, i2, w1, w2 = _route(logits)
    lane = lax.broadcasted_iota(jnp.int32, logits.shape, 1).astype(F32)
    hot1 = lane == i1
    hot2 = lane == i2
    onehot = jnp.where(jnp.logical_or(hot1, hot2), 1.0, 0.0)
    before = _dot(tri_ref[...], onehot.astype(BF16))
    cnt = jnp.sum(onehot, axis=0, keepdims=True)
    cnt8 = jnp.broadcast_to(cnt, (SUBLANES, ROUTE_LANES))
    seg_lo = _lane_cumsum_exclusive(cnt8)[0:1, :]
    slot = before + seg_lo
    p1 = jnp.sum(jnp.where(hot1, slot, 0.0), axis=-1, keepdims=True)
    p2 = jnp.sum(jnp.where(hot2, slot, 0.0), axis=-1, keepdims=True)
    cols = (i1 - N_GROUPS, i2 - N_GROUPS, w1, w2, p1, p2)
    sel = jnp.zeros_like(logits)
    for k, c in enumerate(cols):
        sel = jnp.where(lane == k, c, sel)
    meta = sel.T[:SUBLANES]
    mf_ref[0] = meta
    mi_ref[0] = meta.astype(jnp.int32)
    cnt_ref[0] = cnt8


def _token_specs(tile, d, n_prompt_tiles):
    last = n_prompt_tiles - 1
    return (pl.BlockSpec((tile, d), lambda i, *_: (jnp.minimum(i, last), 0)),
            pl.BlockSpec((tile, d), lambda i, *_: (jnp.maximum(i - n_prompt_tiles, 0), 0)))


def _router(yp, ys, wts, layer, tile):
    d = yp.shape[1]
    nlt = d // LANES
    n_prompt_tiles = yp.shape[0] // tile
    n_tiles = n_prompt_tiles + ys.shape[0] // tile
    tri = jnp.tril(jnp.ones((tile, tile), BF16), -1)
    meta_spec = pl.BlockSpec((1, SUBLANES, tile), lambda i: (i, 0, 0))
    return pl.pallas_call(
        functools.partial(_router_kernel, n_prompt_tiles=n_prompt_tiles),
        grid=(n_tiles,),
        in_specs=list(_token_specs(tile, d, n_prompt_tiles)) + [
            _layer_spec(wts['norm2'], layer), _layer_spec(wts['w_route'], layer),
            _layer_spec(wts['b_route'], layer), _const_spec((tile, tile))],
        out_specs=(pl.BlockSpec((tile * nlt, LANES), lambda i: (i, 0)), meta_spec, meta_spec,
                   pl.BlockSpec((1, SUBLANES, ROUTE_LANES), lambda i: (i, 0, 0))),
        out_shape=(jax.ShapeDtypeStruct((n_tiles * tile * nlt, LANES), F32),
                   jax.ShapeDtypeStruct((n_tiles, SUBLANES, tile), F32),
                   jax.ShapeDtypeStruct((n_tiles, SUBLANES, tile), jnp.int32),
                   jax.ShapeDtypeStruct((n_tiles, SUBLANES, ROUTE_LANES), F32)),
        compiler_params=pltpu.CompilerParams(
            dimension_semantics=("arbitrary",), vmem_limit_bytes=VMEM_LIMIT),
        name="moe_router",
    )(yp, ys, wts['norm2'], wts['w_route'], wts['b_route'], tri)


def _run_copy(tile_idx, e, lo_ref, cnt_ref, dst_ref, local, remote, sem, nlt, to_remote):
    k = tile_idx * N_EXPERTS + e
    n = cnt_ref[k] * nlt
    loc = local.at[pl.ds(pl.multiple_of(lo_ref[k] * nlt, nlt), n)]
    rem = remote.at[pl.ds(pl.multiple_of(dst_ref[k] * nlt, nlt), n)]
    return pltpu.make_async_copy(loc, rem, sem) if to_remote else pltpu.make_async_copy(rem, loc, sem)


def _start_runs(tile_idx, lo_ref, cnt_ref, dst_ref, local, remote, sem, nlt, to_remote):
    def body(e, c):
        @pl.when(cnt_ref[tile_idx * N_EXPERTS + e] > 0)
        def _():
            _run_copy(tile_idx, e, lo_ref, cnt_ref, dst_ref, local, remote, sem, nlt,
                      to_remote).start()
        return c
    lax.fori_loop(0, N_EXPERTS, body, 0)


def _wait_runs(local, sem):
    pltpu.make_async_copy(local, local, sem).wait()


def _dispatch_kernel(lo_ref, cnt_ref, dst_ref, padst_ref, padn_ref,
                     s1_ref, s2_ref, x_ref, xs_ref, loc, zeros, sem, *, nlt, unroll):
    i = pl.program_id(0)
    last = pl.num_programs(0) - 1
    tile = x_ref.shape[0] // nlt
    slot = i % 2
    buf = loc.at[slot]

    @pl.when(i >= 2)
    def _():
        _wait_runs(buf, sem.at[slot])

    def body(tt, c):
        for u in range(unroll):
            t = tt * unroll + u
            row = x_ref[pl.ds(pl.multiple_of(t * nlt, nlt), nlt), :]
            buf[pl.ds(pl.multiple_of(s1_ref[t] * nlt, nlt), nlt), :] = row
            buf[pl.ds(pl.multiple_of(s2_ref[t] * nlt, nlt), nlt), :] = row
        return c

    lax.fori_loop(0, tile // unroll, body, 0)
    _start_runs(i, lo_ref, cnt_ref, dst_ref, buf, xs_ref, sem.at[slot], nlt, True)

    @pl.when(i == last)
    def _():
        zeros[...] = jnp.zeros_like(zeros)

        def pad_copy(e):
            n = padn_ref[e] * nlt
            dst = xs_ref.at[pl.ds(pl.multiple_of(padst_ref[e] * nlt, nlt), n)]
            return pltpu.make_async_copy(zeros.at[pl.ds(0, n)], dst, sem.at[2])

        def each_pad(fn):
            def b(e, c):
                @pl.when(padn_ref[e] > 0)
                def _():
                    fn(e)
                return c
            lax.fori_loop(0, padn_ref.shape[0], b, 0)

        each_pad(lambda e: pad_copy(e).start())
        _wait_runs(buf, sem.at[slot])

        @pl.when(i >= 1)
        def _():
            _wait_runs(loc.at[1 - slot], sem.at[1 - slot])

        each_pad(lambda e: pad_copy(e).wait())


def _smem_tile(tile):
    return pl.BlockSpec((tile,), lambda i, *_: (i,), memory_space=pltpu.SMEM)


def _dispatch(xn_slab, s1, s2, tables, tile, n_rows, nlt):
    n_tiles = xn_slab.shape[0] // (tile * nlt)
    return pl.pallas_call(
        functools.partial(_dispatch_kernel, nlt=nlt, unroll=8),
        grid_spec=pltpu.PrefetchScalarGridSpec(
            num_scalar_prefetch=5,
            grid=(n_tiles,),
            in_specs=[_smem_tile(tile), _smem_tile(tile),
                      pl.BlockSpec((tile * nlt, LANES), lambda i, *_: (i, 0))],
            out_specs=pl.BlockSpec(memory_space=pl.ANY),
            scratch_shapes=[pltpu.VMEM((2, 2 * tile * nlt, LANES), F32),
                            pltpu.VMEM((MOE_BLOCK * nlt, LANES), F32),
                            pltpu.SemaphoreType.DMA((3,))]),
        out_shape=jax.ShapeDtypeStruct((n_rows * nlt, LANES), F32),
        compiler_params=pltpu.CompilerParams(
            dimension_semantics=("arbitrary",), vmem_limit_bytes=VMEM_LIMIT),
        name="moe_dispatch",
    )(*tables, s1, s2, xn_slab)


def _experts_kernel(blk_ref, nused_ref, xs_ref, wg_ref, wu_ref, wd_ref, ys_ref,
                    wg_s, wu_s, wd_s, *, nlt):
    b = pl.program_id(0)
    rows = xs_ref.shape[0] // nlt

    @pl.when(jnp.logical_or(b == 0, blk_ref[b] != blk_ref[jnp.maximum(b - 1, 0)]))
    def _():
        wg_s[...] = wg_ref[...].astype(BF16)
        wu_s[...] = wu_ref[...].astype(BF16)
        wd_s[...] = wd_ref[...].astype(BF16)

    @pl.when(b < nused_ref[0])
    def _():
        x = _from_slab(xs_ref, rows, nlt).astype(BF16)
        hg = _dot(x, wg_s[...])
        hu = _dot(x, wu_s[...])
        h = (hg * _sigmoid(hg) * hu).astype(BF16)
        _to_slab(ys_ref, _dot(h, wd_s[...]))

    @pl.when(b >= nused_ref[0])
    def _():
        ys_ref[...] = jnp.zeros_like(ys_ref)


def _experts(xs, blk_expert, n_used, wts, layer, nlt):
    n_blocks = xs.shape[0] // (MOE_BLOCK * nlt)
    d = nlt * LANES
    f = wts['w_gate'].shape[-1]
    rows = pl.BlockSpec((MOE_BLOCK * nlt, LANES), lambda b, blk, nu: (b, 0))
    w_spec = lambda r, c: pl.BlockSpec((None, None, r, c), lambda b, blk, nu: (layer, blk[b], 0, 0))
    return pl.pallas_call(
        functools.partial(_experts_kernel, nlt=nlt),
        grid_spec=pltpu.PrefetchScalarGridSpec(
            num_scalar_prefetch=2,
            grid=(n_blocks,),
            in_specs=[rows, w_spec(d, f), w_spec(d, f), w_spec(f, d)],
            out_specs=rows,
            scratch_shapes=[pltpu.VMEM((d, f), BF16), pltpu.VMEM((d, f), BF16),
                            pltpu.VMEM((f, d), BF16)]),
        out_shape=jax.ShapeDtypeStruct(xs.shape, F32),
        compiler_params=pltpu.CompilerParams(
            dimension_semantics=("arbitrary",), vmem_limit_bytes=VMEM_LIMIT),
        name="moe_experts",
    )(blk_expert, n_used, xs, wts['w_gate'], wts['w_up'], wts['w_down'])


def _combine_kernel(lo_ref, cnt_ref, dst_ref, s1_ref, s2_ref, w1_ref, w2_ref, yp_ref, ys_ref,
                    nf_ref, rows_ref, op_ref, os_ref, loc, acc, sem,
                    *, nlt, unroll, final_norm, n_prompt_tiles):
    i = pl.program_id(0)
    tile = yp_ref.shape[0]
    slot = i % 2
    buf = loc.at[slot]
    fetch = lambda t, s: _start_runs(t, lo_ref, cnt_ref, dst_ref, loc.at[s], rows_ref, sem.at[s],
                                     nlt, False)

    @pl.when(i == 0)
    def _():
        fetch(0, 0)

    @pl.when(i + 1 < pl.num_programs(0))
    def _():
        fetch(i + 1, 1 - slot)

    _wait_runs(buf, sem.at[slot])

    def body(tt, c):
        for u in range(unroll):
            t = tt * unroll + u
            r1 = buf[pl.ds(pl.multiple_of(s1_ref[t] * nlt, nlt), nlt), :]
            r2 = buf[pl.ds(pl.multiple_of(s2_ref[t] * nlt, nlt), nlt), :]
            acc[pl.ds(pl.multiple_of(t * nlt, nlt), nlt), :] = w1_ref[t] * r1 + w2_ref[t] * r2
        return c

    lax.fori_loop(0, tile // unroll, body, 0)
    is_prompt = i < n_prompt_tiles
    out = jnp.where(is_prompt, yp_ref[...], ys_ref[...]) + _from_slab(acc, tile, nlt)
    if final_norm:
        out = _rms_norm(out, nf_ref[...])

    @pl.when(is_prompt)
    def _():
        op_ref[...] = out

    @pl.when(jnp.logical_not(is_prompt))
    def _():
        os_ref[...] = out


def _combine(yp, ys, rows, meta, tables, norm_f, final_norm, tile, nlt):
    d = yp.shape[1]
    n_prompt_tiles = yp.shape[0] // tile
    n_tiles = n_prompt_tiles + ys.shape[0] // tile
    tok_p, tok_s = _token_specs(tile, d, n_prompt_tiles)
    return pl.pallas_call(
        functools.partial(_combine_kernel, nlt=nlt, unroll=8, final_norm=final_norm,
                          n_prompt_tiles=n_prompt_tiles),
        grid_spec=pltpu.PrefetchScalarGridSpec(
            num_scalar_prefetch=3,
            grid=(n_tiles,),
            in_specs=[_smem_tile(tile)] * 4 + [tok_p, tok_s, _const_spec((1, d)),
                                               pl.BlockSpec(memory_space=pl.ANY)],
            out_specs=(tok_p, tok_s),
            scratch_shapes=[pltpu.VMEM((2, 2 * tile * nlt, LANES), F32),
                            pltpu.VMEM((tile * nlt, LANES), F32),
                            pltpu.SemaphoreType.DMA((2,))]),
        out_shape=(jax.ShapeDtypeStruct(yp.shape, F32), jax.ShapeDtypeStruct(ys.shape, F32)),
        compiler_params=pltpu.CompilerParams(
            dimension_semantics=("arbitrary",), vmem_limit_bytes=VMEM_LIMIT),
        name="moe_combine",
    )(*tables, *meta, yp, ys, norm_f, rows)


def _moe(yp, ys, wts, layer, norm_f, final_norm, tile):
    d = yp.shape[1]
    n = yp.shape[0] + ys.shape[0]
    assert yp.shape[0] % tile == 0 and ys.shape[0] % tile == 0 and d % LANES == 0
    nlt = d // LANES
    xn_slab, meta_f, meta_i, cnt = _router(yp, ys, wts, layer, tile)
    cnt = cnt[:, 0, N_GROUPS:N_GROUPS + N_EXPERTS].astype(jnp.int32)
    lo = jnp.cumsum(cnt, axis=1) - cnt
    seg = jnp.sum(cnt, axis=0)
    seg_pad = (seg + MOE_BLOCK - 1) // MOE_BLOCK * MOE_BLOCK
    seg_end = jnp.cumsum(seg_pad)
    seg_start = seg_end - seg_pad
    dst = seg_start[None, :] + jnp.cumsum(cnt, axis=0) - cnt
    n_blocks = (2 * n + N_EXPERTS * (MOE_BLOCK - 1)) // MOE_BLOCK
    blk_ids = jnp.arange(n_blocks, dtype=jnp.int32)
    blk_expert = jnp.minimum(
        jnp.sum((seg_end[None, :] <= blk_ids[:, None] * MOE_BLOCK).astype(jnp.int32), axis=1),
        N_EXPERTS - 1)
    n_used = seg_end[-1:] // MOE_BLOCK
    pad_start = jnp.concatenate([seg_start + seg, blk_ids * MOE_BLOCK])
    pad_rows = jnp.concatenate([seg_pad - seg, jnp.where(blk_ids >= n_used[0], MOE_BLOCK, 0)])
    flat = lambda a: a.reshape(-1)
    runs = (flat(lo), flat(cnt), flat(dst))
    s1, s2 = flat(meta_i[:, 4]), flat(meta_i[:, 5])
    w1, w2 = flat(meta_f[:, 2]), flat(meta_f[:, 3])
    xs = _dispatch(xn_slab, s1, s2, runs + (pad_start, pad_rows), tile, n_blocks * MOE_BLOCK, nlt)
    rows = _experts(xs, blk_expert, n_used, wts, layer, nlt)
    return _combine(yp, ys, rows, (s1, s2, w1, w2), runs, norm_f, final_norm, tile, nlt)


def _prep_weights(norm1, w_in, ln_g, ln_b, w_s, b_s, conv_w, conv_b, lru_lambda, w_rg_a, b_rg_a,
                  w_rg_x, b_rg_x, w_out, norm2, w_route_group, b_route_group, w_route_expert,
                  b_route_expert, w_gate, w_up, w_down, n_sample_t):
    depth, d = w_in.shape[0], w_in.shape[1]
    dh = d // H_A
    row = lambda a: a.reshape(depth, 1, -1).astype(F32)
    causal = jnp.tril(jnp.ones((CHUNK, CHUNK), dtype=bool))
    ws = jnp.where(causal, 0.5 * w_s, 0.0)
    bs = jnp.repeat(jnp.swapaxes(0.5 * b_s, 1, 2), dh, axis=2)
    in_scale = jnp.concatenate([jnp.ones((3 * d,), F32), jnp.full((2 * d,), 0.5, F32)])

    def pair_bd(w):
        w = w.reshape(depth, H_B // 2, 2, w.shape[-2], w.shape[-1])
        z = jnp.zeros_like(w[:, :, 0])
        return jnp.concatenate([jnp.concatenate([w[:, :, 0], z], axis=-1),
                                jnp.concatenate([z, w[:, :, 1]], axis=-1)], axis=-2)

    pad = ROUTE_LANES - N_GROUPS - N_EXPERTS
    w_route = jnp.concatenate(
        [w_route_group, w_route_expert, jnp.zeros((depth, d, pad), F32)], axis=-1)
    b_route = jnp.concatenate(
        [b_route_group, b_route_expert, jnp.zeros((depth, pad), F32)], axis=-1)
    nt = n_sample_t
    return dict(
        norm1=row(norm1), w_in=(w_in * in_scale).astype(BF16), ln_g=row(ln_g), ln_b=row(ln_b),
        ws=ws.astype(BF16), bs=bs.astype(F32), conv_w=conv_w.astype(F32), conv_b=row(conv_b),
        lam=row(lru_lambda),
        wbd=(0.5 * jnp.concatenate([pair_bd(w_rg_a), pair_bd(w_rg_x)], axis=-1)).astype(BF16),
        b_a=row(0.5 * b_rg_a), b_x=row(0.5 * b_rg_x), w_out=(0.5 * w_out).astype(BF16),
        norm2=row(norm2), w_route=w_route.astype(BF16), b_route=row(b_route),
        w_gate=w_gate, w_up=w_up, w_down=w_down,
        wsc=jnp.repeat(jnp.transpose(ws[:, :, :nt, :nt], (0, 2, 3, 1)), dh, axis=3).astype(F32),
        bsc=bs[:, :nt].astype(F32),
    )


def kernel(x_prompt, x_sample, state_lru_h, state_conv, norm1, w_in, ln_g, ln_b, w_s, b_s, conv_w, conv_b, lru_lambda, w_rg_a, b_rg_a, w_rg_x, b_rg_x, w_out, norm2, w_route_group, b_route_group, w_route_expert, b_route_expert, w_gate, w_up, w_down, norm_f):
    depth = w_in.shape[0]
    nb, seq, d = x_prompt.shape
    ns, nt, _ = x_sample.shape
    assert nt <= CHUNK
    nf = norm_f.reshape(1, d).astype(F32)
    wts = _prep_weights(norm1, w_in, ln_g, ln_b, w_s, b_s, conv_w, conv_b, lru_lambda, w_rg_a,
                        b_rg_a, w_rg_x, b_rg_x, w_out, norm2, w_route_group, b_route_group,
                        w_route_expert, b_route_expert, w_gate, w_up, w_down, nt)
    conv_state = jnp.transpose(state_conv, (0, 2, 1, 3))
    tile = min(MOE_TILE, nt * ns)
    yp = x_prompt
    ys = jnp.transpose(x_sample, (1, 0, 2))
    hp, cp, vp, hs, cs, vs = [], [], [], [], [], []
    for l in range(depth):
        yp, h, c, v = _mixer_prompt(yp, wts, l)
        hp.append(h)
        cp.append(c)
        vp.append(v)
        ys, h, c, v = _mixer_sample(ys, state_lru_h, conv_state, wts, l)
        hs.append(h)
        cs.append(c)
        vs.append(v)
        yp, ys = _moe(yp.reshape(nb * seq, d), ys.reshape(nt * ns, d), wts, l, nf,
                      l == depth - 1, tile)
        yp = yp.reshape(nb, seq, d)
        ys = ys.reshape(nt, ns, d)
    to_seq_major = lambda a: jnp.transpose(jnp.stack(a), (0, 2, 1, 3))
    return (yp, jnp.transpose(ys, (1, 0, 2)), jnp.stack(hp), jnp.stack(cp), jnp.stack(vp),
            jnp.stack(hs), to_seq_major(cs), to_seq_major(vs))
```

```python
import functools

import jax
import jax.numpy as jnp
from jax import lax
from jax.experimental import pallas as pl
from jax.experimental.pallas import tpu as pltpu

F32 = jnp.float32
BF16 = jnp.bfloat16

CHUNK = 128
H_A = 4
H_B = 8
CONV_W = 4
LRU_C = 8.0
N_GROUPS = 4
E_PER_GROUP = 8
N_EXPERTS = N_GROUPS * E_PER_GROUP
EPS = 1e-6

LANES = 128
SUBLANES = 8
SCAN_PITCH = CHUNK + SUBLANES
ROUTE_LANES = 128
MOE_TILE = 512
MOE_BLOCK = 256
VMEM_LIMIT = 60 * 1024 * 1024


def _dot(a, b):
    return jnp.dot(a, b, preferred_element_type=F32)


def _sigmoid(x):
    return 0.5 * (jnp.tanh(0.5 * x) + 1.0)


def _gelu2(x):
    c = 0.7978845608028654
    return x * (1.0 + jnp.tanh(x * (c + (c * 0.044715) * (x * x))))


def _rms_norm(x, g):
    return x * lax.rsqrt(jnp.mean(x * x, axis=-1, keepdims=True) + EPS) * g


def _layer_norm(x, g, b, eps):
    xc = x - jnp.mean(x, axis=-1, keepdims=True)
    var = jnp.mean(xc * xc, axis=-1, keepdims=True)
    return xc * lax.rsqrt(var + eps) * g + b


def _softplus(x):
    return jnp.maximum(x, 0.0) + jnp.log(1.0 + jnp.exp(-jnp.abs(x)))


def _lru_gates(xc, wbd_ref, b_a, b_x, lam_half):
    d = xc.shape[-1]
    xcb = xc.astype(BF16)
    r_parts, i_parts = [], []
    for q in range(d // 256):
        ri = _dot(xcb[:, q * 256:(q + 1) * 256], wbd_ref[q])
        r_parts.append(ri[:, :256])
        i_parts.append(ri[:, 256:])
    r2 = 1.0 + jnp.tanh(jnp.concatenate(r_parts, axis=-1) + b_a)
    i2 = 1.0 + jnp.tanh(jnp.concatenate(i_parts, axis=-1) + b_x)
    a = jnp.exp(lam_half * r2)
    mult_half = jnp.sqrt(0.25 - 0.25 * (a * a))
    return a, mult_half, i2 * xc


def _shift_rows(x, hist, j):
    sh = pltpu.roll(x, j, axis=0)
    top = jnp.where(lax.broadcasted_iota(jnp.int32, (SUBLANES, 1), 0) < j,
                    pltpu.roll(hist, j, axis=0), sh[:SUBLANES])
    return jnp.concatenate([top, sh[SUBLANES:]], axis=0)


def _mixer_prompt_kernel(x_ref, norm_ref, win_ref, lng_ref, lnb_ref, ws_ref, bs_ref,
                         cw_ref, cb_ref, lam_ref, wbd_ref, ba_ref, bx_ref, wout_ref,
                         y_ref, h_ref, conv_ref, v_ref,
                         xn_s, hist_s, a_s, b_s, ya_s, gb_s, hc_s):
    i = pl.program_id(0)
    last = pl.num_programs(0) - 1
    nseq, _, d = x_ref.shape
    nlt = d // LANES
    pair = 2 * CHUNK

    @pl.when(i == 0)
    def _():
        hist_s[...] = jnp.zeros_like(hist_s)
        hc_s[...] = jnp.zeros_like(hc_s)

    g1 = norm_ref[...]

    def norm_body(s, c):
        row = pl.multiple_of(s * CHUNK, CHUNK)
        xn_s[pl.ds(row, CHUNK), :] = _rms_norm(x_ref[s], g1).astype(BF16)
        return c

    lax.fori_loop(0, nseq, norm_body, 0)

    lam_half = (-0.5 * LRU_C) * _softplus(-lam_ref[...])
    is_first = i == 0

    def branch_body(p, c):
        row = pl.multiple_of(p * pair, pair)
        xnb = xn_s[pl.ds(row, pair), :]
        v = _layer_norm(_gelu2(_dot(xnb, win_ref[:, d:2 * d])), lng_ref[...], lnb_ref[...],
                        4.0 * EPS)

        @pl.when(i == last)
        def _():
            v_ref[2 * p] = v[:CHUNK]
            v_ref[2 * p + 1] = v[CHUNK:]

        vb = v.astype(BF16)
        dh = d // H_A
        z_rows = []
        for k in range(2):
            zs = [_dot(ws_ref[h], vb[k * CHUNK:(k + 1) * CHUNK, h * dh:(h + 1) * dh])
                  for h in range(H_A)]
            z_rows.append(jnp.concatenate(zs, axis=-1) + bs_ref[...])
        z = jnp.concatenate(z_rows, axis=0)
        ya = _gelu2(_dot(xnb, win_ref[:, 0:d])) * z
        ga2 = 1.0 + jnp.tanh(_dot(xnb, win_ref[:, 3 * d:4 * d]))
        ya_s[pl.ds(row, pair), :] = (ga2 * ya).astype(BF16)
        gb_s[pl.ds(row, pair), :] = (1.0 + jnp.tanh(_dot(xnb, win_ref[:, 4 * d:5 * d]))).astype(BF16)
        xr = _dot(xnb, win_ref[:, 2 * d:3 * d])
        xcs = []
        for k in range(2):
            s = 2 * p + k
            xk = xr[k * CHUNK:(k + 1) * CHUNK]
            hist = hist_s[s]
            xc = cb_ref[...] + xk * cw_ref[CONV_W - 1:CONV_W, :]
            for j in range(1, CONV_W):
                xc = xc + _shift_rows(xk, hist, j) * cw_ref[CONV_W - 1 - j:CONV_W - j, :]
            hist_s[s] = xk[CHUNK - SUBLANES:]
            xcs.append(xc)
        xc = jnp.concatenate(xcs, axis=0)
        a, mult_half, ix2 = _lru_gates(xc, wbd_ref, ba_ref[...], bx_ref[...], lam_half)
        bt = mult_half * ix2
        for k in range(2):
            srow = pl.multiple_of((2 * p + k) * SCAN_PITCH, SUBLANES)
            for j in range(nlt):
                a_s[j, pl.ds(srow, CHUNK), :] = a[k * CHUNK:(k + 1) * CHUNK, j * LANES:(j + 1) * LANES]
                b_s[j, pl.ds(srow, CHUNK), :] = bt[k * CHUNK:(k + 1) * CHUNK, j * LANES:(j + 1) * LANES]

        @pl.when(is_first)
        def _():
            first = lax.broadcasted_iota(jnp.int32, (SUBLANES, 1), 0) == 0
            for k in range(2):
                srow = pl.multiple_of((2 * p + k) * SCAN_PITCH, SUBLANES)
                top = slice(k * CHUNK, k * CHUNK + SUBLANES)
                fixed = jnp.where(first, 0.5 * ix2[top], bt[top])
                for j in range(nlt):
                    b_s[j, pl.ds(srow, SUBLANES), :] = fixed[:, j * LANES:(j + 1) * LANES]

        return c

    lax.fori_loop(0, nseq // 2, branch_body, 0)

    def scan_body(t, hs):
        out = []
        for j in range(nlt):
            a_t = a_s[j, pl.ds(t, nseq, stride=SCAN_PITCH), :]
            b_t = b_s[j, pl.ds(t, nseq, stride=SCAN_PITCH), :]
            h = a_t * hs[j] + b_t
            b_s[j, pl.ds(t, nseq, stride=SCAN_PITCH), :] = h
            out.append(h)
        return tuple(out)

    h0 = tuple(hc_s[:, j * LANES:(j + 1) * LANES] for j in range(nlt))
    hs = lax.fori_loop(0, CHUNK, scan_body, h0)
    hfin = jnp.concatenate(hs, axis=-1)
    hc_s[...] = hfin
    h_ref[...] = hfin

    def out_body(p, c):
        row = pl.multiple_of(p * pair, pair)
        hrows = []
        for k in range(2):
            srow = pl.multiple_of((2 * p + k) * SCAN_PITCH, SUBLANES)
            hrows.append(jnp.concatenate([b_s[j, pl.ds(srow, CHUNK), :] for j in range(nlt)], axis=-1))
        h = jnp.concatenate(hrows, axis=0)
        merged = (ya_s[pl.ds(row, pair), :].astype(F32)
                  + gb_s[pl.ds(row, pair), :].astype(F32) * h).astype(BF16)
        o = _dot(merged, wout_ref[...])
        y_ref[2 * p] = x_ref[2 * p] + o[:CHUNK]
        y_ref[2 * p + 1] = x_ref[2 * p + 1] + o[CHUNK:]
        return c

    lax.fori_loop(0, nseq // 2, out_body, 0)

    @pl.when(i == last)
    def _():
        conv_ref[...] = hist_s[:, SUBLANES - (CONV_W - 1):, :]


def _const_spec(shape):
    nd = len(shape)
    return pl.BlockSpec(shape, lambda i, *_, _n=nd: (0,) * _n, pipeline_mode=pl.Buffered(1))


def _layer_spec(stacked, layer):
    shape = stacked.shape[1:]
    nd = len(shape)
    return pl.BlockSpec((None,) + tuple(shape), lambda i, *_, _n=nd: (layer,) + (0,) * _n,
                        pipeline_mode=pl.Buffered(1))


MIXER_WEIGHTS = ('norm1', 'w_in', 'ln_g', 'ln_b', 'ws', 'bs', 'conv_w', 'conv_b', 'lam', 'wbd',
                 'b_a', 'b_x', 'w_out')


def _mixer_prompt(x, wts, layer):
    nseq, seq, d = x.shape
    assert seq % CHUNK == 0 and nseq == SUBLANES and d % 256 == 0
    n_chunks = seq // CHUNK
    nlt = d // LANES
    weights = tuple(wts[k] for k in MIXER_WEIGHTS)
    x_spec = pl.BlockSpec((nseq, CHUNK, d), lambda i: (0, i, 0))
    out_shape = (
        jax.ShapeDtypeStruct((nseq, seq, d), F32),
        jax.ShapeDtypeStruct((nseq, d), F32),
        jax.ShapeDtypeStruct((nseq, CONV_W - 1, d), F32),
        jax.ShapeDtypeStruct((nseq, CHUNK, d), F32),
    )
    out_specs = (
        x_spec,
        _const_spec((nseq, d)),
        _const_spec((nseq, CONV_W - 1, d)),
        _const_spec((nseq, CHUNK, d)),
    )
    scratch = [
        pltpu.VMEM((nseq * CHUNK, d), BF16),
        pltpu.VMEM((nseq, SUBLANES, d), F32),
        pltpu.VMEM((nlt, nseq * SCAN_PITCH, LANES), F32),
        pltpu.VMEM((nlt, nseq * SCAN_PITCH, LANES), F32),
        pltpu.VMEM((nseq * CHUNK, d), BF16),
        pltpu.VMEM((nseq * CHUNK, d), BF16),
        pltpu.VMEM((nseq, d), F32),
    ]
    return pl.pallas_call(
        _mixer_prompt_kernel,
        grid=(n_chunks,),
        in_specs=[x_spec] + [_layer_spec(w, layer) for w in weights],
        out_specs=out_specs,
        out_shape=out_shape,
        scratch_shapes=scratch,
        compiler_params=pltpu.CompilerParams(
            dimension_semantics=("arbitrary",), vmem_limit_bytes=VMEM_LIMIT),
        name="mixer_prompt",
    )(x, *weights)


def _mixer_sample_kernel(x_ref, h0_ref, cst_ref, norm_ref, win_ref, lng_ref, lnb_ref, wsc_ref,
                         bsc_ref, cw_ref, cb_ref, lam_ref, wbd_ref, ba_ref, bx_ref, wout_ref,
                         y_ref, h_ref, conv_ref, v_ref):
    nt, ns, d = x_ref.shape
    lam_half = (-0.5 * LRU_C) * _softplus(-lam_ref[...])
    g1 = norm_ref[...]
    h = h0_ref[...]
    xp = [cst_ref[k] for k in range(CONV_W - 1)]
    for t in range(nt):
        x = x_ref[t]
        xnb = _rms_norm(x, g1).astype(BF16)
        v = _layer_norm(_gelu2(_dot(xnb, win_ref[:, d:2 * d])), lng_ref[...], lnb_ref[...],
                        4.0 * EPS)
        v_ref[t] = v
        z = bsc_ref[t:t + 1, :]
        for s in range(t + 1):
            z = z + wsc_ref[t, s:s + 1, :] * v_ref[s]
        ya = _gelu2(_dot(xnb, win_ref[:, 0:d])) * z
        ga2 = 1.0 + jnp.tanh(_dot(xnb, win_ref[:, 3 * d:4 * d]))
        xr = _dot(xnb, win_ref[:, 2 * d:3 * d])
        xp.append(xr)
        xc = cb_ref[...]
        for k in range(CONV_W):
            xc = xc + xp[t + k] * cw_ref[k:k + 1, :]
        a, mult_half, ix2 = _lru_gates(xc, wbd_ref, ba_ref[...], bx_ref[...], lam_half)
        h = a * h + mult_half * ix2
        gb2 = 1.0 + jnp.tanh(_dot(xnb, win_ref[:, 4 * d:5 * d]))
        merged = (ga2 * ya + gb2 * h).astype(BF16)
        y_ref[t] = x + _dot(merged, wout_ref[...])
    h_ref[...] = h
    for k in range(CONV_W - 1):
        conv_ref[k] = xp[nt + k]


def _mixer_sample(x, h0, conv_state, wts, layer):
    nt, ns, d = x.shape
    names = MIXER_WEIGHTS[:4] + ('wsc', 'bsc') + MIXER_WEIGHTS[6:]
    stacked = (h0, conv_state) + tuple(wts[k] for k in names)
    out_shape = (
        jax.ShapeDtypeStruct((nt, ns, d), F32),
        jax.ShapeDtypeStruct((ns, d), F32),
        jax.ShapeDtypeStruct((CONV_W - 1, ns, d), F32),
        jax.ShapeDtypeStruct((nt, ns, d), F32),
    )
    return pl.pallas_call(
        _mixer_sample_kernel,
        grid=(1,),
        in_specs=[_const_spec(x.shape)] + [_layer_spec(w, layer) for w in stacked],
        out_specs=tuple(_const_spec(s.shape) for s in out_shape),
        out_shape=out_shape,
        compiler_params=pltpu.CompilerParams(
            dimension_semantics=("arbitrary",), vmem_limit_bytes=VMEM_LIMIT),
        name="mixer_sample",
    )(x, *stacked)


def _route(logits):
    lane = lax.broadcasted_iota(jnp.int32, logits.shape, 1).astype(F32)
    neg = jnp.float32(-jnp.inf)
    big = jnp.float32(ROUTE_LANES)

    def argmax_first(vals):
        m = jnp.max(vals, axis=-1, keepdims=True)
        idx = jnp.min(jnp.where(vals == m, lane, big), axis=-1, keepdims=True)
        return m, idx

    gl = jnp.where(lane < N_GROUPS, logits, neg)
    gm, gi = argmax_first(gl)
    pg_top = 1.0 / jnp.sum(jnp.exp(gl - gm), axis=-1, keepdims=True)
    lo = N_GROUPS + gi * E_PER_GROUP
    el = jnp.where(jnp.logical_and(lane >= lo, lane < lo + E_PER_GROUP), logits, neg)
    m1, i1 = argmax_first(el)
    m2, i2 = argmax_first(jnp.where(lane == i1, neg, el))
    e2 = jnp.exp(m2 - m1)
    w1 = pg_top / (1.0 + e2)
    w2 = pg_top * e2 / (1.0 + e2)
    return i1, i2, w1, w2


def _lane_cumsum_exclusive(x):
    lane = lax.broadcasted_iota(jnp.int32, x.shape, 1)
    acc = x
    shift = 1
    while shift < LANES:
        acc = acc + jnp.where(lane >= shift, pltpu.roll(acc, shift, axis=1), 0.0)
        shift *= 2
    return acc - x


def _to_slab(ref, val):
    t, d = val.shape
    nlt = d // LANES
    for j in range(nlt):
        ref[pl.ds(j, t, stride=nlt), :] = val[:, j * LANES:(j + 1) * LANES]


def _from_slab(ref, t, nlt):
    return jnp.concatenate([ref[pl.ds(j, t, stride=nlt), :] for j in range(nlt)], axis=-1)


def _pack_bf16_pairs(x):
    half = x.shape[1] // 2
    bits = lax.bitcast_convert_type(x.astype(BF16).astype(F32), jnp.uint32)
    return (bits[:, :half] >> 16) | bits[:, half:]


def _unpack_bf16_pairs(p):
    lo = lax.bitcast_convert_type(p << 16, F32)
    hi = lax.bitcast_convert_type(p & jnp.uint32(0xFFFF0000), F32)
    return lo, hi


def _router_kernel(yp_ref, ys_ref, norm_ref, wr_ref, br_ref, tri_ref, xn_ref, mf_ref, mi_ref,
                   cnt_ref, *, n_prompt_tiles):
    y = jnp.where(pl.program_id(0) < n_prompt_tiles, yp_ref[...], ys_ref[...])
    xn = _rms_norm(y, norm_ref[...])
    _to_slab(xn_ref, _pack_bf16_pairs(xn))
    logits = _dot(xn.astype(BF16), wr_ref[...]) + br_ref[...]
    i1, i2, w1, w2 = _route(logits)
    lane = lax.broadcasted_iota(jnp.int32, logits.shape, 1).astype(F32)
    hot1 = lane == i1
    hot2 = lane == i2
    onehot = jnp.where(jnp.logical_or(hot1, hot2), 1.0, 0.0)
    before = _dot(tri_ref[...], onehot.astype(BF16))
    cnt = jnp.sum(onehot, axis=0, keepdims=True)
    cnt8 = jnp.broadcast_to(cnt, (SUBLANES, ROUTE_LANES))
    seg_lo = _lane_cumsum_exclusive(cnt8)[0:1, :]
    slot = before + seg_lo
    p1 = jnp.sum(jnp.where(hot1, slot, 0.0), axis=-1, keepdims=True)
    p2 = jnp.sum(jnp.where(hot2, slot, 0.0), axis=-1, keepdims=True)
    cols = (i1 - N_GROUPS, i2 - N_GROUPS, w1, w2, p1, p2)
    sel = jnp.zeros_like(logits)
    for k, c in enumerate(cols):
        sel = jnp.where(lane == k, c, sel)
    meta = sel.T[:SUBLANES]
    mf_ref[0] = meta
    mi_ref[0] = meta.astype(jnp.int32)
    cnt_ref[0] = cnt8


def _token_specs(tile, d, n_prompt_tiles):
    last = n_prompt_tiles - 1
    return (pl.BlockSpec((tile, d), lambda i, *_: (jnp.minimum(i, last), 0)),
            pl.BlockSpec((tile, d), lambda i, *_: (jnp.maximum(i - n_prompt_tiles, 0), 0)))


def _router(yp, ys, wts, layer, tile):
    d = yp.shape[1]
    nlt = d // (2 * LANES)
    n_prompt_tiles = yp.shape[0] // tile
    n_tiles = n_prompt_tiles + ys.shape[0] // tile
    tri = jnp.tril(jnp.ones((tile, tile), BF16), -1)
    meta_spec = pl.BlockSpec((1, SUBLANES, tile), lambda i: (i, 0, 0))
    return pl.pallas_call(
        functools.partial(_router_kernel, n_prompt_tiles=n_prompt_tiles),
        grid=(n_tiles,),
        in_specs=list(_token_specs(tile, d, n_prompt_tiles)) + [
            _layer_spec(wts['norm2'], layer), _layer_spec(wts['w_route'], layer),
            _layer_spec(wts['b_route'], layer), _const_spec((tile, tile))],
        out_specs=(pl.BlockSpec((tile * nlt, LANES), lambda i: (i, 0)), meta_spec, meta_spec,
                   pl.BlockSpec((1, SUBLANES, ROUTE_LANES), lambda i: (i, 0, 0))),
        out_shape=(jax.ShapeDtypeStruct((n_tiles * tile * nlt, LANES), jnp.uint32),
                   jax.ShapeDtypeStruct((n_tiles, SUBLANES, tile), F32),
                   jax.ShapeDtypeStruct((n_tiles, SUBLANES, tile), jnp.int32),
                   jax.ShapeDtypeStruct((n_tiles, SUBLANES, ROUTE_LANES), F32)),
        compiler_params=pltpu.CompilerParams(
            dimension_semantics=("arbitrary",), vmem_limit_bytes=VMEM_LIMIT),
        name="moe_router",
    )(yp, ys, wts['norm2'], wts['w_route'], wts['b_route'], tri)


def _run_copy(tile_idx, e, lo_ref, cnt_ref, dst_ref, local, remote, sem, nlt, to_remote):
    k = tile_idx * N_EXPERTS + e
    n = cnt_ref[k] * nlt
    loc = local.at[pl.ds(pl.multiple_of(lo_ref[k] * nlt, nlt), n)]
    rem = remote.at[pl.ds(pl.multiple_of(dst_ref[k] * nlt, nlt), n)]
    return pltpu.make_async_copy(loc, rem, sem) if to_remote else pltpu.make_async_copy(rem, loc, sem)


def _start_runs(tile_idx, lo_ref, cnt_ref, dst_ref, local, remote, sem, nlt, to_remote):
    def body(e, c):
        @pl.when(cnt_ref[tile_idx * N_EXPERTS + e] > 0)
        def _():
            _run_copy(tile_idx, e, lo_ref, cnt_ref, dst_ref, local, remote, sem, nlt,
                      to_remote).start()
        return c
    lax.fori_loop(0, N_EXPERTS, body, 0)


def _wait_runs(local, sem):
    pltpu.make_async_copy(local, local, sem).wait()


def _dispatch_kernel(lo_ref, cnt_ref, dst_ref, padst_ref, padn_ref,
                     s1_ref, s2_ref, x_ref, xs_ref, loc, zeros, sem, *, nlt, unroll):
    i = pl.program_id(0)
    last = pl.num_programs(0) - 1
    tile = x_ref.shape[0] // nlt
    slot = i % 2
    buf = loc.at[slot]

    @pl.when(i >= 2)
    def _():
        _wait_runs(buf, sem.at[slot])

    def body(tt, c):
        for u in range(unroll):
            t = tt * unroll + u
            row = x_ref[pl.ds(pl.multiple_of(t * nlt, nlt), nlt), :]
            buf[pl.ds(pl.multiple_of(s1_ref[t] * nlt, nlt), nlt), :] = row
            buf[pl.ds(pl.multiple_of(s2_ref[t] * nlt, nlt), nlt), :] = row
        return c

    lax.fori_loop(0, tile // unroll, body, 0)
    _start_runs(i, lo_ref, cnt_ref, dst_ref, buf, xs_ref, sem.at[slot], nlt, True)

    @pl.when(i == last)
    def _():
        zeros[...] = jnp.zeros_like(zeros)

        def pad_copy(e):
            n = padn_ref[e] * nlt
            dst = xs_ref.at[pl.ds(pl.multiple_of(padst_ref[e] * nlt, nlt), n)]
            return pltpu.make_async_copy(zeros.at[pl.ds(0, n)], dst, sem.at[2])

        def each_pad(fn):
            def b(e, c):
                @pl.when(padn_ref[e] > 0)
                def _():
                    fn(e)
                return c
            lax.fori_loop(0, padn_ref.shape[0], b, 0)

        each_pad(lambda e: pad_copy(e).start())
        _wait_runs(buf, sem.at[slot])

        @pl.when(i >= 1)
        def _():
            _wait_runs(loc.at[1 - slot], sem.at[1 - slot])

        each_pad(lambda e: pad_copy(e).wait())


def _smem_tile(tile):
    return pl.BlockSpec((tile,), lambda i, *_: (i,), memory_space=pltpu.SMEM)


def _dispatch(xn_slab, s1, s2, tables, tile, n_rows, nlt):
    n_tiles = xn_slab.shape[0] // (tile * nlt)
    return pl.pallas_call(
        functools.partial(_dispatch_kernel, nlt=nlt, unroll=8),
        grid_spec=pltpu.PrefetchScalarGridSpec(
            num_scalar_prefetch=5,
            grid=(n_tiles,),
            in_specs=[_smem_tile(tile), _smem_tile(tile),
                      pl.BlockSpec((tile * nlt, LANES), lambda i, *_: (i, 0))],
            out_specs=pl.BlockSpec(memory_space=pl.ANY),
            scratch_shapes=[pltpu.VMEM((2, 2 * tile * nlt, LANES), xn_slab.dtype),
                            pltpu.VMEM((MOE_BLOCK * nlt, LANES), xn_slab.dtype),
                            pltpu.SemaphoreType.DMA((3,))]),
        out_shape=jax.ShapeDtypeStruct((n_rows * nlt, LANES), xn_slab.dtype),
        compiler_params=pltpu.CompilerParams(
            dimension_semantics=("arbitrary",), vmem_limit_bytes=VMEM_LIMIT),
        name="moe_dispatch",
    )(*tables, s1, s2, xn_slab)


def _experts_kernel(blk_ref, nused_ref, xs_ref, wg_ref, wu_ref, wd_ref, ys_ref,
                    wg_s, wu_s, wd_s, *, nlt):
    b = pl.program_id(0)
    rows = xs_ref.shape[0] // nlt

    @pl.when(jnp.logical_or(b == 0, blk_ref[b] != blk_ref[jnp.maximum(b - 1, 0)]))
    def _():
        wg_s[...] = wg_ref[...].astype(BF16)
        wu_s[...] = wu_ref[...].astype(BF16)
        wd_s[...] = wd_ref[...].astype(BF16)

    @pl.when(b < nused_ref[0])
    def _():
        x = jnp.concatenate(_unpack_bf16_pairs(_from_slab(xs_ref, rows, nlt)), axis=-1).astype(BF16)
        hg = _dot(x, wg_s[...])
        hu = _dot(x, wu_s[...])
        h = (hg * _sigmoid(hg) * hu).astype(BF16)
        _to_slab(ys_ref, _pack_bf16_pairs(_dot(h, wd_s[...])))

    @pl.when(b >= nused_ref[0])
    def _():
        ys_ref[...] = jnp.zeros_like(ys_ref)


def _experts(xs, blk_expert, n_used, wts, layer, nlt):
    n_blocks = xs.shape[0] // (MOE_BLOCK * nlt)
    d = 2 * nlt * LANES
    f = wts['w_gate'].shape[-1]
    rows = pl.BlockSpec((MOE_BLOCK * nlt, LANES), lambda b, blk, nu: (b, 0))
    w_spec = lambda r, c: pl.BlockSpec((None, None, r, c), lambda b, blk, nu: (layer, blk[b], 0, 0))
    return pl.pallas_call(
        functools.partial(_experts_kernel, nlt=nlt),
        grid_spec=pltpu.PrefetchScalarGridSpec(
            num_scalar_prefetch=2,
            grid=(n_blocks,),
            in_specs=[rows, w_spec(d, f), w_spec(d, f), w_spec(f, d)],
            out_specs=rows,
            scratch_shapes=[pltpu.VMEM((d, f), BF16), pltpu.VMEM((d, f), BF16),
                            pltpu.VMEM((f, d), BF16)]),
        out_shape=jax.ShapeDtypeStruct(xs.shape, xs.dtype),
        compiler_params=pltpu.CompilerParams(
            dimension_semantics=("arbitrary",), vmem_limit_bytes=VMEM_LIMIT),
        name="moe_experts",
    )(blk_expert, n_used, xs, wts['w_gate'], wts['w_up'], wts['w_down'])


def _combine_kernel(lo_ref, cnt_ref, dst_ref, s1_ref, s2_ref, w1_ref, w2_ref, yp_ref, ys_ref,
                    nf_ref, rows_ref, op_ref, os_ref, loc, acc, sem,
                    *, nlt, unroll, final_norm, n_prompt_tiles):
    i = pl.program_id(0)
    tile = yp_ref.shape[0]
    slot = i % 2
    buf = loc.at[slot]
    fetch = lambda t, s: _start_runs(t, lo_ref, cnt_ref, dst_ref, loc.at[s], rows_ref, sem.at[s],
                                     nlt, False)

    @pl.when(i == 0)
    def _():
        fetch(0, 0)

    @pl.when(i + 1 < pl.num_programs(0))
    def _():
        fetch(i + 1, 1 - slot)

    _wait_runs(buf, sem.at[slot])

    def body(tt, c):
        for u in range(unroll):
            t = tt * unroll + u
            lo1, hi1 = _unpack_bf16_pairs(buf[pl.ds(pl.multiple_of(s1_ref[t] * nlt, nlt), nlt), :])
            lo2, hi2 = _unpack_bf16_pairs(buf[pl.ds(pl.multiple_of(s2_ref[t] * nlt, nlt), nlt), :])
            w1, w2 = w1_ref[t], w2_ref[t]
            row = pl.multiple_of(t * 2 * nlt, 2 * nlt)
            acc[pl.ds(row, nlt), :] = w1 * lo1 + w2 * lo2
            acc[pl.ds(row + nlt, nlt), :] = w1 * hi1 + w2 * hi2
        return c

    lax.fori_loop(0, tile // unroll, body, 0)
    is_prompt = i < n_prompt_tiles
    out = jnp.where(is_prompt, yp_ref[...], ys_ref[...]) + _from_slab(acc, tile, 2 * nlt)
    if final_norm:
        out = _rms_norm(out, nf_ref[...])

    @pl.when(is_prompt)
    def _():
        op_ref[...] = out

    @pl.when(jnp.logical_not(is_prompt))
    def _():
        os_ref[...] = out


def _combine(yp, ys, rows, meta, tables, norm_f, final_norm, tile, nlt):
    d = yp.shape[1]
    n_prompt_tiles = yp.shape[0] // tile
    n_tiles = n_prompt_tiles + ys.shape[0] // tile
    tok_p, tok_s = _token_specs(tile, d, n_prompt_tiles)
    return pl.pallas_call(
        functools.partial(_combine_kernel, nlt=nlt, unroll=8, final_norm=final_norm,
                          n_prompt_tiles=n_prompt_tiles),
        grid_spec=pltpu.PrefetchScalarGridSpec(
            num_scalar_prefetch=3,
            grid=(n_tiles,),
            in_specs=[_smem_tile(tile)] * 4 + [tok_p, tok_s, _const_spec((1, d)),
                                               pl.BlockSpec(memory_space=pl.ANY)],
            out_specs=(tok_p, tok_s),
            scratch_shapes=[pltpu.VMEM((2, 2 * tile * nlt, LANES), rows.dtype),
                            pltpu.VMEM((tile * 2 * nlt, LANES), F32),
                            pltpu.SemaphoreType.DMA((2,))]),
        out_shape=(jax.ShapeDtypeStruct(yp.shape, F32), jax.ShapeDtypeStruct(ys.shape, F32)),
        compiler_params=pltpu.CompilerParams(
            dimension_semantics=("arbitrary",), vmem_limit_bytes=VMEM_LIMIT),
        name="moe_combine",
    )(*tables, *meta, yp, ys, norm_f, rows)


def _moe(yp, ys, wts, layer, norm_f, final_norm, tile):
    d = yp.shape[1]
    n = yp.shape[0] + ys.shape[0]
    assert yp.shape[0] % tile == 0 and ys.shape[0] % tile == 0 and d % (2 * LANES) == 0
    nlt = d // (2 * LANES)
    xn_slab, meta_f, meta_i, cnt = _router(yp, ys, wts, layer, tile)
    cnt = cnt[:, 0, N_GROUPS:N_GROUPS + N_EXPERTS].astype(jnp.int32)
    lo = jnp.cumsum(cnt, axis=1) - cnt
    seg = jnp.sum(cnt, axis=0)
    seg_pad = (seg + MOE_BLOCK - 1) // MOE_BLOCK * MOE_BLOCK
    seg_end = jnp.cumsum(seg_pad)
    seg_start = seg_end - seg_pad
    dst = seg_start[None, :] + jnp.cumsum(cnt, axis=0) - cnt
    n_blocks = (2 * n + N_EXPERTS * (MOE_BLOCK - 1)) // MOE_BLOCK
    blk_ids = jnp.arange(n_blocks, dtype=jnp.int32)
    blk_expert = jnp.minimum(
        jnp.sum((seg_end[None, :] <= blk_ids[:, None] * MOE_BLOCK).astype(jnp.int32), axis=1),
        N_EXPERTS - 1)
    n_used = seg_end[-1:] // MOE_BLOCK
    pad_start = jnp.concatenate([seg_start + seg, blk_ids * MOE_BLOCK])
    pad_rows = jnp.concatenate([seg_pad - seg, jnp.where(blk_ids >= n_used[0], MOE_BLOCK, 0)])
    flat = lambda a: a.reshape(-1)
    runs = (flat(lo), flat(cnt), flat(dst))
    s1, s2 = flat(meta_i[:, 4]), flat(meta_i[:, 5])
    w1, w2 = flat(meta_f[:, 2]), flat(meta_f[:, 3])
    xs = _dispatch(xn_slab, s1, s2, runs + (pad_start, pad_rows), tile, n_blocks * MOE_BLOCK, nlt)
    rows = _experts(xs, blk_expert, n_used, wts, layer, nlt)
    return _combine(yp, ys, rows, (s1, s2, w1, w2), runs, norm_f, final_norm, tile, nlt)


def _prep_weights(norm1, w_in, ln_g, ln_b, w_s, b_s, conv_w, conv_b, lru_lambda, w_rg_a, b_rg_a,
                  w_rg_x, b_rg_x, w_out, norm2, w_route_group, b_route_group, w_route_expert,
                  b_route_expert, w_gate, w_up, w_down, n_sample_t):
    depth, d = w_in.shape[0], w_in.shape[1]
    dh = d // H_A
    row = lambda a: a.reshape(depth, 1, -1).astype(F32)
    causal = jnp.tril(jnp.ones((CHUNK, CHUNK), dtype=bool))
    ws = jnp.where(causal, 0.5 * w_s, 0.0)
    bs = jnp.repeat(jnp.swapaxes(0.5 * b_s, 1, 2), dh, axis=2)
    in_scale = jnp.concatenate([jnp.ones((3 * d,), F32), jnp.full((2 * d,), 0.5, F32)])

    def pair_bd(w):
        w = w.reshape(depth, H_B // 2, 2, w.shape[-2], w.shape[-1])
        z = jnp.zeros_like(w[:, :, 0])
        return jnp.concatenate([jnp.concatenate([w[:, :, 0], z], axis=-1),
                                jnp.concatenate([z, w[:, :, 1]], axis=-1)], axis=-2)

    pad = ROUTE_LANES - N_GROUPS - N_EXPERTS
    w_route = jnp.concatenate(
        [w_route_group, w_route_expert, jnp.zeros((depth, d, pad), F32)], axis=-1)
    b_route = jnp.concatenate(
        [b_route_group, b_route_expert, jnp.zeros((depth, pad), F32)], axis=-1)
    nt = n_sample_t
    return dict(
        norm1=row(norm1), w_in=(w_in * in_scale).astype(BF16), ln_g=row(ln_g), ln_b=row(ln_b),
        ws=ws.astype(BF16), bs=bs.astype(F32), conv_w=conv_w.astype(F32), conv_b=row(conv_b),
        lam=row(lru_lambda),
        wbd=(0.5 * jnp.concatenate([pair_bd(w_rg_a), pair_bd(w_rg_x)], axis=-1)).astype(BF16),
        b_a=row(0.5 * b_rg_a), b_x=row(0.5 * b_rg_x), w_out=(0.5 * w_out).astype(BF16),
        norm2=row(norm2), w_route=w_route.astype(BF16), b_route=row(b_route),
        w_gate=w_gate, w_up=w_up, w_down=w_down,
        wsc=jnp.repeat(jnp.transpose(ws[:, :, :nt, :nt], (0, 2, 3, 1)), dh, axis=3).astype(F32),
        bsc=bs[:, :nt].astype(F32),
    )


def kernel(x_prompt, x_sample, state_lru_h, state_conv, norm1, w_in, ln_g, ln_b, w_s, b_s, conv_w, conv_b, lru_lambda, w_rg_a, b_rg_a, w_rg_x, b_rg_x, w_out, norm2, w_route_group, b_route_group, w_route_expert, b_route_expert, w_gate, w_up, w_down, norm_f):
    depth = w_in.shape[0]
    nb, seq, d = x_prompt.shape
    ns, nt, _ = x_sample.shape
    assert nt <= CHUNK
    nf = norm_f.reshape(1, d).astype(F32)
    wts = _prep_weights(norm1, w_in, ln_g, ln_b, w_s, b_s, conv_w, conv_b, lru_lambda, w_rg_a,
                        b_rg_a, w_rg_x, b_rg_x, w_out, norm2, w_route_group, b_route_group,
                        w_route_expert, b_route_expert, w_gate, w_up, w_down, nt)
    conv_state = jnp.transpose(state_conv, (0, 2, 1, 3))
    tile = min(MOE_TILE, nt * ns)
    yp = x_prompt
    ys = jnp.transpose(x_sample, (1, 0, 2))
    hp, cp, vp, hs, cs, vs = [], [], [], [], [], []
    for l in range(depth):
        yp, h, c, v = _mixer_prompt(yp, wts, l)
        hp.append(h)
        cp.append(c)
        vp.append(v)
        ys, h, c, v = _mixer_sample(ys, state_lru_h, conv_state, wts, l)
        hs.append(h)
        cs.append(c)
        vs.append(v)
        yp, ys = _moe(yp.reshape(nb * seq, d), ys.reshape(nt * ns, d), wts, l, nf,
                      l == depth - 1, tile)
        yp = yp.reshape(nb, seq, d)
        ys = ys.reshape(nt, ns, d)
    to_seq_major = lambda a: jnp.transpose(jnp.stack(a), (0, 2, 1, 3))
    return (yp, jnp.transpose(ys, (1, 0, 2)), jnp.stack(hp), jnp.stack(cp), jnp.stack(vp),
            jnp.stack(hs), to_seq_major(cs), to_seq_major(vs))
```

```python
import functools

import jax
import jax.numpy as jnp
from jax import lax
from jax.experimental import pallas as pl
from jax.experimental.pallas import tpu as pltpu

F32 = jnp.float32
BF16 = jnp.bfloat16

CHUNK = 128
H_A = 4
H_B = 8
CONV_W = 4
LRU_C = 8.0
N_GROUPS = 4
E_PER_GROUP = 8
N_EXPERTS = N_GROUPS * E_PER_GROUP
EPS = 1e-6

LANES = 128
SUBLANES = 8
SCAN_PITCH = CHUNK + SUBLANES
ROUTE_LANES = 128
MOE_TILE = 512
MOE_BLOCK = 256
VMEM_LIMIT = 60 * 1024 * 1024


def _dot(a, b):
    return jnp.dot(a, b, preferred_element_type=F32)


def _sigmoid(x):
    return 0.5 * (jnp.tanh(0.5 * x) + 1.0)


def _gelu2(x):
    c = 0.7978845608028654
    return x * (1.0 + jnp.tanh(x * (c + (c * 0.044715) * (x * x))))


def _rms_norm(x, g):
    return x * lax.rsqrt(jnp.mean(x * x, axis=-1, keepdims=True) + EPS) * g


def _layer_norm(x, g, b, eps):
    xc = x - jnp.mean(x, axis=-1, keepdims=True)
    var = jnp.mean(xc * xc, axis=-1, keepdims=True)
    return xc * lax.rsqrt(var + eps) * g + b


def _softplus(x):
    return jnp.maximum(x, 0.0) + jnp.log(1.0 + jnp.exp(-jnp.abs(x)))


def _lru_gates(xc, wbd_ref, b_a, b_x, lam_half):
    d = xc.shape[-1]
    xcb = xc.astype(BF16)
    r_parts, i_parts = [], []
    for q in range(d // 256):
        ri = _dot(xcb[:, q * 256:(q + 1) * 256], wbd_ref[q])
        r_parts.append(ri[:, :256])
        i_parts.append(ri[:, 256:])
    r2 = 1.0 + jnp.tanh(jnp.concatenate(r_parts, axis=-1) + b_a)
    i2 = 1.0 + jnp.tanh(jnp.concatenate(i_parts, axis=-1) + b_x)
    a = jnp.exp(lam_half * r2)
    mult_half = jnp.sqrt(0.25 - 0.25 * (a * a))
    return a, mult_half, i2 * xc


def _shift_rows(x, hist, j):
    sh = pltpu.roll(x, j, axis=0)
    top = jnp.where(lax.broadcasted_iota(jnp.int32, (SUBLANES, 1), 0) < j,
                    pltpu.roll(hist, j, axis=0), sh[:SUBLANES])
    return jnp.concatenate([top, sh[SUBLANES:]], axis=0)


def _mixer_prompt_kernel(x_ref, norm_ref, win_ref, lng_ref, lnb_ref, ws_ref, bs_ref,
                         cw_ref, cb_ref, lam_ref, wbd_ref, ba_ref, bx_ref, wout_ref,
                         y_ref, h_ref, conv_ref, v_ref,
                         xn_s, hist_s, a_s, b_s, ya_s, gb_s, hc_s):
    i = pl.program_id(0)
    last = pl.num_programs(0) - 1
    nseq, _, d = x_ref.shape
    nlt = d // LANES
    pair = 2 * CHUNK

    @pl.when(i == 0)
    def _():
        hist_s[...] = jnp.zeros_like(hist_s)
        hc_s[...] = jnp.zeros_like(hc_s)

    g1 = norm_ref[...]

    def norm_body(s, c):
        row = pl.multiple_of(s * CHUNK, CHUNK)
        xn_s[pl.ds(row, CHUNK), :] = _rms_norm(x_ref[s], g1).astype(BF16)
        return c

    lax.fori_loop(0, nseq, norm_body, 0)

    lam_half = (-0.5 * LRU_C) * _softplus(-lam_ref[...])
    is_first = i == 0

    def branch_body(p, c):
        row = pl.multiple_of(p * pair, pair)
        xnb = xn_s[pl.ds(row, pair), :]
        v = _layer_norm(_gelu2(_dot(xnb, win_ref[:, d:2 * d])), lng_ref[...], lnb_ref[...],
                        4.0 * EPS)

        @pl.when(i == last)
        def _():
            v_ref[2 * p] = v[:CHUNK]
            v_ref[2 * p + 1] = v[CHUNK:]

        vb = v.astype(BF16)
        dh = d // H_A
        z_rows = []
        for k in range(2):
            zs = [_dot(ws_ref[h], vb[k * CHUNK:(k + 1) * CHUNK, h * dh:(h + 1) * dh])
                  for h in range(H_A)]
            z_rows.append(jnp.concatenate(zs, axis=-1) + bs_ref[...])
        z = jnp.concatenate(z_rows, axis=0)
        ya = _gelu2(_dot(xnb, win_ref[:, 0:d])) * z
        ga2 = 1.0 + jnp.tanh(_dot(xnb, win_ref[:, 3 * d:4 * d]))
        ya_s[pl.ds(row, pair), :] = (ga2 * ya).astype(BF16)
        gb_s[pl.ds(row, pair), :] = (1.0 + jnp.tanh(_dot(xnb, win_ref[:, 4 * d:5 * d]))).astype(BF16)
        xr = _dot(xnb, win_ref[:, 2 * d:3 * d])
        xcs = []
        for k in range(2):
            s = 2 * p + k
            xk = xr[k * CHUNK:(k + 1) * CHUNK]
            hist = hist_s[s]
            xc = cb_ref[...] + xk * cw_ref[CONV_W - 1:CONV_W, :]
            for j in range(1, CONV_W):
                xc = xc + _shift_rows(xk, hist, j) * cw_ref[CONV_W - 1 - j:CONV_W - j, :]
            hist_s[s] = xk[CHUNK - SUBLANES:]
            xcs.append(xc)
        xc = jnp.concatenate(xcs, axis=0)
        a, mult_half, ix2 = _lru_gates(xc, wbd_ref, ba_ref[...], bx_ref[...], lam_half)
        bt = mult_half * ix2
        for k in range(2):
            srow = pl.multiple_of((2 * p + k) * SCAN_PITCH, SUBLANES)
            for j in range(nlt):
                a_s[j, pl.ds(srow, CHUNK), :] = a[k * CHUNK:(k + 1) * CHUNK, j * LANES:(j + 1) * LANES]
                b_s[j, pl.ds(srow, CHUNK), :] = bt[k * CHUNK:(k + 1) * CHUNK, j * LANES:(j + 1) * LANES]

        @pl.when(is_first)
        def _():
            first = lax.broadcasted_iota(jnp.int32, (SUBLANES, 1), 0) == 0
            for k in range(2):
                srow = pl.multiple_of((2 * p + k) * SCAN_PITCH, SUBLANES)
                top = slice(k * CHUNK, k * CHUNK + SUBLANES)
                fixed = jnp.where(first, 0.5 * ix2[top], bt[top])
                for j in range(nlt):
                    b_s[j, pl.ds(srow, SUBLANES), :] = fixed[:, j * LANES:(j + 1) * LANES]

        return c

    lax.fori_loop(0, nseq // 2, branch_body, 0)

    def scan_body(t, hs):
        out = []
        for j in range(nlt):
            a_t = a_s[j, pl.ds(t, nseq, stride=SCAN_PITCH), :]
            b_t = b_s[j, pl.ds(t, nseq, stride=SCAN_PITCH), :]
            h = a_t * hs[j] + b_t
            b_s[j, pl.ds(t, nseq, stride=SCAN_PITCH), :] = h
            out.append(h)
        return tuple(out)

    h0 = tuple(hc_s[:, j * LANES:(j + 1) * LANES] for j in range(nlt))
    hs = lax.fori_loop(0, CHUNK, scan_body, h0)
    hfin = jnp.concatenate(hs, axis=-1)
    hc_s[...] = hfin
    h_ref[...] = hfin

    def out_body(p, c):
        row = pl.multiple_of(p * pair, pair)
        hrows = []
        for k in range(2):
            srow = pl.multiple_of((2 * p + k) * SCAN_PITCH, SUBLANES)
            hrows.append(jnp.concatenate([b_s[j, pl.ds(srow, CHUNK), :] for j in range(nlt)], axis=-1))
        h = jnp.concatenate(hrows, axis=0)
        merged = (ya_s[pl.ds(row, pair), :].astype(F32)
                  + gb_s[pl.ds(row, pair), :].astype(F32) * h).astype(BF16)
        o = _dot(merged, wout_ref[...])
        y_ref[2 * p] = x_ref[2 * p] + o[:CHUNK]
        y_ref[2 * p + 1] = x_ref[2 * p + 1] + o[CHUNK:]
        return c

    lax.fori_loop(0, nseq // 2, out_body, 0)

    @pl.when(i == last)
    def _():
        conv_ref[...] = hist_s[:, SUBLANES - (CONV_W - 1):, :]


def _const_spec(shape):
    nd = len(shape)
    return pl.BlockSpec(shape, lambda i, *_, _n=nd: (0,) * _n, pipeline_mode=pl.Buffered(1))


def _layer_spec(stacked, layer):
    shape = stacked.shape[1:]
    nd = len(shape)
    return pl.BlockSpec((None,) + tuple(shape), lambda i, *_, _n=nd: (layer,) + (0,) * _n,
                        pipeline_mode=pl.Buffered(1))


MIXER_WEIGHTS = ('norm1', 'w_in', 'ln_g', 'ln_b', 'ws', 'bs', 'conv_w', 'conv_b', 'lam', 'wbd',
                 'b_a', 'b_x', 'w_out')


def _mixer_prompt(x, wts, layer):
    nseq, seq, d = x.shape
    assert seq % CHUNK == 0 and nseq == SUBLANES and d % 256 == 0
    n_chunks = seq // CHUNK
    nlt = d // LANES
    weights = tuple(wts[k] for k in MIXER_WEIGHTS)
    x_spec = pl.BlockSpec((nseq, CHUNK, d), lambda i: (0, i, 0))
    out_shape = (
        jax.ShapeDtypeStruct((nseq, seq, d), F32),
        jax.ShapeDtypeStruct((nseq, d), F32),
        jax.ShapeDtypeStruct((nseq, CONV_W - 1, d), F32),
        jax.ShapeDtypeStruct((nseq, CHUNK, d), F32),
    )
    out_specs = (
        x_spec,
        _const_spec((nseq, d)),
        _const_spec((nseq, CONV_W - 1, d)),
        _const_spec((nseq, CHUNK, d)),
    )
    scratch = [
        pltpu.VMEM((nseq * CHUNK, d), BF16),
        pltpu.VMEM((nseq, SUBLANES, d), F32),
        pltpu.VMEM((nlt, nseq * SCAN_PITCH, LANES), F32),
        pltpu.VMEM((nlt, nseq * SCAN_PITCH, LANES), F32),
        pltpu.VMEM((nseq * CHUNK, d), BF16),
        pltpu.VMEM((nseq * CHUNK, d), BF16),
        pltpu.VMEM((nseq, d), F32),
    ]
    return pl.pallas_call(
        _mixer_prompt_kernel,
        grid=(n_chunks,),
        in_specs=[x_spec] + [_layer_spec(w, layer) for w in weights],
        out_specs=out_specs,
        out_shape=out_shape,
        scratch_shapes=scratch,
        compiler_params=pltpu.CompilerParams(
            dimension_semantics=("arbitrary",), vmem_limit_bytes=VMEM_LIMIT),
        name="mixer_prompt",
    )(x, *weights)


def _mixer_sample_kernel(x_ref, h0_ref, cst_ref, norm_ref, win_ref, lng_ref, lnb_ref, wsc_ref,
                         bsc_ref, cw_ref, cb_ref, lam_ref, wbd_ref, ba_ref, bx_ref, wout_ref,
                         y_ref, h_ref, conv_ref, v_ref):
    nt, ns, d = x_ref.shape
    lam_half = (-0.5 * LRU_C) * _softplus(-lam_ref[...])
    g1 = norm_ref[...]
    h = h0_ref[...]
    xp = [cst_ref[k] for k in range(CONV_W - 1)]
    for t in range(nt):
        x = x_ref[t]
        xnb = _rms_norm(x, g1).astype(BF16)
        v = _layer_norm(_gelu2(_dot(xnb, win_ref[:, d:2 * d])), lng_ref[...], lnb_ref[...],
                        4.0 * EPS)
        v_ref[t] = v
        z = bsc_ref[t:t + 1, :]
        for s in range(t + 1):
            z = z + wsc_ref[t, s:s + 1, :] * v_ref[s]
        ya = _gelu2(_dot(xnb, win_ref[:, 0:d])) * z
        ga2 = 1.0 + jnp.tanh(_dot(xnb, win_ref[:, 3 * d:4 * d]))
        xr = _dot(xnb, win_ref[:, 2 * d:3 * d])
        xp.append(xr)
        xc = cb_ref[...]
        for k in range(CONV_W):
            xc = xc + xp[t + k] * cw_ref[k:k + 1, :]
        a, mult_half, ix2 = _lru_gates(xc, wbd_ref, ba_ref[...], bx_ref[...], lam_half)
        h = a * h + mult_half * ix2
        gb2 = 1.0 + jnp.tanh(_dot(xnb, win_ref[:, 4 * d:5 * d]))
        merged = (ga2 * ya + gb2 * h).astype(BF16)
        y_ref[t] = x + _dot(merged, wout_ref[...])
    h_ref[...] = h
    for k in range(CONV_W - 1):
        conv_ref[k] = xp[nt + k]


def _mixer_sample(x, h0, conv_state, wts, layer):
    nt, ns, d = x.shape
    names = MIXER_WEIGHTS[:4] + ('wsc', 'bsc') + MIXER_WEIGHTS[6:]
    stacked = (h0, conv_state) + tuple(wts[k] for k in names)
    out_shape = (
        jax.ShapeDtypeStruct((nt, ns, d), F32),
        jax.ShapeDtypeStruct((ns, d), F32),
        jax.ShapeDtypeStruct((CONV_W - 1, ns, d), F32),
        jax.ShapeDtypeStruct((nt, ns, d), F32),
    )
    return pl.pallas_call(
        _mixer_sample_kernel,
        grid=(1,),
        in_specs=[_const_spec(x.shape)] + [_layer_spec(w, layer) for w in stacked],
        out_specs=tuple(_const_spec(s.shape) for s in out_shape),
        out_shape=out_shape,
        compiler_params=pltpu.CompilerParams(
            dimension_semantics=("arbitrary",), vmem_limit_bytes=VMEM_LIMIT),
        name="mixer_sample",
    )(x, *stacked)


def _route(logits):
    lane = lax.broadcasted_iota(jnp.int32, logits.shape, 1).astype(F32)
    neg = jnp.float32(-jnp.inf)
    big = jnp.float32(ROUTE_LANES)

    def argmax_first(vals):
        m = jnp.max(vals, axis=-1, keepdims=True)
        idx = jnp.min(jnp.where(vals == m, lane, big), axis=-1, keepdims=True)
        return m, idx

    gl = jnp.where(lane < N_GROUPS, logits, neg)
    gm, gi = argmax_first(gl)
    pg_top = 1.0 / jnp.sum(jnp.exp(gl - gm), axis=-1, keepdims=True)
    lo = N_GROUPS + gi * E_PER_GROUP
    el = jnp.where(jnp.logical_and(lane >= lo, lane < lo + E_PER_GROUP), logits, neg)
    m1, i1 = argmax_first(el)
    m2, i2 = argmax_first(jnp.where(lane == i1, neg, el))
    e2 = jnp.exp(m2 - m1)
    w1 = pg_top / (1.0 + e2)
    w2 = pg_top * e2 / (1.0 + e2)
    return i1, i2, w1, w2


def _lane_cumsum_exclusive(x):
    lane = lax.broadcasted_iota(jnp.int32, x.shape, 1)
    acc = x
    shift = 1
    while shift < LANES:
        acc = acc + jnp.where(lane >= shift, pltpu.roll(acc, shift, axis=1), 0.0)
        shift *= 2
    return acc - x


def _to_slab(ref, val):
    t, d = val.shape
    nlt = d // LANES
    for j in range(nlt):
        ref[pl.ds(j, t, stride=nlt), :] = val[:, j * LANES:(j + 1) * LANES]


def _from_slab(ref, t, nlt):
    return jnp.concatenate([ref[pl.ds(j, t, stride=nlt), :] for j in range(nlt)], axis=-1)


def _tokens_from_slab(slab_ref, nlt):
    return slab_ref[...].reshape(slab_ref.shape[0] // nlt, nlt, LANES).astype(BF16)


def _tokens_to_slab(slab_ref, tokens):
    slab_ref[...] = tokens.astype(F32).reshape(slab_ref.shape)


def _router_kernel(yp_ref, ys_ref, norm_ref, wr_ref, br_ref, tri_ref, xn_ref, mf_ref, mi_ref,
                   cnt_ref, slab_s, *, n_prompt_tiles):
    y = jnp.where(pl.program_id(0) < n_prompt_tiles, yp_ref[...], ys_ref[...])
    xn = _rms_norm(y, norm_ref[...])
    _to_slab(slab_s, xn)
    xn_ref[...] = _tokens_from_slab(slab_s, xn_ref.shape[1])
    logits = _dot(xn.astype(BF16), wr_ref[...]) + br_ref[...]
    i1, i2, w1, w2 = _route(logits)
    lane = lax.broadcasted_iota(jnp.int32, logits.shape, 1).astype(F32)
    hot1 = lane == i1
    hot2 = lane == i2
    onehot = jnp.where(jnp.logical_or(hot1, hot2), 1.0, 0.0)
    before = _dot(tri_ref[...], onehot.astype(BF16))
    cnt = jnp.sum(onehot, axis=0, keepdims=True)
    cnt8 = jnp.broadcast_to(cnt, (SUBLANES, ROUTE_LANES))
    seg_lo = _lane_cumsum_exclusive(cnt8)[0:1, :]
    slot = before + seg_lo
    p1 = jnp.sum(jnp.where(hot1, slot, 0.0), axis=-1, keepdims=True)
    p2 = jnp.sum(jnp.where(hot2, slot, 0.0), axis=-1, keepdims=True)
    cols = (i1 - N_GROUPS, i2 - N_GROUPS, w1, w2, p1, p2)
    sel = jnp.zeros_like(logits)
    for k, c in enumerate(cols):
        sel = jnp.where(lane == k, c, sel)
    meta = sel.T[:SUBLANES]
    mf_ref[0] = meta
    mi_ref[0] = meta.astype(jnp.int32)
    cnt_ref[0] = cnt8


def _token_specs(tile, d, n_prompt_tiles):
    last = n_prompt_tiles - 1
    return (pl.BlockSpec((tile, d), lambda i, *_: (jnp.minimum(i, last), 0)),
            pl.BlockSpec((tile, d), lambda i, *_: (jnp.maximum(i - n_prompt_tiles, 0), 0)))


def _router(yp, ys, wts, layer, tile):
    d = yp.shape[1]
    nlt = d // LANES
    n_prompt_tiles = yp.shape[0] // tile
    n_tiles = n_prompt_tiles + ys.shape[0] // tile
    tri = jnp.tril(jnp.ones((tile, tile), BF16), -1)
    meta_spec = pl.BlockSpec((1, SUBLANES, tile), lambda i: (i, 0, 0))
    return pl.pallas_call(
        functools.partial(_router_kernel, n_prompt_tiles=n_prompt_tiles),
        grid=(n_tiles,),
        in_specs=list(_token_specs(tile, d, n_prompt_tiles)) + [
            _layer_spec(wts['norm2'], layer), _layer_spec(wts['w_route'], layer),
            _layer_spec(wts['b_route'], layer), _const_spec((tile, tile))],
        out_specs=(pl.BlockSpec((tile, nlt, LANES), lambda i: (i, 0, 0)), meta_spec, meta_spec,
                   pl.BlockSpec((1, SUBLANES, ROUTE_LANES), lambda i: (i, 0, 0))),
        out_shape=(jax.ShapeDtypeStruct((n_tiles * tile, nlt, LANES), BF16),
                   jax.ShapeDtypeStruct((n_tiles, SUBLANES, tile), F32),
                   jax.ShapeDtypeStruct((n_tiles, SUBLANES, tile), jnp.int32),
                   jax.ShapeDtypeStruct((n_tiles, SUBLANES, ROUTE_LANES), F32)),
        scratch_shapes=[pltpu.VMEM((tile * nlt, LANES), F32)],
        compiler_params=pltpu.CompilerParams(
            dimension_semantics=("arbitrary",), vmem_limit_bytes=VMEM_LIMIT),
        name="moe_router",
    )(yp, ys, wts['norm2'], wts['w_route'], wts['b_route'], tri)


def _run_copy(tile_idx, e, lo_ref, cnt_ref, dst_ref, local, remote, sem, to_remote):
    k = tile_idx * N_EXPERTS + e
    loc = local.at[pl.ds(lo_ref[k], cnt_ref[k])]
    rem = remote.at[pl.ds(dst_ref[k], cnt_ref[k])]
    return pltpu.make_async_copy(loc, rem, sem) if to_remote else pltpu.make_async_copy(rem, loc, sem)


def _start_runs(tile_idx, lo_ref, cnt_ref, dst_ref, local, remote, sem, to_remote):
    def body(e, c):
        @pl.when(cnt_ref[tile_idx * N_EXPERTS + e] > 0)
        def _():
            _run_copy(tile_idx, e, lo_ref, cnt_ref, dst_ref, local, remote, sem, to_remote).start()
        return c
    lax.fori_loop(0, N_EXPERTS, body, 0)


def _wait_runs(local, sem):
    pltpu.make_async_copy(local, local, sem).wait()


def _dispatch_kernel(lo_ref, cnt_ref, dst_ref, padst_ref, padn_ref,
                     s1_ref, s2_ref, x_ref, xs_ref, loc, zeros, sem, *, unroll):
    i = pl.program_id(0)
    last = pl.num_programs(0) - 1
    tile = x_ref.shape[0]
    slot = i % 2
    buf = loc.at[slot]

    @pl.when(i >= 2)
    def _():
        _wait_runs(buf, sem.at[slot])

    def body(tt, c):
        for u in range(unroll):
            t = tt * unroll + u
            row = x_ref[t]
            buf[s1_ref[t]] = row
            buf[s2_ref[t]] = row
        return c

    lax.fori_loop(0, tile // unroll, body, 0)
    _start_runs(i, lo_ref, cnt_ref, dst_ref, buf, xs_ref, sem.at[slot], True)

    @pl.when(i == last)
    def _():
        zeros[...] = jnp.zeros_like(zeros)

        def pad_copy(e):
            n = padn_ref[e]
            dst = xs_ref.at[pl.ds(padst_ref[e], n)]
            return pltpu.make_async_copy(zeros.at[pl.ds(0, n)], dst, sem.at[2])

        def each_pad(fn):
            def b(e, c):
                @pl.when(padn_ref[e] > 0)
                def _():
                    fn(e)
                return c
            lax.fori_loop(0, padn_ref.shape[0], b, 0)

        each_pad(lambda e: pad_copy(e).start())
        _wait_runs(buf, sem.at[slot])

        @pl.when(i >= 1)
        def _():
            _wait_runs(loc.at[1 - slot], sem.at[1 - slot])

        each_pad(lambda e: pad_copy(e).wait())


def _smem_tile(tile):
    return pl.BlockSpec((tile,), lambda i, *_: (i,), memory_space=pltpu.SMEM)


def _dispatch(xn_slab, s1, s2, tables, tile, n_rows):
    token = xn_slab.shape[1:]
    return pl.pallas_call(
        functools.partial(_dispatch_kernel, unroll=8),
        grid_spec=pltpu.PrefetchScalarGridSpec(
            num_scalar_prefetch=5,
            grid=(xn_slab.shape[0] // tile,),
            in_specs=[_smem_tile(tile), _smem_tile(tile),
                      pl.BlockSpec((tile,) + token, lambda i, *_: (i, 0, 0))],
            out_specs=pl.BlockSpec(memory_space=pl.ANY),
            scratch_shapes=[pltpu.VMEM((2, 2 * tile) + token, xn_slab.dtype),
                            pltpu.VMEM((MOE_BLOCK,) + token, xn_slab.dtype),
                            pltpu.SemaphoreType.DMA((3,))]),
        out_shape=jax.ShapeDtypeStruct((n_rows,) + token, xn_slab.dtype),
        compiler_params=pltpu.CompilerParams(
            dimension_semantics=("arbitrary",), vmem_limit_bytes=VMEM_LIMIT),
        name="moe_dispatch",
    )(*tables, s1, s2, xn_slab)


def _experts_kernel(blk_ref, nused_ref, xs_ref, wg_ref, wu_ref, wd_ref, ys_ref,
                    wg_s, wu_s, wd_s, slab_s):
    b = pl.program_id(0)
    rows, nlt, _ = xs_ref.shape

    @pl.when(jnp.logical_or(b == 0, blk_ref[b] != blk_ref[jnp.maximum(b - 1, 0)]))
    def _():
        wg_s[...] = wg_ref[...].astype(BF16)
        wu_s[...] = wu_ref[...].astype(BF16)
        wd_s[...] = wd_ref[...].astype(BF16)

    @pl.when(b < nused_ref[0])
    def _():
        _tokens_to_slab(slab_s, xs_ref[...])
        x = _from_slab(slab_s, rows, nlt).astype(BF16)
        hg = _dot(x, wg_s[...])
        hu = _dot(x, wu_s[...])
        h = (hg * _sigmoid(hg) * hu).astype(BF16)
        _to_slab(slab_s, _dot(h, wd_s[...]))
        ys_ref[...] = _tokens_from_slab(slab_s, nlt)

    @pl.when(b >= nused_ref[0])
    def _():
        ys_ref[...] = jnp.zeros_like(ys_ref)


def _experts(xs, blk_expert, n_used, wts, layer):
    n_rows, nlt, _ = xs.shape
    d = nlt * LANES
    f = wts['w_gate'].shape[-1]
    rows = pl.BlockSpec((MOE_BLOCK, nlt, LANES), lambda b, blk, nu: (b, 0, 0))
    w_spec = lambda r, c: pl.BlockSpec((None, None, r, c), lambda b, blk, nu: (layer, blk[b], 0, 0))
    return pl.pallas_call(
        _experts_kernel,
        grid_spec=pltpu.PrefetchScalarGridSpec(
            num_scalar_prefetch=2,
            grid=(n_rows // MOE_BLOCK,),
            in_specs=[rows, w_spec(d, f), w_spec(d, f), w_spec(f, d)],
            out_specs=rows,
            scratch_shapes=[pltpu.VMEM((d, f), BF16), pltpu.VMEM((d, f), BF16),
                            pltpu.VMEM((f, d), BF16),
                            pltpu.VMEM((MOE_BLOCK * nlt, LANES), F32)]),
        out_shape=jax.ShapeDtypeStruct(xs.shape, xs.dtype),
        compiler_params=pltpu.CompilerParams(
            dimension_semantics=("arbitrary",), vmem_limit_bytes=VMEM_LIMIT),
        name="moe_experts",
    )(blk_expert, n_used, xs, wts['w_gate'], wts['w_up'], wts['w_down'])


def _combine_kernel(lo_ref, cnt_ref, dst_ref, s1_ref, s2_ref, w1_ref, w2_ref, yp_ref, ys_ref,
                    nf_ref, rows_ref, op_ref, os_ref, loc, acc, sem,
                    *, unroll, final_norm, n_prompt_tiles):
    i = pl.program_id(0)
    tile = yp_ref.shape[0]
    nlt = loc.shape[2]
    slot = i % 2
    buf = loc.at[slot]
    fetch = lambda t, s: _start_runs(t, lo_ref, cnt_ref, dst_ref, loc.at[s], rows_ref, sem.at[s],
                                     False)

    @pl.when(i == 0)
    def _():
        fetch(0, 0)

    @pl.when(i + 1 < pl.num_programs(0))
    def _():
        fetch(i + 1, 1 - slot)

    _wait_runs(buf, sem.at[slot])

    def body(tt, c):
        for u in range(unroll):
            t = tt * unroll + u
            r1 = buf[s1_ref[t]].astype(F32)
            r2 = buf[s2_ref[t]].astype(F32)
            acc[pl.ds(pl.multiple_of(t * nlt, nlt), nlt), :] = w1_ref[t] * r1 + w2_ref[t] * r2
        return c

    lax.fori_loop(0, tile // unroll, body, 0)
    is_prompt = i < n_prompt_tiles
    out = jnp.where(is_prompt, yp_ref[...], ys_ref[...]) + _from_slab(acc, tile, nlt)
    if final_norm:
        out = _rms_norm(out, nf_ref[...])

    @pl.when(is_prompt)
    def _():
        op_ref[...] = out

    @pl.when(jnp.logical_not(is_prompt))
    def _():
        os_ref[...] = out


def _combine(yp, ys, rows, meta, tables, norm_f, final_norm, tile):
    d = yp.shape[1]
    token = rows.shape[1:]
    n_prompt_tiles = yp.shape[0] // tile
    n_tiles = n_prompt_tiles + ys.shape[0] // tile
    tok_p, tok_s = _token_specs(tile, d, n_prompt_tiles)
    return pl.pallas_call(
        functools.partial(_combine_kernel, unroll=8, final_norm=final_norm,
                          n_prompt_tiles=n_prompt_tiles),
        grid_spec=pltpu.PrefetchScalarGridSpec(
            num_scalar_prefetch=3,
            grid=(n_tiles,),
            in_specs=[_smem_tile(tile)] * 4 + [tok_p, tok_s, _const_spec((1, d)),
                                               pl.BlockSpec(memory_space=pl.ANY)],
            out_specs=(tok_p, tok_s),
            scratch_shapes=[pltpu.VMEM((2, 2 * tile) + token, rows.dtype),
                            pltpu.VMEM((tile * token[0], LANES), F32),
                            pltpu.SemaphoreType.DMA((2,))]),
        out_shape=(jax.ShapeDtypeStruct(yp.shape, F32), jax.ShapeDtypeStruct(ys.shape, F32)),
        compiler_params=pltpu.CompilerParams(
            dimension_semantics=("arbitrary",), vmem_limit_bytes=VMEM_LIMIT),
        name="moe_combine",
    )(*tables, *meta, yp, ys, norm_f, rows)


def _moe(yp, ys, wts, layer, norm_f, final_norm, tile):
    d = yp.shape[1]
    n = yp.shape[0] + ys.shape[0]
    assert yp.shape[0] % tile == 0 and ys.shape[0] % tile == 0 and d % LANES == 0
    xn_slab, meta_f, meta_i, cnt = _router(yp, ys, wts, layer, tile)
    cnt = cnt[:, 0, N_GROUPS:N_GROUPS + N_EXPERTS].astype(jnp.int32)
    lo = jnp.cumsum(cnt, axis=1) - cnt
    seg = jnp.sum(cnt, axis=0)
    seg_pad = (seg + MOE_BLOCK - 1) // MOE_BLOCK * MOE_BLOCK
    seg_end = jnp.cumsum(seg_pad)
    seg_start = seg_end - seg_pad
    dst = seg_start[None, :] + jnp.cumsum(cnt, axis=0) - cnt
    n_blocks = (2 * n + N_EXPERTS * (MOE_BLOCK - 1)) // MOE_BLOCK
    blk_ids = jnp.arange(n_blocks, dtype=jnp.int32)
    blk_expert = jnp.minimum(
        jnp.sum((seg_end[None, :] <= blk_ids[:, None] * MOE_BLOCK).astype(jnp.int32), axis=1),
        N_EXPERTS - 1)
    n_used = seg_end[-1:] // MOE_BLOCK
    pad_start = jnp.concatenate([seg_start + seg, blk_ids * MOE_BLOCK])
    pad_rows = jnp.concatenate([seg_pad - seg, jnp.where(blk_ids >= n_used[0], MOE_BLOCK, 0)])
    flat = lambda a: a.reshape(-1)
    runs = (flat(lo), flat(cnt), flat(dst))
    s1, s2 = flat(meta_i[:, 4]), flat(meta_i[:, 5])
    w1, w2 = flat(meta_f[:, 2]), flat(meta_f[:, 3])
    xs = _dispatch(xn_slab, s1, s2, runs + (pad_start, pad_rows), tile, n_blocks * MOE_BLOCK)
    rows = _experts(xs, blk_expert, n_used, wts, layer)
    return _combine(yp, ys, rows, (s1, s2, w1, w2), runs, norm_f, final_norm, tile)


def _prep_weights(norm1, w_in, ln_g, ln_b, w_s, b_s, conv_w, conv_b, lru_lambda, w_rg_a, b_rg_a,
                  w_rg_x, b_rg_x, w_out, norm2, w_route_group, b_route_group, w_route_expert,
                  b_route_expert, w_gate, w_up, w_down, n_sample_t):
    depth, d = w_in.shape[0], w_in.shape[1]
    dh = d // H_A
    row = lambda a: a.reshape(depth, 1, -1).astype(F32)
    causal = jnp.tril(jnp.ones((CHUNK, CHUNK), dtype=bool))
    ws = jnp.where(causal, 0.5 * w_s, 0.0)
    bs = jnp.repeat(jnp.swapaxes(0.5 * b_s, 1, 2), dh, axis=2)
    in_scale = jnp.concatenate([jnp.ones((3 * d,), F32), jnp.full((2 * d,), 0.5, F32)])

    def pair_bd(w):
        w = w.reshape(depth, H_B // 2, 2, w.shape[-2], w.shape[-1])
        z = jnp.zeros_like(w[:, :, 0])
        return jnp.concatenate([jnp.concatenate([w[:, :, 0], z], axis=-1),
                                jnp.concatenate([z, w[:, :, 1]], axis=-1)], axis=-2)

    pad = ROUTE_LANES - N_GROUPS - N_EXPERTS
    w_route = jnp.concatenate(
        [w_route_group, w_route_expert, jnp.zeros((depth, d, pad), F32)], axis=-1)
    b_route = jnp.concatenate(
        [b_route_group, b_route_expert, jnp.zeros((depth, pad), F32)], axis=-1)
    nt = n_sample_t
    return dict(
        norm1=row(norm1), w_in=(w_in * in_scale).astype(BF16), ln_g=row(ln_g), ln_b=row(ln_b),
        ws=ws.astype(BF16), bs=bs.astype(F32), conv_w=conv_w.astype(F32), conv_b=row(conv_b),
        lam=row(lru_lambda),
        wbd=(0.5 * jnp.concatenate([pair_bd(w_rg_a), pair_bd(w_rg_x)], axis=-1)).astype(BF16),
        b_a=row(0.5 * b_rg_a), b_x=row(0.5 * b_rg_x), w_out=(0.5 * w_out).astype(BF16),
        norm2=row(norm2), w_route=w_route.astype(BF16), b_route=row(b_route),
        w_gate=w_gate, w_up=w_up, w_down=w_down,
        wsc=jnp.repeat(jnp.transpose(ws[:, :, :nt, :nt], (0, 2, 3, 1)), dh, axis=3).astype(F32),
        bsc=bs[:, :nt].astype(F32),
    )


def kernel(x_prompt, x_sample, state_lru_h, state_conv, norm1, w_in, ln_g, ln_b, w_s, b_s, conv_w, conv_b, lru_lambda, w_rg_a, b_rg_a, w_rg_x, b_rg_x, w_out, norm2, w_route_group, b_route_group, w_route_expert, b_route_expert, w_gate, w_up, w_down, norm_f):
    depth = w_in.shape[0]
    nb, seq, d = x_prompt.shape
    ns, nt, _ = x_sample.shape
    assert nt <= CHUNK
    nf = norm_f.reshape(1, d).astype(F32)
    wts = _prep_weights(norm1, w_in, ln_g, ln_b, w_s, b_s, conv_w, conv_b, lru_lambda, w_rg_a,
                        b_rg_a, w_rg_x, b_rg_x, w_out, norm2, w_route_group, b_route_group,
                        w_route_expert, b_route_expert, w_gate, w_up, w_down, nt)
    conv_state = jnp.transpose(state_conv, (0, 2, 1, 3))
    tile = min(MOE_TILE, nt * ns)
    yp = x_prompt
    ys = jnp.transpose(x_sample, (1, 0, 2))
    hp, cp, vp, hs, cs, vs = [], [], [], [], [], []
    for l in range(depth):
        yp, h, c, v = _mixer_prompt(yp, wts, l)
        hp.append(h)
        cp.append(c)
        vp.append(v)
        ys, h, c, v = _mixer_sample(ys, state_lru_h, conv_state, wts, l)
        hs.append(h)
        cs.append(c)
        vs.append(v)
        yp, ys = _moe(yp.reshape(nb * seq, d), ys.reshape(nt * ns, d), wts, l, nf,
                      l == depth - 1, tile)
        yp = yp.reshape(nb, seq, d)
        ys = ys.reshape(nt, ns, d)
    to_seq_major = lambda a: jnp.transpose(jnp.stack(a), (0, 2, 1, 3))
    return (yp, jnp.transpose(ys, (1, 0, 2)), jnp.stack(hp), jnp.stack(cp), jnp.stack(vp),
            jnp.stack(hs), to_seq_major(cs), to_seq_major(vs))
```

```python
import functools

import jax
import jax.numpy as jnp
from jax import lax
from jax.experimental import pallas as pl
from jax.experimental.pallas import tpu as pltpu

F32 = jnp.float32
BF16 = jnp.bfloat16

CHUNK = 128
H_A = 4
H_B = 8
CONV_W = 4
LRU_C = 8.0
N_GROUPS = 4
E_PER_GROUP = 8
N_EXPERTS = N_GROUPS * E_PER_GROUP
EPS = 1e-6

LANES = 128
SUBLANES = 8
SCAN_UNROLL = 4
ROUTE_LANES = 128
MOE_TILE = 512
MOE_BLOCK = 256
VMEM_LIMIT = 60 * 1024 * 1024


def _dot(a, b):
    return jnp.dot(a, b, preferred_element_type=F32)


def _sigmoid(x):
    return 0.5 * (jnp.tanh(0.5 * x) + 1.0)


def _gelu2(x):
    c = 0.7978845608028654
    return x * (1.0 + jnp.tanh(x * (c + (c * 0.044715) * (x * x))))


def _rms_norm(x, g):
    return x * lax.rsqrt(jnp.mean(x * x, axis=-1, keepdims=True) + EPS) * g


def _layer_norm(x, g, b, eps):
    xc = x - jnp.mean(x, axis=-1, keepdims=True)
    var = jnp.mean(xc * xc, axis=-1, keepdims=True)
    return xc * lax.rsqrt(var + eps) * g + b


def _softplus(x):
    return jnp.maximum(x, 0.0) + jnp.log(1.0 + jnp.exp(-jnp.abs(x)))


def _lru_gates(xc, wbd_ref, b_a, b_x, lam_half):
    d = xc.shape[-1]
    xcb = xc.astype(BF16)
    r_parts, i_parts = [], []
    for q in range(d // 256):
        ri = _dot(xcb[:, q * 256:(q + 1) * 256], wbd_ref[q])
        r_parts.append(ri[:, :256])
        i_parts.append(ri[:, 256:])
    r2 = 1.0 + jnp.tanh(jnp.concatenate(r_parts, axis=-1) + b_a)
    i2 = 1.0 + jnp.tanh(jnp.concatenate(i_parts, axis=-1) + b_x)
    a = jnp.exp(lam_half * r2)
    mult_half = jnp.sqrt(0.25 - 0.25 * (a * a))
    return a, mult_half, i2 * xc


def _shift_rows(x, hist, j):
    sh = pltpu.roll(x, j, axis=0)
    top = jnp.where(lax.broadcasted_iota(jnp.int32, (SUBLANES, 1), 0) < j,
                    pltpu.roll(hist, j, axis=0), sh[:SUBLANES])
    return jnp.concatenate([top, sh[SUBLANES:]], axis=0)


def _mixer_prompt_kernel(x_ref, norm_ref, win_ref, lng_ref, lnb_ref, ws_ref, bs_ref,
                         cw_ref, cb_ref, lam_ref, wbd_ref, ba_ref, bx_ref, wout_ref,
                         y_ref, h_ref, conv_ref, v_ref,
                         hist_s, a_s, b_s, ya_s, gb_s, hc_s):
    i = pl.program_id(0)
    last = pl.num_programs(0) - 1
    nseq, _, d = x_ref.shape
    nlt = d // LANES
    pair = 2 * CHUNK

    @pl.when(i == 0)
    def _():
        hist_s[...] = jnp.zeros_like(hist_s)
        hc_s[...] = jnp.zeros_like(hc_s)

    lam_half = (-0.5 * LRU_C) * _softplus(-lam_ref[...])
    is_first = i == 0

    def branch_body(p, c):
        row = pl.multiple_of(p * pair, pair)
        x = jnp.concatenate([x_ref[2 * p], x_ref[2 * p + 1]], axis=0)
        xnb = _rms_norm(x, norm_ref[...]).astype(BF16)
        v = _layer_norm(_gelu2(_dot(xnb, win_ref[:, d:2 * d])), lng_ref[...], lnb_ref[...],
                        4.0 * EPS)

        @pl.when(i == last)
        def _():
            v_ref[2 * p] = v[:CHUNK]
            v_ref[2 * p + 1] = v[CHUNK:]

        vb = v.astype(BF16)
        dh = d // H_A
        z_rows = []
        for k in range(2):
            zs = [_dot(ws_ref[h], vb[k * CHUNK:(k + 1) * CHUNK, h * dh:(h + 1) * dh])
                  for h in range(H_A)]
            z_rows.append(jnp.concatenate(zs, axis=-1) + bs_ref[...])
        z = jnp.concatenate(z_rows, axis=0)
        ya = _gelu2(_dot(xnb, win_ref[:, 0:d])) * z
        ga2 = 1.0 + jnp.tanh(_dot(xnb, win_ref[:, 3 * d:4 * d]))
        ya_s[pl.ds(row, pair), :] = (ga2 * ya).astype(BF16)
        gb_s[pl.ds(row, pair), :] = (1.0 + jnp.tanh(_dot(xnb, win_ref[:, 4 * d:5 * d]))).astype(BF16)
        xr = _dot(xnb, win_ref[:, 2 * d:3 * d])
        xcs = []
        for k in range(2):
            s = 2 * p + k
            xk = xr[k * CHUNK:(k + 1) * CHUNK]
            hist = hist_s[s]
            xc = cb_ref[...] + xk * cw_ref[CONV_W - 1:CONV_W, :]
            for j in range(1, CONV_W):
                xc = xc + _shift_rows(xk, hist, j) * cw_ref[CONV_W - 1 - j:CONV_W - j, :]
            hist_s[s] = xk[CHUNK - SUBLANES:]
            xcs.append(xc)
        xc = jnp.concatenate(xcs, axis=0)
        a, mult_half, ix2 = _lru_gates(xc, wbd_ref, ba_ref[...], bx_ref[...], lam_half)
        bt = mult_half * ix2
        for k in range(2):
            s = 2 * p + k
            for j in range(nlt):
                a_s[j, pl.ds(s, CHUNK, stride=nseq), :] = a[k * CHUNK:(k + 1) * CHUNK, j * LANES:(j + 1) * LANES]
                b_s[j, pl.ds(s, CHUNK, stride=nseq), :] = bt[k * CHUNK:(k + 1) * CHUNK, j * LANES:(j + 1) * LANES]

        @pl.when(is_first)
        def _():
            for k in range(2):
                first = 0.5 * ix2[k * CHUNK:k * CHUNK + 1]
                for j in range(nlt):
                    b_s[j, pl.ds(2 * p + k, 1), :] = first[:, j * LANES:(j + 1) * LANES]

        return c

    lax.fori_loop(0, nseq // 2, branch_body, 0)

    def scan_body(tt, hs):
        for u in range(SCAN_UNROLL):
            row = pl.multiple_of((tt * SCAN_UNROLL + u) * nseq, nseq)
            out = []
            for j in range(nlt):
                h = a_s[j, pl.ds(row, nseq), :] * hs[j] + b_s[j, pl.ds(row, nseq), :]
                b_s[j, pl.ds(row, nseq), :] = h
                out.append(h)
            hs = tuple(out)
        return hs

    h0 = tuple(hc_s[:, j * LANES:(j + 1) * LANES] for j in range(nlt))
    hs = lax.fori_loop(0, CHUNK // SCAN_UNROLL, scan_body, h0)
    hfin = jnp.concatenate(hs, axis=-1)
    hc_s[...] = hfin
    h_ref[...] = hfin

    def out_body(p, c):
        row = pl.multiple_of(p * pair, pair)
        hrows = []
        for k in range(2):
            hrows.append(jnp.concatenate(
                [b_s[j, pl.ds(2 * p + k, CHUNK, stride=nseq), :] for j in range(nlt)], axis=-1))
        h = jnp.concatenate(hrows, axis=0)
        merged = (ya_s[pl.ds(row, pair), :].astype(F32)
                  + gb_s[pl.ds(row, pair), :].astype(F32) * h).astype(BF16)
        o = _dot(merged, wout_ref[...])
        y_ref[2 * p] = x_ref[2 * p] + o[:CHUNK]
        y_ref[2 * p + 1] = x_ref[2 * p + 1] + o[CHUNK:]
        return c

    lax.fori_loop(0, nseq // 2, out_body, 0)

    @pl.when(i == last)
    def _():
        conv_ref[...] = hist_s[:, SUBLANES - (CONV_W - 1):, :]


def _const_spec(shape):
    nd = len(shape)
    return pl.BlockSpec(shape, lambda i, *_, _n=nd: (0,) * _n, pipeline_mode=pl.Buffered(1))


def _layer_spec(stacked, layer):
    shape = stacked.shape[1:]
    nd = len(shape)
    return pl.BlockSpec((None,) + tuple(shape), lambda i, *_, _n=nd: (layer,) + (0,) * _n,
                        pipeline_mode=pl.Buffered(1))


MIXER_WEIGHTS = ('norm1', 'w_in', 'ln_g', 'ln_b', 'ws', 'bs', 'conv_w', 'conv_b', 'lam', 'wbd',
                 'b_a', 'b_x', 'w_out')


def _mixer_prompt(x, wts, layer):
    nseq, seq, d = x.shape
    assert seq % CHUNK == 0 and nseq == SUBLANES and d % 256 == 0
    n_chunks = seq // CHUNK
    nlt = d // LANES
    weights = tuple(wts[k] for k in MIXER_WEIGHTS)
    x_spec = pl.BlockSpec((nseq, CHUNK, d), lambda i: (0, i, 0))
    out_shape = (
        jax.ShapeDtypeStruct((nseq, seq, d), F32),
        jax.ShapeDtypeStruct((nseq, d), F32),
        jax.ShapeDtypeStruct((nseq, CONV_W - 1, d), F32),
        jax.ShapeDtypeStruct((nseq, CHUNK, d), F32),
    )
    out_specs = (
        x_spec,
        _const_spec((nseq, d)),
        _const_spec((nseq, CONV_W - 1, d)),
        _const_spec((nseq, CHUNK, d)),
    )
    scratch = [
        pltpu.VMEM((nseq, SUBLANES, d), F32),
        pltpu.VMEM((nlt, CHUNK * nseq, LANES), F32),
        pltpu.VMEM((nlt, CHUNK * nseq, LANES), F32),
        pltpu.VMEM((nseq * CHUNK, d), BF16),
        pltpu.VMEM((nseq * CHUNK, d), BF16),
        pltpu.VMEM((nseq, d), F32),
    ]
    return pl.pallas_call(
        _mixer_prompt_kernel,
        grid=(n_chunks,),
        in_specs=[x_spec] + [_layer_spec(w, layer) for w in weights],
        out_specs=out_specs,
        out_shape=out_shape,
        scratch_shapes=scratch,
        compiler_params=pltpu.CompilerParams(
            dimension_semantics=("arbitrary",), vmem_limit_bytes=VMEM_LIMIT),
        name="mixer_prompt",
    )(x, *weights)


def _mixer_sample_kernel(x_ref, h0_ref, cst_ref, norm_ref, win_ref, lng_ref, lnb_ref, wsc_ref,
                         bsc_ref, cw_ref, cb_ref, lam_ref, wbd_ref, ba_ref, bx_ref, wout_ref,
                         y_ref, h_ref, conv_ref, v_ref):
    nt, ns, d = x_ref.shape
    lam_half = (-0.5 * LRU_C) * _softplus(-lam_ref[...])
    g1 = norm_ref[...]
    h = h0_ref[...]
    xp = [cst_ref[k] for k in range(CONV_W - 1)]
    for t in range(nt):
        x = x_ref[t]
        xnb = _rms_norm(x, g1).astype(BF16)
        v = _layer_norm(_gelu2(_dot(xnb, win_ref[:, d:2 * d])), lng_ref[...], lnb_ref[...],
                        4.0 * EPS)
        v_ref[t] = v
        z = bsc_ref[t:t + 1, :]
        for s in range(t + 1):
            z = z + wsc_ref[t, s:s + 1, :] * v_ref[s]
        ya = _gelu2(_dot(xnb, win_ref[:, 0:d])) * z
        ga2 = 1.0 + jnp.tanh(_dot(xnb, win_ref[:, 3 * d:4 * d]))
        xr = _dot(xnb, win_ref[:, 2 * d:3 * d])
        xp.append(xr)
        xc = cb_ref[...]
        for k in range(CONV_W):
            xc = xc + xp[t + k] * cw_ref[k:k + 1, :]
        a, mult_half, ix2 = _lru_gates(xc, wbd_ref, ba_ref[...], bx_ref[...], lam_half)
        h = a * h + mult_half * ix2
        gb2 = 1.0 + jnp.tanh(_dot(xnb, win_ref[:, 4 * d:5 * d]))
        merged = (ga2 * ya + gb2 * h).astype(BF16)
        y_ref[t] = x + _dot(merged, wout_ref[...])
    h_ref[...] = h
    for k in range(CONV_W - 1):
        conv_ref[k] = xp[nt + k]


def _mixer_sample(x, h0, conv_state, wts, layer):
    nt, ns, d = x.shape
    names = MIXER_WEIGHTS[:4] + ('wsc', 'bsc') + MIXER_WEIGHTS[6:]
    stacked = (h0, conv_state) + tuple(wts[k] for k in names)
    out_shape = (
        jax.ShapeDtypeStruct((nt, ns, d), F32),
        jax.ShapeDtypeStruct((ns, d), F32),
        jax.ShapeDtypeStruct((CONV_W - 1, ns, d), F32),
        jax.ShapeDtypeStruct((nt, ns, d), F32),
    )
    return pl.pallas_call(
        _mixer_sample_kernel,
        grid=(1,),
        in_specs=[_const_spec(x.shape)] + [_layer_spec(w, layer) for w in stacked],
        out_specs=tuple(_const_spec(s.shape) for s in out_shape),
        out_shape=out_shape,
        compiler_params=pltpu.CompilerParams(
            dimension_semantics=("arbitrary",), vmem_limit_bytes=VMEM_LIMIT),
        name="mixer_sample",
    )(x, *stacked)


def _route(logits):
    lane = lax.broadcasted_iota(jnp.int32, logits.shape, 1).astype(F32)
    neg = jnp.float32(-jnp.inf)
    big = jnp.float32(ROUTE_LANES)

    def argmax_first(vals):
        m = jnp.max(vals, axis=-1, keepdims=True)
        idx = jnp.min(jnp.where(vals == m, lane, big), axis=-1, keepdims=True)
        return m, idx

    gl = jnp.where(lane < N_GROUPS, logits, neg)
    gm, gi = argmax_first(gl)
    pg_top = 1.0 / jnp.sum(jnp.exp(gl - gm), axis=-1, keepdims=True)
    lo = N_GROUPS + gi * E_PER_GROUP
    el = jnp.where(jnp.logical_and(lane >= lo, lane < lo + E_PER_GROUP), logits, neg)
    m1, i1 = argmax_first(el)
    m2, i2 = argmax_first(jnp.where(lane == i1, neg, el))
    e2 = jnp.exp(m2 - m1)
    w1 = pg_top / (1.0 + e2)
    w2 = pg_top * e2 / (1.0 + e2)
    return i1, i2, w1, w2


def _lane_cumsum_exclusive(x):
    lane = lax.broadcasted_iota(jnp.int32, x.shape, 1)
    acc = x
    shift = 1
    while shift < LANES:
        acc = acc + jnp.where(lane >= shift, pltpu.roll(acc, shift, axis=1), 0.0)
        shift *= 2
    return acc - x


def _to_slab(ref, val):
    t, d = val.shape
    nlt = d // LANES
    for j in range(nlt):
        ref[pl.ds(j, t, stride=nlt), :] = val[:, j * LANES:(j + 1) * LANES]


def _from_slab(ref, t, nlt):
    return jnp.concatenate([ref[pl.ds(j, t, stride=nlt), :] for j in range(nlt)], axis=-1)


def _tokens_from_slab(slab_ref, nlt):
    return slab_ref[...].reshape(slab_ref.shape[0] // nlt, nlt, LANES).astype(BF16)


def _tokens_to_slab(slab_ref, tokens):
    slab_ref[...] = tokens.astype(F32).reshape(slab_ref.shape)


def _router_kernel(yp_ref, ys_ref, norm_ref, wr_ref, br_ref, tri_ref, xn_ref, mf_ref, mi_ref,
                   cnt_ref, slab_s, *, n_prompt_tiles):
    y = jnp.where(pl.program_id(0) < n_prompt_tiles, yp_ref[...], ys_ref[...])
    xn = _rms_norm(y, norm_ref[...])
    _to_slab(slab_s, xn)
    xn_ref[...] = _tokens_from_slab(slab_s, xn_ref.shape[1])
    logits = _dot(xn.astype(BF16), wr_ref[...]) + br_ref[...]
    i1, i2, w1, w2 = _route(logits)
    lane = lax.broadcasted_iota(jnp.int32, logits.shape, 1).astype(F32)
    hot1 = lane == i1
    hot2 = lane == i2
    onehot = jnp.where(jnp.logical_or(hot1, hot2), 1.0, 0.0)
    before = _dot(tri_ref[...], onehot.astype(BF16))
    cnt = jnp.sum(onehot, axis=0, keepdims=True)
    cnt8 = jnp.broadcast_to(cnt, (SUBLANES, ROUTE_LANES))
    seg_lo = _lane_cumsum_exclusive(cnt8)[0:1, :]
    slot = before + seg_lo
    p1 = jnp.sum(jnp.where(hot1, slot, 0.0), axis=-1, keepdims=True)
    p2 = jnp.sum(jnp.where(hot2, slot, 0.0), axis=-1, keepdims=True)
    cols = (i1 - N_GROUPS, i2 - N_GROUPS, w1, w2, p1, p2)
    sel = jnp.zeros_like(logits)
    for k, c in enumerate(cols):
        sel = jnp.where(lane == k, c, sel)
    meta = sel.T[:SUBLANES]
    mf_ref[0] = meta
    mi_ref[0] = meta.astype(jnp.int32)
    cnt_ref[0] = cnt8


def _token_specs(tile, d, n_prompt_tiles):
    last = n_prompt_tiles - 1
    return (pl.BlockSpec((tile, d), lambda i, *_: (jnp.minimum(i, last), 0)),
            pl.BlockSpec((tile, d), lambda i, *_: (jnp.maximum(i - n_prompt_tiles, 0), 0)))


def _router(yp, ys, wts, layer, tile):
    d = yp.shape[1]
    nlt = d // LANES
    n_prompt_tiles = yp.shape[0] // tile
    n_tiles = n_prompt_tiles + ys.shape[0] // tile
    tri = jnp.tril(jnp.ones((tile, tile), BF16), -1)
    meta_spec = pl.BlockSpec((1, SUBLANES, tile), lambda i: (i, 0, 0))
    return pl.pallas_call(
        functools.partial(_router_kernel, n_prompt_tiles=n_prompt_tiles),
        grid=(n_tiles,),
        in_specs=list(_token_specs(tile, d, n_prompt_tiles)) + [
            _layer_spec(wts['norm2'], layer), _layer_spec(wts['w_route'], layer),
            _layer_spec(wts['b_route'], layer), _const_spec((tile, tile))],
        out_specs=(pl.BlockSpec((tile, nlt, LANES), lambda i: (i, 0, 0)), meta_spec, meta_spec,
                   pl.BlockSpec((1, SUBLANES, ROUTE_LANES), lambda i: (i, 0, 0))),
        out_shape=(jax.ShapeDtypeStruct((n_tiles * tile, nlt, LANES), BF16),
                   jax.ShapeDtypeStruct((n_tiles, SUBLANES, tile), F32),
                   jax.ShapeDtypeStruct((n_tiles, SUBLANES, tile), jnp.int32),
                   jax.ShapeDtypeStruct((n_tiles, SUBLANES, ROUTE_LANES), F32)),
        scratch_shapes=[pltpu.VMEM((tile * nlt, LANES), F32)],
        compiler_params=pltpu.CompilerParams(
            dimension_semantics=("arbitrary",), vmem_limit_bytes=VMEM_LIMIT),
        name="moe_router",
    )(yp, ys, wts['norm2'], wts['w_route'], wts['b_route'], tri)


def _run_copy(tile_idx, e, lo_ref, cnt_ref, dst_ref, local, remote, sem, to_remote):
    k = tile_idx * N_EXPERTS + e
    loc = local.at[pl.ds(lo_ref[k], cnt_ref[k])]
    rem = remote.at[pl.ds(dst_ref[k], cnt_ref[k])]
    return pltpu.make_async_copy(loc, rem, sem) if to_remote else pltpu.make_async_copy(rem, loc, sem)


def _start_runs(tile_idx, lo_ref, cnt_ref, dst_ref, local, remote, sem, to_remote):
    def body(e, c):
        @pl.when(cnt_ref[tile_idx * N_EXPERTS + e] > 0)
        def _():
            _run_copy(tile_idx, e, lo_ref, cnt_ref, dst_ref, local, remote, sem, to_remote).start()
        return c
    lax.fori_loop(0, N_EXPERTS, body, 0)


def _wait_runs(local, sem):
    pltpu.make_async_copy(local, local, sem).wait()


def _dispatch_kernel(lo_ref, cnt_ref, dst_ref, padst_ref, padn_ref,
                     s1_ref, s2_ref, x_ref, xs_ref, loc, zeros, sem, *, unroll):
    i = pl.program_id(0)
    last = pl.num_programs(0) - 1
    tile = x_ref.shape[0]
    slot = i % 2
    buf = loc.at[slot]

    @pl.when(i >= 2)
    def _():
        _wait_runs(buf, sem.at[slot])

    def body(tt, c):
        for u in range(unroll):
            t = tt * unroll + u
            row = x_ref[t]
            buf[s1_ref[t]] = row
            buf[s2_ref[t]] = row
        return c

    lax.fori_loop(0, tile // unroll, body, 0)
    _start_runs(i, lo_ref, cnt_ref, dst_ref, buf, xs_ref, sem.at[slot], True)

    @pl.when(i == last)
    def _():
        zeros[...] = jnp.zeros_like(zeros)

        def pad_copy(e):
            n = padn_ref[e]
            dst = xs_ref.at[pl.ds(padst_ref[e], n)]
            return pltpu.make_async_copy(zeros.at[pl.ds(0, n)], dst, sem.at[2])

        def each_pad(fn):
            def b(e, c):
                @pl.when(padn_ref[e] > 0)
                def _():
                    fn(e)
                return c
            lax.fori_loop(0, padn_ref.shape[0], b, 0)

        each_pad(lambda e: pad_copy(e).start())
        _wait_runs(buf, sem.at[slot])

        @pl.when(i >= 1)
        def _():
            _wait_runs(loc.at[1 - slot], sem.at[1 - slot])

        each_pad(lambda e: pad_copy(e).wait())


def _smem_tile(tile):
    return pl.BlockSpec((tile,), lambda i, *_: (i,), memory_space=pltpu.SMEM)


def _dispatch(xn_slab, s1, s2, tables, tile, n_rows):
    token = xn_slab.shape[1:]
    return pl.pallas_call(
        functools.partial(_dispatch_kernel, unroll=8),
        grid_spec=pltpu.PrefetchScalarGridSpec(
            num_scalar_prefetch=5,
            grid=(xn_slab.shape[0] // tile,),
            in_specs=[_smem_tile(tile), _smem_tile(tile),
                      pl.BlockSpec((tile,) + token, lambda i, *_: (i, 0, 0))],
            out_specs=pl.BlockSpec(memory_space=pl.ANY),
            scratch_shapes=[pltpu.VMEM((2, 2 * tile) + token, xn_slab.dtype),
                            pltpu.VMEM((MOE_BLOCK,) + token, xn_slab.dtype),
                            pltpu.SemaphoreType.DMA((3,))]),
        out_shape=jax.ShapeDtypeStruct((n_rows,) + token, xn_slab.dtype),
        compiler_params=pltpu.CompilerParams(
            dimension_semantics=("arbitrary",), vmem_limit_bytes=VMEM_LIMIT),
        name="moe_dispatch",
    )(*tables, s1, s2, xn_slab)


def _experts_kernel(blk_ref, nused_ref, xs_ref, wg_ref, wu_ref, wd_ref, ys_ref,
                    wg_s, wu_s, wd_s, slab_s):
    b = pl.program_id(0)
    rows, nlt, _ = xs_ref.shape

    @pl.when(jnp.logical_or(b == 0, blk_ref[b] != blk_ref[jnp.maximum(b - 1, 0)]))
    def _():
        wg_s[...] = wg_ref[...].astype(BF16)
        wu_s[...] = wu_ref[...].astype(BF16)
        wd_s[...] = wd_ref[...].astype(BF16)

    @pl.when(b < nused_ref[0])
    def _():
        _tokens_to_slab(slab_s, xs_ref[...])
        x = _from_slab(slab_s, rows, nlt).astype(BF16)
        hg = _dot(x, wg_s[...])
        hu = _dot(x, wu_s[...])
        h = (hg * _sigmoid(hg) * hu).astype(BF16)
        _to_slab(slab_s, _dot(h, wd_s[...]))
        ys_ref[...] = _tokens_from_slab(slab_s, nlt)

    @pl.when(b >= nused_ref[0])
    def _():
        ys_ref[...] = jnp.zeros_like(ys_ref)


def _experts(xs, blk_expert, n_used, wts, layer):
    n_rows, nlt, _ = xs.shape
    d = nlt * LANES
    f = wts['w_gate'].shape[-1]
    rows = pl.BlockSpec((MOE_BLOCK, nlt, LANES), lambda b, blk, nu: (b, 0, 0))
    w_spec = lambda r, c: pl.BlockSpec((None, None, r, c), lambda b, blk, nu: (layer, blk[b], 0, 0))
    return pl.pallas_call(
        _experts_kernel,
        grid_spec=pltpu.PrefetchScalarGridSpec(
            num_scalar_prefetch=2,
            grid=(n_rows // MOE_BLOCK,),
            in_specs=[rows, w_spec(d, f), w_spec(d, f), w_spec(f, d)],
            out_specs=rows,
            scratch_shapes=[pltpu.VMEM((d, f), BF16), pltpu.VMEM((d, f), BF16),
                            pltpu.VMEM((f, d), BF16),
                            pltpu.VMEM((MOE_BLOCK * nlt, LANES), F32)]),
        out_shape=jax.ShapeDtypeStruct(xs.shape, xs.dtype),
        compiler_params=pltpu.CompilerParams(
            dimension_semantics=("arbitrary",), vmem_limit_bytes=VMEM_LIMIT),
        name="moe_experts",
    )(blk_expert, n_used, xs, wts['w_gate'], wts['w_up'], wts['w_down'])


def _combine_kernel(lo_ref, cnt_ref, dst_ref, s1_ref, s2_ref, w1_ref, w2_ref, yp_ref, ys_ref,
                    nf_ref, rows_ref, op_ref, os_ref, loc, acc, sem,
                    *, unroll, final_norm, n_prompt_tiles):
    i = pl.program_id(0)
    tile = yp_ref.shape[0]
    nlt = loc.shape[2]
    slot = i % 2
    buf = loc.at[slot]
    fetch = lambda t, s: _start_runs(t, lo_ref, cnt_ref, dst_ref, loc.at[s], rows_ref, sem.at[s],
                                     False)

    @pl.when(i == 0)
    def _():
        fetch(0, 0)

    @pl.when(i + 1 < pl.num_programs(0))
    def _():
        fetch(i + 1, 1 - slot)

    _wait_runs(buf, sem.at[slot])

    def body(tt, c):
        for u in range(unroll):
            t = tt * unroll + u
            r1 = buf[s1_ref[t]].astype(F32)
            r2 = buf[s2_ref[t]].astype(F32)
            acc[pl.ds(pl.multiple_of(t * nlt, nlt), nlt), :] = w1_ref[t] * r1 + w2_ref[t] * r2
        return c

    lax.fori_loop(0, tile // unroll, body, 0)
    is_prompt = i < n_prompt_tiles
    out = jnp.where(is_prompt, yp_ref[...], ys_ref[...]) + _from_slab(acc, tile, nlt)
    if final_norm:
        out = _rms_norm(out, nf_ref[...])

    @pl.when(is_prompt)
    def _():
        op_ref[...] = out

    @pl.when(jnp.logical_not(is_prompt))
    def _():
        os_ref[...] = out


def _combine(yp, ys, rows, meta, tables, norm_f, final_norm, tile):
    d = yp.shape[1]
    token = rows.shape[1:]
    n_prompt_tiles = yp.shape[0] // tile
    n_tiles = n_prompt_tiles + ys.shape[0] // tile
    tok_p, tok_s = _token_specs(tile, d, n_prompt_tiles)
    return pl.pallas_call(
        functools.partial(_combine_kernel, unroll=8, final_norm=final_norm,
                          n_prompt_tiles=n_prompt_tiles),
        grid_spec=pltpu.PrefetchScalarGridSpec(
            num_scalar_prefetch=3,
            grid=(n_tiles,),
            in_specs=[_smem_tile(tile)] * 4 + [tok_p, tok_s, _const_spec((1, d)),
                                               pl.BlockSpec(memory_space=pl.ANY)],
            out_specs=(tok_p, tok_s),
            scratch_shapes=[pltpu.VMEM((2, 2 * tile) + token, rows.dtype),
                            pltpu.VMEM((tile * token[0], LANES), F32),
                            pltpu.SemaphoreType.DMA((2,))]),
        out_shape=(jax.ShapeDtypeStruct(yp.shape, F32), jax.ShapeDtypeStruct(ys.shape, F32)),
        compiler_params=pltpu.CompilerParams(
            dimension_semantics=("arbitrary",), vmem_limit_bytes=VMEM_LIMIT),
        name="moe_combine",
    )(*tables, *meta, yp, ys, norm_f, rows)


def _moe(yp, ys, wts, layer, norm_f, final_norm, tile):
    d = yp.shape[1]
    n = yp.shape[0] + ys.shape[0]
    assert yp.shape[0] % tile == 0 and ys.shape[0] % tile == 0 and d % LANES == 0
    xn_slab, meta_f, meta_i, cnt = _router(yp, ys, wts, layer, tile)
    cnt = cnt[:, 0, N_GROUPS:N_GROUPS + N_EXPERTS].astype(jnp.int32)
    lo = jnp.cumsum(cnt, axis=1) - cnt
    seg = jnp.sum(cnt, axis=0)
    seg_pad = (seg + MOE_BLOCK - 1) // MOE_BLOCK * MOE_BLOCK
    seg_end = jnp.cumsum(seg_pad)
    seg_start = seg_end - seg_pad
    dst = seg_start[None, :] + jnp.cumsum(cnt, axis=0) - cnt
    n_blocks = (2 * n + N_EXPERTS * (MOE_BLOCK - 1)) // MOE_BLOCK
    blk_ids = jnp.arange(n_blocks, dtype=jnp.int32)
    blk_expert = jnp.minimum(
        jnp.sum((seg_end[None, :] <= blk_ids[:, None] * MOE_BLOCK).astype(jnp.int32), axis=1),
        N_EXPERTS - 1)
    n_used = seg_end[-1:] // MOE_BLOCK
    pad_start = jnp.concatenate([seg_start + seg, blk_ids * MOE_BLOCK])
    pad_rows = jnp.concatenate([seg_pad - seg, jnp.where(blk_ids >= n_used[0], MOE_BLOCK, 0)])
    flat = lambda a: a.reshape(-1)
    runs = (flat(lo), flat(cnt), flat(dst))
    s1, s2 = flat(meta_i[:, 4]), flat(meta_i[:, 5])
    w1, w2 = flat(meta_f[:, 2]), flat(meta_f[:, 3])
    xs = _dispatch(xn_slab, s1, s2, runs + (pad_start, pad_rows), tile, n_blocks * MOE_BLOCK)
    rows = _experts(xs, blk_expert, n_used, wts, layer)
    return _combine(yp, ys, rows, (s1, s2, w1, w2), runs, norm_f, final_norm, tile)


def _prep_weights(norm1, w_in, ln_g, ln_b, w_s, b_s, conv_w, conv_b, lru_lambda, w_rg_a, b_rg_a,
                  w_rg_x, b_rg_x, w_out, norm2, w_route_group, b_route_group, w_route_expert,
                  b_route_expert, w_gate, w_up, w_down, n_sample_t):
    depth, d = w_in.shape[0], w_in.shape[1]
    dh = d // H_A
    row = lambda a: a.reshape(depth, 1, -1).astype(F32)
    causal = jnp.tril(jnp.ones((CHUNK, CHUNK), dtype=bool))
    ws = jnp.where(causal, 0.5 * w_s, 0.0)
    bs = jnp.repeat(jnp.swapaxes(0.5 * b_s, 1, 2), dh, axis=2)
    in_scale = jnp.concatenate([jnp.ones((3 * d,), F32), jnp.full((2 * d,), 0.5, F32)])

    def pair_bd(w):
        w = w.reshape(depth, H_B // 2, 2, w.shape[-2], w.shape[-1])
        z = jnp.zeros_like(w[:, :, 0])
        return jnp.concatenate([jnp.concatenate([w[:, :, 0], z], axis=-1),
                                jnp.concatenate([z, w[:, :, 1]], axis=-1)], axis=-2)

    pad = ROUTE_LANES - N_GROUPS - N_EXPERTS
    w_route = jnp.concatenate(
        [w_route_group, w_route_expert, jnp.zeros((depth, d, pad), F32)], axis=-1)
    b_route = jnp.concatenate(
        [b_route_group, b_route_expert, jnp.zeros((depth, pad), F32)], axis=-1)
    nt = n_sample_t
    return dict(
        norm1=row(norm1), w_in=(w_in * in_scale).astype(BF16), ln_g=row(ln_g), ln_b=row(ln_b),
        ws=ws.astype(BF16), bs=bs.astype(F32), conv_w=conv_w.astype(F32), conv_b=row(conv_b),
        lam=row(lru_lambda),
        wbd=(0.5 * jnp.concatenate([pair_bd(w_rg_a), pair_bd(w_rg_x)], axis=-1)).astype(BF16),
        b_a=row(0.5 * b_rg_a), b_x=row(0.5 * b_rg_x), w_out=(0.5 * w_out).astype(BF16),
        norm2=row(norm2), w_route=w_route.astype(BF16), b_route=row(b_route),
        w_gate=w_gate, w_up=w_up, w_down=w_down,
        wsc=jnp.repeat(jnp.transpose(ws[:, :, :nt, :nt], (0, 2, 3, 1)), dh, axis=3).astype(F32),
        bsc=bs[:, :nt].astype(F32),
    )


def kernel(x_prompt, x_sample, state_lru_h, state_conv, norm1, w_in, ln_g, ln_b, w_s, b_s, conv_w, conv_b, lru_lambda, w_rg_a, b_rg_a, w_rg_x, b_rg_x, w_out, norm2, w_route_group, b_route_group, w_route_expert, b_route_expert, w_gate, w_up, w_down, norm_f):
    depth = w_in.shape[0]
    nb, seq, d = x_prompt.shape
    ns, nt, _ = x_sample.shape
    assert nt <= CHUNK
    nf = norm_f.reshape(1, d).astype(F32)
    wts = _prep_weights(norm1, w_in, ln_g, ln_b, w_s, b_s, conv_w, conv_b, lru_lambda, w_rg_a,
                        b_rg_a, w_rg_x, b_rg_x, w_out, norm2, w_route_group, b_route_group,
                        w_route_expert, b_route_expert, w_gate, w_up, w_down, nt)
    conv_state = jnp.transpose(state_conv, (0, 2, 1, 3))
    tile = min(MOE_TILE, nt * ns)
    yp = x_prompt
    ys = jnp.transpose(x_sample, (1, 0, 2))
    hp, cp, vp, hs, cs, vs = [], [], [], [], [], []
    for l in range(depth):
        yp, h, c, v = _mixer_prompt(yp, wts, l)
        hp.append(h)
        cp.append(c)
        vp.append(v)
        ys, h, c, v = _mixer_sample(ys, state_lru_h, conv_state, wts, l)
        hs.append(h)
        cs.append(c)
        vs.append(v)
        yp, ys = _moe(yp.reshape(nb * seq, d), ys.reshape(nt * ns, d), wts, l, nf,
                      l == depth - 1, tile)
        yp = yp.reshape(nb, seq, d)
        ys = ys.reshape(nt, ns, d)
    to_seq_major = lambda a: jnp.transpose(jnp.stack(a), (0, 2, 1, 3))
    return (yp, jnp.transpose(ys, (1, 0, 2)), jnp.stack(hp), jnp.stack(cp), jnp.stack(vp),
            jnp.stack(hs), to_seq_major(cs), to_seq_major(vs))
```

```python
import functools

import jax
import jax.numpy as jnp
from jax import lax
from jax.experimental import pallas as pl
from jax.experimental.pallas import tpu as pltpu

F32 = jnp.float32
BF16 = jnp.bfloat16

CHUNK = 128
H_A = 4
H_B = 8
CONV_W = 4
LRU_C = 8.0
N_GROUPS = 4
E_PER_GROUP = 8
N_EXPERTS = N_GROUPS * E_PER_GROUP
EPS = 1e-6

LANES = 128
SUBLANES = 8
SCAN_UNROLL = 4
ROUTE_LANES = 128
MOE_TILE = 512
MOE_BLOCK = 512
VMEM_LIMIT = 60 * 1024 * 1024


def _dot(a, b):
    return jnp.dot(a, b, preferred_element_type=F32)


def _sigmoid(x):
    return 0.5 * (jnp.tanh(0.5 * x) + 1.0)


def _gelu2(x):
    c = 0.7978845608028654
    return x * (1.0 + jnp.tanh(x * (c + (c * 0.044715) * (x * x))))


def _rms_norm(x, g):
    return x * lax.rsqrt(jnp.mean(x * x, axis=-1, keepdims=True) + EPS) * g


def _layer_norm(x, g, b, eps):
    xc = x - jnp.mean(x, axis=-1, keepdims=True)
    var = jnp.mean(xc * xc, axis=-1, keepdims=True)
    return xc * lax.rsqrt(var + eps) * g + b


def _softplus(x):
    return jnp.maximum(x, 0.0) + jnp.log(1.0 + jnp.exp(-jnp.abs(x)))


def _lru_gates(xc, wbd_ref, b_a, b_x, lam_half):
    d = xc.shape[-1]
    xcb = xc.astype(BF16)
    r_parts, i_parts = [], []
    for q in range(d // 256):
        ri = _dot(xcb[:, q * 256:(q + 1) * 256], wbd_ref[q])
        r_parts.append(ri[:, :256])
        i_parts.append(ri[:, 256:])
    r2 = 1.0 + jnp.tanh(jnp.concatenate(r_parts, axis=-1) + b_a)
    i2 = 1.0 + jnp.tanh(jnp.concatenate(i_parts, axis=-1) + b_x)
    a = jnp.exp(lam_half * r2)
    mult_half = jnp.sqrt(0.25 - 0.25 * (a * a))
    return a, mult_half, i2 * xc


def _shift_rows(x, hist, j):
    sh = pltpu.roll(x, j, axis=0)
    top = jnp.where(lax.broadcasted_iota(jnp.int32, (SUBLANES, 1), 0) < j,
                    pltpu.roll(hist, j, axis=0), sh[:SUBLANES])
    return jnp.concatenate([top, sh[SUBLANES:]], axis=0)


def _mixer_prompt_kernel(x_ref, norm_ref, win_ref, lng_ref, lnb_ref, ws_ref, bs_ref,
                         cw_ref, cb_ref, lam_ref, wbd_ref, ba_ref, bx_ref, wout_ref,
                         y_ref, h_ref, conv_ref, v_ref,
                         hist_s, a_s, b_s, ya_s, gb_s, hc_s):
    i = pl.program_id(0)
    last = pl.num_programs(0) - 1
    nseq, _, d = x_ref.shape
    nlt = d // LANES
    pair = 2 * CHUNK

    @pl.when(i == 0)
    def _():
        hist_s[...] = jnp.zeros_like(hist_s)
        hc_s[...] = jnp.zeros_like(hc_s)

    lam_half = (-0.5 * LRU_C) * _softplus(-lam_ref[...])
    is_first = i == 0

    def branch_body(p, c):
        row = pl.multiple_of(p * pair, pair)
        x = jnp.concatenate([x_ref[2 * p], x_ref[2 * p + 1]], axis=0)
        xnb = _rms_norm(x, norm_ref[...]).astype(BF16)
        v = _layer_norm(_gelu2(_dot(xnb, win_ref[:, d:2 * d])), lng_ref[...], lnb_ref[...],
                        4.0 * EPS)

        @pl.when(i == last)
        def _():
            v_ref[2 * p] = v[:CHUNK]
            v_ref[2 * p + 1] = v[CHUNK:]

        vb = v.astype(BF16)
        dh = d // H_A
        z_rows = []
        for k in range(2):
            zs = [_dot(ws_ref[h], vb[k * CHUNK:(k + 1) * CHUNK, h * dh:(h + 1) * dh])
                  for h in range(H_A)]
            z_rows.append(jnp.concatenate(zs, axis=-1) + bs_ref[...])
        z = jnp.concatenate(z_rows, axis=0)
        ya = _gelu2(_dot(xnb, win_ref[:, 0:d])) * z
        ga2 = 1.0 + jnp.tanh(_dot(xnb, win_ref[:, 3 * d:4 * d]))
        ya_s[pl.ds(row, pair), :] = (ga2 * ya).astype(BF16)
        gb_s[pl.ds(row, pair), :] = (1.0 + jnp.tanh(_dot(xnb, win_ref[:, 4 * d:5 * d]))).astype(BF16)
        xr = _dot(xnb, win_ref[:, 2 * d:3 * d])
        xcs = []
        for k in range(2):
            s = 2 * p + k
            xk = xr[k * CHUNK:(k + 1) * CHUNK]
            hist = hist_s[s]
            xc = cb_ref[...] + xk * cw_ref[CONV_W - 1:CONV_W, :]
            for j in range(1, CONV_W):
                xc = xc + _shift_rows(xk, hist, j) * cw_ref[CONV_W - 1 - j:CONV_W - j, :]
            hist_s[s] = xk[CHUNK - SUBLANES:]
            xcs.append(xc)
        xc = jnp.concatenate(xcs, axis=0)
        a, mult_half, ix2 = _lru_gates(xc, wbd_ref, ba_ref[...], bx_ref[...], lam_half)
        bt = mult_half * ix2
        for k in range(2):
            s = 2 * p + k
            for j in range(nlt):
                a_s[j, pl.ds(s, CHUNK, stride=nseq), :] = a[k * CHUNK:(k + 1) * CHUNK, j * LANES:(j + 1) * LANES]
                b_s[j, pl.ds(s, CHUNK, stride=nseq), :] = bt[k * CHUNK:(k + 1) * CHUNK, j * LANES:(j + 1) * LANES]

        @pl.when(is_first)
        def _():
            for k in range(2):
                first = 0.5 * ix2[k * CHUNK:k * CHUNK + 1]
                for j in range(nlt):
                    b_s[j, pl.ds(2 * p + k, 1), :] = first[:, j * LANES:(j + 1) * LANES]

        return c

    lax.fori_loop(0, nseq // 2, branch_body, 0)

    def scan_body(tt, hs):
        for u in range(SCAN_UNROLL):
            row = pl.multiple_of((tt * SCAN_UNROLL + u) * nseq, nseq)
            out = []
            for j in range(nlt):
                h = a_s[j, pl.ds(row, nseq), :] * hs[j] + b_s[j, pl.ds(row, nseq), :]
                b_s[j, pl.ds(row, nseq), :] = h
                out.append(h)
            hs = tuple(out)
        return hs

    h0 = tuple(hc_s[:, j * LANES:(j + 1) * LANES] for j in range(nlt))
    hs = lax.fori_loop(0, CHUNK // SCAN_UNROLL, scan_body, h0)
    hfin = jnp.concatenate(hs, axis=-1)
    hc_s[...] = hfin
    h_ref[...] = hfin

    def out_body(p, c):
        row = pl.multiple_of(p * pair, pair)
        hrows = []
        for k in range(2):
            hrows.append(jnp.concatenate(
                [b_s[j, pl.ds(2 * p + k, CHUNK, stride=nseq), :] for j in range(nlt)], axis=-1))
        h = jnp.concatenate(hrows, axis=0)
        merged = (ya_s[pl.ds(row, pair), :].astype(F32)
                  + gb_s[pl.ds(row, pair), :].astype(F32) * h).astype(BF16)
        o = _dot(merged, wout_ref[...])
        y_ref[2 * p] = x_ref[2 * p] + o[:CHUNK]
        y_ref[2 * p + 1] = x_ref[2 * p + 1] + o[CHUNK:]
        return c

    lax.fori_loop(0, nseq // 2, out_body, 0)

    @pl.when(i == last)
    def _():
        conv_ref[...] = hist_s[:, SUBLANES - (CONV_W - 1):, :]


def _const_spec(shape):
    nd = len(shape)
    return pl.BlockSpec(shape, lambda i, *_, _n=nd: (0,) * _n, pipeline_mode=pl.Buffered(1))


def _layer_spec(stacked, layer):
    shape = stacked.shape[1:]
    nd = len(shape)
    return pl.BlockSpec((None,) + tuple(shape), lambda i, *_, _n=nd: (layer,) + (0,) * _n,
                        pipeline_mode=pl.Buffered(1))


MIXER_WEIGHTS = ('norm1', 'w_in', 'ln_g', 'ln_b', 'ws', 'bs', 'conv_w', 'conv_b', 'lam', 'wbd',
                 'b_a', 'b_x', 'w_out')


def _mixer_prompt(x, wts, layer):
    nseq, seq, d = x.shape
    assert seq % CHUNK == 0 and nseq == SUBLANES and d % 256 == 0
    n_chunks = seq // CHUNK
    nlt = d // LANES
    weights = tuple(wts[k] for k in MIXER_WEIGHTS)
    x_spec = pl.BlockSpec((nseq, CHUNK, d), lambda i: (0, i, 0))
    out_shape = (
        jax.ShapeDtypeStruct((nseq, seq, d), F32),
        jax.ShapeDtypeStruct((nseq, d), F32),
        jax.ShapeDtypeStruct((nseq, CONV_W - 1, d), F32),
        jax.ShapeDtypeStruct((nseq, CHUNK, d), F32),
    )
    out_specs = (
        x_spec,
        _const_spec((nseq, d)),
        _const_spec((nseq, CONV_W - 1, d)),
        _const_spec((nseq, CHUNK, d)),
    )
    scratch = [
        pltpu.VMEM((nseq, SUBLANES, d), F32),
        pltpu.VMEM((nlt, CHUNK * nseq, LANES), F32),
        pltpu.VMEM((nlt, CHUNK * nseq, LANES), F32),
        pltpu.VMEM((nseq * CHUNK, d), BF16),
        pltpu.VMEM((nseq * CHUNK, d), BF16),
        pltpu.VMEM((nseq, d), F32),
    ]
    return pl.pallas_call(
        _mixer_prompt_kernel,
        grid=(n_chunks,),
        in_specs=[x_spec] + [_layer_spec(w, layer) for w in weights],
        out_specs=out_specs,
        out_shape=out_shape,
        scratch_shapes=scratch,
        compiler_params=pltpu.CompilerParams(
            dimension_semantics=("arbitrary",), vmem_limit_bytes=VMEM_LIMIT),
        name="mixer_prompt",
    )(x, *weights)


def _mixer_sample_kernel(x_ref, h0_ref, cst_ref, norm_ref, win_ref, lng_ref, lnb_ref, wsc_ref,
                         bsc_ref, cw_ref, cb_ref, lam_ref, wbd_ref, ba_ref, bx_ref, wout_ref,
                         y_ref, h_ref, conv_ref, v_ref):
    nt, ns, d = x_ref.shape
    lam_half = (-0.5 * LRU_C) * _softplus(-lam_ref[...])
    g1 = norm_ref[...]
    h = h0_ref[...]
    xp = [cst_ref[k] for k in range(CONV_W - 1)]
    for t in range(nt):
        x = x_ref[t]
        xnb = _rms_norm(x, g1).astype(BF16)
        v = _layer_norm(_gelu2(_dot(xnb, win_ref[:, d:2 * d])), lng_ref[...], lnb_ref[...],
                        4.0 * EPS)
        v_ref[t] = v
        z = bsc_ref[t:t + 1, :]
        for s in range(t + 1):
            z = z + wsc_ref[t, s:s + 1, :] * v_ref[s]
        ya = _gelu2(_dot(xnb, win_ref[:, 0:d])) * z
        ga2 = 1.0 + jnp.tanh(_dot(xnb, win_ref[:, 3 * d:4 * d]))
        xr = _dot(xnb, win_ref[:, 2 * d:3 * d])
        xp.append(xr)
        xc = cb_ref[...]
        for k in range(CONV_W):
            xc = xc + xp[t + k] * cw_ref[k:k + 1, :]
        a, mult_half, ix2 = _lru_gates(xc, wbd_ref, ba_ref[...], bx_ref[...], lam_half)
        h = a * h + mult_half * ix2
        gb2 = 1.0 + jnp.tanh(_dot(xnb, win_ref[:, 4 * d:5 * d]))
        merged = (ga2 * ya + gb2 * h).astype(BF16)
        y_ref[t] = x + _dot(merged, wout_ref[...])
    h_ref[...] = h
    for k in range(CONV_W - 1):
        conv_ref[k] = xp[nt + k]


def _mixer_sample(x, h0, conv_state, wts, layer):
    nt, ns, d = x.shape
    names = MIXER_WEIGHTS[:4] + ('wsc', 'bsc') + MIXER_WEIGHTS[6:]
    stacked = (h0, conv_state) + tuple(wts[k] for k in names)
    out_shape = (
        jax.ShapeDtypeStruct((nt, ns, d), F32),
        jax.ShapeDtypeStruct((ns, d), F32),
        jax.ShapeDtypeStruct((CONV_W - 1, ns, d), F32),
        jax.ShapeDtypeStruct((nt, ns, d), F32),
    )
    return pl.pallas_call(
        _mixer_sample_kernel,
        grid=(1,),
        in_specs=[_const_spec(x.shape)] + [_layer_spec(w, layer) for w in stacked],
        out_specs=tuple(_const_spec(s.shape) for s in out_shape),
        out_shape=out_shape,
        compiler_params=pltpu.CompilerParams(
            dimension_semantics=("arbitrary",), vmem_limit_bytes=VMEM_LIMIT),
        name="mixer_sample",
    )(x, *stacked)


def _route(logits):
    lane = lax.broadcasted_iota(jnp.int32, logits.shape, 1).astype(F32)
    neg = jnp.float32(-jnp.inf)
    big = jnp.float32(ROUTE_LANES)

    def argmax_first(vals):
        m = jnp.max(vals, axis=-1, keepdims=True)
        idx = jnp.min(jnp.where(vals == m, lane, big), axis=-1, keepdims=True)
        return m, idx

    gl = jnp.where(lane < N_GROUPS, logits, neg)
    gm, gi = argmax_first(gl)
    pg_top = 1.0 / jnp.sum(jnp.exp(gl - gm), axis=-1, keepdims=True)
    lo = N_GROUPS + gi * E_PER_GROUP
    el = jnp.where(jnp.logical_and(lane >= lo, lane < lo + E_PER_GROUP), logits, neg)
    m1, i1 = argmax_first(el)
    m2, i2 = argmax_first(jnp.where(lane == i1, neg, el))
    e2 = jnp.exp(m2 - m1)
    w1 = pg_top / (1.0 + e2)
    w2 = pg_top * e2 / (1.0 + e2)
    return i1, i2, w1, w2


def _lane_cumsum_exclusive(x):
    lane = lax.broadcasted_iota(jnp.int32, x.shape, 1)
    acc = x
    shift = 1
    while shift < LANES:
        acc = acc + jnp.where(lane >= shift, pltpu.roll(acc, shift, axis=1), 0.0)
        shift *= 2
    return acc - x


def _to_slab(ref, val):
    t, d = val.shape
    nlt = d // LANES
    for j in range(nlt):
        ref[pl.ds(j, t, stride=nlt), :] = val[:, j * LANES:(j + 1) * LANES]


def _from_slab(ref, t, nlt):
    return jnp.concatenate([ref[pl.ds(j, t, stride=nlt), :] for j in range(nlt)], axis=-1)


def _tokens_from_slab(slab_ref, nlt):
    return slab_ref[...].reshape(slab_ref.shape[0] // nlt, nlt, LANES).astype(BF16)


def _tokens_to_slab(slab_ref, tokens):
    slab_ref[...] = tokens.astype(F32).reshape(slab_ref.shape)


def _router_kernel(yp_ref, ys_ref, norm_ref, wr_ref, br_ref, tri_ref, xn_ref, mf_ref, mi_ref,
                   cnt_ref, slab_s, *, n_prompt_tiles):
    y = jnp.where(pl.program_id(0) < n_prompt_tiles, yp_ref[...], ys_ref[...])
    xn = _rms_norm(y, norm_ref[...])
    _to_slab(slab_s, xn)
    xn_ref[...] = _tokens_from_slab(slab_s, xn_ref.shape[1])
    logits = _dot(xn.astype(BF16), wr_ref[...]) + br_ref[...]
    i1, i2, w1, w2 = _route(logits)
    lane = lax.broadcasted_iota(jnp.int32, logits.shape, 1).astype(F32)
    hot1 = lane == i1
    hot2 = lane == i2
    onehot = jnp.where(jnp.logical_or(hot1, hot2), 1.0, 0.0)
    before = _dot(tri_ref[...], onehot.astype(BF16))
    cnt = jnp.sum(onehot, axis=0, keepdims=True)
    cnt8 = jnp.broadcast_to(cnt, (SUBLANES, ROUTE_LANES))
    seg_lo = _lane_cumsum_exclusive(cnt8)[0:1, :]
    slot = before + seg_lo
    p1 = jnp.sum(jnp.where(hot1, slot, 0.0), axis=-1, keepdims=True)
    p2 = jnp.sum(jnp.where(hot2, slot, 0.0), axis=-1, keepdims=True)
    cols = (i1 - N_GROUPS, i2 - N_GROUPS, w1, w2, p1, p2)
    sel = jnp.zeros_like(logits)
    for k, c in enumerate(cols):
        sel = jnp.where(lane == k, c, sel)
    meta = sel.T[:SUBLANES]
    mf_ref[0] = meta
    mi_ref[0] = meta.astype(jnp.int32)
    cnt_ref[0] = cnt8


def _token_specs(tile, d, n_prompt_tiles):
    last = n_prompt_tiles - 1
    return (pl.BlockSpec((tile, d), lambda i, *_: (jnp.minimum(i, last), 0)),
            pl.BlockSpec((tile, d), lambda i, *_: (jnp.maximum(i - n_prompt_tiles, 0), 0)))


def _router(yp, ys, wts, layer, tile):
    d = yp.shape[1]
    nlt = d // LANES
    n_prompt_tiles = yp.shape[0] // tile
    n_tiles = n_prompt_tiles + ys.shape[0] // tile
    tri = jnp.tril(jnp.ones((tile, tile), BF16), -1)
    meta_spec = pl.BlockSpec((1, SUBLANES, tile), lambda i: (i, 0, 0))
    return pl.pallas_call(
        functools.partial(_router_kernel, n_prompt_tiles=n_prompt_tiles),
        grid=(n_tiles,),
        in_specs=list(_token_specs(tile, d, n_prompt_tiles)) + [
            _layer_spec(wts['norm2'], layer), _layer_spec(wts['w_route'], layer),
            _layer_spec(wts['b_route'], layer), _const_spec((tile, tile))],
        out_specs=(pl.BlockSpec((tile, nlt, LANES), lambda i: (i, 0, 0)), meta_spec, meta_spec,
                   pl.BlockSpec((1, SUBLANES, ROUTE_LANES), lambda i: (i, 0, 0))),
        out_shape=(jax.ShapeDtypeStruct((n_tiles * tile, nlt, LANES), BF16),
                   jax.ShapeDtypeStruct((n_tiles, SUBLANES, tile), F32),
                   jax.ShapeDtypeStruct((n_tiles, SUBLANES, tile), jnp.int32),
                   jax.ShapeDtypeStruct((n_tiles, SUBLANES, ROUTE_LANES), F32)),
        scratch_shapes=[pltpu.VMEM((tile * nlt, LANES), F32)],
        compiler_params=pltpu.CompilerParams(
            dimension_semantics=("arbitrary",), vmem_limit_bytes=VMEM_LIMIT),
        name="moe_router",
    )(yp, ys, wts['norm2'], wts['w_route'], wts['b_route'], tri)


def _run_copy(tile_idx, e, lo_ref, cnt_ref, dst_ref, local, remote, sem, to_remote):
    k = tile_idx * N_EXPERTS + e
    loc = local.at[pl.ds(lo_ref[k], cnt_ref[k])]
    rem = remote.at[pl.ds(dst_ref[k], cnt_ref[k])]
    return pltpu.make_async_copy(loc, rem, sem) if to_remote else pltpu.make_async_copy(rem, loc, sem)


def _start_runs(tile_idx, lo_ref, cnt_ref, dst_ref, local, remote, sem, to_remote):
    def body(e, c):
        @pl.when(cnt_ref[tile_idx * N_EXPERTS + e] > 0)
        def _():
            _run_copy(tile_idx, e, lo_ref, cnt_ref, dst_ref, local, remote, sem, to_remote).start()
        return c
    lax.fori_loop(0, N_EXPERTS, body, 0)


def _wait_runs(local, sem):
    pltpu.make_async_copy(local, local, sem).wait()


def _dispatch_kernel(lo_ref, cnt_ref, dst_ref, padst_ref, padn_ref,
                     s1_ref, s2_ref, x_ref, xs_ref, loc, zeros, sem, *, unroll):
    i = pl.program_id(0)
    last = pl.num_programs(0) - 1
    tile = x_ref.shape[0]
    slot = i % 2
    buf = loc.at[slot]

    @pl.when(i >= 2)
    def _():
        _wait_runs(buf, sem.at[slot])

    def body(tt, c):
        for u in range(unroll):
            t = tt * unroll + u
            row = x_ref[t]
            buf[s1_ref[t]] = row
            buf[s2_ref[t]] = row
        return c

    lax.fori_loop(0, tile // unroll, body, 0)
    _start_runs(i, lo_ref, cnt_ref, dst_ref, buf, xs_ref, sem.at[slot], True)

    @pl.when(i == last)
    def _():
        zeros[...] = jnp.zeros_like(zeros)

        def pad_copy(e):
            n = padn_ref[e]
            dst = xs_ref.at[pl.ds(padst_ref[e], n)]
            return pltpu.make_async_copy(zeros.at[pl.ds(0, n)], dst, sem.at[2])

        def each_pad(fn):
            def b(e, c):
                @pl.when(padn_ref[e] > 0)
                def _():
                    fn(e)
                return c
            lax.fori_loop(0, padn_ref.shape[0], b, 0)

        each_pad(lambda e: pad_copy(e).start())
        _wait_runs(buf, sem.at[slot])

        @pl.when(i >= 1)
        def _():
            _wait_runs(loc.at[1 - slot], sem.at[1 - slot])

        each_pad(lambda e: pad_copy(e).wait())


def _smem_tile(tile):
    return pl.BlockSpec((tile,), lambda i, *_: (i,), memory_space=pltpu.SMEM)


def _dispatch(xn_slab, s1, s2, tables, tile, n_rows):
    token = xn_slab.shape[1:]
    return pl.pallas_call(
        functools.partial(_dispatch_kernel, unroll=8),
        grid_spec=pltpu.PrefetchScalarGridSpec(
            num_scalar_prefetch=5,
            grid=(xn_slab.shape[0] // tile,),
            in_specs=[_smem_tile(tile), _smem_tile(tile),
                      pl.BlockSpec((tile,) + token, lambda i, *_: (i, 0, 0))],
            out_specs=pl.BlockSpec(memory_space=pl.ANY),
            scratch_shapes=[pltpu.VMEM((2, 2 * tile) + token, xn_slab.dtype),
                            pltpu.VMEM((MOE_BLOCK,) + token, xn_slab.dtype),
                            pltpu.SemaphoreType.DMA((3,))]),
        out_shape=jax.ShapeDtypeStruct((n_rows,) + token, xn_slab.dtype),
        compiler_params=pltpu.CompilerParams(
            dimension_semantics=("arbitrary",), vmem_limit_bytes=VMEM_LIMIT),
        name="moe_dispatch",
    )(*tables, s1, s2, xn_slab)


def _experts_kernel(blk_ref, nused_ref, xs_ref, wg_ref, wu_ref, wd_ref, ys_ref,
                    wg_s, wu_s, wd_s, slab_s):
    b = pl.program_id(0)
    rows, nlt, _ = xs_ref.shape

    @pl.when(jnp.logical_or(b == 0, blk_ref[b] != blk_ref[jnp.maximum(b - 1, 0)]))
    def _():
        wg_s[...] = wg_ref[...].astype(BF16)
        wu_s[...] = wu_ref[...].astype(BF16)
        wd_s[...] = wd_ref[...].astype(BF16)

    @pl.when(b < nused_ref[0])
    def _():
        _tokens_to_slab(slab_s, xs_ref[...])
        x = _from_slab(slab_s, rows, nlt).astype(BF16)
        hg = _dot(x, wg_s[...])
        hu = _dot(x, wu_s[...])
        h = (hg * _sigmoid(hg) * hu).astype(BF16)
        _to_slab(slab_s, _dot(h, wd_s[...]))
        ys_ref[...] = _tokens_from_slab(slab_s, nlt)

    @pl.when(b >= nused_ref[0])
    def _():
        ys_ref[...] = jnp.zeros_like(ys_ref)


def _experts(xs, blk_expert, n_used, wts, layer):
    n_rows, nlt, _ = xs.shape
    d = nlt * LANES
    f = wts['w_gate'].shape[-1]
    rows = pl.BlockSpec((MOE_BLOCK, nlt, LANES), lambda b, blk, nu: (b, 0, 0))
    w_spec = lambda r, c: pl.BlockSpec((None, None, r, c), lambda b, blk, nu: (layer, blk[b], 0, 0))
    return pl.pallas_call(
        _experts_kernel,
        grid_spec=pltpu.PrefetchScalarGridSpec(
            num_scalar_prefetch=2,
            grid=(n_rows // MOE_BLOCK,),
            in_specs=[rows, w_spec(d, f), w_spec(d, f), w_spec(f, d)],
            out_specs=rows,
            scratch_shapes=[pltpu.VMEM((d, f), BF16), pltpu.VMEM((d, f), BF16),
                            pltpu.VMEM((f, d), BF16),
                            pltpu.VMEM((MOE_BLOCK * nlt, LANES), F32)]),
        out_shape=jax.ShapeDtypeStruct(xs.shape, xs.dtype),
        compiler_params=pltpu.CompilerParams(
            dimension_semantics=("arbitrary",), vmem_limit_bytes=VMEM_LIMIT),
        name="moe_experts",
    )(blk_expert, n_used, xs, wts['w_gate'], wts['w_up'], wts['w_down'])


def _combine_kernel(lo_ref, cnt_ref, dst_ref, s1_ref, s2_ref, w1_ref, w2_ref, yp_ref, ys_ref,
                    nf_ref, rows_ref, op_ref, os_ref, loc, acc, sem,
                    *, unroll, final_norm, n_prompt_tiles):
    i = pl.program_id(0)
    tile = yp_ref.shape[0]
    nlt = loc.shape[2]
    slot = i % 2
    buf = loc.at[slot]
    fetch = lambda t, s: _start_runs(t, lo_ref, cnt_ref, dst_ref, loc.at[s], rows_ref, sem.at[s],
                                     False)

    @pl.when(i == 0)
    def _():
        fetch(0, 0)

    @pl.when(i + 1 < pl.num_programs(0))
    def _():
        fetch(i + 1, 1 - slot)

    _wait_runs(buf, sem.at[slot])

    def body(tt, c):
        for u in range(unroll):
            t = tt * unroll + u
            r1 = buf[s1_ref[t]].astype(F32)
            r2 = buf[s2_ref[t]].astype(F32)
            acc[pl.ds(pl.multiple_of(t * nlt, nlt), nlt), :] = w1_ref[t] * r1 + w2_ref[t] * r2
        return c

    lax.fori_loop(0, tile // unroll, body, 0)
    is_prompt = i < n_prompt_tiles
    out = jnp.where(is_prompt, yp_ref[...], ys_ref[...]) + _from_slab(acc, tile, nlt)
    if final_norm:
        out = _rms_norm(out, nf_ref[...])

    @pl.when(is_prompt)
    def _():
        op_ref[...] = out

    @pl.when(jnp.logical_not(is_prompt))
    def _():
        os_ref[...] = out


def _combine(yp, ys, rows, meta, tables, norm_f, final_norm, tile):
    d = yp.shape[1]
    token = rows.shape[1:]
    n_prompt_tiles = yp.shape[0] // tile
    n_tiles = n_prompt_tiles + ys.shape[0] // tile
    tok_p, tok_s = _token_specs(tile, d, n_prompt_tiles)
    return pl.pallas_call(
        functools.partial(_combine_kernel, unroll=8, final_norm=final_norm,
                          n_prompt_tiles=n_prompt_tiles),
        grid_spec=pltpu.PrefetchScalarGridSpec(
            num_scalar_prefetch=3,
            grid=(n_tiles,),
            in_specs=[_smem_tile(tile)] * 4 + [tok_p, tok_s, _const_spec((1, d)),
                                               pl.BlockSpec(memory_space=pl.ANY)],
            out_specs=(tok_p, tok_s),
            scratch_shapes=[pltpu.VMEM((2, 2 * tile) + token, rows.dtype),
                            pltpu.VMEM((tile * token[0], LANES), F32),
                            pltpu.SemaphoreType.DMA((2,))]),
        out_shape=(jax.ShapeDtypeStruct(yp.shape, F32), jax.ShapeDtypeStruct(ys.shape, F32)),
        compiler_params=pltpu.CompilerParams(
            dimension_semantics=("arbitrary",), vmem_limit_bytes=VMEM_LIMIT),
        name="moe_combine",
    )(*tables, *meta, yp, ys, norm_f, rows)


def _moe(yp, ys, wts, layer, norm_f, final_norm, tile):
    d = yp.shape[1]
    n = yp.shape[0] + ys.shape[0]
    assert yp.shape[0] % tile == 0 and ys.shape[0] % tile == 0 and d % LANES == 0
    xn_slab, meta_f, meta_i, cnt = _router(yp, ys, wts, layer, tile)
    cnt = cnt[:, 0, N_GROUPS:N_GROUPS + N_EXPERTS].astype(jnp.int32)
    lo = jnp.cumsum(cnt, axis=1) - cnt
    seg = jnp.sum(cnt, axis=0)
    seg_pad = (seg + MOE_BLOCK - 1) // MOE_BLOCK * MOE_BLOCK
    seg_end = jnp.cumsum(seg_pad)
    seg_start = seg_end - seg_pad
    dst = seg_start[None, :] + jnp.cumsum(cnt, axis=0) - cnt
    n_blocks = (2 * n + N_EXPERTS * (MOE_BLOCK - 1)) // MOE_BLOCK
    blk_ids = jnp.arange(n_blocks, dtype=jnp.int32)
    blk_expert = jnp.minimum(
        jnp.sum((seg_end[None, :] <= blk_ids[:, None] * MOE_BLOCK).astype(jnp.int32), axis=1),
        N_EXPERTS - 1)
    n_used = seg_end[-1:] // MOE_BLOCK
    pad_start = jnp.concatenate([seg_start + seg, blk_ids * MOE_BLOCK])
    pad_rows = jnp.concatenate([seg_pad - seg, jnp.where(blk_ids >= n_used[0], MOE_BLOCK, 0)])
    flat = lambda a: a.reshape(-1)
    runs = (flat(lo), flat(cnt), flat(dst))
    s1, s2 = flat(meta_i[:, 4]), flat(meta_i[:, 5])
    w1, w2 = flat(meta_f[:, 2]), flat(meta_f[:, 3])
    xs = _dispatch(xn_slab, s1, s2, runs + (pad_start, pad_rows), tile, n_blocks * MOE_BLOCK)
    rows = _experts(xs, blk_expert, n_used, wts, layer)
    return _combine(yp, ys, rows, (s1, s2, w1, w2), runs, norm_f, final_norm, tile)


def _prep_weights(norm1, w_in, ln_g, ln_b, w_s, b_s, conv_w, conv_b, lru_lambda, w_rg_a, b_rg_a,
                  w_rg_x, b_rg_x, w_out, norm2, w_route_group, b_route_group, w_route_expert,
                  b_route_expert, w_gate, w_up, w_down, n_sample_t):
    depth, d = w_in.shape[0], w_in.shape[1]
    dh = d // H_A
    row = lambda a: a.reshape(depth, 1, -1).astype(F32)
    causal = jnp.tril(jnp.ones((CHUNK, CHUNK), dtype=bool))
    ws = jnp.where(causal, 0.5 * w_s, 0.0)
    bs = jnp.repeat(jnp.swapaxes(0.5 * b_s, 1, 2), dh, axis=2)
    in_scale = jnp.concatenate([jnp.ones((3 * d,), F32), jnp.full((2 * d,), 0.5, F32)])

    def pair_bd(w):
        w = w.reshape(depth, H_B // 2, 2, w.shape[-2], w.shape[-1])
        z = jnp.zeros_like(w[:, :, 0])
        return jnp.concatenate([jnp.concatenate([w[:, :, 0], z], axis=-1),
                                jnp.concatenate([z, w[:, :, 1]], axis=-1)], axis=-2)

    pad = ROUTE_LANES - N_GROUPS - N_EXPERTS
    w_route = jnp.concatenate(
        [w_route_group, w_route_expert, jnp.zeros((depth, d, pad), F32)], axis=-1)
    b_route = jnp.concatenate(
        [b_route_group, b_route_expert, jnp.zeros((depth, pad), F32)], axis=-1)
    nt = n_sample_t
    return dict(
        norm1=row(norm1), w_in=(w_in * in_scale).astype(BF16), ln_g=row(ln_g), ln_b=row(ln_b),
        ws=ws.astype(BF16), bs=bs.astype(F32), conv_w=conv_w.astype(F32), conv_b=row(conv_b),
        lam=row(lru_lambda),
        wbd=(0.5 * jnp.concatenate([pair_bd(w_rg_a), pair_bd(w_rg_x)], axis=-1)).astype(BF16),
        b_a=row(0.5 * b_rg_a), b_x=row(0.5 * b_rg_x), w_out=(0.5 * w_out).astype(BF16),
        norm2=row(norm2), w_route=w_route.astype(BF16), b_route=row(b_route),
        w_gate=w_gate, w_up=w_up, w_down=w_down,
        wsc=jnp.repeat(jnp.transpose(ws[:, :, :nt, :nt], (0, 2, 3, 1)), dh, axis=3).astype(F32),
        bsc=bs[:, :nt].astype(F32),
    )


def kernel(x_prompt, x_sample, state_lru_h, state_conv, norm1, w_in, ln_g, ln_b, w_s, b_s, conv_w, conv_b, lru_lambda, w_rg_a, b_rg_a, w_rg_x, b_rg_x, w_out, norm2, w_route_group, b_route_group, w_route_expert, b_route_expert, w_gate, w_up, w_down, norm_f):
    depth = w_in.shape[0]
    nb, seq, d = x_prompt.shape
    ns, nt, _ = x_sample.shape
    assert nt <= CHUNK
    nf = norm_f.reshape(1, d).astype(F32)
    wts = _prep_weights(norm1, w_in, ln_g, ln_b, w_s, b_s, conv_w, conv_b, lru_lambda, w_rg_a,
                        b_rg_a, w_rg_x, b_rg_x, w_out, norm2, w_route_group, b_route_group,
                        w_route_expert, b_route_expert, w_gate, w_up, w_down, nt)
    conv_state = jnp.transpose(state_conv, (0, 2, 1, 3))
    tile = min(MOE_TILE, nt * ns)
    yp = x_prompt
    ys = jnp.transpose(x_sample, (1, 0, 2))
    hp, cp, vp, hs, cs, vs = [], [], [], [], [], []
    for l in range(depth):
        yp, h, c, v = _mixer_prompt(yp, wts, l)
        hp.append(h)
        cp.append(c)
        vp.append(v)
        ys, h, c, v = _mixer_sample(ys, state_lru_h, conv_state, wts, l)
        hs.append(h)
        cs.append(c)
        vs.append(v)
        yp, ys = _moe(yp.reshape(nb * seq, d), ys.reshape(nt * ns, d), wts, l, nf,
                      l == depth - 1, tile)
        yp = yp.reshape(nb, seq, d)
        ys = ys.reshape(nt, ns, d)
    to_seq_major = lambda a: jnp.transpose(jnp.stack(a), (0, 2, 1, 3))
    return (yp, jnp.transpose(ys, (1, 0, 2)), jnp.stack(hp), jnp.stack(cp), jnp.stack(vp),
            jnp.stack(hs), to_seq_major(cs), to_seq_major(vs))
```

```python
import functools

import jax
import jax.numpy as jnp
from jax import lax
from jax.experimental import pallas as pl
from jax.experimental.pallas import tpu as pltpu

F32 = jnp.float32
BF16 = jnp.bfloat16

CHUNK = 128
H_A = 4
H_B = 8
CONV_W = 4
LRU_C = 8.0
N_GROUPS = 4
E_PER_GROUP = 8
N_EXPERTS = N_GROUPS * E_PER_GROUP
EPS = 1e-6

LANES = 128
SUBLANES = 8
SCAN_UNROLL = 4
SEQS_PER_TRIP = 4
ROUTE_LANES = 128
MOE_TILE = 512
MOE_BLOCK = 512
VMEM_LIMIT = 60 * 1024 * 1024


def _dot(a, b):
    return jnp.dot(a, b, preferred_element_type=F32)


def _sigmoid(x):
    return 0.5 * (jnp.tanh(0.5 * x) + 1.0)


def _gelu2(x):
    c = 0.7978845608028654
    return x * (1.0 + jnp.tanh(x * (c + (c * 0.044715) * (x * x))))


def _rms_norm(x, g):
    return x * lax.rsqrt(jnp.mean(x * x, axis=-1, keepdims=True) + EPS) * g


def _layer_norm(x, g, b, eps):
    xc = x - jnp.mean(x, axis=-1, keepdims=True)
    var = jnp.mean(xc * xc, axis=-1, keepdims=True)
    return xc * lax.rsqrt(var + eps) * g + b


def _softplus(x):
    return jnp.maximum(x, 0.0) + jnp.log(1.0 + jnp.exp(-jnp.abs(x)))


def _lru_gates(xc, wbd_ref, b_a, b_x, lam_half):
    d = xc.shape[-1]
    xcb = xc.astype(BF16)
    r_parts, i_parts = [], []
    for q in range(d // 256):
        ri = _dot(xcb[:, q * 256:(q + 1) * 256], wbd_ref[q])
        r_parts.append(ri[:, :256])
        i_parts.append(ri[:, 256:])
    r2 = 1.0 + jnp.tanh(jnp.concatenate(r_parts, axis=-1) + b_a)
    i2 = 1.0 + jnp.tanh(jnp.concatenate(i_parts, axis=-1) + b_x)
    a = jnp.exp(lam_half * r2)
    mult_half = jnp.sqrt(0.25 - 0.25 * (a * a))
    return a, mult_half, i2 * xc


def _shift_rows(x, hist, j):
    sh = pltpu.roll(x, j, axis=0)
    top = jnp.where(lax.broadcasted_iota(jnp.int32, (SUBLANES, 1), 0) < j,
                    pltpu.roll(hist, j, axis=0), sh[:SUBLANES])
    return jnp.concatenate([top, sh[SUBLANES:]], axis=0)


def _mixer_prompt_kernel(x_ref, norm_ref, win_ref, lng_ref, lnb_ref, ws_ref, bs_ref,
                         cw_ref, cb_ref, lam_ref, wbd_ref, ba_ref, bx_ref, wout_ref,
                         y_ref, h_ref, conv_ref, v_ref,
                         hist_s, a_s, b_s, ya_s, gb_s, hc_s):
    i = pl.program_id(0)
    last = pl.num_programs(0) - 1
    nseq, _, d = x_ref.shape
    nlt = d // LANES
    spt = SEQS_PER_TRIP
    pair = spt * CHUNK

    @pl.when(i == 0)
    def _():
        hist_s[...] = jnp.zeros_like(hist_s)
        hc_s[...] = jnp.zeros_like(hc_s)

    lam_half = (-0.5 * LRU_C) * _softplus(-lam_ref[...])
    is_first = i == 0

    def branch_body(p, c):
        row = pl.multiple_of(p * pair, pair)
        x = jnp.concatenate([x_ref[spt * p + k] for k in range(spt)], axis=0)
        xnb = _rms_norm(x, norm_ref[...]).astype(BF16)
        v = _layer_norm(_gelu2(_dot(xnb, win_ref[:, d:2 * d])), lng_ref[...], lnb_ref[...],
                        4.0 * EPS)

        @pl.when(i == last)
        def _():
            for k in range(spt):
                v_ref[spt * p + k] = v[k * CHUNK:(k + 1) * CHUNK]

        vb = v.astype(BF16)
        dh = d // H_A
        z_rows = []
        for k in range(spt):
            zs = [_dot(ws_ref[h], vb[k * CHUNK:(k + 1) * CHUNK, h * dh:(h + 1) * dh])
                  for h in range(H_A)]
            z_rows.append(jnp.concatenate(zs, axis=-1) + bs_ref[...])
        z = jnp.concatenate(z_rows, axis=0)
        ya = _gelu2(_dot(xnb, win_ref[:, 0:d])) * z
        ga2 = 1.0 + jnp.tanh(_dot(xnb, win_ref[:, 3 * d:4 * d]))
        ya_s[pl.ds(row, pair), :] = (ga2 * ya).astype(BF16)
        gb_s[pl.ds(row, pair), :] = (1.0 + jnp.tanh(_dot(xnb, win_ref[:, 4 * d:5 * d]))).astype(BF16)
        xr = _dot(xnb, win_ref[:, 2 * d:3 * d])
        xcs = []
        for k in range(spt):
            s = spt * p + k
            xk = xr[k * CHUNK:(k + 1) * CHUNK]
            hist = hist_s[s]
            xc = cb_ref[...] + xk * cw_ref[CONV_W - 1:CONV_W, :]
            for j in range(1, CONV_W):
                xc = xc + _shift_rows(xk, hist, j) * cw_ref[CONV_W - 1 - j:CONV_W - j, :]
            hist_s[s] = xk[CHUNK - SUBLANES:]
            xcs.append(xc)
        xc = jnp.concatenate(xcs, axis=0)
        a, mult_half, ix2 = _lru_gates(xc, wbd_ref, ba_ref[...], bx_ref[...], lam_half)
        bt = mult_half * ix2
        for k in range(spt):
            s = spt * p + k
            for j in range(nlt):
                a_s[j, pl.ds(s, CHUNK, stride=nseq), :] = a[k * CHUNK:(k + 1) * CHUNK, j * LANES:(j + 1) * LANES]
                b_s[j, pl.ds(s, CHUNK, stride=nseq), :] = bt[k * CHUNK:(k + 1) * CHUNK, j * LANES:(j + 1) * LANES]

        @pl.when(is_first)
        def _():
            for k in range(spt):
                first = 0.5 * ix2[k * CHUNK:k * CHUNK + 1]
                for j in range(nlt):
                    b_s[j, pl.ds(spt * p + k, 1), :] = first[:, j * LANES:(j + 1) * LANES]

        return c

    lax.fori_loop(0, nseq // spt, branch_body, 0)

    def scan_body(tt, hs):
        for u in range(SCAN_UNROLL):
            row = pl.multiple_of((tt * SCAN_UNROLL + u) * nseq, nseq)
            out = []
            for j in range(nlt):
                h = a_s[j, pl.ds(row, nseq), :] * hs[j] + b_s[j, pl.ds(row, nseq), :]
                b_s[j, pl.ds(row, nseq), :] = h
                out.append(h)
            hs = tuple(out)
        return hs

    h0 = tuple(hc_s[:, j * LANES:(j + 1) * LANES] for j in range(nlt))
    hs = lax.fori_loop(0, CHUNK // SCAN_UNROLL, scan_body, h0)
    hfin = jnp.concatenate(hs, axis=-1)
    hc_s[...] = hfin
    h_ref[...] = hfin

    def out_body(p, c):
        row = pl.multiple_of(p * pair, pair)
        hrows = []
        for k in range(spt):
            hrows.append(jnp.concatenate(
                [b_s[j, pl.ds(spt * p + k, CHUNK, stride=nseq), :] for j in range(nlt)], axis=-1))
        h = jnp.concatenate(hrows, axis=0)
        merged = (ya_s[pl.ds(row, pair), :].astype(F32)
                  + gb_s[pl.ds(row, pair), :].astype(F32) * h).astype(BF16)
        o = _dot(merged, wout_ref[...])
        for k in range(spt):
            y_ref[spt * p + k] = x_ref[spt * p + k] + o[k * CHUNK:(k + 1) * CHUNK]
        return c

    lax.fori_loop(0, nseq // spt, out_body, 0)

    @pl.when(i == last)
    def _():
        conv_ref[...] = hist_s[:, SUBLANES - (CONV_W - 1):, :]


def _const_spec(shape):
    nd = len(shape)
    return pl.BlockSpec(shape, lambda i, *_, _n=nd: (0,) * _n, pipeline_mode=pl.Buffered(1))


def _layer_spec(stacked, layer):
    shape = stacked.shape[1:]
    nd = len(shape)
    return pl.BlockSpec((None,) + tuple(shape), lambda i, *_, _n=nd: (layer,) + (0,) * _n,
                        pipeline_mode=pl.Buffered(1))


MIXER_WEIGHTS = ('norm1', 'w_in', 'ln_g', 'ln_b', 'ws', 'bs', 'conv_w', 'conv_b', 'lam', 'wbd',
                 'b_a', 'b_x', 'w_out')


def _mixer_prompt(x, wts, layer):
    nseq, seq, d = x.shape
    assert seq % CHUNK == 0 and nseq == SUBLANES and d % 256 == 0
    n_chunks = seq // CHUNK
    nlt = d // LANES
    weights = tuple(wts[k] for k in MIXER_WEIGHTS)
    x_spec = pl.BlockSpec((nseq, CHUNK, d), lambda i: (0, i, 0))
    out_shape = (
        jax.ShapeDtypeStruct((nseq, seq, d), F32),
        jax.ShapeDtypeStruct((nseq, d), F32),
        jax.ShapeDtypeStruct((nseq, CONV_W - 1, d), F32),
        jax.ShapeDtypeStruct((nseq, CHUNK, d), F32),
    )
    out_specs = (
        x_spec,
        _const_spec((nseq, d)),
        _const_spec((nseq, CONV_W - 1, d)),
        _const_spec((nseq, CHUNK, d)),
    )
    scratch = [
        pltpu.VMEM((nseq, SUBLANES, d), F32),
        pltpu.VMEM((nlt, CHUNK * nseq, LANES), F32),
        pltpu.VMEM((nlt, CHUNK * nseq, LANES), F32),
        pltpu.VMEM((nseq * CHUNK, d), BF16),
        pltpu.VMEM((nseq * CHUNK, d), BF16),
        pltpu.VMEM((nseq, d), F32),
    ]
    return pl.pallas_call(
        _mixer_prompt_kernel,
        grid=(n_chunks,),
        in_specs=[x_spec] + [_layer_spec(w, layer) for w in weights],
        out_specs=out_specs,
        out_shape=out_shape,
        scratch_shapes=scratch,
        compiler_params=pltpu.CompilerParams(
            dimension_semantics=("arbitrary",), vmem_limit_bytes=VMEM_LIMIT),
        name="mixer_prompt",
    )(x, *weights)


def _mixer_sample_kernel(x_ref, h0_ref, cst_ref, norm_ref, win_ref, lng_ref, lnb_ref, wsc_ref,
                         bsc_ref, cw_ref, cb_ref, lam_ref, wbd_ref, ba_ref, bx_ref, wout_ref,
                         y_ref, h_ref, conv_ref, v_ref):
    nt, ns, d = x_ref.shape
    lam_half = (-0.5 * LRU_C) * _softplus(-lam_ref[...])
    g1 = norm_ref[...]
    h = h0_ref[...]
    xp = [cst_ref[k] for k in range(CONV_W - 1)]
    for t in range(nt):
        x = x_ref[t]
        xnb = _rms_norm(x, g1).astype(BF16)
        v = _layer_norm(_gelu2(_dot(xnb, win_ref[:, d:2 * d])), lng_ref[...], lnb_ref[...],
                        4.0 * EPS)
        v_ref[t] = v
        z = bsc_ref[t:t + 1, :]
        for s in range(t + 1):
            z = z + wsc_ref[t, s:s + 1, :] * v_ref[s]
        ya = _gelu2(_dot(xnb, win_ref[:, 0:d])) * z
        ga2 = 1.0 + jnp.tanh(_dot(xnb, win_ref[:, 3 * d:4 * d]))
        xr = _dot(xnb, win_ref[:, 2 * d:3 * d])
        xp.append(xr)
        xc = cb_ref[...]
        for k in range(CONV_W):
            xc = xc + xp[t + k] * cw_ref[k:k + 1, :]
        a, mult_half, ix2 = _lru_gates(xc, wbd_ref, ba_ref[...], bx_ref[...], lam_half)
        h = a * h + mult_half * ix2
        gb2 = 1.0 + jnp.tanh(_dot(xnb, win_ref[:, 4 * d:5 * d]))
        merged = (ga2 * ya + gb2 * h).astype(BF16)
        y_ref[t] = x + _dot(merged, wout_ref[...])
    h_ref[...] = h
    for k in range(CONV_W - 1):
        conv_ref[k] = xp[nt + k]


def _mixer_sample(x, h0, conv_state, wts, layer):
    nt, ns, d = x.shape
    names = MIXER_WEIGHTS[:4] + ('wsc', 'bsc') + MIXER_WEIGHTS[6:]
    stacked = (h0, conv_state) + tuple(wts[k] for k in names)
    out_shape = (
        jax.ShapeDtypeStruct((nt, ns, d), F32),
        jax.ShapeDtypeStruct((ns, d), F32),
        jax.ShapeDtypeStruct((CONV_W - 1, ns, d), F32),
        jax.ShapeDtypeStruct((nt, ns, d), F32),
    )
    return pl.pallas_call(
        _mixer_sample_kernel,
        grid=(1,),
        in_specs=[_const_spec(x.shape)] + [_layer_spec(w, layer) for w in stacked],
        out_specs=tuple(_const_spec(s.shape) for s in out_shape),
        out_shape=out_shape,
        compiler_params=pltpu.CompilerParams(
            dimension_semantics=("arbitrary",), vmem_limit_bytes=VMEM_LIMIT),
        name="mixer_sample",
    )(x, *stacked)


def _route(logits):
    lane = lax.broadcasted_iota(jnp.int32, logits.shape, 1).astype(F32)
    neg = jnp.float32(-jnp.inf)
    big = jnp.float32(ROUTE_LANES)

    def argmax_first(vals):
        m = jnp.max(vals, axis=-1, keepdims=True)
        idx = jnp.min(jnp.where(vals == m, lane, big), axis=-1, keepdims=True)
        return m, idx

    gl = jnp.where(lane < N_GROUPS, logits, neg)
    gm, gi = argmax_first(gl)
    pg_top = 1.0 / jnp.sum(jnp.exp(gl - gm), axis=-1, keepdims=True)
    lo = N_GROUPS + gi * E_PER_GROUP
    el = jnp.where(jnp.logical_and(lane >= lo, lane < lo + E_PER_GROUP), logits, neg)
    m1, i1 = argmax_first(el)
    m2, i2 = argmax_first(jnp.where(lane == i1, neg, el))
    e2 = jnp.exp(m2 - m1)
    w1 = pg_top / (1.0 + e2)
    w2 = pg_top * e2 / (1.0 + e2)
    return i1, i2, w1, w2


def _lane_cumsum_exclusive(x):
    lane = lax.broadcasted_iota(jnp.int32, x.shape, 1)
    acc = x
    shift = 1
    while shift < LANES:
        acc = acc + jnp.where(lane >= shift, pltpu.roll(acc, shift, axis=1), 0.0)
        shift *= 2
    return acc - x


def _to_slab(ref, val):
    t, d = val.shape
    nlt = d // LANES
    for j in range(nlt):
        ref[pl.ds(j, t, stride=nlt), :] = val[:, j * LANES:(j + 1) * LANES]


def _from_slab(ref, t, nlt):
    return jnp.concatenate([ref[pl.ds(j, t, stride=nlt), :] for j in range(nlt)], axis=-1)


def _tokens_from_slab(slab_ref, nlt):
    return slab_ref[...].reshape(slab_ref.shape[0] // nlt, nlt, LANES).astype(BF16)


def _tokens_to_slab(slab_ref, tokens):
    slab_ref[...] = tokens.astype(F32).reshape(slab_ref.shape)


def _router_kernel(yp_ref, ys_ref, norm_ref, wr_ref, br_ref, tri_ref, xn_ref, mf_ref, mi_ref,
                   cnt_ref, slab_s, *, n_prompt_tiles):
    y = jnp.where(pl.program_id(0) < n_prompt_tiles, yp_ref[...], ys_ref[...])
    xn = _rms_norm(y, norm_ref[...])
    _to_slab(slab_s, xn)
    xn_ref[...] = _tokens_from_slab(slab_s, xn_ref.shape[1])
    logits = _dot(xn.astype(BF16), wr_ref[...]) + br_ref[...]
    i1, i2, w1, w2 = _route(logits)
    lane = lax.broadcasted_iota(jnp.int32, logits.shape, 1).astype(F32)
    hot1 = lane == i1
    hot2 = lane == i2
    onehot = jnp.where(jnp.logical_or(hot1, hot2), 1.0, 0.0)
    before = _dot(tri_ref[...], onehot.astype(BF16))
    cnt = jnp.sum(onehot, axis=0, keepdims=True)
    cnt8 = jnp.broadcast_to(cnt, (SUBLANES, ROUTE_LANES))
    seg_lo = _lane_cumsum_exclusive(cnt8)[0:1, :]
    slot = before + seg_lo
    p1 = jnp.sum(jnp.where(hot1, slot, 0.0), axis=-1, keepdims=True)
    p2 = jnp.sum(jnp.where(hot2, slot, 0.0), axis=-1, keepdims=True)
    cols = (i1 - N_GROUPS, i2 - N_GROUPS, w1, w2, p1, p2)
    sel = jnp.zeros_like(logits)
    for k, c in enumerate(cols):
        sel = jnp.where(lane == k, c, sel)
    meta = sel.T[:SUBLANES]
    mf_ref[0] = meta
    mi_ref[0] = meta.astype(jnp.int32)
    cnt_ref[0] = cnt8


def _token_specs(tile, d, n_prompt_tiles):
    last = n_prompt_tiles - 1
    return (pl.BlockSpec((tile, d), lambda i, *_: (jnp.minimum(i, last), 0)),
            pl.BlockSpec((tile, d), lambda i, *_: (jnp.maximum(i - n_prompt_tiles, 0), 0)))


def _router(yp, ys, wts, layer, tile):
    d = yp.shape[1]
    nlt = d // LANES
    n_prompt_tiles = yp.shape[0] // tile
    n_tiles = n_prompt_tiles + ys.shape[0] // tile
    tri = jnp.tril(jnp.ones((tile, tile), BF16), -1)
    meta_spec = pl.BlockSpec((1, SUBLANES, tile), lambda i: (i, 0, 0))
    return pl.pallas_call(
        functools.partial(_router_kernel, n_prompt_tiles=n_prompt_tiles),
        grid=(n_tiles,),
        in_specs=list(_token_specs(tile, d, n_prompt_tiles)) + [
            _layer_spec(wts['norm2'], layer), _layer_spec(wts['w_route'], layer),
            _layer_spec(wts['b_route'], layer), _const_spec((tile, tile))],
        out_specs=(pl.BlockSpec((tile, nlt, LANES), lambda i: (i, 0, 0)), meta_spec, meta_spec,
                   pl.BlockSpec((1, SUBLANES, ROUTE_LANES), lambda i: (i, 0, 0))),
        out_shape=(jax.ShapeDtypeStruct((n_tiles * tile, nlt, LANES), BF16),
                   jax.ShapeDtypeStruct((n_tiles, SUBLANES, tile), F32),
                   jax.ShapeDtypeStruct((n_tiles, SUBLANES, tile), jnp.int32),
                   jax.ShapeDtypeStruct((n_tiles, SUBLANES, ROUTE_LANES), F32)),
        scratch_shapes=[pltpu.VMEM((tile * nlt, LANES), F32)],
        compiler_params=pltpu.CompilerParams(
            dimension_semantics=("arbitrary",), vmem_limit_bytes=VMEM_LIMIT),
        name="moe_router",
    )(yp, ys, wts['norm2'], wts['w_route'], wts['b_route'], tri)


def _run_copy(tile_idx, e, lo_ref, cnt_ref, dst_ref, local, remote, sem, to_remote):
    k = tile_idx * N_EXPERTS + e
    loc = local.at[pl.ds(lo_ref[k], cnt_ref[k])]
    rem = remote.at[pl.ds(dst_ref[k], cnt_ref[k])]
    return pltpu.make_async_copy(loc, rem, sem) if to_remote else pltpu.make_async_copy(rem, loc, sem)


def _start_runs(tile_idx, lo_ref, cnt_ref, dst_ref, local, remote, sem, to_remote):
    def body(e, c):
        @pl.when(cnt_ref[tile_idx * N_EXPERTS + e] > 0)
        def _():
            _run_copy(tile_idx, e, lo_ref, cnt_ref, dst_ref, local, remote, sem, to_remote).start()
        return c
    lax.fori_loop(0, N_EXPERTS, body, 0)


def _wait_runs(local, sem):
    pltpu.make_async_copy(local, local, sem).wait()


def _dispatch_kernel(lo_ref, cnt_ref, dst_ref, padst_ref, padn_ref,
                     s1_ref, s2_ref, x_ref, xs_ref, loc, zeros, sem, *, unroll):
    i = pl.program_id(0)
    last = pl.num_programs(0) - 1
    tile = x_ref.shape[0]
    slot = i % 2
    buf = loc.at[slot]

    @pl.when(i >= 2)
    def _():
        _wait_runs(buf, sem.at[slot])

    def body(tt, c):
        for u in range(unroll):
            t = tt * unroll + u
            row = x_ref[t]
            buf[s1_ref[t]] = row
            buf[s2_ref[t]] = row
        return c

    lax.fori_loop(0, tile // unroll, body, 0)
    _start_runs(i, lo_ref, cnt_ref, dst_ref, buf, xs_ref, sem.at[slot], True)

    @pl.when(i == last)
    def _():
        zeros[...] = jnp.zeros_like(zeros)

        def pad_copy(e):
            n = padn_ref[e]
            dst = xs_ref.at[pl.ds(padst_ref[e], n)]
            return pltpu.make_async_copy(zeros.at[pl.ds(0, n)], dst, sem.at[2])

        def each_pad(fn):
            def b(e, c):
                @pl.when(padn_ref[e] > 0)
                def _():
                    fn(e)
                return c
            lax.fori_loop(0, padn_ref.shape[0], b, 0)

        each_pad(lambda e: pad_copy(e).start())
        _wait_runs(buf, sem.at[slot])

        @pl.when(i >= 1)
        def _():
            _wait_runs(loc.at[1 - slot], sem.at[1 - slot])

        each_pad(lambda e: pad_copy(e).wait())


def _smem_tile(tile):
    return pl.BlockSpec((tile,), lambda i, *_: (i,), memory_space=pltpu.SMEM)


def _dispatch(xn_slab, s1, s2, tables, tile, n_rows):
    token = xn_slab.shape[1:]
    return pl.pallas_call(
        functools.partial(_dispatch_kernel, unroll=8),
        grid_spec=pltpu.PrefetchScalarGridSpec(
            num_scalar_prefetch=5,
            grid=(xn_slab.shape[0] // tile,),
            in_specs=[_smem_tile(tile), _smem_tile(tile),
                      pl.BlockSpec((tile,) + token, lambda i, *_: (i, 0, 0))],
            out_specs=pl.BlockSpec(memory_space=pl.ANY),
            scratch_shapes=[pltpu.VMEM((2, 2 * tile) + token, xn_slab.dtype),
                            pltpu.VMEM((MOE_BLOCK,) + token, xn_slab.dtype),
                            pltpu.SemaphoreType.DMA((3,))]),
        out_shape=jax.ShapeDtypeStruct((n_rows,) + token, xn_slab.dtype),
        compiler_params=pltpu.CompilerParams(
            dimension_semantics=("arbitrary",), vmem_limit_bytes=VMEM_LIMIT),
        name="moe_dispatch",
    )(*tables, s1, s2, xn_slab)


def _experts_kernel(blk_ref, nused_ref, xs_ref, wg_ref, wu_ref, wd_ref, ys_ref,
                    wg_s, wu_s, wd_s, slab_s):
    b = pl.program_id(0)
    rows, nlt, _ = xs_ref.shape

    @pl.when(jnp.logical_or(b == 0, blk_ref[b] != blk_ref[jnp.maximum(b - 1, 0)]))
    def _():
        wg_s[...] = wg_ref[...].astype(BF16)
        wu_s[...] = wu_ref[...].astype(BF16)
        wd_s[...] = wd_ref[...].astype(BF16)

    @pl.when(b < nused_ref[0])
    def _():
        _tokens_to_slab(slab_s, xs_ref[...])
        x = _from_slab(slab_s, rows, nlt).astype(BF16)
        hg = _dot(x, wg_s[...])
        hu = _dot(x, wu_s[...])
        h = (hg * _sigmoid(hg) * hu).astype(BF16)
        _to_slab(slab_s, _dot(h, wd_s[...]))
        ys_ref[...] = _tokens_from_slab(slab_s, nlt)

    @pl.when(b >= nused_ref[0])
    def _():
        ys_ref[...] = jnp.zeros_like(ys_ref)


def _experts(xs, blk_expert, n_used, wts, layer):
    n_rows, nlt, _ = xs.shape
    d = nlt * LANES
    f = wts['w_gate'].shape[-1]
    rows = pl.BlockSpec((MOE_BLOCK, nlt, LANES), lambda b, blk, nu: (b, 0, 0))
    w_spec = lambda r, c: pl.BlockSpec((None, None, r, c), lambda b, blk, nu: (layer, blk[b], 0, 0))
    return pl.pallas_call(
        _experts_kernel,
        grid_spec=pltpu.PrefetchScalarGridSpec(
            num_scalar_prefetch=2,
            grid=(n_rows // MOE_BLOCK,),
            in_specs=[rows, w_spec(d, f), w_spec(d, f), w_spec(f, d)],
            out_specs=rows,
            scratch_shapes=[pltpu.VMEM((d, f), BF16), pltpu.VMEM((d, f), BF16),
                            pltpu.VMEM((f, d), BF16),
                            pltpu.VMEM((MOE_BLOCK * nlt, LANES), F32)]),
        out_shape=jax.ShapeDtypeStruct(xs.shape, xs.dtype),
        compiler_params=pltpu.CompilerParams(
            dimension_semantics=("arbitrary",), vmem_limit_bytes=VMEM_LIMIT),
        name="moe_experts",
    )(blk_expert, n_used, xs, wts['w_gate'], wts['w_up'], wts['w_down'])


def _combine_kernel(lo_ref, cnt_ref, dst_ref, s1_ref, s2_ref, w1_ref, w2_ref, yp_ref, ys_ref,
                    nf_ref, rows_ref, op_ref, os_ref, loc, acc, sem,
                    *, unroll, final_norm, n_prompt_tiles):
    i = pl.program_id(0)
    tile = yp_ref.shape[0]
    nlt = loc.shape[2]
    slot = i % 2
    buf = loc.at[slot]
    fetch = lambda t, s: _start_runs(t, lo_ref, cnt_ref, dst_ref, loc.at[s], rows_ref, sem.at[s],
                                     False)

    @pl.when(i == 0)
    def _():
        fetch(0, 0)

    @pl.when(i + 1 < pl.num_programs(0))
    def _():
        fetch(i + 1, 1 - slot)

    _wait_runs(buf, sem.at[slot])

    def body(tt, c):
        for u in range(unroll):
            t = tt * unroll + u
            r1 = buf[s1_ref[t]].astype(F32)
            r2 = buf[s2_ref[t]].astype(F32)
            acc[pl.ds(pl.multiple_of(t * nlt, nlt), nlt), :] = w1_ref[t] * r1 + w2_ref[t] * r2
        return c

    lax.fori_loop(0, tile // unroll, body, 0)
    is_prompt = i < n_prompt_tiles
    out = jnp.where(is_prompt, yp_ref[...], ys_ref[...]) + _from_slab(acc, tile, nlt)
    if final_norm:
        out = _rms_norm(out, nf_ref[...])

    @pl.when(is_prompt)
    def _():
        op_ref[...] = out

    @pl.when(jnp.logical_not(is_prompt))
    def _():
        os_ref[...] = out


def _combine(yp, ys, rows, meta, tables, norm_f, final_norm, tile):
    d = yp.shape[1]
    token = rows.shape[1:]
    n_prompt_tiles = yp.shape[0] // tile
    n_tiles = n_prompt_tiles + ys.shape[0] // tile
    tok_p, tok_s = _token_specs(tile, d, n_prompt_tiles)
    return pl.pallas_call(
        functools.partial(_combine_kernel, unroll=8, final_norm=final_norm,
                          n_prompt_tiles=n_prompt_tiles),
        grid_spec=pltpu.PrefetchScalarGridSpec(
            num_scalar_prefetch=3,
            grid=(n_tiles,),
            in_specs=[_smem_tile(tile)] * 4 + [tok_p, tok_s, _const_spec((1, d)),
                                               pl.BlockSpec(memory_space=pl.ANY)],
            out_specs=(tok_p, tok_s),
            scratch_shapes=[pltpu.VMEM((2, 2 * tile) + token, rows.dtype),
                            pltpu.VMEM((tile * token[0], LANES), F32),
                            pltpu.SemaphoreType.DMA((2,))]),
        out_shape=(jax.ShapeDtypeStruct(yp.shape, F32), jax.ShapeDtypeStruct(ys.shape, F32)),
        compiler_params=pltpu.CompilerParams(
            dimension_semantics=("arbitrary",), vmem_limit_bytes=VMEM_LIMIT),
        name="moe_combine",
    )(*tables, *meta, yp, ys, norm_f, rows)


def _moe(yp, ys, wts, layer, norm_f, final_norm, tile):
    d = yp.shape[1]
    n = yp.shape[0] + ys.shape[0]
    assert yp.shape[0] % tile == 0 and ys.shape[0] % tile == 0 and d % LANES == 0
    xn_slab, meta_f, meta_i, cnt = _router(yp, ys, wts, layer, tile)
    cnt = cnt[:, 0, N_GROUPS:N_GROUPS + N_EXPERTS].astype(jnp.int32)
    lo = jnp.cumsum(cnt, axis=1) - cnt
    seg = jnp.sum(cnt, axis=0)
    seg_pad = (seg + MOE_BLOCK - 1) // MOE_BLOCK * MOE_BLOCK
    seg_end = jnp.cumsum(seg_pad)
    seg_start = seg_end - seg_pad
    dst = seg_start[None, :] + jnp.cumsum(cnt, axis=0) - cnt
    n_blocks = (2 * n + N_EXPERTS * (MOE_BLOCK - 1)) // MOE_BLOCK
    blk_ids = jnp.arange(n_blocks, dtype=jnp.int32)
    blk_expert = jnp.minimum(
        jnp.sum((seg_end[None, :] <= blk_ids[:, None] * MOE_BLOCK).astype(jnp.int32), axis=1),
        N_EXPERTS - 1)
    n_used = seg_end[-1:] // MOE_BLOCK
    pad_start = jnp.concatenate([seg_start + seg, blk_ids * MOE_BLOCK])
    pad_rows = jnp.concatenate([seg_pad - seg, jnp.where(blk_ids >= n_used[0], MOE_BLOCK, 0)])
    flat = lambda a: a.reshape(-1)
    runs = (flat(lo), flat(cnt), flat(dst))
    s1, s2 = flat(meta_i[:, 4]), flat(meta_i[:, 5])
    w1, w2 = flat(meta_f[:, 2]), flat(meta_f[:, 3])
    xs = _dispatch(xn_slab, s1, s2, runs + (pad_start, pad_rows), tile, n_blocks * MOE_BLOCK)
    rows = _experts(xs, blk_expert, n_used, wts, layer)
    return _combine(yp, ys, rows, (s1, s2, w1, w2), runs, norm_f, final_norm, tile)


def _prep_weights(norm1, w_in, ln_g, ln_b, w_s, b_s, conv_w, conv_b, lru_lambda, w_rg_a, b_rg_a,
                  w_rg_x, b_rg_x, w_out, norm2, w_route_group, b_route_group, w_route_expert,
                  b_route_expert, w_gate, w_up, w_down, n_sample_t):
    depth, d = w_in.shape[0], w_in.shape[1]
    dh = d // H_A
    row = lambda a: a.reshape(depth, 1, -1).astype(F32)
    causal = jnp.tril(jnp.ones((CHUNK, CHUNK), dtype=bool))
    ws = jnp.where(causal, 0.5 * w_s, 0.0)
    bs = jnp.repeat(jnp.swapaxes(0.5 * b_s, 1, 2), dh, axis=2)
    in_scale = jnp.concatenate([jnp.ones((3 * d,), F32), jnp.full((2 * d,), 0.5, F32)])

    def pair_bd(w):
        w = w.reshape(depth, H_B // 2, 2, w.shape[-2], w.shape[-1])
        z = jnp.zeros_like(w[:, :, 0])
        return jnp.concatenate([jnp.concatenate([w[:, :, 0], z], axis=-1),
                                jnp.concatenate([z, w[:, :, 1]], axis=-1)], axis=-2)

    pad = ROUTE_LANES - N_GROUPS - N_EXPERTS
    w_route = jnp.concatenate(
        [w_route_group, w_route_expert, jnp.zeros((depth, d, pad), F32)], axis=-1)
    b_route = jnp.concatenate(
        [b_route_group, b_route_expert, jnp.zeros((depth, pad), F32)], axis=-1)
    nt = n_sample_t
    return dict(
        norm1=row(norm1), w_in=(w_in * in_scale).astype(BF16), ln_g=row(ln_g), ln_b=row(ln_b),
        ws=ws.astype(BF16), bs=bs.astype(F32), conv_w=conv_w.astype(F32), conv_b=row(conv_b),
        lam=row(lru_lambda),
        wbd=(0.5 * jnp.concatenate([pair_bd(w_rg_a), pair_bd(w_rg_x)], axis=-1)).astype(BF16),
        b_a=row(0.5 * b_rg_a), b_x=row(0.5 * b_rg_x), w_out=(0.5 * w_out).astype(BF16),
        norm2=row(norm2), w_route=w_route.astype(BF16), b_route=row(b_route),
        w_gate=w_gate, w_up=w_up, w_down=w_down,
        wsc=jnp.repeat(jnp.transpose(ws[:, :, :nt, :nt], (0, 2, 3, 1)), dh, axis=3).astype(F32),
        bsc=bs[:, :nt].astype(F32),
    )


def kernel(x_prompt, x_sample, state_lru_h, state_conv, norm1, w_in, ln_g, ln_b, w_s, b_s, conv_w, conv_b, lru_lambda, w_rg_a, b_rg_a, w_rg_x, b_rg_x, w_out, norm2, w_route_group, b_route_group, w_route_expert, b_route_expert, w_gate, w_up, w_down, norm_f):
    depth = w_in.shape[0]
    nb, seq, d = x_prompt.shape
    ns, nt, _ = x_sample.shape
    assert nt <= CHUNK
    nf = norm_f.reshape(1, d).astype(F32)
    wts = _prep_weights(norm1, w_in, ln_g, ln_b, w_s, b_s, conv_w, conv_b, lru_lambda, w_rg_a,
                        b_rg_a, w_rg_x, b_rg_x, w_out, norm2, w_route_group, b_route_group,
                        w_route_expert, b_route_expert, w_gate, w_up, w_down, nt)
    conv_state = jnp.transpose(state_conv, (0, 2, 1, 3))
    tile = min(MOE_TILE, nt * ns)
    yp = x_prompt
    ys = jnp.transpose(x_sample, (1, 0, 2))
    hp, cp, vp, hs, cs, vs = [], [], [], [], [], []
    for l in range(depth):
        yp, h, c, v = _mixer_prompt(yp, wts, l)
        hp.append(h)
        cp.append(c)
        vp.append(v)
        ys, h, c, v = _mixer_sample(ys, state_lru_h, conv_state, wts, l)
        hs.append(h)
        cs.append(c)
        vs.append(v)
        yp, ys = _moe(yp.reshape(nb * seq, d), ys.reshape(nt * ns, d), wts, l, nf,
                      l == depth - 1, tile)
        yp = yp.reshape(nb, seq, d)
        ys = ys.reshape(nt, ns, d)
    to_seq_major = lambda a: jnp.transpose(jnp.stack(a), (0, 2, 1, 3))
    return (yp, jnp.transpose(ys, (1, 0, 2)), jnp.stack(hp), jnp.stack(cp), jnp.stack(vp),
            jnp.stack(hs), to_seq_major(cs), to_seq_major(vs))
```

```python
import functools

import jax
import jax.numpy as jnp
from jax import lax
from jax.experimental import pallas as pl
from jax.experimental.pallas import tpu as pltpu

F32 = jnp.float32
BF16 = jnp.bfloat16

CHUNK = 128
H_A = 4
H_B = 8
CONV_W = 4
LRU_C = 8.0
N_GROUPS = 4
E_PER_GROUP = 8
N_EXPERTS = N_GROUPS * E_PER_GROUP
EPS = 1e-6

LANES = 128
SUBLANES = 8
SCAN_UNROLL = 4
SEQS_PER_TRIP = 4
ROUTE_ROWS = 48
MOE_TILE = 512
MOE_BLOCK = 512
VMEM_LIMIT = 60 * 1024 * 1024


def _dot(a, b):
    return jnp.dot(a, b, preferred_element_type=F32)


def _sigmoid(x):
    return 0.5 * (jnp.tanh(0.5 * x) + 1.0)


def _gelu2(x):
    c = 0.7978845608028654
    return x * (1.0 + jnp.tanh(x * (c + (c * 0.044715) * (x * x))))


def _rms_norm(x, g):
    return x * lax.rsqrt(jnp.mean(x * x, axis=-1, keepdims=True) + EPS) * g


def _layer_norm(x, g, b, eps):
    xc = x - jnp.mean(x, axis=-1, keepdims=True)
    var = jnp.mean(xc * xc, axis=-1, keepdims=True)
    return xc * lax.rsqrt(var + eps) * g + b


def _softplus(x):
    return jnp.maximum(x, 0.0) + jnp.log(1.0 + jnp.exp(-jnp.abs(x)))


def _lru_gates(xc, wbd_ref, b_a, b_x, lam_half):
    d = xc.shape[-1]
    xcb = xc.astype(BF16)
    r_parts, i_parts = [], []
    for q in range(d // 256):
        ri = _dot(xcb[:, q * 256:(q + 1) * 256], wbd_ref[q])
        r_parts.append(ri[:, :256])
        i_parts.append(ri[:, 256:])
    r2 = 1.0 + jnp.tanh(jnp.concatenate(r_parts, axis=-1) + b_a)
    i2 = 1.0 + jnp.tanh(jnp.concatenate(i_parts, axis=-1) + b_x)
    a = jnp.exp(lam_half * r2)
    mult_half = jnp.sqrt(0.25 - 0.25 * (a * a))
    return a, mult_half, i2 * xc


def _shift_rows(x, hist, j):
    sh = pltpu.roll(x, j, axis=0)
    top = jnp.where(lax.broadcasted_iota(jnp.int32, (SUBLANES, 1), 0) < j,
                    pltpu.roll(hist, j, axis=0), sh[:SUBLANES])
    return jnp.concatenate([top, sh[SUBLANES:]], axis=0)


def _mixer_prompt_kernel(x_ref, norm_ref, win_ref, lng_ref, lnb_ref, ws_ref, bs_ref,
                         cw_ref, cb_ref, lam_ref, wbd_ref, ba_ref, bx_ref, wout_ref,
                         y_ref, h_ref, conv_ref, v_ref,
                         hist_s, a_s, b_s, ya_s, gb_s, hc_s):
    i = pl.program_id(0)
    last = pl.num_programs(0) - 1
    nseq, _, d = x_ref.shape
    nlt = d // LANES
    spt = SEQS_PER_TRIP
    pair = spt * CHUNK

    @pl.when(i == 0)
    def _():
        hist_s[...] = jnp.zeros_like(hist_s)
        hc_s[...] = jnp.zeros_like(hc_s)

    lam_half = (-0.5 * LRU_C) * _softplus(-lam_ref[...])
    is_first = i == 0

    def branch_body(p, c):
        row = pl.multiple_of(p * pair, pair)
        x = jnp.concatenate([x_ref[spt * p + k] for k in range(spt)], axis=0)
        xnb = _rms_norm(x, norm_ref[...]).astype(BF16)
        v = _layer_norm(_gelu2(_dot(xnb, win_ref[:, d:2 * d])), lng_ref[...], lnb_ref[...],
                        4.0 * EPS)

        @pl.when(i == last)
        def _():
            for k in range(spt):
                v_ref[spt * p + k] = v[k * CHUNK:(k + 1) * CHUNK]

        vb = v.astype(BF16)
        dh = d // H_A
        z_rows = []
        for k in range(spt):
            zs = [_dot(ws_ref[h], vb[k * CHUNK:(k + 1) * CHUNK, h * dh:(h + 1) * dh])
                  for h in range(H_A)]
            z_rows.append(jnp.concatenate(zs, axis=-1) + bs_ref[...])
        z = jnp.concatenate(z_rows, axis=0)
        ya = _gelu2(_dot(xnb, win_ref[:, 0:d])) * z
        ga2 = 1.0 + jnp.tanh(_dot(xnb, win_ref[:, 3 * d:4 * d]))
        ya_s[pl.ds(row, pair), :] = (ga2 * ya).astype(BF16)
        gb_s[pl.ds(row, pair), :] = (1.0 + jnp.tanh(_dot(xnb, win_ref[:, 4 * d:5 * d]))).astype(BF16)
        xr = _dot(xnb, win_ref[:, 2 * d:3 * d])
        xcs = []
        for k in range(spt):
            s = spt * p + k
            xk = xr[k * CHUNK:(k + 1) * CHUNK]
            hist = hist_s[s]
            xc = cb_ref[...] + xk * cw_ref[CONV_W - 1:CONV_W, :]
            for j in range(1, CONV_W):
                xc = xc + _shift_rows(xk, hist, j) * cw_ref[CONV_W - 1 - j:CONV_W - j, :]
            hist_s[s] = xk[CHUNK - SUBLANES:]
            xcs.append(xc)
        xc = jnp.concatenate(xcs, axis=0)
        a, mult_half, ix2 = _lru_gates(xc, wbd_ref, ba_ref[...], bx_ref[...], lam_half)
        bt = mult_half * ix2
        for k in range(spt):
            s = spt * p + k
            for j in range(nlt):
                a_s[j, pl.ds(s, CHUNK, stride=nseq), :] = a[k * CHUNK:(k + 1) * CHUNK, j * LANES:(j + 1) * LANES]
                b_s[j, pl.ds(s, CHUNK, stride=nseq), :] = bt[k * CHUNK:(k + 1) * CHUNK, j * LANES:(j + 1) * LANES]

        @pl.when(is_first)
        def _():
            for k in range(spt):
                first = 0.5 * ix2[k * CHUNK:k * CHUNK + 1]
                for j in range(nlt):
                    b_s[j, pl.ds(spt * p + k, 1), :] = first[:, j * LANES:(j + 1) * LANES]

        return c

    lax.fori_loop(0, nseq // spt, branch_body, 0)

    def scan_body(tt, hs):
        for u in range(SCAN_UNROLL):
            row = pl.multiple_of((tt * SCAN_UNROLL + u) * nseq, nseq)
            out = []
            for j in range(nlt):
                h = a_s[j, pl.ds(row, nseq), :] * hs[j] + b_s[j, pl.ds(row, nseq), :]
                b_s[j, pl.ds(row, nseq), :] = h
                out.append(h)
            hs = tuple(out)
        return hs

    h0 = tuple(hc_s[:, j * LANES:(j + 1) * LANES] for j in range(nlt))
    hs = lax.fori_loop(0, CHUNK // SCAN_UNROLL, scan_body, h0)
    hfin = jnp.concatenate(hs, axis=-1)
    hc_s[...] = hfin
    h_ref[...] = hfin

    def out_body(p, c):
        row = pl.multiple_of(p * pair, pair)
        hrows = []
        for k in range(spt):
            hrows.append(jnp.concatenate(
                [b_s[j, pl.ds(spt * p + k, CHUNK, stride=nseq), :] for j in range(nlt)], axis=-1))
        h = jnp.concatenate(hrows, axis=0)
        merged = (ya_s[pl.ds(row, pair), :].astype(F32)
                  + gb_s[pl.ds(row, pair), :].astype(F32) * h).astype(BF16)
        o = _dot(merged, wout_ref[...])
        for k in range(spt):
            y_ref[spt * p + k] = x_ref[spt * p + k] + o[k * CHUNK:(k + 1) * CHUNK]
        return c

    lax.fori_loop(0, nseq // spt, out_body, 0)

    @pl.when(i == last)
    def _():
        conv_ref[...] = hist_s[:, SUBLANES - (CONV_W - 1):, :]


def _const_spec(shape):
    nd = len(shape)
    return pl.BlockSpec(shape, lambda i, *_, _n=nd: (0,) * _n, pipeline_mode=pl.Buffered(1))


def _layer_spec(stacked, layer):
    shape = stacked.shape[1:]
    nd = len(shape)
    return pl.BlockSpec((None,) + tuple(shape), lambda i, *_, _n=nd: (layer,) + (0,) * _n,
                        pipeline_mode=pl.Buffered(1))


MIXER_WEIGHTS = ('norm1', 'w_in', 'ln_g', 'ln_b', 'ws', 'bs', 'conv_w', 'conv_b', 'lam', 'wbd',
                 'b_a', 'b_x', 'w_out')


def _mixer_prompt(x, wts, layer):
    nseq, seq, d = x.shape
    assert seq % CHUNK == 0 and nseq == SUBLANES and d % 256 == 0
    n_chunks = seq // CHUNK
    nlt = d // LANES
    weights = tuple(wts[k] for k in MIXER_WEIGHTS)
    x_spec = pl.BlockSpec((nseq, CHUNK, d), lambda i: (0, i, 0))
    out_shape = (
        jax.ShapeDtypeStruct((nseq, seq, d), F32),
        jax.ShapeDtypeStruct((nseq, d), F32),
        jax.ShapeDtypeStruct((nseq, CONV_W - 1, d), F32),
        jax.ShapeDtypeStruct((nseq, CHUNK, d), F32),
    )
    out_specs = (
        x_spec,
        _const_spec((nseq, d)),
        _const_spec((nseq, CONV_W - 1, d)),
        _const_spec((nseq, CHUNK, d)),
    )
    scratch = [
        pltpu.VMEM((nseq, SUBLANES, d), F32),
        pltpu.VMEM((nlt, CHUNK * nseq, LANES), F32),
        pltpu.VMEM((nlt, CHUNK * nseq, LANES), F32),
        pltpu.VMEM((nseq * CHUNK, d), BF16),
        pltpu.VMEM((nseq * CHUNK, d), BF16),
        pltpu.VMEM((nseq, d), F32),
    ]
    return pl.pallas_call(
        _mixer_prompt_kernel,
        grid=(n_chunks,),
        in_specs=[x_spec] + [_layer_spec(w, layer) for w in weights],
        out_specs=out_specs,
        out_shape=out_shape,
        scratch_shapes=scratch,
        compiler_params=pltpu.CompilerParams(
            dimension_semantics=("arbitrary",), vmem_limit_bytes=VMEM_LIMIT),
        name="mixer_prompt",
    )(x, *weights)


def _mixer_sample_kernel(x_ref, h0_ref, cst_ref, norm_ref, win_ref, lng_ref, lnb_ref, wsc_ref,
                         bsc_ref, cw_ref, cb_ref, lam_ref, wbd_ref, ba_ref, bx_ref, wout_ref,
                         y_ref, h_ref, conv_ref, v_ref):
    nt, ns, d = x_ref.shape
    lam_half = (-0.5 * LRU_C) * _softplus(-lam_ref[...])
    g1 = norm_ref[...]
    h = h0_ref[...]
    xp = [cst_ref[k] for k in range(CONV_W - 1)]
    for t in range(nt):
        x = x_ref[t]
        xnb = _rms_norm(x, g1).astype(BF16)
        v = _layer_norm(_gelu2(_dot(xnb, win_ref[:, d:2 * d])), lng_ref[...], lnb_ref[...],
                        4.0 * EPS)
        v_ref[t] = v
        z = bsc_ref[t:t + 1, :]
        for s in range(t + 1):
            z = z + wsc_ref[t, s:s + 1, :] * v_ref[s]
        ya = _gelu2(_dot(xnb, win_ref[:, 0:d])) * z
        ga2 = 1.0 + jnp.tanh(_dot(xnb, win_ref[:, 3 * d:4 * d]))
        xr = _dot(xnb, win_ref[:, 2 * d:3 * d])
        xp.append(xr)
        xc = cb_ref[...]
        for k in range(CONV_W):
            xc = xc + xp[t + k] * cw_ref[k:k + 1, :]
        a, mult_half, ix2 = _lru_gates(xc, wbd_ref, ba_ref[...], bx_ref[...], lam_half)
        h = a * h + mult_half * ix2
        gb2 = 1.0 + jnp.tanh(_dot(xnb, win_ref[:, 4 * d:5 * d]))
        merged = (ga2 * ya + gb2 * h).astype(BF16)
        y_ref[t] = x + _dot(merged, wout_ref[...])
    h_ref[...] = h
    for k in range(CONV_W - 1):
        conv_ref[k] = xp[nt + k]


def _mixer_sample(x, h0, conv_state, wts, layer):
    nt, ns, d = x.shape
    names = MIXER_WEIGHTS[:4] + ('wsc', 'bsc') + MIXER_WEIGHTS[6:]
    stacked = (h0, conv_state) + tuple(wts[k] for k in names)
    out_shape = (
        jax.ShapeDtypeStruct((nt, ns, d), F32),
        jax.ShapeDtypeStruct((ns, d), F32),
        jax.ShapeDtypeStruct((CONV_W - 1, ns, d), F32),
        jax.ShapeDtypeStruct((nt, ns, d), F32),
    )
    return pl.pallas_call(
        _mixer_sample_kernel,
        grid=(1,),
        in_specs=[_const_spec(x.shape)] + [_layer_spec(w, layer) for w in stacked],
        out_specs=tuple(_const_spec(s.shape) for s in out_shape),
        out_shape=out_shape,
        compiler_params=pltpu.CompilerParams(
            dimension_semantics=("arbitrary",), vmem_limit_bytes=VMEM_LIMIT),
        name="mixer_sample",
    )(x, *stacked)


def _to_slab(ref, val):
    t, d = val.shape
    nlt = d // LANES
    for j in range(nlt):
        ref[pl.ds(j, t, stride=nlt), :] = val[:, j * LANES:(j + 1) * LANES]


def _from_slab(ref, t, nlt):
    return jnp.concatenate([ref[pl.ds(j, t, stride=nlt), :] for j in range(nlt)], axis=-1)


def _tokens_from_slab(slab_ref, nlt):
    return slab_ref[...].reshape(slab_ref.shape[0] // nlt, nlt, LANES).astype(BF16)


def _tokens_to_slab(slab_ref, tokens):
    slab_ref[...] = tokens.astype(F32).reshape(slab_ref.shape)


def _router_kernel(yp_ref, ys_ref, norm_ref, wr_ref, br_ref, tri_ref, low_ref, xn_ref, mf_ref,
                   mi_ref, cnt_ref, slab_s, *, n_prompt_tiles):
    y = jnp.where(pl.program_id(0) < n_prompt_tiles, yp_ref[...], ys_ref[...])
    xn = _rms_norm(y, norm_ref[...])
    _to_slab(slab_s, xn)
    xn_ref[...] = _tokens_from_slab(slab_s, xn_ref.shape[1])
    lt = lax.dot_general(wr_ref[...], xn.astype(BF16), (((1,), (1,)), ((), ())),
                         preferred_element_type=F32) + br_ref[...]
    tile = lt.shape[1]
    row = lax.broadcasted_iota(jnp.int32, (SUBLANES, tile), 0).astype(F32)
    neg = jnp.float32(-jnp.inf)
    big = jnp.float32(SUBLANES)

    def argmax_first(vals):
        m = jnp.max(vals, axis=0, keepdims=True)
        return m, jnp.min(jnp.where(vals == m, row, big), axis=0, keepdims=True)

    def group_block(x, gi):
        out = x[SUBLANES:2 * SUBLANES]
        for g in range(1, N_GROUPS):
            out = jnp.where(gi == g, x[(g + 1) * SUBLANES:(g + 2) * SUBLANES], out)
        return out

    gl = jnp.where(row < N_GROUPS, lt[0:SUBLANES], neg)
    gm, gi = argmax_first(gl)
    pg_top = 1.0 / jnp.sum(jnp.exp(gl - gm), axis=0, keepdims=True)
    el = group_block(lt, gi)
    m1, i1 = argmax_first(el)
    m2, i2 = argmax_first(jnp.where(row == i1, neg, el))
    e2 = jnp.exp(m2 - m1)
    w1 = pg_top / (1.0 + e2)
    w2 = pg_top * e2 / (1.0 + e2)
    hot1 = row == i1
    hot2 = row == i2
    picked = jnp.where(jnp.logical_or(hot1, hot2), 1.0, 0.0)
    zero = jnp.zeros_like(picked)
    onehot = jnp.concatenate(
        [zero] + [jnp.where(gi == g, picked, 0.0) for g in range(N_GROUPS)] + [zero],
        axis=0).astype(BF16)
    before = _dot(onehot, tri_ref[...])
    lower = jnp.sum(_dot(low_ref[...], onehot), axis=1, keepdims=True)
    slot = group_block(before + lower, gi)
    p1 = jnp.sum(jnp.where(hot1, slot, 0.0), axis=0, keepdims=True)
    p2 = jnp.sum(jnp.where(hot2, slot, 0.0), axis=0, keepdims=True)
    meta = jnp.zeros((SUBLANES, tile), F32)
    for k, c in enumerate((gi * E_PER_GROUP + i1, gi * E_PER_GROUP + i2, w1, w2, p1, p2)):
        meta = jnp.where(row == k, c, meta)
    mf_ref[0] = meta
    mi_ref[0] = meta.astype(jnp.int32)
    cnt = jnp.sum(onehot.astype(F32), axis=1, keepdims=True)
    cnt_ref[0] = jnp.broadcast_to(cnt, cnt_ref.shape[1:])


def _token_specs(tile, d, n_prompt_tiles):
    last = n_prompt_tiles - 1
    return (pl.BlockSpec((tile, d), lambda i, *_: (jnp.minimum(i, last), 0)),
            pl.BlockSpec((tile, d), lambda i, *_: (jnp.maximum(i - n_prompt_tiles, 0), 0)))


def _router(yp, ys, wts, layer, tile):
    d = yp.shape[1]
    nlt = d // LANES
    n_prompt_tiles = yp.shape[0] // tile
    n_tiles = n_prompt_tiles + ys.shape[0] // tile
    tri = jnp.triu(jnp.ones((tile, tile), BF16), 1)
    low = jnp.tril(jnp.ones((ROUTE_ROWS, ROUTE_ROWS), BF16), -1)
    meta_spec = pl.BlockSpec((1, SUBLANES, tile), lambda i: (i, 0, 0))
    return pl.pallas_call(
        functools.partial(_router_kernel, n_prompt_tiles=n_prompt_tiles),
        grid=(n_tiles,),
        in_specs=list(_token_specs(tile, d, n_prompt_tiles)) + [
            _layer_spec(wts['norm2'], layer), _layer_spec(wts['w_route'], layer),
            _layer_spec(wts['b_route'], layer), _const_spec((tile, tile)),
            _const_spec((ROUTE_ROWS, ROUTE_ROWS))],
        out_specs=(pl.BlockSpec((tile, nlt, LANES), lambda i: (i, 0, 0)), meta_spec, meta_spec,
                   pl.BlockSpec((1, ROUTE_ROWS, LANES), lambda i: (i, 0, 0))),
        out_shape=(jax.ShapeDtypeStruct((n_tiles * tile, nlt, LANES), BF16),
                   jax.ShapeDtypeStruct((n_tiles, SUBLANES, tile), F32),
                   jax.ShapeDtypeStruct((n_tiles, SUBLANES, tile), jnp.int32),
                   jax.ShapeDtypeStruct((n_tiles, ROUTE_ROWS, LANES), F32)),
        scratch_shapes=[pltpu.VMEM((tile * nlt, LANES), F32)],
        compiler_params=pltpu.CompilerParams(
            dimension_semantics=("arbitrary",), vmem_limit_bytes=VMEM_LIMIT),
        name="moe_router",
    )(yp, ys, wts['norm2'], wts['w_route'], wts['b_route'], tri, low)


def _run_copy(tile_idx, e, lo_ref, cnt_ref, dst_ref, local, remote, sem, to_remote):
    k = tile_idx * N_EXPERTS + e
    loc = local.at[pl.ds(lo_ref[k], cnt_ref[k])]
    rem = remote.at[pl.ds(dst_ref[k], cnt_ref[k])]
    return pltpu.make_async_copy(loc, rem, sem) if to_remote else pltpu.make_async_copy(rem, loc, sem)


def _start_runs(tile_idx, lo_ref, cnt_ref, dst_ref, local, remote, sem, to_remote):
    def body(e, c):
        @pl.when(cnt_ref[tile_idx * N_EXPERTS + e] > 0)
        def _():
            _run_copy(tile_idx, e, lo_ref, cnt_ref, dst_ref, local, remote, sem, to_remote).start()
        return c
    lax.fori_loop(0, N_EXPERTS, body, 0)


def _wait_runs(local, sem):
    pltpu.make_async_copy(local, local, sem).wait()


def _dispatch_kernel(lo_ref, cnt_ref, dst_ref, padst_ref, padn_ref,
                     s1_ref, s2_ref, x_ref, xs_ref, loc, zeros, sem, *, unroll):
    i = pl.program_id(0)
    last = pl.num_programs(0) - 1
    tile = x_ref.shape[0]
    slot = i % 2
    buf = loc.at[slot]

    @pl.when(i >= 2)
    def _():
        _wait_runs(buf, sem.at[slot])

    def body(tt, c):
        for u in range(unroll):
            t = tt * unroll + u
            row = x_ref[t]
            buf[s1_ref[t]] = row
            buf[s2_ref[t]] = row
        return c

    lax.fori_loop(0, tile // unroll, body, 0)
    _start_runs(i, lo_ref, cnt_ref, dst_ref, buf, xs_ref, sem.at[slot], True)

    @pl.when(i == last)
    def _():
        zeros[...] = jnp.zeros_like(zeros)

        def pad_copy(e):
            n = padn_ref[e]
            dst = xs_ref.at[pl.ds(padst_ref[e], n)]
            return pltpu.make_async_copy(zeros.at[pl.ds(0, n)], dst, sem.at[2])

        def each_pad(fn):
            def b(e, c):
                @pl.when(padn_ref[e] > 0)
                def _():
                    fn(e)
                return c
            lax.fori_loop(0, padn_ref.shape[0], b, 0)

        each_pad(lambda e: pad_copy(e).start())
        _wait_runs(buf, sem.at[slot])

        @pl.when(i >= 1)
        def _():
            _wait_runs(loc.at[1 - slot], sem.at[1 - slot])

        each_pad(lambda e: pad_copy(e).wait())


def _smem_tile(tile):
    return pl.BlockSpec((tile,), lambda i, *_: (i,), memory_space=pltpu.SMEM)


def _dispatch(xn_slab, s1, s2, tables, tile, n_rows):
    token = xn_slab.shape[1:]
    return pl.pallas_call(
        functools.partial(_dispatch_kernel, unroll=8),
        grid_spec=pltpu.PrefetchScalarGridSpec(
            num_scalar_prefetch=5,
            grid=(xn_slab.shape[0] // tile,),
            in_specs=[_smem_tile(tile), _smem_tile(tile),
                      pl.BlockSpec((tile,) + token, lambda i, *_: (i, 0, 0))],
            out_specs=pl.BlockSpec(memory_space=pl.ANY),
            scratch_shapes=[pltpu.VMEM((2, 2 * tile) + token, xn_slab.dtype),
                            pltpu.VMEM((MOE_BLOCK,) + token, xn_slab.dtype),
                            pltpu.SemaphoreType.DMA((3,))]),
        out_shape=jax.ShapeDtypeStruct((n_rows,) + token, xn_slab.dtype),
        compiler_params=pltpu.CompilerParams(
            dimension_semantics=("arbitrary",), vmem_limit_bytes=VMEM_LIMIT),
        name="moe_dispatch",
    )(*tables, s1, s2, xn_slab)


def _experts_kernel(blk_ref, nused_ref, xs_ref, wg_ref, wu_ref, wd_ref, ys_ref,
                    wg_s, wu_s, wd_s, slab_s):
    b = pl.program_id(0)
    rows, nlt, _ = xs_ref.shape

    @pl.when(jnp.logical_or(b == 0, blk_ref[b] != blk_ref[jnp.maximum(b - 1, 0)]))
    def _():
        wg_s[...] = wg_ref[...].astype(BF16)
        wu_s[...] = wu_ref[...].astype(BF16)
        wd_s[...] = wd_ref[...].astype(BF16)

    @pl.when(b < nused_ref[0])
    def _():
        _tokens_to_slab(slab_s, xs_ref[...])
        x = _from_slab(slab_s, rows, nlt).astype(BF16)
        hg = _dot(x, wg_s[...])
        hu = _dot(x, wu_s[...])
        h = (hg * _sigmoid(hg) * hu).astype(BF16)
        _to_slab(slab_s, _dot(h, wd_s[...]))
        ys_ref[...] = _tokens_from_slab(slab_s, nlt)

    @pl.when(b >= nused_ref[0])
    def _():
        ys_ref[...] = jnp.zeros_like(ys_ref)


def _experts(xs, blk_expert, n_used, wts, layer):
    n_rows, nlt, _ = xs.shape
    d = nlt * LANES
    f = wts['w_gate'].shape[-1]
    rows = pl.BlockSpec((MOE_BLOCK, nlt, LANES), lambda b, blk, nu: (b, 0, 0))
    w_spec = lambda r, c: pl.BlockSpec((None, None, r, c), lambda b, blk, nu: (layer, blk[b], 0, 0))
    return pl.pallas_call(
        _experts_kernel,
        grid_spec=pltpu.PrefetchScalarGridSpec(
            num_scalar_prefetch=2,
            grid=(n_rows // MOE_BLOCK,),
            in_specs=[rows, w_spec(d, f), w_spec(d, f), w_spec(f, d)],
            out_specs=rows,
            scratch_shapes=[pltpu.VMEM((d, f), BF16), pltpu.VMEM((d, f), BF16),
                            pltpu.VMEM((f, d), BF16),
                            pltpu.VMEM((MOE_BLOCK * nlt, LANES), F32)]),
        out_shape=jax.ShapeDtypeStruct(xs.shape, xs.dtype),
        compiler_params=pltpu.CompilerParams(
            dimension_semantics=("arbitrary",), vmem_limit_bytes=VMEM_LIMIT),
        name="moe_experts",
    )(blk_expert, n_used, xs, wts['w_gate'], wts['w_up'], wts['w_down'])


def _combine_kernel(lo_ref, cnt_ref, dst_ref, s1_ref, s2_ref, w1_ref, w2_ref, yp_ref, ys_ref,
                    nf_ref, rows_ref, op_ref, os_ref, loc, acc, sem,
                    *, unroll, final_norm, n_prompt_tiles):
    i = pl.program_id(0)
    tile = yp_ref.shape[0]
    nlt = loc.shape[2]
    slot = i % 2
    buf = loc.at[slot]
    fetch = lambda t, s: _start_runs(t, lo_ref, cnt_ref, dst_ref, loc.at[s], rows_ref, sem.at[s],
                                     False)

    @pl.when(i == 0)
    def _():
        fetch(0, 0)

    @pl.when(i + 1 < pl.num_programs(0))
    def _():
        fetch(i + 1, 1 - slot)

    _wait_runs(buf, sem.at[slot])

    def body(tt, c):
        for u in range(unroll):
            t = tt * unroll + u
            r1 = buf[s1_ref[t]].astype(F32)
            r2 = buf[s2_ref[t]].astype(F32)
            acc[pl.ds(pl.multiple_of(t * nlt, nlt), nlt), :] = w1_ref[t] * r1 + w2_ref[t] * r2
        return c

    lax.fori_loop(0, tile // unroll, body, 0)
    is_prompt = i < n_prompt_tiles
    out = jnp.where(is_prompt, yp_ref[...], ys_ref[...]) + _from_slab(acc, tile, nlt)
    if final_norm:
        out = _rms_norm(out, nf_ref[...])

    @pl.when(is_prompt)
    def _():
        op_ref[...] = out

    @pl.when(jnp.logical_not(is_prompt))
    def _():
        os_ref[...] = out


def _combine(yp, ys, rows, meta, tables, norm_f, final_norm, tile):
    d = yp.shape[1]
    token = rows.shape[1:]
    n_prompt_tiles = yp.shape[0] // tile
    n_tiles = n_prompt_tiles + ys.shape[0] // tile
    tok_p, tok_s = _token_specs(tile, d, n_prompt_tiles)
    return pl.pallas_call(
        functools.partial(_combine_kernel, unroll=8, final_norm=final_norm,
                          n_prompt_tiles=n_prompt_tiles),
        grid_spec=pltpu.PrefetchScalarGridSpec(
            num_scalar_prefetch=3,
            grid=(n_tiles,),
            in_specs=[_smem_tile(tile)] * 4 + [tok_p, tok_s, _const_spec((1, d)),
                                               pl.BlockSpec(memory_space=pl.ANY)],
            out_specs=(tok_p, tok_s),
            scratch_shapes=[pltpu.VMEM((2, 2 * tile) + token, rows.dtype),
                            pltpu.VMEM((tile * token[0], LANES), F32),
                            pltpu.SemaphoreType.DMA((2,))]),
        out_shape=(jax.ShapeDtypeStruct(yp.shape, F32), jax.ShapeDtypeStruct(ys.shape, F32)),
        compiler_params=pltpu.CompilerParams(
            dimension_semantics=("arbitrary",), vmem_limit_bytes=VMEM_LIMIT),
        name="moe_combine",
    )(*tables, *meta, yp, ys, norm_f, rows)


def _moe(yp, ys, wts, layer, norm_f, final_norm, tile):
    d = yp.shape[1]
    n = yp.shape[0] + ys.shape[0]
    assert yp.shape[0] % tile == 0 and ys.shape[0] % tile == 0 and d % LANES == 0
    xn_slab, meta_f, meta_i, cnt = _router(yp, ys, wts, layer, tile)
    cnt = cnt[:, SUBLANES:SUBLANES + N_EXPERTS, 0].astype(jnp.int32)
    lo = jnp.cumsum(cnt, axis=1) - cnt
    seg = jnp.sum(cnt, axis=0)
    seg_pad = (seg + MOE_BLOCK - 1) // MOE_BLOCK * MOE_BLOCK
    seg_end = jnp.cumsum(seg_pad)
    seg_start = seg_end - seg_pad
    dst = seg_start[None, :] + jnp.cumsum(cnt, axis=0) - cnt
    n_blocks = (2 * n + N_EXPERTS * (MOE_BLOCK - 1)) // MOE_BLOCK
    blk_ids = jnp.arange(n_blocks, dtype=jnp.int32)
    blk_expert = jnp.minimum(
        jnp.sum((seg_end[None, :] <= blk_ids[:, None] * MOE_BLOCK).astype(jnp.int32), axis=1),
        N_EXPERTS - 1)
    n_used = seg_end[-1:] // MOE_BLOCK
    pad_start = jnp.concatenate([seg_start + seg, blk_ids * MOE_BLOCK])
    pad_rows = jnp.concatenate([seg_pad - seg, jnp.where(blk_ids >= n_used[0], MOE_BLOCK, 0)])
    flat = lambda a: a.reshape(-1)
    runs = (flat(lo), flat(cnt), flat(dst))
    s1, s2 = flat(meta_i[:, 4]), flat(meta_i[:, 5])
    w1, w2 = flat(meta_f[:, 2]), flat(meta_f[:, 3])
    xs = _dispatch(xn_slab, s1, s2, runs + (pad_start, pad_rows), tile, n_blocks * MOE_BLOCK)
    rows = _experts(xs, blk_expert, n_used, wts, layer)
    return _combine(yp, ys, rows, (s1, s2, w1, w2), runs, norm_f, final_norm, tile)


def _prep_weights(norm1, w_in, ln_g, ln_b, w_s, b_s, conv_w, conv_b, lru_lambda, w_rg_a, b_rg_a,
                  w_rg_x, b_rg_x, w_out, norm2, w_route_group, b_route_group, w_route_expert,
                  b_route_expert, w_gate, w_up, w_down, n_sample_t):
    depth, d = w_in.shape[0], w_in.shape[1]
    dh = d // H_A
    row = lambda a: a.reshape(depth, 1, -1).astype(F32)
    causal = jnp.tril(jnp.ones((CHUNK, CHUNK), dtype=bool))
    ws = jnp.where(causal, 0.5 * w_s, 0.0)
    bs = jnp.repeat(jnp.swapaxes(0.5 * b_s, 1, 2), dh, axis=2)
    in_scale = jnp.concatenate([jnp.ones((3 * d,), F32), jnp.full((2 * d,), 0.5, F32)])

    def pair_bd(w):
        w = w.reshape(depth, H_B // 2, 2, w.shape[-2], w.shape[-1])
        z = jnp.zeros_like(w[:, :, 0])
        return jnp.concatenate([jnp.concatenate([w[:, :, 0], z], axis=-1),
                                jnp.concatenate([z, w[:, :, 1]], axis=-1)], axis=-2)

    gap = SUBLANES - N_GROUPS
    tail = ROUTE_ROWS - SUBLANES - N_EXPERTS
    w_route = jnp.concatenate(
        [jnp.swapaxes(w_route_group, 1, 2), jnp.zeros((depth, gap, d), F32),
         jnp.swapaxes(w_route_expert, 1, 2), jnp.zeros((depth, tail, d), F32)], axis=1)
    b_route = jnp.concatenate(
        [b_route_group, jnp.zeros((depth, gap), F32), b_route_expert,
         jnp.zeros((depth, tail), F32)], axis=1)[..., None]
    nt = n_sample_t
    return dict(
        norm1=row(norm1), w_in=(w_in * in_scale).astype(BF16), ln_g=row(ln_g), ln_b=row(ln_b),
        ws=ws.astype(BF16), bs=bs.astype(F32), conv_w=conv_w.astype(F32), conv_b=row(conv_b),
        lam=row(lru_lambda),
        wbd=(0.5 * jnp.concatenate([pair_bd(w_rg_a), pair_bd(w_rg_x)], axis=-1)).astype(BF16),
        b_a=row(0.5 * b_rg_a), b_x=row(0.5 * b_rg_x), w_out=(0.5 * w_out).astype(BF16),
        norm2=row(norm2), w_route=w_route.astype(BF16), b_route=b_route.astype(F32),
        w_gate=w_gate, w_up=w_up, w_down=w_down,
        wsc=jnp.repeat(jnp.transpose(ws[:, :, :nt, :nt], (0, 2, 3, 1)), dh, axis=3).astype(F32),
        bsc=bs[:, :nt].astype(F32),
    )


def kernel(x_prompt, x_sample, state_lru_h, state_conv, norm1, w_in, ln_g, ln_b, w_s, b_s, conv_w, conv_b, lru_lambda, w_rg_a, b_rg_a, w_rg_x, b_rg_x, w_out, norm2, w_route_group, b_route_group, w_route_expert, b_route_expert, w_gate, w_up, w_down, norm_f):
    depth = w_in.shape[0]
    nb, seq, d = x_prompt.shape
    ns, nt, _ = x_sample.shape
    assert nt <= CHUNK
    nf = norm_f.reshape(1, d).astype(F32)
    wts = _prep_weights(norm1, w_in, ln_g, ln_b, w_s, b_s, conv_w, conv_b, lru_lambda, w_rg_a,
                        b_rg_a, w_rg_x, b_rg_x, w_out, norm2, w_route_group, b_route_group,
                        w_route_expert, b_route_expert, w_gate, w_up, w_down, nt)
    conv_state = jnp.transpose(state_conv, (0, 2, 1, 3))
    tile = min(MOE_TILE, nt * ns)
    yp = x_prompt
    ys = jnp.transpose(x_sample, (1, 0, 2))
    hp, cp, vp, hs, cs, vs = [], [], [], [], [], []
    for l in range(depth):
        yp, h, c, v = _mixer_prompt(yp, wts, l)
        hp.append(h)
        cp.append(c)
        vp.append(v)
        ys, h, c, v = _mixer_sample(ys, state_lru_h, conv_state, wts, l)
        hs.append(h)
        cs.append(c)
        vs.append(v)
        yp, ys = _moe(yp.reshape(nb * seq, d), ys.reshape(nt * ns, d), wts, l, nf,
                      l == depth - 1, tile)
        yp = yp.reshape(nb, seq, d)
        ys = ys.reshape(nt, ns, d)
    to_seq_major = lambda a: jnp.transpose(jnp.stack(a), (0, 2, 1, 3))
    return (yp, jnp.transpose(ys, (1, 0, 2)), jnp.stack(hp), jnp.stack(cp), jnp.stack(vp),
            jnp.stack(hs), to_seq_major(cs), to_seq_major(vs))
```

```python
import functools

import jax
import jax.numpy as jnp
from jax import lax
from jax.experimental import pallas as pl
from jax.experimental.pallas import tpu as pltpu

F32 = jnp.float32
BF16 = jnp.bfloat16

CHUNK = 128
H_A = 4
H_B = 8
CONV_W = 4
LRU_C = 8.0
N_GROUPS = 4
E_PER_GROUP = 8
N_EXPERTS = N_GROUPS * E_PER_GROUP
EPS = 1e-6

LANES = 128
SUBLANES = 8
SCAN_UNROLL = 4
SEQS_PER_TRIP = 4
ROUTE_ROWS = 48
MOE_TILE = 512
MOE_BLOCK = 512
VMEM_LIMIT = 60 * 1024 * 1024


def _dot(a, b):
    return jnp.dot(a, b, preferred_element_type=F32)


def _sigmoid(x):
    return 0.5 * (jnp.tanh(0.5 * x) + 1.0)


def _gelu2(x):
    c = 0.7978845608028654
    return x * (1.0 + jnp.tanh(x * (c + (c * 0.044715) * (x * x))))


def _rms_norm(x, g):
    return x * lax.rsqrt(jnp.mean(x * x, axis=-1, keepdims=True) + EPS) * g


def _layer_norm(x, g, b, eps):
    xc = x - jnp.mean(x, axis=-1, keepdims=True)
    var = jnp.mean(xc * xc, axis=-1, keepdims=True)
    return xc * lax.rsqrt(var + eps) * g + b


def _softplus(x):
    return jnp.maximum(x, 0.0) + jnp.log(1.0 + jnp.exp(-jnp.abs(x)))


def _lru_gates(xc, wbd_ref, b_a, b_x, lam_half):
    d = xc.shape[-1]
    xcb = xc.astype(BF16)
    r_parts, i_parts = [], []
    for q in range(d // 256):
        ri = _dot(xcb[:, q * 256:(q + 1) * 256], wbd_ref[q])
        r_parts.append(ri[:, :256])
        i_parts.append(ri[:, 256:])
    r2 = 1.0 + jnp.tanh(jnp.concatenate(r_parts, axis=-1) + b_a)
    i2 = 1.0 + jnp.tanh(jnp.concatenate(i_parts, axis=-1) + b_x)
    a = jnp.exp(lam_half * r2)
    mult_half = jnp.sqrt(0.25 - 0.25 * (a * a))
    return a, mult_half, i2 * xc


def _shift_rows(x, hist, j):
    sh = pltpu.roll(x, j, axis=0)
    top = jnp.where(lax.broadcasted_iota(jnp.int32, (SUBLANES, 1), 0) < j,
                    pltpu.roll(hist, j, axis=0), sh[:SUBLANES])
    return jnp.concatenate([top, sh[SUBLANES:]], axis=0)


def _mixer_prompt_kernel(x_ref, norm_ref, win_ref, lng_ref, lnb_ref, ws_ref, bs_ref,
                         cw_ref, cb_ref, lam_ref, wbd_ref, ba_ref, bx_ref, wout_ref,
                         y_ref, h_ref, conv_ref, v_ref,
                         hist_s, a_s, b_s, ya_s, gb_s, hc_s):
    i = pl.program_id(0)
    last = pl.num_programs(0) - 1
    nseq, _, d = x_ref.shape
    nlt = d // LANES
    spt = SEQS_PER_TRIP
    pair = spt * CHUNK

    @pl.when(i == 0)
    def _():
        hist_s[...] = jnp.zeros_like(hist_s)
        hc_s[...] = jnp.zeros_like(hc_s)

    lam_half = (-0.5 * LRU_C) * _softplus(-lam_ref[...])
    is_first = i == 0

    def branch_body(p, c):
        row = pl.multiple_of(p * pair, pair)
        x = jnp.concatenate([x_ref[spt * p + k] for k in range(spt)], axis=0)
        xnb = _rms_norm(x, norm_ref[...]).astype(BF16)
        v = _layer_norm(_gelu2(_dot(xnb, win_ref[:, d:2 * d])), lng_ref[...], lnb_ref[...],
                        4.0 * EPS)

        @pl.when(i == last)
        def _():
            for k in range(spt):
                v_ref[spt * p + k] = v[k * CHUNK:(k + 1) * CHUNK]

        vb = v.astype(BF16)
        dh = d // H_A
        z_rows = []
        for k in range(spt):
            zs = [_dot(ws_ref[h], vb[k * CHUNK:(k + 1) * CHUNK, h * dh:(h + 1) * dh])
                  for h in range(H_A)]
            z_rows.append(jnp.concatenate(zs, axis=-1) + bs_ref[...])
        z = jnp.concatenate(z_rows, axis=0)
        ya = _gelu2(_dot(xnb, win_ref[:, 0:d])) * z
        ga2 = 1.0 + jnp.tanh(_dot(xnb, win_ref[:, 3 * d:4 * d]))
        ya_s[pl.ds(row, pair), :] = (ga2 * ya).astype(BF16)
        gb_s[pl.ds(row, pair), :] = (1.0 + jnp.tanh(_dot(xnb, win_ref[:, 4 * d:5 * d]))).astype(BF16)
        xr = _dot(xnb, win_ref[:, 2 * d:3 * d])
        xcs = []
        for k in range(spt):
            s = spt * p + k
            xk = xr[k * CHUNK:(k + 1) * CHUNK]
            hist = hist_s[s]
            xc = cb_ref[...] + xk * cw_ref[CONV_W - 1:CONV_W, :]
            for j in range(1, CONV_W):
                xc = xc + _shift_rows(xk, hist, j) * cw_ref[CONV_W - 1 - j:CONV_W - j, :]
            hist_s[s] = xk[CHUNK - SUBLANES:]
            xcs.append(xc)
        xc = jnp.concatenate(xcs, axis=0)
        a, mult_half, ix2 = _lru_gates(xc, wbd_ref, ba_ref[...], bx_ref[...], lam_half)
        bt = mult_half * ix2
        for k in range(spt):
            s = spt * p + k
            for j in range(nlt):
                a_s[j, pl.ds(s, CHUNK, stride=nseq), :] = a[k * CHUNK:(k + 1) * CHUNK, j * LANES:(j + 1) * LANES]
                b_s[j, pl.ds(s, CHUNK, stride=nseq), :] = bt[k * CHUNK:(k + 1) * CHUNK, j * LANES:(j + 1) * LANES]

        @pl.when(is_first)
        def _():
            for k in range(spt):
                first = 0.5 * ix2[k * CHUNK:k * CHUNK + 1]
                for j in range(nlt):
                    b_s[j, pl.ds(spt * p + k, 1), :] = first[:, j * LANES:(j + 1) * LANES]

        return c

    lax.fori_loop(0, nseq // spt, branch_body, 0)

    def scan_body(tt, hs):
        for u in range(SCAN_UNROLL):
            row = pl.multiple_of((tt * SCAN_UNROLL + u) * nseq, nseq)
            out = []
            for j in range(nlt):
                h = a_s[j, pl.ds(row, nseq), :] * hs[j] + b_s[j, pl.ds(row, nseq), :]
                b_s[j, pl.ds(row, nseq), :] = h
                out.append(h)
            hs = tuple(out)
        return hs

    h0 = tuple(hc_s[:, j * LANES:(j + 1) * LANES] for j in range(nlt))
    hs = lax.fori_loop(0, CHUNK // SCAN_UNROLL, scan_body, h0)
    hfin = jnp.concatenate(hs, axis=-1)
    hc_s[...] = hfin
    h_ref[...] = hfin

    def out_body(p, c):
        row = pl.multiple_of(p * pair, pair)
        hrows = []
        for k in range(spt):
            hrows.append(jnp.concatenate(
                [b_s[j, pl.ds(spt * p + k, CHUNK, stride=nseq), :] for j in range(nlt)], axis=-1))
        h = jnp.concatenate(hrows, axis=0)
        merged = (ya_s[pl.ds(row, pair), :].astype(F32)
                  + gb_s[pl.ds(row, pair), :].astype(F32) * h).astype(BF16)
        o = _dot(merged, wout_ref[...])
        for k in range(spt):
            y_ref[spt * p + k] = x_ref[spt * p + k] + o[k * CHUNK:(k + 1) * CHUNK]
        return c

    lax.fori_loop(0, nseq // spt, out_body, 0)

    @pl.when(i == last)
    def _():
        conv_ref[...] = hist_s[:, SUBLANES - (CONV_W - 1):, :]


def _const_spec(shape):
    nd = len(shape)
    return pl.BlockSpec(shape, lambda i, *_, _n=nd: (0,) * _n, pipeline_mode=pl.Buffered(1))


def _layer_spec(stacked, layer):
    shape = stacked.shape[1:]
    nd = len(shape)
    return pl.BlockSpec((None,) + tuple(shape), lambda i, *_, _n=nd: (layer,) + (0,) * _n,
                        pipeline_mode=pl.Buffered(1))


MIXER_WEIGHTS = ('norm1', 'w_in', 'ln_g', 'ln_b', 'ws', 'bs', 'conv_w', 'conv_b', 'lam', 'wbd',
                 'b_a', 'b_x', 'w_out')


def _mixer_prompt(x, wts, layer):
    nseq, seq, d = x.shape
    assert seq % CHUNK == 0 and nseq == SUBLANES and d % 256 == 0
    n_chunks = seq // CHUNK
    nlt = d // LANES
    weights = tuple(wts[k] for k in MIXER_WEIGHTS)
    x_spec = pl.BlockSpec((nseq, CHUNK, d), lambda i: (0, i, 0))
    out_shape = (
        jax.ShapeDtypeStruct((nseq, seq, d), F32),
        jax.ShapeDtypeStruct((nseq, d), F32),
        jax.ShapeDtypeStruct((nseq, CONV_W - 1, d), F32),
        jax.ShapeDtypeStruct((nseq, CHUNK, d), F32),
    )
    out_specs = (
        x_spec,
        _const_spec((nseq, d)),
        _const_spec((nseq, CONV_W - 1, d)),
        _const_spec((nseq, CHUNK, d)),
    )
    scratch = [
        pltpu.VMEM((nseq, SUBLANES, d), F32),
        pltpu.VMEM((nlt, CHUNK * nseq, LANES), F32),
        pltpu.VMEM((nlt, CHUNK * nseq, LANES), F32),
        pltpu.VMEM((nseq * CHUNK, d), BF16),
        pltpu.VMEM((nseq * CHUNK, d), BF16),
        pltpu.VMEM((nseq, d), F32),
    ]
    return pl.pallas_call(
        _mixer_prompt_kernel,
        grid=(n_chunks,),
        in_specs=[x_spec] + [_layer_spec(w, layer) for w in weights],
        out_specs=out_specs,
        out_shape=out_shape,
        scratch_shapes=scratch,
        compiler_params=pltpu.CompilerParams(
            dimension_semantics=("arbitrary",), vmem_limit_bytes=VMEM_LIMIT),
        name="mixer_prompt",
    )(x, *weights)


def _mixer_sample_kernel(x_ref, h0_ref, cst_ref, norm_ref, win_ref, lng_ref, lnb_ref, wsc_ref,
                         bsc_ref, cw_ref, cb_ref, lam_ref, wbd_ref, ba_ref, bx_ref, wout_ref,
                         y_ref, h_ref, conv_ref, v_ref):
    nt, ns, d = x_ref.shape
    lam_half = (-0.5 * LRU_C) * _softplus(-lam_ref[...])
    g1 = norm_ref[...]
    h = h0_ref[...]
    xp = [cst_ref[k] for k in range(CONV_W - 1)]
    for t in range(nt):
        x = x_ref[t]
        xnb = _rms_norm(x, g1).astype(BF16)
        v = _layer_norm(_gelu2(_dot(xnb, win_ref[:, d:2 * d])), lng_ref[...], lnb_ref[...],
                        4.0 * EPS)
        v_ref[t] = v
        z = bsc_ref[t:t + 1, :]
        for s in range(t + 1):
            z = z + wsc_ref[t, s:s + 1, :] * v_ref[s]
        ya = _gelu2(_dot(xnb, win_ref[:, 0:d])) * z
        ga2 = 1.0 + jnp.tanh(_dot(xnb, win_ref[:, 3 * d:4 * d]))
        xr = _dot(xnb, win_ref[:, 2 * d:3 * d])
        xp.append(xr)
        xc = cb_ref[...]
        for k in range(CONV_W):
            xc = xc + xp[t + k] * cw_ref[k:k + 1, :]
        a, mult_half, ix2 = _lru_gates(xc, wbd_ref, ba_ref[...], bx_ref[...], lam_half)
        h = a * h + mult_half * ix2
        gb2 = 1.0 + jnp.tanh(_dot(xnb, win_ref[:, 4 * d:5 * d]))
        merged = (ga2 * ya + gb2 * h).astype(BF16)
        y_ref[t] = x + _dot(merged, wout_ref[...])
    h_ref[...] = h
    for k in range(CONV_W - 1):
        conv_ref[k] = xp[nt + k]


def _mixer_sample(x, h0, conv_state, wts, layer):
    nt, ns, d = x.shape
    names = MIXER_WEIGHTS[:4] + ('wsc', 'bsc') + MIXER_WEIGHTS[6:]
    stacked = (h0, conv_state) + tuple(wts[k] for k in names)
    out_shape = (
        jax.ShapeDtypeStruct((nt, ns, d), F32),
        jax.ShapeDtypeStruct((ns, d), F32),
        jax.ShapeDtypeStruct((CONV_W - 1, ns, d), F32),
        jax.ShapeDtypeStruct((nt, ns, d), F32),
    )
    return pl.pallas_call(
        _mixer_sample_kernel,
        grid=(1,),
        in_specs=[_const_spec(x.shape)] + [_layer_spec(w, layer) for w in stacked],
        out_specs=tuple(_const_spec(s.shape) for s in out_shape),
        out_shape=out_shape,
        compiler_params=pltpu.CompilerParams(
            dimension_semantics=("arbitrary",), vmem_limit_bytes=VMEM_LIMIT),
        name="mixer_sample",
    )(x, *stacked)


def _to_slab(ref, val):
    t, d = val.shape
    nlt = d // LANES
    for j in range(nlt):
        ref[pl.ds(j, t, stride=nlt), :] = val[:, j * LANES:(j + 1) * LANES]


def _from_slab(ref, t, nlt):
    return jnp.concatenate([ref[pl.ds(j, t, stride=nlt), :] for j in range(nlt)], axis=-1)


def _tokens_from_slab(slab_ref, nlt):
    return slab_ref[...].reshape(slab_ref.shape[0] // nlt, nlt, LANES).astype(BF16)


def _tokens_to_slab(slab_ref, tokens):
    slab_ref[...] = tokens.astype(F32).reshape(slab_ref.shape)


def _router_kernel(yp_ref, ys_ref, norm_ref, wr_ref, br_ref, tri_ref, low_ref, xn_ref, mf_ref,
                   mi_ref, cnt_ref, slab_s, *, n_prompt_tiles):
    y = jnp.where(pl.program_id(0) < n_prompt_tiles, yp_ref[...], ys_ref[...])
    xn = _rms_norm(y, norm_ref[...])
    _to_slab(slab_s, xn)
    xn_ref[...] = _tokens_from_slab(slab_s, xn_ref.shape[1])
    lt = lax.dot_general(wr_ref[...], xn.astype(BF16), (((1,), (1,)), ((), ())),
                         preferred_element_type=F32) + br_ref[...]
    tile = lt.shape[1]
    row = lax.broadcasted_iota(jnp.int32, (SUBLANES, tile), 0).astype(F32)
    neg = jnp.float32(-jnp.inf)
    big = jnp.float32(SUBLANES)

    def argmax_first(vals):
        m = jnp.max(vals, axis=0, keepdims=True)
        return m, jnp.min(jnp.where(vals == m, row, big), axis=0, keepdims=True)

    def group_block(x, gi):
        out = x[SUBLANES:2 * SUBLANES]
        for g in range(1, N_GROUPS):
            out = jnp.where(gi == g, x[(g + 1) * SUBLANES:(g + 2) * SUBLANES], out)
        return out

    gl = jnp.where(row < N_GROUPS, lt[0:SUBLANES], neg)
    gm, gi = argmax_first(gl)
    pg_top = 1.0 / jnp.sum(jnp.exp(gl - gm), axis=0, keepdims=True)
    el = group_block(lt, gi)
    m1, i1 = argmax_first(el)
    m2, i2 = argmax_first(jnp.where(row == i1, neg, el))
    e2 = jnp.exp(m2 - m1)
    w1 = pg_top / (1.0 + e2)
    w2 = pg_top * e2 / (1.0 + e2)
    hot1 = row == i1
    hot2 = row == i2
    picked = jnp.where(jnp.logical_or(hot1, hot2), 1.0, 0.0)
    zero = jnp.zeros_like(picked)
    onehot = jnp.concatenate(
        [zero] + [jnp.where(gi == g, picked, 0.0) for g in range(N_GROUPS)] + [zero],
        axis=0).astype(BF16)
    before = _dot(onehot, tri_ref[...])
    lower = jnp.sum(_dot(low_ref[...], onehot), axis=1, keepdims=True)
    slot = group_block(before + lower, gi)
    p1 = jnp.sum(jnp.where(hot1, slot, 0.0), axis=0, keepdims=True)
    p2 = jnp.sum(jnp.where(hot2, slot, 0.0), axis=0, keepdims=True)
    meta = jnp.zeros((SUBLANES, tile), F32)
    for k, c in enumerate((gi * E_PER_GROUP + i1, gi * E_PER_GROUP + i2, w1, w2, p1, p2)):
        meta = jnp.where(row == k, c, meta)
    mf_ref[0] = meta
    mi_ref[0] = meta.astype(jnp.int32)
    cnt = jnp.sum(onehot.astype(F32), axis=1, keepdims=True)
    cnt_ref[0] = jnp.broadcast_to(cnt, cnt_ref.shape[1:])


def _token_specs(tile, d, n_prompt_tiles):
    last = n_prompt_tiles - 1
    return (pl.BlockSpec((tile, d), lambda i, *_: (jnp.minimum(i, last), 0)),
            pl.BlockSpec((tile, d), lambda i, *_: (jnp.maximum(i - n_prompt_tiles, 0), 0)))


def _router(yp, ys, wts, layer, tile):
    d = yp.shape[1]
    nlt = d // LANES
    n_prompt_tiles = yp.shape[0] // tile
    n_tiles = n_prompt_tiles + ys.shape[0] // tile
    tri = jnp.triu(jnp.ones((tile, tile), BF16), 1)
    low = jnp.tril(jnp.ones((ROUTE_ROWS, ROUTE_ROWS), BF16), -1)
    meta_spec = pl.BlockSpec((1, SUBLANES, tile), lambda i: (i, 0, 0))
    return pl.pallas_call(
        functools.partial(_router_kernel, n_prompt_tiles=n_prompt_tiles),
        grid=(n_tiles,),
        in_specs=list(_token_specs(tile, d, n_prompt_tiles)) + [
            _layer_spec(wts['norm2'], layer), _layer_spec(wts['w_route'], layer),
            _layer_spec(wts['b_route'], layer), _const_spec((tile, tile)),
            _const_spec((ROUTE_ROWS, ROUTE_ROWS))],
        out_specs=(pl.BlockSpec((tile, nlt, LANES), lambda i: (i, 0, 0)), meta_spec, meta_spec,
                   pl.BlockSpec((1, ROUTE_ROWS, LANES), lambda i: (i, 0, 0))),
        out_shape=(jax.ShapeDtypeStruct((n_tiles * tile, nlt, LANES), BF16),
                   jax.ShapeDtypeStruct((n_tiles, SUBLANES, tile), F32),
                   jax.ShapeDtypeStruct((n_tiles, SUBLANES, tile), jnp.int32),
                   jax.ShapeDtypeStruct((n_tiles, ROUTE_ROWS, LANES), F32)),
        scratch_shapes=[pltpu.VMEM((tile * nlt, LANES), F32)],
        compiler_params=pltpu.CompilerParams(
            dimension_semantics=("arbitrary",), vmem_limit_bytes=VMEM_LIMIT),
        name="moe_router",
    )(yp, ys, wts['norm2'], wts['w_route'], wts['b_route'], tri, low)


def _run_copy(tile_idx, e, lo_ref, cnt_ref, dst_ref, local, remote, sem, to_remote):
    k = tile_idx * N_EXPERTS + e
    loc = local.at[pl.ds(lo_ref[k], cnt_ref[k])]
    rem = remote.at[pl.ds(dst_ref[k], cnt_ref[k])]
    return pltpu.make_async_copy(loc, rem, sem) if to_remote else pltpu.make_async_copy(rem, loc, sem)


def _start_runs(tile_idx, lo_ref, cnt_ref, dst_ref, local, remote, sem, to_remote):
    for e in range(N_EXPERTS):
        @pl.when(cnt_ref[tile_idx * N_EXPERTS + e] > 0)
        def _():
            _run_copy(tile_idx, e, lo_ref, cnt_ref, dst_ref, local, remote, sem, to_remote).start()


def _wait_runs(local, sem):
    pltpu.make_async_copy(local, local, sem).wait()


def _dispatch_kernel(lo_ref, cnt_ref, dst_ref, padst_ref, padn_ref,
                     s1_ref, s2_ref, x_ref, xs_ref, loc, zeros, sem, *, unroll):
    i = pl.program_id(0)
    last = pl.num_programs(0) - 1
    tile = x_ref.shape[0]
    slot = i % 2
    buf = loc.at[slot]

    @pl.when(i >= 2)
    def _():
        _wait_runs(buf, sem.at[slot])

    def body(tt, c):
        for u in range(unroll):
            t = tt * unroll + u
            row = x_ref[t]
            buf[s1_ref[t]] = row
            buf[s2_ref[t]] = row
        return c

    lax.fori_loop(0, tile // unroll, body, 0)
    _start_runs(i, lo_ref, cnt_ref, dst_ref, buf, xs_ref, sem.at[slot], True)

    @pl.when(i == last)
    def _():
        zeros[...] = jnp.zeros_like(zeros)

        def pad_copy(e):
            n = padn_ref[e]
            dst = xs_ref.at[pl.ds(padst_ref[e], n)]
            return pltpu.make_async_copy(zeros.at[pl.ds(0, n)], dst, sem.at[2])

        def each_pad(fn):
            def b(e, c):
                @pl.when(padn_ref[e] > 0)
                def _():
                    fn(e)
                return c
            lax.fori_loop(0, padn_ref.shape[0], b, 0)

        each_pad(lambda e: pad_copy(e).start())
        _wait_runs(buf, sem.at[slot])

        @pl.when(i >= 1)
        def _():
            _wait_runs(loc.at[1 - slot], sem.at[1 - slot])

        each_pad(lambda e: pad_copy(e).wait())


def _smem_tile(tile):
    return pl.BlockSpec((tile,), lambda i, *_: (i,), memory_space=pltpu.SMEM)


def _dispatch(xn_slab, s1, s2, tables, tile, n_rows):
    token = xn_slab.shape[1:]
    return pl.pallas_call(
        functools.partial(_dispatch_kernel, unroll=8),
        grid_spec=pltpu.PrefetchScalarGridSpec(
            num_scalar_prefetch=5,
            grid=(xn_slab.shape[0] // tile,),
            in_specs=[_smem_tile(tile), _smem_tile(tile),
                      pl.BlockSpec((tile,) + token, lambda i, *_: (i, 0, 0))],
            out_specs=pl.BlockSpec(memory_space=pl.ANY),
            scratch_shapes=[pltpu.VMEM((2, 2 * tile) + token, xn_slab.dtype),
                            pltpu.VMEM((MOE_BLOCK,) + token, xn_slab.dtype),
                            pltpu.SemaphoreType.DMA((3,))]),
        out_shape=jax.ShapeDtypeStruct((n_rows,) + token, xn_slab.dtype),
        compiler_params=pltpu.CompilerParams(
            dimension_semantics=("arbitrary",), vmem_limit_bytes=VMEM_LIMIT),
        name="moe_dispatch",
    )(*tables, s1, s2, xn_slab)


def _experts_kernel(blk_ref, nused_ref, xs_ref, wg_ref, wu_ref, wd_ref, ys_ref,
                    wg_s, wu_s, wd_s, slab_s):
    b = pl.program_id(0)
    rows, nlt, _ = xs_ref.shape

    @pl.when(jnp.logical_or(b == 0, blk_ref[b] != blk_ref[jnp.maximum(b - 1, 0)]))
    def _():
        wg_s[...] = wg_ref[...].astype(BF16)
        wu_s[...] = wu_ref[...].astype(BF16)
        wd_s[...] = wd_ref[...].astype(BF16)

    @pl.when(b < nused_ref[0])
    def _():
        _tokens_to_slab(slab_s, xs_ref[...])
        x = _from_slab(slab_s, rows, nlt).astype(BF16)
        hg = _dot(x, wg_s[...])
        hu = _dot(x, wu_s[...])
        h = (hg * _sigmoid(hg) * hu).astype(BF16)
        _to_slab(slab_s, _dot(h, wd_s[...]))
        ys_ref[...] = _tokens_from_slab(slab_s, nlt)

    @pl.when(b >= nused_ref[0])
    def _():
        ys_ref[...] = jnp.zeros_like(ys_ref)


def _experts(xs, blk_expert, n_used, wts, layer):
    n_rows, nlt, _ = xs.shape
    d = nlt * LANES
    f = wts['w_gate'].shape[-1]
    rows = pl.BlockSpec((MOE_BLOCK, nlt, LANES), lambda b, blk, nu: (b, 0, 0))
    w_spec = lambda r, c: pl.BlockSpec((None, None, r, c), lambda b, blk, nu: (layer, blk[b], 0, 0))
    return pl.pallas_call(
        _experts_kernel,
        grid_spec=pltpu.PrefetchScalarGridSpec(
            num_scalar_prefetch=2,
            grid=(n_rows // MOE_BLOCK,),
            in_specs=[rows, w_spec(d, f), w_spec(d, f), w_spec(f, d)],
            out_specs=rows,
            scratch_shapes=[pltpu.VMEM((d, f), BF16), pltpu.VMEM((d, f), BF16),
                            pltpu.VMEM((f, d), BF16),
                            pltpu.VMEM((MOE_BLOCK * nlt, LANES), F32)]),
        out_shape=jax.ShapeDtypeStruct(xs.shape, xs.dtype),
        compiler_params=pltpu.CompilerParams(
            dimension_semantics=("arbitrary",), vmem_limit_bytes=VMEM_LIMIT),
        name="moe_experts",
    )(blk_expert, n_used, xs, wts['w_gate'], wts['w_up'], wts['w_down'])


def _combine_kernel(lo_ref, cnt_ref, dst_ref, s1_ref, s2_ref, w1_ref, w2_ref, yp_ref, ys_ref,
                    nf_ref, rows_ref, op_ref, os_ref, loc, acc, sem,
                    *, unroll, final_norm, n_prompt_tiles):
    i = pl.program_id(0)
    tile = yp_ref.shape[0]
    nlt = loc.shape[2]
    slot = i % 2
    buf = loc.at[slot]
    fetch = lambda t, s: _start_runs(t, lo_ref, cnt_ref, dst_ref, loc.at[s], rows_ref, sem.at[s],
                                     False)

    @pl.when(i == 0)
    def _():
        fetch(0, 0)

    @pl.when(i + 1 < pl.num_programs(0))
    def _():
        fetch(i + 1, 1 - slot)

    _wait_runs(buf, sem.at[slot])

    def body(tt, c):
        for u in range(unroll):
            t = tt * unroll + u
            r1 = buf[s1_ref[t]].astype(F32)
            r2 = buf[s2_ref[t]].astype(F32)
            acc[pl.ds(pl.multiple_of(t * nlt, nlt), nlt), :] = w1_ref[t] * r1 + w2_ref[t] * r2
        return c

    lax.fori_loop(0, tile // unroll, body, 0)
    def finish(y_ref, o_ref):
        out = y_ref[...] + _from_slab(acc, tile, nlt)
        if final_norm:
            out = _rms_norm(out, nf_ref[...])
        o_ref[...] = out

    is_prompt = i < n_prompt_tiles
    pl.when(is_prompt)(lambda: finish(yp_ref, op_ref))
    pl.when(jnp.logical_not(is_prompt))(lambda: finish(ys_ref, os_ref))


def _combine(yp, ys, rows, meta, tables, norm_f, final_norm, tile):
    d = yp.shape[1]
    token = rows.shape[1:]
    n_prompt_tiles = yp.shape[0] // tile
    n_tiles = n_prompt_tiles + ys.shape[0] // tile
    tok_p, tok_s = _token_specs(tile, d, n_prompt_tiles)
    return pl.pallas_call(
        functools.partial(_combine_kernel, unroll=8, final_norm=final_norm,
                          n_prompt_tiles=n_prompt_tiles),
        grid_spec=pltpu.PrefetchScalarGridSpec(
            num_scalar_prefetch=3,
            grid=(n_tiles,),
            in_specs=[_smem_tile(tile)] * 4 + [tok_p, tok_s, _const_spec((1, d)),
                                               pl.BlockSpec(memory_space=pl.ANY)],
            out_specs=(tok_p, tok_s),
            scratch_shapes=[pltpu.VMEM((2, 2 * tile) + token, rows.dtype),
                            pltpu.VMEM((tile * token[0], LANES), F32),
                            pltpu.SemaphoreType.DMA((2,))]),
        out_shape=(jax.ShapeDtypeStruct(yp.shape, F32), jax.ShapeDtypeStruct(ys.shape, F32)),
        compiler_params=pltpu.CompilerParams(
            dimension_semantics=("arbitrary",), vmem_limit_bytes=VMEM_LIMIT),
        name="moe_combine",
    )(*tables, *meta, yp, ys, norm_f, rows)


def _moe(yp, ys, wts, layer, norm_f, final_norm, tile):
    d = yp.shape[1]
    n = yp.shape[0] + ys.shape[0]
    assert yp.shape[0] % tile == 0 and ys.shape[0] % tile == 0 and d % LANES == 0
    xn_slab, meta_f, meta_i, cnt = _router(yp, ys, wts, layer, tile)
    cnt = cnt[:, SUBLANES:SUBLANES + N_EXPERTS, 0].astype(jnp.int32)
    lo = jnp.cumsum(cnt, axis=1) - cnt
    seg = jnp.sum(cnt, axis=0)
    seg_pad = (seg + MOE_BLOCK - 1) // MOE_BLOCK * MOE_BLOCK
    seg_end = jnp.cumsum(seg_pad)
    seg_start = seg_end - seg_pad
    dst = seg_start[None, :] + jnp.cumsum(cnt, axis=0) - cnt
    n_blocks = (2 * n + N_EXPERTS * (MOE_BLOCK - 1)) // MOE_BLOCK
    blk_ids = jnp.arange(n_blocks, dtype=jnp.int32)
    blk_expert = jnp.minimum(
        jnp.sum((seg_end[None, :] <= blk_ids[:, None] * MOE_BLOCK).astype(jnp.int32), axis=1),
        N_EXPERTS - 1)
    n_used = seg_end[-1:] // MOE_BLOCK
    pad_start = jnp.concatenate([seg_start + seg, blk_ids * MOE_BLOCK])
    pad_rows = jnp.concatenate([seg_pad - seg, jnp.where(blk_ids >= n_used[0], MOE_BLOCK, 0)])
    flat = lambda a: a.reshape(-1)
    runs = (flat(lo), flat(cnt), flat(dst))
    s1, s2 = flat(meta_i[:, 4]), flat(meta_i[:, 5])
    w1, w2 = flat(meta_f[:, 2]), flat(meta_f[:, 3])
    xs = _dispatch(xn_slab, s1, s2, runs + (pad_start, pad_rows), tile, n_blocks * MOE_BLOCK)
    rows = _experts(xs, blk_expert, n_used, wts, layer)
    return _combine(yp, ys, rows, (s1, s2, w1, w2), runs, norm_f, final_norm, tile)


def _prep_weights(norm1, w_in, ln_g, ln_b, w_s, b_s, conv_w, conv_b, lru_lambda, w_rg_a, b_rg_a,
                  w_rg_x, b_rg_x, w_out, norm2, w_route_group, b_route_group, w_route_expert,
                  b_route_expert, w_gate, w_up, w_down, n_sample_t):
    depth, d = w_in.shape[0], w_in.shape[1]
    dh = d // H_A
    row = lambda a: a.reshape(depth, 1, -1).astype(F32)
    causal = jnp.tril(jnp.ones((CHUNK, CHUNK), dtype=bool))
    ws = jnp.where(causal, 0.5 * w_s, 0.0)
    bs = jnp.repeat(jnp.swapaxes(0.5 * b_s, 1, 2), dh, axis=2)
    in_scale = jnp.concatenate([jnp.ones((3 * d,), F32), jnp.full((2 * d,), 0.5, F32)])

    def pair_bd(w):
        w = w.reshape(depth, H_B // 2, 2, w.shape[-2], w.shape[-1])
        z = jnp.zeros_like(w[:, :, 0])
        return jnp.concatenate([jnp.concatenate([w[:, :, 0], z], axis=-1),
                                jnp.concatenate([z, w[:, :, 1]], axis=-1)], axis=-2)

    gap = SUBLANES - N_GROUPS
    tail = ROUTE_ROWS - SUBLANES - N_EXPERTS
    w_route = jnp.concatenate(
        [jnp.swapaxes(w_route_group, 1, 2), jnp.zeros((depth, gap, d), F32),
         jnp.swapaxes(w_route_expert, 1, 2), jnp.zeros((depth, tail, d), F32)], axis=1)
    b_route = jnp.concatenate(
        [b_route_group, jnp.zeros((depth, gap), F32), b_route_expert,
         jnp.zeros((depth, tail), F32)], axis=1)[..., None]
    nt = n_sample_t
    return dict(
        norm1=row(norm1), w_in=(w_in * in_scale).astype(BF16), ln_g=row(ln_g), ln_b=row(ln_b),
        ws=ws.astype(BF16), bs=bs.astype(F32), conv_w=conv_w.astype(F32), conv_b=row(conv_b),
        lam=row(lru_lambda),
        wbd=(0.5 * jnp.concatenate([pair_bd(w_rg_a), pair_bd(w_rg_x)], axis=-1)).astype(BF16),
        b_a=row(0.5 * b_rg_a), b_x=row(0.5 * b_rg_x), w_out=(0.5 * w_out).astype(BF16),
        norm2=row(norm2), w_route=w_route.astype(BF16), b_route=b_route.astype(F32),
        w_gate=w_gate, w_up=w_up, w_down=w_down,
        wsc=jnp.repeat(jnp.transpose(ws[:, :, :nt, :nt], (0, 2, 3, 1)), dh, axis=3).astype(F32),
        bsc=bs[:, :nt].astype(F32),
    )


def kernel(x_prompt, x_sample, state_lru_h, state_conv, norm1, w_in, ln_g, ln_b, w_s, b_s, conv_w, conv_b, lru_lambda, w_rg_a, b_rg_a, w_rg_x, b_rg_x, w_out, norm2, w_route_group, b_route_group, w_route_expert, b_route_expert, w_gate, w_up, w_down, norm_f):
    depth = w_in.shape[0]
    nb, seq, d = x_prompt.shape
    ns, nt, _ = x_sample.shape
    assert nt <= CHUNK
    nf = norm_f.reshape(1, d).astype(F32)
    wts = _prep_weights(norm1, w_in, ln_g, ln_b, w_s, b_s, conv_w, conv_b, lru_lambda, w_rg_a,
                        b_rg_a, w_rg_x, b_rg_x, w_out, norm2, w_route_group, b_route_group,
                        w_route_expert, b_route_expert, w_gate, w_up, w_down, nt)
    conv_state = jnp.transpose(state_conv, (0, 2, 1, 3))
    tile = min(MOE_TILE, nt * ns)
    yp = x_prompt
    ys = jnp.transpose(x_sample, (1, 0, 2))
    hp, cp, vp, hs, cs, vs = [], [], [], [], [], []
    for l in range(depth):
        yp, h, c, v = _mixer_prompt(yp, wts, l)
        hp.append(h)
        cp.append(c)
        vp.append(v)
        ys, h, c, v = _mixer_sample(ys, state_lru_h, conv_state, wts, l)
        hs.append(h)
        cs.append(c)
        vs.append(v)
        yp, ys = _moe(yp.reshape(nb * seq, d), ys.reshape(nt * ns, d), wts, l, nf,
                      l == depth - 1, tile)
        yp = yp.reshape(nb, seq, d)
        ys = ys.reshape(nt, ns, d)
    to_seq_major = lambda a: jnp.transpose(jnp.stack(a), (0, 2, 1, 3))
    return (yp, jnp.transpose(ys, (1, 0, 2)), jnp.stack(hp), jnp.stack(cp), jnp.stack(vp),
            jnp.stack(hs), to_seq_major(cs), to_seq_major(vs))
```

```python
import functools

import jax
import jax.numpy as jnp
from jax import lax
from jax.experimental import pallas as pl
from jax.experimental.pallas import tpu as pltpu

F32 = jnp.float32
BF16 = jnp.bfloat16

CHUNK = 128
H_A = 4
H_B = 8
CONV_W = 4
LRU_C = 8.0
N_GROUPS = 4
E_PER_GROUP = 8
N_EXPERTS = N_GROUPS * E_PER_GROUP
EPS = 1e-6

LANES = 128
SUBLANES = 8
SCAN_UNROLL = 4
SEQS_PER_TRIP = 8
ROUTE_ROWS = 48
MOE_TILE = 512
MOE_BLOCK = 512
VMEM_LIMIT = 60 * 1024 * 1024


def _dot(a, b):
    return jnp.dot(a, b, preferred_element_type=F32)


def _sigmoid(x):
    return 0.5 * (jnp.tanh(0.5 * x) + 1.0)


def _gelu2(x):
    c = 0.7978845608028654
    return x * (1.0 + jnp.tanh(x * (c + (c * 0.044715) * (x * x))))


def _rms_norm(x, g):
    return x * lax.rsqrt(jnp.mean(x * x, axis=-1, keepdims=True) + EPS) * g


def _layer_norm(x, g, b, eps):
    xc = x - jnp.mean(x, axis=-1, keepdims=True)
    var = jnp.mean(xc * xc, axis=-1, keepdims=True)
    return xc * lax.rsqrt(var + eps) * g + b


def _softplus(x):
    return jnp.maximum(x, 0.0) + jnp.log(1.0 + jnp.exp(-jnp.abs(x)))


def _lru_gates(xc, wbd_ref, b_a, b_x, lam_half):
    d = xc.shape[-1]
    xcb = xc.astype(BF16)
    r_parts, i_parts = [], []
    for q in range(d // 256):
        ri = _dot(xcb[:, q * 256:(q + 1) * 256], wbd_ref[q])
        r_parts.append(ri[:, :256])
        i_parts.append(ri[:, 256:])
    r2 = 1.0 + jnp.tanh(jnp.concatenate(r_parts, axis=-1) + b_a)
    i2 = 1.0 + jnp.tanh(jnp.concatenate(i_parts, axis=-1) + b_x)
    a = jnp.exp(lam_half * r2)
    mult_half = jnp.sqrt(0.25 - 0.25 * (a * a))
    return a, mult_half, i2 * xc


def _shift_rows(x, hist, j):
    sh = pltpu.roll(x, j, axis=0)
    top = jnp.where(lax.broadcasted_iota(jnp.int32, (SUBLANES, 1), 0) < j,
                    pltpu.roll(hist, j, axis=0), sh[:SUBLANES])
    return jnp.concatenate([top, sh[SUBLANES:]], axis=0)


def _mixer_prompt_kernel(x_ref, norm_ref, win_ref, lng_ref, lnb_ref, ws_ref, bs_ref,
                         cw_ref, cb_ref, lam_ref, wbd_ref, ba_ref, bx_ref, wout_ref,
                         y_ref, h_ref, conv_ref, v_ref,
                         hist_s, a_s, b_s, ya_s, gb_s, hc_s):
    i = pl.program_id(0)
    last = pl.num_programs(0) - 1
    nseq, _, d = x_ref.shape
    nlt = d // LANES
    spt = SEQS_PER_TRIP
    pair = spt * CHUNK

    @pl.when(i == 0)
    def _():
        hist_s[...] = jnp.zeros_like(hist_s)
        hc_s[...] = jnp.zeros_like(hc_s)

    lam_half = (-0.5 * LRU_C) * _softplus(-lam_ref[...])
    is_first = i == 0

    def branch_body(p, c):
        row = pl.multiple_of(p * pair, pair)
        x = jnp.concatenate([x_ref[spt * p + k] for k in range(spt)], axis=0)
        xnb = _rms_norm(x, norm_ref[...]).astype(BF16)
        v = _layer_norm(_gelu2(_dot(xnb, win_ref[:, d:2 * d])), lng_ref[...], lnb_ref[...],
                        4.0 * EPS)

        @pl.when(i == last)
        def _():
            for k in range(spt):
                v_ref[spt * p + k] = v[k * CHUNK:(k + 1) * CHUNK]

        vb = v.astype(BF16)
        dh = d // H_A
        z_rows = []
        for k in range(spt):
            zs = [_dot(ws_ref[h], vb[k * CHUNK:(k + 1) * CHUNK, h * dh:(h + 1) * dh])
                  for h in range(H_A)]
            z_rows.append(jnp.concatenate(zs, axis=-1) + bs_ref[...])
        z = jnp.concatenate(z_rows, axis=0)
        ya = _gelu2(_dot(xnb, win_ref[:, 0:d])) * z
        ga2 = 1.0 + jnp.tanh(_dot(xnb, win_ref[:, 3 * d:4 * d]))
        ya_s[pl.ds(row, pair), :] = (ga2 * ya).astype(BF16)
        gb_s[pl.ds(row, pair), :] = (1.0 + jnp.tanh(_dot(xnb, win_ref[:, 4 * d:5 * d]))).astype(BF16)
        xr = _dot(xnb, win_ref[:, 2 * d:3 * d])
        xcs = []
        for k in range(spt):
            s = spt * p + k
            xk = xr[k * CHUNK:(k + 1) * CHUNK]
            hist = hist_s[s]
            xc = cb_ref[...] + xk * cw_ref[CONV_W - 1:CONV_W, :]
            for j in range(1, CONV_W):
                xc = xc + _shift_rows(xk, hist, j) * cw_ref[CONV_W - 1 - j:CONV_W - j, :]
            hist_s[s] = xk[CHUNK - SUBLANES:]
            xcs.append(xc)
        xc = jnp.concatenate(xcs, axis=0)
        a, mult_half, ix2 = _lru_gates(xc, wbd_ref, ba_ref[...], bx_ref[...], lam_half)
        bt = mult_half * ix2
        for k in range(spt):
            s = spt * p + k
            for j in range(nlt):
                a_s[j, pl.ds(s, CHUNK, stride=nseq), :] = a[k * CHUNK:(k + 1) * CHUNK, j * LANES:(j + 1) * LANES]
                b_s[j, pl.ds(s, CHUNK, stride=nseq), :] = bt[k * CHUNK:(k + 1) * CHUNK, j * LANES:(j + 1) * LANES]

        @pl.when(is_first)
        def _():
            for k in range(spt):
                first = 0.5 * ix2[k * CHUNK:k * CHUNK + 1]
                for j in range(nlt):
                    b_s[j, pl.ds(spt * p + k, 1), :] = first[:, j * LANES:(j + 1) * LANES]

        return c

    lax.fori_loop(0, nseq // spt, branch_body, 0)

    def scan_body(tt, hs):
        for u in range(SCAN_UNROLL):
            row = pl.multiple_of((tt * SCAN_UNROLL + u) * nseq, nseq)
            out = []
            for j in range(nlt):
                h = a_s[j, pl.ds(row, nseq), :] * hs[j] + b_s[j, pl.ds(row, nseq), :]
                b_s[j, pl.ds(row, nseq), :] = h
                out.append(h)
            hs = tuple(out)
        return hs

    h0 = tuple(hc_s[:, j * LANES:(j + 1) * LANES] for j in range(nlt))
    hs = lax.fori_loop(0, CHUNK // SCAN_UNROLL, scan_body, h0)
    hfin = jnp.concatenate(hs, axis=-1)
    hc_s[...] = hfin
    h_ref[...] = hfin

    def out_body(p, c):
        row = pl.multiple_of(p * pair, pair)
        hrows = []
        for k in range(spt):
            hrows.append(jnp.concatenate(
                [b_s[j, pl.ds(spt * p + k, CHUNK, stride=nseq), :] for j in range(nlt)], axis=-1))
        h = jnp.concatenate(hrows, axis=0)
        merged = (ya_s[pl.ds(row, pair), :].astype(F32)
                  + gb_s[pl.ds(row, pair), :].astype(F32) * h).astype(BF16)
        o = _dot(merged, wout_ref[...])
        for k in range(spt):
            y_ref[spt * p + k] = x_ref[spt * p + k] + o[k * CHUNK:(k + 1) * CHUNK]
        return c

    lax.fori_loop(0, nseq // spt, out_body, 0)

    @pl.when(i == last)
    def _():
        conv_ref[...] = hist_s[:, SUBLANES - (CONV_W - 1):, :]


def _const_spec(shape):
    nd = len(shape)
    return pl.BlockSpec(shape, lambda i, *_, _n=nd: (0,) * _n, pipeline_mode=pl.Buffered(1))


def _layer_spec(stacked, layer):
    shape = stacked.shape[1:]
    nd = len(shape)
    return pl.BlockSpec((None,) + tuple(shape), lambda i, *_, _n=nd: (layer,) + (0,) * _n,
                        pipeline_mode=pl.Buffered(1))


MIXER_WEIGHTS = ('norm1', 'w_in', 'ln_g', 'ln_b', 'ws', 'bs', 'conv_w', 'conv_b', 'lam', 'wbd',
                 'b_a', 'b_x', 'w_out')


def _mixer_prompt(x, wts, layer):
    nseq, seq, d = x.shape
    assert seq % CHUNK == 0 and nseq == SUBLANES and d % 256 == 0
    n_chunks = seq // CHUNK
    nlt = d // LANES
    weights = tuple(wts[k] for k in MIXER_WEIGHTS)
    x_spec = pl.BlockSpec((nseq, CHUNK, d), lambda i: (0, i, 0))
    out_shape = (
        jax.ShapeDtypeStruct((nseq, seq, d), F32),
        jax.ShapeDtypeStruct((nseq, d), F32),
        jax.ShapeDtypeStruct((nseq, CONV_W - 1, d), F32),
        jax.ShapeDtypeStruct((nseq, CHUNK, d), F32),
    )
    out_specs = (
        x_spec,
        _const_spec((nseq, d)),
        _const_spec((nseq, CONV_W - 1, d)),
        _const_spec((nseq, CHUNK, d)),
    )
    scratch = [
        pltpu.VMEM((nseq, SUBLANES, d), F32),
        pltpu.VMEM((nlt, CHUNK * nseq, LANES), F32),
        pltpu.VMEM((nlt, CHUNK * nseq, LANES), F32),
        pltpu.VMEM((nseq * CHUNK, d), BF16),
        pltpu.VMEM((nseq * CHUNK, d), BF16),
        pltpu.VMEM((nseq, d), F32),
    ]
    return pl.pallas_call(
        _mixer_prompt_kernel,
        grid=(n_chunks,),
        in_specs=[x_spec] + [_layer_spec(w, layer) for w in weights],
        out_specs=out_specs,
        out_shape=out_shape,
        scratch_shapes=scratch,
        compiler_params=pltpu.CompilerParams(
            dimension_semantics=("arbitrary",), vmem_limit_bytes=VMEM_LIMIT),
        name="mixer_prompt",
    )(x, *weights)


def _mixer_sample_kernel(x_ref, h0_ref, cst_ref, norm_ref, win_ref, lng_ref, lnb_ref, wsc_ref,
                         bsc_ref, cw_ref, cb_ref, lam_ref, wbd_ref, ba_ref, bx_ref, wout_ref,
                         y_ref, h_ref, conv_ref, v_ref):
    nt, ns, d = x_ref.shape
    lam_half = (-0.5 * LRU_C) * _softplus(-lam_ref[...])
    g1 = norm_ref[...]
    h = h0_ref[...]
    xp = [cst_ref[k] for k in range(CONV_W - 1)]
    for t in range(nt):
        x = x_ref[t]
        xnb = _rms_norm(x, g1).astype(BF16)
        v = _layer_norm(_gelu2(_dot(xnb, win_ref[:, d:2 * d])), lng_ref[...], lnb_ref[...],
                        4.0 * EPS)
        v_ref[t] = v
        z = bsc_ref[t:t + 1, :]
        for s in range(t + 1):
            z = z + wsc_ref[t, s:s + 1, :] * v_ref[s]
        ya = _gelu2(_dot(xnb, win_ref[:, 0:d])) * z
        ga2 = 1.0 + jnp.tanh(_dot(xnb, win_ref[:, 3 * d:4 * d]))
        xr = _dot(xnb, win_ref[:, 2 * d:3 * d])
        xp.append(xr)
        xc = cb_ref[...]
        for k in range(CONV_W):
            xc = xc + xp[t + k] * cw_ref[k:k + 1, :]
        a, mult_half, ix2 = _lru_gates(xc, wbd_ref, ba_ref[...], bx_ref[...], lam_half)
        h = a * h + mult_half * ix2
        gb2 = 1.0 + jnp.tanh(_dot(xnb, win_ref[:, 4 * d:5 * d]))
        merged = (ga2 * ya + gb2 * h).astype(BF16)
        y_ref[t] = x + _dot(merged, wout_ref[...])
    h_ref[...] = h
    for k in range(CONV_W - 1):
        conv_ref[k] = xp[nt + k]


def _mixer_sample(x, h0, conv_state, wts, layer):
    nt, ns, d = x.shape
    names = MIXER_WEIGHTS[:4] + ('wsc', 'bsc') + MIXER_WEIGHTS[6:]
    stacked = (h0, conv_state) + tuple(wts[k] for k in names)
    out_shape = (
        jax.ShapeDtypeStruct((nt, ns, d), F32),
        jax.ShapeDtypeStruct((ns, d), F32),
        jax.ShapeDtypeStruct((CONV_W - 1, ns, d), F32),
        jax.ShapeDtypeStruct((nt, ns, d), F32),
    )
    return pl.pallas_call(
        _mixer_sample_kernel,
        grid=(1,),
        in_specs=[_const_spec(x.shape)] + [_layer_spec(w, layer) for w in stacked],
        out_specs=tuple(_const_spec(s.shape) for s in out_shape),
        out_shape=out_shape,
        compiler_params=pltpu.CompilerParams(
            dimension_semantics=("arbitrary",), vmem_limit_bytes=VMEM_LIMIT),
        name="mixer_sample",
    )(x, *stacked)


def _to_slab(ref, val):
    t, d = val.shape
    nlt = d // LANES
    for j in range(nlt):
        ref[pl.ds(j, t, stride=nlt), :] = val[:, j * LANES:(j + 1) * LANES]


def _from_slab(ref, t, nlt):
    return jnp.concatenate([ref[pl.ds(j, t, stride=nlt), :] for j in range(nlt)], axis=-1)


def _tokens_from_slab(slab_ref, nlt):
    return slab_ref[...].reshape(slab_ref.shape[0] // nlt, nlt, LANES).astype(BF16)


def _tokens_to_slab(slab_ref, tokens):
    slab_ref[...] = tokens.astype(F32).reshape(slab_ref.shape)


def _router_kernel(yp_ref, ys_ref, norm_ref, wr_ref, br_ref, tri_ref, low_ref, xn_ref, mf_ref,
                   mi_ref, cnt_ref, slab_s, *, n_prompt_tiles):
    y = jnp.where(pl.program_id(0) < n_prompt_tiles, yp_ref[...], ys_ref[...])
    xn = _rms_norm(y, norm_ref[...])
    _to_slab(slab_s, xn)
    xn_ref[...] = _tokens_from_slab(slab_s, xn_ref.shape[1])
    lt = lax.dot_general(wr_ref[...], xn.astype(BF16), (((1,), (1,)), ((), ())),
                         preferred_element_type=F32) + br_ref[...]
    tile = lt.shape[1]
    row = lax.broadcasted_iota(jnp.int32, (SUBLANES, tile), 0).astype(F32)
    neg = jnp.float32(-jnp.inf)
    big = jnp.float32(SUBLANES)

    def argmax_first(vals):
        m = jnp.max(vals, axis=0, keepdims=True)
        return m, jnp.min(jnp.where(vals == m, row, big), axis=0, keepdims=True)

    def group_block(x, gi):
        out = x[SUBLANES:2 * SUBLANES]
        for g in range(1, N_GROUPS):
            out = jnp.where(gi == g, x[(g + 1) * SUBLANES:(g + 2) * SUBLANES], out)
        return out

    gl = jnp.where(row < N_GROUPS, lt[0:SUBLANES], neg)
    gm, gi = argmax_first(gl)
    pg_top = 1.0 / jnp.sum(jnp.exp(gl - gm), axis=0, keepdims=True)
    el = group_block(lt, gi)
    m1, i1 = argmax_first(el)
    m2, i2 = argmax_first(jnp.where(row == i1, neg, el))
    e2 = jnp.exp(m2 - m1)
    w1 = pg_top / (1.0 + e2)
    w2 = pg_top * e2 / (1.0 + e2)
    hot1 = row == i1
    hot2 = row == i2
    picked = jnp.where(jnp.logical_or(hot1, hot2), 1.0, 0.0)
    zero = jnp.zeros_like(picked)
    onehot = jnp.concatenate(
        [zero] + [jnp.where(gi == g, picked, 0.0) for g in range(N_GROUPS)] + [zero],
        axis=0).astype(BF16)
    before = _dot(onehot, tri_ref[...])
    lower = jnp.sum(_dot(low_ref[...], onehot), axis=1, keepdims=True)
    slot = group_block(before + lower, gi)
    p1 = jnp.sum(jnp.where(hot1, slot, 0.0), axis=0, keepdims=True)
    p2 = jnp.sum(jnp.where(hot2, slot, 0.0), axis=0, keepdims=True)
    meta = jnp.zeros((SUBLANES, tile), F32)
    for k, c in enumerate((gi * E_PER_GROUP + i1, gi * E_PER_GROUP + i2, w1, w2, p1, p2)):
        meta = jnp.where(row == k, c, meta)
    mf_ref[0] = meta
    mi_ref[0] = meta.astype(jnp.int32)
    cnt = jnp.sum(onehot.astype(F32), axis=1, keepdims=True)
    cnt_ref[0] = jnp.broadcast_to(cnt, cnt_ref.shape[1:])


def _token_specs(tile, d, n_prompt_tiles):
    last = n_prompt_tiles - 1
    return (pl.BlockSpec((tile, d), lambda i, *_: (jnp.minimum(i, last), 0)),
            pl.BlockSpec((tile, d), lambda i, *_: (jnp.maximum(i - n_prompt_tiles, 0), 0)))


def _router(yp, ys, wts, layer, tile):
    d = yp.shape[1]
    nlt = d // LANES
    n_prompt_tiles = yp.shape[0] // tile
    n_tiles = n_prompt_tiles + ys.shape[0] // tile
    tri = jnp.triu(jnp.ones((tile, tile), BF16), 1)
    low = jnp.tril(jnp.ones((ROUTE_ROWS, ROUTE_ROWS), BF16), -1)
    meta_spec = pl.BlockSpec((1, SUBLANES, tile), lambda i: (i, 0, 0))
    return pl.pallas_call(
        functools.partial(_router_kernel, n_prompt_tiles=n_prompt_tiles),
        grid=(n_tiles,),
        in_specs=list(_token_specs(tile, d, n_prompt_tiles)) + [
            _layer_spec(wts['norm2'], layer), _layer_spec(wts['w_route'], layer),
            _layer_spec(wts['b_route'], layer), _const_spec((tile, tile)),
            _const_spec((ROUTE_ROWS, ROUTE_ROWS))],
        out_specs=(pl.BlockSpec((tile, nlt, LANES), lambda i: (i, 0, 0)), meta_spec, meta_spec,
                   pl.BlockSpec((1, ROUTE_ROWS, LANES), lambda i: (i, 0, 0))),
        out_shape=(jax.ShapeDtypeStruct((n_tiles * tile, nlt, LANES), BF16),
                   jax.ShapeDtypeStruct((n_tiles, SUBLANES, tile), F32),
                   jax.ShapeDtypeStruct((n_tiles, SUBLANES, tile), jnp.int32),
                   jax.ShapeDtypeStruct((n_tiles, ROUTE_ROWS, LANES), F32)),
        scratch_shapes=[pltpu.VMEM((tile * nlt, LANES), F32)],
        compiler_params=pltpu.CompilerParams(
            dimension_semantics=("arbitrary",), vmem_limit_bytes=VMEM_LIMIT),
        name="moe_router",
    )(yp, ys, wts['norm2'], wts['w_route'], wts['b_route'], tri, low)


def _run_copy(tile_idx, e, lo_ref, cnt_ref, dst_ref, local, remote, sem, to_remote):
    k = tile_idx * N_EXPERTS + e
    loc = local.at[pl.ds(lo_ref[k], cnt_ref[k])]
    rem = remote.at[pl.ds(dst_ref[k], cnt_ref[k])]
    return pltpu.make_async_copy(loc, rem, sem) if to_remote else pltpu.make_async_copy(rem, loc, sem)


def _start_runs(tile_idx, lo_ref, cnt_ref, dst_ref, local, remote, sem, to_remote):
    for e in range(N_EXPERTS):
        @pl.when(cnt_ref[tile_idx * N_EXPERTS + e] > 0)
        def _():
            _run_copy(tile_idx, e, lo_ref, cnt_ref, dst_ref, local, remote, sem, to_remote).start()


def _wait_runs(local, sem):
    pltpu.make_async_copy(local, local, sem).wait()


def _dispatch_kernel(lo_ref, cnt_ref, dst_ref, padst_ref, padn_ref,
                     s1_ref, s2_ref, x_ref, xs_ref, loc, zeros, sem, *, unroll):
    i = pl.program_id(0)
    last = pl.num_programs(0) - 1
    tile = x_ref.shape[0]
    slot = i % 2
    buf = loc.at[slot]

    @pl.when(i >= 2)
    def _():
        _wait_runs(buf, sem.at[slot])

    def body(tt, c):
        for u in range(unroll):
            t = tt * unroll + u
            row = x_ref[t]
            buf[s1_ref[t]] = row
            buf[s2_ref[t]] = row
        return c

    lax.fori_loop(0, tile // unroll, body, 0)
    _start_runs(i, lo_ref, cnt_ref, dst_ref, buf, xs_ref, sem.at[slot], True)

    @pl.when(i == last)
    def _():
        zeros[...] = jnp.zeros_like(zeros)

        def pad_copy(e):
            n = padn_ref[e]
            dst = xs_ref.at[pl.ds(padst_ref[e], n)]
            return pltpu.make_async_copy(zeros.at[pl.ds(0, n)], dst, sem.at[2])

        def each_pad(fn):
            def b(e, c):
                @pl.when(padn_ref[e] > 0)
                def _():
                    fn(e)
                return c
            lax.fori_loop(0, padn_ref.shape[0], b, 0)

        each_pad(lambda e: pad_copy(e).start())
        _wait_runs(buf, sem.at[slot])

        @pl.when(i >= 1)
        def _():
            _wait_runs(loc.at[1 - slot], sem.at[1 - slot])

        each_pad(lambda e: pad_copy(e).wait())


def _smem_tile(tile):
    return pl.BlockSpec((tile,), lambda i, *_: (i,), memory_space=pltpu.SMEM)


def _dispatch(xn_slab, s1, s2, tables, tile, n_rows):
    token = xn_slab.shape[1:]
    return pl.pallas_call(
        functools.partial(_dispatch_kernel, unroll=8),
        grid_spec=pltpu.PrefetchScalarGridSpec(
            num_scalar_prefetch=5,
            grid=(xn_slab.shape[0] // tile,),
            in_specs=[_smem_tile(tile), _smem_tile(tile),
                      pl.BlockSpec((tile,) + token, lambda i, *_: (i, 0, 0))],
            out_specs=pl.BlockSpec(memory_space=pl.ANY),
            scratch_shapes=[pltpu.VMEM((2, 2 * tile) + token, xn_slab.dtype),
                            pltpu.VMEM((MOE_BLOCK,) + token, xn_slab.dtype),
                            pltpu.SemaphoreType.DMA((3,))]),
        out_shape=jax.ShapeDtypeStruct((n_rows,) + token, xn_slab.dtype),
        compiler_params=pltpu.CompilerParams(
            dimension_semantics=("arbitrary",), vmem_limit_bytes=VMEM_LIMIT),
        name="moe_dispatch",
    )(*tables, s1, s2, xn_slab)


def _experts_kernel(blk_ref, nused_ref, xs_ref, wg_ref, wu_ref, wd_ref, ys_ref,
                    wg_s, wu_s, wd_s, slab_s):
    b = pl.program_id(0)
    rows, nlt, _ = xs_ref.shape

    @pl.when(jnp.logical_or(b == 0, blk_ref[b] != blk_ref[jnp.maximum(b - 1, 0)]))
    def _():
        wg_s[...] = wg_ref[...].astype(BF16)
        wu_s[...] = wu_ref[...].astype(BF16)
        wd_s[...] = wd_ref[...].astype(BF16)

    @pl.when(b < nused_ref[0])
    def _():
        _tokens_to_slab(slab_s, xs_ref[...])
        x = _from_slab(slab_s, rows, nlt).astype(BF16)
        hg = _dot(x, wg_s[...])
        hu = _dot(x, wu_s[...])
        h = (hg * _sigmoid(hg) * hu).astype(BF16)
        _to_slab(slab_s, _dot(h, wd_s[...]))
        ys_ref[...] = _tokens_from_slab(slab_s, nlt)

    @pl.when(b >= nused_ref[0])
    def _():
        ys_ref[...] = jnp.zeros_like(ys_ref)


def _experts(xs, blk_expert, n_used, wts, layer):
    n_rows, nlt, _ = xs.shape
    d = nlt * LANES
    f = wts['w_gate'].shape[-1]
    rows = pl.BlockSpec((MOE_BLOCK, nlt, LANES), lambda b, blk, nu: (b, 0, 0))
    w_spec = lambda r, c: pl.BlockSpec((None, None, r, c), lambda b, blk, nu: (layer, blk[b], 0, 0))
    return pl.pallas_call(
        _experts_kernel,
        grid_spec=pltpu.PrefetchScalarGridSpec(
            num_scalar_prefetch=2,
            grid=(n_rows // MOE_BLOCK,),
            in_specs=[rows, w_spec(d, f), w_spec(d, f), w_spec(f, d)],
            out_specs=rows,
            scratch_shapes=[pltpu.VMEM((d, f), BF16), pltpu.VMEM((d, f), BF16),
                            pltpu.VMEM((f, d), BF16),
                            pltpu.VMEM((MOE_BLOCK * nlt, LANES), F32)]),
        out_shape=jax.ShapeDtypeStruct(xs.shape, xs.dtype),
        compiler_params=pltpu.CompilerParams(
            dimension_semantics=("arbitrary",), vmem_limit_bytes=VMEM_LIMIT),
        name="moe_experts",
    )(blk_expert, n_used, xs, wts['w_gate'], wts['w_up'], wts['w_down'])


def _combine_kernel(lo_ref, cnt_ref, dst_ref, s1_ref, s2_ref, w1_ref, w2_ref, yp_ref, ys_ref,
                    nf_ref, rows_ref, op_ref, os_ref, loc, acc, sem,
                    *, unroll, final_norm, n_prompt_tiles):
    i = pl.program_id(0)
    tile = yp_ref.shape[0]
    nlt = loc.shape[2]
    slot = i % 2
    buf = loc.at[slot]
    fetch = lambda t, s: _start_runs(t, lo_ref, cnt_ref, dst_ref, loc.at[s], rows_ref, sem.at[s],
                                     False)

    @pl.when(i == 0)
    def _():
        fetch(0, 0)

    @pl.when(i + 1 < pl.num_programs(0))
    def _():
        fetch(i + 1, 1 - slot)

    _wait_runs(buf, sem.at[slot])

    def body(tt, c):
        for u in range(unroll):
            t = tt * unroll + u
            r1 = buf[s1_ref[t]].astype(F32)
            r2 = buf[s2_ref[t]].astype(F32)
            acc[pl.ds(pl.multiple_of(t * nlt, nlt), nlt), :] = w1_ref[t] * r1 + w2_ref[t] * r2
        return c

    lax.fori_loop(0, tile // unroll, body, 0)
    def finish(y_ref, o_ref):
        out = y_ref[...] + _from_slab(acc, tile, nlt)
        if final_norm:
            out = _rms_norm(out, nf_ref[...])
        o_ref[...] = out

    is_prompt = i < n_prompt_tiles
    pl.when(is_prompt)(lambda: finish(yp_ref, op_ref))
    pl.when(jnp.logical_not(is_prompt))(lambda: finish(ys_ref, os_ref))


def _combine(yp, ys, rows, meta, tables, norm_f, final_norm, tile):
    d = yp.shape[1]
    token = rows.shape[1:]
    n_prompt_tiles = yp.shape[0] // tile
    n_tiles = n_prompt_tiles + ys.shape[0] // tile
    tok_p, tok_s = _token_specs(tile, d, n_prompt_tiles)
    return pl.pallas_call(
        functools.partial(_combine_kernel, unroll=8, final_norm=final_norm,
                          n_prompt_tiles=n_prompt_tiles),
        grid_spec=pltpu.PrefetchScalarGridSpec(
            num_scalar_prefetch=3,
            grid=(n_tiles,),
            in_specs=[_smem_tile(tile)] * 4 + [tok_p, tok_s, _const_spec((1, d)),
                                               pl.BlockSpec(memory_space=pl.ANY)],
            out_specs=(tok_p, tok_s),
            scratch_shapes=[pltpu.VMEM((2, 2 * tile) + token, rows.dtype),
                            pltpu.VMEM((tile * token[0], LANES), F32),
                            pltpu.SemaphoreType.DMA((2,))]),
        out_shape=(jax.ShapeDtypeStruct(yp.shape, F32), jax.ShapeDtypeStruct(ys.shape, F32)),
        compiler_params=pltpu.CompilerParams(
            dimension_semantics=("arbitrary",), vmem_limit_bytes=VMEM_LIMIT),
        name="moe_combine",
    )(*tables, *meta, yp, ys, norm_f, rows)


def _moe(yp, ys, wts, layer, norm_f, final_norm, tile):
    d = yp.shape[1]
    n = yp.shape[0] + ys.shape[0]
    assert yp.shape[0] % tile == 0 and ys.shape[0] % tile == 0 and d % LANES == 0
    xn_slab, meta_f, meta_i, cnt = _router(yp, ys, wts, layer, tile)
    cnt = cnt[:, SUBLANES:SUBLANES + N_EXPERTS, 0].astype(jnp.int32)
    lo = jnp.cumsum(cnt, axis=1) - cnt
    seg = jnp.sum(cnt, axis=0)
    seg_pad = (seg + MOE_BLOCK - 1) // MOE_BLOCK * MOE_BLOCK
    seg_end = jnp.cumsum(seg_pad)
    seg_start = seg_end - seg_pad
    dst = seg_start[None, :] + jnp.cumsum(cnt, axis=0) - cnt
    n_blocks = (2 * n + N_EXPERTS * (MOE_BLOCK - 1)) // MOE_BLOCK
    blk_ids = jnp.arange(n_blocks, dtype=jnp.int32)
    blk_expert = jnp.minimum(
        jnp.sum((seg_end[None, :] <= blk_ids[:, None] * MOE_BLOCK).astype(jnp.int32), axis=1),
        N_EXPERTS - 1)
    n_used = seg_end[-1:] // MOE_BLOCK
    pad_start = jnp.concatenate([seg_start + seg, blk_ids * MOE_BLOCK])
    pad_rows = jnp.concatenate([seg_pad - seg, jnp.where(blk_ids >= n_used[0], MOE_BLOCK, 0)])
    flat = lambda a: a.reshape(-1)
    runs = (flat(lo), flat(cnt), flat(dst))
    s1, s2 = flat(meta_i[:, 4]), flat(meta_i[:, 5])
    w1, w2 = flat(meta_f[:, 2]), flat(meta_f[:, 3])
    xs = _dispatch(xn_slab, s1, s2, runs + (pad_start, pad_rows), tile, n_blocks * MOE_BLOCK)
    rows = _experts(xs, blk_expert, n_used, wts, layer)
    return _combine(yp, ys, rows, (s1, s2, w1, w2), runs, norm_f, final_norm, tile)


def _prep_weights(norm1, w_in, ln_g, ln_b, w_s, b_s, conv_w, conv_b, lru_lambda, w_rg_a, b_rg_a,
                  w_rg_x, b_rg_x, w_out, norm2, w_route_group, b_route_group, w_route_expert,
                  b_route_expert, w_gate, w_up, w_down, n_sample_t):
    depth, d = w_in.shape[0], w_in.shape[1]
    dh = d // H_A
    row = lambda a: a.reshape(depth, 1, -1).astype(F32)
    causal = jnp.tril(jnp.ones((CHUNK, CHUNK), dtype=bool))
    ws = jnp.where(causal, 0.5 * w_s, 0.0)
    bs = jnp.repeat(jnp.swapaxes(0.5 * b_s, 1, 2), dh, axis=2)
    in_scale = jnp.concatenate([jnp.ones((3 * d,), F32), jnp.full((2 * d,), 0.5, F32)])

    def pair_bd(w):
        w = w.reshape(depth, H_B // 2, 2, w.shape[-2], w.shape[-1])
        z = jnp.zeros_like(w[:, :, 0])
        return jnp.concatenate([jnp.concatenate([w[:, :, 0], z], axis=-1),
                                jnp.concatenate([z, w[:, :, 1]], axis=-1)], axis=-2)

    gap = SUBLANES - N_GROUPS
    tail = ROUTE_ROWS - SUBLANES - N_EXPERTS
    w_route = jnp.concatenate(
        [jnp.swapaxes(w_route_group, 1, 2), jnp.zeros((depth, gap, d), F32),
         jnp.swapaxes(w_route_expert, 1, 2), jnp.zeros((depth, tail, d), F32)], axis=1)
    b_route = jnp.concatenate(
        [b_route_group, jnp.zeros((depth, gap), F32), b_route_expert,
         jnp.zeros((depth, tail), F32)], axis=1)[..., None]
    nt = n_sample_t
    return dict(
        norm1=row(norm1), w_in=(w_in * in_scale).astype(BF16), ln_g=row(ln_g), ln_b=row(ln_b),
        ws=ws.astype(BF16), bs=bs.astype(F32), conv_w=conv_w.astype(F32), conv_b=row(conv_b),
        lam=row(lru_lambda),
        wbd=(0.5 * jnp.concatenate([pair_bd(w_rg_a), pair_bd(w_rg_x)], axis=-1)).astype(BF16),
        b_a=row(0.5 * b_rg_a), b_x=row(0.5 * b_rg_x), w_out=(0.5 * w_out).astype(BF16),
        norm2=row(norm2), w_route=w_route.astype(BF16), b_route=b_route.astype(F32),
        w_gate=w_gate, w_up=w_up, w_down=w_down,
        wsc=jnp.repeat(jnp.transpose(ws[:, :, :nt, :nt], (0, 2, 3, 1)), dh, axis=3).astype(F32),
        bsc=bs[:, :nt].astype(F32),
    )


def kernel(x_prompt, x_sample, state_lru_h, state_conv, norm1, w_in, ln_g, ln_b, w_s, b_s, conv_w, conv_b, lru_lambda, w_rg_a, b_rg_a, w_rg_x, b_rg_x, w_out, norm2, w_route_group, b_route_group, w_route_expert, b_route_expert, w_gate, w_up, w_down, norm_f):
    depth = w_in.shape[0]
    nb, seq, d = x_prompt.shape
    ns, nt, _ = x_sample.shape
    assert nt <= CHUNK
    nf = norm_f.reshape(1, d).astype(F32)
    wts = _prep_weights(norm1, w_in, ln_g, ln_b, w_s, b_s, conv_w, conv_b, lru_lambda, w_rg_a,
                        b_rg_a, w_rg_x, b_rg_x, w_out, norm2, w_route_group, b_route_group,
                        w_route_expert, b_route_expert, w_gate, w_up, w_down, nt)
    conv_state = jnp.transpose(state_conv, (0, 2, 1, 3))
    tile = min(MOE_TILE, nt * ns)
    yp = x_prompt
    ys = jnp.transpose(x_sample, (1, 0, 2))
    hp, cp, vp, hs, cs, vs = [], [], [], [], [], []
    for l in range(depth):
        yp, h, c, v = _mixer_prompt(yp, wts, l)
        hp.append(h)
        cp.append(c)
        vp.append(v)
        ys, h, c, v = _mixer_sample(ys, state_lru_h, conv_state, wts, l)
        hs.append(h)
        cs.append(c)
        vs.append(v)
        yp, ys = _moe(yp.reshape(nb * seq, d), ys.reshape(nt * ns, d), wts, l, nf,
                      l == depth - 1, tile)
        yp = yp.reshape(nb, seq, d)
        ys = ys.reshape(nt, ns, d)
    to_seq_major = lambda a: jnp.transpose(jnp.stack(a), (0, 2, 1, 3))
    return (yp, jnp.transpose(ys, (1, 0, 2)), jnp.stack(hp), jnp.stack(cp), jnp.stack(vp),
            jnp.stack(hs), to_seq_major(cs), to_seq_major(vs))
```

```python
import functools

import jax
import jax.numpy as jnp
from jax import lax
from jax.experimental import pallas as pl
from jax.experimental.pallas import tpu as pltpu

F32 = jnp.float32
BF16 = jnp.bfloat16

CHUNK = 128
H_A = 4
H_B = 8
CONV_W = 4
LRU_C = 8.0
N_GROUPS = 4
E_PER_GROUP = 8
N_EXPERTS = N_GROUPS * E_PER_GROUP
EPS = 1e-6

LANES = 128
SUBLANES = 8
SCAN_UNROLL = 4
SEQS_PER_TRIP = 8
ROUTE_ROWS = 48
MOE_TILE = 512
MOE_BLOCK = 512
EXPERT_PARTS = 1
VMEM_LIMIT = 60 * 1024 * 1024


def _dot(a, b):
    return jnp.dot(a, b, preferred_element_type=F32)


def _sigmoid(x):
    return 0.5 * (jnp.tanh(0.5 * x) + 1.0)


def _gelu2(x):
    c = 0.7978845608028654
    return x * (1.0 + jnp.tanh(x * (c + (c * 0.044715) * (x * x))))


def _rms_norm(x, g):
    return x * lax.rsqrt(jnp.mean(x * x, axis=-1, keepdims=True) + EPS) * g


def _layer_norm(x, g, b, eps):
    xc = x - jnp.mean(x, axis=-1, keepdims=True)
    var = jnp.mean(xc * xc, axis=-1, keepdims=True)
    return xc * lax.rsqrt(var + eps) * g + b


def _softplus(x):
    return jnp.maximum(x, 0.0) + jnp.log(1.0 + jnp.exp(-jnp.abs(x)))


def _lru_gates(xc, wbd_ref, b_a, b_x, lam_half):
    d = xc.shape[-1]
    xcb = xc.astype(BF16)
    r_parts, i_parts = [], []
    for q in range(d // 256):
        ri = _dot(xcb[:, q * 256:(q + 1) * 256], wbd_ref[q])
        r_parts.append(ri[:, :256])
        i_parts.append(ri[:, 256:])
    r2 = 1.0 + jnp.tanh(jnp.concatenate(r_parts, axis=-1) + b_a)
    i2 = 1.0 + jnp.tanh(jnp.concatenate(i_parts, axis=-1) + b_x)
    a = jnp.exp(lam_half * r2)
    mult_half = jnp.sqrt(0.25 - 0.25 * (a * a))
    return a, mult_half, i2 * xc


def _shift_rows(x, hist, j):
    sh = pltpu.roll(x, j, axis=0)
    top = jnp.where(lax.broadcasted_iota(jnp.int32, (SUBLANES, 1), 0) < j,
                    pltpu.roll(hist, j, axis=0), sh[:SUBLANES])
    return jnp.concatenate([top, sh[SUBLANES:]], axis=0)


def _mixer_prompt_kernel(x_ref, norm_ref, win_ref, lng_ref, lnb_ref, ws_ref, bs_ref,
                         cw_ref, cb_ref, lam_ref, wbd_ref, ba_ref, bx_ref, wout_ref,
                         y_ref, h_ref, conv_ref, v_ref,
                         hist_s, a_s, b_s, ya_s, gb_s, hc_s):
    i = pl.program_id(0)
    last = pl.num_programs(0) - 1
    nseq, _, d = x_ref.shape
    nlt = d // LANES
    spt = SEQS_PER_TRIP
    pair = spt * CHUNK

    @pl.when(i == 0)
    def _():
        hist_s[...] = jnp.zeros_like(hist_s)
        hc_s[...] = jnp.zeros_like(hc_s)

    lam_half = (-0.5 * LRU_C) * _softplus(-lam_ref[...])
    is_first = i == 0

    def branch_body(p, c):
        row = pl.multiple_of(p * pair, pair)
        x = jnp.concatenate([x_ref[spt * p + k] for k in range(spt)], axis=0)
        xnb = _rms_norm(x, norm_ref[...]).astype(BF16)
        v = _layer_norm(_gelu2(_dot(xnb, win_ref[:, d:2 * d])), lng_ref[...], lnb_ref[...],
                        4.0 * EPS)

        @pl.when(i == last)
        def _():
            for k in range(spt):
                v_ref[spt * p + k] = v[k * CHUNK:(k + 1) * CHUNK]

        vb = v.astype(BF16)
        dh = d // H_A
        z_rows = []
        for k in range(spt):
            zs = [_dot(ws_ref[h], vb[k * CHUNK:(k + 1) * CHUNK, h * dh:(h + 1) * dh])
                  for h in range(H_A)]
            z_rows.append(jnp.concatenate(zs, axis=-1) + bs_ref[...])
        z = jnp.concatenate(z_rows, axis=0)
        ya = _gelu2(_dot(xnb, win_ref[:, 0:d])) * z
        ga2 = 1.0 + jnp.tanh(_dot(xnb, win_ref[:, 3 * d:4 * d]))
        ya_s[pl.ds(row, pair), :] = (ga2 * ya).astype(BF16)
        gb_s[pl.ds(row, pair), :] = (1.0 + jnp.tanh(_dot(xnb, win_ref[:, 4 * d:5 * d]))).astype(BF16)
        xr = _dot(xnb, win_ref[:, 2 * d:3 * d])
        xcs = []
        for k in range(spt):
            s = spt * p + k
            xk = xr[k * CHUNK:(k + 1) * CHUNK]
            hist = hist_s[s]
            xc = cb_ref[...] + xk * cw_ref[CONV_W - 1:CONV_W, :]
            for j in range(1, CONV_W):
                xc = xc + _shift_rows(xk, hist, j) * cw_ref[CONV_W - 1 - j:CONV_W - j, :]
            hist_s[s] = xk[CHUNK - SUBLANES:]
            xcs.append(xc)
        xc = jnp.concatenate(xcs, axis=0)
        a, mult_half, ix2 = _lru_gates(xc, wbd_ref, ba_ref[...], bx_ref[...], lam_half)
        bt = mult_half * ix2
        for k in range(spt):
            s = spt * p + k
            for j in range(nlt):
                a_s[j, pl.ds(s, CHUNK, stride=nseq), :] = a[k * CHUNK:(k + 1) * CHUNK, j * LANES:(j + 1) * LANES]
                b_s[j, pl.ds(s, CHUNK, stride=nseq), :] = bt[k * CHUNK:(k + 1) * CHUNK, j * LANES:(j + 1) * LANES]

        @pl.when(is_first)
        def _():
            for k in range(spt):
                first = 0.5 * ix2[k * CHUNK:k * CHUNK + 1]
                for j in range(nlt):
                    b_s[j, pl.ds(spt * p + k, 1), :] = first[:, j * LANES:(j + 1) * LANES]

        return c

    lax.fori_loop(0, nseq // spt, branch_body, 0)

    def scan_body(tt, hs):
        for u in range(SCAN_UNROLL):
            row = pl.multiple_of((tt * SCAN_UNROLL + u) * nseq, nseq)
            out = []
            for j in range(nlt):
                h = a_s[j, pl.ds(row, nseq), :] * hs[j] + b_s[j, pl.ds(row, nseq), :]
                b_s[j, pl.ds(row, nseq), :] = h
                out.append(h)
            hs = tuple(out)
        return hs

    h0 = tuple(hc_s[:, j * LANES:(j + 1) * LANES] for j in range(nlt))
    hs = lax.fori_loop(0, CHUNK // SCAN_UNROLL, scan_body, h0)
    hfin = jnp.concatenate(hs, axis=-1)
    hc_s[...] = hfin
    h_ref[...] = hfin

    def out_body(p, c):
        row = pl.multiple_of(p * pair, pair)
        hrows = []
        for k in range(spt):
            hrows.append(jnp.concatenate(
                [b_s[j, pl.ds(spt * p + k, CHUNK, stride=nseq), :] for j in range(nlt)], axis=-1))
        h = jnp.concatenate(hrows, axis=0)
        merged = (ya_s[pl.ds(row, pair), :].astype(F32)
                  + gb_s[pl.ds(row, pair), :].astype(F32) * h).astype(BF16)
        o = _dot(merged, wout_ref[...])
        for k in range(spt):
            y_ref[spt * p + k] = x_ref[spt * p + k] + o[k * CHUNK:(k + 1) * CHUNK]
        return c

    lax.fori_loop(0, nseq // spt, out_body, 0)

    @pl.when(i == last)
    def _():
        conv_ref[...] = hist_s[:, SUBLANES - (CONV_W - 1):, :]


def _const_spec(shape):
    nd = len(shape)
    return pl.BlockSpec(shape, lambda i, *_, _n=nd: (0,) * _n, pipeline_mode=pl.Buffered(1))


def _layer_spec(stacked, layer):
    shape = stacked.shape[1:]
    nd = len(shape)
    return pl.BlockSpec((None,) + tuple(shape), lambda i, *_, _n=nd: (layer,) + (0,) * _n,
                        pipeline_mode=pl.Buffered(1))


MIXER_WEIGHTS = ('norm1', 'w_in', 'ln_g', 'ln_b', 'ws', 'bs', 'conv_w', 'conv_b', 'lam', 'wbd',
                 'b_a', 'b_x', 'w_out')


def _mixer_prompt(x, wts, layer):
    nseq, seq, d = x.shape
    assert seq % CHUNK == 0 and nseq == SUBLANES and d % 256 == 0
    n_chunks = seq // CHUNK
    nlt = d // LANES
    weights = tuple(wts[k] for k in MIXER_WEIGHTS)
    x_spec = pl.BlockSpec((nseq, CHUNK, d), lambda i: (0, i, 0))
    out_shape = (
        jax.ShapeDtypeStruct((nseq, seq, d), F32),
        jax.ShapeDtypeStruct((nseq, d), F32),
        jax.ShapeDtypeStruct((nseq, CONV_W - 1, d), F32),
        jax.ShapeDtypeStruct((nseq, CHUNK, d), F32),
    )
    out_specs = (
        x_spec,
        _const_spec((nseq, d)),
        _const_spec((nseq, CONV_W - 1, d)),
        _const_spec((nseq, CHUNK, d)),
    )
    scratch = [
        pltpu.VMEM((nseq, SUBLANES, d), F32),
        pltpu.VMEM((nlt, CHUNK * nseq, LANES), F32),
        pltpu.VMEM((nlt, CHUNK * nseq, LANES), F32),
        pltpu.VMEM((nseq * CHUNK, d), BF16),
        pltpu.VMEM((nseq * CHUNK, d), BF16),
        pltpu.VMEM((nseq, d), F32),
    ]
    return pl.pallas_call(
        _mixer_prompt_kernel,
        grid=(n_chunks,),
        in_specs=[x_spec] + [_layer_spec(w, layer) for w in weights],
        out_specs=out_specs,
        out_shape=out_shape,
        scratch_shapes=scratch,
        compiler_params=pltpu.CompilerParams(
            dimension_semantics=("arbitrary",), vmem_limit_bytes=VMEM_LIMIT),
        name="mixer_prompt",
    )(x, *weights)


def _mixer_sample_kernel(x_ref, h0_ref, cst_ref, norm_ref, win_ref, lng_ref, lnb_ref, wsc_ref,
                         bsc_ref, cw_ref, cb_ref, lam_ref, wbd_ref, ba_ref, bx_ref, wout_ref,
                         y_ref, h_ref, conv_ref, v_ref):
    nt, ns, d = x_ref.shape
    at = lambda a, t: a[t * ns:(t + 1) * ns]
    lam_half = (-0.5 * LRU_C) * _softplus(-lam_ref[...])
    x = x_ref[...].reshape(nt * ns, d)
    xnb = _rms_norm(x, norm_ref[...]).astype(BF16)
    v = _layer_norm(_gelu2(_dot(xnb, win_ref[:, d:2 * d])), lng_ref[...], lnb_ref[...], 4.0 * EPS)
    v_ref[...] = v.reshape(nt, ns, d)
    zs = []
    for t in range(nt):
        z = bsc_ref[t:t + 1, :]
        for s in range(t + 1):
            z = z + wsc_ref[t, s:s + 1, :] * at(v, s)
        zs.append(z)
    ya = _gelu2(_dot(xnb, win_ref[:, 0:d])) * jnp.concatenate(zs, axis=0)
    ga2 = 1.0 + jnp.tanh(_dot(xnb, win_ref[:, 3 * d:4 * d]))
    gb2 = 1.0 + jnp.tanh(_dot(xnb, win_ref[:, 4 * d:5 * d]))
    xr = _dot(xnb, win_ref[:, 2 * d:3 * d])
    xp = [cst_ref[k] for k in range(CONV_W - 1)] + [at(xr, t) for t in range(nt)]
    xcs = []
    for t in range(nt):
        xc = cb_ref[...]
        for k in range(CONV_W):
            xc = xc + xp[t + k] * cw_ref[k:k + 1, :]
        xcs.append(xc)
    a, mult_half, ix2 = _lru_gates(jnp.concatenate(xcs, axis=0), wbd_ref, ba_ref[...],
                                   bx_ref[...], lam_half)
    bt = mult_half * ix2
    h = h0_ref[...]
    hs = []
    for t in range(nt):
        h = at(a, t) * h + at(bt, t)
        hs.append(h)
    merged = (ga2 * ya + gb2 * jnp.concatenate(hs, axis=0)).astype(BF16)
    y_ref[...] = (x + _dot(merged, wout_ref[...])).reshape(nt, ns, d)
    h_ref[...] = h
    for k in range(CONV_W - 1):
        conv_ref[k] = xp[nt + k]


def _mixer_sample(x, h0, conv_state, wts, layer):
    nt, ns, d = x.shape
    names = MIXER_WEIGHTS[:4] + ('wsc', 'bsc') + MIXER_WEIGHTS[6:]
    stacked = (h0, conv_state) + tuple(wts[k] for k in names)
    out_shape = (
        jax.ShapeDtypeStruct((nt, ns, d), F32),
        jax.ShapeDtypeStruct((ns, d), F32),
        jax.ShapeDtypeStruct((CONV_W - 1, ns, d), F32),
        jax.ShapeDtypeStruct((nt, ns, d), F32),
    )
    return pl.pallas_call(
        _mixer_sample_kernel,
        grid=(1,),
        in_specs=[_const_spec(x.shape)] + [_layer_spec(w, layer) for w in stacked],
        out_specs=tuple(_const_spec(s.shape) for s in out_shape),
        out_shape=out_shape,
        compiler_params=pltpu.CompilerParams(
            dimension_semantics=("arbitrary",), vmem_limit_bytes=VMEM_LIMIT),
        name="mixer_sample",
    )(x, *stacked)


def _to_slab(ref, val):
    t, d = val.shape
    nlt = d // LANES
    for j in range(nlt):
        ref[pl.ds(j, t, stride=nlt), :] = val[:, j * LANES:(j + 1) * LANES]


def _from_slab(ref, t, nlt):
    return jnp.concatenate([ref[pl.ds(j, t, stride=nlt), :] for j in range(nlt)], axis=-1)


def _tokens_from_slab(slab_ref, nlt):
    return slab_ref[...].reshape(slab_ref.shape[0] // nlt, nlt, LANES).astype(BF16)


def _tokens_to_slab(slab_ref, tokens):
    slab_ref[...] = tokens.astype(F32).reshape(slab_ref.shape)


def _router_kernel(yp_ref, ys_ref, norm_ref, wr_ref, br_ref, tri_ref, low_ref, xn_ref, mf_ref,
                   mi_ref, cnt_ref, slab_s, *, n_prompt_tiles):
    y = jnp.where(pl.program_id(0) < n_prompt_tiles, yp_ref[...], ys_ref[...])
    xn = _rms_norm(y, norm_ref[...])
    _to_slab(slab_s, xn)
    xn_ref[...] = _tokens_from_slab(slab_s, xn_ref.shape[1])
    lt = lax.dot_general(wr_ref[...], xn.astype(BF16), (((1,), (1,)), ((), ())),
                         preferred_element_type=F32) + br_ref[...]
    tile = lt.shape[1]
    row = lax.broadcasted_iota(jnp.int32, (SUBLANES, tile), 0).astype(F32)
    neg = jnp.float32(-jnp.inf)
    big = jnp.float32(SUBLANES)

    def argmax_first(vals):
        m = jnp.max(vals, axis=0, keepdims=True)
        return m, jnp.min(jnp.where(vals == m, row, big), axis=0, keepdims=True)

    def group_block(x, gi):
        out = x[SUBLANES:2 * SUBLANES]
        for g in range(1, N_GROUPS):
            out = jnp.where(gi == g, x[(g + 1) * SUBLANES:(g + 2) * SUBLANES], out)
        return out

    gl = jnp.where(row < N_GROUPS, lt[0:SUBLANES], neg)
    gm, gi = argmax_first(gl)
    pg_top = 1.0 / jnp.sum(jnp.exp(gl - gm), axis=0, keepdims=True)
    el = group_block(lt, gi)
    m1, i1 = argmax_first(el)
    m2, i2 = argmax_first(jnp.where(row == i1, neg, el))
    e2 = jnp.exp(m2 - m1)
    w1 = pg_top / (1.0 + e2)
    w2 = pg_top * e2 / (1.0 + e2)
    hot1 = row == i1
    hot2 = row == i2
    picked = jnp.where(jnp.logical_or(hot1, hot2), 1.0, 0.0)
    zero = jnp.zeros_like(picked)
    onehot = jnp.concatenate(
        [zero] + [jnp.where(gi == g, picked, 0.0) for g in range(N_GROUPS)] + [zero],
        axis=0).astype(BF16)
    before = _dot(onehot, tri_ref[...])
    lower = jnp.sum(_dot(low_ref[...], onehot), axis=1, keepdims=True)
    slot = group_block(before + lower, gi)
    p1 = jnp.sum(jnp.where(hot1, slot, 0.0), axis=0, keepdims=True)
    p2 = jnp.sum(jnp.where(hot2, slot, 0.0), axis=0, keepdims=True)
    meta = jnp.zeros((SUBLANES, tile), F32)
    for k, c in enumerate((gi * E_PER_GROUP + i1, gi * E_PER_GROUP + i2, w1, w2, p1, p2)):
        meta = jnp.where(row == k, c, meta)
    mf_ref[0] = meta
    mi_ref[0] = meta.astype(jnp.int32)
    cnt = jnp.sum(onehot.astype(F32), axis=1, keepdims=True)
    cnt_ref[0] = jnp.broadcast_to(cnt, cnt_ref.shape[1:])


def _token_specs(tile, d, n_prompt_tiles):
    last = n_prompt_tiles - 1
    return (pl.BlockSpec((tile, d), lambda i, *_: (jnp.minimum(i, last), 0)),
            pl.BlockSpec((tile, d), lambda i, *_: (jnp.maximum(i - n_prompt_tiles, 0), 0)))


def _router(yp, ys, wts, layer, tile):
    d = yp.shape[1]
    nlt = d // LANES
    n_prompt_tiles = yp.shape[0] // tile
    n_tiles = n_prompt_tiles + ys.shape[0] // tile
    tri = jnp.triu(jnp.ones((tile, tile), BF16), 1)
    low = jnp.tril(jnp.ones((ROUTE_ROWS, ROUTE_ROWS), BF16), -1)
    meta_spec = pl.BlockSpec((1, SUBLANES, tile), lambda i: (i, 0, 0))
    return pl.pallas_call(
        functools.partial(_router_kernel, n_prompt_tiles=n_prompt_tiles),
        grid=(n_tiles,),
        in_specs=list(_token_specs(tile, d, n_prompt_tiles)) + [
            _layer_spec(wts['norm2'], layer), _layer_spec(wts['w_route'], layer),
            _layer_spec(wts['b_route'], layer), _const_spec((tile, tile)),
            _const_spec((ROUTE_ROWS, ROUTE_ROWS))],
        out_specs=(pl.BlockSpec((tile, nlt, LANES), lambda i: (i, 0, 0)), meta_spec, meta_spec,
                   pl.BlockSpec((1, ROUTE_ROWS, LANES), lambda i: (i, 0, 0))),
        out_shape=(jax.ShapeDtypeStruct((n_tiles * tile, nlt, LANES), BF16),
                   jax.ShapeDtypeStruct((n_tiles, SUBLANES, tile), F32),
                   jax.ShapeDtypeStruct((n_tiles, SUBLANES, tile), jnp.int32),
                   jax.ShapeDtypeStruct((n_tiles, ROUTE_ROWS, LANES), F32)),
        scratch_shapes=[pltpu.VMEM((tile * nlt, LANES), F32)],
        compiler_params=pltpu.CompilerParams(
            dimension_semantics=("arbitrary",), vmem_limit_bytes=VMEM_LIMIT),
        name="moe_router",
    )(yp, ys, wts['norm2'], wts['w_route'], wts['b_route'], tri, low)


def _run_copy(tile_idx, e, lo_ref, cnt_ref, dst_ref, local, remote, sem, to_remote):
    k = tile_idx * N_EXPERTS + e
    loc = local.at[pl.ds(lo_ref[k], cnt_ref[k])]
    rem = remote.at[pl.ds(dst_ref[k], cnt_ref[k])]
    return pltpu.make_async_copy(loc, rem, sem) if to_remote else pltpu.make_async_copy(rem, loc, sem)


def _start_runs(tile_idx, lo_ref, cnt_ref, dst_ref, local, remote, sem, to_remote):
    for e in range(N_EXPERTS):
        @pl.when(cnt_ref[tile_idx * N_EXPERTS + e] > 0)
        def _():
            _run_copy(tile_idx, e, lo_ref, cnt_ref, dst_ref, local, remote, sem, to_remote).start()


def _wait_runs(local, sem):
    pltpu.make_async_copy(local, local, sem).wait()


def _dispatch_kernel(lo_ref, cnt_ref, dst_ref, padst_ref, padn_ref,
                     s1_ref, s2_ref, x_ref, xs_ref, loc, zeros, sem, *, unroll):
    i = pl.program_id(0)
    last = pl.num_programs(0) - 1
    tile = x_ref.shape[0]
    slot = i % 2
    buf = loc.at[slot]

    @pl.when(i >= 2)
    def _():
        _wait_runs(buf, sem.at[slot])

    def body(tt, c):
        for u in range(unroll):
            t = tt * unroll + u
            row = x_ref[t]
            buf[s1_ref[t]] = row
            buf[s2_ref[t]] = row
        return c

    lax.fori_loop(0, tile // unroll, body, 0)
    _start_runs(i, lo_ref, cnt_ref, dst_ref, buf, xs_ref, sem.at[slot], True)

    @pl.when(i == last)
    def _():
        zeros[...] = jnp.zeros_like(zeros)

        def pad_copy(e):
            n = padn_ref[e]
            dst = xs_ref.at[pl.ds(padst_ref[e], n)]
            return pltpu.make_async_copy(zeros.at[pl.ds(0, n)], dst, sem.at[2])

        def each_pad(fn):
            def b(e, c):
                @pl.when(padn_ref[e] > 0)
                def _():
                    fn(e)
                return c
            lax.fori_loop(0, padn_ref.shape[0], b, 0)

        each_pad(lambda e: pad_copy(e).start())
        _wait_runs(buf, sem.at[slot])

        @pl.when(i >= 1)
        def _():
            _wait_runs(loc.at[1 - slot], sem.at[1 - slot])

        each_pad(lambda e: pad_copy(e).wait())


def _smem_tile(tile):
    return pl.BlockSpec((tile,), lambda i, *_: (i,), memory_space=pltpu.SMEM)


def _dispatch(xn_slab, s1, s2, tables, tile, n_rows):
    token = xn_slab.shape[1:]
    return pl.pallas_call(
        functools.partial(_dispatch_kernel, unroll=8),
        grid_spec=pltpu.PrefetchScalarGridSpec(
            num_scalar_prefetch=5,
            grid=(xn_slab.shape[0] // tile,),
            in_specs=[_smem_tile(tile), _smem_tile(tile),
                      pl.BlockSpec((tile,) + token, lambda i, *_: (i, 0, 0))],
            out_specs=pl.BlockSpec(memory_space=pl.ANY),
            scratch_shapes=[pltpu.VMEM((2, 2 * tile) + token, xn_slab.dtype),
                            pltpu.VMEM((MOE_BLOCK,) + token, xn_slab.dtype),
                            pltpu.SemaphoreType.DMA((3,))]),
        out_shape=jax.ShapeDtypeStruct((n_rows,) + token, xn_slab.dtype),
        compiler_params=pltpu.CompilerParams(
            dimension_semantics=("arbitrary",), vmem_limit_bytes=VMEM_LIMIT),
        name="moe_dispatch",
    )(*tables, s1, s2, xn_slab)


def _experts_kernel(blk_ref, nused_ref, xs_ref, wg_ref, wu_ref, wd_ref, ys_ref,
                    wg_s, wu_s, wd_s, slab_in, slab_out):
    b = pl.program_id(0)
    rows, nlt, _ = xs_ref.shape
    part = rows // EXPERT_PARTS

    @pl.when(jnp.logical_or(b == 0, blk_ref[b] != blk_ref[jnp.maximum(b - 1, 0)]))
    def _():
        wg_s[...] = wg_ref[...].astype(BF16)
        wu_s[...] = wu_ref[...].astype(BF16)
        wd_s[...] = wd_ref[...].astype(BF16)

    @pl.when(b < nused_ref[0])
    def _():
        for k in range(EXPERT_PARTS):
            sin = slab_in.at[pl.ds(k * part * nlt, part * nlt)]
            sout = slab_out.at[pl.ds(k * part * nlt, part * nlt)]
            _tokens_to_slab(sin, xs_ref[k * part:(k + 1) * part])
            x = _from_slab(sin, part, nlt).astype(BF16)
            hg = _dot(x, wg_s[...])
            hu = _dot(x, wu_s[...])
            h = (hg * _sigmoid(hg) * hu).astype(BF16)
            _to_slab(sout, _dot(h, wd_s[...]))
            ys_ref[k * part:(k + 1) * part] = _tokens_from_slab(sout, nlt)

    @pl.when(b >= nused_ref[0])
    def _():
        ys_ref[...] = jnp.zeros_like(ys_ref)


def _experts(xs, blk_expert, n_used, wts, layer):
    n_rows, nlt, _ = xs.shape
    d = nlt * LANES
    f = wts['w_gate'].shape[-1]
    rows = pl.BlockSpec((MOE_BLOCK, nlt, LANES), lambda b, blk, nu: (b, 0, 0))
    w_spec = lambda r, c: pl.BlockSpec((None, None, r, c), lambda b, blk, nu: (layer, blk[b], 0, 0))
    return pl.pallas_call(
        _experts_kernel,
        grid_spec=pltpu.PrefetchScalarGridSpec(
            num_scalar_prefetch=2,
            grid=(n_rows // MOE_BLOCK,),
            in_specs=[rows, w_spec(d, f), w_spec(d, f), w_spec(f, d)],
            out_specs=rows,
            scratch_shapes=[pltpu.VMEM((d, f), BF16), pltpu.VMEM((d, f), BF16),
                            pltpu.VMEM((f, d), BF16),
                            pltpu.VMEM((MOE_BLOCK * nlt, LANES), F32),
                            pltpu.VMEM((MOE_BLOCK * nlt, LANES), F32)]),
        out_shape=jax.ShapeDtypeStruct(xs.shape, xs.dtype),
        compiler_params=pltpu.CompilerParams(
            dimension_semantics=("arbitrary",), vmem_limit_bytes=VMEM_LIMIT),
        name="moe_experts",
    )(blk_expert, n_used, xs, wts['w_gate'], wts['w_up'], wts['w_down'])


def _combine_kernel(lo_ref, cnt_ref, dst_ref, s1_ref, s2_ref, w1_ref, w2_ref, yp_ref, ys_ref,
                    nf_ref, rows_ref, op_ref, os_ref, loc, acc, sem,
                    *, unroll, final_norm, n_prompt_tiles):
    i = pl.program_id(0)
    tile = yp_ref.shape[0]
    nlt = loc.shape[2]
    slot = i % 2
    buf = loc.at[slot]
    fetch = lambda t, s: _start_runs(t, lo_ref, cnt_ref, dst_ref, loc.at[s], rows_ref, sem.at[s],
                                     False)

    @pl.when(i == 0)
    def _():
        fetch(0, 0)

    @pl.when(i + 1 < pl.num_programs(0))
    def _():
        fetch(i + 1, 1 - slot)

    _wait_runs(buf, sem.at[slot])

    def body(tt, c):
        for u in range(unroll):
            t = tt * unroll + u
            r1 = buf[s1_ref[t]].astype(F32)
            r2 = buf[s2_ref[t]].astype(F32)
            acc[pl.ds(pl.multiple_of(t * nlt, nlt), nlt), :] = w1_ref[t] * r1 + w2_ref[t] * r2
        return c

    lax.fori_loop(0, tile // unroll, body, 0)
    def finish(y_ref, o_ref):
        out = y_ref[...] + _from_slab(acc, tile, nlt)
        if final_norm:
            out = _rms_norm(out, nf_ref[...])
        o_ref[...] = out

    is_prompt = i < n_prompt_tiles
    pl.when(is_prompt)(lambda: finish(yp_ref, op_ref))
    pl.when(jnp.logical_not(is_prompt))(lambda: finish(ys_ref, os_ref))


def _combine(yp, ys, rows, meta, tables, norm_f, final_norm, tile):
    d = yp.shape[1]
    token = rows.shape[1:]
    n_prompt_tiles = yp.shape[0] // tile
    n_tiles = n_prompt_tiles + ys.shape[0] // tile
    tok_p, tok_s = _token_specs(tile, d, n_prompt_tiles)
    return pl.pallas_call(
        functools.partial(_combine_kernel, unroll=8, final_norm=final_norm,
                          n_prompt_tiles=n_prompt_tiles),
        grid_spec=pltpu.PrefetchScalarGridSpec(
            num_scalar_prefetch=3,
            grid=(n_tiles,),
            in_specs=[_smem_tile(tile)] * 4 + [tok_p, tok_s, _const_spec((1, d)),
                                               pl.BlockSpec(memory_space=pl.ANY)],
            out_specs=(tok_p, tok_s),
            scratch_shapes=[pltpu.VMEM((2, 2 * tile) + token, rows.dtype),
                            pltpu.VMEM((tile * token[0], LANES), F32),
                            pltpu.SemaphoreType.DMA((2,))]),
        out_shape=(jax.ShapeDtypeStruct(yp.shape, F32), jax.ShapeDtypeStruct(ys.shape, F32)),
        compiler_params=pltpu.CompilerParams(
            dimension_semantics=("arbitrary",), vmem_limit_bytes=VMEM_LIMIT),
        name="moe_combine",
    )(*tables, *meta, yp, ys, norm_f, rows)


def _moe(yp, ys, wts, layer, norm_f, final_norm, tile):
    d = yp.shape[1]
    n = yp.shape[0] + ys.shape[0]
    assert yp.shape[0] % tile == 0 and ys.shape[0] % tile == 0 and d % LANES == 0
    xn_slab, meta_f, meta_i, cnt = _router(yp, ys, wts, layer, tile)
    cnt = cnt[:, SUBLANES:SUBLANES + N_EXPERTS, 0].astype(jnp.int32)
    lo = jnp.cumsum(cnt, axis=1) - cnt
    seg = jnp.sum(cnt, axis=0)
    seg_pad = (seg + MOE_BLOCK - 1) // MOE_BLOCK * MOE_BLOCK
    seg_end = jnp.cumsum(seg_pad)
    seg_start = seg_end - seg_pad
    dst = seg_start[None, :] + jnp.cumsum(cnt, axis=0) - cnt
    n_blocks = (2 * n + N_EXPERTS * (MOE_BLOCK - 1)) // MOE_BLOCK
    blk_ids = jnp.arange(n_blocks, dtype=jnp.int32)
    blk_expert = jnp.minimum(
        jnp.sum((seg_end[None, :] <= blk_ids[:, None] * MOE_BLOCK).astype(jnp.int32), axis=1),
        N_EXPERTS - 1)
    n_used = seg_end[-1:] // MOE_BLOCK
    pad_start = jnp.concatenate([seg_start + seg, blk_ids * MOE_BLOCK])
    pad_rows = jnp.concatenate([seg_pad - seg, jnp.where(blk_ids >= n_used[0], MOE_BLOCK, 0)])
    flat = lambda a: a.reshape(-1)
    runs = (flat(lo), flat(cnt), flat(dst))
    s1, s2 = flat(meta_i[:, 4]), flat(meta_i[:, 5])
    w1, w2 = flat(meta_f[:, 2]), flat(meta_f[:, 3])
    xs = _dispatch(xn_slab, s1, s2, runs + (pad_start, pad_rows), tile, n_blocks * MOE_BLOCK)
    rows = _experts(xs, blk_expert, n_used, wts, layer)
    return _combine(yp, ys, rows, (s1, s2, w1, w2), runs, norm_f, final_norm, tile)


def _prep_weights(norm1, w_in, ln_g, ln_b, w_s, b_s, conv_w, conv_b, lru_lambda, w_rg_a, b_rg_a,
                  w_rg_x, b_rg_x, w_out, norm2, w_route_group, b_route_group, w_route_expert,
                  b_route_expert, w_gate, w_up, w_down, n_sample_t):
    depth, d = w_in.shape[0], w_in.shape[1]
    dh = d // H_A
    row = lambda a: a.reshape(depth, 1, -1).astype(F32)
    causal = jnp.tril(jnp.ones((CHUNK, CHUNK), dtype=bool))
    ws = jnp.where(causal, 0.5 * w_s, 0.0)
    bs = jnp.repeat(jnp.swapaxes(0.5 * b_s, 1, 2), dh, axis=2)
    in_scale = jnp.concatenate([jnp.ones((3 * d,), F32), jnp.full((2 * d,), 0.5, F32)])

    def pair_bd(w):
        w = w.reshape(depth, H_B // 2, 2, w.shape[-2], w.shape[-1])
        z = jnp.zeros_like(w[:, :, 0])
        return jnp.concatenate([jnp.concatenate([w[:, :, 0], z], axis=-1),
                                jnp.concatenate([z, w[:, :, 1]], axis=-1)], axis=-2)

    gap = SUBLANES - N_GROUPS
    tail = ROUTE_ROWS - SUBLANES - N_EXPERTS
    w_route = jnp.concatenate(
        [jnp.swapaxes(w_route_group, 1, 2), jnp.zeros((depth, gap, d), F32),
         jnp.swapaxes(w_route_expert, 1, 2), jnp.zeros((depth, tail, d), F32)], axis=1)
    b_route = jnp.concatenate(
        [b_route_group, jnp.zeros((depth, gap), F32), b_route_expert,
         jnp.zeros((depth, tail), F32)], axis=1)[..., None]
    nt = n_sample_t
    return dict(
        norm1=row(norm1), w_in=(w_in * in_scale).astype(BF16), ln_g=row(ln_g), ln_b=row(ln_b),
        ws=ws.astype(BF16), bs=bs.astype(F32), conv_w=conv_w.astype(F32), conv_b=row(conv_b),
        lam=row(lru_lambda),
        wbd=(0.5 * jnp.concatenate([pair_bd(w_rg_a), pair_bd(w_rg_x)], axis=-1)).astype(BF16),
        b_a=row(0.5 * b_rg_a), b_x=row(0.5 * b_rg_x), w_out=(0.5 * w_out).astype(BF16),
        norm2=row(norm2), w_route=w_route.astype(BF16), b_route=b_route.astype(F32),
        w_gate=w_gate, w_up=w_up, w_down=w_down,
        wsc=jnp.repeat(jnp.transpose(ws[:, :, :nt, :nt], (0, 2, 3, 1)), dh, axis=3).astype(F32),
        bsc=bs[:, :nt].astype(F32),
    )


def kernel(x_prompt, x_sample, state_lru_h, state_conv, norm1, w_in, ln_g, ln_b, w_s, b_s, conv_w, conv_b, lru_lambda, w_rg_a, b_rg_a, w_rg_x, b_rg_x, w_out, norm2, w_route_group, b_route_group, w_route_expert, b_route_expert, w_gate, w_up, w_down, norm_f):
    depth = w_in.shape[0]
    nb, seq, d = x_prompt.shape
    ns, nt, _ = x_sample.shape
    assert nt <= CHUNK
    nf = norm_f.reshape(1, d).astype(F32)
    wts = _prep_weights(norm1, w_in, ln_g, ln_b, w_s, b_s, conv_w, conv_b, lru_lambda, w_rg_a,
                        b_rg_a, w_rg_x, b_rg_x, w_out, norm2, w_route_group, b_route_group,
                        w_route_expert, b_route_expert, w_gate, w_up, w_down, nt)
    conv_state = jnp.transpose(state_conv, (0, 2, 1, 3))
    tile = min(MOE_TILE, nt * ns)
    yp = x_prompt
    ys = jnp.transpose(x_sample, (1, 0, 2))
    hp, cp, vp, hs, cs, vs = [], [], [], [], [], []
    for l in range(depth):
        yp, h, c, v = _mixer_prompt(yp, wts, l)
        hp.append(h)
        cp.append(c)
        vp.append(v)
        ys, h, c, v = _mixer_sample(ys, state_lru_h, conv_state, wts, l)
        hs.append(h)
        cs.append(c)
        vs.append(v)
        yp, ys = _moe(yp.reshape(nb * seq, d), ys.reshape(nt * ns, d), wts, l, nf,
                      l == depth - 1, tile)
        yp = yp.reshape(nb, seq, d)
        ys = ys.reshape(nt, ns, d)
    to_seq_major = lambda a: jnp.transpose(jnp.stack(a), (0, 2, 1, 3))
    return (yp, jnp.transpose(ys, (1, 0, 2)), jnp.stack(hp), jnp.stack(cp), jnp.stack(vp),
            jnp.stack(hs), to_seq_major(cs), to_seq_major(vs))
```

```python
import functools

import jax
import jax.numpy as jnp
from jax import lax
from jax.experimental import pallas as pl
from jax.experimental.pallas import tpu as pltpu

F32 = jnp.float32
BF16 = jnp.bfloat16

CHUNK = 128
H_A = 4
H_B = 8
CONV_W = 4
LRU_C = 8.0
N_GROUPS = 4
E_PER_GROUP = 8
N_EXPERTS = N_GROUPS * E_PER_GROUP
EPS = 1e-6

LANES = 128
SUBLANES = 8
SCAN_UNROLL = 4
SEQS_PER_TRIP = 8
ROUTE_ROWS = 48
MOE_TILE = 1024
MOE_BLOCK = 512
VMEM_LIMIT = 60 * 1024 * 1024


def _dot(a, b):
    return jnp.dot(a, b, preferred_element_type=F32)


def _sigmoid(x):
    return 0.5 * (jnp.tanh(0.5 * x) + 1.0)


def _gelu2(x):
    c = 0.7978845608028654
    return x * (1.0 + jnp.tanh(x * (c + (c * 0.044715) * (x * x))))


def _rms_norm(x, g):
    return x * lax.rsqrt(jnp.mean(x * x, axis=-1, keepdims=True) + EPS) * g


def _layer_norm(x, g, b, eps):
    xc = x - jnp.mean(x, axis=-1, keepdims=True)
    var = jnp.mean(xc * xc, axis=-1, keepdims=True)
    return xc * lax.rsqrt(var + eps) * g + b


def _softplus(x):
    return jnp.maximum(x, 0.0) + jnp.log(1.0 + jnp.exp(-jnp.abs(x)))


def _lru_gates(xc, wbd_ref, b_a, b_x, lam_half):
    d = xc.shape[-1]
    xcb = xc.astype(BF16)
    r_parts, i_parts = [], []
    for q in range(d // 256):
        ri = _dot(xcb[:, q * 256:(q + 1) * 256], wbd_ref[q])
        r_parts.append(ri[:, :256])
        i_parts.append(ri[:, 256:])
    r2 = 1.0 + jnp.tanh(jnp.concatenate(r_parts, axis=-1) + b_a)
    i2 = 1.0 + jnp.tanh(jnp.concatenate(i_parts, axis=-1) + b_x)
    a = jnp.exp(lam_half * r2)
    mult_half = jnp.sqrt(0.25 - 0.25 * (a * a))
    return a, mult_half, i2 * xc


def _shift_rows(x, hist, j):
    sh = pltpu.roll(x, j, axis=0)
    top = jnp.where(lax.broadcasted_iota(jnp.int32, (SUBLANES, 1), 0) < j,
                    pltpu.roll(hist, j, axis=0), sh[:SUBLANES])
    return jnp.concatenate([top, sh[SUBLANES:]], axis=0)


def _mixer_prompt_kernel(x_ref, norm_ref, win_ref, lng_ref, lnb_ref, ws_ref, bs_ref,
                         cw_ref, cb_ref, lam_ref, wbd_ref, ba_ref, bx_ref, wout_ref,
                         y_ref, h_ref, conv_ref, v_ref,
                         hist_s, a_s, b_s, ya_s, gb_s, hc_s):
    i = pl.program_id(0)
    last = pl.num_programs(0) - 1
    nseq, _, d = x_ref.shape
    nlt = d // LANES
    spt = SEQS_PER_TRIP
    pair = spt * CHUNK

    @pl.when(i == 0)
    def _():
        hist_s[...] = jnp.zeros_like(hist_s)
        hc_s[...] = jnp.zeros_like(hc_s)

    lam_half = (-0.5 * LRU_C) * _softplus(-lam_ref[...])
    is_first = i == 0

    def branch_body(p, c):
        row = pl.multiple_of(p * pair, pair)
        x = jnp.concatenate([x_ref[spt * p + k] for k in range(spt)], axis=0)
        xnb = _rms_norm(x, norm_ref[...]).astype(BF16)
        v = _layer_norm(_gelu2(_dot(xnb, win_ref[:, d:2 * d])), lng_ref[...], lnb_ref[...],
                        4.0 * EPS)

        @pl.when(i == last)
        def _():
            for k in range(spt):
                v_ref[spt * p + k] = v[k * CHUNK:(k + 1) * CHUNK]

        vb = v.astype(BF16)
        dh = d // H_A
        z_rows = []
        for k in range(spt):
            zs = [_dot(ws_ref[h], vb[k * CHUNK:(k + 1) * CHUNK, h * dh:(h + 1) * dh])
                  for h in range(H_A)]
            z_rows.append(jnp.concatenate(zs, axis=-1) + bs_ref[...])
        z = jnp.concatenate(z_rows, axis=0)
        ya = _gelu2(_dot(xnb, win_ref[:, 0:d])) * z
        ga2 = 1.0 + jnp.tanh(_dot(xnb, win_ref[:, 3 * d:4 * d]))
        ya_s[pl.ds(row, pair), :] = (ga2 * ya).astype(BF16)
        gb_s[pl.ds(row, pair), :] = (1.0 + jnp.tanh(_dot(xnb, win_ref[:, 4 * d:5 * d]))).astype(BF16)
        xr = _dot(xnb, win_ref[:, 2 * d:3 * d])
        xcs = []
        for k in range(spt):
            s = spt * p + k
            xk = xr[k * CHUNK:(k + 1) * CHUNK]
            hist = hist_s[s]
            xc = cb_ref[...] + xk * cw_ref[CONV_W - 1:CONV_W, :]
            for j in range(1, CONV_W):
                xc = xc + _shift_rows(xk, hist, j) * cw_ref[CONV_W - 1 - j:CONV_W - j, :]
            hist_s[s] = xk[CHUNK - SUBLANES:]
            xcs.append(xc)
        xc = jnp.concatenate(xcs, axis=0)
        a, mult_half, ix2 = _lru_gates(xc, wbd_ref, ba_ref[...], bx_ref[...], lam_half)
        bt = mult_half * ix2
        for k in range(spt):
            s = spt * p + k
            for j in range(nlt):
                a_s[j, pl.ds(s, CHUNK, stride=nseq), :] = a[k * CHUNK:(k + 1) * CHUNK, j * LANES:(j + 1) * LANES]
                b_s[j, pl.ds(s, CHUNK, stride=nseq), :] = bt[k * CHUNK:(k + 1) * CHUNK, j * LANES:(j + 1) * LANES]

        @pl.when(is_first)
        def _():
            for k in range(spt):
                first = 0.5 * ix2[k * CHUNK:k * CHUNK + 1]
                for j in range(nlt):
                    b_s[j, pl.ds(spt * p + k, 1), :] = first[:, j * LANES:(j + 1) * LANES]

        return c

    lax.fori_loop(0, nseq // spt, branch_body, 0)

    def scan_body(tt, hs):
        for u in range(SCAN_UNROLL):
            row = pl.multiple_of((tt * SCAN_UNROLL + u) * nseq, nseq)
            out = []
            for j in range(nlt):
                h = a_s[j, pl.ds(row, nseq), :] * hs[j] + b_s[j, pl.ds(row, nseq), :]
                b_s[j, pl.ds(row, nseq), :] = h
                out.append(h)
            hs = tuple(out)
        return hs

    h0 = tuple(hc_s[:, j * LANES:(j + 1) * LANES] for j in range(nlt))
    hs = lax.fori_loop(0, CHUNK // SCAN_UNROLL, scan_body, h0)
    hfin = jnp.concatenate(hs, axis=-1)
    hc_s[...] = hfin
    h_ref[...] = hfin

    def out_body(p, c):
        row = pl.multiple_of(p * pair, pair)
        hrows = []
        for k in range(spt):
            hrows.append(jnp.concatenate(
                [b_s[j, pl.ds(spt * p + k, CHUNK, stride=nseq), :] for j in range(nlt)], axis=-1))
        h = jnp.concatenate(hrows, axis=0)
        merged = (ya_s[pl.ds(row, pair), :].astype(F32)
                  + gb_s[pl.ds(row, pair), :].astype(F32) * h).astype(BF16)
        o = _dot(merged, wout_ref[...])
        for k in range(spt):
            y_ref[spt * p + k] = x_ref[spt * p + k] + o[k * CHUNK:(k + 1) * CHUNK]
        return c

    lax.fori_loop(0, nseq // spt, out_body, 0)

    @pl.when(i == last)
    def _():
        conv_ref[...] = hist_s[:, SUBLANES - (CONV_W - 1):, :]


def _const_spec(shape):
    nd = len(shape)
    return pl.BlockSpec(shape, lambda i, *_, _n=nd: (0,) * _n, pipeline_mode=pl.Buffered(1))


def _layer_spec(stacked, layer):
    shape = stacked.shape[1:]
    nd = len(shape)
    return pl.BlockSpec((None,) + tuple(shape), lambda i, *_, _n=nd: (layer,) + (0,) * _n,
                        pipeline_mode=pl.Buffered(1))


MIXER_WEIGHTS = ('norm1', 'w_in', 'ln_g', 'ln_b', 'ws', 'bs', 'conv_w', 'conv_b', 'lam', 'wbd',
                 'b_a', 'b_x', 'w_out')


def _mixer_prompt(x, wts, layer):
    nseq, seq, d = x.shape
    assert seq % CHUNK == 0 and nseq == SUBLANES and d % 256 == 0
    n_chunks = seq // CHUNK
    nlt = d // LANES
    weights = tuple(wts[k] for k in MIXER_WEIGHTS)
    x_spec = pl.BlockSpec((nseq, CHUNK, d), lambda i: (0, i, 0))
    out_shape = (
        jax.ShapeDtypeStruct((nseq, seq, d), F32),
        jax.ShapeDtypeStruct((nseq, d), F32),
        jax.ShapeDtypeStruct((nseq, CONV_W - 1, d), F32),
        jax.ShapeDtypeStruct((nseq, CHUNK, d), F32),
    )
    out_specs = (
        x_spec,
        _const_spec((nseq, d)),
        _const_spec((nseq, CONV_W - 1, d)),
        _const_spec((nseq, CHUNK, d)),
    )
    scratch = [
        pltpu.VMEM((nseq, SUBLANES, d), F32),
        pltpu.VMEM((nlt, CHUNK * nseq, LANES), F32),
        pltpu.VMEM((nlt, CHUNK * nseq, LANES), F32),
        pltpu.VMEM((nseq * CHUNK, d), BF16),
        pltpu.VMEM((nseq * CHUNK, d), BF16),
        pltpu.VMEM((nseq, d), F32),
    ]
    return pl.pallas_call(
        _mixer_prompt_kernel,
        grid=(n_chunks,),
        in_specs=[x_spec] + [_layer_spec(w, layer) for w in weights],
        out_specs=out_specs,
        out_shape=out_shape,
        scratch_shapes=scratch,
        compiler_params=pltpu.CompilerParams(
            dimension_semantics=("arbitrary",), vmem_limit_bytes=VMEM_LIMIT),
        name="mixer_prompt",
    )(x, *weights)


def _mixer_sample_kernel(x_ref, h0_ref, cst_ref, norm_ref, win_ref, lng_ref, lnb_ref, wsc_ref,
                         bsc_ref, cw_ref, cb_ref, lam_ref, wbd_ref, ba_ref, bx_ref, wout_ref,
                         y_ref, h_ref, conv_ref, v_ref):
    nt, ns, d = x_ref.shape
    at = lambda a, t: a[t * ns:(t + 1) * ns]
    lam_half = (-0.5 * LRU_C) * _softplus(-lam_ref[...])
    x = x_ref[...].reshape(nt * ns, d)
    xnb = _rms_norm(x, norm_ref[...]).astype(BF16)
    v = _layer_norm(_gelu2(_dot(xnb, win_ref[:, d:2 * d])), lng_ref[...], lnb_ref[...], 4.0 * EPS)
    v_ref[...] = v.reshape(nt, ns, d)
    zs = []
    for t in range(nt):
        z = bsc_ref[t:t + 1, :]
        for s in range(t + 1):
            z = z + wsc_ref[t, s:s + 1, :] * at(v, s)
        zs.append(z)
    ya = _gelu2(_dot(xnb, win_ref[:, 0:d])) * jnp.concatenate(zs, axis=0)
    ga2 = 1.0 + jnp.tanh(_dot(xnb, win_ref[:, 3 * d:4 * d]))
    gb2 = 1.0 + jnp.tanh(_dot(xnb, win_ref[:, 4 * d:5 * d]))
    xr = _dot(xnb, win_ref[:, 2 * d:3 * d])
    xp = [cst_ref[k] for k in range(CONV_W - 1)] + [at(xr, t) for t in range(nt)]
    xcs = []
    for t in range(nt):
        xc = cb_ref[...]
        for k in range(CONV_W):
            xc = xc + xp[t + k] * cw_ref[k:k + 1, :]
        xcs.append(xc)
    a, mult_half, ix2 = _lru_gates(jnp.concatenate(xcs, axis=0), wbd_ref, ba_ref[...],
                                   bx_ref[...], lam_half)
    bt = mult_half * ix2
    h = h0_ref[...]
    hs = []
    for t in range(nt):
        h = at(a, t) * h + at(bt, t)
        hs.append(h)
    merged = (ga2 * ya + gb2 * jnp.concatenate(hs, axis=0)).astype(BF16)
    y_ref[...] = (x + _dot(merged, wout_ref[...])).reshape(nt, ns, d)
    h_ref[...] = h
    for k in range(CONV_W - 1):
        conv_ref[k] = xp[nt + k]


def _mixer_sample(x, h0, conv_state, wts, layer):
    nt, ns, d = x.shape
    names = MIXER_WEIGHTS[:4] + ('wsc', 'bsc') + MIXER_WEIGHTS[6:]
    stacked = (h0, conv_state) + tuple(wts[k] for k in names)
    out_shape = (
        jax.ShapeDtypeStruct((nt, ns, d), F32),
        jax.ShapeDtypeStruct((ns, d), F32),
        jax.ShapeDtypeStruct((CONV_W - 1, ns, d), F32),
        jax.ShapeDtypeStruct((nt, ns, d), F32),
    )
    return pl.pallas_call(
        _mixer_sample_kernel,
        grid=(1,),
        in_specs=[_const_spec(x.shape)] + [_layer_spec(w, layer) for w in stacked],
        out_specs=tuple(_const_spec(s.shape) for s in out_shape),
        out_shape=out_shape,
        compiler_params=pltpu.CompilerParams(
            dimension_semantics=("arbitrary",), vmem_limit_bytes=VMEM_LIMIT),
        name="mixer_sample",
    )(x, *stacked)


def _to_slab(ref, val):
    t, d = val.shape
    nlt = d // LANES
    for j in range(nlt):
        ref[pl.ds(j, t, stride=nlt), :] = val[:, j * LANES:(j + 1) * LANES]


def _from_slab(ref, t, nlt):
    return jnp.concatenate([ref[pl.ds(j, t, stride=nlt), :] for j in range(nlt)], axis=-1)


def _tokens_from_slab(slab_ref, nlt):
    return slab_ref[...].reshape(slab_ref.shape[0] // nlt, nlt, LANES).astype(BF16)


def _tokens_to_slab(slab_ref, tokens):
    slab_ref[...] = tokens.astype(F32).reshape(slab_ref.shape)


def _router_kernel(yp_ref, ys_ref, norm_ref, wr_ref, br_ref, tri_ref, low_ref, xn_ref, mf_ref,
                   mi_ref, cnt_ref, slab_s, *, n_prompt_tiles):
    y = jnp.where(pl.program_id(0) < n_prompt_tiles, yp_ref[...], ys_ref[...])
    xn = _rms_norm(y, norm_ref[...])
    _to_slab(slab_s, xn)
    xn_ref[...] = _tokens_from_slab(slab_s, xn_ref.shape[1])
    lt = lax.dot_general(wr_ref[...], xn.astype(BF16), (((1,), (1,)), ((), ())),
                         preferred_element_type=F32) + br_ref[...]
    tile = lt.shape[1]
    row = lax.broadcasted_iota(jnp.int32, (SUBLANES, tile), 0).astype(F32)
    neg = jnp.float32(-jnp.inf)
    big = jnp.float32(SUBLANES)

    def argmax_first(vals):
        m = jnp.max(vals, axis=0, keepdims=True)
        return m, jnp.min(jnp.where(vals == m, row, big), axis=0, keepdims=True)

    def group_block(x, gi):
        out = x[SUBLANES:2 * SUBLANES]
        for g in range(1, N_GROUPS):
            out = jnp.where(gi == g, x[(g + 1) * SUBLANES:(g + 2) * SUBLANES], out)
        return out

    gl = jnp.where(row < N_GROUPS, lt[0:SUBLANES], neg)
    gm, gi = argmax_first(gl)
    pg_top = 1.0 / jnp.sum(jnp.exp(gl - gm), axis=0, keepdims=True)
    el = group_block(lt, gi)
    m1, i1 = argmax_first(el)
    m2, i2 = argmax_first(jnp.where(row == i1, neg, el))
    e2 = jnp.exp(m2 - m1)
    w1 = pg_top / (1.0 + e2)
    w2 = pg_top * e2 / (1.0 + e2)
    hot1 = row == i1
    hot2 = row == i2
    picked = jnp.where(jnp.logical_or(hot1, hot2), 1.0, 0.0)
    zero = jnp.zeros_like(picked)
    onehot = jnp.concatenate(
        [zero] + [jnp.where(gi == g, picked, 0.0) for g in range(N_GROUPS)] + [zero],
        axis=0).astype(BF16)
    before = _dot(onehot, tri_ref[...])
    lower = jnp.sum(_dot(low_ref[...], onehot), axis=1, keepdims=True)
    slot = group_block(before + lower, gi)
    p1 = jnp.sum(jnp.where(hot1, slot, 0.0), axis=0, keepdims=True)
    p2 = jnp.sum(jnp.where(hot2, slot, 0.0), axis=0, keepdims=True)
    meta = jnp.zeros((SUBLANES, tile), F32)
    for k, c in enumerate((gi * E_PER_GROUP + i1, gi * E_PER_GROUP + i2, w1, w2, p1, p2)):
        meta = jnp.where(row == k, c, meta)
    mf_ref[0] = meta
    mi_ref[0] = meta.astype(jnp.int32)
    cnt = jnp.sum(onehot.astype(F32), axis=1, keepdims=True)
    cnt_ref[0] = jnp.broadcast_to(cnt, cnt_ref.shape[1:])


def _token_specs(tile, d, n_prompt_tiles):
    last = n_prompt_tiles - 1
    return (pl.BlockSpec((tile, d), lambda i, *_: (jnp.minimum(i, last), 0)),
            pl.BlockSpec((tile, d), lambda i, *_: (jnp.maximum(i - n_prompt_tiles, 0), 0)))


def _router(yp, ys, wts, layer, tile):
    d = yp.shape[1]
    nlt = d // LANES
    n_prompt_tiles = yp.shape[0] // tile
    n_tiles = n_prompt_tiles + ys.shape[0] // tile
    tri = jnp.triu(jnp.ones((tile, tile), BF16), 1)
    low = jnp.tril(jnp.ones((ROUTE_ROWS, ROUTE_ROWS), BF16), -1)
    meta_spec = pl.BlockSpec((1, SUBLANES, tile), lambda i: (i, 0, 0))
    return pl.pallas_call(
        functools.partial(_router_kernel, n_prompt_tiles=n_prompt_tiles),
        grid=(n_tiles,),
        in_specs=list(_token_specs(tile, d, n_prompt_tiles)) + [
            _layer_spec(wts['norm2'], layer), _layer_spec(wts['w_route'], layer),
            _layer_spec(wts['b_route'], layer), _const_spec((tile, tile)),
            _const_spec((ROUTE_ROWS, ROUTE_ROWS))],
        out_specs=(pl.BlockSpec((tile, nlt, LANES), lambda i: (i, 0, 0)), meta_spec, meta_spec,
                   pl.BlockSpec((1, ROUTE_ROWS, LANES), lambda i: (i, 0, 0))),
        out_shape=(jax.ShapeDtypeStruct((n_tiles * tile, nlt, LANES), BF16),
                   jax.ShapeDtypeStruct((n_tiles, SUBLANES, tile), F32),
                   jax.ShapeDtypeStruct((n_tiles, SUBLANES, tile), jnp.int32),
                   jax.ShapeDtypeStruct((n_tiles, ROUTE_ROWS, LANES), F32)),
        scratch_shapes=[pltpu.VMEM((tile * nlt, LANES), F32)],
        compiler_params=pltpu.CompilerParams(
            dimension_semantics=("arbitrary",), vmem_limit_bytes=VMEM_LIMIT),
        name="moe_router",
    )(yp, ys, wts['norm2'], wts['w_route'], wts['b_route'], tri, low)


def _run_copy(tile_idx, e, lo_ref, cnt_ref, dst_ref, local, remote, sem, to_remote):
    k = tile_idx * N_EXPERTS + e
    loc = local.at[pl.ds(lo_ref[k], cnt_ref[k])]
    rem = remote.at[pl.ds(dst_ref[k], cnt_ref[k])]
    return pltpu.make_async_copy(loc, rem, sem) if to_remote else pltpu.make_async_copy(rem, loc, sem)


def _start_runs(tile_idx, lo_ref, cnt_ref, dst_ref, local, remote, sem, to_remote):
    for e in range(N_EXPERTS):
        @pl.when(cnt_ref[tile_idx * N_EXPERTS + e] > 0)
        def _():
            _run_copy(tile_idx, e, lo_ref, cnt_ref, dst_ref, local, remote, sem, to_remote).start()


def _wait_runs(local, sem):
    pltpu.make_async_copy(local, local, sem).wait()


def _dispatch_kernel(lo_ref, cnt_ref, dst_ref, padst_ref, padn_ref,
                     s1_ref, s2_ref, x_ref, xs_ref, loc, zeros, sem, *, unroll):
    i = pl.program_id(0)
    last = pl.num_programs(0) - 1
    tile = x_ref.shape[0]
    slot = i % 2
    buf = loc.at[slot]

    @pl.when(i >= 2)
    def _():
        _wait_runs(buf, sem.at[slot])

    def body(tt, c):
        for u in range(unroll):
            t = tt * unroll + u
            row = x_ref[t]
            buf[s1_ref[t]] = row
            buf[s2_ref[t]] = row
        return c

    lax.fori_loop(0, tile // unroll, body, 0)
    _start_runs(i, lo_ref, cnt_ref, dst_ref, buf, xs_ref, sem.at[slot], True)

    @pl.when(i == last)
    def _():
        zeros[...] = jnp.zeros_like(zeros)

        def pad_copy(e):
            n = padn_ref[e]
            dst = xs_ref.at[pl.ds(padst_ref[e], n)]
            return pltpu.make_async_copy(zeros.at[pl.ds(0, n)], dst, sem.at[2])

        def each_pad(fn):
            def b(e, c):
                @pl.when(padn_ref[e] > 0)
                def _():
                    fn(e)
                return c
            lax.fori_loop(0, padn_ref.shape[0], b, 0)

        each_pad(lambda e: pad_copy(e).start())
        _wait_runs(buf, sem.at[slot])

        @pl.when(i >= 1)
        def _():
            _wait_runs(loc.at[1 - slot], sem.at[1 - slot])

        each_pad(lambda e: pad_copy(e).wait())


def _smem_tile(tile):
    return pl.BlockSpec((tile,), lambda i, *_: (i,), memory_space=pltpu.SMEM)


def _dispatch(xn_slab, s1, s2, tables, tile, n_rows):
    token = xn_slab.shape[1:]
    return pl.pallas_call(
        functools.partial(_dispatch_kernel, unroll=8),
        grid_spec=pltpu.PrefetchScalarGridSpec(
            num_scalar_prefetch=5,
            grid=(xn_slab.shape[0] // tile,),
            in_specs=[_smem_tile(tile), _smem_tile(tile),
                      pl.BlockSpec((tile,) + token, lambda i, *_: (i, 0, 0))],
            out_specs=pl.BlockSpec(memory_space=pl.ANY),
            scratch_shapes=[pltpu.VMEM((2, 2 * tile) + token, xn_slab.dtype),
                            pltpu.VMEM((MOE_BLOCK,) + token, xn_slab.dtype),
                            pltpu.SemaphoreType.DMA((3,))]),
        out_shape=jax.ShapeDtypeStruct((n_rows,) + token, xn_slab.dtype),
        compiler_params=pltpu.CompilerParams(
            dimension_semantics=("arbitrary",), vmem_limit_bytes=VMEM_LIMIT),
        name="moe_dispatch",
    )(*tables, s1, s2, xn_slab)


def _experts_kernel(blk_ref, nused_ref, xs_ref, wg_ref, wu_ref, wd_ref, ys_ref,
                    wg_s, wu_s, wd_s, slab_in, slab_out):
    b = pl.program_id(0)
    rows, nlt, _ = xs_ref.shape

    @pl.when(jnp.logical_or(b == 0, blk_ref[b] != blk_ref[jnp.maximum(b - 1, 0)]))
    def _():
        wg_s[...] = wg_ref[...].astype(BF16)
        wu_s[...] = wu_ref[...].astype(BF16)
        wd_s[...] = wd_ref[...].astype(BF16)

    @pl.when(b < nused_ref[0])
    def _():
        _tokens_to_slab(slab_in, xs_ref[...])
        x = _from_slab(slab_in, rows, nlt).astype(BF16)
        hg = _dot(x, wg_s[...])
        hu = _dot(x, wu_s[...])
        h = (hg * _sigmoid(hg) * hu).astype(BF16)
        _to_slab(slab_out, _dot(h, wd_s[...]))
        ys_ref[...] = _tokens_from_slab(slab_out, nlt)

    @pl.when(b >= nused_ref[0])
    def _():
        ys_ref[...] = jnp.zeros_like(ys_ref)


def _experts(xs, blk_expert, n_used, wts, layer):
    n_rows, nlt, _ = xs.shape
    d = nlt * LANES
    f = wts['w_gate'].shape[-1]
    rows = pl.BlockSpec((MOE_BLOCK, nlt, LANES), lambda b, blk, nu: (b, 0, 0))
    w_spec = lambda r, c: pl.BlockSpec((None, None, r, c), lambda b, blk, nu: (layer, blk[b], 0, 0))
    return pl.pallas_call(
        _experts_kernel,
        grid_spec=pltpu.PrefetchScalarGridSpec(
            num_scalar_prefetch=2,
            grid=(n_rows // MOE_BLOCK,),
            in_specs=[rows, w_spec(d, f), w_spec(d, f), w_spec(f, d)],
            out_specs=rows,
            scratch_shapes=[pltpu.VMEM((d, f), BF16), pltpu.VMEM((d, f), BF16),
                            pltpu.VMEM((f, d), BF16),
                            pltpu.VMEM((MOE_BLOCK * nlt, LANES), F32),
                            pltpu.VMEM((MOE_BLOCK * nlt, LANES), F32)]),
        out_shape=jax.ShapeDtypeStruct(xs.shape, xs.dtype),
        compiler_params=pltpu.CompilerParams(
            dimension_semantics=("arbitrary",), vmem_limit_bytes=VMEM_LIMIT),
        name="moe_experts",
    )(blk_expert, n_used, xs, wts['w_gate'], wts['w_up'], wts['w_down'])


def _combine_kernel(lo_ref, cnt_ref, dst_ref, s1_ref, s2_ref, w1_ref, w2_ref, yp_ref, ys_ref,
                    nf_ref, rows_ref, op_ref, os_ref, loc, acc, sem,
                    *, unroll, final_norm, n_prompt_tiles):
    i = pl.program_id(0)
    tile = yp_ref.shape[0]
    nlt = loc.shape[2]
    slot = i % 2
    buf = loc.at[slot]
    fetch = lambda t, s: _start_runs(t, lo_ref, cnt_ref, dst_ref, loc.at[s], rows_ref, sem.at[s],
                                     False)

    @pl.when(i == 0)
    def _():
        fetch(0, 0)

    @pl.when(i + 1 < pl.num_programs(0))
    def _():
        fetch(i + 1, 1 - slot)

    _wait_runs(buf, sem.at[slot])

    def body(tt, c):
        for u in range(unroll):
            t = tt * unroll + u
            r1 = buf[s1_ref[t]].astype(F32)
            r2 = buf[s2_ref[t]].astype(F32)
            acc[pl.ds(pl.multiple_of(t * nlt, nlt), nlt), :] = w1_ref[t] * r1 + w2_ref[t] * r2
        return c

    lax.fori_loop(0, tile // unroll, body, 0)
    def finish(y_ref, o_ref):
        out = y_ref[...] + _from_slab(acc, tile, nlt)
        if final_norm:
            out = _rms_norm(out, nf_ref[...])
        o_ref[...] = out

    is_prompt = i < n_prompt_tiles
    pl.when(is_prompt)(lambda: finish(yp_ref, op_ref))
    pl.when(jnp.logical_not(is_prompt))(lambda: finish(ys_ref, os_ref))


def _combine(yp, ys, rows, meta, tables, norm_f, final_norm, tile):
    d = yp.shape[1]
    token = rows.shape[1:]
    n_prompt_tiles = yp.shape[0] // tile
    n_tiles = n_prompt_tiles + ys.shape[0] // tile
    tok_p, tok_s = _token_specs(tile, d, n_prompt_tiles)
    return pl.pallas_call(
        functools.partial(_combine_kernel, unroll=8, final_norm=final_norm,
                          n_prompt_tiles=n_prompt_tiles),
        grid_spec=pltpu.PrefetchScalarGridSpec(
            num_scalar_prefetch=3,
            grid=(n_tiles,),
            in_specs=[_smem_tile(tile)] * 4 + [tok_p, tok_s, _const_spec((1, d)),
                                               pl.BlockSpec(memory_space=pl.ANY)],
            out_specs=(tok_p, tok_s),
            scratch_shapes=[pltpu.VMEM((2, 2 * tile) + token, rows.dtype),
                            pltpu.VMEM((tile * token[0], LANES), F32),
                            pltpu.SemaphoreType.DMA((2,))]),
        out_shape=(jax.ShapeDtypeStruct(yp.shape, F32), jax.ShapeDtypeStruct(ys.shape, F32)),
        compiler_params=pltpu.CompilerParams(
            dimension_semantics=("arbitrary",), vmem_limit_bytes=VMEM_LIMIT),
        name="moe_combine",
    )(*tables, *meta, yp, ys, norm_f, rows)


def _moe(yp, ys, wts, layer, norm_f, final_norm, tile):
    d = yp.shape[1]
    n = yp.shape[0] + ys.shape[0]
    assert yp.shape[0] % tile == 0 and ys.shape[0] % tile == 0 and d % LANES == 0
    xn_slab, meta_f, meta_i, cnt = _router(yp, ys, wts, layer, tile)
    cnt = cnt[:, SUBLANES:SUBLANES + N_EXPERTS, 0].astype(jnp.int32)
    lo = jnp.cumsum(cnt, axis=1) - cnt
    seg = jnp.sum(cnt, axis=0)
    seg_pad = (seg + MOE_BLOCK - 1) // MOE_BLOCK * MOE_BLOCK
    seg_end = jnp.cumsum(seg_pad)
    seg_start = seg_end - seg_pad
    dst = seg_start[None, :] + jnp.cumsum(cnt, axis=0) - cnt
    n_blocks = (2 * n + N_EXPERTS * (MOE_BLOCK - 1)) // MOE_BLOCK
    blk_ids = jnp.arange(n_blocks, dtype=jnp.int32)
    blk_expert = jnp.minimum(
        jnp.sum((seg_end[None, :] <= blk_ids[:, None] * MOE_BLOCK).astype(jnp.int32), axis=1),
        N_EXPERTS - 1)
    n_used = seg_end[-1:] // MOE_BLOCK
    pad_start = jnp.concatenate([seg_start + seg, blk_ids * MOE_BLOCK])
    pad_rows = jnp.concatenate([seg_pad - seg, jnp.where(blk_ids >= n_used[0], MOE_BLOCK, 0)])
    flat = lambda a: a.reshape(-1)
    runs = (flat(lo), flat(cnt), flat(dst))
    s1, s2 = flat(meta_i[:, 4]), flat(meta_i[:, 5])
    w1, w2 = flat(meta_f[:, 2]), flat(meta_f[:, 3])
    xs = _dispatch(xn_slab, s1, s2, runs + (pad_start, pad_rows), tile, n_blocks * MOE_BLOCK)
    rows = _experts(xs, blk_expert, n_used, wts, layer)
    return _combine(yp, ys, rows, (s1, s2, w1, w2), runs, norm_f, final_norm, tile)


def _prep_weights(norm1, w_in, ln_g, ln_b, w_s, b_s, conv_w, conv_b, lru_lambda, w_rg_a, b_rg_a,
                  w_rg_x, b_rg_x, w_out, norm2, w_route_group, b_route_group, w_route_expert,
                  b_route_expert, w_gate, w_up, w_down, n_sample_t):
    depth, d = w_in.shape[0], w_in.shape[1]
    dh = d // H_A
    row = lambda a: a.reshape(depth, 1, -1).astype(F32)
    causal = jnp.tril(jnp.ones((CHUNK, CHUNK), dtype=bool))
    ws = jnp.where(causal, 0.5 * w_s, 0.0)
    bs = jnp.repeat(jnp.swapaxes(0.5 * b_s, 1, 2), dh, axis=2)
    in_scale = jnp.concatenate([jnp.ones((3 * d,), F32), jnp.full((2 * d,), 0.5, F32)])

    def pair_bd(w):
        w = w.reshape(depth, H_B // 2, 2, w.shape[-2], w.shape[-1])
        z = jnp.zeros_like(w[:, :, 0])
        return jnp.concatenate([jnp.concatenate([w[:, :, 0], z], axis=-1),
                                jnp.concatenate([z, w[:, :, 1]], axis=-1)], axis=-2)

    gap = SUBLANES - N_GROUPS
    tail = ROUTE_ROWS - SUBLANES - N_EXPERTS
    w_route = jnp.concatenate(
        [jnp.swapaxes(w_route_group, 1, 2), jnp.zeros((depth, gap, d), F32),
         jnp.swapaxes(w_route_expert, 1, 2), jnp.zeros((depth, tail, d), F32)], axis=1)
    b_route = jnp.concatenate(
        [b_route_group, jnp.zeros((depth, gap), F32), b_route_expert,
         jnp.zeros((depth, tail), F32)], axis=1)[..., None]
    nt = n_sample_t
    return dict(
        norm1=row(norm1), w_in=(w_in * in_scale).astype(BF16), ln_g=row(ln_g), ln_b=row(ln_b),
        ws=ws.astype(BF16), bs=bs.astype(F32), conv_w=conv_w.astype(F32), conv_b=row(conv_b),
        lam=row(lru_lambda),
        wbd=(0.5 * jnp.concatenate([pair_bd(w_rg_a), pair_bd(w_rg_x)], axis=-1)).astype(BF16),
        b_a=row(0.5 * b_rg_a), b_x=row(0.5 * b_rg_x), w_out=(0.5 * w_out).astype(BF16),
        norm2=row(norm2), w_route=w_route.astype(BF16), b_route=b_route.astype(F32),
        w_gate=w_gate, w_up=w_up, w_down=w_down,
        wsc=jnp.repeat(jnp.transpose(ws[:, :, :nt, :nt], (0, 2, 3, 1)), dh, axis=3).astype(F32),
        bsc=bs[:, :nt].astype(F32),
    )


def kernel(x_prompt, x_sample, state_lru_h, state_conv, norm1, w_in, ln_g, ln_b, w_s, b_s, conv_w, conv_b, lru_lambda, w_rg_a, b_rg_a, w_rg_x, b_rg_x, w_out, norm2, w_route_group, b_route_group, w_route_expert, b_route_expert, w_gate, w_up, w_down, norm_f):
    depth = w_in.shape[0]
    nb, seq, d = x_prompt.shape
    ns, nt, _ = x_sample.shape
    assert nt <= CHUNK
    nf = norm_f.reshape(1, d).astype(F32)
    wts = _prep_weights(norm1, w_in, ln_g, ln_b, w_s, b_s, conv_w, conv_b, lru_lambda, w_rg_a,
                        b_rg_a, w_rg_x, b_rg_x, w_out, norm2, w_route_group, b_route_group,
                        w_route_expert, b_route_expert, w_gate, w_up, w_down, nt)
    conv_state = jnp.transpose(state_conv, (0, 2, 1, 3))
    assert (nb * seq) % MOE_TILE == 0
    sample_pad = (-nt * ns) % MOE_TILE
    yp = x_prompt
    ys = jnp.transpose(x_sample, (1, 0, 2))
    hp, cp, vp, hs, cs, vs = [], [], [], [], [], []
    for l in range(depth):
        yp, h, c, v = _mixer_prompt(yp, wts, l)
        hp.append(h)
        cp.append(c)
        vp.append(v)
        ys, h, c, v = _mixer_sample(ys, state_lru_h, conv_state, wts, l)
        hs.append(h)
        cs.append(c)
        vs.append(v)
        ys = jnp.pad(ys.reshape(nt * ns, d), ((0, sample_pad), (0, 0)))
        yp, ys = _moe(yp.reshape(nb * seq, d), ys, wts, l, nf, l == depth - 1, MOE_TILE)
        yp = yp.reshape(nb, seq, d)
        ys = ys[:nt * ns].reshape(nt, ns, d)
    to_seq_major = lambda a: jnp.transpose(jnp.stack(a), (0, 2, 1, 3))
    return (yp, jnp.transpose(ys, (1, 0, 2)), jnp.stack(hp), jnp.stack(cp), jnp.stack(vp),
            jnp.stack(hs), to_seq_major(cs), to_seq_major(vs))
```

```python
import functools

import jax
import jax.numpy as jnp
from jax import lax
from jax.experimental import pallas as pl
from jax.experimental.pallas import tpu as pltpu

F32 = jnp.float32
BF16 = jnp.bfloat16

CHUNK = 128
H_A = 4
H_B = 8
CONV_W = 4
LRU_C = 8.0
N_GROUPS = 4
E_PER_GROUP = 8
N_EXPERTS = N_GROUPS * E_PER_GROUP
EPS = 1e-6

LANES = 128
SUBLANES = 8
SCAN_UNROLL = 4
SEQS_PER_TRIP = 8
ROUTE_ROWS = 48
MOE_TILE = 1024
MOE_BLOCK = 512
VMEM_LIMIT = 60 * 1024 * 1024


def _dot(a, b):
    return jnp.dot(a, b, preferred_element_type=F32)


def _sigmoid(x):
    return 0.5 * (jnp.tanh(0.5 * x) + 1.0)


def _gelu2(x):
    c = 0.7978845608028654
    return x * (1.0 + jnp.tanh(x * (c + (c * 0.044715) * (x * x))))


def _rms_norm(x, g):
    return x * lax.rsqrt(jnp.mean(x * x, axis=-1, keepdims=True) + EPS) * g


def _layer_norm(x, g, b, eps):
    xc = x - jnp.mean(x, axis=-1, keepdims=True)
    var = jnp.mean(xc * xc, axis=-1, keepdims=True)
    return xc * lax.rsqrt(var + eps) * g + b


def _softplus(x):
    return jnp.maximum(x, 0.0) + jnp.log(1.0 + jnp.exp(-jnp.abs(x)))


def _lru_gates(xc, wbd_ref, b_a, b_x, lam_half):
    d = xc.shape[-1]
    xcb = xc.astype(BF16)
    r_parts, i_parts = [], []
    for q in range(d // 256):
        ri = _dot(xcb[:, q * 256:(q + 1) * 256], wbd_ref[q])
        r_parts.append(ri[:, :256])
        i_parts.append(ri[:, 256:])
    r2 = 1.0 + jnp.tanh(jnp.concatenate(r_parts, axis=-1) + b_a)
    i2 = 1.0 + jnp.tanh(jnp.concatenate(i_parts, axis=-1) + b_x)
    a = jnp.exp(lam_half * r2)
    mult_half = jnp.sqrt(0.25 - 0.25 * (a * a))
    return a, mult_half, i2 * xc


def _shift_rows(x, hist, j):
    sh = pltpu.roll(x, j, axis=0)
    top = jnp.where(lax.broadcasted_iota(jnp.int32, (SUBLANES, 1), 0) < j,
                    pltpu.roll(hist, j, axis=0), sh[:SUBLANES])
    return jnp.concatenate([top, sh[SUBLANES:]], axis=0)


def _mixer_prompt_kernel(x_ref, norm_ref, win_ref, lng_ref, lnb_ref, ws_ref, bs_ref,
                         cw_ref, cb_ref, lam_ref, wbd_ref, ba_ref, bx_ref, wout_ref,
                         y_ref, h_ref, conv_ref, v_ref,
                         hist_s, a_s, b_s, ya_s, gb_s, hc_s):
    i = pl.program_id(0)
    last = pl.num_programs(0) - 1
    nseq, _, d = x_ref.shape
    nlt = d // LANES
    spt = SEQS_PER_TRIP
    pair = spt * CHUNK

    @pl.when(i == 0)
    def _():
        hist_s[...] = jnp.zeros_like(hist_s)
        hc_s[...] = jnp.zeros_like(hc_s)

    lam_half = (-0.5 * LRU_C) * _softplus(-lam_ref[...])
    is_first = i == 0

    def branch_body(p, c):
        row = pl.multiple_of(p * pair, pair)
        x = jnp.concatenate([x_ref[spt * p + k] for k in range(spt)], axis=0)
        xnb = _rms_norm(x, norm_ref[...]).astype(BF16)
        v = _layer_norm(_gelu2(_dot(xnb, win_ref[:, d:2 * d])), lng_ref[...], lnb_ref[...],
                        4.0 * EPS)

        @pl.when(i == last)
        def _():
            for k in range(spt):
                v_ref[spt * p + k] = v[k * CHUNK:(k + 1) * CHUNK]

        vb = v.astype(BF16)
        dh = d // H_A
        z_rows = []
        for k in range(spt):
            zs = [_dot(ws_ref[h], vb[k * CHUNK:(k + 1) * CHUNK, h * dh:(h + 1) * dh])
                  for h in range(H_A)]
            z_rows.append(jnp.concatenate(zs, axis=-1) + bs_ref[...])
        z = jnp.concatenate(z_rows, axis=0)
        ya = _gelu2(_dot(xnb, win_ref[:, 0:d])) * z
        ga2 = 1.0 + jnp.tanh(_dot(xnb, win_ref[:, 3 * d:4 * d]))
        ya_s[pl.ds(row, pair), :] = (ga2 * ya).astype(BF16)
        gb_s[pl.ds(row, pair), :] = (1.0 + jnp.tanh(_dot(xnb, win_ref[:, 4 * d:5 * d]))).astype(BF16)
        xr = _dot(xnb, win_ref[:, 2 * d:3 * d])
        xcs = []
        for k in range(spt):
            s = spt * p + k
            xk = xr[k * CHUNK:(k + 1) * CHUNK]
            hist = hist_s[s]
            xc = cb_ref[...] + xk * cw_ref[CONV_W - 1:CONV_W, :]
            for j in range(1, CONV_W):
                xc = xc + _shift_rows(xk, hist, j) * cw_ref[CONV_W - 1 - j:CONV_W - j, :]
            hist_s[s] = xk[CHUNK - SUBLANES:]
            xcs.append(xc)
        xc = jnp.concatenate(xcs, axis=0)
        a, mult_half, ix2 = _lru_gates(xc, wbd_ref, ba_ref[...], bx_ref[...], lam_half)
        bt = mult_half * ix2
        for k in range(spt):
            s = spt * p + k
            for j in range(nlt):
                a_s[j, pl.ds(s, CHUNK, stride=nseq), :] = a[k * CHUNK:(k + 1) * CHUNK, j * LANES:(j + 1) * LANES]
                b_s[j, pl.ds(s, CHUNK, stride=nseq), :] = bt[k * CHUNK:(k + 1) * CHUNK, j * LANES:(j + 1) * LANES]

        @pl.when(is_first)
        def _():
            for k in range(spt):
                first = 0.5 * ix2[k * CHUNK:k * CHUNK + 1]
                for j in range(nlt):
                    b_s[j, pl.ds(spt * p + k, 1), :] = first[:, j * LANES:(j + 1) * LANES]

        return c

    lax.fori_loop(0, nseq // spt, branch_body, 0)

    def scan_body(tt, hs):
        for u in range(SCAN_UNROLL):
            row = pl.multiple_of((tt * SCAN_UNROLL + u) * nseq, nseq)
            out = []
            for j in range(nlt):
                h = a_s[j, pl.ds(row, nseq), :] * hs[j] + b_s[j, pl.ds(row, nseq), :]
                b_s[j, pl.ds(row, nseq), :] = h
                out.append(h)
            hs = tuple(out)
        return hs

    h0 = tuple(hc_s[:, j * LANES:(j + 1) * LANES] for j in range(nlt))
    hs = lax.fori_loop(0, CHUNK // SCAN_UNROLL, scan_body, h0)
    hfin = jnp.concatenate(hs, axis=-1)
    hc_s[...] = hfin
    h_ref[...] = hfin

    def out_body(p, c):
        row = pl.multiple_of(p * pair, pair)
        hrows = []
        for k in range(spt):
            hrows.append(jnp.concatenate(
                [b_s[j, pl.ds(spt * p + k, CHUNK, stride=nseq), :] for j in range(nlt)], axis=-1))
        h = jnp.concatenate(hrows, axis=0)
        merged = (ya_s[pl.ds(row, pair), :].astype(F32)
                  + gb_s[pl.ds(row, pair), :].astype(F32) * h).astype(BF16)
        o = _dot(merged, wout_ref[...])
        for k in range(spt):
            y_ref[spt * p + k] = x_ref[spt * p + k] + o[k * CHUNK:(k + 1) * CHUNK]
        return c

    lax.fori_loop(0, nseq // spt, out_body, 0)

    @pl.when(i == last)
    def _():
        conv_ref[...] = hist_s[:, SUBLANES - (CONV_W - 1):, :]


def _const_spec(shape):
    nd = len(shape)
    return pl.BlockSpec(shape, lambda i, *_, _n=nd: (0,) * _n, pipeline_mode=pl.Buffered(1))


def _layer_spec(stacked, layer):
    shape = stacked.shape[1:]
    nd = len(shape)
    return pl.BlockSpec((None,) + tuple(shape), lambda i, *_, _n=nd: (layer,) + (0,) * _n,
                        pipeline_mode=pl.Buffered(1))


MIXER_WEIGHTS = ('norm1', 'w_in', 'ln_g', 'ln_b', 'ws', 'bs', 'conv_w', 'conv_b', 'lam', 'wbd',
                 'b_a', 'b_x', 'w_out')


def _mixer_prompt(x, wts, layer):
    nseq, seq, d = x.shape
    assert seq % CHUNK == 0 and nseq == SUBLANES and d % 256 == 0
    n_chunks = seq // CHUNK
    nlt = d // LANES
    weights = tuple(wts[k] for k in MIXER_WEIGHTS)
    x_spec = pl.BlockSpec((nseq, CHUNK, d), lambda i: (0, i, 0))
    out_shape = (
        jax.ShapeDtypeStruct((nseq, seq, d), F32),
        jax.ShapeDtypeStruct((nseq, d), F32),
        jax.ShapeDtypeStruct((nseq, CONV_W - 1, d), F32),
        jax.ShapeDtypeStruct((nseq, CHUNK, d), F32),
    )
    out_specs = (
        x_spec,
        _const_spec((nseq, d)),
        _const_spec((nseq, CONV_W - 1, d)),
        _const_spec((nseq, CHUNK, d)),
    )
    scratch = [
        pltpu.VMEM((nseq, SUBLANES, d), F32),
        pltpu.VMEM((nlt, CHUNK * nseq, LANES), F32),
        pltpu.VMEM((nlt, CHUNK * nseq, LANES), F32),
        pltpu.VMEM((nseq * CHUNK, d), BF16),
        pltpu.VMEM((nseq * CHUNK, d), BF16),
        pltpu.VMEM((nseq, d), F32),
    ]
    return pl.pallas_call(
        _mixer_prompt_kernel,
        grid=(n_chunks,),
        in_specs=[x_spec] + [_layer_spec(w, layer) for w in weights],
        out_specs=out_specs,
        out_shape=out_shape,
        scratch_shapes=scratch,
        compiler_params=pltpu.CompilerParams(
            dimension_semantics=("arbitrary",), vmem_limit_bytes=VMEM_LIMIT),
        name="mixer_prompt",
    )(x, *weights)


def _mixer_sample_kernel(x_ref, h0_ref, cst_ref, norm_ref, win_ref, lng_ref, lnb_ref, wsc_ref,
                         bsc_ref, cw_ref, cb_ref, lam_ref, wbd_ref, ba_ref, bx_ref, wout_ref,
                         y_ref, h_ref, conv_ref, v_ref):
    nt, ns, d = x_ref.shape
    at = lambda a, t: a[t * ns:(t + 1) * ns]
    lam_half = (-0.5 * LRU_C) * _softplus(-lam_ref[...])
    x = x_ref[...].reshape(nt * ns, d)
    xnb = _rms_norm(x, norm_ref[...]).astype(BF16)
    v = _layer_norm(_gelu2(_dot(xnb, win_ref[:, d:2 * d])), lng_ref[...], lnb_ref[...], 4.0 * EPS)
    v_ref[...] = v.reshape(nt, ns, d)
    zs = []
    for t in range(nt):
        z = bsc_ref[t:t + 1, :]
        for s in range(t + 1):
            z = z + wsc_ref[t, s:s + 1, :] * at(v, s)
        zs.append(z)
    ya = _gelu2(_dot(xnb, win_ref[:, 0:d])) * jnp.concatenate(zs, axis=0)
    ga2 = 1.0 + jnp.tanh(_dot(xnb, win_ref[:, 3 * d:4 * d]))
    gb2 = 1.0 + jnp.tanh(_dot(xnb, win_ref[:, 4 * d:5 * d]))
    xr = _dot(xnb, win_ref[:, 2 * d:3 * d])
    xp = [cst_ref[k] for k in range(CONV_W - 1)] + [at(xr, t) for t in range(nt)]
    xcs = []
    for t in range(nt):
        xc = cb_ref[...]
        for k in range(CONV_W):
            xc = xc + xp[t + k] * cw_ref[k:k + 1, :]
        xcs.append(xc)
    a, mult_half, ix2 = _lru_gates(jnp.concatenate(xcs, axis=0), wbd_ref, ba_ref[...],
                                   bx_ref[...], lam_half)
    bt = mult_half * ix2
    h = h0_ref[...]
    hs = []
    for t in range(nt):
        h = at(a, t) * h + at(bt, t)
        hs.append(h)
    merged = (ga2 * ya + gb2 * jnp.concatenate(hs, axis=0)).astype(BF16)
    y_ref[0:nt] = (x + _dot(merged, wout_ref[...])).reshape(nt, ns, d)
    if y_ref.shape[0] > nt:
        y_ref[nt:] = jnp.zeros((y_ref.shape[0] - nt, ns, d), F32)
    h_ref[...] = h
    for k in range(CONV_W - 1):
        conv_ref[k] = xp[nt + k]


def _mixer_sample(x, h0, conv_state, wts, layer, nt, pad_t):
    _, ns, d = x.shape
    names = MIXER_WEIGHTS[:4] + ('wsc', 'bsc') + MIXER_WEIGHTS[6:]
    stacked = (h0, conv_state) + tuple(wts[k] for k in names)
    out_shape = (
        jax.ShapeDtypeStruct((nt + pad_t, ns, d), F32),
        jax.ShapeDtypeStruct((ns, d), F32),
        jax.ShapeDtypeStruct((CONV_W - 1, ns, d), F32),
        jax.ShapeDtypeStruct((nt, ns, d), F32),
    )
    return pl.pallas_call(
        _mixer_sample_kernel,
        grid=(1,),
        in_specs=[_const_spec((nt, ns, d))] + [_layer_spec(w, layer) for w in stacked],
        out_specs=tuple(_const_spec(s.shape) for s in out_shape),
        out_shape=out_shape,
        compiler_params=pltpu.CompilerParams(
            dimension_semantics=("arbitrary",), vmem_limit_bytes=VMEM_LIMIT),
        name="mixer_sample",
    )(x, *stacked)


def _to_slab(ref, val):
    t, d = val.shape
    nlt = d // LANES
    for j in range(nlt):
        ref[pl.ds(j, t, stride=nlt), :] = val[:, j * LANES:(j + 1) * LANES]


def _from_slab(ref, t, nlt):
    return jnp.concatenate([ref[pl.ds(j, t, stride=nlt), :] for j in range(nlt)], axis=-1)


def _tokens_from_slab(slab_ref, nlt):
    return slab_ref[...].reshape(slab_ref.shape[0] // nlt, nlt, LANES).astype(BF16)


def _tokens_to_slab(slab_ref, tokens):
    slab_ref[...] = tokens.astype(F32).reshape(slab_ref.shape)


def _router_kernel(yp_ref, ys_ref, norm_ref, wr_ref, br_ref, tri_ref, low_ref, xn_ref, s1_ref,
                   s2_ref, w1_ref, w2_ref, cnt_ref, slab_s, *, n_prompt_tiles):
    y = jnp.where(pl.program_id(0) < n_prompt_tiles, yp_ref[...], ys_ref[...])
    xn = _rms_norm(y, norm_ref[...])
    _to_slab(slab_s, xn)
    xn_ref[...] = _tokens_from_slab(slab_s, xn_ref.shape[1])
    lt = lax.dot_general(wr_ref[...], xn.astype(BF16), (((1,), (1,)), ((), ())),
                         preferred_element_type=F32) + br_ref[...]
    tile = lt.shape[1]
    row = lax.broadcasted_iota(jnp.int32, (SUBLANES, tile), 0).astype(F32)
    neg = jnp.float32(-jnp.inf)
    big = jnp.float32(SUBLANES)

    def argmax_first(vals):
        m = jnp.max(vals, axis=0, keepdims=True)
        return m, jnp.min(jnp.where(vals == m, row, big), axis=0, keepdims=True)

    def group_block(x, gi):
        out = x[SUBLANES:2 * SUBLANES]
        for g in range(1, N_GROUPS):
            out = jnp.where(gi == g, x[(g + 1) * SUBLANES:(g + 2) * SUBLANES], out)
        return out

    gl = jnp.where(row < N_GROUPS, lt[0:SUBLANES], neg)
    gm, gi = argmax_first(gl)
    pg_top = 1.0 / jnp.sum(jnp.exp(gl - gm), axis=0, keepdims=True)
    el = group_block(lt, gi)
    m1, i1 = argmax_first(el)
    m2, i2 = argmax_first(jnp.where(row == i1, neg, el))
    e2 = jnp.exp(m2 - m1)
    w1 = pg_top / (1.0 + e2)
    w2 = pg_top * e2 / (1.0 + e2)
    hot1 = row == i1
    hot2 = row == i2
    picked = jnp.where(jnp.logical_or(hot1, hot2), 1.0, 0.0)
    zero = jnp.zeros_like(picked)
    onehot = jnp.concatenate(
        [zero] + [jnp.where(gi == g, picked, 0.0) for g in range(N_GROUPS)] + [zero],
        axis=0).astype(BF16)
    before = _dot(onehot, tri_ref[...])
    lower = jnp.sum(_dot(low_ref[...], onehot), axis=1, keepdims=True)
    slot = group_block(before + lower, gi)
    p1 = jnp.sum(jnp.where(hot1, slot, 0.0), axis=0, keepdims=True)
    p2 = jnp.sum(jnp.where(hot2, slot, 0.0), axis=0, keepdims=True)
    s1_ref[0] = p1.astype(jnp.int32)
    s2_ref[0] = p2.astype(jnp.int32)
    w1_ref[0] = w1
    w2_ref[0] = w2
    cnt = jnp.sum(onehot.astype(F32), axis=1, keepdims=True)
    cnt_ref[0] = jnp.broadcast_to(cnt, cnt_ref.shape[1:])


def _token_specs(tile, d, n_prompt_tiles):
    last = n_prompt_tiles - 1
    return (pl.BlockSpec((tile, d), lambda i, *_: (jnp.minimum(i, last), 0)),
            pl.BlockSpec((tile, d), lambda i, *_: (jnp.maximum(i - n_prompt_tiles, 0), 0)))


def _router(yp, ys, wts, layer, tile):
    d = yp.shape[1]
    nlt = d // LANES
    n_prompt_tiles = yp.shape[0] // tile
    n_tiles = n_prompt_tiles + ys.shape[0] // tile
    tri = jnp.triu(jnp.ones((tile, tile), BF16), 1)
    low = jnp.tril(jnp.ones((ROUTE_ROWS, ROUTE_ROWS), BF16), -1)
    per_token = pl.BlockSpec((1, 1, tile), lambda i: (i, 0, 0))
    return pl.pallas_call(
        functools.partial(_router_kernel, n_prompt_tiles=n_prompt_tiles),
        grid=(n_tiles,),
        in_specs=list(_token_specs(tile, d, n_prompt_tiles)) + [
            _layer_spec(wts['norm2'], layer), _layer_spec(wts['w_route'], layer),
            _layer_spec(wts['b_route'], layer), _const_spec((tile, tile)),
            _const_spec((ROUTE_ROWS, ROUTE_ROWS))],
        out_specs=(pl.BlockSpec((tile, nlt, LANES), lambda i: (i, 0, 0)),
                   per_token, per_token, per_token, per_token,
                   pl.BlockSpec((1, ROUTE_ROWS, LANES), lambda i: (i, 0, 0))),
        out_shape=(jax.ShapeDtypeStruct((n_tiles * tile, nlt, LANES), BF16),
                   jax.ShapeDtypeStruct((n_tiles, 1, tile), jnp.int32),
                   jax.ShapeDtypeStruct((n_tiles, 1, tile), jnp.int32),
                   jax.ShapeDtypeStruct((n_tiles, 1, tile), F32),
                   jax.ShapeDtypeStruct((n_tiles, 1, tile), F32),
                   jax.ShapeDtypeStruct((n_tiles, ROUTE_ROWS, LANES), F32)),
        scratch_shapes=[pltpu.VMEM((tile * nlt, LANES), F32)],
        compiler_params=pltpu.CompilerParams(
            dimension_semantics=("arbitrary",), vmem_limit_bytes=VMEM_LIMIT),
        name="moe_router",
    )(yp, ys, wts['norm2'], wts['w_route'], wts['b_route'], tri, low)


def _run_copy(tile_idx, e, lo_ref, cnt_ref, dst_ref, local, remote, sem, to_remote):
    k = tile_idx * N_EXPERTS + e
    loc = local.at[pl.ds(lo_ref[k], cnt_ref[k])]
    rem = remote.at[pl.ds(dst_ref[k], cnt_ref[k])]
    return pltpu.make_async_copy(loc, rem, sem) if to_remote else pltpu.make_async_copy(rem, loc, sem)


def _start_runs(tile_idx, lo_ref, cnt_ref, dst_ref, local, remote, sem, to_remote):
    for e in range(N_EXPERTS):
        @pl.when(cnt_ref[tile_idx * N_EXPERTS + e] > 0)
        def _():
            _run_copy(tile_idx, e, lo_ref, cnt_ref, dst_ref, local, remote, sem, to_remote).start()


def _wait_runs(local, sem):
    pltpu.make_async_copy(local, local, sem).wait()


def _dispatch_kernel(lo_ref, cnt_ref, dst_ref, padst_ref, padn_ref,
                     s1_ref, s2_ref, x_ref, xs_ref, loc, zeros, sem, *, unroll):
    i = pl.program_id(0)
    last = pl.num_programs(0) - 1
    tile = x_ref.shape[0]
    slot = i % 2
    buf = loc.at[slot]

    @pl.when(i >= 2)
    def _():
        _wait_runs(buf, sem.at[slot])

    def body(tt, c):
        for u in range(unroll):
            t = tt * unroll + u
            row = x_ref[t]
            buf[s1_ref[t]] = row
            buf[s2_ref[t]] = row
        return c

    lax.fori_loop(0, tile // unroll, body, 0)
    _start_runs(i, lo_ref, cnt_ref, dst_ref, buf, xs_ref, sem.at[slot], True)

    @pl.when(i == last)
    def _():
        zeros[...] = jnp.zeros_like(zeros)

        def pad_copy(e):
            n = padn_ref[e]
            dst = xs_ref.at[pl.ds(padst_ref[e], n)]
            return pltpu.make_async_copy(zeros.at[pl.ds(0, n)], dst, sem.at[2])

        def each_pad(fn):
            def b(e, c):
                @pl.when(padn_ref[e] > 0)
                def _():
                    fn(e)
                return c
            lax.fori_loop(0, padn_ref.shape[0], b, 0)

        each_pad(lambda e: pad_copy(e).start())
        _wait_runs(buf, sem.at[slot])

        @pl.when(i >= 1)
        def _():
            _wait_runs(loc.at[1 - slot], sem.at[1 - slot])

        each_pad(lambda e: pad_copy(e).wait())


def _smem_tile(tile):
    return pl.BlockSpec((None, None, tile), lambda i, *_: (i, 0, 0), memory_space=pltpu.SMEM)


def _dispatch(xn_slab, s1, s2, tables, tile, n_rows):
    token = xn_slab.shape[1:]
    return pl.pallas_call(
        functools.partial(_dispatch_kernel, unroll=8),
        grid_spec=pltpu.PrefetchScalarGridSpec(
            num_scalar_prefetch=5,
            grid=(xn_slab.shape[0] // tile,),
            in_specs=[_smem_tile(tile), _smem_tile(tile),
                      pl.BlockSpec((tile,) + token, lambda i, *_: (i, 0, 0))],
            out_specs=pl.BlockSpec(memory_space=pl.ANY),
            scratch_shapes=[pltpu.VMEM((2, 2 * tile) + token, xn_slab.dtype),
                            pltpu.VMEM((MOE_BLOCK,) + token, xn_slab.dtype),
                            pltpu.SemaphoreType.DMA((3,))]),
        out_shape=jax.ShapeDtypeStruct((n_rows,) + token, xn_slab.dtype),
        compiler_params=pltpu.CompilerParams(
            dimension_semantics=("arbitrary",), vmem_limit_bytes=VMEM_LIMIT),
        name="moe_dispatch",
    )(*tables, s1, s2, xn_slab)


def _experts_kernel(blk_ref, nused_ref, xs_ref, wg_ref, wu_ref, wd_ref, ys_ref,
                    wg_s, wu_s, wd_s, slab_in, slab_out):
    b = pl.program_id(0)
    rows, nlt, _ = xs_ref.shape

    @pl.when(jnp.logical_or(b == 0, blk_ref[b] != blk_ref[jnp.maximum(b - 1, 0)]))
    def _():
        wg_s[...] = wg_ref[...].astype(BF16)
        wu_s[...] = wu_ref[...].astype(BF16)
        wd_s[...] = wd_ref[...].astype(BF16)

    @pl.when(b < nused_ref[0])
    def _():
        _tokens_to_slab(slab_in, xs_ref[...])
        x = _from_slab(slab_in, rows, nlt).astype(BF16)
        hg = _dot(x, wg_s[...])
        hu = _dot(x, wu_s[...])
        h = (hg * _sigmoid(hg) * hu).astype(BF16)
        _to_slab(slab_out, _dot(h, wd_s[...]))
        ys_ref[...] = _tokens_from_slab(slab_out, nlt)

    @pl.when(b >= nused_ref[0])
    def _():
        ys_ref[...] = jnp.zeros_like(ys_ref)


def _experts(xs, blk_expert, n_used, wts, layer):
    n_rows, nlt, _ = xs.shape
    d = nlt * LANES
    f = wts['w_gate'].shape[-1]
    rows = pl.BlockSpec((MOE_BLOCK, nlt, LANES), lambda b, blk, nu: (b, 0, 0))
    w_spec = lambda r, c: pl.BlockSpec((None, None, r, c), lambda b, blk, nu: (layer, blk[b], 0, 0))
    return pl.pallas_call(
        _experts_kernel,
        grid_spec=pltpu.PrefetchScalarGridSpec(
            num_scalar_prefetch=2,
            grid=(n_rows // MOE_BLOCK,),
            in_specs=[rows, w_spec(d, f), w_spec(d, f), w_spec(f, d)],
            out_specs=rows,
            scratch_shapes=[pltpu.VMEM((d, f), BF16), pltpu.VMEM((d, f), BF16),
                            pltpu.VMEM((f, d), BF16),
                            pltpu.VMEM((MOE_BLOCK * nlt, LANES), F32),
                            pltpu.VMEM((MOE_BLOCK * nlt, LANES), F32)]),
        out_shape=jax.ShapeDtypeStruct(xs.shape, xs.dtype),
        compiler_params=pltpu.CompilerParams(
            dimension_semantics=("arbitrary",), vmem_limit_bytes=VMEM_LIMIT),
        name="moe_experts",
    )(blk_expert, n_used, xs, wts['w_gate'], wts['w_up'], wts['w_down'])


def _combine_kernel(lo_ref, cnt_ref, dst_ref, s1_ref, s2_ref, w1_ref, w2_ref, yp_ref, ys_ref,
                    nf_ref, rows_ref, op_ref, os_ref, loc, acc, sem,
                    *, unroll, final_norm, n_prompt_tiles):
    i = pl.program_id(0)
    tile = yp_ref.shape[0]
    nlt = loc.shape[2]
    slot = i % 2
    buf = loc.at[slot]
    fetch = lambda t, s: _start_runs(t, lo_ref, cnt_ref, dst_ref, loc.at[s], rows_ref, sem.at[s],
                                     False)

    @pl.when(i == 0)
    def _():
        fetch(0, 0)

    @pl.when(i + 1 < pl.num_programs(0))
    def _():
        fetch(i + 1, 1 - slot)

    _wait_runs(buf, sem.at[slot])

    def body(tt, c):
        for u in range(unroll):
            t = tt * unroll + u
            r1 = buf[s1_ref[t]].astype(F32)
            r2 = buf[s2_ref[t]].astype(F32)
            acc[pl.ds(pl.multiple_of(t * nlt, nlt), nlt), :] = w1_ref[t] * r1 + w2_ref[t] * r2
        return c

    lax.fori_loop(0, tile // unroll, body, 0)
    def finish(y_ref, o_ref):
        out = y_ref[...] + _from_slab(acc, tile, nlt)
        if final_norm:
            out = _rms_norm(out, nf_ref[...])
        o_ref[...] = out

    is_prompt = i < n_prompt_tiles
    pl.when(is_prompt)(lambda: finish(yp_ref, op_ref))
    pl.when(jnp.logical_not(is_prompt))(lambda: finish(ys_ref, os_ref))


def _combine(yp, ys, rows, meta, tables, norm_f, final_norm, tile):
    d = yp.shape[1]
    token = rows.shape[1:]
    n_prompt_tiles = yp.shape[0] // tile
    n_tiles = n_prompt_tiles + ys.shape[0] // tile
    tok_p, tok_s = _token_specs(tile, d, n_prompt_tiles)
    return pl.pallas_call(
        functools.partial(_combine_kernel, unroll=8, final_norm=final_norm,
                          n_prompt_tiles=n_prompt_tiles),
        grid_spec=pltpu.PrefetchScalarGridSpec(
            num_scalar_prefetch=3,
            grid=(n_tiles,),
            in_specs=[_smem_tile(tile)] * 4 + [tok_p, tok_s, _const_spec((1, d)),
                                               pl.BlockSpec(memory_space=pl.ANY)],
            out_specs=(tok_p, tok_s),
            scratch_shapes=[pltpu.VMEM((2, 2 * tile) + token, rows.dtype),
                            pltpu.VMEM((tile * token[0], LANES), F32),
                            pltpu.SemaphoreType.DMA((2,))]),
        out_shape=(jax.ShapeDtypeStruct(yp.shape, F32), jax.ShapeDtypeStruct(ys.shape, F32)),
        compiler_params=pltpu.CompilerParams(
            dimension_semantics=("arbitrary",), vmem_limit_bytes=VMEM_LIMIT),
        name="moe_combine",
    )(*tables, *meta, yp, ys, norm_f, rows)


def _moe(yp, ys, wts, layer, norm_f, final_norm, tile):
    d = yp.shape[1]
    n = yp.shape[0] + ys.shape[0]
    assert yp.shape[0] % tile == 0 and ys.shape[0] % tile == 0 and d % LANES == 0
    xn_slab, s1, s2, w1, w2, cnt = _router(yp, ys, wts, layer, tile)
    cnt = cnt[:, SUBLANES:SUBLANES + N_EXPERTS, 0].astype(jnp.int32)
    lo = jnp.cumsum(cnt, axis=1) - cnt
    seg = jnp.sum(cnt, axis=0)
    seg_pad = (seg + MOE_BLOCK - 1) // MOE_BLOCK * MOE_BLOCK
    seg_end = jnp.cumsum(seg_pad)
    seg_start = seg_end - seg_pad
    dst = seg_start[None, :] + jnp.cumsum(cnt, axis=0) - cnt
    n_blocks = (2 * n + N_EXPERTS * (MOE_BLOCK - 1)) // MOE_BLOCK
    blk_ids = jnp.arange(n_blocks, dtype=jnp.int32)
    blk_expert = jnp.minimum(
        jnp.sum((seg_end[None, :] <= blk_ids[:, None] * MOE_BLOCK).astype(jnp.int32), axis=1),
        N_EXPERTS - 1)
    n_used = seg_end[-1:] // MOE_BLOCK
    pad_start = jnp.concatenate([seg_start + seg, blk_ids * MOE_BLOCK])
    pad_rows = jnp.concatenate([seg_pad - seg, jnp.where(blk_ids >= n_used[0], MOE_BLOCK, 0)])
    flat = lambda a: a.reshape(-1)
    runs = (flat(lo), flat(cnt), flat(dst))
    xs = _dispatch(xn_slab, s1, s2, runs + (pad_start, pad_rows), tile, n_blocks * MOE_BLOCK)
    rows = _experts(xs, blk_expert, n_used, wts, layer)
    return _combine(yp, ys, rows, (s1, s2, w1, w2), runs, norm_f, final_norm, tile)


def _prep_weights(norm1, w_in, ln_g, ln_b, w_s, b_s, conv_w, conv_b, lru_lambda, w_rg_a, b_rg_a,
                  w_rg_x, b_rg_x, w_out, norm2, w_route_group, b_route_group, w_route_expert,
                  b_route_expert, w_gate, w_up, w_down, n_sample_t):
    depth, d = w_in.shape[0], w_in.shape[1]
    dh = d // H_A
    row = lambda a: a.reshape(depth, 1, -1).astype(F32)
    causal = jnp.tril(jnp.ones((CHUNK, CHUNK), dtype=bool))
    ws = jnp.where(causal, 0.5 * w_s, 0.0)
    bs = jnp.repeat(jnp.swapaxes(0.5 * b_s, 1, 2), dh, axis=2)
    in_scale = jnp.concatenate([jnp.ones((3 * d,), F32), jnp.full((2 * d,), 0.5, F32)])

    def pair_bd(w):
        w = w.reshape(depth, H_B // 2, 2, w.shape[-2], w.shape[-1])
        z = jnp.zeros_like(w[:, :, 0])
        return jnp.concatenate([jnp.concatenate([w[:, :, 0], z], axis=-1),
                                jnp.concatenate([z, w[:, :, 1]], axis=-1)], axis=-2)

    gap = SUBLANES - N_GROUPS
    tail = ROUTE_ROWS - SUBLANES - N_EXPERTS
    w_route = jnp.concatenate(
        [jnp.swapaxes(w_route_group, 1, 2), jnp.zeros((depth, gap, d), F32),
         jnp.swapaxes(w_route_expert, 1, 2), jnp.zeros((depth, tail, d), F32)], axis=1)
    b_route = jnp.concatenate(
        [b_route_group, jnp.zeros((depth, gap), F32), b_route_expert,
         jnp.zeros((depth, tail), F32)], axis=1)[..., None]
    nt = n_sample_t
    return dict(
        norm1=row(norm1), w_in=(w_in * in_scale).astype(BF16), ln_g=row(ln_g), ln_b=row(ln_b),
        ws=ws.astype(BF16), bs=bs.astype(F32), conv_w=conv_w.astype(F32), conv_b=row(conv_b),
        lam=row(lru_lambda),
        wbd=(0.5 * jnp.concatenate([pair_bd(w_rg_a), pair_bd(w_rg_x)], axis=-1)).astype(BF16),
        b_a=row(0.5 * b_rg_a), b_x=row(0.5 * b_rg_x), w_out=(0.5 * w_out).astype(BF16),
        norm2=row(norm2), w_route=w_route.astype(BF16), b_route=b_route.astype(F32),
        w_gate=w_gate, w_up=w_up, w_down=w_down,
        wsc=jnp.repeat(jnp.transpose(ws[:, :, :nt, :nt], (0, 2, 3, 1)), dh, axis=3).astype(F32),
        bsc=bs[:, :nt].astype(F32),
    )


def kernel(x_prompt, x_sample, state_lru_h, state_conv, norm1, w_in, ln_g, ln_b, w_s, b_s, conv_w, conv_b, lru_lambda, w_rg_a, b_rg_a, w_rg_x, b_rg_x, w_out, norm2, w_route_group, b_route_group, w_route_expert, b_route_expert, w_gate, w_up, w_down, norm_f):
    depth = w_in.shape[0]
    nb, seq, d = x_prompt.shape
    ns, nt, _ = x_sample.shape
    assert nt <= CHUNK
    nf = norm_f.reshape(1, d).astype(F32)
    wts = _prep_weights(norm1, w_in, ln_g, ln_b, w_s, b_s, conv_w, conv_b, lru_lambda, w_rg_a,
                        b_rg_a, w_rg_x, b_rg_x, w_out, norm2, w_route_group, b_route_group,
                        w_route_expert, b_route_expert, w_gate, w_up, w_down, nt)
    conv_state = jnp.transpose(state_conv, (0, 2, 1, 3))
    assert (nb * seq) % MOE_TILE == 0
    sample_pad = (-nt * ns) % MOE_TILE
    assert sample_pad % ns == 0
    pad_t = sample_pad // ns
    yp = x_prompt
    ys = jnp.transpose(x_sample, (1, 0, 2))
    hp, cp, vp, hs, cs, vs = [], [], [], [], [], []
    for l in range(depth):
        yp, h, c, v = _mixer_prompt(yp, wts, l)
        hp.append(h)
        cp.append(c)
        vp.append(v)
        ys, h, c, v = _mixer_sample(ys, state_lru_h, conv_state, wts, l, nt, pad_t)
        hs.append(h)
        cs.append(c)
        vs.append(v)
        yp, ys = _moe(yp.reshape(nb * seq, d), ys.reshape((nt + pad_t) * ns, d), wts, l, nf,
                      l == depth - 1, MOE_TILE)
        yp = yp.reshape(nb, seq, d)
        ys = ys.reshape(nt + pad_t, ns, d)
    to_seq_major = lambda a: jnp.transpose(jnp.stack(a), (0, 2, 1, 3))
    return (yp, jnp.transpose(ys[:nt], (1, 0, 2)), jnp.stack(hp), jnp.stack(cp), jnp.stack(vp),
            jnp.stack(hs), to_seq_major(cs), to_seq_major(vs))
```

```python
import functools

import jax
import jax.numpy as jnp
from jax import lax
from jax.experimental import pallas as pl
from jax.experimental.pallas import tpu as pltpu

F32 = jnp.float32
BF16 = jnp.bfloat16

CHUNK = 128
H_A = 4
H_B = 8
CONV_W = 4
LRU_C = 8.0
N_GROUPS = 4
E_PER_GROUP = 8
N_EXPERTS = N_GROUPS * E_PER_GROUP
EPS = 1e-6

LANES = 128
SUBLANES = 8
SCAN_UNROLL = 4
SEQS_PER_TRIP = 8
ROUTE_ROWS = 48
MOE_TILE = 1024
MOE_BLOCK = 512
VMEM_LIMIT = 60 * 1024 * 1024


def _dot(a, b):
    return jnp.dot(a, b, preferred_element_type=F32)


def _sigmoid(x):
    return 0.5 * (jnp.tanh(0.5 * x) + 1.0)


def _gelu2(x):
    c = 0.7978845608028654
    return x * (1.0 + jnp.tanh(x * (c + (c * 0.044715) * (x * x))))


def _rms_norm(x, g):
    return x * lax.rsqrt(jnp.mean(x * x, axis=-1, keepdims=True) + EPS) * g


def _layer_norm(x, g, b, eps):
    xc = x - jnp.mean(x, axis=-1, keepdims=True)
    var = jnp.mean(xc * xc, axis=-1, keepdims=True)
    return xc * lax.rsqrt(var + eps) * g + b


def _softplus(x):
    return jnp.maximum(x, 0.0) + jnp.log(1.0 + jnp.exp(-jnp.abs(x)))


def _lru_gates(xc, wbd_ref, b_a, b_x, lam_half):
    d = xc.shape[-1]
    xcb = xc.astype(BF16)
    r_parts, i_parts = [], []
    for q in range(d // 256):
        ri = _dot(xcb[:, q * 256:(q + 1) * 256], wbd_ref[q])
        r_parts.append(ri[:, :256])
        i_parts.append(ri[:, 256:])
    r2 = 1.0 + jnp.tanh(jnp.concatenate(r_parts, axis=-1) + b_a)
    i2 = 1.0 + jnp.tanh(jnp.concatenate(i_parts, axis=-1) + b_x)
    a = jnp.exp(lam_half * r2)
    mult_half = jnp.sqrt(0.25 - 0.25 * (a * a))
    return a, mult_half, i2 * xc


def _shift_rows(x, hist, j):
    sh = pltpu.roll(x, j, axis=0)
    top = jnp.where(lax.broadcasted_iota(jnp.int32, (SUBLANES, 1), 0) < j,
                    pltpu.roll(hist, j, axis=0), sh[:SUBLANES])
    return jnp.concatenate([top, sh[SUBLANES:]], axis=0)


def _mixer_prompt_kernel(x_ref, norm_ref, win_ref, lng_ref, lnb_ref, ws_ref, bs_ref,
                         cw_ref, cb_ref, lam_ref, wbd_ref, ba_ref, bx_ref, wout_ref,
                         y_ref, h_ref, conv_ref, v_ref,
                         hist_s, a_s, b_s, ya_s, gb_s, hc_s):
    i = pl.program_id(0)
    last = pl.num_programs(0) - 1
    nseq, _, d = x_ref.shape
    nlt = d // LANES
    spt = SEQS_PER_TRIP
    pair = spt * CHUNK

    @pl.when(i == 0)
    def _():
        hist_s[...] = jnp.zeros_like(hist_s)
        hc_s[...] = jnp.zeros_like(hc_s)

    lam_half = (-0.5 * LRU_C) * _softplus(-lam_ref[...])
    is_first = i == 0

    def branch_body(p, c):
        row = pl.multiple_of(p * pair, pair)
        x = jnp.concatenate([x_ref[spt * p + k] for k in range(spt)], axis=0)
        xnb = _rms_norm(x, norm_ref[...]).astype(BF16)
        v = _layer_norm(_gelu2(_dot(xnb, win_ref[:, d:2 * d])), lng_ref[...], lnb_ref[...],
                        4.0 * EPS)

        @pl.when(i == last)
        def _():
            for k in range(spt):
                v_ref[spt * p + k] = v[k * CHUNK:(k + 1) * CHUNK]

        vb = v.astype(BF16)
        dh = d // H_A
        z_rows = []
        for k in range(spt):
            zs = [_dot(ws_ref[h], vb[k * CHUNK:(k + 1) * CHUNK, h * dh:(h + 1) * dh])
                  for h in range(H_A)]
            z_rows.append(jnp.concatenate(zs, axis=-1) + bs_ref[...])
        z = jnp.concatenate(z_rows, axis=0)
        ya = _gelu2(_dot(xnb, win_ref[:, 0:d])) * z
        ga2 = 1.0 + jnp.tanh(_dot(xnb, win_ref[:, 3 * d:4 * d]))
        ya_s[pl.ds(row, pair), :] = (ga2 * ya).astype(BF16)
        gb_s[pl.ds(row, pair), :] = (1.0 + jnp.tanh(_dot(xnb, win_ref[:, 4 * d:5 * d]))).astype(BF16)
        xr = _dot(xnb, win_ref[:, 2 * d:3 * d])
        xcs = []
        for k in range(spt):
            s = spt * p + k
            xk = xr[k * CHUNK:(k + 1) * CHUNK]
            hist = hist_s[s]
            xc = cb_ref[...] + xk * cw_ref[CONV_W - 1:CONV_W, :]
            for j in range(1, CONV_W):
                xc = xc + _shift_rows(xk, hist, j) * cw_ref[CONV_W - 1 - j:CONV_W - j, :]
            hist_s[s] = xk[CHUNK - SUBLANES:]
            xcs.append(xc)
        xc = jnp.concatenate(xcs, axis=0)
        a, mult_half, ix2 = _lru_gates(xc, wbd_ref, ba_ref[...], bx_ref[...], lam_half)
        bt = mult_half * ix2
        for k in range(spt):
            s = spt * p + k
            for j in range(nlt):
                a_s[j, pl.ds(s, CHUNK, stride=nseq), :] = a[k * CHUNK:(k + 1) * CHUNK, j * LANES:(j + 1) * LANES]
                b_s[j, pl.ds(s, CHUNK, stride=nseq), :] = bt[k * CHUNK:(k + 1) * CHUNK, j * LANES:(j + 1) * LANES]

        @pl.when(is_first)
        def _():
            for k in range(spt):
                first = 0.5 * ix2[k * CHUNK:k * CHUNK + 1]
                for j in range(nlt):
                    b_s[j, pl.ds(spt * p + k, 1), :] = first[:, j * LANES:(j + 1) * LANES]

        return c

    lax.fori_loop(0, nseq // spt, branch_body, 0)

    def scan_body(tt, hs):
        for u in range(SCAN_UNROLL):
            row = pl.multiple_of((tt * SCAN_UNROLL + u) * nseq, nseq)
            out = []
            for j in range(nlt):
                h = a_s[j, pl.ds(row, nseq), :] * hs[j] + b_s[j, pl.ds(row, nseq), :]
                b_s[j, pl.ds(row, nseq), :] = h
                out.append(h)
            hs = tuple(out)
        return hs

    h0 = tuple(hc_s[:, j * LANES:(j + 1) * LANES] for j in range(nlt))
    hs = lax.fori_loop(0, CHUNK // SCAN_UNROLL, scan_body, h0)
    hfin = jnp.concatenate(hs, axis=-1)
    hc_s[...] = hfin
    h_ref[...] = hfin

    def out_body(p, c):
        row = pl.multiple_of(p * pair, pair)
        hrows = []
        for k in range(spt):
            hrows.append(jnp.concatenate(
                [b_s[j, pl.ds(spt * p + k, CHUNK, stride=nseq), :] for j in range(nlt)], axis=-1))
        h = jnp.concatenate(hrows, axis=0)
        merged = (ya_s[pl.ds(row, pair), :].astype(F32)
                  + gb_s[pl.ds(row, pair), :].astype(F32) * h).astype(BF16)
        o = _dot(merged, wout_ref[...])
        for k in range(spt):
            y_ref[spt * p + k] = x_ref[spt * p + k] + o[k * CHUNK:(k + 1) * CHUNK]
        return c

    lax.fori_loop(0, nseq // spt, out_body, 0)

    @pl.when(i == last)
    def _():
        conv_ref[...] = hist_s[:, SUBLANES - (CONV_W - 1):, :]


def _const_spec(shape):
    nd = len(shape)
    return pl.BlockSpec(shape, lambda i, *_, _n=nd: (0,) * _n, pipeline_mode=pl.Buffered(1))


def _layer_spec(stacked, layer):
    shape = stacked.shape[1:]
    nd = len(shape)
    return pl.BlockSpec((None,) + tuple(shape), lambda i, *_, _n=nd: (layer,) + (0,) * _n,
                        pipeline_mode=pl.Buffered(1))


MIXER_WEIGHTS = ('norm1', 'w_in', 'ln_g', 'ln_b', 'ws', 'bs', 'conv_w', 'conv_b', 'lam', 'wbd',
                 'b_a', 'b_x', 'w_out')


def _mixer_prompt(x, wts, layer):
    nseq, seq, d = x.shape
    assert seq % CHUNK == 0 and nseq == SUBLANES and d % 256 == 0
    n_chunks = seq // CHUNK
    nlt = d // LANES
    weights = tuple(wts[k] for k in MIXER_WEIGHTS)
    x_spec = pl.BlockSpec((nseq, CHUNK, d), lambda i: (0, i, 0))
    out_shape = (
        jax.ShapeDtypeStruct((nseq, seq, d), F32),
        jax.ShapeDtypeStruct((nseq, d), F32),
        jax.ShapeDtypeStruct((nseq, CONV_W - 1, d), F32),
        jax.ShapeDtypeStruct((nseq, CHUNK, d), F32),
    )
    out_specs = (
        x_spec,
        _const_spec((nseq, d)),
        _const_spec((nseq, CONV_W - 1, d)),
        _const_spec((nseq, CHUNK, d)),
    )
    scratch = [
        pltpu.VMEM((nseq, SUBLANES, d), F32),
        pltpu.VMEM((nlt, CHUNK * nseq, LANES), F32),
        pltpu.VMEM((nlt, CHUNK * nseq, LANES), F32),
        pltpu.VMEM((nseq * CHUNK, d), BF16),
        pltpu.VMEM((nseq * CHUNK, d), BF16),
        pltpu.VMEM((nseq, d), F32),
    ]
    return pl.pallas_call(
        _mixer_prompt_kernel,
        grid=(n_chunks,),
        in_specs=[x_spec] + [_layer_spec(w, layer) for w in weights],
        out_specs=out_specs,
        out_shape=out_shape,
        scratch_shapes=scratch,
        compiler_params=pltpu.CompilerParams(
            dimension_semantics=("arbitrary",), vmem_limit_bytes=VMEM_LIMIT),
        name="mixer_prompt",
    )(x, *weights)


def _mixer_sample_kernel(x_ref, h0_ref, cst_ref, norm_ref, win_ref, lng_ref, lnb_ref, wsc_ref,
                         bsc_ref, cw_ref, cb_ref, lam_ref, wbd_ref, ba_ref, bx_ref, wout_ref,
                         y_ref, h_ref, conv_ref, v_ref):
    nt, ns, d = x_ref.shape
    at = lambda a, t: a[t * ns:(t + 1) * ns]
    lam_half = (-0.5 * LRU_C) * _softplus(-lam_ref[...])
    x = x_ref[...].reshape(nt * ns, d)
    xnb = _rms_norm(x, norm_ref[...]).astype(BF16)
    v = _layer_norm(_gelu2(_dot(xnb, win_ref[:, d:2 * d])), lng_ref[...], lnb_ref[...], 4.0 * EPS)
    v_ref[...] = v.reshape(nt, ns, d)
    zs = []
    for t in range(nt):
        z = bsc_ref[t:t + 1, :]
        for s in range(t + 1):
            z = z + wsc_ref[t, s:s + 1, :] * at(v, s)
        zs.append(z)
    ya = _gelu2(_dot(xnb, win_ref[:, 0:d])) * jnp.concatenate(zs, axis=0)
    ga2 = 1.0 + jnp.tanh(_dot(xnb, win_ref[:, 3 * d:4 * d]))
    gb2 = 1.0 + jnp.tanh(_dot(xnb, win_ref[:, 4 * d:5 * d]))
    xr = _dot(xnb, win_ref[:, 2 * d:3 * d])
    xp = [cst_ref[k] for k in range(CONV_W - 1)] + [at(xr, t) for t in range(nt)]
    xcs = []
    for t in range(nt):
        xc = cb_ref[...]
        for k in range(CONV_W):
            xc = xc + xp[t + k] * cw_ref[k:k + 1, :]
        xcs.append(xc)
    a, mult_half, ix2 = _lru_gates(jnp.concatenate(xcs, axis=0), wbd_ref, ba_ref[...],
                                   bx_ref[...], lam_half)
    bt = mult_half * ix2
    h = h0_ref[...]
    hs = []
    for t in range(nt):
        h = at(a, t) * h + at(bt, t)
        hs.append(h)
    merged = (ga2 * ya + gb2 * jnp.concatenate(hs, axis=0)).astype(BF16)
    y_ref[0:nt] = (x + _dot(merged, wout_ref[...])).reshape(nt, ns, d)
    if y_ref.shape[0] > nt:
        y_ref[nt:] = jnp.zeros((y_ref.shape[0] - nt, ns, d), F32)
    h_ref[...] = h
    for k in range(CONV_W - 1):
        conv_ref[k] = xp[nt + k]


def _mixer_sample(x, h0, conv_state, wts, layer, nt, pad_t):
    _, ns, d = x.shape
    names = MIXER_WEIGHTS[:4] + ('wsc', 'bsc') + MIXER_WEIGHTS[6:]
    stacked = (h0, conv_state) + tuple(wts[k] for k in names)
    out_shape = (
        jax.ShapeDtypeStruct((nt + pad_t, ns, d), F32),
        jax.ShapeDtypeStruct((ns, d), F32),
        jax.ShapeDtypeStruct((CONV_W - 1, ns, d), F32),
        jax.ShapeDtypeStruct((nt, ns, d), F32),
    )
    return pl.pallas_call(
        _mixer_sample_kernel,
        grid=(1,),
        in_specs=[_const_spec((nt, ns, d))] + [_layer_spec(w, layer) for w in stacked],
        out_specs=tuple(_const_spec(s.shape) for s in out_shape),
        out_shape=out_shape,
        compiler_params=pltpu.CompilerParams(
            dimension_semantics=("arbitrary",), vmem_limit_bytes=VMEM_LIMIT),
        name="mixer_sample",
    )(x, *stacked)


def _to_slab(ref, val):
    t, d = val.shape
    nlt = d // LANES
    for j in range(nlt):
        ref[pl.ds(j, t, stride=nlt), :] = val[:, j * LANES:(j + 1) * LANES]


def _from_slab(ref, t, nlt):
    return jnp.concatenate([ref[pl.ds(j, t, stride=nlt), :] for j in range(nlt)], axis=-1)


def _tokens_from_slab(slab_ref, nlt):
    return slab_ref[...].reshape(slab_ref.shape[0] // nlt, nlt, LANES).astype(BF16)


def _tokens_to_slab(slab_ref, tokens):
    slab_ref[...] = tokens.astype(F32).reshape(slab_ref.shape)


def _router_kernel(yp_ref, ys_ref, norm_ref, wr_ref, br_ref, tri_ref, low_ref, xn_ref, s1_ref,
                   s2_ref, w1_ref, w2_ref, cnt_ref, slab_s, *, n_prompt_tiles):
    y = jnp.where(pl.program_id(0) < n_prompt_tiles, yp_ref[...], ys_ref[...])
    xn = _rms_norm(y, norm_ref[...])
    _to_slab(slab_s, xn)
    xn_ref[...] = _tokens_from_slab(slab_s, xn_ref.shape[1])
    lt = lax.dot_general(wr_ref[...], xn.astype(BF16), (((1,), (1,)), ((), ())),
                         preferred_element_type=F32) + br_ref[...]
    tile = lt.shape[1]
    row = lax.broadcasted_iota(jnp.int32, (SUBLANES, tile), 0).astype(F32)
    neg = jnp.float32(-jnp.inf)
    big = jnp.float32(SUBLANES)

    def argmax_first(vals):
        m = jnp.max(vals, axis=0, keepdims=True)
        return m, jnp.min(jnp.where(vals == m, row, big), axis=0, keepdims=True)

    def group_block(x, gi):
        out = x[SUBLANES:2 * SUBLANES]
        for g in range(1, N_GROUPS):
            out = jnp.where(gi == g, x[(g + 1) * SUBLANES:(g + 2) * SUBLANES], out)
        return out

    gl = jnp.where(row < N_GROUPS, lt[0:SUBLANES], neg)
    gm, gi = argmax_first(gl)
    pg_top = 1.0 / jnp.sum(jnp.exp(gl - gm), axis=0, keepdims=True)
    el = group_block(lt, gi)
    m1, i1 = argmax_first(el)
    m2, i2 = argmax_first(jnp.where(row == i1, neg, el))
    e2 = jnp.exp(m2 - m1)
    w1 = pg_top / (1.0 + e2)
    w2 = pg_top * e2 / (1.0 + e2)
    hot1 = row == i1
    hot2 = row == i2
    picked = jnp.where(jnp.logical_or(hot1, hot2), 1.0, 0.0)
    zero = jnp.zeros_like(picked)
    onehot = jnp.concatenate(
        [zero] + [jnp.where(gi == g, picked, 0.0) for g in range(N_GROUPS)] + [zero],
        axis=0).astype(BF16)
    before = _dot(onehot, tri_ref[...])
    lower = jnp.sum(_dot(low_ref[...], onehot), axis=1, keepdims=True)
    slot = group_block(before + lower, gi)
    p1 = jnp.sum(jnp.where(hot1, slot, 0.0), axis=0, keepdims=True)
    p2 = jnp.sum(jnp.where(hot2, slot, 0.0), axis=0, keepdims=True)
    s1_ref[0] = p1.astype(jnp.int32)
    s2_ref[0] = p2.astype(jnp.int32)
    w1_ref[0] = w1
    w2_ref[0] = w2
    cnt = jnp.sum(onehot.astype(F32), axis=1, keepdims=True)
    cnt_ref[0] = jnp.broadcast_to(cnt, cnt_ref.shape[1:])


def _token_specs(tile, d, n_prompt_tiles):
    last = n_prompt_tiles - 1
    return (pl.BlockSpec((tile, d), lambda i, *_: (jnp.minimum(i, last), 0)),
            pl.BlockSpec((tile, d), lambda i, *_: (jnp.maximum(i - n_prompt_tiles, 0), 0)))


def _router(yp, ys, wts, layer, tile):
    d = yp.shape[1]
    nlt = d // LANES
    n_prompt_tiles = yp.shape[0] // tile
    n_tiles = n_prompt_tiles + ys.shape[0] // tile
    tri = jnp.triu(jnp.ones((tile, tile), BF16), 1)
    low = jnp.tril(jnp.ones((ROUTE_ROWS, ROUTE_ROWS), BF16), -1)
    per_token = pl.BlockSpec((1, 1, tile), lambda i: (i, 0, 0))
    return pl.pallas_call(
        functools.partial(_router_kernel, n_prompt_tiles=n_prompt_tiles),
        grid=(n_tiles,),
        in_specs=list(_token_specs(tile, d, n_prompt_tiles)) + [
            _layer_spec(wts['norm2'], layer), _layer_spec(wts['w_route'], layer),
            _layer_spec(wts['b_route'], layer), _const_spec((tile, tile)),
            _const_spec((ROUTE_ROWS, ROUTE_ROWS))],
        out_specs=(pl.BlockSpec((tile, nlt, LANES), lambda i: (i, 0, 0)),
                   per_token, per_token, per_token, per_token,
                   pl.BlockSpec((1, ROUTE_ROWS, LANES), lambda i: (i, 0, 0))),
        out_shape=(jax.ShapeDtypeStruct((n_tiles * tile, nlt, LANES), BF16),
                   jax.ShapeDtypeStruct((n_tiles, 1, tile), jnp.int32),
                   jax.ShapeDtypeStruct((n_tiles, 1, tile), jnp.int32),
                   jax.ShapeDtypeStruct((n_tiles, 1, tile), F32),
                   jax.ShapeDtypeStruct((n_tiles, 1, tile), F32),
                   jax.ShapeDtypeStruct((n_tiles, ROUTE_ROWS, LANES), F32)),
        scratch_shapes=[pltpu.VMEM((tile * nlt, LANES), F32)],
        compiler_params=pltpu.CompilerParams(
            dimension_semantics=("arbitrary",), vmem_limit_bytes=VMEM_LIMIT),
        name="moe_router",
    )(yp, ys, wts['norm2'], wts['w_route'], wts['b_route'], tri, low)


def _run_copy(tile_idx, e, lo_ref, cnt_ref, dst_ref, local, remote, sem, to_remote):
    k = tile_idx * N_EXPERTS + e
    loc = local.at[pl.ds(lo_ref[k], cnt_ref[k])]
    rem = remote.at[pl.ds(dst_ref[k], cnt_ref[k])]
    return pltpu.make_async_copy(loc, rem, sem) if to_remote else pltpu.make_async_copy(rem, loc, sem)


def _start_runs(tile_idx, lo_ref, cnt_ref, dst_ref, local, remote, sem, to_remote):
    for e in range(N_EXPERTS):
        @pl.when(cnt_ref[tile_idx * N_EXPERTS + e] > 0)
        def _():
            _run_copy(tile_idx, e, lo_ref, cnt_ref, dst_ref, local, remote, sem, to_remote).start()


def _wait_runs(local, sem):
    pltpu.make_async_copy(local, local, sem).wait()


def _dispatch_kernel(lo_ref, cnt_ref, dst_ref, padst_ref, padn_ref,
                     s1_ref, s2_ref, x_ref, xs_ref, loc, zeros, sem, *, unroll):
    i = pl.program_id(0)
    last = pl.num_programs(0) - 1
    tile = x_ref.shape[0]
    slot = i % 2
    buf = loc.at[slot]

    @pl.when(i >= 2)
    def _():
        _wait_runs(buf, sem.at[slot])

    def body(tt, c):
        for u in range(unroll):
            t = tt * unroll + u
            row = x_ref[t]
            buf[s1_ref[t]] = row
            buf[s2_ref[t]] = row
        return c

    lax.fori_loop(0, tile // unroll, body, 0)
    _start_runs(i, lo_ref, cnt_ref, dst_ref, buf, xs_ref, sem.at[slot], True)

    @pl.when(i == last)
    def _():
        zeros[...] = jnp.zeros_like(zeros)

        def pad_copy(e):
            n = padn_ref[e]
            dst = xs_ref.at[pl.ds(padst_ref[e], n)]
            return pltpu.make_async_copy(zeros.at[pl.ds(0, n)], dst, sem.at[2])

        def each_pad(fn):
            def b(e, c):
                @pl.when(padn_ref[e] > 0)
                def _():
                    fn(e)
                return c
            lax.fori_loop(0, padn_ref.shape[0], b, 0)

        each_pad(lambda e: pad_copy(e).start())
        _wait_runs(buf, sem.at[slot])

        @pl.when(i >= 1)
        def _():
            _wait_runs(loc.at[1 - slot], sem.at[1 - slot])

        each_pad(lambda e: pad_copy(e).wait())


def _smem_tile(tile):
    return pl.BlockSpec((None, None, tile), lambda i, *_: (i, 0, 0), memory_space=pltpu.SMEM)


def _dispatch(xn_slab, s1, s2, tables, tile, n_rows):
    token = xn_slab.shape[1:]
    return pl.pallas_call(
        functools.partial(_dispatch_kernel, unroll=8),
        grid_spec=pltpu.PrefetchScalarGridSpec(
            num_scalar_prefetch=5,
            grid=(xn_slab.shape[0] // tile,),
            in_specs=[_smem_tile(tile), _smem_tile(tile),
                      pl.BlockSpec((tile,) + token, lambda i, *_: (i, 0, 0))],
            out_specs=pl.BlockSpec(memory_space=pl.ANY),
            scratch_shapes=[pltpu.VMEM((2, 2 * tile) + token, xn_slab.dtype),
                            pltpu.VMEM((MOE_BLOCK,) + token, xn_slab.dtype),
                            pltpu.SemaphoreType.DMA((3,))]),
        out_shape=jax.ShapeDtypeStruct((n_rows,) + token, xn_slab.dtype),
        compiler_params=pltpu.CompilerParams(
            dimension_semantics=("arbitrary",), vmem_limit_bytes=VMEM_LIMIT),
        name="moe_dispatch",
    )(*tables, s1, s2, xn_slab)


def _experts_kernel(blk_ref, next_ref, nused_ref, xs_ref, wg_ref, wu_ref, wd_ref, ys_ref,
                    xin, yout, wg_f, wu_f, wd_f, wg_s, wu_s, wd_s, slab_in, slab_out,
                    sem_in, sem_out, sem_w, *, layer):
    n_blocks = ys_ref.shape[0] // MOE_BLOCK
    nlt = xs_ref.shape[1]
    n_used = nused_ref[0]
    block = lambda ref, b: ref.at[pl.ds(pl.multiple_of(b * MOE_BLOCK, MOE_BLOCK), MOE_BLOCK)]
    rows_in = lambda b, slot: pltpu.make_async_copy(block(xs_ref, b), xin.at[slot], sem_in.at[slot])
    rows_out = lambda b, slot: pltpu.make_async_copy(yout.at[slot], block(ys_ref, b),
                                                     sem_out.at[slot])

    def weights(e, slot):
        pairs = ((wg_ref, wg_f), (wu_ref, wu_f), (wd_ref, wd_f))
        return [pltpu.make_async_copy(src.at[layer, e], dst.at[slot], sem_w.at[slot])
                for src, dst in pairs]

    @pl.when(n_used > 0)
    def _():
        rows_in(0, 0).start()
        for c in weights(blk_ref[0], 0):
            c.start()

    def step(b, wslot):
        slot = b % 2
        e = blk_ref[b]
        rows_in(b, slot).wait()

        @pl.when(b + 1 < n_used)
        def _():
            rows_in(b + 1, 1 - slot).start()

        new_expert = jnp.logical_or(b == 0, e != blk_ref[jnp.maximum(b - 1, 0)])
        wslot = jnp.where(new_expert, 1 - wslot, wslot)

        @pl.when(new_expert)
        def _():
            for c in weights(e, wslot):
                c.wait()
            wg_s[...] = wg_f[wslot].astype(BF16)
            wu_s[...] = wu_f[wslot].astype(BF16)
            wd_s[...] = wd_f[wslot].astype(BF16)

            @pl.when(next_ref[e] >= 0)
            def _():
                for c in weights(next_ref[e], 1 - wslot):
                    c.start()

        _tokens_to_slab(slab_in, xin[slot])
        x = _from_slab(slab_in, MOE_BLOCK, nlt).astype(BF16)
        hg = _dot(x, wg_s[...])
        hu = _dot(x, wu_s[...])
        h = (hg * _sigmoid(hg) * hu).astype(BF16)
        _to_slab(slab_out, _dot(h, wd_s[...]))

        @pl.when(b >= 2)
        def _():
            rows_out(b - 2, slot).wait()

        yout[slot] = _tokens_from_slab(slab_out, nlt)
        rows_out(b, slot).start()
        return wslot

    lax.fori_loop(0, n_used, step, jnp.int32(1))

    @pl.when(n_used >= 2)
    def _():
        rows_out(n_used - 2, n_used % 2).wait()

    @pl.when(n_used >= 1)
    def _():
        rows_out(n_used - 1, (n_used - 1) % 2).wait()

    yout[0] = jnp.zeros(yout.shape[1:], yout.dtype)

    def fill(b, c):
        rows_out(b, 0).start()
        rows_out(b, 0).wait()
        return c

    lax.fori_loop(n_used, n_blocks, fill, 0)


def _experts(xs, blk_expert, next_expert, n_used, wts, layer):
    n_rows, nlt, _ = xs.shape
    d = nlt * LANES
    f = wts['w_gate'].shape[-1]
    token = (nlt, LANES)
    any_spec = pl.BlockSpec(memory_space=pl.ANY)
    return pl.pallas_call(
        functools.partial(_experts_kernel, layer=layer),
        grid_spec=pltpu.PrefetchScalarGridSpec(
            num_scalar_prefetch=3,
            grid=(1,),
            in_specs=[any_spec] * 4,
            out_specs=any_spec,
            scratch_shapes=[pltpu.VMEM((2, MOE_BLOCK) + token, xs.dtype),
                            pltpu.VMEM((2, MOE_BLOCK) + token, xs.dtype),
                            pltpu.VMEM((2, d, f), F32), pltpu.VMEM((2, d, f), F32),
                            pltpu.VMEM((2, f, d), F32),
                            pltpu.VMEM((d, f), BF16), pltpu.VMEM((d, f), BF16),
                            pltpu.VMEM((f, d), BF16),
                            pltpu.VMEM((MOE_BLOCK * nlt, LANES), F32),
                            pltpu.VMEM((MOE_BLOCK * nlt, LANES), F32),
                            pltpu.SemaphoreType.DMA((2,)), pltpu.SemaphoreType.DMA((2,)),
                            pltpu.SemaphoreType.DMA((2,))]),
        out_shape=jax.ShapeDtypeStruct(xs.shape, xs.dtype),
        compiler_params=pltpu.CompilerParams(
            dimension_semantics=("arbitrary",), vmem_limit_bytes=VMEM_LIMIT),
        name="moe_experts",
    )(blk_expert, next_expert, n_used, xs, wts['w_gate'], wts['w_up'], wts['w_down'])


def _combine_kernel(lo_ref, cnt_ref, dst_ref, s1_ref, s2_ref, w1_ref, w2_ref, yp_ref, ys_ref,
                    nf_ref, rows_ref, op_ref, os_ref, loc, acc, sem,
                    *, unroll, final_norm, n_prompt_tiles):
    i = pl.program_id(0)
    tile = yp_ref.shape[0]
    nlt = loc.shape[2]
    slot = i % 2
    buf = loc.at[slot]
    fetch = lambda t, s: _start_runs(t, lo_ref, cnt_ref, dst_ref, loc.at[s], rows_ref, sem.at[s],
                                     False)

    @pl.when(i == 0)
    def _():
        fetch(0, 0)

    @pl.when(i + 1 < pl.num_programs(0))
    def _():
        fetch(i + 1, 1 - slot)

    _wait_runs(buf, sem.at[slot])

    def body(tt, c):
        for u in range(unroll):
            t = tt * unroll + u
            r1 = buf[s1_ref[t]].astype(F32)
            r2 = buf[s2_ref[t]].astype(F32)
            acc[pl.ds(pl.multiple_of(t * nlt, nlt), nlt), :] = w1_ref[t] * r1 + w2_ref[t] * r2
        return c

    lax.fori_loop(0, tile // unroll, body, 0)
    def finish(y_ref, o_ref):
        out = y_ref[...] + _from_slab(acc, tile, nlt)
        if final_norm:
            out = _rms_norm(out, nf_ref[...])
        o_ref[...] = out

    is_prompt = i < n_prompt_tiles
    pl.when(is_prompt)(lambda: finish(yp_ref, op_ref))
    pl.when(jnp.logical_not(is_prompt))(lambda: finish(ys_ref, os_ref))


def _combine(yp, ys, rows, meta, tables, norm_f, final_norm, tile):
    d = yp.shape[1]
    token = rows.shape[1:]
    n_prompt_tiles = yp.shape[0] // tile
    n_tiles = n_prompt_tiles + ys.shape[0] // tile
    tok_p, tok_s = _token_specs(tile, d, n_prompt_tiles)
    return pl.pallas_call(
        functools.partial(_combine_kernel, unroll=8, final_norm=final_norm,
                          n_prompt_tiles=n_prompt_tiles),
        grid_spec=pltpu.PrefetchScalarGridSpec(
            num_scalar_prefetch=3,
            grid=(n_tiles,),
            in_specs=[_smem_tile(tile)] * 4 + [tok_p, tok_s, _const_spec((1, d)),
                                               pl.BlockSpec(memory_space=pl.ANY)],
            out_specs=(tok_p, tok_s),
            scratch_shapes=[pltpu.VMEM((2, 2 * tile) + token, rows.dtype),
                            pltpu.VMEM((tile * token[0], LANES), F32),
                            pltpu.SemaphoreType.DMA((2,))]),
        out_shape=(jax.ShapeDtypeStruct(yp.shape, F32), jax.ShapeDtypeStruct(ys.shape, F32)),
        compiler_params=pltpu.CompilerParams(
            dimension_semantics=("arbitrary",), vmem_limit_bytes=VMEM_LIMIT),
        name="moe_combine",
    )(*tables, *meta, yp, ys, norm_f, rows)


def _moe(yp, ys, wts, layer, norm_f, final_norm, tile):
    d = yp.shape[1]
    n = yp.shape[0] + ys.shape[0]
    assert yp.shape[0] % tile == 0 and ys.shape[0] % tile == 0 and d % LANES == 0
    xn_slab, s1, s2, w1, w2, cnt = _router(yp, ys, wts, layer, tile)
    cnt = cnt[:, SUBLANES:SUBLANES + N_EXPERTS, 0].astype(jnp.int32)
    lo = jnp.cumsum(cnt, axis=1) - cnt
    seg = jnp.sum(cnt, axis=0)
    seg_pad = (seg + MOE_BLOCK - 1) // MOE_BLOCK * MOE_BLOCK
    seg_end = jnp.cumsum(seg_pad)
    seg_start = seg_end - seg_pad
    dst = seg_start[None, :] + jnp.cumsum(cnt, axis=0) - cnt
    n_blocks = (2 * n + N_EXPERTS * (MOE_BLOCK - 1)) // MOE_BLOCK
    blk_ids = jnp.arange(n_blocks, dtype=jnp.int32)
    blk_expert = jnp.minimum(
        jnp.sum((seg_end[None, :] <= blk_ids[:, None] * MOE_BLOCK).astype(jnp.int32), axis=1),
        N_EXPERTS - 1)
    n_used = seg_end[-1:] // MOE_BLOCK
    pad_start = jnp.concatenate([seg_start + seg, blk_ids * MOE_BLOCK])
    pad_rows = jnp.concatenate([seg_pad - seg, jnp.where(blk_ids >= n_used[0], MOE_BLOCK, 0)])
    flat = lambda a: a.reshape(-1)
    runs = (flat(lo), flat(cnt), flat(dst))
    ids = jnp.arange(N_EXPERTS, dtype=jnp.int32)
    later = jnp.where((seg_pad[None, :] > 0) & (ids[None, :] > ids[:, None]), ids[None, :],
                      N_EXPERTS)
    next_expert = jnp.min(later, axis=1)
    next_expert = jnp.where(next_expert < N_EXPERTS, next_expert, -1)
    xs = _dispatch(xn_slab, s1, s2, runs + (pad_start, pad_rows), tile, n_blocks * MOE_BLOCK)
    rows = _experts(xs, blk_expert, next_expert, n_used, wts, layer)
    return _combine(yp, ys, rows, (s1, s2, w1, w2), runs, norm_f, final_norm, tile)


def _prep_weights(norm1, w_in, ln_g, ln_b, w_s, b_s, conv_w, conv_b, lru_lambda, w_rg_a, b_rg_a,
                  w_rg_x, b_rg_x, w_out, norm2, w_route_group, b_route_group, w_route_expert,
                  b_route_expert, w_gate, w_up, w_down, n_sample_t):
    depth, d = w_in.shape[0], w_in.shape[1]
    dh = d // H_A
    row = lambda a: a.reshape(depth, 1, -1).astype(F32)
    causal = jnp.tril(jnp.ones((CHUNK, CHUNK), dtype=bool))
    ws = jnp.where(causal, 0.5 * w_s, 0.0)
    bs = jnp.repeat(jnp.swapaxes(0.5 * b_s, 1, 2), dh, axis=2)
    in_scale = jnp.concatenate([jnp.ones((3 * d,), F32), jnp.full((2 * d,), 0.5, F32)])

    def pair_bd(w):
        w = w.reshape(depth, H_B // 2, 2, w.shape[-2], w.shape[-1])
        z = jnp.zeros_like(w[:, :, 0])
        return jnp.concatenate([jnp.concatenate([w[:, :, 0], z], axis=-1),
                                jnp.concatenate([z, w[:, :, 1]], axis=-1)], axis=-2)

    gap = SUBLANES - N_GROUPS
    tail = ROUTE_ROWS - SUBLANES - N_EXPERTS
    w_route = jnp.concatenate(
        [jnp.swapaxes(w_route_group, 1, 2), jnp.zeros((depth, gap, d), F32),
         jnp.swapaxes(w_route_expert, 1, 2), jnp.zeros((depth, tail, d), F32)], axis=1)
    b_route = jnp.concatenate(
        [b_route_group, jnp.zeros((depth, gap), F32), b_route_expert,
         jnp.zeros((depth, tail), F32)], axis=1)[..., None]
    nt = n_sample_t
    return dict(
        norm1=row(norm1), w_in=(w_in * in_scale).astype(BF16), ln_g=row(ln_g), ln_b=row(ln_b),
        ws=ws.astype(BF16), bs=bs.astype(F32), conv_w=conv_w.astype(F32), conv_b=row(conv_b),
        lam=row(lru_lambda),
        wbd=(0.5 * jnp.concatenate([pair_bd(w_rg_a), pair_bd(w_rg_x)], axis=-1)).astype(BF16),
        b_a=row(0.5 * b_rg_a), b_x=row(0.5 * b_rg_x), w_out=(0.5 * w_out).astype(BF16),
        norm2=row(norm2), w_route=w_route.astype(BF16), b_route=b_route.astype(F32),
        w_gate=w_gate, w_up=w_up, w_down=w_down,
        wsc=jnp.repeat(jnp.transpose(ws[:, :, :nt, :nt], (0, 2, 3, 1)), dh, axis=3).astype(F32),
        bsc=bs[:, :nt].astype(F32),
    )


def kernel(x_prompt, x_sample, state_lru_h, state_conv, norm1, w_in, ln_g, ln_b, w_s, b_s, conv_w, conv_b, lru_lambda, w_rg_a, b_rg_a, w_rg_x, b_rg_x, w_out, norm2, w_route_group, b_route_group, w_route_expert, b_route_expert, w_gate, w_up, w_down, norm_f):
    depth = w_in.shape[0]
    nb, seq, d = x_prompt.shape
    ns, nt, _ = x_sample.shape
    assert nt <= CHUNK
    nf = norm_f.reshape(1, d).astype(F32)
    wts = _prep_weights(norm1, w_in, ln_g, ln_b, w_s, b_s, conv_w, conv_b, lru_lambda, w_rg_a,
                        b_rg_a, w_rg_x, b_rg_x, w_out, norm2, w_route_group, b_route_group,
                        w_route_expert, b_route_expert, w_gate, w_up, w_down, nt)
    conv_state = jnp.transpose(state_conv, (0, 2, 1, 3))
    assert (nb * seq) % MOE_TILE == 0
    sample_pad = (-nt * ns) % MOE_TILE
    assert sample_pad % ns == 0
    pad_t = sample_pad // ns
    yp = x_prompt
    ys = jnp.transpose(x_sample, (1, 0, 2))
    hp, cp, vp, hs, cs, vs = [], [], [], [], [], []
    for l in range(depth):
        yp, h, c, v = _mixer_prompt(yp, wts, l)
        hp.append(h)
        cp.append(c)
        vp.append(v)
        ys, h, c, v = _mixer_sample(ys, state_lru_h, conv_state, wts, l, nt, pad_t)
        hs.append(h)
        cs.append(c)
        vs.append(v)
        yp, ys = _moe(yp.reshape(nb * seq, d), ys.reshape((nt + pad_t) * ns, d), wts, l, nf,
                      l == depth - 1, MOE_TILE)
        yp = yp.reshape(nb, seq, d)
        ys = ys.reshape(nt + pad_t, ns, d)
    to_seq_major = lambda a: jnp.transpose(jnp.stack(a), (0, 2, 1, 3))
    return (yp, jnp.transpose(ys[:nt], (1, 0, 2)), jnp.stack(hp), jnp.stack(cp), jnp.stack(vp),
            jnp.stack(hs), to_seq_major(cs), to_seq_major(vs))
```

```python
import functools

import jax
import jax.numpy as jnp
from jax import lax
from jax.experimental import pallas as pl
from jax.experimental.pallas import tpu as pltpu

F32 = jnp.float32
BF16 = jnp.bfloat16

CHUNK = 128
H_A = 4
H_B = 8
CONV_W = 4
LRU_C = 8.0
N_GROUPS = 4
E_PER_GROUP = 8
N_EXPERTS = N_GROUPS * E_PER_GROUP
EPS = 1e-6

LANES = 128
SUBLANES = 8
SCAN_UNROLL = 4
SEQS_PER_TRIP = 8
ROUTE_ROWS = 48
MOE_TILE = 1024
MOE_BLOCK = 256
VMEM_LIMIT = 60 * 1024 * 1024


def _dot(a, b):
    return jnp.dot(a, b, preferred_element_type=F32)


def _sigmoid(x):
    return 0.5 * (jnp.tanh(0.5 * x) + 1.0)


def _gelu2(x):
    c = 0.7978845608028654
    return x * (1.0 + jnp.tanh(x * (c + (c * 0.044715) * (x * x))))


def _rms_norm(x, g):
    return x * lax.rsqrt(jnp.mean(x * x, axis=-1, keepdims=True) + EPS) * g


def _layer_norm(x, g, b, eps):
    xc = x - jnp.mean(x, axis=-1, keepdims=True)
    var = jnp.mean(xc * xc, axis=-1, keepdims=True)
    return xc * lax.rsqrt(var + eps) * g + b


def _softplus(x):
    return jnp.maximum(x, 0.0) + jnp.log(1.0 + jnp.exp(-jnp.abs(x)))


def _lru_gates(xc, wbd_ref, b_a, b_x, lam_half):
    d = xc.shape[-1]
    xcb = xc.astype(BF16)
    r_parts, i_parts = [], []
    for q in range(d // 256):
        ri = _dot(xcb[:, q * 256:(q + 1) * 256], wbd_ref[q])
        r_parts.append(ri[:, :256])
        i_parts.append(ri[:, 256:])
    r2 = 1.0 + jnp.tanh(jnp.concatenate(r_parts, axis=-1) + b_a)
    i2 = 1.0 + jnp.tanh(jnp.concatenate(i_parts, axis=-1) + b_x)
    a = jnp.exp(lam_half * r2)
    mult_half = jnp.sqrt(0.25 - 0.25 * (a * a))
    return a, mult_half, i2 * xc


def _shift_rows(x, hist, j):
    sh = pltpu.roll(x, j, axis=0)
    top = jnp.where(lax.broadcasted_iota(jnp.int32, (SUBLANES, 1), 0) < j,
                    pltpu.roll(hist, j, axis=0), sh[:SUBLANES])
    return jnp.concatenate([top, sh[SUBLANES:]], axis=0)


def _mixer_prompt_kernel(x_ref, norm_ref, win_ref, lng_ref, lnb_ref, ws_ref, bs_ref,
                         cw_ref, cb_ref, lam_ref, wbd_ref, ba_ref, bx_ref, wout_ref,
                         y_ref, h_ref, conv_ref, v_ref,
                         hist_s, a_s, b_s, ya_s, gb_s, hc_s):
    i = pl.program_id(0)
    last = pl.num_programs(0) - 1
    nseq, _, d = x_ref.shape
    nlt = d // LANES
    spt = SEQS_PER_TRIP
    pair = spt * CHUNK

    @pl.when(i == 0)
    def _():
        hist_s[...] = jnp.zeros_like(hist_s)
        hc_s[...] = jnp.zeros_like(hc_s)

    lam_half = (-0.5 * LRU_C) * _softplus(-lam_ref[...])
    is_first = i == 0

    def branch_body(p, c):
        row = pl.multiple_of(p * pair, pair)
        x = jnp.concatenate([x_ref[spt * p + k] for k in range(spt)], axis=0)
        xnb = _rms_norm(x, norm_ref[...]).astype(BF16)
        v = _layer_norm(_gelu2(_dot(xnb, win_ref[:, d:2 * d])), lng_ref[...], lnb_ref[...],
                        4.0 * EPS)

        @pl.when(i == last)
        def _():
            for k in range(spt):
                v_ref[spt * p + k] = v[k * CHUNK:(k + 1) * CHUNK]

        vb = v.astype(BF16)
        dh = d // H_A
        z_rows = []
        for k in range(spt):
            zs = [_dot(ws_ref[h], vb[k * CHUNK:(k + 1) * CHUNK, h * dh:(h + 1) * dh])
                  for h in range(H_A)]
            z_rows.append(jnp.concatenate(zs, axis=-1) + bs_ref[...])
        z = jnp.concatenate(z_rows, axis=0)
        ya = _gelu2(_dot(xnb, win_ref[:, 0:d])) * z
        ga2 = 1.0 + jnp.tanh(_dot(xnb, win_ref[:, 3 * d:4 * d]))
        ya_s[pl.ds(row, pair), :] = (ga2 * ya).astype(BF16)
        gb_s[pl.ds(row, pair), :] = (1.0 + jnp.tanh(_dot(xnb, win_ref[:, 4 * d:5 * d]))).astype(BF16)
        xr = _dot(xnb, win_ref[:, 2 * d:3 * d])
        xcs = []
        for k in range(spt):
            s = spt * p + k
            xk = xr[k * CHUNK:(k + 1) * CHUNK]
            hist = hist_s[s]
            xc = cb_ref[...] + xk * cw_ref[CONV_W - 1:CONV_W, :]
            for j in range(1, CONV_W):
                xc = xc + _shift_rows(xk, hist, j) * cw_ref[CONV_W - 1 - j:CONV_W - j, :]
            hist_s[s] = xk[CHUNK - SUBLANES:]
            xcs.append(xc)
        xc = jnp.concatenate(xcs, axis=0)
        a, mult_half, ix2 = _lru_gates(xc, wbd_ref, ba_ref[...], bx_ref[...], lam_half)
        bt = mult_half * ix2
        for k in range(spt):
            s = spt * p + k
            for j in range(nlt):
                a_s[j, pl.ds(s, CHUNK, stride=nseq), :] = a[k * CHUNK:(k + 1) * CHUNK, j * LANES:(j + 1) * LANES]
                b_s[j, pl.ds(s, CHUNK, stride=nseq), :] = bt[k * CHUNK:(k + 1) * CHUNK, j * LANES:(j + 1) * LANES]

        @pl.when(is_first)
        def _():
            for k in range(spt):
                first = 0.5 * ix2[k * CHUNK:k * CHUNK + 1]
                for j in range(nlt):
                    b_s[j, pl.ds(spt * p + k, 1), :] = first[:, j * LANES:(j + 1) * LANES]

        return c

    lax.fori_loop(0, nseq // spt, branch_body, 0)

    def scan_body(tt, hs):
        for u in range(SCAN_UNROLL):
            row = pl.multiple_of((tt * SCAN_UNROLL + u) * nseq, nseq)
            out = []
            for j in range(nlt):
                h = a_s[j, pl.ds(row, nseq), :] * hs[j] + b_s[j, pl.ds(row, nseq), :]
                b_s[j, pl.ds(row, nseq), :] = h
                out.append(h)
            hs = tuple(out)
        return hs

    h0 = tuple(hc_s[:, j * LANES:(j + 1) * LANES] for j in range(nlt))
    hs = lax.fori_loop(0, CHUNK // SCAN_UNROLL, scan_body, h0)
    hfin = jnp.concatenate(hs, axis=-1)
    hc_s[...] = hfin
    h_ref[...] = hfin

    def out_body(p, c):
        row = pl.multiple_of(p * pair, pair)
        hrows = []
        for k in range(spt):
            hrows.append(jnp.concatenate(
                [b_s[j, pl.ds(spt * p + k, CHUNK, stride=nseq), :] for j in range(nlt)], axis=-1))
        h = jnp.concatenate(hrows, axis=0)
        merged = (ya_s[pl.ds(row, pair), :].astype(F32)
                  + gb_s[pl.ds(row, pair), :].astype(F32) * h).astype(BF16)
        o = _dot(merged, wout_ref[...])
        for k in range(spt):
            y_ref[spt * p + k] = x_ref[spt * p + k] + o[k * CHUNK:(k + 1) * CHUNK]
        return c

    lax.fori_loop(0, nseq // spt, out_body, 0)

    @pl.when(i == last)
    def _():
        conv_ref[...] = hist_s[:, SUBLANES - (CONV_W - 1):, :]


def _const_spec(shape):
    nd = len(shape)
    return pl.BlockSpec(shape, lambda i, *_, _n=nd: (0,) * _n, pipeline_mode=pl.Buffered(1))


def _layer_spec(stacked, layer):
    shape = stacked.shape[1:]
    nd = len(shape)
    return pl.BlockSpec((None,) + tuple(shape), lambda i, *_, _n=nd: (layer,) + (0,) * _n,
                        pipeline_mode=pl.Buffered(1))


MIXER_WEIGHTS = ('norm1', 'w_in', 'ln_g', 'ln_b', 'ws', 'bs', 'conv_w', 'conv_b', 'lam', 'wbd',
                 'b_a', 'b_x', 'w_out')


def _mixer_prompt(x, wts, layer):
    nseq, seq, d = x.shape
    assert seq % CHUNK == 0 and nseq == SUBLANES and d % 256 == 0
    n_chunks = seq // CHUNK
    nlt = d // LANES
    weights = tuple(wts[k] for k in MIXER_WEIGHTS)
    x_spec = pl.BlockSpec((nseq, CHUNK, d), lambda i: (0, i, 0))
    out_shape = (
        jax.ShapeDtypeStruct((nseq, seq, d), F32),
        jax.ShapeDtypeStruct((nseq, d), F32),
        jax.ShapeDtypeStruct((nseq, CONV_W - 1, d), F32),
        jax.ShapeDtypeStruct((nseq, CHUNK, d), F32),
    )
    out_specs = (
        x_spec,
        _const_spec((nseq, d)),
        _const_spec((nseq, CONV_W - 1, d)),
        _const_spec((nseq, CHUNK, d)),
    )
    scratch = [
        pltpu.VMEM((nseq, SUBLANES, d), F32),
        pltpu.VMEM((nlt, CHUNK * nseq, LANES), F32),
        pltpu.VMEM((nlt, CHUNK * nseq, LANES), F32),
        pltpu.VMEM((nseq * CHUNK, d), BF16),
        pltpu.VMEM((nseq * CHUNK, d), BF16),
        pltpu.VMEM((nseq, d), F32),
    ]
    return pl.pallas_call(
        _mixer_prompt_kernel,
        grid=(n_chunks,),
        in_specs=[x_spec] + [_layer_spec(w, layer) for w in weights],
        out_specs=out_specs,
        out_shape=out_shape,
        scratch_shapes=scratch,
        compiler_params=pltpu.CompilerParams(
            dimension_semantics=("arbitrary",), vmem_limit_bytes=VMEM_LIMIT),
        name="mixer_prompt",
    )(x, *weights)


def _mixer_sample_kernel(x_ref, h0_ref, cst_ref, norm_ref, win_ref, lng_ref, lnb_ref, wsc_ref,
                         bsc_ref, cw_ref, cb_ref, lam_ref, wbd_ref, ba_ref, bx_ref, wout_ref,
                         y_ref, h_ref, conv_ref, v_ref):
    nt, ns, d = x_ref.shape
    at = lambda a, t: a[t * ns:(t + 1) * ns]
    lam_half = (-0.5 * LRU_C) * _softplus(-lam_ref[...])
    x = x_ref[...].reshape(nt * ns, d)
    xnb = _rms_norm(x, norm_ref[...]).astype(BF16)
    v = _layer_norm(_gelu2(_dot(xnb, win_ref[:, d:2 * d])), lng_ref[...], lnb_ref[...], 4.0 * EPS)
    v_ref[...] = v.reshape(nt, ns, d)
    zs = []
    for t in range(nt):
        z = bsc_ref[t:t + 1, :]
        for s in range(t + 1):
            z = z + wsc_ref[t, s:s + 1, :] * at(v, s)
        zs.append(z)
    ya = _gelu2(_dot(xnb, win_ref[:, 0:d])) * jnp.concatenate(zs, axis=0)
    ga2 = 1.0 + jnp.tanh(_dot(xnb, win_ref[:, 3 * d:4 * d]))
    gb2 = 1.0 + jnp.tanh(_dot(xnb, win_ref[:, 4 * d:5 * d]))
    xr = _dot(xnb, win_ref[:, 2 * d:3 * d])
    xp = [cst_ref[k] for k in range(CONV_W - 1)] + [at(xr, t) for t in range(nt)]
    xcs = []
    for t in range(nt):
        xc = cb_ref[...]
        for k in range(CONV_W):
            xc = xc + xp[t + k] * cw_ref[k:k + 1, :]
        xcs.append(xc)
    a, mult_half, ix2 = _lru_gates(jnp.concatenate(xcs, axis=0), wbd_ref, ba_ref[...],
                                   bx_ref[...], lam_half)
    bt = mult_half * ix2
    h = h0_ref[...]
    hs = []
    for t in range(nt):
        h = at(a, t) * h + at(bt, t)
        hs.append(h)
    merged = (ga2 * ya + gb2 * jnp.concatenate(hs, axis=0)).astype(BF16)
    y_ref[0:nt] = (x + _dot(merged, wout_ref[...])).reshape(nt, ns, d)
    if y_ref.shape[0] > nt:
        y_ref[nt:] = jnp.zeros((y_ref.shape[0] - nt, ns, d), F32)
    h_ref[...] = h
    for k in range(CONV_W - 1):
        conv_ref[k] = xp[nt + k]


def _mixer_sample(x, h0, conv_state, wts, layer, nt, pad_t):
    _, ns, d = x.shape
    names = MIXER_WEIGHTS[:4] + ('wsc', 'bsc') + MIXER_WEIGHTS[6:]
    stacked = (h0, conv_state) + tuple(wts[k] for k in names)
    out_shape = (
        jax.ShapeDtypeStruct((nt + pad_t, ns, d), F32),
        jax.ShapeDtypeStruct((ns, d), F32),
        jax.ShapeDtypeStruct((CONV_W - 1, ns, d), F32),
        jax.ShapeDtypeStruct((nt, ns, d), F32),
    )
    return pl.pallas_call(
        _mixer_sample_kernel,
        grid=(1,),
        in_specs=[_const_spec((nt, ns, d))] + [_layer_spec(w, layer) for w in stacked],
        out_specs=tuple(_const_spec(s.shape) for s in out_shape),
        out_shape=out_shape,
        compiler_params=pltpu.CompilerParams(
            dimension_semantics=("arbitrary",), vmem_limit_bytes=VMEM_LIMIT),
        name="mixer_sample",
    )(x, *stacked)


def _to_slab(ref, val):
    t, d = val.shape
    nlt = d // LANES
    for j in range(nlt):
        ref[pl.ds(j, t, stride=nlt), :] = val[:, j * LANES:(j + 1) * LANES]


def _from_slab(ref, t, nlt):
    return jnp.concatenate([ref[pl.ds(j, t, stride=nlt), :] for j in range(nlt)], axis=-1)


def _tokens_from_slab(slab_ref, nlt):
    return slab_ref[...].reshape(slab_ref.shape[0] // nlt, nlt, LANES).astype(BF16)


def _tokens_to_slab(slab_ref, tokens):
    slab_ref[...] = tokens.astype(F32).reshape(slab_ref.shape)


def _router_kernel(yp_ref, ys_ref, norm_ref, wr_ref, br_ref, tri_ref, low_ref, xn_ref, s1_ref,
                   s2_ref, w1_ref, w2_ref, cnt_ref, slab_s, *, n_prompt_tiles):
    y = jnp.where(pl.program_id(0) < n_prompt_tiles, yp_ref[...], ys_ref[...])
    xn = _rms_norm(y, norm_ref[...])
    _to_slab(slab_s, xn)
    xn_ref[...] = _tokens_from_slab(slab_s, xn_ref.shape[1])
    lt = lax.dot_general(wr_ref[...], xn.astype(BF16), (((1,), (1,)), ((), ())),
                         preferred_element_type=F32) + br_ref[...]
    tile = lt.shape[1]
    row = lax.broadcasted_iota(jnp.int32, (SUBLANES, tile), 0).astype(F32)
    neg = jnp.float32(-jnp.inf)
    big = jnp.float32(SUBLANES)

    def argmax_first(vals):
        m = jnp.max(vals, axis=0, keepdims=True)
        return m, jnp.min(jnp.where(vals == m, row, big), axis=0, keepdims=True)

    def group_block(x, gi):
        out = x[SUBLANES:2 * SUBLANES]
        for g in range(1, N_GROUPS):
            out = jnp.where(gi == g, x[(g + 1) * SUBLANES:(g + 2) * SUBLANES], out)
        return out

    gl = jnp.where(row < N_GROUPS, lt[0:SUBLANES], neg)
    gm, gi = argmax_first(gl)
    pg_top = 1.0 / jnp.sum(jnp.exp(gl - gm), axis=0, keepdims=True)
    el = group_block(lt, gi)
    m1, i1 = argmax_first(el)
    m2, i2 = argmax_first(jnp.where(row == i1, neg, el))
    e2 = jnp.exp(m2 - m1)
    w1 = pg_top / (1.0 + e2)
    w2 = pg_top * e2 / (1.0 + e2)
    hot1 = row == i1
    hot2 = row == i2
    picked = jnp.where(jnp.logical_or(hot1, hot2), 1.0, 0.0)
    zero = jnp.zeros_like(picked)
    onehot = jnp.concatenate(
        [zero] + [jnp.where(gi == g, picked, 0.0) for g in range(N_GROUPS)] + [zero],
        axis=0).astype(BF16)
    before = _dot(onehot, tri_ref[...])
    lower = jnp.sum(_dot(low_ref[...], onehot), axis=1, keepdims=True)
    slot = group_block(before + lower, gi)
    p1 = jnp.sum(jnp.where(hot1, slot, 0.0), axis=0, keepdims=True)
    p2 = jnp.sum(jnp.where(hot2, slot, 0.0), axis=0, keepdims=True)
    s1_ref[0] = p1.astype(jnp.int32)
    s2_ref[0] = p2.astype(jnp.int32)
    w1_ref[0] = w1
    w2_ref[0] = w2
    cnt = jnp.sum(onehot.astype(F32), axis=1, keepdims=True)
    cnt_ref[0] = jnp.broadcast_to(cnt, cnt_ref.shape[1:])


def _token_specs(tile, d, n_prompt_tiles):
    last = n_prompt_tiles - 1
    return (pl.BlockSpec((tile, d), lambda i, *_: (jnp.minimum(i, last), 0)),
            pl.BlockSpec((tile, d), lambda i, *_: (jnp.maximum(i - n_prompt_tiles, 0), 0)))


def _router(yp, ys, wts, layer, tile):
    d = yp.shape[1]
    nlt = d // LANES
    n_prompt_tiles = yp.shape[0] // tile
    n_tiles = n_prompt_tiles + ys.shape[0] // tile
    tri = jnp.triu(jnp.ones((tile, tile), BF16), 1)
    low = jnp.tril(jnp.ones((ROUTE_ROWS, ROUTE_ROWS), BF16), -1)
    per_token = pl.BlockSpec((1, 1, tile), lambda i: (i, 0, 0))
    return pl.pallas_call(
        functools.partial(_router_kernel, n_prompt_tiles=n_prompt_tiles),
        grid=(n_tiles,),
        in_specs=list(_token_specs(tile, d, n_prompt_tiles)) + [
            _layer_spec(wts['norm2'], layer), _layer_spec(wts['w_route'], layer),
            _layer_spec(wts['b_route'], layer), _const_spec((tile, tile)),
            _const_spec((ROUTE_ROWS, ROUTE_ROWS))],
        out_specs=(pl.BlockSpec((tile, nlt, LANES), lambda i: (i, 0, 0)),
                   per_token, per_token, per_token, per_token,
                   pl.BlockSpec((1, ROUTE_ROWS, LANES), lambda i: (i, 0, 0))),
        out_shape=(jax.ShapeDtypeStruct((n_tiles * tile, nlt, LANES), BF16),
                   jax.ShapeDtypeStruct((n_tiles, 1, tile), jnp.int32),
                   jax.ShapeDtypeStruct((n_tiles, 1, tile), jnp.int32),
                   jax.ShapeDtypeStruct((n_tiles, 1, tile), F32),
                   jax.ShapeDtypeStruct((n_tiles, 1, tile), F32),
                   jax.ShapeDtypeStruct((n_tiles, ROUTE_ROWS, LANES), F32)),
        scratch_shapes=[pltpu.VMEM((tile * nlt, LANES), F32)],
        compiler_params=pltpu.CompilerParams(
            dimension_semantics=("arbitrary",), vmem_limit_bytes=VMEM_LIMIT),
        name="moe_router",
    )(yp, ys, wts['norm2'], wts['w_route'], wts['b_route'], tri, low)


def _run_copy(tile_idx, e, lo_ref, cnt_ref, dst_ref, local, remote, sem, to_remote):
    k = tile_idx * N_EXPERTS + e
    loc = local.at[pl.ds(lo_ref[k], cnt_ref[k])]
    rem = remote.at[pl.ds(dst_ref[k], cnt_ref[k])]
    return pltpu.make_async_copy(loc, rem, sem) if to_remote else pltpu.make_async_copy(rem, loc, sem)


def _start_runs(tile_idx, lo_ref, cnt_ref, dst_ref, local, remote, sem, to_remote):
    for e in range(N_EXPERTS):
        @pl.when(cnt_ref[tile_idx * N_EXPERTS + e] > 0)
        def _():
            _run_copy(tile_idx, e, lo_ref, cnt_ref, dst_ref, local, remote, sem, to_remote).start()


def _wait_runs(local, sem):
    pltpu.make_async_copy(local, local, sem).wait()


def _dispatch_kernel(lo_ref, cnt_ref, dst_ref, padst_ref, padn_ref,
                     s1_ref, s2_ref, x_ref, xs_ref, loc, zeros, sem, *, unroll):
    i = pl.program_id(0)
    last = pl.num_programs(0) - 1
    tile = x_ref.shape[0]
    slot = i % 2
    buf = loc.at[slot]

    @pl.when(i >= 2)
    def _():
        _wait_runs(buf, sem.at[slot])

    def body(tt, c):
        for u in range(unroll):
            t = tt * unroll + u
            row = x_ref[t]
            buf[s1_ref[t]] = row
            buf[s2_ref[t]] = row
        return c

    lax.fori_loop(0, tile // unroll, body, 0)
    _start_runs(i, lo_ref, cnt_ref, dst_ref, buf, xs_ref, sem.at[slot], True)

    @pl.when(i == last)
    def _():
        zeros[...] = jnp.zeros_like(zeros)

        def pad_copy(e):
            n = padn_ref[e]
            dst = xs_ref.at[pl.ds(padst_ref[e], n)]
            return pltpu.make_async_copy(zeros.at[pl.ds(0, n)], dst, sem.at[2])

        def each_pad(fn):
            def b(e, c):
                @pl.when(padn_ref[e] > 0)
                def _():
                    fn(e)
                return c
            lax.fori_loop(0, padn_ref.shape[0], b, 0)

        each_pad(lambda e: pad_copy(e).start())
        _wait_runs(buf, sem.at[slot])

        @pl.when(i >= 1)
        def _():
            _wait_runs(loc.at[1 - slot], sem.at[1 - slot])

        each_pad(lambda e: pad_copy(e).wait())


def _smem_tile(tile):
    return pl.BlockSpec((None, None, tile), lambda i, *_: (i, 0, 0), memory_space=pltpu.SMEM)


def _dispatch(xn_slab, s1, s2, tables, tile, n_rows):
    token = xn_slab.shape[1:]
    return pl.pallas_call(
        functools.partial(_dispatch_kernel, unroll=8),
        grid_spec=pltpu.PrefetchScalarGridSpec(
            num_scalar_prefetch=5,
            grid=(xn_slab.shape[0] // tile,),
            in_specs=[_smem_tile(tile), _smem_tile(tile),
                      pl.BlockSpec((tile,) + token, lambda i, *_: (i, 0, 0))],
            out_specs=pl.BlockSpec(memory_space=pl.ANY),
            scratch_shapes=[pltpu.VMEM((2, 2 * tile) + token, xn_slab.dtype),
                            pltpu.VMEM((MOE_BLOCK,) + token, xn_slab.dtype),
                            pltpu.SemaphoreType.DMA((3,))]),
        out_shape=jax.ShapeDtypeStruct((n_rows,) + token, xn_slab.dtype),
        compiler_params=pltpu.CompilerParams(
            dimension_semantics=("arbitrary",), vmem_limit_bytes=VMEM_LIMIT),
        name="moe_dispatch",
    )(*tables, s1, s2, xn_slab)


def _experts_kernel(blk_ref, next_ref, nused_ref, xs_ref, wg_ref, wu_ref, wd_ref, ys_ref,
                    xin, yout, wg_f, wu_f, wd_f, wg_s, wu_s, wd_s, slab_in, slab_out,
                    sem_in, sem_out, sem_w, *, layer):
    n_blocks = ys_ref.shape[0] // MOE_BLOCK
    nlt = xs_ref.shape[1]
    n_used = nused_ref[0]
    block = lambda ref, b: ref.at[pl.ds(pl.multiple_of(b * MOE_BLOCK, MOE_BLOCK), MOE_BLOCK)]
    rows_in = lambda b, slot: pltpu.make_async_copy(block(xs_ref, b), xin.at[slot], sem_in.at[slot])
    rows_out = lambda b, slot: pltpu.make_async_copy(yout.at[slot], block(ys_ref, b),
                                                     sem_out.at[slot])

    def weights(e, slot):
        pairs = ((wg_ref, wg_f), (wu_ref, wu_f), (wd_ref, wd_f))
        return [pltpu.make_async_copy(src.at[layer, e], dst.at[slot], sem_w.at[slot])
                for src, dst in pairs]

    @pl.when(n_used > 0)
    def _():
        rows_in(0, 0).start()
        for c in weights(blk_ref[0], 0):
            c.start()

    def step(b, wslot):
        slot = b % 2
        e = blk_ref[b]
        rows_in(b, slot).wait()

        @pl.when(b + 1 < n_used)
        def _():
            rows_in(b + 1, 1 - slot).start()

        new_expert = jnp.logical_or(b == 0, e != blk_ref[jnp.maximum(b - 1, 0)])
        wslot = jnp.where(new_expert, 1 - wslot, wslot)

        @pl.when(new_expert)
        def _():
            for c in weights(e, wslot):
                c.wait()
            wg_s[...] = wg_f[wslot].astype(BF16)
            wu_s[...] = wu_f[wslot].astype(BF16)
            wd_s[...] = wd_f[wslot].astype(BF16)

            @pl.when(next_ref[e] >= 0)
            def _():
                for c in weights(next_ref[e], 1 - wslot):
                    c.start()

        _tokens_to_slab(slab_in, xin[slot])
        x = _from_slab(slab_in, MOE_BLOCK, nlt).astype(BF16)
        hg = _dot(x, wg_s[...])
        hu = _dot(x, wu_s[...])
        h = (hg * _sigmoid(hg) * hu).astype(BF16)
        _to_slab(slab_out, _dot(h, wd_s[...]))

        @pl.when(b >= 2)
        def _():
            rows_out(b - 2, slot).wait()

        yout[slot] = _tokens_from_slab(slab_out, nlt)
        rows_out(b, slot).start()
        return wslot

    lax.fori_loop(0, n_used, step, jnp.int32(1))

    @pl.when(n_used >= 2)
    def _():
        rows_out(n_used - 2, n_used % 2).wait()

    @pl.when(n_used >= 1)
    def _():
        rows_out(n_used - 1, (n_used - 1) % 2).wait()

    yout[0] = jnp.zeros(yout.shape[1:], yout.dtype)

    def fill(b, c):
        rows_out(b, 0).start()
        rows_out(b, 0).wait()
        return c

    lax.fori_loop(n_used, n_blocks, fill, 0)


def _experts(xs, blk_expert, next_expert, n_used, wts, layer):
    n_rows, nlt, _ = xs.shape
    d = nlt * LANES
    f = wts['w_gate'].shape[-1]
    token = (nlt, LANES)
    any_spec = pl.BlockSpec(memory_space=pl.ANY)
    return pl.pallas_call(
        functools.partial(_experts_kernel, layer=layer),
        grid_spec=pltpu.PrefetchScalarGridSpec(
            num_scalar_prefetch=3,
            grid=(1,),
            in_specs=[any_spec] * 4,
            out_specs=any_spec,
            scratch_shapes=[pltpu.VMEM((2, MOE_BLOCK) + token, xs.dtype),
                            pltpu.VMEM((2, MOE_BLOCK) + token, xs.dtype),
                            pltpu.VMEM((2, d, f), F32), pltpu.VMEM((2, d, f), F32),
                            pltpu.VMEM((2, f, d), F32),
                            pltpu.VMEM((d, f), BF16), pltpu.VMEM((d, f), BF16),
                            pltpu.VMEM((f, d), BF16),
                            pltpu.VMEM((MOE_BLOCK * nlt, LANES), F32),
                            pltpu.VMEM((MOE_BLOCK * nlt, LANES), F32),
                            pltpu.SemaphoreType.DMA((2,)), pltpu.SemaphoreType.DMA((2,)),
                            pltpu.SemaphoreType.DMA((2,))]),
        out_shape=jax.ShapeDtypeStruct(xs.shape, xs.dtype),
        compiler_params=pltpu.CompilerParams(
            dimension_semantics=("arbitrary",), vmem_limit_bytes=VMEM_LIMIT),
        name="moe_experts",
    )(blk_expert, next_expert, n_used, xs, wts['w_gate'], wts['w_up'], wts['w_down'])


def _combine_kernel(lo_ref, cnt_ref, dst_ref, s1_ref, s2_ref, w1_ref, w2_ref, yp_ref, ys_ref,
                    nf_ref, rows_ref, op_ref, os_ref, loc, acc, sem,
                    *, unroll, final_norm, n_prompt_tiles):
    i = pl.program_id(0)
    tile = yp_ref.shape[0]
    nlt = loc.shape[2]
    slot = i % 2
    buf = loc.at[slot]
    fetch = lambda t, s: _start_runs(t, lo_ref, cnt_ref, dst_ref, loc.at[s], rows_ref, sem.at[s],
                                     False)

    @pl.when(i == 0)
    def _():
        fetch(0, 0)

    @pl.when(i + 1 < pl.num_programs(0))
    def _():
        fetch(i + 1, 1 - slot)

    _wait_runs(buf, sem.at[slot])

    def body(tt, c):
        for u in range(unroll):
            t = tt * unroll + u
            r1 = buf[s1_ref[t]].astype(F32)
            r2 = buf[s2_ref[t]].astype(F32)
            acc[pl.ds(pl.multiple_of(t * nlt, nlt), nlt), :] = w1_ref[t] * r1 + w2_ref[t] * r2
        return c

    lax.fori_loop(0, tile // unroll, body, 0)
    def finish(y_ref, o_ref):
        out = y_ref[...] + _from_slab(acc, tile, nlt)
        if final_norm:
            out = _rms_norm(out, nf_ref[...])
        o_ref[...] = out

    is_prompt = i < n_prompt_tiles
    pl.when(is_prompt)(lambda: finish(yp_ref, op_ref))
    pl.when(jnp.logical_not(is_prompt))(lambda: finish(ys_ref, os_ref))


def _combine(yp, ys, rows, meta, tables, norm_f, final_norm, tile):
    d = yp.shape[1]
    token = rows.shape[1:]
    n_prompt_tiles = yp.shape[0] // tile
    n_tiles = n_prompt_tiles + ys.shape[0] // tile
    tok_p, tok_s = _token_specs(tile, d, n_prompt_tiles)
    return pl.pallas_call(
        functools.partial(_combine_kernel, unroll=8, final_norm=final_norm,
                          n_prompt_tiles=n_prompt_tiles),
        grid_spec=pltpu.PrefetchScalarGridSpec(
            num_scalar_prefetch=3,
            grid=(n_tiles,),
            in_specs=[_smem_tile(tile)] * 4 + [tok_p, tok_s, _const_spec((1, d)),
                                               pl.BlockSpec(memory_space=pl.ANY)],
            out_specs=(tok_p, tok_s),
            scratch_shapes=[pltpu.VMEM((2, 2 * tile) + token, rows.dtype),
                            pltpu.VMEM((tile * token[0], LANES), F32),
                            pltpu.SemaphoreType.DMA((2,))]),
        out_shape=(jax.ShapeDtypeStruct(yp.shape, F32), jax.ShapeDtypeStruct(ys.shape, F32)),
        compiler_params=pltpu.CompilerParams(
            dimension_semantics=("arbitrary",), vmem_limit_bytes=VMEM_LIMIT),
        name="moe_combine",
    )(*tables, *meta, yp, ys, norm_f, rows)


def _moe(yp, ys, wts, layer, norm_f, final_norm, tile):
    d = yp.shape[1]
    n = yp.shape[0] + ys.shape[0]
    assert yp.shape[0] % tile == 0 and ys.shape[0] % tile == 0 and d % LANES == 0
    xn_slab, s1, s2, w1, w2, cnt = _router(yp, ys, wts, layer, tile)
    cnt = cnt[:, SUBLANES:SUBLANES + N_EXPERTS, 0].astype(jnp.int32)
    lo = jnp.cumsum(cnt, axis=1) - cnt
    seg = jnp.sum(cnt, axis=0)
    seg_pad = (seg + MOE_BLOCK - 1) // MOE_BLOCK * MOE_BLOCK
    seg_end = jnp.cumsum(seg_pad)
    seg_start = seg_end - seg_pad
    dst = seg_start[None, :] + jnp.cumsum(cnt, axis=0) - cnt
    n_blocks = (2 * n + N_EXPERTS * (MOE_BLOCK - 1)) // MOE_BLOCK
    blk_ids = jnp.arange(n_blocks, dtype=jnp.int32)
    blk_expert = jnp.minimum(
        jnp.sum((seg_end[None, :] <= blk_ids[:, None] * MOE_BLOCK).astype(jnp.int32), axis=1),
        N_EXPERTS - 1)
    n_used = seg_end[-1:] // MOE_BLOCK
    pad_start = jnp.concatenate([seg_start + seg, blk_ids * MOE_BLOCK])
    pad_rows = jnp.concatenate([seg_pad - seg, jnp.where(blk_ids >= n_used[0], MOE_BLOCK, 0)])
    flat = lambda a: a.reshape(-1)
    runs = (flat(lo), flat(cnt), flat(dst))
    ids = jnp.arange(N_EXPERTS, dtype=jnp.int32)
    later = jnp.where((seg_pad[None, :] > 0) & (ids[None, :] > ids[:, None]), ids[None, :],
                      N_EXPERTS)
    next_expert = jnp.min(later, axis=1)
    next_expert = jnp.where(next_expert < N_EXPERTS, next_expert, -1)
    xs = _dispatch(xn_slab, s1, s2, runs + (pad_start, pad_rows), tile, n_blocks * MOE_BLOCK)
    rows = _experts(xs, blk_expert, next_expert, n_used, wts, layer)
    return _combine(yp, ys, rows, (s1, s2, w1, w2), runs, norm_f, final_norm, tile)


def _prep_weights(norm1, w_in, ln_g, ln_b, w_s, b_s, conv_w, conv_b, lru_lambda, w_rg_a, b_rg_a,
                  w_rg_x, b_rg_x, w_out, norm2, w_route_group, b_route_group, w_route_expert,
                  b_route_expert, w_gate, w_up, w_down, n_sample_t):
    depth, d = w_in.shape[0], w_in.shape[1]
    dh = d // H_A
    row = lambda a: a.reshape(depth, 1, -1).astype(F32)
    causal = jnp.tril(jnp.ones((CHUNK, CHUNK), dtype=bool))
    ws = jnp.where(causal, 0.5 * w_s, 0.0)
    bs = jnp.repeat(jnp.swapaxes(0.5 * b_s, 1, 2), dh, axis=2)
    in_scale = jnp.concatenate([jnp.ones((3 * d,), F32), jnp.full((2 * d,), 0.5, F32)])

    def pair_bd(w):
        w = w.reshape(depth, H_B // 2, 2, w.shape[-2], w.shape[-1])
        z = jnp.zeros_like(w[:, :, 0])
        return jnp.concatenate([jnp.concatenate([w[:, :, 0], z], axis=-1),
                                jnp.concatenate([z, w[:, :, 1]], axis=-1)], axis=-2)

    gap = SUBLANES - N_GROUPS
    tail = ROUTE_ROWS - SUBLANES - N_EXPERTS
    w_route = jnp.concatenate(
        [jnp.swapaxes(w_route_group, 1, 2), jnp.zeros((depth, gap, d), F32),
         jnp.swapaxes(w_route_expert, 1, 2), jnp.zeros((depth, tail, d), F32)], axis=1)
    b_route = jnp.concatenate(
        [b_route_group, jnp.zeros((depth, gap), F32), b_route_expert,
         jnp.zeros((depth, tail), F32)], axis=1)[..., None]
    nt = n_sample_t
    return dict(
        norm1=row(norm1), w_in=(w_in * in_scale).astype(BF16), ln_g=row(ln_g), ln_b=row(ln_b),
        ws=ws.astype(BF16), bs=bs.astype(F32), conv_w=conv_w.astype(F32), conv_b=row(conv_b),
        lam=row(lru_lambda),
        wbd=(0.5 * jnp.concatenate([pair_bd(w_rg_a), pair_bd(w_rg_x)], axis=-1)).astype(BF16),
        b_a=row(0.5 * b_rg_a), b_x=row(0.5 * b_rg_x), w_out=(0.5 * w_out).astype(BF16),
        norm2=row(norm2), w_route=w_route.astype(BF16), b_route=b_route.astype(F32),
        w_gate=w_gate, w_up=w_up, w_down=w_down,
        wsc=jnp.repeat(jnp.transpose(ws[:, :, :nt, :nt], (0, 2, 3, 1)), dh, axis=3).astype(F32),
        bsc=bs[:, :nt].astype(F32),
    )


def kernel(x_prompt, x_sample, state_lru_h, state_conv, norm1, w_in, ln_g, ln_b, w_s, b_s, conv_w, conv_b, lru_lambda, w_rg_a, b_rg_a, w_rg_x, b_rg_x, w_out, norm2, w_route_group, b_route_group, w_route_expert, b_route_expert, w_gate, w_up, w_down, norm_f):
    depth = w_in.shape[0]
    nb, seq, d = x_prompt.shape
    ns, nt, _ = x_sample.shape
    assert nt <= CHUNK
    nf = norm_f.reshape(1, d).astype(F32)
    wts = _prep_weights(norm1, w_in, ln_g, ln_b, w_s, b_s, conv_w, conv_b, lru_lambda, w_rg_a,
                        b_rg_a, w_rg_x, b_rg_x, w_out, norm2, w_route_group, b_route_group,
                        w_route_expert, b_route_expert, w_gate, w_up, w_down, nt)
    conv_state = jnp.transpose(state_conv, (0, 2, 1, 3))
    assert (nb * seq) % MOE_TILE == 0
    sample_pad = (-nt * ns) % MOE_TILE
    assert sample_pad % ns == 0
    pad_t = sample_pad // ns
    yp = x_prompt
    ys = jnp.transpose(x_sample, (1, 0, 2))
    hp, cp, vp, hs, cs, vs = [], [], [], [], [], []
    for l in range(depth):
        yp, h, c, v = _mixer_prompt(yp, wts, l)
        hp.append(h)
        cp.append(c)
        vp.append(v)
        ys, h, c, v = _mixer_sample(ys, state_lru_h, conv_state, wts, l, nt, pad_t)
        hs.append(h)
        cs.append(c)
        vs.append(v)
        yp, ys = _moe(yp.reshape(nb * seq, d), ys.reshape((nt + pad_t) * ns, d), wts, l, nf,
                      l == depth - 1, MOE_TILE)
        yp = yp.reshape(nb, seq, d)
        ys = ys.reshape(nt + pad_t, ns, d)
    to_seq_major = lambda a: jnp.transpose(jnp.stack(a), (0, 2, 1, 3))
    return (yp, jnp.transpose(ys[:nt], (1, 0, 2)), jnp.stack(hp), jnp.stack(cp), jnp.stack(vp),
            jnp.stack(hs), to_seq_major(cs), to_seq_major(vs))
```

```python
import functools

import jax
import jax.numpy as jnp
from jax import lax
from jax.experimental import pallas as pl
from jax.experimental.pallas import tpu as pltpu

F32 = jnp.float32
BF16 = jnp.bfloat16

CHUNK = 128
H_A = 4
H_B = 8
CONV_W = 4
LRU_C = 8.0
N_GROUPS = 4
E_PER_GROUP = 8
N_EXPERTS = N_GROUPS * E_PER_GROUP
EPS = 1e-6

LANES = 128
SUBLANES = 8
BF16_ROWS = 16
V7X_VMEM_BYTES = 64 * 1024 * 1024
VMEM_LIMIT = V7X_VMEM_BYTES - 4 * 1024 * 1024
SCAN_UNROLL = 4
SEQS_PER_TRIP = 8
MOE_TILE = 1024
MOE_BLOCK = 512
assert E_PER_GROUP == SUBLANES and N_GROUPS <= SUBLANES
ROUTE_ROWS = -(-SUBLANES * (N_GROUPS + 1) // BF16_ROWS) * BF16_ROWS


def _dot(a, b):
    return jnp.dot(a, b, preferred_element_type=F32)


def _sigmoid(x):
    return 0.5 * (jnp.tanh(0.5 * x) + 1.0)


def _gelu2(x):
    c = 0.7978845608028654
    return x * (1.0 + jnp.tanh(x * (c + (c * 0.044715) * (x * x))))


def _rms_norm(x, g):
    return x * lax.rsqrt(jnp.mean(x * x, axis=-1, keepdims=True) + EPS) * g


def _layer_norm(x, g, b, eps):
    xc = x - jnp.mean(x, axis=-1, keepdims=True)
    var = jnp.mean(xc * xc, axis=-1, keepdims=True)
    return xc * lax.rsqrt(var + eps) * g + b


def _softplus(x):
    return jnp.maximum(x, 0.0) + jnp.log(1.0 + jnp.exp(-jnp.abs(x)))


def _lru_gates(xc, wbd_ref, b_a, b_x, lam_half):
    d = xc.shape[-1]
    xcb = xc.astype(BF16)
    r_parts, i_parts = [], []
    for q in range(d // 256):
        ri = _dot(xcb[:, q * 256:(q + 1) * 256], wbd_ref[q])
        r_parts.append(ri[:, :256])
        i_parts.append(ri[:, 256:])
    r2 = 1.0 + jnp.tanh(jnp.concatenate(r_parts, axis=-1) + b_a)
    i2 = 1.0 + jnp.tanh(jnp.concatenate(i_parts, axis=-1) + b_x)
    a = jnp.exp(lam_half * r2)
    mult_half = jnp.sqrt(0.25 - 0.25 * (a * a))
    return a, mult_half, i2 * xc


def _shift_rows(x, hist, j):
    sh = pltpu.roll(x, j, axis=0)
    top = jnp.where(lax.broadcasted_iota(jnp.int32, (SUBLANES, 1), 0) < j,
                    pltpu.roll(hist, j, axis=0), sh[:SUBLANES])
    return jnp.concatenate([top, sh[SUBLANES:]], axis=0)


def _mixer_prompt_kernel(x_ref, norm_ref, win_ref, lng_ref, lnb_ref, ws_ref, bs_ref,
                         cw_ref, cb_ref, lam_ref, wbd_ref, ba_ref, bx_ref, wout_ref,
                         y_ref, h_ref, conv_ref, v_ref,
                         hist_s, a_s, b_s, ya_s, gb_s, hc_s):
    i = pl.program_id(0)
    last = pl.num_programs(0) - 1
    nseq, _, d = x_ref.shape
    nlt = d // LANES
    spt = SEQS_PER_TRIP
    pair = spt * CHUNK

    @pl.when(i == 0)
    def _():
        hist_s[...] = jnp.zeros_like(hist_s)
        hc_s[...] = jnp.zeros_like(hc_s)

    lam_half = (-0.5 * LRU_C) * _softplus(-lam_ref[...])
    is_first = i == 0

    def branch_body(p, c):
        row = pl.multiple_of(p * pair, pair)
        x = jnp.concatenate([x_ref[spt * p + k] for k in range(spt)], axis=0)
        xnb = _rms_norm(x, norm_ref[...]).astype(BF16)
        v = _layer_norm(_gelu2(_dot(xnb, win_ref[:, d:2 * d])), lng_ref[...], lnb_ref[...],
                        4.0 * EPS)

        @pl.when(i == last)
        def _():
            for k in range(spt):
                v_ref[spt * p + k] = v[k * CHUNK:(k + 1) * CHUNK]

        vb = v.astype(BF16)
        dh = d // H_A
        z_rows = []
        for k in range(spt):
            zs = [_dot(ws_ref[h], vb[k * CHUNK:(k + 1) * CHUNK, h * dh:(h + 1) * dh])
                  for h in range(H_A)]
            z_rows.append(jnp.concatenate(zs, axis=-1) + bs_ref[...])
        z = jnp.concatenate(z_rows, axis=0)
        ya = _gelu2(_dot(xnb, win_ref[:, 0:d])) * z
        ga2 = 1.0 + jnp.tanh(_dot(xnb, win_ref[:, 3 * d:4 * d]))
        ya_s[pl.ds(row, pair), :] = (ga2 * ya).astype(BF16)
        gb_s[pl.ds(row, pair), :] = (1.0 + jnp.tanh(_dot(xnb, win_ref[:, 4 * d:5 * d]))).astype(BF16)
        xr = _dot(xnb, win_ref[:, 2 * d:3 * d])
        xcs = []
        for k in range(spt):
            s = spt * p + k
            xk = xr[k * CHUNK:(k + 1) * CHUNK]
            hist = hist_s[s]
            xc = cb_ref[...] + xk * cw_ref[CONV_W - 1:CONV_W, :]
            for j in range(1, CONV_W):
                xc = xc + _shift_rows(xk, hist, j) * cw_ref[CONV_W - 1 - j:CONV_W - j, :]
            hist_s[s] = xk[CHUNK - SUBLANES:]
            xcs.append(xc)
        xc = jnp.concatenate(xcs, axis=0)
        a, mult_half, ix2 = _lru_gates(xc, wbd_ref, ba_ref[...], bx_ref[...], lam_half)
        bt = mult_half * ix2
        for k in range(spt):
            s = spt * p + k
            for j in range(nlt):
                a_s[j, pl.ds(s, CHUNK, stride=nseq), :] = a[k * CHUNK:(k + 1) * CHUNK, j * LANES:(j + 1) * LANES]
                b_s[j, pl.ds(s, CHUNK, stride=nseq), :] = bt[k * CHUNK:(k + 1) * CHUNK, j * LANES:(j + 1) * LANES]

        @pl.when(is_first)
        def _():
            for k in range(spt):
                first = 0.5 * ix2[k * CHUNK:k * CHUNK + 1]
                for j in range(nlt):
                    b_s[j, pl.ds(spt * p + k, 1), :] = first[:, j * LANES:(j + 1) * LANES]

        return c

    lax.fori_loop(0, nseq // spt, branch_body, 0)

    def scan_body(tt, hs):
        for u in range(SCAN_UNROLL):
            row = pl.multiple_of((tt * SCAN_UNROLL + u) * nseq, nseq)
            out = []
            for j in range(nlt):
                h = a_s[j, pl.ds(row, nseq), :] * hs[j] + b_s[j, pl.ds(row, nseq), :]
                b_s[j, pl.ds(row, nseq), :] = h
                out.append(h)
            hs = tuple(out)
        return hs

    h0 = tuple(hc_s[:, j * LANES:(j + 1) * LANES] for j in range(nlt))
    hs = lax.fori_loop(0, CHUNK // SCAN_UNROLL, scan_body, h0)
    hfin = jnp.concatenate(hs, axis=-1)
    hc_s[...] = hfin
    h_ref[...] = hfin

    def out_body(p, c):
        row = pl.multiple_of(p * pair, pair)
        hrows = []
        for k in range(spt):
            hrows.append(jnp.concatenate(
                [b_s[j, pl.ds(spt * p + k, CHUNK, stride=nseq), :] for j in range(nlt)], axis=-1))
        h = jnp.concatenate(hrows, axis=0)
        merged = (ya_s[pl.ds(row, pair), :].astype(F32)
                  + gb_s[pl.ds(row, pair), :].astype(F32) * h).astype(BF16)
        o = _dot(merged, wout_ref[...])
        for k in range(spt):
            y_ref[spt * p + k] = x_ref[spt * p + k] + o[k * CHUNK:(k + 1) * CHUNK]
        return c

    lax.fori_loop(0, nseq // spt, out_body, 0)

    @pl.when(i == last)
    def _():
        conv_ref[...] = hist_s[:, SUBLANES - (CONV_W - 1):, :]


def _const_spec(shape):
    nd = len(shape)
    return pl.BlockSpec(shape, lambda i, *_, _n=nd: (0,) * _n, pipeline_mode=pl.Buffered(1))


def _layer_spec(stacked, layer):
    shape = stacked.shape[1:]
    nd = len(shape)
    return pl.BlockSpec((None,) + tuple(shape), lambda i, *_, _n=nd: (layer,) + (0,) * _n,
                        pipeline_mode=pl.Buffered(1))


MIXER_WEIGHTS = ('norm1', 'w_in', 'ln_g', 'ln_b', 'ws', 'bs', 'conv_w', 'conv_b', 'lam', 'wbd',
                 'b_a', 'b_x', 'w_out')


def _mixer_prompt(x, wts, layer):
    nseq, seq, d = x.shape
    assert seq % CHUNK == 0 and nseq == SUBLANES and d % 256 == 0
    n_chunks = seq // CHUNK
    nlt = d // LANES
    weights = tuple(wts[k] for k in MIXER_WEIGHTS)
    x_spec = pl.BlockSpec((nseq, CHUNK, d), lambda i: (0, i, 0))
    out_shape = (
        jax.ShapeDtypeStruct((nseq, seq, d), F32),
        jax.ShapeDtypeStruct((nseq, d), F32),
        jax.ShapeDtypeStruct((nseq, CONV_W - 1, d), F32),
        jax.ShapeDtypeStruct((nseq, CHUNK, d), F32),
    )
    out_specs = (
        x_spec,
        _const_spec((nseq, d)),
        _const_spec((nseq, CONV_W - 1, d)),
        _const_spec((nseq, CHUNK, d)),
    )
    scratch = [
        pltpu.VMEM((nseq, SUBLANES, d), F32),
        pltpu.VMEM((nlt, CHUNK * nseq, LANES), F32),
        pltpu.VMEM((nlt, CHUNK * nseq, LANES), F32),
        pltpu.VMEM((nseq * CHUNK, d), BF16),
        pltpu.VMEM((nseq * CHUNK, d), BF16),
        pltpu.VMEM((nseq, d), F32),
    ]
    return pl.pallas_call(
        _mixer_prompt_kernel,
        grid=(n_chunks,),
        in_specs=[x_spec] + [_layer_spec(w, layer) for w in weights],
        out_specs=out_specs,
        out_shape=out_shape,
        scratch_shapes=scratch,
        compiler_params=pltpu.CompilerParams(
            dimension_semantics=("arbitrary",), vmem_limit_bytes=VMEM_LIMIT),
        name="mixer_prompt",
    )(x, *weights)


def _mixer_sample_kernel(x_ref, h0_ref, cst_ref, norm_ref, win_ref, lng_ref, lnb_ref, wsc_ref,
                         bsc_ref, cw_ref, cb_ref, lam_ref, wbd_ref, ba_ref, bx_ref, wout_ref,
                         y_ref, h_ref, conv_ref, v_ref):
    nt, ns, d = x_ref.shape
    at = lambda a, t: a[t * ns:(t + 1) * ns]
    lam_half = (-0.5 * LRU_C) * _softplus(-lam_ref[...])
    x = x_ref[...].reshape(nt * ns, d)
    xnb = _rms_norm(x, norm_ref[...]).astype(BF16)
    v = _layer_norm(_gelu2(_dot(xnb, win_ref[:, d:2 * d])), lng_ref[...], lnb_ref[...], 4.0 * EPS)
    v_ref[...] = v.reshape(nt, ns, d)
    zs = []
    for t in range(nt):
        z = bsc_ref[t:t + 1, :]
        for s in range(t + 1):
            z = z + wsc_ref[t, s:s + 1, :] * at(v, s)
        zs.append(z)
    ya = _gelu2(_dot(xnb, win_ref[:, 0:d])) * jnp.concatenate(zs, axis=0)
    ga2 = 1.0 + jnp.tanh(_dot(xnb, win_ref[:, 3 * d:4 * d]))
    gb2 = 1.0 + jnp.tanh(_dot(xnb, win_ref[:, 4 * d:5 * d]))
    xr = _dot(xnb, win_ref[:, 2 * d:3 * d])
    xp = [cst_ref[k] for k in range(CONV_W - 1)] + [at(xr, t) for t in range(nt)]
    xcs = []
    for t in range(nt):
        xc = cb_ref[...]
        for k in range(CONV_W):
            xc = xc + xp[t + k] * cw_ref[k:k + 1, :]
        xcs.append(xc)
    a, mult_half, ix2 = _lru_gates(jnp.concatenate(xcs, axis=0), wbd_ref, ba_ref[...],
                                   bx_ref[...], lam_half)
    bt = mult_half * ix2
    h = h0_ref[...]
    hs = []
    for t in range(nt):
        h = at(a, t) * h + at(bt, t)
        hs.append(h)
    merged = (ga2 * ya + gb2 * jnp.concatenate(hs, axis=0)).astype(BF16)
    y_ref[0:nt] = (x + _dot(merged, wout_ref[...])).reshape(nt, ns, d)
    if y_ref.shape[0] > nt:
        y_ref[nt:] = jnp.zeros((y_ref.shape[0] - nt, ns, d), F32)
    h_ref[...] = h
    for k in range(CONV_W - 1):
        conv_ref[k] = xp[nt + k]


def _mixer_sample(x, h0, conv_state, wts, layer, nt, pad_t):
    _, ns, d = x.shape
    names = MIXER_WEIGHTS[:4] + ('wsc', 'bsc') + MIXER_WEIGHTS[6:]
    stacked = (h0, conv_state) + tuple(wts[k] for k in names)
    out_shape = (
        jax.ShapeDtypeStruct((nt + pad_t, ns, d), F32),
        jax.ShapeDtypeStruct((ns, d), F32),
        jax.ShapeDtypeStruct((CONV_W - 1, ns, d), F32),
        jax.ShapeDtypeStruct((nt, ns, d), F32),
    )
    return pl.pallas_call(
        _mixer_sample_kernel,
        grid=(1,),
        in_specs=[_const_spec((nt, ns, d))] + [_layer_spec(w, layer) for w in stacked],
        out_specs=tuple(_const_spec(s.shape) for s in out_shape),
        out_shape=out_shape,
        compiler_params=pltpu.CompilerParams(
            dimension_semantics=("arbitrary",), vmem_limit_bytes=VMEM_LIMIT),
        name="mixer_sample",
    )(x, *stacked)


def _to_slab(ref, val):
    t, d = val.shape
    nlt = d // LANES
    for j in range(nlt):
        ref[pl.ds(j, t, stride=nlt), :] = val[:, j * LANES:(j + 1) * LANES]


def _from_slab(ref, t, nlt):
    return jnp.concatenate([ref[pl.ds(j, t, stride=nlt), :] for j in range(nlt)], axis=-1)


def _tokens_from_slab(slab_ref, nlt):
    return slab_ref[...].reshape(slab_ref.shape[0] // nlt, nlt, LANES).astype(BF16)


def _tokens_to_slab(slab_ref, tokens):
    slab_ref[...] = tokens.astype(F32).reshape(slab_ref.shape)


def _router_kernel(yp_ref, ys_ref, norm_ref, wr_ref, br_ref, tri_ref, low_ref, xn_ref, s1_ref,
                   s2_ref, w1_ref, w2_ref, cnt_ref, slab_s, *, n_prompt_tiles):
    y = jnp.where(pl.program_id(0) < n_prompt_tiles, yp_ref[...], ys_ref[...])
    xn = _rms_norm(y, norm_ref[...])
    _to_slab(slab_s, xn)
    xn_ref[...] = _tokens_from_slab(slab_s, xn_ref.shape[1])
    lt = lax.dot_general(wr_ref[...], xn.astype(BF16), (((1,), (1,)), ((), ())),
                         preferred_element_type=F32) + br_ref[...]
    tile = lt.shape[1]
    row = lax.broadcasted_iota(jnp.int32, (SUBLANES, tile), 0).astype(F32)
    neg = jnp.float32(-jnp.inf)
    big = jnp.float32(SUBLANES)

    def argmax_first(vals):
        m = jnp.max(vals, axis=0, keepdims=True)
        return m, jnp.min(jnp.where(vals == m, row, big), axis=0, keepdims=True)

    def group_block(x, gi):
        out = x[SUBLANES:2 * SUBLANES]
        for g in range(1, N_GROUPS):
            out = jnp.where(gi == g, x[(g + 1) * SUBLANES:(g + 2) * SUBLANES], out)
        return out

    gl = jnp.where(row < N_GROUPS, lt[0:SUBLANES], neg)
    gm, gi = argmax_first(gl)
    pg_top = 1.0 / jnp.sum(jnp.exp(gl - gm), axis=0, keepdims=True)
    el = group_block(lt, gi)
    m1, i1 = argmax_first(el)
    m2, i2 = argmax_first(jnp.where(row == i1, neg, el))
    e2 = jnp.exp(m2 - m1)
    w1 = pg_top / (1.0 + e2)
    w2 = pg_top * e2 / (1.0 + e2)
    hot1 = row == i1
    hot2 = row == i2
    picked = jnp.where(jnp.logical_or(hot1, hot2), 1.0, 0.0)
    zero = jnp.zeros_like(picked)
    onehot = jnp.concatenate(
        [zero] + [jnp.where(gi == g, picked, 0.0) for g in range(N_GROUPS)] + [zero],
        axis=0).astype(BF16)
    before = _dot(onehot, tri_ref[...])
    lower = jnp.sum(_dot(low_ref[...], onehot), axis=1, keepdims=True)
    slot = group_block(before + lower, gi)
    p1 = jnp.sum(jnp.where(hot1, slot, 0.0), axis=0, keepdims=True)
    p2 = jnp.sum(jnp.where(hot2, slot, 0.0), axis=0, keepdims=True)
    s1_ref[0] = p1.astype(jnp.int32)
    s2_ref[0] = p2.astype(jnp.int32)
    w1_ref[0] = w1
    w2_ref[0] = w2
    cnt = jnp.sum(onehot.astype(F32), axis=1, keepdims=True)
    cnt_ref[0] = jnp.broadcast_to(cnt, cnt_ref.shape[1:])


def _token_specs(tile, d, n_prompt_tiles):
    last = n_prompt_tiles - 1
    return (pl.BlockSpec((tile, d), lambda i, *_: (jnp.minimum(i, last), 0)),
            pl.BlockSpec((tile, d), lambda i, *_: (jnp.maximum(i - n_prompt_tiles, 0), 0)))


def _router(yp, ys, wts, layer, tile):
    d = yp.shape[1]
    nlt = d // LANES
    n_prompt_tiles = yp.shape[0] // tile
    n_tiles = n_prompt_tiles + ys.shape[0] // tile
    tri = jnp.triu(jnp.ones((tile, tile), BF16), 1)
    low = jnp.tril(jnp.ones((ROUTE_ROWS, ROUTE_ROWS), BF16), -1)
    per_token = pl.BlockSpec((1, 1, tile), lambda i: (i, 0, 0))
    return pl.pallas_call(
        functools.partial(_router_kernel, n_prompt_tiles=n_prompt_tiles),
        grid=(n_tiles,),
        in_specs=list(_token_specs(tile, d, n_prompt_tiles)) + [
            _layer_spec(wts['norm2'], layer), _layer_spec(wts['w_route'], layer),
            _layer_spec(wts['b_route'], layer), _const_spec((tile, tile)),
            _const_spec((ROUTE_ROWS, ROUTE_ROWS))],
        out_specs=(pl.BlockSpec((tile, nlt, LANES), lambda i: (i, 0, 0)),
                   per_token, per_token, per_token, per_token,
                   pl.BlockSpec((1, ROUTE_ROWS, LANES), lambda i: (i, 0, 0))),
        out_shape=(jax.ShapeDtypeStruct((n_tiles * tile, nlt, LANES), BF16),
                   jax.ShapeDtypeStruct((n_tiles, 1, tile), jnp.int32),
                   jax.ShapeDtypeStruct((n_tiles, 1, tile), jnp.int32),
                   jax.ShapeDtypeStruct((n_tiles, 1, tile), F32),
                   jax.ShapeDtypeStruct((n_tiles, 1, tile), F32),
                   jax.ShapeDtypeStruct((n_tiles, ROUTE_ROWS, LANES), F32)),
        scratch_shapes=[pltpu.VMEM((tile * nlt, LANES), F32)],
        compiler_params=pltpu.CompilerParams(
            dimension_semantics=("arbitrary",), vmem_limit_bytes=VMEM_LIMIT),
        name="moe_router",
    )(yp, ys, wts['norm2'], wts['w_route'], wts['b_route'], tri, low)


def _run_copy(tile_idx, e, lo_ref, cnt_ref, dst_ref, local, remote, sem, to_remote):
    k = tile_idx * N_EXPERTS + e
    loc = local.at[pl.ds(lo_ref[k], cnt_ref[k])]
    rem = remote.at[pl.ds(dst_ref[k], cnt_ref[k])]
    return pltpu.make_async_copy(loc, rem, sem) if to_remote else pltpu.make_async_copy(rem, loc, sem)


def _start_runs(tile_idx, lo_ref, cnt_ref, dst_ref, local, remote, sem, to_remote):
    for e in range(N_EXPERTS):
        @pl.when(cnt_ref[tile_idx * N_EXPERTS + e] > 0)
        def _():
            _run_copy(tile_idx, e, lo_ref, cnt_ref, dst_ref, local, remote, sem, to_remote).start()


def _wait_runs(local, sem):
    pltpu.make_async_copy(local, local, sem).wait()


def _dispatch_kernel(lo_ref, cnt_ref, dst_ref, padst_ref, padn_ref,
                     s1_ref, s2_ref, x_ref, xs_ref, loc, zeros, sem, *, unroll):
    i = pl.program_id(0)
    last = pl.num_programs(0) - 1
    tile = x_ref.shape[0]
    slot = i % 2
    buf = loc.at[slot]

    @pl.when(i >= 2)
    def _():
        _wait_runs(buf, sem.at[slot])

    def body(tt, c):
        for u in range(unroll):
            t = tt * unroll + u
            row = x_ref[t]
            buf[s1_ref[t]] = row
            buf[s2_ref[t]] = row
        return c

    lax.fori_loop(0, tile // unroll, body, 0)
    _start_runs(i, lo_ref, cnt_ref, dst_ref, buf, xs_ref, sem.at[slot], True)

    @pl.when(i == last)
    def _():
        zeros[...] = jnp.zeros_like(zeros)

        def pad_copy(e):
            n = padn_ref[e]
            dst = xs_ref.at[pl.ds(padst_ref[e], n)]
            return pltpu.make_async_copy(zeros.at[pl.ds(0, n)], dst, sem.at[2])

        def each_pad(fn):
            def b(e, c):
                @pl.when(padn_ref[e] > 0)
                def _():
                    fn(e)
                return c
            lax.fori_loop(0, padn_ref.shape[0], b, 0)

        each_pad(lambda e: pad_copy(e).start())
        _wait_runs(buf, sem.at[slot])

        @pl.when(i >= 1)
        def _():
            _wait_runs(loc.at[1 - slot], sem.at[1 - slot])

        each_pad(lambda e: pad_copy(e).wait())


def _smem_tile(tile):
    return pl.BlockSpec((None, None, tile), lambda i, *_: (i, 0, 0), memory_space=pltpu.SMEM)


def _dispatch(xn_slab, s1, s2, tables, tile, n_rows):
    token = xn_slab.shape[1:]
    return pl.pallas_call(
        functools.partial(_dispatch_kernel, unroll=8),
        grid_spec=pltpu.PrefetchScalarGridSpec(
            num_scalar_prefetch=5,
            grid=(xn_slab.shape[0] // tile,),
            in_specs=[_smem_tile(tile), _smem_tile(tile),
                      pl.BlockSpec((tile,) + token, lambda i, *_: (i, 0, 0))],
            out_specs=pl.BlockSpec(memory_space=pl.ANY),
            scratch_shapes=[pltpu.VMEM((2, 2 * tile) + token, xn_slab.dtype),
                            pltpu.VMEM((MOE_BLOCK,) + token, xn_slab.dtype),
                            pltpu.SemaphoreType.DMA((3,))]),
        out_shape=jax.ShapeDtypeStruct((n_rows,) + token, xn_slab.dtype),
        compiler_params=pltpu.CompilerParams(
            dimension_semantics=("arbitrary",), vmem_limit_bytes=VMEM_LIMIT),
        name="moe_dispatch",
    )(*tables, s1, s2, xn_slab)


def _experts_kernel(blk_ref, next_ref, nused_ref, xs_ref, wg_ref, wu_ref, wd_ref, ys_ref,
                    xin, yout, wg_f, wu_f, wd_f, wg_s, wu_s, wd_s, slab_in, slab_out,
                    sem_in, sem_out, sem_w, *, layer):
    n_blocks = ys_ref.shape[0] // MOE_BLOCK
    nlt = xs_ref.shape[1]
    n_used = nused_ref[0]
    block = lambda ref, b: ref.at[pl.ds(pl.multiple_of(b * MOE_BLOCK, MOE_BLOCK), MOE_BLOCK)]
    ring = xin.shape[0]
    rows_in = lambda b: pltpu.make_async_copy(block(xs_ref, b), xin.at[b % ring],
                                              sem_in.at[b % ring])
    rows_out = lambda b, slot: pltpu.make_async_copy(yout.at[slot], block(ys_ref, b),
                                                     sem_out.at[slot])

    def weights(e, slot):
        pairs = ((wg_ref, wg_f), (wu_ref, wu_f), (wd_ref, wd_f))
        return [pltpu.make_async_copy(src.at[layer, e], dst.at[slot], sem_w.at[slot])
                for src, dst in pairs]

    for k in range(ring - 1):
        @pl.when(k < n_used)
        def _():
            rows_in(k).start()

    @pl.when(n_used > 0)
    def _():
        for c in weights(blk_ref[0], 0):
            c.start()

    def step(b, wslot):
        slot = b % 2
        e = blk_ref[b]
        rows_in(b).wait()

        @pl.when(b + ring - 1 < n_used)
        def _():
            rows_in(b + ring - 1).start()

        new_expert = jnp.logical_or(b == 0, e != blk_ref[jnp.maximum(b - 1, 0)])
        wslot = jnp.where(new_expert, 1 - wslot, wslot)

        @pl.when(new_expert)
        def _():
            for c in weights(e, wslot):
                c.wait()
            wg_s[...] = wg_f[wslot].astype(BF16)
            wu_s[...] = wu_f[wslot].astype(BF16)
            wd_s[...] = wd_f[wslot].astype(BF16)

            @pl.when(next_ref[e] >= 0)
            def _():
                for c in weights(next_ref[e], 1 - wslot):
                    c.start()

        _tokens_to_slab(slab_in, xin[b % ring])
        x = _from_slab(slab_in, MOE_BLOCK, nlt).astype(BF16)
        hg = _dot(x, wg_s[...])
        hu = _dot(x, wu_s[...])
        h = (hg * _sigmoid(hg) * hu).astype(BF16)
        _to_slab(slab_out, _dot(h, wd_s[...]))

        @pl.when(b >= 2)
        def _():
            rows_out(b - 2, slot).wait()

        yout[slot] = _tokens_from_slab(slab_out, nlt)
        rows_out(b, slot).start()
        return wslot

    lax.fori_loop(0, n_used, step, jnp.int32(1))

    @pl.when(n_used >= 2)
    def _():
        rows_out(n_used - 2, n_used % 2).wait()

    @pl.when(n_used >= 1)
    def _():
        rows_out(n_used - 1, (n_used - 1) % 2).wait()

    yout[0] = jnp.zeros(yout.shape[1:], yout.dtype)

    def fill(b, c):
        rows_out(b, 0).start()
        rows_out(b, 0).wait()
        return c

    lax.fori_loop(n_used, n_blocks, fill, 0)


def _experts(xs, blk_expert, next_expert, n_used, wts, layer):
    n_rows, nlt, _ = xs.shape
    d = nlt * LANES
    f = wts['w_gate'].shape[-1]
    token = (nlt, LANES)
    any_spec = pl.BlockSpec(memory_space=pl.ANY)
    return pl.pallas_call(
        functools.partial(_experts_kernel, layer=layer),
        grid_spec=pltpu.PrefetchScalarGridSpec(
            num_scalar_prefetch=3,
            grid=(1,),
            in_specs=[any_spec] * 4,
            out_specs=any_spec,
            scratch_shapes=[pltpu.VMEM((3, MOE_BLOCK) + token, xs.dtype),
                            pltpu.VMEM((2, MOE_BLOCK) + token, xs.dtype),
                            pltpu.VMEM((2, d, f), F32), pltpu.VMEM((2, d, f), F32),
                            pltpu.VMEM((2, f, d), F32),
                            pltpu.VMEM((d, f), BF16), pltpu.VMEM((d, f), BF16),
                            pltpu.VMEM((f, d), BF16),
                            pltpu.VMEM((MOE_BLOCK * nlt, LANES), F32),
                            pltpu.VMEM((MOE_BLOCK * nlt, LANES), F32),
                            pltpu.SemaphoreType.DMA((3,)), pltpu.SemaphoreType.DMA((2,)),
                            pltpu.SemaphoreType.DMA((2,))]),
        out_shape=jax.ShapeDtypeStruct(xs.shape, xs.dtype),
        compiler_params=pltpu.CompilerParams(
            dimension_semantics=("arbitrary",), vmem_limit_bytes=VMEM_LIMIT),
        name="moe_experts",
    )(blk_expert, next_expert, n_used, xs, wts['w_gate'], wts['w_up'], wts['w_down'])


def _combine_kernel(lo_ref, cnt_ref, dst_ref, s1_ref, s2_ref, w1_ref, w2_ref, yp_ref, ys_ref,
                    nf_ref, rows_ref, op_ref, os_ref, loc, acc, sem,
                    *, unroll, final_norm, n_prompt_tiles):
    i = pl.program_id(0)
    tile = yp_ref.shape[0]
    nlt = loc.shape[2]
    slot = i % 2
    buf = loc.at[slot]
    fetch = lambda t, s: _start_runs(t, lo_ref, cnt_ref, dst_ref, loc.at[s], rows_ref, sem.at[s],
                                     False)

    @pl.when(i == 0)
    def _():
        fetch(0, 0)

    @pl.when(i + 1 < pl.num_programs(0))
    def _():
        fetch(i + 1, 1 - slot)

    _wait_runs(buf, sem.at[slot])

    def body(tt, c):
        for u in range(unroll):
            t = tt * unroll + u
            r1 = buf[s1_ref[t]].astype(F32)
            r2 = buf[s2_ref[t]].astype(F32)
            acc[pl.ds(pl.multiple_of(t * nlt, nlt), nlt), :] = w1_ref[t] * r1 + w2_ref[t] * r2
        return c

    lax.fori_loop(0, tile // unroll, body, 0)
    def finish(y_ref, o_ref):
        out = y_ref[...] + _from_slab(acc, tile, nlt)
        if final_norm:
            out = _rms_norm(out, nf_ref[...])
        o_ref[...] = out

    is_prompt = i < n_prompt_tiles
    pl.when(is_prompt)(lambda: finish(yp_ref, op_ref))
    pl.when(jnp.logical_not(is_prompt))(lambda: finish(ys_ref, os_ref))


def _combine(yp, ys, rows, meta, tables, norm_f, final_norm, tile):
    d = yp.shape[1]
    token = rows.shape[1:]
    n_prompt_tiles = yp.shape[0] // tile
    n_tiles = n_prompt_tiles + ys.shape[0] // tile
    tok_p, tok_s = _token_specs(tile, d, n_prompt_tiles)
    return pl.pallas_call(
        functools.partial(_combine_kernel, unroll=8, final_norm=final_norm,
                          n_prompt_tiles=n_prompt_tiles),
        grid_spec=pltpu.PrefetchScalarGridSpec(
            num_scalar_prefetch=3,
            grid=(n_tiles,),
            in_specs=[_smem_tile(tile)] * 4 + [tok_p, tok_s, _const_spec((1, d)),
                                               pl.BlockSpec(memory_space=pl.ANY)],
            out_specs=(tok_p, tok_s),
            scratch_shapes=[pltpu.VMEM((2, 2 * tile) + token, rows.dtype),
                            pltpu.VMEM((tile * token[0], LANES), F32),
                            pltpu.SemaphoreType.DMA((2,))]),
        out_shape=(jax.ShapeDtypeStruct(yp.shape, F32), jax.ShapeDtypeStruct(ys.shape, F32)),
        compiler_params=pltpu.CompilerParams(
            dimension_semantics=("arbitrary",), vmem_limit_bytes=VMEM_LIMIT),
        name="moe_combine",
    )(*tables, *meta, yp, ys, norm_f, rows)


def _moe(yp, ys, wts, layer, norm_f, final_norm, tile):
    d = yp.shape[1]
    n = yp.shape[0] + ys.shape[0]
    assert yp.shape[0] % tile == 0 and ys.shape[0] % tile == 0 and d % LANES == 0
    xn_slab, s1, s2, w1, w2, cnt = _router(yp, ys, wts, layer, tile)
    cnt = cnt[:, SUBLANES:SUBLANES + N_EXPERTS, 0].astype(jnp.int32)
    lo = jnp.cumsum(cnt, axis=1) - cnt
    seg = jnp.sum(cnt, axis=0)
    seg_pad = (seg + MOE_BLOCK - 1) // MOE_BLOCK * MOE_BLOCK
    seg_end = jnp.cumsum(seg_pad)
    seg_start = seg_end - seg_pad
    dst = seg_start[None, :] + jnp.cumsum(cnt, axis=0) - cnt
    n_blocks = (2 * n + N_EXPERTS * (MOE_BLOCK - 1)) // MOE_BLOCK
    blk_ids = jnp.arange(n_blocks, dtype=jnp.int32)
    blk_expert = jnp.minimum(
        jnp.sum((seg_end[None, :] <= blk_ids[:, None] * MOE_BLOCK).astype(jnp.int32), axis=1),
        N_EXPERTS - 1)
    n_used = seg_end[-1:] // MOE_BLOCK
    pad_start = jnp.concatenate([seg_start + seg, blk_ids * MOE_BLOCK])
    pad_rows = jnp.concatenate([seg_pad - seg, jnp.where(blk_ids >= n_used[0], MOE_BLOCK, 0)])
    flat = lambda a: a.reshape(-1)
    runs = (flat(lo), flat(cnt), flat(dst))
    ids = jnp.arange(N_EXPERTS, dtype=jnp.int32)
    later = jnp.where((seg_pad[None, :] > 0) & (ids[None, :] > ids[:, None]), ids[None, :],
                      N_EXPERTS)
    next_expert = jnp.min(later, axis=1)
    next_expert = jnp.where(next_expert < N_EXPERTS, next_expert, -1)
    xs = _dispatch(xn_slab, s1, s2, runs + (pad_start, pad_rows), tile, n_blocks * MOE_BLOCK)
    rows = _experts(xs, blk_expert, next_expert, n_used, wts, layer)
    return _combine(yp, ys, rows, (s1, s2, w1, w2), runs, norm_f, final_norm, tile)


def _prep_weights(norm1, w_in, ln_g, ln_b, w_s, b_s, conv_w, conv_b, lru_lambda, w_rg_a, b_rg_a,
                  w_rg_x, b_rg_x, w_out, norm2, w_route_group, b_route_group, w_route_expert,
                  b_route_expert, w_gate, w_up, w_down, n_sample_t):
    depth, d = w_in.shape[0], w_in.shape[1]
    dh = d // H_A
    row = lambda a: a.reshape(depth, 1, -1).astype(F32)
    causal = jnp.tril(jnp.ones((CHUNK, CHUNK), dtype=bool))
    ws = jnp.where(causal, 0.5 * w_s, 0.0)
    bs = jnp.repeat(jnp.swapaxes(0.5 * b_s, 1, 2), dh, axis=2)
    in_scale = jnp.concatenate([jnp.ones((3 * d,), F32), jnp.full((2 * d,), 0.5, F32)])

    def pair_bd(w):
        w = w.reshape(depth, H_B // 2, 2, w.shape[-2], w.shape[-1])
        z = jnp.zeros_like(w[:, :, 0])
        return jnp.concatenate([jnp.concatenate([w[:, :, 0], z], axis=-1),
                                jnp.concatenate([z, w[:, :, 1]], axis=-1)], axis=-2)

    gap = SUBLANES - N_GROUPS
    tail = ROUTE_ROWS - SUBLANES - N_EXPERTS
    w_route = jnp.concatenate(
        [jnp.swapaxes(w_route_group, 1, 2), jnp.zeros((depth, gap, d), F32),
         jnp.swapaxes(w_route_expert, 1, 2), jnp.zeros((depth, tail, d), F32)], axis=1)
    b_route = jnp.concatenate(
        [b_route_group, jnp.zeros((depth, gap), F32), b_route_expert,
         jnp.zeros((depth, tail), F32)], axis=1)[..., None]
    nt = n_sample_t
    return dict(
        norm1=row(norm1), w_in=(w_in * in_scale).astype(BF16), ln_g=row(ln_g), ln_b=row(ln_b),
        ws=ws.astype(BF16), bs=bs.astype(F32), conv_w=conv_w.astype(F32), conv_b=row(conv_b),
        lam=row(lru_lambda),
        wbd=(0.5 * jnp.concatenate([pair_bd(w_rg_a), pair_bd(w_rg_x)], axis=-1)).astype(BF16),
        b_a=row(0.5 * b_rg_a), b_x=row(0.5 * b_rg_x), w_out=(0.5 * w_out).astype(BF16),
        norm2=row(norm2), w_route=w_route.astype(BF16), b_route=b_route.astype(F32),
        w_gate=w_gate, w_up=w_up, w_down=w_down,
        wsc=jnp.repeat(jnp.transpose(ws[:, :, :nt, :nt], (0, 2, 3, 1)), dh, axis=3).astype(F32),
        bsc=bs[:, :nt].astype(F32),
    )


def kernel(x_prompt, x_sample, state_lru_h, state_conv, norm1, w_in, ln_g, ln_b, w_s, b_s, conv_w, conv_b, lru_lambda, w_rg_a, b_rg_a, w_rg_x, b_rg_x, w_out, norm2, w_route_group, b_route_group, w_route_expert, b_route_expert, w_gate, w_up, w_down, norm_f):
    depth = w_in.shape[0]
    nb, seq, d = x_prompt.shape
    ns, nt, _ = x_sample.shape
    assert nt <= CHUNK
    nf = norm_f.reshape(1, d).astype(F32)
    wts = _prep_weights(norm1, w_in, ln_g, ln_b, w_s, b_s, conv_w, conv_b, lru_lambda, w_rg_a,
                        b_rg_a, w_rg_x, b_rg_x, w_out, norm2, w_route_group, b_route_group,
                        w_route_expert, b_route_expert, w_gate, w_up, w_down, nt)
    conv_state = jnp.transpose(state_conv, (0, 2, 1, 3))
    assert (nb * seq) % MOE_TILE == 0
    sample_pad = (-nt * ns) % MOE_TILE
    assert sample_pad % ns == 0
    pad_t = sample_pad // ns
    yp = x_prompt
    ys = jnp.transpose(x_sample, (1, 0, 2))
    hp, cp, vp, hs, cs, vs = [], [], [], [], [], []
    for l in range(depth):
        yp, h, c, v = _mixer_prompt(yp, wts, l)
        hp.append(h)
        cp.append(c)
        vp.append(v)
        ys, h, c, v = _mixer_sample(ys, state_lru_h, conv_state, wts, l, nt, pad_t)
        hs.append(h)
        cs.append(c)
        vs.append(v)
        yp, ys = _moe(yp.reshape(nb * seq, d), ys.reshape((nt + pad_t) * ns, d), wts, l, nf,
                      l == depth - 1, MOE_TILE)
        yp = yp.reshape(nb, seq, d)
        ys = ys.reshape(nt + pad_t, ns, d)
    to_seq_major = lambda a: jnp.transpose(jnp.stack(a), (0, 2, 1, 3))
    return (yp, jnp.transpose(ys[:nt], (1, 0, 2)), jnp.stack(hp), jnp.stack(cp), jnp.stack(vp),
            jnp.stack(hs), to_seq_major(cs), to_seq_major(vs))
```

```python
import functools

import jax
import jax.numpy as jnp
from jax import lax
from jax.experimental import pallas as pl
from jax.experimental.pallas import tpu as pltpu

F32 = jnp.float32
BF16 = jnp.bfloat16

CHUNK = 128
H_A = 4
H_B = 8
CONV_W = 4
LRU_C = 8.0
N_GROUPS = 4
E_PER_GROUP = 8
N_EXPERTS = N_GROUPS * E_PER_GROUP
EPS = 1e-6

LANES = 128
SUBLANES = 8
BF16_ROWS = 16
V7X_VMEM_BYTES = 64 * 1024 * 1024
VMEM_LIMIT = V7X_VMEM_BYTES - 4 * 1024 * 1024
SCAN_UNROLL = 4
SEQS_PER_TRIP = 8
MOE_TILE = 1024
MOE_BLOCK = 512
EXPERT_ROW_RING = 4
EXPERT_WEIGHT_RING = 3
assert E_PER_GROUP == SUBLANES and N_GROUPS <= SUBLANES
ROUTE_ROWS = -(-SUBLANES * (N_GROUPS + 1) // BF16_ROWS) * BF16_ROWS


def _dot(a, b):
    return jnp.dot(a, b, preferred_element_type=F32)


def _sigmoid(x):
    return 0.5 * (jnp.tanh(0.5 * x) + 1.0)


def _gelu2(x):
    c = 0.7978845608028654
    return x * (1.0 + jnp.tanh(x * (c + (c * 0.044715) * (x * x))))


def _rms_norm(x, g):
    return x * lax.rsqrt(jnp.mean(x * x, axis=-1, keepdims=True) + EPS) * g


def _layer_norm(x, g, b, eps):
    xc = x - jnp.mean(x, axis=-1, keepdims=True)
    var = jnp.mean(xc * xc, axis=-1, keepdims=True)
    return xc * lax.rsqrt(var + eps) * g + b


def _softplus(x):
    return jnp.maximum(x, 0.0) + jnp.log(1.0 + jnp.exp(-jnp.abs(x)))


def _lru_gates(xc, wbd_ref, b_a, b_x, lam_half):
    d = xc.shape[-1]
    xcb = xc.astype(BF16)
    r_parts, i_parts = [], []
    for q in range(d // 256):
        ri = _dot(xcb[:, q * 256:(q + 1) * 256], wbd_ref[q])
        r_parts.append(ri[:, :256])
        i_parts.append(ri[:, 256:])
    r2 = 1.0 + jnp.tanh(jnp.concatenate(r_parts, axis=-1) + b_a)
    i2 = 1.0 + jnp.tanh(jnp.concatenate(i_parts, axis=-1) + b_x)
    a = jnp.exp(lam_half * r2)
    mult_half = jnp.sqrt(0.25 - 0.25 * (a * a))
    return a, mult_half, i2 * xc


def _shift_rows(x, hist, j):
    sh = pltpu.roll(x, j, axis=0)
    top = jnp.where(lax.broadcasted_iota(jnp.int32, (SUBLANES, 1), 0) < j,
                    pltpu.roll(hist, j, axis=0), sh[:SUBLANES])
    return jnp.concatenate([top, sh[SUBLANES:]], axis=0)


def _mixer_prompt_kernel(x_ref, norm_ref, win_ref, lng_ref, lnb_ref, ws_ref, bs_ref,
                         cw_ref, cb_ref, lam_ref, wbd_ref, ba_ref, bx_ref, wout_ref,
                         y_ref, h_ref, conv_ref, v_ref,
                         hist_s, a_s, b_s, ya_s, gb_s, hc_s):
    i = pl.program_id(0)
    last = pl.num_programs(0) - 1
    nseq, _, d = x_ref.shape
    nlt = d // LANES
    spt = SEQS_PER_TRIP
    pair = spt * CHUNK

    @pl.when(i == 0)
    def _():
        hist_s[...] = jnp.zeros_like(hist_s)
        hc_s[...] = jnp.zeros_like(hc_s)

    lam_half = (-0.5 * LRU_C) * _softplus(-lam_ref[...])
    is_first = i == 0

    def branch_body(p, c):
        row = pl.multiple_of(p * pair, pair)
        x = jnp.concatenate([x_ref[spt * p + k] for k in range(spt)], axis=0)
        xnb = _rms_norm(x, norm_ref[...]).astype(BF16)
        v = _layer_norm(_gelu2(_dot(xnb, win_ref[:, d:2 * d])), lng_ref[...], lnb_ref[...],
                        4.0 * EPS)

        @pl.when(i == last)
        def _():
            for k in range(spt):
                v_ref[spt * p + k] = v[k * CHUNK:(k + 1) * CHUNK]

        vb = v.astype(BF16)
        dh = d // H_A
        z_rows = []
        for k in range(spt):
            zs = [_dot(ws_ref[h], vb[k * CHUNK:(k + 1) * CHUNK, h * dh:(h + 1) * dh])
                  for h in range(H_A)]
            z_rows.append(jnp.concatenate(zs, axis=-1) + bs_ref[...])
        z = jnp.concatenate(z_rows, axis=0)
        ya = _gelu2(_dot(xnb, win_ref[:, 0:d])) * z
        ga2 = 1.0 + jnp.tanh(_dot(xnb, win_ref[:, 3 * d:4 * d]))
        ya_s[pl.ds(row, pair), :] = (ga2 * ya).astype(BF16)
        gb_s[pl.ds(row, pair), :] = (1.0 + jnp.tanh(_dot(xnb, win_ref[:, 4 * d:5 * d]))).astype(BF16)
        xr = _dot(xnb, win_ref[:, 2 * d:3 * d])
        xcs = []
        for k in range(spt):
            s = spt * p + k
            xk = xr[k * CHUNK:(k + 1) * CHUNK]
            hist = hist_s[s]
            xc = cb_ref[...] + xk * cw_ref[CONV_W - 1:CONV_W, :]
            for j in range(1, CONV_W):
                xc = xc + _shift_rows(xk, hist, j) * cw_ref[CONV_W - 1 - j:CONV_W - j, :]
            hist_s[s] = xk[CHUNK - SUBLANES:]
            xcs.append(xc)
        xc = jnp.concatenate(xcs, axis=0)
        a, mult_half, ix2 = _lru_gates(xc, wbd_ref, ba_ref[...], bx_ref[...], lam_half)
        bt = mult_half * ix2
        for k in range(spt):
            s = spt * p + k
            for j in range(nlt):
                a_s[j, pl.ds(s, CHUNK, stride=nseq), :] = a[k * CHUNK:(k + 1) * CHUNK, j * LANES:(j + 1) * LANES]
                b_s[j, pl.ds(s, CHUNK, stride=nseq), :] = bt[k * CHUNK:(k + 1) * CHUNK, j * LANES:(j + 1) * LANES]

        @pl.when(is_first)
        def _():
            for k in range(spt):
                first = 0.5 * ix2[k * CHUNK:k * CHUNK + 1]
                for j in range(nlt):
                    b_s[j, pl.ds(spt * p + k, 1), :] = first[:, j * LANES:(j + 1) * LANES]

        return c

    lax.fori_loop(0, nseq // spt, branch_body, 0)

    def scan_body(tt, hs):
        for u in range(SCAN_UNROLL):
            row = pl.multiple_of((tt * SCAN_UNROLL + u) * nseq, nseq)
            out = []
            for j in range(nlt):
                h = a_s[j, pl.ds(row, nseq), :] * hs[j] + b_s[j, pl.ds(row, nseq), :]
                b_s[j, pl.ds(row, nseq), :] = h
                out.append(h)
            hs = tuple(out)
        return hs

    h0 = tuple(hc_s[:, j * LANES:(j + 1) * LANES] for j in range(nlt))
    hs = lax.fori_loop(0, CHUNK // SCAN_UNROLL, scan_body, h0)
    hfin = jnp.concatenate(hs, axis=-1)
    hc_s[...] = hfin
    h_ref[...] = hfin

    def out_body(p, c):
        row = pl.multiple_of(p * pair, pair)
        hrows = []
        for k in range(spt):
            hrows.append(jnp.concatenate(
                [b_s[j, pl.ds(spt * p + k, CHUNK, stride=nseq), :] for j in range(nlt)], axis=-1))
        h = jnp.concatenate(hrows, axis=0)
        merged = (ya_s[pl.ds(row, pair), :].astype(F32)
                  + gb_s[pl.ds(row, pair), :].astype(F32) * h).astype(BF16)
        o = _dot(merged, wout_ref[...])
        for k in range(spt):
            y_ref[spt * p + k] = x_ref[spt * p + k] + o[k * CHUNK:(k + 1) * CHUNK]
        return c

    lax.fori_loop(0, nseq // spt, out_body, 0)

    @pl.when(i == last)
    def _():
        conv_ref[...] = hist_s[:, SUBLANES - (CONV_W - 1):, :]


def _const_spec(shape):
    nd = len(shape)
    return pl.BlockSpec(shape, lambda i, *_, _n=nd: (0,) * _n, pipeline_mode=pl.Buffered(1))


def _layer_spec(stacked, layer):
    shape = stacked.shape[1:]
    nd = len(shape)
    return pl.BlockSpec((None,) + tuple(shape), lambda i, *_, _n=nd: (layer,) + (0,) * _n,
                        pipeline_mode=pl.Buffered(1))


MIXER_WEIGHTS = ('norm1', 'w_in', 'ln_g', 'ln_b', 'ws', 'bs', 'conv_w', 'conv_b', 'lam', 'wbd',
                 'b_a', 'b_x', 'w_out')


def _mixer_prompt(x, wts, layer):
    nseq, seq, d = x.shape
    assert seq % CHUNK == 0 and nseq == SUBLANES and d % 256 == 0
    n_chunks = seq // CHUNK
    nlt = d // LANES
    weights = tuple(wts[k] for k in MIXER_WEIGHTS)
    x_spec = pl.BlockSpec((nseq, CHUNK, d), lambda i: (0, i, 0))
    out_shape = (
        jax.ShapeDtypeStruct((nseq, seq, d), F32),
        jax.ShapeDtypeStruct((nseq, d), F32),
        jax.ShapeDtypeStruct((nseq, CONV_W - 1, d), F32),
        jax.ShapeDtypeStruct((nseq, CHUNK, d), F32),
    )
    out_specs = (
        x_spec,
        _const_spec((nseq, d)),
        _const_spec((nseq, CONV_W - 1, d)),
        _const_spec((nseq, CHUNK, d)),
    )
    scratch = [
        pltpu.VMEM((nseq, SUBLANES, d), F32),
        pltpu.VMEM((nlt, CHUNK * nseq, LANES), F32),
        pltpu.VMEM((nlt, CHUNK * nseq, LANES), F32),
        pltpu.VMEM((nseq * CHUNK, d), BF16),
        pltpu.VMEM((nseq * CHUNK, d), BF16),
        pltpu.VMEM((nseq, d), F32),
    ]
    return pl.pallas_call(
        _mixer_prompt_kernel,
        grid=(n_chunks,),
        in_specs=[x_spec] + [_layer_spec(w, layer) for w in weights],
        out_specs=out_specs,
        out_shape=out_shape,
        scratch_shapes=scratch,
        compiler_params=pltpu.CompilerParams(
            dimension_semantics=("arbitrary",), vmem_limit_bytes=VMEM_LIMIT),
        name="mixer_prompt",
    )(x, *weights)


def _mixer_sample_kernel(x_ref, h0_ref, cst_ref, norm_ref, win_ref, lng_ref, lnb_ref, wsc_ref,
                         bsc_ref, cw_ref, cb_ref, lam_ref, wbd_ref, ba_ref, bx_ref, wout_ref,
                         y_ref, h_ref, conv_ref, v_ref):
    nt, ns, d = x_ref.shape
    at = lambda a, t: a[t * ns:(t + 1) * ns]
    lam_half = (-0.5 * LRU_C) * _softplus(-lam_ref[...])
    x = x_ref[...].reshape(nt * ns, d)
    xnb = _rms_norm(x, norm_ref[...]).astype(BF16)
    v = _layer_norm(_gelu2(_dot(xnb, win_ref[:, d:2 * d])), lng_ref[...], lnb_ref[...], 4.0 * EPS)
    v_ref[...] = v.reshape(nt, ns, d)
    zs = []
    for t in range(nt):
        z = bsc_ref[t:t + 1, :]
        for s in range(t + 1):
            z = z + wsc_ref[t, s:s + 1, :] * at(v, s)
        zs.append(z)
    ya = _gelu2(_dot(xnb, win_ref[:, 0:d])) * jnp.concatenate(zs, axis=0)
    ga2 = 1.0 + jnp.tanh(_dot(xnb, win_ref[:, 3 * d:4 * d]))
    gb2 = 1.0 + jnp.tanh(_dot(xnb, win_ref[:, 4 * d:5 * d]))
    xr = _dot(xnb, win_ref[:, 2 * d:3 * d])
    xp = [cst_ref[k] for k in range(CONV_W - 1)] + [at(xr, t) for t in range(nt)]
    xcs = []
    for t in range(nt):
        xc = cb_ref[...]
        for k in range(CONV_W):
            xc = xc + xp[t + k] * cw_ref[k:k + 1, :]
        xcs.append(xc)
    a, mult_half, ix2 = _lru_gates(jnp.concatenate(xcs, axis=0), wbd_ref, ba_ref[...],
                                   bx_ref[...], lam_half)
    bt = mult_half * ix2
    h = h0_ref[...]
    hs = []
    for t in range(nt):
        h = at(a, t) * h + at(bt, t)
        hs.append(h)
    merged = (ga2 * ya + gb2 * jnp.concatenate(hs, axis=0)).astype(BF16)
    y_ref[0:nt] = (x + _dot(merged, wout_ref[...])).reshape(nt, ns, d)
    if y_ref.shape[0] > nt:
        y_ref[nt:] = jnp.zeros((y_ref.shape[0] - nt, ns, d), F32)
    h_ref[...] = h
    for k in range(CONV_W - 1):
        conv_ref[k] = xp[nt + k]


def _mixer_sample(x, h0, conv_state, wts, layer, nt, pad_t):
    _, ns, d = x.shape
    names = MIXER_WEIGHTS[:4] + ('wsc', 'bsc') + MIXER_WEIGHTS[6:]
    stacked = (h0, conv_state) + tuple(wts[k] for k in names)
    out_shape = (
        jax.ShapeDtypeStruct((nt + pad_t, ns, d), F32),
        jax.ShapeDtypeStruct((ns, d), F32),
        jax.ShapeDtypeStruct((CONV_W - 1, ns, d), F32),
        jax.ShapeDtypeStruct((nt, ns, d), F32),
    )
    return pl.pallas_call(
        _mixer_sample_kernel,
        grid=(1,),
        in_specs=[_const_spec((nt, ns, d))] + [_layer_spec(w, layer) for w in stacked],
        out_specs=tuple(_const_spec(s.shape) for s in out_shape),
        out_shape=out_shape,
        compiler_params=pltpu.CompilerParams(
            dimension_semantics=("arbitrary",), vmem_limit_bytes=VMEM_LIMIT),
        name="mixer_sample",
    )(x, *stacked)


def _to_slab(ref, val):
    t, d = val.shape
    nlt = d // LANES
    for j in range(nlt):
        ref[pl.ds(j, t, stride=nlt), :] = val[:, j * LANES:(j + 1) * LANES]


def _from_slab(ref, t, nlt):
    return jnp.concatenate([ref[pl.ds(j, t, stride=nlt), :] for j in range(nlt)], axis=-1)


def _tokens_from_slab(slab_ref, nlt):
    return slab_ref[...].reshape(slab_ref.shape[0] // nlt, nlt, LANES).astype(BF16)


def _tokens_to_slab(slab_ref, tokens):
    slab_ref[...] = tokens.astype(F32).reshape(slab_ref.shape)


def _router_kernel(yp_ref, ys_ref, norm_ref, wr_ref, br_ref, tri_ref, low_ref, xn_ref, s1_ref,
                   s2_ref, w1_ref, w2_ref, cnt_ref, slab_s, *, n_prompt_tiles):
    y = jnp.where(pl.program_id(0) < n_prompt_tiles, yp_ref[...], ys_ref[...])
    xn = _rms_norm(y, norm_ref[...])
    _to_slab(slab_s, xn)
    xn_ref[...] = _tokens_from_slab(slab_s, xn_ref.shape[1])
    lt = lax.dot_general(wr_ref[...], xn.astype(BF16), (((1,), (1,)), ((), ())),
                         preferred_element_type=F32) + br_ref[...]
    tile = lt.shape[1]
    row = lax.broadcasted_iota(jnp.int32, (SUBLANES, tile), 0).astype(F32)
    neg = jnp.float32(-jnp.inf)
    big = jnp.float32(SUBLANES)

    def argmax_first(vals):
        m = jnp.max(vals, axis=0, keepdims=True)
        return m, jnp.min(jnp.where(vals == m, row, big), axis=0, keepdims=True)

    def group_block(x, gi):
        out = x[SUBLANES:2 * SUBLANES]
        for g in range(1, N_GROUPS):
            out = jnp.where(gi == g, x[(g + 1) * SUBLANES:(g + 2) * SUBLANES], out)
        return out

    gl = jnp.where(row < N_GROUPS, lt[0:SUBLANES], neg)
    gm, gi = argmax_first(gl)
    pg_top = 1.0 / jnp.sum(jnp.exp(gl - gm), axis=0, keepdims=True)
    el = group_block(lt, gi)
    m1, i1 = argmax_first(el)
    m2, i2 = argmax_first(jnp.where(row == i1, neg, el))
    e2 = jnp.exp(m2 - m1)
    w1 = pg_top / (1.0 + e2)
    w2 = pg_top * e2 / (1.0 + e2)
    hot1 = row == i1
    hot2 = row == i2
    picked = jnp.where(jnp.logical_or(hot1, hot2), 1.0, 0.0)
    zero = jnp.zeros_like(picked)
    onehot = jnp.concatenate(
        [zero] + [jnp.where(gi == g, picked, 0.0) for g in range(N_GROUPS)] + [zero],
        axis=0).astype(BF16)
    before = _dot(onehot, tri_ref[...])
    lower = jnp.sum(_dot(low_ref[...], onehot), axis=1, keepdims=True)
    slot = group_block(before + lower, gi)
    p1 = jnp.sum(jnp.where(hot1, slot, 0.0), axis=0, keepdims=True)
    p2 = jnp.sum(jnp.where(hot2, slot, 0.0), axis=0, keepdims=True)
    s1_ref[0] = p1.astype(jnp.int32)
    s2_ref[0] = p2.astype(jnp.int32)
    w1_ref[0] = w1
    w2_ref[0] = w2
    cnt = jnp.sum(onehot.astype(F32), axis=1, keepdims=True)
    cnt_ref[0] = jnp.broadcast_to(cnt, cnt_ref.shape[1:])


def _token_specs(tile, d, n_prompt_tiles):
    last = n_prompt_tiles - 1
    return (pl.BlockSpec((tile, d), lambda i, *_: (jnp.minimum(i, last), 0)),
            pl.BlockSpec((tile, d), lambda i, *_: (jnp.maximum(i - n_prompt_tiles, 0), 0)))


def _router(yp, ys, wts, layer, tile):
    d = yp.shape[1]
    nlt = d // LANES
    n_prompt_tiles = yp.shape[0] // tile
    n_tiles = n_prompt_tiles + ys.shape[0] // tile
    tri = jnp.triu(jnp.ones((tile, tile), BF16), 1)
    low = jnp.tril(jnp.ones((ROUTE_ROWS, ROUTE_ROWS), BF16), -1)
    per_token = pl.BlockSpec((1, 1, tile), lambda i: (i, 0, 0))
    return pl.pallas_call(
        functools.partial(_router_kernel, n_prompt_tiles=n_prompt_tiles),
        grid=(n_tiles,),
        in_specs=list(_token_specs(tile, d, n_prompt_tiles)) + [
            _layer_spec(wts['norm2'], layer), _layer_spec(wts['w_route'], layer),
            _layer_spec(wts['b_route'], layer), _const_spec((tile, tile)),
            _const_spec((ROUTE_ROWS, ROUTE_ROWS))],
        out_specs=(pl.BlockSpec((tile, nlt, LANES), lambda i: (i, 0, 0)),
                   per_token, per_token, per_token, per_token,
                   pl.BlockSpec((1, ROUTE_ROWS, LANES), lambda i: (i, 0, 0))),
        out_shape=(jax.ShapeDtypeStruct((n_tiles * tile, nlt, LANES), BF16),
                   jax.ShapeDtypeStruct((n_tiles, 1, tile), jnp.int32),
                   jax.ShapeDtypeStruct((n_tiles, 1, tile), jnp.int32),
                   jax.ShapeDtypeStruct((n_tiles, 1, tile), F32),
                   jax.ShapeDtypeStruct((n_tiles, 1, tile), F32),
                   jax.ShapeDtypeStruct((n_tiles, ROUTE_ROWS, LANES), F32)),
        scratch_shapes=[pltpu.VMEM((tile * nlt, LANES), F32)],
        compiler_params=pltpu.CompilerParams(
            dimension_semantics=("arbitrary",), vmem_limit_bytes=VMEM_LIMIT),
        name="moe_router",
    )(yp, ys, wts['norm2'], wts['w_route'], wts['b_route'], tri, low)


def _run_copy(tile_idx, e, lo_ref, cnt_ref, dst_ref, local, remote, sem, to_remote):
    k = tile_idx * N_EXPERTS + e
    loc = local.at[pl.ds(lo_ref[k], cnt_ref[k])]
    rem = remote.at[pl.ds(dst_ref[k], cnt_ref[k])]
    return pltpu.make_async_copy(loc, rem, sem) if to_remote else pltpu.make_async_copy(rem, loc, sem)


def _start_runs(tile_idx, lo_ref, cnt_ref, dst_ref, local, remote, sem, to_remote):
    for e in range(N_EXPERTS):
        @pl.when(cnt_ref[tile_idx * N_EXPERTS + e] > 0)
        def _():
            _run_copy(tile_idx, e, lo_ref, cnt_ref, dst_ref, local, remote, sem, to_remote).start()


def _wait_runs(local, sem):
    pltpu.make_async_copy(local, local, sem).wait()


def _dispatch_kernel(lo_ref, cnt_ref, dst_ref, padst_ref, padn_ref,
                     s1_ref, s2_ref, x_ref, xs_ref, loc, zeros, sem, *, unroll):
    i = pl.program_id(0)
    last = pl.num_programs(0) - 1
    tile = x_ref.shape[0]
    slot = i % 2
    buf = loc.at[slot]

    @pl.when(i >= 2)
    def _():
        _wait_runs(buf, sem.at[slot])

    def body(tt, c):
        for u in range(unroll):
            t = tt * unroll + u
            row = x_ref[t]
            buf[s1_ref[t]] = row
            buf[s2_ref[t]] = row
        return c

    lax.fori_loop(0, tile // unroll, body, 0)
    _start_runs(i, lo_ref, cnt_ref, dst_ref, buf, xs_ref, sem.at[slot], True)

    @pl.when(i == last)
    def _():
        zeros[...] = jnp.zeros_like(zeros)

        def pad_copy(e):
            n = padn_ref[e]
            dst = xs_ref.at[pl.ds(padst_ref[e], n)]
            return pltpu.make_async_copy(zeros.at[pl.ds(0, n)], dst, sem.at[2])

        def each_pad(fn):
            def b(e, c):
                @pl.when(padn_ref[e] > 0)
                def _():
                    fn(e)
                return c
            lax.fori_loop(0, padn_ref.shape[0], b, 0)

        each_pad(lambda e: pad_copy(e).start())
        _wait_runs(buf, sem.at[slot])

        @pl.when(i >= 1)
        def _():
            _wait_runs(loc.at[1 - slot], sem.at[1 - slot])

        each_pad(lambda e: pad_copy(e).wait())


def _smem_tile(tile):
    return pl.BlockSpec((None, None, tile), lambda i, *_: (i, 0, 0), memory_space=pltpu.SMEM)


def _dispatch(xn_slab, s1, s2, tables, tile, n_rows):
    token = xn_slab.shape[1:]
    return pl.pallas_call(
        functools.partial(_dispatch_kernel, unroll=8),
        grid_spec=pltpu.PrefetchScalarGridSpec(
            num_scalar_prefetch=5,
            grid=(xn_slab.shape[0] // tile,),
            in_specs=[_smem_tile(tile), _smem_tile(tile),
                      pl.BlockSpec((tile,) + token, lambda i, *_: (i, 0, 0))],
            out_specs=pl.BlockSpec(memory_space=pl.ANY),
            scratch_shapes=[pltpu.VMEM((2, 2 * tile) + token, xn_slab.dtype),
                            pltpu.VMEM((MOE_BLOCK,) + token, xn_slab.dtype),
                            pltpu.SemaphoreType.DMA((3,))]),
        out_shape=jax.ShapeDtypeStruct((n_rows,) + token, xn_slab.dtype),
        compiler_params=pltpu.CompilerParams(
            dimension_semantics=("arbitrary",), vmem_limit_bytes=VMEM_LIMIT),
        name="moe_dispatch",
    )(*tables, s1, s2, xn_slab)


def _experts_kernel(blk_ref, next_ref, nused_ref, xs_ref, wg_ref, wu_ref, wd_ref, ys_ref,
                    xin, yout, wg_f, wu_f, wd_f, wg_s, wu_s, wd_s, slab_in, slab_out,
                    sem_in, sem_out, sem_w, *, layer):
    n_blocks = ys_ref.shape[0] // MOE_BLOCK
    nlt = xs_ref.shape[1]
    n_used = nused_ref[0]
    block = lambda ref, b: ref.at[pl.ds(pl.multiple_of(b * MOE_BLOCK, MOE_BLOCK), MOE_BLOCK)]
    ring = xin.shape[0]
    rows_in = lambda b: pltpu.make_async_copy(block(xs_ref, b), xin.at[b % ring],
                                              sem_in.at[b % ring])
    rows_out = lambda b, slot: pltpu.make_async_copy(yout.at[slot], block(ys_ref, b),
                                                     sem_out.at[slot])

    def weights(e, slot):
        pairs = ((wg_ref, wg_f), (wu_ref, wu_f), (wd_ref, wd_f))
        return [pltpu.make_async_copy(src.at[layer, e], dst.at[slot], sem_w.at[slot])
                for src, dst in pairs]

    for k in range(ring - 1):
        @pl.when(k < n_used)
        def _():
            rows_in(k).start()

    wring = wg_f.shape[0]

    def later_expert(e, hops):
        for _ in range(hops):
            e = jnp.where(e >= 0, next_ref[jnp.maximum(e, 0)], -1)
        return e

    @pl.when(n_used > 0)
    def _():
        for k in range(wring - 1):
            e_k = later_expert(blk_ref[0], k)

            @pl.when(e_k >= 0)
            def _():
                for c in weights(e_k, k):
                    c.start()

    def step(b, wcount):
        slot = b % 2
        e = blk_ref[b]
        rows_in(b).wait()

        @pl.when(b + ring - 1 < n_used)
        def _():
            rows_in(b + ring - 1).start()

        new_expert = jnp.logical_or(b == 0, e != blk_ref[jnp.maximum(b - 1, 0)])
        wcount = jnp.where(new_expert, wcount + 1, wcount)

        @pl.when(new_expert)
        def _():
            wslot = wcount % wring
            for c in weights(e, wslot):
                c.wait()
            wg_s[...] = wg_f[wslot].astype(BF16)
            wu_s[...] = wu_f[wslot].astype(BF16)
            wd_s[...] = wd_f[wslot].astype(BF16)
            e_far = later_expert(e, wring - 1)

            @pl.when(e_far >= 0)
            def _():
                for c in weights(e_far, (wcount + wring - 1) % wring):
                    c.start()

        _tokens_to_slab(slab_in, xin[b % ring])
        x = _from_slab(slab_in, MOE_BLOCK, nlt).astype(BF16)
        hg = _dot(x, wg_s[...])
        hu = _dot(x, wu_s[...])
        h = (hg * _sigmoid(hg) * hu).astype(BF16)
        _to_slab(slab_out, _dot(h, wd_s[...]))

        @pl.when(b >= 2)
        def _():
            rows_out(b - 2, slot).wait()

        yout[slot] = _tokens_from_slab(slab_out, nlt)
        rows_out(b, slot).start()
        return wcount

    lax.fori_loop(0, n_used, step, jnp.int32(-1))

    @pl.when(n_used >= 2)
    def _():
        rows_out(n_used - 2, n_used % 2).wait()

    @pl.when(n_used >= 1)
    def _():
        rows_out(n_used - 1, (n_used - 1) % 2).wait()

    yout[0] = jnp.zeros(yout.shape[1:], yout.dtype)

    def fill(b, c):
        rows_out(b, 0).start()
        rows_out(b, 0).wait()
        return c

    lax.fori_loop(n_used, n_blocks, fill, 0)


def _experts(xs, blk_expert, next_expert, n_used, wts, layer):
    n_rows, nlt, _ = xs.shape
    d = nlt * LANES
    f = wts['w_gate'].shape[-1]
    token = (nlt, LANES)
    any_spec = pl.BlockSpec(memory_space=pl.ANY)
    return pl.pallas_call(
        functools.partial(_experts_kernel, layer=layer),
        grid_spec=pltpu.PrefetchScalarGridSpec(
            num_scalar_prefetch=3,
            grid=(1,),
            in_specs=[any_spec] * 4,
            out_specs=any_spec,
            scratch_shapes=[pltpu.VMEM((EXPERT_ROW_RING, MOE_BLOCK) + token, xs.dtype),
                            pltpu.VMEM((2, MOE_BLOCK) + token, xs.dtype),
                            pltpu.VMEM((EXPERT_WEIGHT_RING, d, f), F32),
                            pltpu.VMEM((EXPERT_WEIGHT_RING, d, f), F32),
                            pltpu.VMEM((EXPERT_WEIGHT_RING, f, d), F32),
                            pltpu.VMEM((d, f), BF16), pltpu.VMEM((d, f), BF16),
                            pltpu.VMEM((f, d), BF16),
                            pltpu.VMEM((MOE_BLOCK * nlt, LANES), F32),
                            pltpu.VMEM((MOE_BLOCK * nlt, LANES), F32),
                            pltpu.SemaphoreType.DMA((EXPERT_ROW_RING,)),
                            pltpu.SemaphoreType.DMA((2,)),
                            pltpu.SemaphoreType.DMA((EXPERT_WEIGHT_RING,))]),
        out_shape=jax.ShapeDtypeStruct(xs.shape, xs.dtype),
        compiler_params=pltpu.CompilerParams(
            dimension_semantics=("arbitrary",), vmem_limit_bytes=VMEM_LIMIT),
        name="moe_experts",
    )(blk_expert, next_expert, n_used, xs, wts['w_gate'], wts['w_up'], wts['w_down'])


def _combine_kernel(lo_ref, cnt_ref, dst_ref, s1_ref, s2_ref, w1_ref, w2_ref, yp_ref, ys_ref,
                    nf_ref, rows_ref, op_ref, os_ref, loc, acc, sem,
                    *, unroll, final_norm, n_prompt_tiles):
    i = pl.program_id(0)
    tile = yp_ref.shape[0]
    nlt = loc.shape[2]
    slot = i % 2
    buf = loc.at[slot]
    fetch = lambda t, s: _start_runs(t, lo_ref, cnt_ref, dst_ref, loc.at[s], rows_ref, sem.at[s],
                                     False)

    @pl.when(i == 0)
    def _():
        fetch(0, 0)

    @pl.when(i + 1 < pl.num_programs(0))
    def _():
        fetch(i + 1, 1 - slot)

    _wait_runs(buf, sem.at[slot])

    def body(tt, c):
        for u in range(unroll):
            t = tt * unroll + u
            r1 = buf[s1_ref[t]].astype(F32)
            r2 = buf[s2_ref[t]].astype(F32)
            acc[pl.ds(pl.multiple_of(t * nlt, nlt), nlt), :] = w1_ref[t] * r1 + w2_ref[t] * r2
        return c

    lax.fori_loop(0, tile // unroll, body, 0)
    def finish(y_ref, o_ref):
        out = y_ref[...] + _from_slab(acc, tile, nlt)
        if final_norm:
            out = _rms_norm(out, nf_ref[...])
        o_ref[...] = out

    is_prompt = i < n_prompt_tiles
    pl.when(is_prompt)(lambda: finish(yp_ref, op_ref))
    pl.when(jnp.logical_not(is_prompt))(lambda: finish(ys_ref, os_ref))


def _combine(yp, ys, rows, meta, tables, norm_f, final_norm, tile):
    d = yp.shape[1]
    token = rows.shape[1:]
    n_prompt_tiles = yp.shape[0] // tile
    n_tiles = n_prompt_tiles + ys.shape[0] // tile
    tok_p, tok_s = _token_specs(tile, d, n_prompt_tiles)
    return pl.pallas_call(
        functools.partial(_combine_kernel, unroll=8, final_norm=final_norm,
                          n_prompt_tiles=n_prompt_tiles),
        grid_spec=pltpu.PrefetchScalarGridSpec(
            num_scalar_prefetch=3,
            grid=(n_tiles,),
            in_specs=[_smem_tile(tile)] * 4 + [tok_p, tok_s, _const_spec((1, d)),
                                               pl.BlockSpec(memory_space=pl.ANY)],
            out_specs=(tok_p, tok_s),
            scratch_shapes=[pltpu.VMEM((2, 2 * tile) + token, rows.dtype),
                            pltpu.VMEM((tile * token[0], LANES), F32),
                            pltpu.SemaphoreType.DMA((2,))]),
        out_shape=(jax.ShapeDtypeStruct(yp.shape, F32), jax.ShapeDtypeStruct(ys.shape, F32)),
        compiler_params=pltpu.CompilerParams(
            dimension_semantics=("arbitrary",), vmem_limit_bytes=VMEM_LIMIT),
        name="moe_combine",
    )(*tables, *meta, yp, ys, norm_f, rows)


def _moe(yp, ys, wts, layer, norm_f, final_norm, tile):
    d = yp.shape[1]
    n = yp.shape[0] + ys.shape[0]
    assert yp.shape[0] % tile == 0 and ys.shape[0] % tile == 0 and d % LANES == 0
    xn_slab, s1, s2, w1, w2, cnt = _router(yp, ys, wts, layer, tile)
    cnt = cnt[:, SUBLANES:SUBLANES + N_EXPERTS, 0].astype(jnp.int32)
    lo = jnp.cumsum(cnt, axis=1) - cnt
    seg = jnp.sum(cnt, axis=0)
    seg_pad = (seg + MOE_BLOCK - 1) // MOE_BLOCK * MOE_BLOCK
    seg_end = jnp.cumsum(seg_pad)
    seg_start = seg_end - seg_pad
    dst = seg_start[None, :] + jnp.cumsum(cnt, axis=0) - cnt
    n_blocks = (2 * n + N_EXPERTS * (MOE_BLOCK - 1)) // MOE_BLOCK
    blk_ids = jnp.arange(n_blocks, dtype=jnp.int32)
    blk_expert = jnp.minimum(
        jnp.sum((seg_end[None, :] <= blk_ids[:, None] * MOE_BLOCK).astype(jnp.int32), axis=1),
        N_EXPERTS - 1)
    n_used = seg_end[-1:] // MOE_BLOCK
    pad_start = jnp.concatenate([seg_start + seg, blk_ids * MOE_BLOCK])
    pad_rows = jnp.concatenate([seg_pad - seg, jnp.where(blk_ids >= n_used[0], MOE_BLOCK, 0)])
    flat = lambda a: a.reshape(-1)
    runs = (flat(lo), flat(cnt), flat(dst))
    ids = jnp.arange(N_EXPERTS, dtype=jnp.int32)
    later = jnp.where((seg_pad[None, :] > 0) & (ids[None, :] > ids[:, None]), ids[None, :],
                      N_EXPERTS)
    next_expert = jnp.min(later, axis=1)
    next_expert = jnp.where(next_expert < N_EXPERTS, next_expert, -1)
    xs = _dispatch(xn_slab, s1, s2, runs + (pad_start, pad_rows), tile, n_blocks * MOE_BLOCK)
    rows = _experts(xs, blk_expert, next_expert, n_used, wts, layer)
    return _combine(yp, ys, rows, (s1, s2, w1, w2), runs, norm_f, final_norm, tile)


def _prep_weights(norm1, w_in, ln_g, ln_b, w_s, b_s, conv_w, conv_b, lru_lambda, w_rg_a, b_rg_a,
                  w_rg_x, b_rg_x, w_out, norm2, w_route_group, b_route_group, w_route_expert,
                  b_route_expert, w_gate, w_up, w_down, n_sample_t):
    depth, d = w_in.shape[0], w_in.shape[1]
    dh = d // H_A
    row = lambda a: a.reshape(depth, 1, -1).astype(F32)
    causal = jnp.tril(jnp.ones((CHUNK, CHUNK), dtype=bool))
    ws = jnp.where(causal, 0.5 * w_s, 0.0)
    bs = jnp.repeat(jnp.swapaxes(0.5 * b_s, 1, 2), dh, axis=2)
    in_scale = jnp.concatenate([jnp.ones((3 * d,), F32), jnp.full((2 * d,), 0.5, F32)])

    def pair_bd(w):
        w = w.reshape(depth, H_B // 2, 2, w.shape[-2], w.shape[-1])
        z = jnp.zeros_like(w[:, :, 0])
        return jnp.concatenate([jnp.concatenate([w[:, :, 0], z], axis=-1),
                                jnp.concatenate([z, w[:, :, 1]], axis=-1)], axis=-2)

    gap = SUBLANES - N_GROUPS
    tail = ROUTE_ROWS - SUBLANES - N_EXPERTS
    w_route = jnp.concatenate(
        [jnp.swapaxes(w_route_group, 1, 2), jnp.zeros((depth, gap, d), F32),
         jnp.swapaxes(w_route_expert, 1, 2), jnp.zeros((depth, tail, d), F32)], axis=1)
    b_route = jnp.concatenate(
        [b_route_group, jnp.zeros((depth, gap), F32), b_route_expert,
         jnp.zeros((depth, tail), F32)], axis=1)[..., None]
    nt = n_sample_t
    return dict(
        norm1=row(norm1), w_in=(w_in * in_scale).astype(BF16), ln_g=row(ln_g), ln_b=row(ln_b),
        ws=ws.astype(BF16), bs=bs.astype(F32), conv_w=conv_w.astype(F32), conv_b=row(conv_b),
        lam=row(lru_lambda),
        wbd=(0.5 * jnp.concatenate([pair_bd(w_rg_a), pair_bd(w_rg_x)], axis=-1)).astype(BF16),
        b_a=row(0.5 * b_rg_a), b_x=row(0.5 * b_rg_x), w_out=(0.5 * w_out).astype(BF16),
        norm2=row(norm2), w_route=w_route.astype(BF16), b_route=b_route.astype(F32),
        w_gate=w_gate, w_up=w_up, w_down=w_down,
        wsc=jnp.repeat(jnp.transpose(ws[:, :, :nt, :nt], (0, 2, 3, 1)), dh, axis=3).astype(F32),
        bsc=bs[:, :nt].astype(F32),
    )


def kernel(x_prompt, x_sample, state_lru_h, state_conv, norm1, w_in, ln_g, ln_b, w_s, b_s, conv_w, conv_b, lru_lambda, w_rg_a, b_rg_a, w_rg_x, b_rg_x, w_out, norm2, w_route_group, b_route_group, w_route_expert, b_route_expert, w_gate, w_up, w_down, norm_f):
    depth = w_in.shape[0]
    nb, seq, d = x_prompt.shape
    ns, nt, _ = x_sample.shape
    assert nt <= CHUNK
    nf = norm_f.reshape(1, d).astype(F32)
    wts = _prep_weights(norm1, w_in, ln_g, ln_b, w_s, b_s, conv_w, conv_b, lru_lambda, w_rg_a,
                        b_rg_a, w_rg_x, b_rg_x, w_out, norm2, w_route_group, b_route_group,
                        w_route_expert, b_route_expert, w_gate, w_up, w_down, nt)
    conv_state = jnp.transpose(state_conv, (0, 2, 1, 3))
    assert (nb * seq) % MOE_TILE == 0
    sample_pad = (-nt * ns) % MOE_TILE
    assert sample_pad % ns == 0
    pad_t = sample_pad // ns
    yp = x_prompt
    ys = jnp.transpose(x_sample, (1, 0, 2))
    hp, cp, vp, hs, cs, vs = [], [], [], [], [], []
    for l in range(depth):
        yp, h, c, v = _mixer_prompt(yp, wts, l)
        hp.append(h)
        cp.append(c)
        vp.append(v)
        ys, h, c, v = _mixer_sample(ys, state_lru_h, conv_state, wts, l, nt, pad_t)
        hs.append(h)
        cs.append(c)
        vs.append(v)
        yp, ys = _moe(yp.reshape(nb * seq, d), ys.reshape((nt + pad_t) * ns, d), wts, l, nf,
                      l == depth - 1, MOE_TILE)
        yp = yp.reshape(nb, seq, d)
        ys = ys.reshape(nt + pad_t, ns, d)
    to_seq_major = lambda a: jnp.transpose(jnp.stack(a), (0, 2, 1, 3))
    return (yp, jnp.transpose(ys[:nt], (1, 0, 2)), jnp.stack(hp), jnp.stack(cp), jnp.stack(vp),
            jnp.stack(hs), to_seq_major(cs), to_seq_major(vs))
```

```python
import functools

import jax
import jax.numpy as jnp
from jax import lax
from jax.experimental import pallas as pl
from jax.experimental.pallas import tpu as pltpu

F32 = jnp.float32
BF16 = jnp.bfloat16

CHUNK = 128
H_A = 4
H_B = 8
CONV_W = 4
LRU_C = 8.0
N_GROUPS = 4
E_PER_GROUP = 8
N_EXPERTS = N_GROUPS * E_PER_GROUP
EPS = 1e-6

LANES = 128
SUBLANES = 8
BF16_ROWS = 16
V7X_VMEM_BYTES = 64 * 1024 * 1024
VMEM_LIMIT = V7X_VMEM_BYTES - 4 * 1024 * 1024
SCAN_UNROLL = 4
SEQS_PER_TRIP = 8
MOE_TILE = 1024
MOE_BLOCK = 512
EXPERT_ROW_RING = 4
EXPERT_WEIGHT_RING = 3
assert E_PER_GROUP == SUBLANES and N_GROUPS <= SUBLANES
ROUTE_ROWS = -(-SUBLANES * (N_GROUPS + 1) // BF16_ROWS) * BF16_ROWS


def _dot(a, b):
    return jnp.dot(a, b, preferred_element_type=F32)


def _sigmoid(x):
    return 0.5 * (jnp.tanh(0.5 * x) + 1.0)


def _gelu2(x):
    c = 0.7978845608028654
    return x * (1.0 + jnp.tanh(x * (c + (c * 0.044715) * (x * x))))


def _rms_norm(x, g):
    return x * lax.rsqrt(jnp.mean(x * x, axis=-1, keepdims=True) + EPS) * g


def _layer_norm(x, g, b, eps):
    xc = x - jnp.mean(x, axis=-1, keepdims=True)
    var = jnp.mean(xc * xc, axis=-1, keepdims=True)
    return xc * lax.rsqrt(var + eps) * g + b


def _softplus(x):
    return jnp.maximum(x, 0.0) + jnp.log(1.0 + jnp.exp(-jnp.abs(x)))


def _lru_gates(xc, wbd_ref, b_a, b_x, lam_half):
    d = xc.shape[-1]
    xcb = xc.astype(BF16)
    r_parts, i_parts = [], []
    for q in range(d // 256):
        ri = _dot(xcb[:, q * 256:(q + 1) * 256], wbd_ref[q])
        r_parts.append(ri[:, :256])
        i_parts.append(ri[:, 256:])
    r2 = 1.0 + jnp.tanh(jnp.concatenate(r_parts, axis=-1) + b_a)
    i2 = 1.0 + jnp.tanh(jnp.concatenate(i_parts, axis=-1) + b_x)
    a = jnp.exp(lam_half * r2)
    mult_half = jnp.sqrt(0.25 - 0.25 * (a * a))
    return a, mult_half, i2 * xc


def _shift_rows(x, hist, j):
    sh = pltpu.roll(x, j, axis=0)
    top = jnp.where(lax.broadcasted_iota(jnp.int32, (SUBLANES, 1), 0) < j,
                    pltpu.roll(hist, j, axis=0), sh[:SUBLANES])
    return jnp.concatenate([top, sh[SUBLANES:]], axis=0)


def _mixer_prompt_kernel(x_ref, norm_ref, win_ref, lng_ref, lnb_ref, ws_ref, bs_ref,
                         cw_ref, cb_ref, lam_ref, wbd_ref, ba_ref, bx_ref, wout_ref,
                         y_ref, h_ref, conv_ref, v_ref,
                         hist_s, a_s, b_s, ya_s, gb_s, hc_s):
    i = pl.program_id(0)
    last = pl.num_programs(0) - 1
    nseq, _, d = x_ref.shape
    nlt = d // LANES
    spt = SEQS_PER_TRIP
    pair = spt * CHUNK

    @pl.when(i == 0)
    def _():
        hist_s[...] = jnp.zeros_like(hist_s)
        hc_s[...] = jnp.zeros_like(hc_s)

    lam_half = (-0.5 * LRU_C) * _softplus(-lam_ref[...])
    is_first = i == 0

    def branch_body(p, c):
        row = pl.multiple_of(p * pair, pair)
        x = jnp.concatenate([x_ref[spt * p + k] for k in range(spt)], axis=0)
        xnb = _rms_norm(x, norm_ref[...]).astype(BF16)
        v = _layer_norm(_gelu2(_dot(xnb, win_ref[:, d:2 * d])), lng_ref[...], lnb_ref[...],
                        4.0 * EPS)

        @pl.when(i == last)
        def _():
            for k in range(spt):
                v_ref[spt * p + k] = v[k * CHUNK:(k + 1) * CHUNK]

        vb = v.astype(BF16)
        dh = d // H_A
        z_rows = []
        for k in range(spt):
            zs = [_dot(ws_ref[h], vb[k * CHUNK:(k + 1) * CHUNK, h * dh:(h + 1) * dh])
                  for h in range(H_A)]
            z_rows.append(jnp.concatenate(zs, axis=-1) + bs_ref[...])
        z = jnp.concatenate(z_rows, axis=0)
        ya = _gelu2(_dot(xnb, win_ref[:, 0:d])) * z
        ga2 = 1.0 + jnp.tanh(_dot(xnb, win_ref[:, 3 * d:4 * d]))
        ya_s[pl.ds(row, pair), :] = (ga2 * ya).astype(BF16)
        gb_s[pl.ds(row, pair), :] = (1.0 + jnp.tanh(_dot(xnb, win_ref[:, 4 * d:5 * d]))).astype(BF16)
        xr = _dot(xnb, win_ref[:, 2 * d:3 * d])
        xcs = []
        for k in range(spt):
            s = spt * p + k
            xk = xr[k * CHUNK:(k + 1) * CHUNK]
            hist = hist_s[s]
            xc = cb_ref[...] + xk * cw_ref[CONV_W - 1:CONV_W, :]
            for j in range(1, CONV_W):
                xc = xc + _shift_rows(xk, hist, j) * cw_ref[CONV_W - 1 - j:CONV_W - j, :]
            hist_s[s] = xk[CHUNK - SUBLANES:]
            xcs.append(xc)
        xc = jnp.concatenate(xcs, axis=0)
        a, mult_half, ix2 = _lru_gates(xc, wbd_ref, ba_ref[...], bx_ref[...], lam_half)
        bt = mult_half * ix2
        for k in range(spt):
            s = spt * p + k
            for j in range(nlt):
                a_s[j, pl.ds(s, CHUNK, stride=nseq), :] = a[k * CHUNK:(k + 1) * CHUNK, j * LANES:(j + 1) * LANES]
                b_s[j, pl.ds(s, CHUNK, stride=nseq), :] = bt[k * CHUNK:(k + 1) * CHUNK, j * LANES:(j + 1) * LANES]

        @pl.when(is_first)
        def _():
            for k in range(spt):
                first = 0.5 * ix2[k * CHUNK:k * CHUNK + 1]
                for j in range(nlt):
                    b_s[j, pl.ds(spt * p + k, 1), :] = first[:, j * LANES:(j + 1) * LANES]

        return c

    lax.fori_loop(0, nseq // spt, branch_body, 0)

    def scan_body(tt, hs):
        for u in range(SCAN_UNROLL):
            row = pl.multiple_of((tt * SCAN_UNROLL + u) * nseq, nseq)
            out = []
            for j in range(nlt):
                h = a_s[j, pl.ds(row, nseq), :] * hs[j] + b_s[j, pl.ds(row, nseq), :]
                b_s[j, pl.ds(row, nseq), :] = h
                out.append(h)
            hs = tuple(out)
        return hs

    h0 = tuple(hc_s[:, j * LANES:(j + 1) * LANES] for j in range(nlt))
    hs = lax.fori_loop(0, CHUNK // SCAN_UNROLL, scan_body, h0)
    hfin = jnp.concatenate(hs, axis=-1)
    hc_s[...] = hfin
    h_ref[...] = hfin

    def out_body(p, c):
        row = pl.multiple_of(p * pair, pair)
        hrows = []
        for k in range(spt):
            hrows.append(jnp.concatenate(
                [b_s[j, pl.ds(spt * p + k, CHUNK, stride=nseq), :] for j in range(nlt)], axis=-1))
        h = jnp.concatenate(hrows, axis=0)
        merged = (ya_s[pl.ds(row, pair), :].astype(F32)
                  + gb_s[pl.ds(row, pair), :].astype(F32) * h).astype(BF16)
        o = _dot(merged, wout_ref[...])
        for k in range(spt):
            y_ref[spt * p + k] = x_ref[spt * p + k] + o[k * CHUNK:(k + 1) * CHUNK]
        return c

    lax.fori_loop(0, nseq // spt, out_body, 0)

    @pl.when(i == last)
    def _():
        conv_ref[...] = hist_s[:, SUBLANES - (CONV_W - 1):, :]


def _const_spec(shape):
    nd = len(shape)
    return pl.BlockSpec(shape, lambda i, *_, _n=nd: (0,) * _n, pipeline_mode=pl.Buffered(1))


def _layer_spec(stacked, layer):
    shape = stacked.shape[1:]
    nd = len(shape)
    return pl.BlockSpec((None,) + tuple(shape), lambda i, *_, _n=nd: (layer,) + (0,) * _n,
                        pipeline_mode=pl.Buffered(1))


MIXER_WEIGHTS = ('norm1', 'w_in', 'ln_g', 'ln_b', 'ws', 'bs', 'conv_w', 'conv_b', 'lam', 'wbd',
                 'b_a', 'b_x', 'w_out')


def _mixer_prompt(x, wts, layer):
    nseq, seq, d = x.shape
    assert seq % CHUNK == 0 and nseq == SUBLANES and d % 256 == 0
    n_chunks = seq // CHUNK
    nlt = d // LANES
    weights = tuple(wts[k] for k in MIXER_WEIGHTS)
    x_spec = pl.BlockSpec((nseq, CHUNK, d), lambda i: (0, i, 0))
    out_shape = (
        jax.ShapeDtypeStruct((nseq, seq, d), F32),
        jax.ShapeDtypeStruct((nseq, d), F32),
        jax.ShapeDtypeStruct((nseq, CONV_W - 1, d), F32),
        jax.ShapeDtypeStruct((nseq, CHUNK, d), F32),
    )
    out_specs = (
        x_spec,
        _const_spec((nseq, d)),
        _const_spec((nseq, CONV_W - 1, d)),
        _const_spec((nseq, CHUNK, d)),
    )
    scratch = [
        pltpu.VMEM((nseq, SUBLANES, d), F32),
        pltpu.VMEM((nlt, CHUNK * nseq, LANES), F32),
        pltpu.VMEM((nlt, CHUNK * nseq, LANES), F32),
        pltpu.VMEM((nseq * CHUNK, d), BF16),
        pltpu.VMEM((nseq * CHUNK, d), BF16),
        pltpu.VMEM((nseq, d), F32),
    ]
    return pl.pallas_call(
        _mixer_prompt_kernel,
        grid=(n_chunks,),
        in_specs=[x_spec] + [_layer_spec(w, layer) for w in weights],
        out_specs=out_specs,
        out_shape=out_shape,
        scratch_shapes=scratch,
        compiler_params=pltpu.CompilerParams(
            dimension_semantics=("arbitrary",), vmem_limit_bytes=VMEM_LIMIT),
        name="mixer_prompt",
    )(x, *weights)


def _mixer_sample_kernel(x_ref, h0_ref, cst_ref, norm_ref, win_ref, lng_ref, lnb_ref, wsc_ref,
                         bsc_ref, cw_ref, cb_ref, lam_ref, wbd_ref, ba_ref, bx_ref, wout_ref,
                         y_ref, h_ref, conv_ref, v_ref):
    nt, ns, d = x_ref.shape
    at = lambda a, t: a[t * ns:(t + 1) * ns]
    lam_half = (-0.5 * LRU_C) * _softplus(-lam_ref[...])
    x = x_ref[...].reshape(nt * ns, d)
    xnb = _rms_norm(x, norm_ref[...]).astype(BF16)
    v = _layer_norm(_gelu2(_dot(xnb, win_ref[:, d:2 * d])), lng_ref[...], lnb_ref[...], 4.0 * EPS)
    v_ref[...] = v.reshape(nt, ns, d)
    zs = []
    for t in range(nt):
        z = bsc_ref[t:t + 1, :]
        for s in range(t + 1):
            z = z + wsc_ref[t, s:s + 1, :] * at(v, s)
        zs.append(z)
    ya = _gelu2(_dot(xnb, win_ref[:, 0:d])) * jnp.concatenate(zs, axis=0)
    ga2 = 1.0 + jnp.tanh(_dot(xnb, win_ref[:, 3 * d:4 * d]))
    gb2 = 1.0 + jnp.tanh(_dot(xnb, win_ref[:, 4 * d:5 * d]))
    xr = _dot(xnb, win_ref[:, 2 * d:3 * d])
    xp = [cst_ref[k] for k in range(CONV_W - 1)] + [at(xr, t) for t in range(nt)]
    xcs = []
    for t in range(nt):
        xc = cb_ref[...]
        for k in range(CONV_W):
            xc = xc + xp[t + k] * cw_ref[k:k + 1, :]
        xcs.append(xc)
    a, mult_half, ix2 = _lru_gates(jnp.concatenate(xcs, axis=0), wbd_ref, ba_ref[...],
                                   bx_ref[...], lam_half)
    bt = mult_half * ix2
    h = h0_ref[...]
    hs = []
    for t in range(nt):
        h = at(a, t) * h + at(bt, t)
        hs.append(h)
    merged = (ga2 * ya + gb2 * jnp.concatenate(hs, axis=0)).astype(BF16)
    y_ref[0:nt] = (x + _dot(merged, wout_ref[...])).reshape(nt, ns, d)
    if y_ref.shape[0] > nt:
        y_ref[nt:] = jnp.zeros((y_ref.shape[0] - nt, ns, d), F32)
    h_ref[...] = h
    for k in range(CONV_W - 1):
        conv_ref[k] = xp[nt + k]


def _mixer_sample(x, h0, conv_state, wts, layer, nt, pad_t):
    _, ns, d = x.shape
    names = MIXER_WEIGHTS[:4] + ('wsc', 'bsc') + MIXER_WEIGHTS[6:]
    stacked = (h0, conv_state) + tuple(wts[k] for k in names)
    out_shape = (
        jax.ShapeDtypeStruct((nt + pad_t, ns, d), F32),
        jax.ShapeDtypeStruct((ns, d), F32),
        jax.ShapeDtypeStruct((CONV_W - 1, ns, d), F32),
        jax.ShapeDtypeStruct((nt, ns, d), F32),
    )
    return pl.pallas_call(
        _mixer_sample_kernel,
        grid=(1,),
        in_specs=[_const_spec((nt, ns, d))] + [_layer_spec(w, layer) for w in stacked],
        out_specs=tuple(_const_spec(s.shape) for s in out_shape),
        out_shape=out_shape,
        compiler_params=pltpu.CompilerParams(
            dimension_semantics=("arbitrary",), vmem_limit_bytes=VMEM_LIMIT),
        name="mixer_sample",
    )(x, *stacked)


def _to_slab(ref, val):
    t, d = val.shape
    nlt = d // LANES
    for j in range(nlt):
        ref[pl.ds(j, t, stride=nlt), :] = val[:, j * LANES:(j + 1) * LANES]


def _from_slab(ref, t, nlt):
    return jnp.concatenate([ref[pl.ds(j, t, stride=nlt), :] for j in range(nlt)], axis=-1)


def _tokens_from_slab(slab_ref, nlt):
    return slab_ref[...].reshape(slab_ref.shape[0] // nlt, nlt, LANES).astype(BF16)


def _tokens_to_slab(slab_ref, tokens):
    slab_ref[...] = tokens.astype(F32).reshape(slab_ref.shape)


def _router_kernel(yp_ref, ys_ref, norm_ref, wr_ref, br_ref, tri_ref, low_ref, xn_ref, s1_ref,
                   s2_ref, w1_ref, w2_ref, cnt_ref, slab_s, *, n_prompt_tiles):
    y = jnp.where(pl.program_id(0) < n_prompt_tiles, yp_ref[...], ys_ref[...])
    xn = _rms_norm(y, norm_ref[...])
    _to_slab(slab_s, xn)
    xn_ref[...] = _tokens_from_slab(slab_s, xn_ref.shape[1])
    lt = lax.dot_general(wr_ref[...], xn.astype(BF16), (((1,), (1,)), ((), ())),
                         preferred_element_type=F32) + br_ref[...]
    tile = lt.shape[1]
    row = lax.broadcasted_iota(jnp.int32, (SUBLANES, tile), 0).astype(F32)
    neg = jnp.float32(-jnp.inf)
    big = jnp.float32(SUBLANES)

    def argmax_first(vals):
        m = jnp.max(vals, axis=0, keepdims=True)
        return m, jnp.min(jnp.where(vals == m, row, big), axis=0, keepdims=True)

    def group_block(x, gi):
        out = x[SUBLANES:2 * SUBLANES]
        for g in range(1, N_GROUPS):
            out = jnp.where(gi == g, x[(g + 1) * SUBLANES:(g + 2) * SUBLANES], out)
        return out

    gl = jnp.where(row < N_GROUPS, lt[0:SUBLANES], neg)
    gm, gi = argmax_first(gl)
    pg_top = 1.0 / jnp.sum(jnp.exp(gl - gm), axis=0, keepdims=True)
    el = group_block(lt, gi)
    m1, i1 = argmax_first(el)
    m2, i2 = argmax_first(jnp.where(row == i1, neg, el))
    e2 = jnp.exp(m2 - m1)
    w1 = pg_top / (1.0 + e2)
    w2 = pg_top * e2 / (1.0 + e2)
    hot1 = row == i1
    hot2 = row == i2
    picked = jnp.where(jnp.logical_or(hot1, hot2), 1.0, 0.0)
    zero = jnp.zeros_like(picked)
    onehot = jnp.concatenate(
        [zero] + [jnp.where(gi == g, picked, 0.0) for g in range(N_GROUPS)] + [zero],
        axis=0).astype(BF16)
    before = _dot(onehot, tri_ref[...])
    lower = jnp.sum(_dot(low_ref[...], onehot), axis=1, keepdims=True)
    slot = group_block(before + lower, gi)
    p1 = jnp.sum(jnp.where(hot1, slot, 0.0), axis=0, keepdims=True)
    p2 = jnp.sum(jnp.where(hot2, slot, 0.0), axis=0, keepdims=True)
    s1_ref[0] = p1.astype(jnp.int32)
    s2_ref[0] = p2.astype(jnp.int32)
    w1_ref[0] = w1
    w2_ref[0] = w2
    cnt = jnp.sum(onehot.astype(F32), axis=1, keepdims=True)
    cnt_ref[0] = jnp.broadcast_to(cnt, cnt_ref.shape[1:])


def _token_specs(tile, d, n_prompt_tiles):
    last = n_prompt_tiles - 1
    return (pl.BlockSpec((tile, d), lambda i, *_: (jnp.minimum(i, last), 0)),
            pl.BlockSpec((tile, d), lambda i, *_: (jnp.maximum(i - n_prompt_tiles, 0), 0)))


def _router(yp, ys, wts, layer, tile):
    d = yp.shape[1]
    nlt = d // LANES
    n_prompt_tiles = yp.shape[0] // tile
    n_tiles = n_prompt_tiles + ys.shape[0] // tile
    tri = jnp.triu(jnp.ones((tile, tile), BF16), 1)
    low = jnp.tril(jnp.ones((ROUTE_ROWS, ROUTE_ROWS), BF16), -1)
    per_token = pl.BlockSpec((1, 1, tile), lambda i: (i, 0, 0))
    return pl.pallas_call(
        functools.partial(_router_kernel, n_prompt_tiles=n_prompt_tiles),
        grid=(n_tiles,),
        in_specs=list(_token_specs(tile, d, n_prompt_tiles)) + [
            _layer_spec(wts['norm2'], layer), _layer_spec(wts['w_route'], layer),
            _layer_spec(wts['b_route'], layer), _const_spec((tile, tile)),
            _const_spec((ROUTE_ROWS, ROUTE_ROWS))],
        out_specs=(pl.BlockSpec((tile, nlt, LANES), lambda i: (i, 0, 0)),
                   per_token, per_token, per_token, per_token,
                   pl.BlockSpec((1, ROUTE_ROWS, LANES), lambda i: (i, 0, 0))),
        out_shape=(jax.ShapeDtypeStruct((n_tiles * tile, nlt, LANES), BF16),
                   jax.ShapeDtypeStruct((n_tiles, 1, tile), jnp.int32),
                   jax.ShapeDtypeStruct((n_tiles, 1, tile), jnp.int32),
                   jax.ShapeDtypeStruct((n_tiles, 1, tile), F32),
                   jax.ShapeDtypeStruct((n_tiles, 1, tile), F32),
                   jax.ShapeDtypeStruct((n_tiles, ROUTE_ROWS, LANES), F32)),
        scratch_shapes=[pltpu.VMEM((tile * nlt, LANES), F32)],
        compiler_params=pltpu.CompilerParams(
            dimension_semantics=("arbitrary",), vmem_limit_bytes=VMEM_LIMIT),
        name="moe_router",
    )(yp, ys, wts['norm2'], wts['w_route'], wts['b_route'], tri, low)


def _run_copy(tile_idx, e, lo_ref, cnt_ref, dst_ref, local, remote, sem, to_remote):
    k = tile_idx * N_EXPERTS + e
    loc = local.at[pl.ds(lo_ref[k], cnt_ref[k])]
    rem = remote.at[pl.ds(dst_ref[k], cnt_ref[k])]
    return pltpu.make_async_copy(loc, rem, sem) if to_remote else pltpu.make_async_copy(rem, loc, sem)


def _start_runs(tile_idx, lo_ref, cnt_ref, dst_ref, local, remote, sem, to_remote):
    for e in range(N_EXPERTS):
        @pl.when(cnt_ref[tile_idx * N_EXPERTS + e] > 0)
        def _():
            _run_copy(tile_idx, e, lo_ref, cnt_ref, dst_ref, local, remote, sem, to_remote).start()


def _wait_runs(local, sem):
    pltpu.make_async_copy(local, local, sem).wait()


def _dispatch_kernel(lo_ref, cnt_ref, dst_ref, padst_ref, padn_ref,
                     s1_ref, s2_ref, x_ref, xs_ref, loc, zeros, sem, *, unroll):
    i = pl.program_id(0)
    last = pl.num_programs(0) - 1
    tile = x_ref.shape[0]
    slot = i % 2
    buf = loc.at[slot]

    @pl.when(i >= 2)
    def _():
        _wait_runs(buf, sem.at[slot])

    def body(tt, c):
        for u in range(unroll):
            t = tt * unroll + u
            row = x_ref[t]
            buf[s1_ref[t]] = row
            buf[s2_ref[t]] = row
        return c

    lax.fori_loop(0, tile // unroll, body, 0)
    _start_runs(i, lo_ref, cnt_ref, dst_ref, buf, xs_ref, sem.at[slot], True)

    @pl.when(i == last)
    def _():
        zeros[...] = jnp.zeros_like(zeros)

        def pad_copy(e):
            n = padn_ref[e]
            dst = xs_ref.at[pl.ds(padst_ref[e], n)]
            return pltpu.make_async_copy(zeros.at[pl.ds(0, n)], dst, sem.at[2])

        def each_pad(fn):
            def b(e, c):
                @pl.when(padn_ref[e] > 0)
                def _():
                    fn(e)
                return c
            lax.fori_loop(0, padn_ref.shape[0], b, 0)

        each_pad(lambda e: pad_copy(e).start())
        _wait_runs(buf, sem.at[slot])

        @pl.when(i >= 1)
        def _():
            _wait_runs(loc.at[1 - slot], sem.at[1 - slot])

        each_pad(lambda e: pad_copy(e).wait())


def _smem_tile(tile):
    return pl.BlockSpec((None, None, tile), lambda i, *_: (i, 0, 0), memory_space=pltpu.SMEM)


def _dispatch(xn_slab, s1, s2, tables, tile, n_rows):
    token = xn_slab.shape[1:]
    return pl.pallas_call(
        functools.partial(_dispatch_kernel, unroll=8),
        grid_spec=pltpu.PrefetchScalarGridSpec(
            num_scalar_prefetch=5,
            grid=(xn_slab.shape[0] // tile,),
            in_specs=[_smem_tile(tile), _smem_tile(tile),
                      pl.BlockSpec((tile,) + token, lambda i, *_: (i, 0, 0))],
            out_specs=pl.BlockSpec(memory_space=pl.ANY),
            scratch_shapes=[pltpu.VMEM((2, 2 * tile) + token, xn_slab.dtype),
                            pltpu.VMEM((MOE_BLOCK,) + token, xn_slab.dtype),
                            pltpu.SemaphoreType.DMA((3,))]),
        out_shape=jax.ShapeDtypeStruct((n_rows,) + token, xn_slab.dtype),
        compiler_params=pltpu.CompilerParams(
            dimension_semantics=("arbitrary",), vmem_limit_bytes=VMEM_LIMIT),
        name="moe_dispatch",
    )(*tables, s1, s2, xn_slab)


def _experts_kernel(blk_ref, next_ref, nused_ref, xs_ref, wg_ref, wu_ref, wd_ref, ys_ref,
                    xin, yout, wg_f, wu_f, wd_f, wg_s, wu_s, wd_s, slab_in,
                    sem_in, sem_out, sem_w, *, layer):
    n_blocks = ys_ref.shape[0] // MOE_BLOCK
    nlt = xs_ref.shape[1]
    n_used = nused_ref[0]
    block = lambda ref, b: ref.at[pl.ds(pl.multiple_of(b * MOE_BLOCK, MOE_BLOCK), MOE_BLOCK)]
    ring = xin.shape[0]
    rows_in = lambda b: pltpu.make_async_copy(block(xs_ref, b), xin.at[b % ring],
                                              sem_in.at[b % ring])
    rows_out = lambda b, slot: pltpu.make_async_copy(yout.at[slot], block(ys_ref, b),
                                                     sem_out.at[slot])

    def weights(e, slot):
        pairs = ((wg_ref, wg_f), (wu_ref, wu_f), (wd_ref, wd_f))
        return [pltpu.make_async_copy(src.at[layer, e], dst.at[slot], sem_w.at[slot])
                for src, dst in pairs]

    for k in range(ring - 1):
        @pl.when(k < n_used)
        def _():
            rows_in(k).start()

    wring = wg_f.shape[0]

    def later_expert(e, hops):
        for _ in range(hops):
            e = jnp.where(e >= 0, next_ref[jnp.maximum(e, 0)], -1)
        return e

    @pl.when(n_used > 0)
    def _():
        for k in range(wring - 1):
            e_k = later_expert(blk_ref[0], k)

            @pl.when(e_k >= 0)
            def _():
                for c in weights(e_k, k):
                    c.start()

    def step(b, wcount):
        slot = b % 2
        e = blk_ref[b]
        rows_in(b).wait()

        @pl.when(b + ring - 1 < n_used)
        def _():
            rows_in(b + ring - 1).start()

        new_expert = jnp.logical_or(b == 0, e != blk_ref[jnp.maximum(b - 1, 0)])
        wcount = jnp.where(new_expert, wcount + 1, wcount)

        @pl.when(new_expert)
        def _():
            wslot = wcount % wring
            for c in weights(e, wslot):
                c.wait()
            wg_s[...] = wg_f[wslot].astype(BF16)
            wu_s[...] = wu_f[wslot].astype(BF16)
            wd_s[...] = wd_f[wslot].astype(BF16)
            e_far = later_expert(e, wring - 1)

            @pl.when(e_far >= 0)
            def _():
                for c in weights(e_far, (wcount + wring - 1) % wring):
                    c.start()

        _tokens_to_slab(slab_in, xin[b % ring])
        x = _from_slab(slab_in, MOE_BLOCK, nlt).astype(BF16)
        hg = _dot(x, wg_s[...])
        hu = _dot(x, wu_s[...])
        h = (hg * _sigmoid(hg) * hu).astype(BF16)
        o = _dot(h, wd_s[...]).reshape(MOE_BLOCK, nlt, LANES).astype(BF16)

        @pl.when(b >= 2)
        def _():
            rows_out(b - 2, slot).wait()

        yout[slot] = o
        rows_out(b, slot).start()
        return wcount

    lax.fori_loop(0, n_used, step, jnp.int32(-1))

    @pl.when(n_used >= 2)
    def _():
        rows_out(n_used - 2, n_used % 2).wait()

    @pl.when(n_used >= 1)
    def _():
        rows_out(n_used - 1, (n_used - 1) % 2).wait()

    yout[0] = jnp.zeros(yout.shape[1:], yout.dtype)

    def fill(b, c):
        rows_out(b, 0).start()
        rows_out(b, 0).wait()
        return c

    lax.fori_loop(n_used, n_blocks, fill, 0)


def _experts(xs, blk_expert, next_expert, n_used, wts, layer):
    n_rows, nlt, _ = xs.shape
    d = nlt * LANES
    f = wts['w_gate'].shape[-1]
    token = (nlt, LANES)
    any_spec = pl.BlockSpec(memory_space=pl.ANY)
    return pl.pallas_call(
        functools.partial(_experts_kernel, layer=layer),
        grid_spec=pltpu.PrefetchScalarGridSpec(
            num_scalar_prefetch=3,
            grid=(1,),
            in_specs=[any_spec] * 4,
            out_specs=any_spec,
            scratch_shapes=[pltpu.VMEM((EXPERT_ROW_RING, MOE_BLOCK) + token, xs.dtype),
                            pltpu.VMEM((2, MOE_BLOCK) + token, xs.dtype),
                            pltpu.VMEM((EXPERT_WEIGHT_RING, d, f), F32),
                            pltpu.VMEM((EXPERT_WEIGHT_RING, d, f), F32),
                            pltpu.VMEM((EXPERT_WEIGHT_RING, f, d), F32),
                            pltpu.VMEM((d, f), BF16), pltpu.VMEM((d, f), BF16),
                            pltpu.VMEM((f, d), BF16),
                            pltpu.VMEM((MOE_BLOCK * nlt, LANES), F32),
                            pltpu.SemaphoreType.DMA((EXPERT_ROW_RING,)),
                            pltpu.SemaphoreType.DMA((2,)),
                            pltpu.SemaphoreType.DMA((EXPERT_WEIGHT_RING,))]),
        out_shape=jax.ShapeDtypeStruct(xs.shape, xs.dtype),
        compiler_params=pltpu.CompilerParams(
            dimension_semantics=("arbitrary",), vmem_limit_bytes=VMEM_LIMIT),
        name="moe_experts",
    )(blk_expert, next_expert, n_used, xs, wts['w_gate'], wts['w_up'], wts['w_down'])


def _combine_kernel(lo_ref, cnt_ref, dst_ref, s1_ref, s2_ref, w1_ref, w2_ref, yp_ref, ys_ref,
                    nf_ref, rows_ref, op_ref, os_ref, loc, acc, sem,
                    *, unroll, final_norm, n_prompt_tiles):
    i = pl.program_id(0)
    tile = yp_ref.shape[0]
    nlt = loc.shape[2]
    slot = i % 2
    buf = loc.at[slot]
    fetch = lambda t, s: _start_runs(t, lo_ref, cnt_ref, dst_ref, loc.at[s], rows_ref, sem.at[s],
                                     False)

    @pl.when(i == 0)
    def _():
        fetch(0, 0)

    @pl.when(i + 1 < pl.num_programs(0))
    def _():
        fetch(i + 1, 1 - slot)

    _wait_runs(buf, sem.at[slot])

    def body(tt, c):
        for u in range(unroll):
            t = tt * unroll + u
            r1 = buf[s1_ref[t]].astype(F32)
            r2 = buf[s2_ref[t]].astype(F32)
            acc[pl.ds(pl.multiple_of(t * nlt, nlt), nlt), :] = w1_ref[t] * r1 + w2_ref[t] * r2
        return c

    lax.fori_loop(0, tile // unroll, body, 0)
    def finish(y_ref, o_ref):
        out = y_ref[...] + acc[...].reshape(tile, nlt, LANES).reshape(tile, nlt * LANES)
        if final_norm:
            out = _rms_norm(out, nf_ref[...])
        o_ref[...] = out

    is_prompt = i < n_prompt_tiles
    pl.when(is_prompt)(lambda: finish(yp_ref, op_ref))
    pl.when(jnp.logical_not(is_prompt))(lambda: finish(ys_ref, os_ref))


def _combine(yp, ys, rows, meta, tables, norm_f, final_norm, tile):
    d = yp.shape[1]
    token = rows.shape[1:]
    n_prompt_tiles = yp.shape[0] // tile
    n_tiles = n_prompt_tiles + ys.shape[0] // tile
    tok_p, tok_s = _token_specs(tile, d, n_prompt_tiles)
    return pl.pallas_call(
        functools.partial(_combine_kernel, unroll=8, final_norm=final_norm,
                          n_prompt_tiles=n_prompt_tiles),
        grid_spec=pltpu.PrefetchScalarGridSpec(
            num_scalar_prefetch=3,
            grid=(n_tiles,),
            in_specs=[_smem_tile(tile)] * 4 + [tok_p, tok_s, _const_spec((1, d)),
                                               pl.BlockSpec(memory_space=pl.ANY)],
            out_specs=(tok_p, tok_s),
            scratch_shapes=[pltpu.VMEM((2, 2 * tile) + token, rows.dtype),
                            pltpu.VMEM((tile * token[0], LANES), F32),
                            pltpu.SemaphoreType.DMA((2,))]),
        out_shape=(jax.ShapeDtypeStruct(yp.shape, F32), jax.ShapeDtypeStruct(ys.shape, F32)),
        compiler_params=pltpu.CompilerParams(
            dimension_semantics=("arbitrary",), vmem_limit_bytes=VMEM_LIMIT),
        name="moe_combine",
    )(*tables, *meta, yp, ys, norm_f, rows)


def _moe(yp, ys, wts, layer, norm_f, final_norm, tile):
    d = yp.shape[1]
    n = yp.shape[0] + ys.shape[0]
    assert yp.shape[0] % tile == 0 and ys.shape[0] % tile == 0 and d % LANES == 0
    xn_slab, s1, s2, w1, w2, cnt = _router(yp, ys, wts, layer, tile)
    cnt = cnt[:, SUBLANES:SUBLANES + N_EXPERTS, 0].astype(jnp.int32)
    lo = jnp.cumsum(cnt, axis=1) - cnt
    seg = jnp.sum(cnt, axis=0)
    seg_pad = (seg + MOE_BLOCK - 1) // MOE_BLOCK * MOE_BLOCK
    seg_end = jnp.cumsum(seg_pad)
    seg_start = seg_end - seg_pad
    dst = seg_start[None, :] + jnp.cumsum(cnt, axis=0) - cnt
    n_blocks = (2 * n + N_EXPERTS * (MOE_BLOCK - 1)) // MOE_BLOCK
    blk_ids = jnp.arange(n_blocks, dtype=jnp.int32)
    blk_expert = jnp.minimum(
        jnp.sum((seg_end[None, :] <= blk_ids[:, None] * MOE_BLOCK).astype(jnp.int32), axis=1),
        N_EXPERTS - 1)
    n_used = seg_end[-1:] // MOE_BLOCK
    pad_start = jnp.concatenate([seg_start + seg, blk_ids * MOE_BLOCK])
    pad_rows = jnp.concatenate([seg_pad - seg, jnp.where(blk_ids >= n_used[0], MOE_BLOCK, 0)])
    flat = lambda a: a.reshape(-1)
    runs = (flat(lo), flat(cnt), flat(dst))
    ids = jnp.arange(N_EXPERTS, dtype=jnp.int32)
    later = jnp.where((seg_pad[None, :] > 0) & (ids[None, :] > ids[:, None]), ids[None, :],
                      N_EXPERTS)
    next_expert = jnp.min(later, axis=1)
    next_expert = jnp.where(next_expert < N_EXPERTS, next_expert, -1)
    xs = _dispatch(xn_slab, s1, s2, runs + (pad_start, pad_rows), tile, n_blocks * MOE_BLOCK)
    rows = _experts(xs, blk_expert, next_expert, n_used, wts, layer)
    return _combine(yp, ys, rows, (s1, s2, w1, w2), runs, norm_f, final_norm, tile)


def _prep_weights(norm1, w_in, ln_g, ln_b, w_s, b_s, conv_w, conv_b, lru_lambda, w_rg_a, b_rg_a,
                  w_rg_x, b_rg_x, w_out, norm2, w_route_group, b_route_group, w_route_expert,
                  b_route_expert, w_gate, w_up, w_down, n_sample_t):
    depth, d = w_in.shape[0], w_in.shape[1]
    dh = d // H_A
    row = lambda a: a.reshape(depth, 1, -1).astype(F32)
    causal = jnp.tril(jnp.ones((CHUNK, CHUNK), dtype=bool))
    ws = jnp.where(causal, 0.5 * w_s, 0.0)
    bs = jnp.repeat(jnp.swapaxes(0.5 * b_s, 1, 2), dh, axis=2)
    in_scale = jnp.concatenate([jnp.ones((3 * d,), F32), jnp.full((2 * d,), 0.5, F32)])

    def pair_bd(w):
        w = w.reshape(depth, H_B // 2, 2, w.shape[-2], w.shape[-1])
        z = jnp.zeros_like(w[:, :, 0])
        return jnp.concatenate([jnp.concatenate([w[:, :, 0], z], axis=-1),
                                jnp.concatenate([z, w[:, :, 1]], axis=-1)], axis=-2)

    gap = SUBLANES - N_GROUPS
    tail = ROUTE_ROWS - SUBLANES - N_EXPERTS
    w_route = jnp.concatenate(
        [jnp.swapaxes(w_route_group, 1, 2), jnp.zeros((depth, gap, d), F32),
         jnp.swapaxes(w_route_expert, 1, 2), jnp.zeros((depth, tail, d), F32)], axis=1)
    b_route = jnp.concatenate(
        [b_route_group, jnp.zeros((depth, gap), F32), b_route_expert,
         jnp.zeros((depth, tail), F32)], axis=1)[..., None]
    nt = n_sample_t
    return dict(
        norm1=row(norm1), w_in=(w_in * in_scale).astype(BF16), ln_g=row(ln_g), ln_b=row(ln_b),
        ws=ws.astype(BF16), bs=bs.astype(F32), conv_w=conv_w.astype(F32), conv_b=row(conv_b),
        lam=row(lru_lambda),
        wbd=(0.5 * jnp.concatenate([pair_bd(w_rg_a), pair_bd(w_rg_x)], axis=-1)).astype(BF16),
        b_a=row(0.5 * b_rg_a), b_x=row(0.5 * b_rg_x), w_out=(0.5 * w_out).astype(BF16),
        norm2=row(norm2), w_route=w_route.astype(BF16), b_route=b_route.astype(F32),
        w_gate=w_gate, w_up=w_up, w_down=w_down,
        wsc=jnp.repeat(jnp.transpose(ws[:, :, :nt, :nt], (0, 2, 3, 1)), dh, axis=3).astype(F32),
        bsc=bs[:, :nt].astype(F32),
    )


def kernel(x_prompt, x_sample, state_lru_h, state_conv, norm1, w_in, ln_g, ln_b, w_s, b_s, conv_w, conv_b, lru_lambda, w_rg_a, b_rg_a, w_rg_x, b_rg_x, w_out, norm2, w_route_group, b_route_group, w_route_expert, b_route_expert, w_gate, w_up, w_down, norm_f):
    depth = w_in.shape[0]
    nb, seq, d = x_prompt.shape
    ns, nt, _ = x_sample.shape
    assert nt <= CHUNK
    nf = norm_f.reshape(1, d).astype(F32)
    wts = _prep_weights(norm1, w_in, ln_g, ln_b, w_s, b_s, conv_w, conv_b, lru_lambda, w_rg_a,
                        b_rg_a, w_rg_x, b_rg_x, w_out, norm2, w_route_group, b_route_group,
                        w_route_expert, b_route_expert, w_gate, w_up, w_down, nt)
    conv_state = jnp.transpose(state_conv, (0, 2, 1, 3))
    assert (nb * seq) % MOE_TILE == 0
    sample_pad = (-nt * ns) % MOE_TILE
    assert sample_pad % ns == 0
    pad_t = sample_pad // ns
    yp = x_prompt
    ys = jnp.transpose(x_sample, (1, 0, 2))
    hp, cp, vp, hs, cs, vs = [], [], [], [], [], []
    for l in range(depth):
        yp, h, c, v = _mixer_prompt(yp, wts, l)
        hp.append(h)
        cp.append(c)
        vp.append(v)
        ys, h, c, v = _mixer_sample(ys, state_lru_h, conv_state, wts, l, nt, pad_t)
        hs.append(h)
        cs.append(c)
        vs.append(v)
        yp, ys = _moe(yp.reshape(nb * seq, d), ys.reshape((nt + pad_t) * ns, d), wts, l, nf,
                      l == depth - 1, MOE_TILE)
        yp = yp.reshape(nb, seq, d)
        ys = ys.reshape(nt + pad_t, ns, d)
    to_seq_major = lambda a: jnp.transpose(jnp.stack(a), (0, 2, 1, 3))
    return (yp, jnp.transpose(ys[:nt], (1, 0, 2)), jnp.stack(hp), jnp.stack(cp), jnp.stack(vp),
            jnp.stack(hs), to_seq_major(cs), to_seq_major(vs))
```

```python
import functools

import jax
import jax.numpy as jnp
from jax import lax
from jax.experimental import pallas as pl
from jax.experimental.pallas import tpu as pltpu

F32 = jnp.float32
BF16 = jnp.bfloat16

CHUNK = 128
H_A = 4
H_B = 8
CONV_W = 4
LRU_C = 8.0
N_GROUPS = 4
E_PER_GROUP = 8
N_EXPERTS = N_GROUPS * E_PER_GROUP
EPS = 1e-6

LANES = 128
SUBLANES = 8
BF16_ROWS = 16
V7X_VMEM_BYTES = 64 * 1024 * 1024
VMEM_LIMIT = V7X_VMEM_BYTES - 4 * 1024 * 1024
SCAN_UNROLL = 4
SEQS_PER_TRIP = 8
MOE_TILE = 1024
ROUTE_SPAN = 256
MOE_BLOCK = 512
EXPERT_ROW_RING = 4
EXPERT_WEIGHT_RING = 3
assert E_PER_GROUP == SUBLANES and N_GROUPS <= SUBLANES
ROUTE_ROWS = -(-SUBLANES * (N_GROUPS + 1) // BF16_ROWS) * BF16_ROWS


def _dot(a, b):
    return jnp.dot(a, b, preferred_element_type=F32)


def _sigmoid(x):
    return 0.5 * (jnp.tanh(0.5 * x) + 1.0)


def _gelu2(x):
    c = 0.7978845608028654
    return x * (1.0 + jnp.tanh(x * (c + (c * 0.044715) * (x * x))))


def _rms_norm(x, g):
    return x * lax.rsqrt(jnp.mean(x * x, axis=-1, keepdims=True) + EPS) * g


def _layer_norm(x, g, b, eps):
    xc = x - jnp.mean(x, axis=-1, keepdims=True)
    var = jnp.mean(xc * xc, axis=-1, keepdims=True)
    return xc * lax.rsqrt(var + eps) * g + b


def _softplus(x):
    return jnp.maximum(x, 0.0) + jnp.log(1.0 + jnp.exp(-jnp.abs(x)))


def _lru_gates(xc, wbd_ref, b_a, b_x, lam_half):
    d = xc.shape[-1]
    xcb = xc.astype(BF16)
    r_parts, i_parts = [], []
    for q in range(d // 256):
        ri = _dot(xcb[:, q * 256:(q + 1) * 256], wbd_ref[q])
        r_parts.append(ri[:, :256])
        i_parts.append(ri[:, 256:])
    r2 = 1.0 + jnp.tanh(jnp.concatenate(r_parts, axis=-1) + b_a)
    i2 = 1.0 + jnp.tanh(jnp.concatenate(i_parts, axis=-1) + b_x)
    a = jnp.exp(lam_half * r2)
    mult_half = jnp.sqrt(0.25 - 0.25 * (a * a))
    return a, mult_half, i2 * xc


def _shift_rows(x, hist, j):
    sh = pltpu.roll(x, j, axis=0)
    top = jnp.where(lax.broadcasted_iota(jnp.int32, (SUBLANES, 1), 0) < j,
                    pltpu.roll(hist, j, axis=0), sh[:SUBLANES])
    return jnp.concatenate([top, sh[SUBLANES:]], axis=0)


def _mixer_prompt_kernel(x_ref, norm_ref, win_ref, lng_ref, lnb_ref, ws_ref, bs_ref,
                         cw_ref, cb_ref, lam_ref, wbd_ref, ba_ref, bx_ref, wout_ref,
                         y_ref, h_ref, conv_ref, v_ref,
                         hist_s, a_s, b_s, ya_s, gb_s, hc_s):
    i = pl.program_id(0)
    last = pl.num_programs(0) - 1
    nseq, _, d = x_ref.shape
    nlt = d // LANES
    spt = SEQS_PER_TRIP
    pair = spt * CHUNK

    @pl.when(i == 0)
    def _():
        hist_s[...] = jnp.zeros_like(hist_s)
        hc_s[...] = jnp.zeros_like(hc_s)

    lam_half = (-0.5 * LRU_C) * _softplus(-lam_ref[...])
    is_first = i == 0

    def branch_body(p, c):
        row = pl.multiple_of(p * pair, pair)
        x = jnp.concatenate([x_ref[spt * p + k] for k in range(spt)], axis=0)
        xnb = _rms_norm(x, norm_ref[...]).astype(BF16)
        v = _layer_norm(_gelu2(_dot(xnb, win_ref[:, d:2 * d])), lng_ref[...], lnb_ref[...],
                        4.0 * EPS)

        @pl.when(i == last)
        def _():
            for k in range(spt):
                v_ref[spt * p + k] = v[k * CHUNK:(k + 1) * CHUNK]

        vb = v.astype(BF16)
        dh = d // H_A
        z_rows = []
        for k in range(spt):
            zs = [_dot(ws_ref[h], vb[k * CHUNK:(k + 1) * CHUNK, h * dh:(h + 1) * dh])
                  for h in range(H_A)]
            z_rows.append(jnp.concatenate(zs, axis=-1) + bs_ref[...])
        z = jnp.concatenate(z_rows, axis=0)
        ya = _gelu2(_dot(xnb, win_ref[:, 0:d])) * z
        ga2 = 1.0 + jnp.tanh(_dot(xnb, win_ref[:, 3 * d:4 * d]))
        ya_s[pl.ds(row, pair), :] = (ga2 * ya).astype(BF16)
        gb_s[pl.ds(row, pair), :] = (1.0 + jnp.tanh(_dot(xnb, win_ref[:, 4 * d:5 * d]))).astype(BF16)
        xr = _dot(xnb, win_ref[:, 2 * d:3 * d])
        xcs = []
        for k in range(spt):
            s = spt * p + k
            xk = xr[k * CHUNK:(k + 1) * CHUNK]
            hist = hist_s[s]
            xc = cb_ref[...] + xk * cw_ref[CONV_W - 1:CONV_W, :]
            for j in range(1, CONV_W):
                xc = xc + _shift_rows(xk, hist, j) * cw_ref[CONV_W - 1 - j:CONV_W - j, :]
            hist_s[s] = xk[CHUNK - SUBLANES:]
            xcs.append(xc)
        xc = jnp.concatenate(xcs, axis=0)
        a, mult_half, ix2 = _lru_gates(xc, wbd_ref, ba_ref[...], bx_ref[...], lam_half)
        bt = mult_half * ix2
        for k in range(spt):
            s = spt * p + k
            for j in range(nlt):
                a_s[j, pl.ds(s, CHUNK, stride=nseq), :] = a[k * CHUNK:(k + 1) * CHUNK, j * LANES:(j + 1) * LANES]
                b_s[j, pl.ds(s, CHUNK, stride=nseq), :] = bt[k * CHUNK:(k + 1) * CHUNK, j * LANES:(j + 1) * LANES]

        @pl.when(is_first)
        def _():
            for k in range(spt):
                first = 0.5 * ix2[k * CHUNK:k * CHUNK + 1]
                for j in range(nlt):
                    b_s[j, pl.ds(spt * p + k, 1), :] = first[:, j * LANES:(j + 1) * LANES]

        return c

    lax.fori_loop(0, nseq // spt, branch_body, 0)

    def scan_body(tt, hs):
        for u in range(SCAN_UNROLL):
            row = pl.multiple_of((tt * SCAN_UNROLL + u) * nseq, nseq)
            out = []
            for j in range(nlt):
                h = a_s[j, pl.ds(row, nseq), :] * hs[j] + b_s[j, pl.ds(row, nseq), :]
                b_s[j, pl.ds(row, nseq), :] = h
                out.append(h)
            hs = tuple(out)
        return hs

    h0 = tuple(hc_s[:, j * LANES:(j + 1) * LANES] for j in range(nlt))
    hs = lax.fori_loop(0, CHUNK // SCAN_UNROLL, scan_body, h0)
    hfin = jnp.concatenate(hs, axis=-1)
    hc_s[...] = hfin
    h_ref[...] = hfin

    def out_body(p, c):
        row = pl.multiple_of(p * pair, pair)
        hrows = []
        for k in range(spt):
            hrows.append(jnp.concatenate(
                [b_s[j, pl.ds(spt * p + k, CHUNK, stride=nseq), :] for j in range(nlt)], axis=-1))
        h = jnp.concatenate(hrows, axis=0)
        merged = (ya_s[pl.ds(row, pair), :].astype(F32)
                  + gb_s[pl.ds(row, pair), :].astype(F32) * h).astype(BF16)
        o = _dot(merged, wout_ref[...])
        for k in range(spt):
            y_ref[spt * p + k] = x_ref[spt * p + k] + o[k * CHUNK:(k + 1) * CHUNK]
        return c

    lax.fori_loop(0, nseq // spt, out_body, 0)

    @pl.when(i == last)
    def _():
        conv_ref[...] = hist_s[:, SUBLANES - (CONV_W - 1):, :]


def _const_spec(shape):
    nd = len(shape)
    return pl.BlockSpec(shape, lambda i, *_, _n=nd: (0,) * _n, pipeline_mode=pl.Buffered(1))


def _layer_spec(stacked, layer):
    shape = stacked.shape[1:]
    nd = len(shape)
    return pl.BlockSpec((None,) + tuple(shape), lambda i, *_, _n=nd: (layer,) + (0,) * _n,
                        pipeline_mode=pl.Buffered(1))


MIXER_WEIGHTS = ('norm1', 'w_in', 'ln_g', 'ln_b', 'ws', 'bs', 'conv_w', 'conv_b', 'lam', 'wbd',
                 'b_a', 'b_x', 'w_out')


def _mixer_prompt(x, wts, layer):
    nseq, seq, d = x.shape
    assert seq % CHUNK == 0 and nseq == SUBLANES and d % 256 == 0
    n_chunks = seq // CHUNK
    nlt = d // LANES
    weights = tuple(wts[k] for k in MIXER_WEIGHTS)
    x_spec = pl.BlockSpec((nseq, CHUNK, d), lambda i: (0, i, 0))
    out_shape = (
        jax.ShapeDtypeStruct((nseq, seq, d), F32),
        jax.ShapeDtypeStruct((nseq, d), F32),
        jax.ShapeDtypeStruct((nseq, CONV_W - 1, d), F32),
        jax.ShapeDtypeStruct((nseq, CHUNK, d), F32),
    )
    out_specs = (
        x_spec,
        _const_spec((nseq, d)),
        _const_spec((nseq, CONV_W - 1, d)),
        _const_spec((nseq, CHUNK, d)),
    )
    scratch = [
        pltpu.VMEM((nseq, SUBLANES, d), F32),
        pltpu.VMEM((nlt, CHUNK * nseq, LANES), F32),
        pltpu.VMEM((nlt, CHUNK * nseq, LANES), F32),
        pltpu.VMEM((nseq * CHUNK, d), BF16),
        pltpu.VMEM((nseq * CHUNK, d), BF16),
        pltpu.VMEM((nseq, d), F32),
    ]
    return pl.pallas_call(
        _mixer_prompt_kernel,
        grid=(n_chunks,),
        in_specs=[x_spec] + [_layer_spec(w, layer) for w in weights],
        out_specs=out_specs,
        out_shape=out_shape,
        scratch_shapes=scratch,
        compiler_params=pltpu.CompilerParams(
            dimension_semantics=("arbitrary",), vmem_limit_bytes=VMEM_LIMIT),
        name="mixer_prompt",
    )(x, *weights)


def _mixer_sample_kernel(x_ref, h0_ref, cst_ref, norm_ref, win_ref, lng_ref, lnb_ref, wsc_ref,
                         bsc_ref, cw_ref, cb_ref, lam_ref, wbd_ref, ba_ref, bx_ref, wout_ref,
                         y_ref, h_ref, conv_ref, v_ref):
    nt, ns, d = x_ref.shape
    at = lambda a, t: a[t * ns:(t + 1) * ns]
    lam_half = (-0.5 * LRU_C) * _softplus(-lam_ref[...])
    x = x_ref[...].reshape(nt * ns, d)
    xnb = _rms_norm(x, norm_ref[...]).astype(BF16)
    v = _layer_norm(_gelu2(_dot(xnb, win_ref[:, d:2 * d])), lng_ref[...], lnb_ref[...], 4.0 * EPS)
    v_ref[...] = v.reshape(nt, ns, d)
    zs = []
    for t in range(nt):
        z = bsc_ref[t:t + 1, :]
        for s in range(t + 1):
            z = z + wsc_ref[t, s:s + 1, :] * at(v, s)
        zs.append(z)
    ya = _gelu2(_dot(xnb, win_ref[:, 0:d])) * jnp.concatenate(zs, axis=0)
    ga2 = 1.0 + jnp.tanh(_dot(xnb, win_ref[:, 3 * d:4 * d]))
    gb2 = 1.0 + jnp.tanh(_dot(xnb, win_ref[:, 4 * d:5 * d]))
    xr = _dot(xnb, win_ref[:, 2 * d:3 * d])
    xp = [cst_ref[k] for k in range(CONV_W - 1)] + [at(xr, t) for t in range(nt)]
    xcs = []
    for t in range(nt):
        xc = cb_ref[...]
        for k in range(CONV_W):
            xc = xc + xp[t + k] * cw_ref[k:k + 1, :]
        xcs.append(xc)
    a, mult_half, ix2 = _lru_gates(jnp.concatenate(xcs, axis=0), wbd_ref, ba_ref[...],
                                   bx_ref[...], lam_half)
    bt = mult_half * ix2
    h = h0_ref[...]
    hs = []
    for t in range(nt):
        h = at(a, t) * h + at(bt, t)
        hs.append(h)
    merged = (ga2 * ya + gb2 * jnp.concatenate(hs, axis=0)).astype(BF16)
    y_ref[0:nt] = (x + _dot(merged, wout_ref[...])).reshape(nt, ns, d)
    if y_ref.shape[0] > nt:
        y_ref[nt:] = jnp.zeros((y_ref.shape[0] - nt, ns, d), F32)
    h_ref[...] = h
    for k in range(CONV_W - 1):
        conv_ref[k] = xp[nt + k]


def _mixer_sample(x, h0, conv_state, wts, layer, nt, pad_t):
    _, ns, d = x.shape
    names = MIXER_WEIGHTS[:4] + ('wsc', 'bsc') + MIXER_WEIGHTS[6:]
    stacked = (h0, conv_state) + tuple(wts[k] for k in names)
    out_shape = (
        jax.ShapeDtypeStruct((nt + pad_t, ns, d), F32),
        jax.ShapeDtypeStruct((ns, d), F32),
        jax.ShapeDtypeStruct((CONV_W - 1, ns, d), F32),
        jax.ShapeDtypeStruct((nt, ns, d), F32),
    )
    return pl.pallas_call(
        _mixer_sample_kernel,
        grid=(1,),
        in_specs=[_const_spec((nt, ns, d))] + [_layer_spec(w, layer) for w in stacked],
        out_specs=tuple(_const_spec(s.shape) for s in out_shape),
        out_shape=out_shape,
        compiler_params=pltpu.CompilerParams(
            dimension_semantics=("arbitrary",), vmem_limit_bytes=VMEM_LIMIT),
        name="mixer_sample",
    )(x, *stacked)


def _to_slab(ref, val):
    t, d = val.shape
    nlt = d // LANES
    for j in range(nlt):
        ref[pl.ds(j, t, stride=nlt), :] = val[:, j * LANES:(j + 1) * LANES]


def _from_slab(ref, t, nlt):
    return jnp.concatenate([ref[pl.ds(j, t, stride=nlt), :] for j in range(nlt)], axis=-1)


def _tokens_from_slab(slab_ref, nlt):
    return slab_ref[...].reshape(slab_ref.shape[0] // nlt, nlt, LANES).astype(BF16)


def _tokens_to_slab(slab_ref, tokens):
    slab_ref[...] = tokens.astype(F32).reshape(slab_ref.shape)


def _router_kernel(yp_ref, ys_ref, norm_ref, wr_ref, br_ref, tri_ref, low_ref, xn_ref, s1_ref,
                   s2_ref, w1_ref, w2_ref, cnt_ref, slab_s, *, n_prompt_tiles):
    y = jnp.where(pl.program_id(0) < n_prompt_tiles, yp_ref[...], ys_ref[...])
    xn = _rms_norm(y, norm_ref[...])
    _to_slab(slab_s, xn)
    xn_ref[...] = _tokens_from_slab(slab_s, xn_ref.shape[1])
    lt = lax.dot_general(wr_ref[...], xn.astype(BF16), (((1,), (1,)), ((), ())),
                         preferred_element_type=F32) + br_ref[...]
    tile = lt.shape[1]
    row = lax.broadcasted_iota(jnp.int32, (SUBLANES, tile), 0).astype(F32)
    neg = jnp.float32(-jnp.inf)
    big = jnp.float32(SUBLANES)

    def argmax_first(vals):
        m = jnp.max(vals, axis=0, keepdims=True)
        return m, jnp.min(jnp.where(vals == m, row, big), axis=0, keepdims=True)

    def group_block(x, gi):
        out = x[SUBLANES:2 * SUBLANES]
        for g in range(1, N_GROUPS):
            out = jnp.where(gi == g, x[(g + 1) * SUBLANES:(g + 2) * SUBLANES], out)
        return out

    gl = jnp.where(row < N_GROUPS, lt[0:SUBLANES], neg)
    gm, gi = argmax_first(gl)
    pg_top = 1.0 / jnp.sum(jnp.exp(gl - gm), axis=0, keepdims=True)
    el = group_block(lt, gi)
    m1, i1 = argmax_first(el)
    m2, i2 = argmax_first(jnp.where(row == i1, neg, el))
    e2 = jnp.exp(m2 - m1)
    w1 = pg_top / (1.0 + e2)
    w2 = pg_top * e2 / (1.0 + e2)
    hot1 = row == i1
    hot2 = row == i2
    picked = jnp.where(jnp.logical_or(hot1, hot2), 1.0, 0.0)
    zero = jnp.zeros_like(picked)
    onehot = jnp.concatenate(
        [zero] + [jnp.where(gi == g, picked, 0.0) for g in range(N_GROUPS)] + [zero],
        axis=0).astype(BF16)
    span = tri_ref.shape[0]
    parts = []
    cnt = jnp.zeros((ROUTE_ROWS, 1), F32)
    for k in range(tile // span):
        blk = onehot[:, k * span:(k + 1) * span]
        parts.append(_dot(blk, tri_ref[...]) + cnt)
        cnt = cnt + jnp.sum(blk.astype(F32), axis=1, keepdims=True)
    before = jnp.concatenate(parts, axis=1)
    lower = jnp.sum(_dot(low_ref[...], onehot), axis=1, keepdims=True)
    slot = group_block(before + lower, gi)
    p1 = jnp.sum(jnp.where(hot1, slot, 0.0), axis=0, keepdims=True)
    p2 = jnp.sum(jnp.where(hot2, slot, 0.0), axis=0, keepdims=True)
    s1_ref[0] = p1.astype(jnp.int32)
    s2_ref[0] = p2.astype(jnp.int32)
    w1_ref[0] = w1
    w2_ref[0] = w2
    cnt_ref[0] = jnp.broadcast_to(cnt, cnt_ref.shape[1:])


def _token_specs(tile, d, n_prompt_tiles):
    last = n_prompt_tiles - 1
    return (pl.BlockSpec((tile, d), lambda i, *_: (jnp.minimum(i, last), 0)),
            pl.BlockSpec((tile, d), lambda i, *_: (jnp.maximum(i - n_prompt_tiles, 0), 0)))


def _router(yp, ys, wts, layer, tile):
    d = yp.shape[1]
    nlt = d // LANES
    n_prompt_tiles = yp.shape[0] // tile
    n_tiles = n_prompt_tiles + ys.shape[0] // tile
    span = min(tile, ROUTE_SPAN)
    assert tile % span == 0
    tri = jnp.triu(jnp.ones((span, span), BF16), 1)
    low = jnp.tril(jnp.ones((ROUTE_ROWS, ROUTE_ROWS), BF16), -1)
    per_token = pl.BlockSpec((1, 1, tile), lambda i: (i, 0, 0))
    return pl.pallas_call(
        functools.partial(_router_kernel, n_prompt_tiles=n_prompt_tiles),
        grid=(n_tiles,),
        in_specs=list(_token_specs(tile, d, n_prompt_tiles)) + [
            _layer_spec(wts['norm2'], layer), _layer_spec(wts['w_route'], layer),
            _layer_spec(wts['b_route'], layer), _const_spec((span, span)),
            _const_spec((ROUTE_ROWS, ROUTE_ROWS))],
        out_specs=(pl.BlockSpec((tile, nlt, LANES), lambda i: (i, 0, 0)),
                   per_token, per_token, per_token, per_token,
                   pl.BlockSpec((1, ROUTE_ROWS, LANES), lambda i: (i, 0, 0))),
        out_shape=(jax.ShapeDtypeStruct((n_tiles * tile, nlt, LANES), BF16),
                   jax.ShapeDtypeStruct((n_tiles, 1, tile), jnp.int32),
                   jax.ShapeDtypeStruct((n_tiles, 1, tile), jnp.int32),
                   jax.ShapeDtypeStruct((n_tiles, 1, tile), F32),
                   jax.ShapeDtypeStruct((n_tiles, 1, tile), F32),
                   jax.ShapeDtypeStruct((n_tiles, ROUTE_ROWS, LANES), F32)),
        scratch_shapes=[pltpu.VMEM((tile * nlt, LANES), F32)],
        compiler_params=pltpu.CompilerParams(
            dimension_semantics=("arbitrary",), vmem_limit_bytes=VMEM_LIMIT),
        name="moe_router",
    )(yp, ys, wts['norm2'], wts['w_route'], wts['b_route'], tri, low)


def _run_copy(tile_idx, e, lo_ref, cnt_ref, dst_ref, local, remote, sem, to_remote):
    k = tile_idx * N_EXPERTS + e
    loc = local.at[pl.ds(lo_ref[k], cnt_ref[k])]
    rem = remote.at[pl.ds(dst_ref[k], cnt_ref[k])]
    return pltpu.make_async_copy(loc, rem, sem) if to_remote else pltpu.make_async_copy(rem, loc, sem)


def _start_runs(tile_idx, lo_ref, cnt_ref, dst_ref, local, remote, sem, to_remote):
    for e in range(N_EXPERTS):
        @pl.when(cnt_ref[tile_idx * N_EXPERTS + e] > 0)
        def _():
            _run_copy(tile_idx, e, lo_ref, cnt_ref, dst_ref, local, remote, sem, to_remote).start()


def _wait_runs(local, sem):
    pltpu.make_async_copy(local, local, sem).wait()


def _dispatch_kernel(lo_ref, cnt_ref, dst_ref, padst_ref, padn_ref,
                     s1_ref, s2_ref, x_ref, xs_ref, loc, zeros, sem, *, unroll):
    i = pl.program_id(0)
    last = pl.num_programs(0) - 1
    tile = x_ref.shape[0]
    slot = i % 2
    buf = loc.at[slot]

    @pl.when(i >= 2)
    def _():
        _wait_runs(buf, sem.at[slot])

    def body(tt, c):
        for u in range(unroll):
            t = tt * unroll + u
            row = x_ref[t]
            buf[s1_ref[t]] = row
            buf[s2_ref[t]] = row
        return c

    lax.fori_loop(0, tile // unroll, body, 0)
    _start_runs(i, lo_ref, cnt_ref, dst_ref, buf, xs_ref, sem.at[slot], True)

    @pl.when(i == last)
    def _():
        zeros[...] = jnp.zeros_like(zeros)

        def pad_copy(e):
            n = padn_ref[e]
            dst = xs_ref.at[pl.ds(padst_ref[e], n)]
            return pltpu.make_async_copy(zeros.at[pl.ds(0, n)], dst, sem.at[2])

        def each_pad(fn):
            def b(e, c):
                @pl.when(padn_ref[e] > 0)
                def _():
                    fn(e)
                return c
            lax.fori_loop(0, padn_ref.shape[0], b, 0)

        each_pad(lambda e: pad_copy(e).start())
        _wait_runs(buf, sem.at[slot])

        @pl.when(i >= 1)
        def _():
            _wait_runs(loc.at[1 - slot], sem.at[1 - slot])

        each_pad(lambda e: pad_copy(e).wait())


def _smem_tile(tile):
    return pl.BlockSpec((None, None, tile), lambda i, *_: (i, 0, 0), memory_space=pltpu.SMEM)


def _dispatch(xn_slab, s1, s2, tables, tile, n_rows):
    token = xn_slab.shape[1:]
    return pl.pallas_call(
        functools.partial(_dispatch_kernel, unroll=8),
        grid_spec=pltpu.PrefetchScalarGridSpec(
            num_scalar_prefetch=5,
            grid=(xn_slab.shape[0] // tile,),
            in_specs=[_smem_tile(tile), _smem_tile(tile),
                      pl.BlockSpec((tile,) + token, lambda i, *_: (i, 0, 0))],
            out_specs=pl.BlockSpec(memory_space=pl.ANY),
            scratch_shapes=[pltpu.VMEM((2, 2 * tile) + token, xn_slab.dtype),
                            pltpu.VMEM((MOE_BLOCK,) + token, xn_slab.dtype),
                            pltpu.SemaphoreType.DMA((3,))]),
        out_shape=jax.ShapeDtypeStruct((n_rows,) + token, xn_slab.dtype),
        compiler_params=pltpu.CompilerParams(
            dimension_semantics=("arbitrary",), vmem_limit_bytes=VMEM_LIMIT),
        name="moe_dispatch",
    )(*tables, s1, s2, xn_slab)


def _experts_kernel(blk_ref, next_ref, nused_ref, xs_ref, wg_ref, wu_ref, wd_ref, ys_ref,
                    xin, yout, wg_f, wu_f, wd_f, wg_s, wu_s, wd_s, slab_in,
                    sem_in, sem_out, sem_w, *, layer):
    n_blocks = ys_ref.shape[0] // MOE_BLOCK
    nlt = xs_ref.shape[1]
    n_used = nused_ref[0]
    block = lambda ref, b: ref.at[pl.ds(pl.multiple_of(b * MOE_BLOCK, MOE_BLOCK), MOE_BLOCK)]
    ring = xin.shape[0]
    rows_in = lambda b: pltpu.make_async_copy(block(xs_ref, b), xin.at[b % ring],
                                              sem_in.at[b % ring])
    rows_out = lambda b, slot: pltpu.make_async_copy(yout.at[slot], block(ys_ref, b),
                                                     sem_out.at[slot])

    def weights(e, slot):
        pairs = ((wg_ref, wg_f), (wu_ref, wu_f), (wd_ref, wd_f))
        return [pltpu.make_async_copy(src.at[layer, e], dst.at[slot], sem_w.at[slot])
                for src, dst in pairs]

    for k in range(ring - 1):
        @pl.when(k < n_used)
        def _():
            rows_in(k).start()

    wring = wg_f.shape[0]

    def later_expert(e, hops):
        for _ in range(hops):
            e = jnp.where(e >= 0, next_ref[jnp.maximum(e, 0)], -1)
        return e

    @pl.when(n_used > 0)
    def _():
        for k in range(wring - 1):
            e_k = later_expert(blk_ref[0], k)

            @pl.when(e_k >= 0)
            def _():
                for c in weights(e_k, k):
                    c.start()

    def step(b, wcount):
        slot = b % 2
        e = blk_ref[b]
        rows_in(b).wait()

        @pl.when(b + ring - 1 < n_used)
        def _():
            rows_in(b + ring - 1).start()

        new_expert = jnp.logical_or(b == 0, e != blk_ref[jnp.maximum(b - 1, 0)])
        wcount = jnp.where(new_expert, wcount + 1, wcount)

        @pl.when(new_expert)
        def _():
            wslot = wcount % wring
            for c in weights(e, wslot):
                c.wait()
            wg_s[...] = wg_f[wslot].astype(BF16)
            wu_s[...] = wu_f[wslot].astype(BF16)
            wd_s[...] = wd_f[wslot].astype(BF16)
            e_far = later_expert(e, wring - 1)

            @pl.when(e_far >= 0)
            def _():
                for c in weights(e_far, (wcount + wring - 1) % wring):
                    c.start()

        _tokens_to_slab(slab_in, xin[b % ring])
        x = _from_slab(slab_in, MOE_BLOCK, nlt).astype(BF16)
        hg = _dot(x, wg_s[...])
        hu = _dot(x, wu_s[...])
        h = (hg * _sigmoid(hg) * hu).astype(BF16)
        o = _dot(h, wd_s[...]).reshape(MOE_BLOCK, nlt, LANES).astype(BF16)

        @pl.when(b >= 2)
        def _():
            rows_out(b - 2, slot).wait()

        yout[slot] = o
        rows_out(b, slot).start()
        return wcount

    lax.fori_loop(0, n_used, step, jnp.int32(-1))

    @pl.when(n_used >= 2)
    def _():
        rows_out(n_used - 2, n_used % 2).wait()

    @pl.when(n_used >= 1)
    def _():
        rows_out(n_used - 1, (n_used - 1) % 2).wait()

    yout[0] = jnp.zeros(yout.shape[1:], yout.dtype)

    def fill(b, c):
        rows_out(b, 0).start()
        rows_out(b, 0).wait()
        return c

    lax.fori_loop(n_used, n_blocks, fill, 0)


def _experts(xs, blk_expert, next_expert, n_used, wts, layer):
    n_rows, nlt, _ = xs.shape
    d = nlt * LANES
    f = wts['w_gate'].shape[-1]
    token = (nlt, LANES)
    any_spec = pl.BlockSpec(memory_space=pl.ANY)
    return pl.pallas_call(
        functools.partial(_experts_kernel, layer=layer),
        grid_spec=pltpu.PrefetchScalarGridSpec(
            num_scalar_prefetch=3,
            grid=(1,),
            in_specs=[any_spec] * 4,
            out_specs=any_spec,
            scratch_shapes=[pltpu.VMEM((EXPERT_ROW_RING, MOE_BLOCK) + token, xs.dtype),
                            pltpu.VMEM((2, MOE_BLOCK) + token, xs.dtype),
                            pltpu.VMEM((EXPERT_WEIGHT_RING, d, f), F32),
                            pltpu.VMEM((EXPERT_WEIGHT_RING, d, f), F32),
                            pltpu.VMEM((EXPERT_WEIGHT_RING, f, d), F32),
                            pltpu.VMEM((d, f), BF16), pltpu.VMEM((d, f), BF16),
                            pltpu.VMEM((f, d), BF16),
                            pltpu.VMEM((MOE_BLOCK * nlt, LANES), F32),
                            pltpu.SemaphoreType.DMA((EXPERT_ROW_RING,)),
                            pltpu.SemaphoreType.DMA((2,)),
                            pltpu.SemaphoreType.DMA((EXPERT_WEIGHT_RING,))]),
        out_shape=jax.ShapeDtypeStruct(xs.shape, xs.dtype),
        compiler_params=pltpu.CompilerParams(
            dimension_semantics=("arbitrary",), vmem_limit_bytes=VMEM_LIMIT),
        name="moe_experts",
    )(blk_expert, next_expert, n_used, xs, wts['w_gate'], wts['w_up'], wts['w_down'])


def _combine_kernel(lo_ref, cnt_ref, dst_ref, s1_ref, s2_ref, w1_ref, w2_ref, yp_ref, ys_ref,
                    nf_ref, rows_ref, op_ref, os_ref, loc, acc, sem,
                    *, unroll, final_norm, n_prompt_tiles):
    i = pl.program_id(0)
    tile = yp_ref.shape[0]
    nlt = loc.shape[2]
    slot = i % 2
    buf = loc.at[slot]
    fetch = lambda t, s: _start_runs(t, lo_ref, cnt_ref, dst_ref, loc.at[s], rows_ref, sem.at[s],
                                     False)

    @pl.when(i == 0)
    def _():
        fetch(0, 0)

    @pl.when(i + 1 < pl.num_programs(0))
    def _():
        fetch(i + 1, 1 - slot)

    _wait_runs(buf, sem.at[slot])

    def body(tt, c):
        for u in range(unroll):
            t = tt * unroll + u
            r1 = buf[s1_ref[t]].astype(F32)
            r2 = buf[s2_ref[t]].astype(F32)
            acc[pl.ds(pl.multiple_of(t * nlt, nlt), nlt), :] = w1_ref[t] * r1 + w2_ref[t] * r2
        return c

    lax.fori_loop(0, tile // unroll, body, 0)
    def finish(y_ref, o_ref):
        out = y_ref[...] + acc[...].reshape(tile, nlt, LANES).reshape(tile, nlt * LANES)
        if final_norm:
            out = _rms_norm(out, nf_ref[...])
        o_ref[...] = out

    is_prompt = i < n_prompt_tiles
    pl.when(is_prompt)(lambda: finish(yp_ref, op_ref))
    pl.when(jnp.logical_not(is_prompt))(lambda: finish(ys_ref, os_ref))


def _combine(yp, ys, rows, meta, tables, norm_f, final_norm, tile):
    d = yp.shape[1]
    token = rows.shape[1:]
    n_prompt_tiles = yp.shape[0] // tile
    n_tiles = n_prompt_tiles + ys.shape[0] // tile
    tok_p, tok_s = _token_specs(tile, d, n_prompt_tiles)
    return pl.pallas_call(
        functools.partial(_combine_kernel, unroll=8, final_norm=final_norm,
                          n_prompt_tiles=n_prompt_tiles),
        grid_spec=pltpu.PrefetchScalarGridSpec(
            num_scalar_prefetch=3,
            grid=(n_tiles,),
            in_specs=[_smem_tile(tile)] * 4 + [tok_p, tok_s, _const_spec((1, d)),
                                               pl.BlockSpec(memory_space=pl.ANY)],
            out_specs=(tok_p, tok_s),
            scratch_shapes=[pltpu.VMEM((2, 2 * tile) + token, rows.dtype),
                            pltpu.VMEM((tile * token[0], LANES), F32),
                            pltpu.SemaphoreType.DMA((2,))]),
        out_shape=(jax.ShapeDtypeStruct(yp.shape, F32), jax.ShapeDtypeStruct(ys.shape, F32)),
        compiler_params=pltpu.CompilerParams(
            dimension_semantics=("arbitrary",), vmem_limit_bytes=VMEM_LIMIT),
        name="moe_combine",
    )(*tables, *meta, yp, ys, norm_f, rows)


def _moe(yp, ys, wts, layer, norm_f, final_norm, tile):
    d = yp.shape[1]
    n = yp.shape[0] + ys.shape[0]
    assert yp.shape[0] % tile == 0 and ys.shape[0] % tile == 0 and d % LANES == 0
    xn_slab, s1, s2, w1, w2, cnt = _router(yp, ys, wts, layer, tile)
    cnt = cnt[:, SUBLANES:SUBLANES + N_EXPERTS, 0].astype(jnp.int32)
    lo = jnp.cumsum(cnt, axis=1) - cnt
    seg = jnp.sum(cnt, axis=0)
    seg_pad = (seg + MOE_BLOCK - 1) // MOE_BLOCK * MOE_BLOCK
    seg_end = jnp.cumsum(seg_pad)
    seg_start = seg_end - seg_pad
    dst = seg_start[None, :] + jnp.cumsum(cnt, axis=0) - cnt
    n_blocks = (2 * n + N_EXPERTS * (MOE_BLOCK - 1)) // MOE_BLOCK
    blk_ids = jnp.arange(n_blocks, dtype=jnp.int32)
    blk_expert = jnp.minimum(
        jnp.sum((seg_end[None, :] <= blk_ids[:, None] * MOE_BLOCK).astype(jnp.int32), axis=1),
        N_EXPERTS - 1)
    n_used = seg_end[-1:] // MOE_BLOCK
    pad_start = jnp.concatenate([seg_start + seg, blk_ids * MOE_BLOCK])
    pad_rows = jnp.concatenate([seg_pad - seg, jnp.where(blk_ids >= n_used[0], MOE_BLOCK, 0)])
    flat = lambda a: a.reshape(-1)
    runs = (flat(lo), flat(cnt), flat(dst))
    ids = jnp.arange(N_EXPERTS, dtype=jnp.int32)
    later = jnp.where((seg_pad[None, :] > 0) & (ids[None, :] > ids[:, None]), ids[None, :],
                      N_EXPERTS)
    next_expert = jnp.min(later, axis=1)
    next_expert = jnp.where(next_expert < N_EXPERTS, next_expert, -1)
    xs = _dispatch(xn_slab, s1, s2, runs + (pad_start, pad_rows), tile, n_blocks * MOE_BLOCK)
    rows = _experts(xs, blk_expert, next_expert, n_used, wts, layer)
    return _combine(yp, ys, rows, (s1, s2, w1, w2), runs, norm_f, final_norm, tile)


def _prep_weights(norm1, w_in, ln_g, ln_b, w_s, b_s, conv_w, conv_b, lru_lambda, w_rg_a, b_rg_a,
                  w_rg_x, b_rg_x, w_out, norm2, w_route_group, b_route_group, w_route_expert,
                  b_route_expert, w_gate, w_up, w_down, n_sample_t):
    depth, d = w_in.shape[0], w_in.shape[1]
    dh = d // H_A
    row = lambda a: a.reshape(depth, 1, -1).astype(F32)
    causal = jnp.tril(jnp.ones((CHUNK, CHUNK), dtype=bool))
    ws = jnp.where(causal, 0.5 * w_s, 0.0)
    bs = jnp.repeat(jnp.swapaxes(0.5 * b_s, 1, 2), dh, axis=2)
    in_scale = jnp.concatenate([jnp.ones((3 * d,), F32), jnp.full((2 * d,), 0.5, F32)])

    def pair_bd(w):
        w = w.reshape(depth, H_B // 2, 2, w.shape[-2], w.shape[-1])
        z = jnp.zeros_like(w[:, :, 0])
        return jnp.concatenate([jnp.concatenate([w[:, :, 0], z], axis=-1),
                                jnp.concatenate([z, w[:, :, 1]], axis=-1)], axis=-2)

    gap = SUBLANES - N_GROUPS
    tail = ROUTE_ROWS - SUBLANES - N_EXPERTS
    w_route = jnp.concatenate(
        [jnp.swapaxes(w_route_group, 1, 2), jnp.zeros((depth, gap, d), F32),
         jnp.swapaxes(w_route_expert, 1, 2), jnp.zeros((depth, tail, d), F32)], axis=1)
    b_route = jnp.concatenate(
        [b_route_group, jnp.zeros((depth, gap), F32), b_route_expert,
         jnp.zeros((depth, tail), F32)], axis=1)[..., None]
    nt = n_sample_t
    return dict(
        norm1=row(norm1), w_in=(w_in * in_scale).astype(BF16), ln_g=row(ln_g), ln_b=row(ln_b),
        ws=ws.astype(BF16), bs=bs.astype(F32), conv_w=conv_w.astype(F32), conv_b=row(conv_b),
        lam=row(lru_lambda),
        wbd=(0.5 * jnp.concatenate([pair_bd(w_rg_a), pair_bd(w_rg_x)], axis=-1)).astype(BF16),
        b_a=row(0.5 * b_rg_a), b_x=row(0.5 * b_rg_x), w_out=(0.5 * w_out).astype(BF16),
        norm2=row(norm2), w_route=w_route.astype(BF16), b_route=b_route.astype(F32),
        w_gate=w_gate, w_up=w_up, w_down=w_down,
        wsc=jnp.repeat(jnp.transpose(ws[:, :, :nt, :nt], (0, 2, 3, 1)), dh, axis=3).astype(F32),
        bsc=bs[:, :nt].astype(F32),
    )


def kernel(x_prompt, x_sample, state_lru_h, state_conv, norm1, w_in, ln_g, ln_b, w_s, b_s, conv_w, conv_b, lru_lambda, w_rg_a, b_rg_a, w_rg_x, b_rg_x, w_out, norm2, w_route_group, b_route_group, w_route_expert, b_route_expert, w_gate, w_up, w_down, norm_f):
    depth = w_in.shape[0]
    nb, seq, d = x_prompt.shape
    ns, nt, _ = x_sample.shape
    assert nt <= CHUNK
    nf = norm_f.reshape(1, d).astype(F32)
    wts = _prep_weights(norm1, w_in, ln_g, ln_b, w_s, b_s, conv_w, conv_b, lru_lambda, w_rg_a,
                        b_rg_a, w_rg_x, b_rg_x, w_out, norm2, w_route_group, b_route_group,
                        w_route_expert, b_route_expert, w_gate, w_up, w_down, nt)
    conv_state = jnp.transpose(state_conv, (0, 2, 1, 3))
    assert (nb * seq) % MOE_TILE == 0
    sample_pad = (-nt * ns) % MOE_TILE
    assert sample_pad % ns == 0
    pad_t = sample_pad // ns
    yp = x_prompt
    ys = jnp.transpose(x_sample, (1, 0, 2))
    hp, cp, vp, hs, cs, vs = [], [], [], [], [], []
    for l in range(depth):
        yp, h, c, v = _mixer_prompt(yp, wts, l)
        hp.append(h)
        cp.append(c)
        vp.append(v)
        ys, h, c, v = _mixer_sample(ys, state_lru_h, conv_state, wts, l, nt, pad_t)
        hs.append(h)
        cs.append(c)
        vs.append(v)
        yp, ys = _moe(yp.reshape(nb * seq, d), ys.reshape((nt + pad_t) * ns, d), wts, l, nf,
                      l == depth - 1, MOE_TILE)
        yp = yp.reshape(nb, seq, d)
        ys = ys.reshape(nt + pad_t, ns, d)
    to_seq_major = lambda a: jnp.transpose(jnp.stack(a), (0, 2, 1, 3))
    return (yp, jnp.transpose(ys[:nt], (1, 0, 2)), jnp.stack(hp), jnp.stack(cp), jnp.stack(vp),
            jnp.stack(hs), to_seq_major(cs), to_seq_major(vs))
```

```python
import functools

import jax
import jax.numpy as jnp
from jax import lax
from jax.experimental import pallas as pl
from jax.experimental.pallas import tpu as pltpu

F32 = jnp.float32
BF16 = jnp.bfloat16

CHUNK = 128
H_A = 4
H_B = 8
CONV_W = 4
LRU_C = 8.0
N_GROUPS = 4
E_PER_GROUP = 8
N_EXPERTS = N_GROUPS * E_PER_GROUP
EPS = 1e-6

LANES = 128
SUBLANES = 8
BF16_ROWS = 16
V7X_VMEM_BYTES = 64 * 1024 * 1024
VMEM_LIMIT = V7X_VMEM_BYTES - 4 * 1024 * 1024
SCAN_UNROLL = 4
SEQS_PER_TRIP = 8
MOE_TILE = 1024
ROUTE_SPAN = 256
MOE_BLOCK = 512
EXPERT_ROW_RING = 6
EXPERT_WEIGHT_RING = 4
assert E_PER_GROUP == SUBLANES and N_GROUPS <= SUBLANES
ROUTE_ROWS = -(-SUBLANES * (N_GROUPS + 1) // BF16_ROWS) * BF16_ROWS


def _dot(a, b):
    return jnp.dot(a, b, preferred_element_type=F32)


def _sigmoid(x):
    return 0.5 * (jnp.tanh(0.5 * x) + 1.0)


def _gelu2(x):
    c = 0.7978845608028654
    return x * (1.0 + jnp.tanh(x * (c + (c * 0.044715) * (x * x))))


def _rms_norm(x, g):
    return x * lax.rsqrt(jnp.mean(x * x, axis=-1, keepdims=True) + EPS) * g


def _layer_norm(x, g, b, eps):
    xc = x - jnp.mean(x, axis=-1, keepdims=True)
    var = jnp.mean(xc * xc, axis=-1, keepdims=True)
    return xc * lax.rsqrt(var + eps) * g + b


def _softplus(x):
    return jnp.maximum(x, 0.0) + jnp.log(1.0 + jnp.exp(-jnp.abs(x)))


def _lru_gates(xc, wbd_ref, b_a, b_x, lam_half):
    d = xc.shape[-1]
    xcb = xc.astype(BF16)
    r_parts, i_parts = [], []
    for q in range(d // 256):
        ri = _dot(xcb[:, q * 256:(q + 1) * 256], wbd_ref[q])
        r_parts.append(ri[:, :256])
        i_parts.append(ri[:, 256:])
    r2 = 1.0 + jnp.tanh(jnp.concatenate(r_parts, axis=-1) + b_a)
    i2 = 1.0 + jnp.tanh(jnp.concatenate(i_parts, axis=-1) + b_x)
    a = jnp.exp(lam_half * r2)
    mult_half = jnp.sqrt(0.25 - 0.25 * (a * a))
    return a, mult_half, i2 * xc


def _shift_rows(x, hist, j):
    sh = pltpu.roll(x, j, axis=0)
    top = jnp.where(lax.broadcasted_iota(jnp.int32, (SUBLANES, 1), 0) < j,
                    pltpu.roll(hist, j, axis=0), sh[:SUBLANES])
    return jnp.concatenate([top, sh[SUBLANES:]], axis=0)


def _mixer_prompt_kernel(x_ref, norm_ref, win_ref, lng_ref, lnb_ref, ws_ref, bs_ref,
                         cw_ref, cb_ref, lam_ref, wbd_ref, ba_ref, bx_ref, wout_ref,
                         y_ref, h_ref, conv_ref, v_ref,
                         hist_s, a_s, b_s, ya_s, gb_s, hc_s):
    i = pl.program_id(0)
    last = pl.num_programs(0) - 1
    nseq, _, d = x_ref.shape
    nlt = d // LANES
    spt = SEQS_PER_TRIP
    pair = spt * CHUNK

    @pl.when(i == 0)
    def _():
        hist_s[...] = jnp.zeros_like(hist_s)
        hc_s[...] = jnp.zeros_like(hc_s)

    lam_half = (-0.5 * LRU_C) * _softplus(-lam_ref[...])
    is_first = i == 0

    def branch_body(p, c):
        row = pl.multiple_of(p * pair, pair)
        x = jnp.concatenate([x_ref[spt * p + k] for k in range(spt)], axis=0)
        xnb = _rms_norm(x, norm_ref[...]).astype(BF16)
        v = _layer_norm(_gelu2(_dot(xnb, win_ref[:, d:2 * d])), lng_ref[...], lnb_ref[...],
                        4.0 * EPS)

        @pl.when(i == last)
        def _():
            for k in range(spt):
                v_ref[spt * p + k] = v[k * CHUNK:(k + 1) * CHUNK]

        vb = v.astype(BF16)
        dh = d // H_A
        z_rows = []
        for k in range(spt):
            zs = [_dot(ws_ref[h], vb[k * CHUNK:(k + 1) * CHUNK, h * dh:(h + 1) * dh])
                  for h in range(H_A)]
            z_rows.append(jnp.concatenate(zs, axis=-1) + bs_ref[...])
        z = jnp.concatenate(z_rows, axis=0)
        ya = _gelu2(_dot(xnb, win_ref[:, 0:d])) * z
        ga2 = 1.0 + jnp.tanh(_dot(xnb, win_ref[:, 3 * d:4 * d]))
        ya_s[pl.ds(row, pair), :] = (ga2 * ya).astype(BF16)
        gb_s[pl.ds(row, pair), :] = (1.0 + jnp.tanh(_dot(xnb, win_ref[:, 4 * d:5 * d]))).astype(BF16)
        xr = _dot(xnb, win_ref[:, 2 * d:3 * d])
        xcs = []
        for k in range(spt):
            s = spt * p + k
            xk = xr[k * CHUNK:(k + 1) * CHUNK]
            hist = hist_s[s]
            xc = cb_ref[...] + xk * cw_ref[CONV_W - 1:CONV_W, :]
            for j in range(1, CONV_W):
                xc = xc + _shift_rows(xk, hist, j) * cw_ref[CONV_W - 1 - j:CONV_W - j, :]
            hist_s[s] = xk[CHUNK - SUBLANES:]
            xcs.append(xc)
        xc = jnp.concatenate(xcs, axis=0)
        a, mult_half, ix2 = _lru_gates(xc, wbd_ref, ba_ref[...], bx_ref[...], lam_half)
        bt = mult_half * ix2
        for k in range(spt):
            s = spt * p + k
            for j in range(nlt):
                a_s[j, pl.ds(s, CHUNK, stride=nseq), :] = a[k * CHUNK:(k + 1) * CHUNK, j * LANES:(j + 1) * LANES]
                b_s[j, pl.ds(s, CHUNK, stride=nseq), :] = bt[k * CHUNK:(k + 1) * CHUNK, j * LANES:(j + 1) * LANES]

        @pl.when(is_first)
        def _():
            for k in range(spt):
                first = 0.5 * ix2[k * CHUNK:k * CHUNK + 1]
                for j in range(nlt):
                    b_s[j, pl.ds(spt * p + k, 1), :] = first[:, j * LANES:(j + 1) * LANES]

        return c

    lax.fori_loop(0, nseq // spt, branch_body, 0)

    def scan_body(tt, hs):
        for u in range(SCAN_UNROLL):
            row = pl.multiple_of((tt * SCAN_UNROLL + u) * nseq, nseq)
            out = []
            for j in range(nlt):
                h = a_s[j, pl.ds(row, nseq), :] * hs[j] + b_s[j, pl.ds(row, nseq), :]
                b_s[j, pl.ds(row, nseq), :] = h
                out.append(h)
            hs = tuple(out)
        return hs

    h0 = tuple(hc_s[:, j * LANES:(j + 1) * LANES] for j in range(nlt))
    hs = lax.fori_loop(0, CHUNK // SCAN_UNROLL, scan_body, h0)
    hfin = jnp.concatenate(hs, axis=-1)
    hc_s[...] = hfin
    h_ref[...] = hfin

    def out_body(p, c):
        row = pl.multiple_of(p * pair, pair)
        hrows = []
        for k in range(spt):
            hrows.append(jnp.concatenate(
                [b_s[j, pl.ds(spt * p + k, CHUNK, stride=nseq), :] for j in range(nlt)], axis=-1))
        h = jnp.concatenate(hrows, axis=0)
        merged = (ya_s[pl.ds(row, pair), :].astype(F32)
                  + gb_s[pl.ds(row, pair), :].astype(F32) * h).astype(BF16)
        o = _dot(merged, wout_ref[...])
        for k in range(spt):
            y_ref[spt * p + k] = x_ref[spt * p + k] + o[k * CHUNK:(k + 1) * CHUNK]
        return c

    lax.fori_loop(0, nseq // spt, out_body, 0)

    @pl.when(i == last)
    def _():
        conv_ref[...] = hist_s[:, SUBLANES - (CONV_W - 1):, :]


def _const_spec(shape):
    nd = len(shape)
    return pl.BlockSpec(shape, lambda i, *_, _n=nd: (0,) * _n, pipeline_mode=pl.Buffered(1))


def _layer_spec(stacked, layer):
    shape = stacked.shape[1:]
    nd = len(shape)
    return pl.BlockSpec((None,) + tuple(shape), lambda i, *_, _n=nd: (layer,) + (0,) * _n,
                        pipeline_mode=pl.Buffered(1))


MIXER_WEIGHTS = ('norm1', 'w_in', 'ln_g', 'ln_b', 'ws', 'bs', 'conv_w', 'conv_b', 'lam', 'wbd',
                 'b_a', 'b_x', 'w_out')


def _mixer_prompt(x, wts, layer):
    nseq, seq, d = x.shape
    assert seq % CHUNK == 0 and nseq == SUBLANES and d % 256 == 0
    n_chunks = seq // CHUNK
    nlt = d // LANES
    weights = tuple(wts[k] for k in MIXER_WEIGHTS)
    x_spec = pl.BlockSpec((nseq, CHUNK, d), lambda i: (0, i, 0))
    out_shape = (
        jax.ShapeDtypeStruct((nseq, seq, d), F32),
        jax.ShapeDtypeStruct((nseq, d), F32),
        jax.ShapeDtypeStruct((nseq, CONV_W - 1, d), F32),
        jax.ShapeDtypeStruct((nseq, CHUNK, d), F32),
    )
    out_specs = (
        x_spec,
        _const_spec((nseq, d)),
        _const_spec((nseq, CONV_W - 1, d)),
        _const_spec((nseq, CHUNK, d)),
    )
    scratch = [
        pltpu.VMEM((nseq, SUBLANES, d), F32),
        pltpu.VMEM((nlt, CHUNK * nseq, LANES), F32),
        pltpu.VMEM((nlt, CHUNK * nseq, LANES), F32),
        pltpu.VMEM((nseq * CHUNK, d), BF16),
        pltpu.VMEM((nseq * CHUNK, d), BF16),
        pltpu.VMEM((nseq, d), F32),
    ]
    return pl.pallas_call(
        _mixer_prompt_kernel,
        grid=(n_chunks,),
        in_specs=[x_spec] + [_layer_spec(w, layer) for w in weights],
        out_specs=out_specs,
        out_shape=out_shape,
        scratch_shapes=scratch,
        compiler_params=pltpu.CompilerParams(
            dimension_semantics=("arbitrary",), vmem_limit_bytes=VMEM_LIMIT),
        name="mixer_prompt",
    )(x, *weights)


def _mixer_sample_kernel(x_ref, h0_ref, cst_ref, norm_ref, win_ref, lng_ref, lnb_ref, wsc_ref,
                         bsc_ref, cw_ref, cb_ref, lam_ref, wbd_ref, ba_ref, bx_ref, wout_ref,
                         y_ref, h_ref, conv_ref, v_ref):
    nt, ns, d = x_ref.shape
    at = lambda a, t: a[t * ns:(t + 1) * ns]
    lam_half = (-0.5 * LRU_C) * _softplus(-lam_ref[...])
    x = x_ref[...].reshape(nt * ns, d)
    xnb = _rms_norm(x, norm_ref[...]).astype(BF16)
    v = _layer_norm(_gelu2(_dot(xnb, win_ref[:, d:2 * d])), lng_ref[...], lnb_ref[...], 4.0 * EPS)
    v_ref[...] = v.reshape(nt, ns, d)
    zs = []
    for t in range(nt):
        z = bsc_ref[t:t + 1, :]
        for s in range(t + 1):
            z = z + wsc_ref[t, s:s + 1, :] * at(v, s)
        zs.append(z)
    ya = _gelu2(_dot(xnb, win_ref[:, 0:d])) * jnp.concatenate(zs, axis=0)
    ga2 = 1.0 + jnp.tanh(_dot(xnb, win_ref[:, 3 * d:4 * d]))
    gb2 = 1.0 + jnp.tanh(_dot(xnb, win_ref[:, 4 * d:5 * d]))
    xr = _dot(xnb, win_ref[:, 2 * d:3 * d])
    xp = [cst_ref[k] for k in range(CONV_W - 1)] + [at(xr, t) for t in range(nt)]
    xcs = []
    for t in range(nt):
        xc = cb_ref[...]
        for k in range(CONV_W):
            xc = xc + xp[t + k] * cw_ref[k:k + 1, :]
        xcs.append(xc)
    a, mult_half, ix2 = _lru_gates(jnp.concatenate(xcs, axis=0), wbd_ref, ba_ref[...],
                                   bx_ref[...], lam_half)
    bt = mult_half * ix2
    h = h0_ref[...]
    hs = []
    for t in range(nt):
        h = at(a, t) * h + at(bt, t)
        hs.append(h)
    merged = (ga2 * ya + gb2 * jnp.concatenate(hs, axis=0)).astype(BF16)
    y_ref[0:nt] = (x + _dot(merged, wout_ref[...])).reshape(nt, ns, d)
    if y_ref.shape[0] > nt:
        y_ref[nt:] = jnp.zeros((y_ref.shape[0] - nt, ns, d), F32)
    h_ref[...] = h
    for k in range(CONV_W - 1):
        conv_ref[k] = xp[nt + k]


def _mixer_sample(x, h0, conv_state, wts, layer, nt, pad_t):
    _, ns, d = x.shape
    names = MIXER_WEIGHTS[:4] + ('wsc', 'bsc') + MIXER_WEIGHTS[6:]
    stacked = (h0, conv_state) + tuple(wts[k] for k in names)
    out_shape = (
        jax.ShapeDtypeStruct((nt + pad_t, ns, d), F32),
        jax.ShapeDtypeStruct((ns, d), F32),
        jax.ShapeDtypeStruct((CONV_W - 1, ns, d), F32),
        jax.ShapeDtypeStruct((nt, ns, d), F32),
    )
    return pl.pallas_call(
        _mixer_sample_kernel,
        grid=(1,),
        in_specs=[_const_spec((nt, ns, d))] + [_layer_spec(w, layer) for w in stacked],
        out_specs=tuple(_const_spec(s.shape) for s in out_shape),
        out_shape=out_shape,
        compiler_params=pltpu.CompilerParams(
            dimension_semantics=("arbitrary",), vmem_limit_bytes=VMEM_LIMIT),
        name="mixer_sample",
    )(x, *stacked)


def _to_slab(ref, val):
    t, d = val.shape
    nlt = d // LANES
    for j in range(nlt):
        ref[pl.ds(j, t, stride=nlt), :] = val[:, j * LANES:(j + 1) * LANES]


def _from_slab(ref, t, nlt):
    return jnp.concatenate([ref[pl.ds(j, t, stride=nlt), :] for j in range(nlt)], axis=-1)


def _tokens_from_slab(slab_ref, nlt):
    return slab_ref[...].reshape(slab_ref.shape[0] // nlt, nlt, LANES).astype(BF16)


def _tokens_to_slab(slab_ref, tokens):
    slab_ref[...] = tokens.astype(F32).reshape(slab_ref.shape)


def _router_kernel(yp_ref, ys_ref, norm_ref, wr_ref, br_ref, tri_ref, low_ref, xn_ref, s1_ref,
                   s2_ref, w1_ref, w2_ref, cnt_ref, slab_s, *, n_prompt_tiles):
    y = jnp.where(pl.program_id(0) < n_prompt_tiles, yp_ref[...], ys_ref[...])
    xn = _rms_norm(y, norm_ref[...])
    _to_slab(slab_s, xn)
    xn_ref[...] = _tokens_from_slab(slab_s, xn_ref.shape[1])
    lt = lax.dot_general(wr_ref[...], xn.astype(BF16), (((1,), (1,)), ((), ())),
                         preferred_element_type=F32) + br_ref[...]
    tile = lt.shape[1]
    row = lax.broadcasted_iota(jnp.int32, (SUBLANES, tile), 0).astype(F32)
    neg = jnp.float32(-jnp.inf)
    big = jnp.float32(SUBLANES)

    def argmax_first(vals):
        m = jnp.max(vals, axis=0, keepdims=True)
        return m, jnp.min(jnp.where(vals == m, row, big), axis=0, keepdims=True)

    def group_block(x, gi):
        out = x[SUBLANES:2 * SUBLANES]
        for g in range(1, N_GROUPS):
            out = jnp.where(gi == g, x[(g + 1) * SUBLANES:(g + 2) * SUBLANES], out)
        return out

    gl = jnp.where(row < N_GROUPS, lt[0:SUBLANES], neg)
    gm, gi = argmax_first(gl)
    pg_top = 1.0 / jnp.sum(jnp.exp(gl - gm), axis=0, keepdims=True)
    el = group_block(lt, gi)
    m1, i1 = argmax_first(el)
    m2, i2 = argmax_first(jnp.where(row == i1, neg, el))
    e2 = jnp.exp(m2 - m1)
    w1 = pg_top / (1.0 + e2)
    w2 = pg_top * e2 / (1.0 + e2)
    hot1 = row == i1
    hot2 = row == i2
    picked = jnp.where(jnp.logical_or(hot1, hot2), 1.0, 0.0)
    zero = jnp.zeros_like(picked)
    onehot = jnp.concatenate(
        [zero] + [jnp.where(gi == g, picked, 0.0) for g in range(N_GROUPS)] + [zero],
        axis=0).astype(BF16)
    span = tri_ref.shape[0]
    parts = []
    cnt = jnp.zeros((ROUTE_ROWS, 1), F32)
    for k in range(tile // span):
        blk = onehot[:, k * span:(k + 1) * span]
        parts.append(_dot(blk, tri_ref[...]) + cnt)
        cnt = cnt + jnp.sum(blk.astype(F32), axis=1, keepdims=True)
    before = jnp.concatenate(parts, axis=1)
    lower = jnp.sum(_dot(low_ref[...], onehot), axis=1, keepdims=True)
    slot = group_block(before + lower, gi)
    p1 = jnp.sum(jnp.where(hot1, slot, 0.0), axis=0, keepdims=True)
    p2 = jnp.sum(jnp.where(hot2, slot, 0.0), axis=0, keepdims=True)
    s1_ref[0] = p1.astype(jnp.int32)
    s2_ref[0] = p2.astype(jnp.int32)
    w1_ref[0] = w1
    w2_ref[0] = w2
    cnt_ref[0] = jnp.broadcast_to(cnt, cnt_ref.shape[1:])


def _token_specs(tile, d, n_prompt_tiles):
    last = n_prompt_tiles - 1
    return (pl.BlockSpec((tile, d), lambda i, *_: (jnp.minimum(i, last), 0)),
            pl.BlockSpec((tile, d), lambda i, *_: (jnp.maximum(i - n_prompt_tiles, 0), 0)))


def _router(yp, ys, wts, layer, tile):
    d = yp.shape[1]
    nlt = d // LANES
    n_prompt_tiles = yp.shape[0] // tile
    n_tiles = n_prompt_tiles + ys.shape[0] // tile
    span = min(tile, ROUTE_SPAN)
    assert tile % span == 0
    tri = jnp.triu(jnp.ones((span, span), BF16), 1)
    low = jnp.tril(jnp.ones((ROUTE_ROWS, ROUTE_ROWS), BF16), -1)
    per_token = pl.BlockSpec((1, 1, tile), lambda i: (i, 0, 0))
    return pl.pallas_call(
        functools.partial(_router_kernel, n_prompt_tiles=n_prompt_tiles),
        grid=(n_tiles,),
        in_specs=list(_token_specs(tile, d, n_prompt_tiles)) + [
            _layer_spec(wts['norm2'], layer), _layer_spec(wts['w_route'], layer),
            _layer_spec(wts['b_route'], layer), _const_spec((span, span)),
            _const_spec((ROUTE_ROWS, ROUTE_ROWS))],
        out_specs=(pl.BlockSpec((tile, nlt, LANES), lambda i: (i, 0, 0)),
                   per_token, per_token, per_token, per_token,
                   pl.BlockSpec((1, ROUTE_ROWS, LANES), lambda i: (i, 0, 0))),
        out_shape=(jax.ShapeDtypeStruct((n_tiles * tile, nlt, LANES), BF16),
                   jax.ShapeDtypeStruct((n_tiles, 1, tile), jnp.int32),
                   jax.ShapeDtypeStruct((n_tiles, 1, tile), jnp.int32),
                   jax.ShapeDtypeStruct((n_tiles, 1, tile), F32),
                   jax.ShapeDtypeStruct((n_tiles, 1, tile), F32),
                   jax.ShapeDtypeStruct((n_tiles, ROUTE_ROWS, LANES), F32)),
        scratch_shapes=[pltpu.VMEM((tile * nlt, LANES), F32)],
        compiler_params=pltpu.CompilerParams(
            dimension_semantics=("arbitrary",), vmem_limit_bytes=VMEM_LIMIT),
        name="moe_router",
    )(yp, ys, wts['norm2'], wts['w_route'], wts['b_route'], tri, low)


def _run_copy(tile_idx, e, lo_ref, cnt_ref, dst_ref, local, remote, sem, to_remote):
    k = tile_idx * N_EXPERTS + e
    loc = local.at[pl.ds(lo_ref[k], cnt_ref[k])]
    rem = remote.at[pl.ds(dst_ref[k], cnt_ref[k])]
    return pltpu.make_async_copy(loc, rem, sem) if to_remote else pltpu.make_async_copy(rem, loc, sem)


def _start_runs(tile_idx, lo_ref, cnt_ref, dst_ref, local, remote, sem, to_remote):
    for e in range(N_EXPERTS):
        @pl.when(cnt_ref[tile_idx * N_EXPERTS + e] > 0)
        def _():
            _run_copy(tile_idx, e, lo_ref, cnt_ref, dst_ref, local, remote, sem, to_remote).start()


def _wait_runs(local, sem):
    pltpu.make_async_copy(local, local, sem).wait()


def _dispatch_kernel(lo_ref, cnt_ref, dst_ref, padst_ref, padn_ref,
                     s1_ref, s2_ref, x_ref, xs_ref, loc, zeros, sem, *, unroll):
    i = pl.program_id(0)
    last = pl.num_programs(0) - 1
    tile = x_ref.shape[0]
    slot = i % 2
    buf = loc.at[slot]

    @pl.when(i >= 2)
    def _():
        _wait_runs(buf, sem.at[slot])

    def body(tt, c):
        for u in range(unroll):
            t = tt * unroll + u
            row = x_ref[t]
            buf[s1_ref[t]] = row
            buf[s2_ref[t]] = row
        return c

    lax.fori_loop(0, tile // unroll, body, 0)
    _start_runs(i, lo_ref, cnt_ref, dst_ref, buf, xs_ref, sem.at[slot], True)

    @pl.when(i == last)
    def _():
        zeros[...] = jnp.zeros_like(zeros)

        def pad_copy(e):
            n = padn_ref[e]
            dst = xs_ref.at[pl.ds(padst_ref[e], n)]
            return pltpu.make_async_copy(zeros.at[pl.ds(0, n)], dst, sem.at[2])

        def each_pad(fn):
            def b(e, c):
                @pl.when(padn_ref[e] > 0)
                def _():
                    fn(e)
                return c
            lax.fori_loop(0, padn_ref.shape[0], b, 0)

        each_pad(lambda e: pad_copy(e).start())
        _wait_runs(buf, sem.at[slot])

        @pl.when(i >= 1)
        def _():
            _wait_runs(loc.at[1 - slot], sem.at[1 - slot])

        each_pad(lambda e: pad_copy(e).wait())


def _smem_tile(tile):
    return pl.BlockSpec((None, None, tile), lambda i, *_: (i, 0, 0), memory_space=pltpu.SMEM)


def _dispatch(xn_slab, s1, s2, tables, tile, n_rows):
    token = xn_slab.shape[1:]
    return pl.pallas_call(
        functools.partial(_dispatch_kernel, unroll=8),
        grid_spec=pltpu.PrefetchScalarGridSpec(
            num_scalar_prefetch=5,
            grid=(xn_slab.shape[0] // tile,),
            in_specs=[_smem_tile(tile), _smem_tile(tile),
                      pl.BlockSpec((tile,) + token, lambda i, *_: (i, 0, 0))],
            out_specs=pl.BlockSpec(memory_space=pl.ANY),
            scratch_shapes=[pltpu.VMEM((2, 2 * tile) + token, xn_slab.dtype),
                            pltpu.VMEM((MOE_BLOCK,) + token, xn_slab.dtype),
                            pltpu.SemaphoreType.DMA((3,))]),
        out_shape=jax.ShapeDtypeStruct((n_rows,) + token, xn_slab.dtype),
        compiler_params=pltpu.CompilerParams(
            dimension_semantics=("arbitrary",), vmem_limit_bytes=VMEM_LIMIT),
        name="moe_dispatch",
    )(*tables, s1, s2, xn_slab)


def _experts_kernel(blk_ref, next_ref, nused_ref, xs_ref, wg_ref, wu_ref, wd_ref, ys_ref,
                    xin, yout, wg_f, wu_f, wd_f, wg_s, wu_s, wd_s, slab_in,
                    sem_in, sem_out, sem_w, *, layer):
    n_blocks = ys_ref.shape[0] // MOE_BLOCK
    nlt = xs_ref.shape[1]
    n_used = nused_ref[0]
    block = lambda ref, b: ref.at[pl.ds(pl.multiple_of(b * MOE_BLOCK, MOE_BLOCK), MOE_BLOCK)]
    ring = xin.shape[0]
    rows_in = lambda b: pltpu.make_async_copy(block(xs_ref, b), xin.at[b % ring],
                                              sem_in.at[b % ring])
    rows_out = lambda b, slot: pltpu.make_async_copy(yout.at[slot], block(ys_ref, b),
                                                     sem_out.at[slot])

    def weights(e, slot):
        pairs = ((wg_ref, wg_f), (wu_ref, wu_f), (wd_ref, wd_f))
        return [pltpu.make_async_copy(src.at[layer, e], dst.at[slot], sem_w.at[slot])
                for src, dst in pairs]

    for k in range(ring - 1):
        @pl.when(k < n_used)
        def _():
            rows_in(k).start()

    wring = wg_f.shape[0]

    def later_expert(e, hops):
        for _ in range(hops):
            e = jnp.where(e >= 0, next_ref[jnp.maximum(e, 0)], -1)
        return e

    @pl.when(n_used > 0)
    def _():
        for k in range(wring - 1):
            e_k = later_expert(blk_ref[0], k)

            @pl.when(e_k >= 0)
            def _():
                for c in weights(e_k, k):
                    c.start()

    def step(b, wcount):
        slot = b % 2
        e = blk_ref[b]
        rows_in(b).wait()

        @pl.when(b + ring - 1 < n_used)
        def _():
            rows_in(b + ring - 1).start()

        new_expert = jnp.logical_or(b == 0, e != blk_ref[jnp.maximum(b - 1, 0)])
        wcount = jnp.where(new_expert, wcount + 1, wcount)

        @pl.when(new_expert)
        def _():
            wslot = wcount % wring
            for c in weights(e, wslot):
                c.wait()
            wg_s[...] = wg_f[wslot].astype(BF16)
            wu_s[...] = wu_f[wslot].astype(BF16)
            wd_s[...] = wd_f[wslot].astype(BF16)
            e_far = later_expert(e, wring - 1)

            @pl.when(e_far >= 0)
            def _():
                for c in weights(e_far, (wcount + wring - 1) % wring):
                    c.start()

        _tokens_to_slab(slab_in, xin[b % ring])
        x = _from_slab(slab_in, MOE_BLOCK, nlt).astype(BF16)
        hg = _dot(x, wg_s[...])
        hu = _dot(x, wu_s[...])
        h = (hg * _sigmoid(hg) * hu).astype(BF16)
        o = _dot(h, wd_s[...]).reshape(MOE_BLOCK, nlt, LANES).astype(BF16)

        @pl.when(b >= 2)
        def _():
            rows_out(b - 2, slot).wait()

        yout[slot] = o
        rows_out(b, slot).start()
        return wcount

    lax.fori_loop(0, n_used, step, jnp.int32(-1))

    @pl.when(n_used >= 2)
    def _():
        rows_out(n_used - 2, n_used % 2).wait()

    @pl.when(n_used >= 1)
    def _():
        rows_out(n_used - 1, (n_used - 1) % 2).wait()

    yout[0] = jnp.zeros(yout.shape[1:], yout.dtype)

    def fill(b, c):
        rows_out(b, 0).start()
        rows_out(b, 0).wait()
        return c

    lax.fori_loop(n_used, n_blocks, fill, 0)


def _experts(xs, blk_expert, next_expert, n_used, wts, layer):
    n_rows, nlt, _ = xs.shape
    d = nlt * LANES
    f = wts['w_gate'].shape[-1]
    token = (nlt, LANES)
    any_spec = pl.BlockSpec(memory_space=pl.ANY)
    return pl.pallas_call(
        functools.partial(_experts_kernel, layer=layer),
        grid_spec=pltpu.PrefetchScalarGridSpec(
            num_scalar_prefetch=3,
            grid=(1,),
            in_specs=[any_spec] * 4,
            out_specs=any_spec,
            scratch_shapes=[pltpu.VMEM((EXPERT_ROW_RING, MOE_BLOCK) + token, xs.dtype),
                            pltpu.VMEM((2, MOE_BLOCK) + token, xs.dtype),
                            pltpu.VMEM((EXPERT_WEIGHT_RING, d, f), F32),
                            pltpu.VMEM((EXPERT_WEIGHT_RING, d, f), F32),
                            pltpu.VMEM((EXPERT_WEIGHT_RING, f, d), F32),
                            pltpu.VMEM((d, f), BF16), pltpu.VMEM((d, f), BF16),
                            pltpu.VMEM((f, d), BF16),
                            pltpu.VMEM((MOE_BLOCK * nlt, LANES), F32),
                            pltpu.SemaphoreType.DMA((EXPERT_ROW_RING,)),
                            pltpu.SemaphoreType.DMA((2,)),
                            pltpu.SemaphoreType.DMA((EXPERT_WEIGHT_RING,))]),
        out_shape=jax.ShapeDtypeStruct(xs.shape, xs.dtype),
        compiler_params=pltpu.CompilerParams(
            dimension_semantics=("arbitrary",), vmem_limit_bytes=VMEM_LIMIT),
        name="moe_experts",
    )(blk_expert, next_expert, n_used, xs, wts['w_gate'], wts['w_up'], wts['w_down'])


def _combine_kernel(lo_ref, cnt_ref, dst_ref, s1_ref, s2_ref, w1_ref, w2_ref, yp_ref, ys_ref,
                    nf_ref, rows_ref, op_ref, os_ref, loc, acc, sem,
                    *, unroll, final_norm, n_prompt_tiles):
    i = pl.program_id(0)
    tile = yp_ref.shape[0]
    nlt = loc.shape[2]
    slot = i % 2
    buf = loc.at[slot]
    fetch = lambda t, s: _start_runs(t, lo_ref, cnt_ref, dst_ref, loc.at[s], rows_ref, sem.at[s],
                                     False)

    @pl.when(i == 0)
    def _():
        fetch(0, 0)

    @pl.when(i + 1 < pl.num_programs(0))
    def _():
        fetch(i + 1, 1 - slot)

    _wait_runs(buf, sem.at[slot])

    def body(tt, c):
        for u in range(unroll):
            t = tt * unroll + u
            r1 = buf[s1_ref[t]].astype(F32)
            r2 = buf[s2_ref[t]].astype(F32)
            acc[pl.ds(pl.multiple_of(t * nlt, nlt), nlt), :] = w1_ref[t] * r1 + w2_ref[t] * r2
        return c

    lax.fori_loop(0, tile // unroll, body, 0)
    def finish(y_ref, o_ref):
        out = y_ref[...] + acc[...].reshape(tile, nlt, LANES).reshape(tile, nlt * LANES)
        if final_norm:
            out = _rms_norm(out, nf_ref[...])
        o_ref[...] = out

    is_prompt = i < n_prompt_tiles
    pl.when(is_prompt)(lambda: finish(yp_ref, op_ref))
    pl.when(jnp.logical_not(is_prompt))(lambda: finish(ys_ref, os_ref))


def _combine(yp, ys, rows, meta, tables, norm_f, final_norm, tile):
    d = yp.shape[1]
    token = rows.shape[1:]
    n_prompt_tiles = yp.shape[0] // tile
    n_tiles = n_prompt_tiles + ys.shape[0] // tile
    tok_p, tok_s = _token_specs(tile, d, n_prompt_tiles)
    return pl.pallas_call(
        functools.partial(_combine_kernel, unroll=8, final_norm=final_norm,
                          n_prompt_tiles=n_prompt_tiles),
        grid_spec=pltpu.PrefetchScalarGridSpec(
            num_scalar_prefetch=3,
            grid=(n_tiles,),
            in_specs=[_smem_tile(tile)] * 4 + [tok_p, tok_s, _const_spec((1, d)),
                                               pl.BlockSpec(memory_space=pl.ANY)],
            out_specs=(tok_p, tok_s),
            scratch_shapes=[pltpu.VMEM((2, 2 * tile) + token, rows.dtype),
                            pltpu.VMEM((tile * token[0], LANES), F32),
                            pltpu.SemaphoreType.DMA((2,))]),
        out_shape=(jax.ShapeDtypeStruct(yp.shape, F32), jax.ShapeDtypeStruct(ys.shape, F32)),
        compiler_params=pltpu.CompilerParams(
            dimension_semantics=("arbitrary",), vmem_limit_bytes=VMEM_LIMIT),
        name="moe_combine",
    )(*tables, *meta, yp, ys, norm_f, rows)


def _moe(yp, ys, wts, layer, norm_f, final_norm, tile):
    d = yp.shape[1]
    n = yp.shape[0] + ys.shape[0]
    assert yp.shape[0] % tile == 0 and ys.shape[0] % tile == 0 and d % LANES == 0
    xn_slab, s1, s2, w1, w2, cnt = _router(yp, ys, wts, layer, tile)
    cnt = cnt[:, SUBLANES:SUBLANES + N_EXPERTS, 0].astype(jnp.int32)
    lo = jnp.cumsum(cnt, axis=1) - cnt
    seg = jnp.sum(cnt, axis=0)
    seg_pad = (seg + MOE_BLOCK - 1) // MOE_BLOCK * MOE_BLOCK
    seg_end = jnp.cumsum(seg_pad)
    seg_start = seg_end - seg_pad
    dst = seg_start[None, :] + jnp.cumsum(cnt, axis=0) - cnt
    n_blocks = (2 * n + N_EXPERTS * (MOE_BLOCK - 1)) // MOE_BLOCK
    blk_ids = jnp.arange(n_blocks, dtype=jnp.int32)
    blk_expert = jnp.minimum(
        jnp.sum((seg_end[None, :] <= blk_ids[:, None] * MOE_BLOCK).astype(jnp.int32), axis=1),
        N_EXPERTS - 1)
    n_used = seg_end[-1:] // MOE_BLOCK
    pad_start = jnp.concatenate([seg_start + seg, blk_ids * MOE_BLOCK])
    pad_rows = jnp.concatenate([seg_pad - seg, jnp.where(blk_ids >= n_used[0], MOE_BLOCK, 0)])
    flat = lambda a: a.reshape(-1)
    runs = (flat(lo), flat(cnt), flat(dst))
    ids = jnp.arange(N_EXPERTS, dtype=jnp.int32)
    later = jnp.where((seg_pad[None, :] > 0) & (ids[None, :] > ids[:, None]), ids[None, :],
                      N_EXPERTS)
    next_expert = jnp.min(later, axis=1)
    next_expert = jnp.where(next_expert < N_EXPERTS, next_expert, -1)
    xs = _dispatch(xn_slab, s1, s2, runs + (pad_start, pad_rows), tile, n_blocks * MOE_BLOCK)
    rows = _experts(xs, blk_expert, next_expert, n_used, wts, layer)
    return _combine(yp, ys, rows, (s1, s2, w1, w2), runs, norm_f, final_norm, tile)


def _prep_weights(norm1, w_in, ln_g, ln_b, w_s, b_s, conv_w, conv_b, lru_lambda, w_rg_a, b_rg_a,
                  w_rg_x, b_rg_x, w_out, norm2, w_route_group, b_route_group, w_route_expert,
                  b_route_expert, w_gate, w_up, w_down, n_sample_t):
    depth, d = w_in.shape[0], w_in.shape[1]
    dh = d // H_A
    row = lambda a: a.reshape(depth, 1, -1).astype(F32)
    causal = jnp.tril(jnp.ones((CHUNK, CHUNK), dtype=bool))
    ws = jnp.where(causal, 0.5 * w_s, 0.0)
    bs = jnp.repeat(jnp.swapaxes(0.5 * b_s, 1, 2), dh, axis=2)
    in_scale = jnp.concatenate([jnp.ones((3 * d,), F32), jnp.full((2 * d,), 0.5, F32)])

    def pair_bd(w):
        w = w.reshape(depth, H_B // 2, 2, w.shape[-2], w.shape[-1])
        z = jnp.zeros_like(w[:, :, 0])
        return jnp.concatenate([jnp.concatenate([w[:, :, 0], z], axis=-1),
                                jnp.concatenate([z, w[:, :, 1]], axis=-1)], axis=-2)

    gap = SUBLANES - N_GROUPS
    tail = ROUTE_ROWS - SUBLANES - N_EXPERTS
    w_route = jnp.concatenate(
        [jnp.swapaxes(w_route_group, 1, 2), jnp.zeros((depth, gap, d), F32),
         jnp.swapaxes(w_route_expert, 1, 2), jnp.zeros((depth, tail, d), F32)], axis=1)
    b_route = jnp.concatenate(
        [b_route_group, jnp.zeros((depth, gap), F32), b_route_expert,
         jnp.zeros((depth, tail), F32)], axis=1)[..., None]
    nt = n_sample_t
    return dict(
        norm1=row(norm1), w_in=(w_in * in_scale).astype(BF16), ln_g=row(ln_g), ln_b=row(ln_b),
        ws=ws.astype(BF16), bs=bs.astype(F32), conv_w=conv_w.astype(F32), conv_b=row(conv_b),
        lam=row(lru_lambda),
        wbd=(0.5 * jnp.concatenate([pair_bd(w_rg_a), pair_bd(w_rg_x)], axis=-1)).astype(BF16),
        b_a=row(0.5 * b_rg_a), b_x=row(0.5 * b_rg_x), w_out=(0.5 * w_out).astype(BF16),
        norm2=row(norm2), w_route=w_route.astype(BF16), b_route=b_route.astype(F32),
        w_gate=w_gate, w_up=w_up, w_down=w_down,
        wsc=jnp.repeat(jnp.transpose(ws[:, :, :nt, :nt], (0, 2, 3, 1)), dh, axis=3).astype(F32),
        bsc=bs[:, :nt].astype(F32),
    )


def kernel(x_prompt, x_sample, state_lru_h, state_conv, norm1, w_in, ln_g, ln_b, w_s, b_s, conv_w, conv_b, lru_lambda, w_rg_a, b_rg_a, w_rg_x, b_rg_x, w_out, norm2, w_route_group, b_route_group, w_route_expert, b_route_expert, w_gate, w_up, w_down, norm_f):
    depth = w_in.shape[0]
    nb, seq, d = x_prompt.shape
    ns, nt, _ = x_sample.shape
    assert nt <= CHUNK
    nf = norm_f.reshape(1, d).astype(F32)
    wts = _prep_weights(norm1, w_in, ln_g, ln_b, w_s, b_s, conv_w, conv_b, lru_lambda, w_rg_a,
                        b_rg_a, w_rg_x, b_rg_x, w_out, norm2, w_route_group, b_route_group,
                        w_route_expert, b_route_expert, w_gate, w_up, w_down, nt)
    conv_state = jnp.transpose(state_conv, (0, 2, 1, 3))
    assert (nb * seq) % MOE_TILE == 0
    sample_pad = (-nt * ns) % MOE_TILE
    assert sample_pad % ns == 0
    pad_t = sample_pad // ns
    yp = x_prompt
    ys = jnp.transpose(x_sample, (1, 0, 2))
    hp, cp, vp, hs, cs, vs = [], [], [], [], [], []
    for l in range(depth):
        yp, h, c, v = _mixer_prompt(yp, wts, l)
        hp.append(h)
        cp.append(c)
        vp.append(v)
        ys, h, c, v = _mixer_sample(ys, state_lru_h, conv_state, wts, l, nt, pad_t)
        hs.append(h)
        cs.append(c)
        vs.append(v)
        yp, ys = _moe(yp.reshape(nb * seq, d), ys.reshape((nt + pad_t) * ns, d), wts, l, nf,
                      l == depth - 1, MOE_TILE)
        yp = yp.reshape(nb, seq, d)
        ys = ys.reshape(nt + pad_t, ns, d)
    to_seq_major = lambda a: jnp.transpose(jnp.stack(a), (0, 2, 1, 3))
    return (yp, jnp.transpose(ys[:nt], (1, 0, 2)), jnp.stack(hp), jnp.stack(cp), jnp.stack(vp),
            jnp.stack(hs), to_seq_major(cs), to_seq_major(vs))
```

```python
import functools

import jax
import jax.numpy as jnp
from jax import lax
from jax.experimental import pallas as pl
from jax.experimental.pallas import tpu as pltpu

F32 = jnp.float32
BF16 = jnp.bfloat16

CHUNK = 128
H_A = 4
H_B = 8
CONV_W = 4
LRU_C = 8.0
N_GROUPS = 4
E_PER_GROUP = 8
N_EXPERTS = N_GROUPS * E_PER_GROUP
EPS = 1e-6

LANES = 128
SUBLANES = 8
BF16_ROWS = 16
V7X_VMEM_BYTES = 64 * 1024 * 1024
VMEM_LIMIT = V7X_VMEM_BYTES - 4 * 1024 * 1024
SCAN_UNROLL = 4
SEQS_PER_TRIP = 8
MOE_TILE = 1024
ROUTE_SPAN = 256
MOE_BLOCK = 512
EXPERT_ROW_RING = 4
EXPERT_WEIGHT_RING = 3
assert E_PER_GROUP == SUBLANES and N_GROUPS <= SUBLANES
ROUTE_ROWS = -(-SUBLANES * (N_GROUPS + 1) // BF16_ROWS) * BF16_ROWS


def _dot(a, b):
    return jnp.dot(a, b, preferred_element_type=F32)


def _sigmoid(x):
    return 0.5 * (jnp.tanh(0.5 * x) + 1.0)


def _gelu2(x):
    c = 0.7978845608028654
    return x * (1.0 + jnp.tanh(x * (c + (c * 0.044715) * (x * x))))


def _rms_norm(x, g):
    return x * lax.rsqrt(jnp.mean(x * x, axis=-1, keepdims=True) + EPS) * g


def _layer_norm(x, g, b, eps):
    xc = x - jnp.mean(x, axis=-1, keepdims=True)
    var = jnp.mean(xc * xc, axis=-1, keepdims=True)
    return xc * lax.rsqrt(var + eps) * g + b


def _softplus(x):
    return jnp.maximum(x, 0.0) + jnp.log(1.0 + jnp.exp(-jnp.abs(x)))


def _lru_gates(xc, wbd_ref, b_a, b_x, lam_half):
    d = xc.shape[-1]
    xcb = xc.astype(BF16)
    r_parts, i_parts = [], []
    for q in range(d // 256):
        ri = _dot(xcb[:, q * 256:(q + 1) * 256], wbd_ref[q])
        r_parts.append(ri[:, :256])
        i_parts.append(ri[:, 256:])
    r2 = 1.0 + jnp.tanh(jnp.concatenate(r_parts, axis=-1) + b_a)
    i2 = 1.0 + jnp.tanh(jnp.concatenate(i_parts, axis=-1) + b_x)
    a = jnp.exp(lam_half * r2)
    mult_half = jnp.sqrt(0.25 - 0.25 * (a * a))
    return a, mult_half, i2 * xc


def _shift_rows(x, hist, j):
    sh = pltpu.roll(x, j, axis=0)
    top = jnp.where(lax.broadcasted_iota(jnp.int32, (SUBLANES, 1), 0) < j,
                    pltpu.roll(hist, j, axis=0), sh[:SUBLANES])
    return jnp.concatenate([top, sh[SUBLANES:]], axis=0)


def _mixer_prompt_kernel(x_ref, norm_ref, win_ref, lng_ref, lnb_ref, ws_ref, bs_ref,
                         cw_ref, cb_ref, lam_ref, wbd_ref, ba_ref, bx_ref, wout_ref,
                         y_ref, h_ref, conv_ref, v_ref,
                         hist_s, a_s, b_s, ya_s, gb_s, hc_s):
    i = pl.program_id(0)
    last = pl.num_programs(0) - 1
    nseq, _, d = x_ref.shape
    nlt = d // LANES
    spt = SEQS_PER_TRIP
    pair = spt * CHUNK

    @pl.when(i == 0)
    def _():
        hist_s[...] = jnp.zeros_like(hist_s)
        hc_s[...] = jnp.zeros_like(hc_s)

    lam_half = (-0.5 * LRU_C) * _softplus(-lam_ref[...])
    is_first = i == 0

    def branch_body(p, c):
        row = pl.multiple_of(p * pair, pair)
        x = jnp.concatenate([x_ref[spt * p + k] for k in range(spt)], axis=0)
        xnb = _rms_norm(x, norm_ref[...]).astype(BF16)
        v = _layer_norm(_gelu2(_dot(xnb, win_ref[:, d:2 * d])), lng_ref[...], lnb_ref[...],
                        4.0 * EPS)

        @pl.when(i == last)
        def _():
            for k in range(spt):
                v_ref[spt * p + k] = v[k * CHUNK:(k + 1) * CHUNK]

        vb = v.astype(BF16)
        dh = d // H_A
        z_rows = []
        for k in range(spt):
            zs = [_dot(ws_ref[h], vb[k * CHUNK:(k + 1) * CHUNK, h * dh:(h + 1) * dh])
                  for h in range(H_A)]
            z_rows.append(jnp.concatenate(zs, axis=-1) + bs_ref[...])
        z = jnp.concatenate(z_rows, axis=0)
        ya = _gelu2(_dot(xnb, win_ref[:, 0:d])) * z
        ga2 = 1.0 + jnp.tanh(_dot(xnb, win_ref[:, 3 * d:4 * d]))
        ya_s[pl.ds(row, pair), :] = (ga2 * ya).astype(BF16)
        gb_s[pl.ds(row, pair), :] = (1.0 + jnp.tanh(_dot(xnb, win_ref[:, 4 * d:5 * d]))).astype(BF16)
        xr = _dot(xnb, win_ref[:, 2 * d:3 * d])
        xcs = []
        for k in range(spt):
            s = spt * p + k
            xk = xr[k * CHUNK:(k + 1) * CHUNK]
            hist = hist_s[s]
            xc = cb_ref[...] + xk * cw_ref[CONV_W - 1:CONV_W, :]
            for j in range(1, CONV_W):
                xc = xc + _shift_rows(xk, hist, j) * cw_ref[CONV_W - 1 - j:CONV_W - j, :]
            hist_s[s] = xk[CHUNK - SUBLANES:]
            xcs.append(xc)
        xc = jnp.concatenate(xcs, axis=0)
        a, mult_half, ix2 = _lru_gates(xc, wbd_ref, ba_ref[...], bx_ref[...], lam_half)
        bt = mult_half * ix2
        for k in range(spt):
            s = spt * p + k
            for j in range(nlt):
                a_s[j, pl.ds(s, CHUNK, stride=nseq), :] = a[k * CHUNK:(k + 1) * CHUNK, j * LANES:(j + 1) * LANES]
                b_s[j, pl.ds(s, CHUNK, stride=nseq), :] = bt[k * CHUNK:(k + 1) * CHUNK, j * LANES:(j + 1) * LANES]

        @pl.when(is_first)
        def _():
            for k in range(spt):
                first = 0.5 * ix2[k * CHUNK:k * CHUNK + 1]
                for j in range(nlt):
                    b_s[j, pl.ds(spt * p + k, 1), :] = first[:, j * LANES:(j + 1) * LANES]

        return c

    lax.fori_loop(0, nseq // spt, branch_body, 0)

    def scan_body(tt, hs):
        for u in range(SCAN_UNROLL):
            row = pl.multiple_of((tt * SCAN_UNROLL + u) * nseq, nseq)
            out = []
            for j in range(nlt):
                h = a_s[j, pl.ds(row, nseq), :] * hs[j] + b_s[j, pl.ds(row, nseq), :]
                b_s[j, pl.ds(row, nseq), :] = h
                out.append(h)
            hs = tuple(out)
        return hs

    h0 = tuple(hc_s[:, j * LANES:(j + 1) * LANES] for j in range(nlt))
    hs = lax.fori_loop(0, CHUNK // SCAN_UNROLL, scan_body, h0)
    hfin = jnp.concatenate(hs, axis=-1)
    hc_s[...] = hfin
    h_ref[...] = hfin

    def out_body(p, c):
        row = pl.multiple_of(p * pair, pair)
        hrows = []
        for k in range(spt):
            hrows.append(jnp.concatenate(
                [b_s[j, pl.ds(spt * p + k, CHUNK, stride=nseq), :] for j in range(nlt)], axis=-1))
        h = jnp.concatenate(hrows, axis=0)
        merged = (ya_s[pl.ds(row, pair), :].astype(F32)
                  + gb_s[pl.ds(row, pair), :].astype(F32) * h).astype(BF16)
        o = _dot(merged, wout_ref[...])
        for k in range(spt):
            y_ref[spt * p + k] = x_ref[spt * p + k] + o[k * CHUNK:(k + 1) * CHUNK]
        return c

    lax.fori_loop(0, nseq // spt, out_body, 0)

    @pl.when(i == last)
    def _():
        conv_ref[...] = hist_s[:, SUBLANES - (CONV_W - 1):, :]


def _const_spec(shape):
    nd = len(shape)
    return pl.BlockSpec(shape, lambda i, *_, _n=nd: (0,) * _n, pipeline_mode=pl.Buffered(1))


def _layer_spec(stacked, layer):
    shape = stacked.shape[1:]
    nd = len(shape)
    return pl.BlockSpec((None,) + tuple(shape), lambda i, *_, _n=nd: (layer,) + (0,) * _n,
                        pipeline_mode=pl.Buffered(1))


MIXER_WEIGHTS = ('norm1', 'w_in', 'ln_g', 'ln_b', 'ws', 'bs', 'conv_w', 'conv_b', 'lam', 'wbd',
                 'b_a', 'b_x', 'w_out')


def _mixer_prompt(x, wts, layer):
    nseq, seq, d = x.shape
    assert seq % CHUNK == 0 and nseq == SUBLANES and d % 256 == 0
    n_chunks = seq // CHUNK
    nlt = d // LANES
    weights = tuple(wts[k] for k in MIXER_WEIGHTS)
    x_spec = pl.BlockSpec((nseq, CHUNK, d), lambda i: (0, i, 0))
    out_shape = (
        jax.ShapeDtypeStruct((nseq, seq, d), F32),
        jax.ShapeDtypeStruct((nseq, d), F32),
        jax.ShapeDtypeStruct((nseq, CONV_W - 1, d), F32),
        jax.ShapeDtypeStruct((nseq, CHUNK, d), F32),
    )
    out_specs = (
        x_spec,
        _const_spec((nseq, d)),
        _const_spec((nseq, CONV_W - 1, d)),
        _const_spec((nseq, CHUNK, d)),
    )
    scratch = [
        pltpu.VMEM((nseq, SUBLANES, d), F32),
        pltpu.VMEM((nlt, CHUNK * nseq, LANES), F32),
        pltpu.VMEM((nlt, CHUNK * nseq, LANES), F32),
        pltpu.VMEM((nseq * CHUNK, d), BF16),
        pltpu.VMEM((nseq * CHUNK, d), BF16),
        pltpu.VMEM((nseq, d), F32),
    ]
    return pl.pallas_call(
        _mixer_prompt_kernel,
        grid=(n_chunks,),
        in_specs=[x_spec] + [_layer_spec(w, layer) for w in weights],
        out_specs=out_specs,
        out_shape=out_shape,
        scratch_shapes=scratch,
        compiler_params=pltpu.CompilerParams(
            dimension_semantics=("arbitrary",), vmem_limit_bytes=VMEM_LIMIT),
        name="mixer_prompt",
    )(x, *weights)


def _mixer_sample_kernel(x_ref, h0_ref, cst_ref, norm_ref, win_ref, lng_ref, lnb_ref, wsc_ref,
                         bsc_ref, cw_ref, cb_ref, lam_ref, wbd_ref, ba_ref, bx_ref, wout_ref,
                         y_ref, h_ref, conv_ref, v_ref):
    nt, ns, d = x_ref.shape
    at = lambda a, t: a[t * ns:(t + 1) * ns]
    lam_half = (-0.5 * LRU_C) * _softplus(-lam_ref[...])
    x = x_ref[...].reshape(nt * ns, d)
    xnb = _rms_norm(x, norm_ref[...]).astype(BF16)
    v = _layer_norm(_gelu2(_dot(xnb, win_ref[:, d:2 * d])), lng_ref[...], lnb_ref[...], 4.0 * EPS)
    v_ref[...] = v.reshape(nt, ns, d)
    zs = []
    for t in range(nt):
        z = bsc_ref[t:t + 1, :]
        for s in range(t + 1):
            z = z + wsc_ref[t, s:s + 1, :] * at(v, s)
        zs.append(z)
    ya = _gelu2(_dot(xnb, win_ref[:, 0:d])) * jnp.concatenate(zs, axis=0)
    ga2 = 1.0 + jnp.tanh(_dot(xnb, win_ref[:, 3 * d:4 * d]))
    gb2 = 1.0 + jnp.tanh(_dot(xnb, win_ref[:, 4 * d:5 * d]))
    xr = _dot(xnb, win_ref[:, 2 * d:3 * d])
    xp = [cst_ref[k] for k in range(CONV_W - 1)] + [at(xr, t) for t in range(nt)]
    xcs = []
    for t in range(nt):
        xc = cb_ref[...]
        for k in range(CONV_W):
            xc = xc + xp[t + k] * cw_ref[k:k + 1, :]
        xcs.append(xc)
    a, mult_half, ix2 = _lru_gates(jnp.concatenate(xcs, axis=0), wbd_ref, ba_ref[...],
                                   bx_ref[...], lam_half)
    bt = mult_half * ix2
    h = h0_ref[...]
    hs = []
    for t in range(nt):
        h = at(a, t) * h + at(bt, t)
        hs.append(h)
    merged = (ga2 * ya + gb2 * jnp.concatenate(hs, axis=0)).astype(BF16)
    y_ref[0:nt] = (x + _dot(merged, wout_ref[...])).reshape(nt, ns, d)
    if y_ref.shape[0] > nt:
        y_ref[nt:] = jnp.zeros((y_ref.shape[0] - nt, ns, d), F32)
    h_ref[...] = h
    for k in range(CONV_W - 1):
        conv_ref[k] = xp[nt + k]


def _mixer_sample(x, h0, conv_state, wts, layer, nt, pad_t):
    _, ns, d = x.shape
    names = MIXER_WEIGHTS[:4] + ('wsc', 'bsc') + MIXER_WEIGHTS[6:]
    stacked = (h0, conv_state) + tuple(wts[k] for k in names)
    out_shape = (
        jax.ShapeDtypeStruct((nt + pad_t, ns, d), F32),
        jax.ShapeDtypeStruct((ns, d), F32),
        jax.ShapeDtypeStruct((CONV_W - 1, ns, d), F32),
        jax.ShapeDtypeStruct((nt, ns, d), F32),
    )
    return pl.pallas_call(
        _mixer_sample_kernel,
        grid=(1,),
        in_specs=[_const_spec((nt, ns, d))] + [_layer_spec(w, layer) for w in stacked],
        out_specs=tuple(_const_spec(s.shape) for s in out_shape),
        out_shape=out_shape,
        compiler_params=pltpu.CompilerParams(
            dimension_semantics=("arbitrary",), vmem_limit_bytes=VMEM_LIMIT),
        name="mixer_sample",
    )(x, *stacked)


def _to_slab(ref, val):
    t, d = val.shape
    nlt = d // LANES
    for j in range(nlt):
        ref[pl.ds(j, t, stride=nlt), :] = val[:, j * LANES:(j + 1) * LANES]


def _from_slab(ref, t, nlt):
    return jnp.concatenate([ref[pl.ds(j, t, stride=nlt), :] for j in range(nlt)], axis=-1)


def _tokens_from_slab(slab_ref, nlt):
    return slab_ref[...].reshape(slab_ref.shape[0] // nlt, nlt, LANES).astype(BF16)


def _tokens_to_slab(slab_ref, tokens):
    slab_ref[...] = tokens.astype(F32).reshape(slab_ref.shape)


def _router_kernel(yp_ref, ys_ref, norm_ref, wr_ref, br_ref, tri_ref, low_ref, xn_ref, s1_ref,
                   s2_ref, w1_ref, w2_ref, cnt_ref, slab_s, *, n_prompt_tiles):
    y = jnp.where(pl.program_id(0) < n_prompt_tiles, yp_ref[...], ys_ref[...])
    xn = _rms_norm(y, norm_ref[...])
    _to_slab(slab_s, xn)
    xn_ref[...] = _tokens_from_slab(slab_s, xn_ref.shape[1])
    lt = lax.dot_general(wr_ref[...], xn.astype(BF16), (((1,), (1,)), ((), ())),
                         preferred_element_type=F32) + br_ref[...]
    tile = lt.shape[1]
    row = lax.broadcasted_iota(jnp.int32, (SUBLANES, tile), 0).astype(F32)
    neg = jnp.float32(-jnp.inf)
    big = jnp.float32(SUBLANES)

    def argmax_first(vals):
        m = jnp.max(vals, axis=0, keepdims=True)
        return m, jnp.min(jnp.where(vals == m, row, big), axis=0, keepdims=True)

    def group_block(x, gi):
        out = x[SUBLANES:2 * SUBLANES]
        for g in range(1, N_GROUPS):
            out = jnp.where(gi == g, x[(g + 1) * SUBLANES:(g + 2) * SUBLANES], out)
        return out

    gl = jnp.where(row < N_GROUPS, lt[0:SUBLANES], neg)
    gm, gi = argmax_first(gl)
    pg_top = 1.0 / jnp.sum(jnp.exp(gl - gm), axis=0, keepdims=True)
    el = group_block(lt, gi)
    m1, i1 = argmax_first(el)
    m2, i2 = argmax_first(jnp.where(row == i1, neg, el))
    e2 = jnp.exp(m2 - m1)
    w1 = pg_top / (1.0 + e2)
    w2 = pg_top * e2 / (1.0 + e2)
    hot1 = row == i1
    hot2 = row == i2
    picked = jnp.where(jnp.logical_or(hot1, hot2), 1.0, 0.0)
    zero = jnp.zeros_like(picked)
    onehot = jnp.concatenate(
        [zero] + [jnp.where(gi == g, picked, 0.0) for g in range(N_GROUPS)] + [zero],
        axis=0).astype(BF16)
    span = tri_ref.shape[0]
    parts = []
    cnt = jnp.zeros((ROUTE_ROWS, 1), F32)
    for k in range(tile // span):
        blk = onehot[:, k * span:(k + 1) * span]
        parts.append(_dot(blk, tri_ref[...]) + cnt)
        cnt = cnt + jnp.sum(blk.astype(F32), axis=1, keepdims=True)
    before = jnp.concatenate(parts, axis=1)
    lower = jnp.sum(_dot(low_ref[...], onehot), axis=1, keepdims=True)
    slot = group_block(before + lower, gi)
    p1 = jnp.sum(jnp.where(hot1, slot, 0.0), axis=0, keepdims=True)
    p2 = jnp.sum(jnp.where(hot2, slot, 0.0), axis=0, keepdims=True)
    s1_ref[0] = p1.astype(jnp.int32)
    s2_ref[0] = p2.astype(jnp.int32)
    w1_ref[0] = w1
    w2_ref[0] = w2
    cnt_ref[0] = jnp.broadcast_to(cnt, cnt_ref.shape[1:])


def _token_specs(tile, d, n_prompt_tiles):
    last = n_prompt_tiles - 1
    return (pl.BlockSpec((tile, d), lambda i, *_: (jnp.minimum(i, last), 0)),
            pl.BlockSpec((tile, d), lambda i, *_: (jnp.maximum(i - n_prompt_tiles, 0), 0)))


def _router(yp, ys, wts, layer, tile):
    d = yp.shape[1]
    nlt = d // LANES
    n_prompt_tiles = yp.shape[0] // tile
    n_tiles = n_prompt_tiles + ys.shape[0] // tile
    span = min(tile, ROUTE_SPAN)
    assert tile % span == 0
    tri = jnp.triu(jnp.ones((span, span), BF16), 1)
    low = jnp.tril(jnp.ones((ROUTE_ROWS, ROUTE_ROWS), BF16), -1)
    per_token = pl.BlockSpec((1, 1, tile), lambda i: (i, 0, 0))
    return pl.pallas_call(
        functools.partial(_router_kernel, n_prompt_tiles=n_prompt_tiles),
        grid=(n_tiles,),
        in_specs=list(_token_specs(tile, d, n_prompt_tiles)) + [
            _layer_spec(wts['norm2'], layer), _layer_spec(wts['w_route'], layer),
            _layer_spec(wts['b_route'], layer), _const_spec((span, span)),
            _const_spec((ROUTE_ROWS, ROUTE_ROWS))],
        out_specs=(pl.BlockSpec((tile, nlt, LANES), lambda i: (i, 0, 0)),
                   per_token, per_token, per_token, per_token,
                   pl.BlockSpec((1, ROUTE_ROWS, LANES), lambda i: (i, 0, 0))),
        out_shape=(jax.ShapeDtypeStruct((n_tiles * tile, nlt, LANES), BF16),
                   jax.ShapeDtypeStruct((n_tiles, 1, tile), jnp.int32),
                   jax.ShapeDtypeStruct((n_tiles, 1, tile), jnp.int32),
                   jax.ShapeDtypeStruct((n_tiles, 1, tile), F32),
                   jax.ShapeDtypeStruct((n_tiles, 1, tile), F32),
                   jax.ShapeDtypeStruct((n_tiles, ROUTE_ROWS, LANES), F32)),
        scratch_shapes=[pltpu.VMEM((tile * nlt, LANES), F32)],
        compiler_params=pltpu.CompilerParams(
            dimension_semantics=("arbitrary",), vmem_limit_bytes=VMEM_LIMIT),
        name="moe_router",
    )(yp, ys, wts['norm2'], wts['w_route'], wts['b_route'], tri, low)


def _run_copy(tile_idx, e, lo_ref, cnt_ref, dst_ref, local, remote, sem, to_remote):
    k = tile_idx * N_EXPERTS + e
    loc = local.at[pl.ds(lo_ref[k], cnt_ref[k])]
    rem = remote.at[pl.ds(dst_ref[k], cnt_ref[k])]
    return pltpu.make_async_copy(loc, rem, sem) if to_remote else pltpu.make_async_copy(rem, loc, sem)


def _start_runs(tile_idx, lo_ref, cnt_ref, dst_ref, local, remote, sem, to_remote):
    for e in range(N_EXPERTS):
        @pl.when(cnt_ref[tile_idx * N_EXPERTS + e] > 0)
        def _():
            _run_copy(tile_idx, e, lo_ref, cnt_ref, dst_ref, local, remote, sem, to_remote).start()


def _wait_runs(local, sem):
    pltpu.make_async_copy(local, local, sem).wait()


def _dispatch_kernel(lo_ref, cnt_ref, dst_ref, padst_ref, padn_ref,
                     s1_ref, s2_ref, x_ref, xs_ref, loc, zeros, sem, *, unroll):
    i = pl.program_id(0)
    last = pl.num_programs(0) - 1
    tile = x_ref.shape[0]
    slot = i % 2
    buf = loc.at[slot]

    @pl.when(i >= 2)
    def _():
        _wait_runs(buf, sem.at[slot])

    def body(tt, c):
        for u in range(unroll):
            t = tt * unroll + u
            row = x_ref[t]
            buf[s1_ref[t]] = row
            buf[s2_ref[t]] = row
        return c

    lax.fori_loop(0, tile // unroll, body, 0)
    _start_runs(i, lo_ref, cnt_ref, dst_ref, buf, xs_ref, sem.at[slot], True)

    @pl.when(i == last)
    def _():
        zeros[...] = jnp.zeros_like(zeros)

        def pad_copy(e):
            n = padn_ref[e]
            dst = xs_ref.at[pl.ds(padst_ref[e], n)]
            return pltpu.make_async_copy(zeros.at[pl.ds(0, n)], dst, sem.at[2])

        def each_pad(fn):
            def b(e, c):
                @pl.when(padn_ref[e] > 0)
                def _():
                    fn(e)
                return c
            lax.fori_loop(0, padn_ref.shape[0], b, 0)

        each_pad(lambda e: pad_copy(e).start())
        _wait_runs(buf, sem.at[slot])

        @pl.when(i >= 1)
        def _():
            _wait_runs(loc.at[1 - slot], sem.at[1 - slot])

        each_pad(lambda e: pad_copy(e).wait())


def _smem_tile(tile):
    return pl.BlockSpec((None, None, tile), lambda i, *_: (i, 0, 0), memory_space=pltpu.SMEM)


def _dispatch(xn_slab, s1, s2, tables, tile, n_rows):
    token = xn_slab.shape[1:]
    return pl.pallas_call(
        functools.partial(_dispatch_kernel, unroll=16),
        grid_spec=pltpu.PrefetchScalarGridSpec(
            num_scalar_prefetch=5,
            grid=(xn_slab.shape[0] // tile,),
            in_specs=[_smem_tile(tile), _smem_tile(tile),
                      pl.BlockSpec((tile,) + token, lambda i, *_: (i, 0, 0))],
            out_specs=pl.BlockSpec(memory_space=pl.ANY),
            scratch_shapes=[pltpu.VMEM((2, 2 * tile) + token, xn_slab.dtype),
                            pltpu.VMEM((MOE_BLOCK,) + token, xn_slab.dtype),
                            pltpu.SemaphoreType.DMA((3,))]),
        out_shape=jax.ShapeDtypeStruct((n_rows,) + token, xn_slab.dtype),
        compiler_params=pltpu.CompilerParams(
            dimension_semantics=("arbitrary",), vmem_limit_bytes=VMEM_LIMIT),
        name="moe_dispatch",
    )(*tables, s1, s2, xn_slab)


def _experts_kernel(blk_ref, next_ref, nused_ref, xs_ref, wg_ref, wu_ref, wd_ref, ys_ref,
                    xin, yout, wg_f, wu_f, wd_f, wg_s, wu_s, wd_s, slab_in,
                    sem_in, sem_out, sem_w, *, layer):
    n_blocks = ys_ref.shape[0] // MOE_BLOCK
    nlt = xs_ref.shape[1]
    n_used = nused_ref[0]
    block = lambda ref, b: ref.at[pl.ds(pl.multiple_of(b * MOE_BLOCK, MOE_BLOCK), MOE_BLOCK)]
    ring = xin.shape[0]
    rows_in = lambda b: pltpu.make_async_copy(block(xs_ref, b), xin.at[b % ring],
                                              sem_in.at[b % ring])
    rows_out = lambda b, slot: pltpu.make_async_copy(yout.at[slot], block(ys_ref, b),
                                                     sem_out.at[slot])

    def weights(e, slot):
        pairs = ((wg_ref, wg_f), (wu_ref, wu_f), (wd_ref, wd_f))
        return [pltpu.make_async_copy(src.at[layer, e], dst.at[slot], sem_w.at[slot])
                for src, dst in pairs]

    for k in range(ring - 1):
        @pl.when(k < n_used)
        def _():
            rows_in(k).start()

    wring = wg_f.shape[0]

    def later_expert(e, hops):
        for _ in range(hops):
            e = jnp.where(e >= 0, next_ref[jnp.maximum(e, 0)], -1)
        return e

    @pl.when(n_used > 0)
    def _():
        for k in range(wring - 1):
            e_k = later_expert(blk_ref[0], k)

            @pl.when(e_k >= 0)
            def _():
                for c in weights(e_k, k):
                    c.start()

    def step(b, wcount):
        slot = b % 2
        e = blk_ref[b]
        rows_in(b).wait()

        @pl.when(b + ring - 1 < n_used)
        def _():
            rows_in(b + ring - 1).start()

        new_expert = jnp.logical_or(b == 0, e != blk_ref[jnp.maximum(b - 1, 0)])
        wcount = jnp.where(new_expert, wcount + 1, wcount)

        @pl.when(new_expert)
        def _():
            wslot = wcount % wring
            for c in weights(e, wslot):
                c.wait()
            wg_s[...] = wg_f[wslot].astype(BF16)
            wu_s[...] = wu_f[wslot].astype(BF16)
            wd_s[...] = wd_f[wslot].astype(BF16)
            e_far = later_expert(e, wring - 1)

            @pl.when(e_far >= 0)
            def _():
                for c in weights(e_far, (wcount + wring - 1) % wring):
                    c.start()

        _tokens_to_slab(slab_in, xin[b % ring])
        x = _from_slab(slab_in, MOE_BLOCK, nlt).astype(BF16)
        hg = _dot(x, wg_s[...])
        hu = _dot(x, wu_s[...])
        h = (hg * _sigmoid(hg) * hu).astype(BF16)
        o = _dot(h, wd_s[...]).reshape(MOE_BLOCK, nlt, LANES).astype(BF16)

        @pl.when(b >= 2)
        def _():
            rows_out(b - 2, slot).wait()

        yout[slot] = o
        rows_out(b, slot).start()
        return wcount

    lax.fori_loop(0, n_used, step, jnp.int32(-1))

    @pl.when(n_used >= 2)
    def _():
        rows_out(n_used - 2, n_used % 2).wait()

    @pl.when(n_used >= 1)
    def _():
        rows_out(n_used - 1, (n_used - 1) % 2).wait()

    yout[0] = jnp.zeros(yout.shape[1:], yout.dtype)

    def fill(b, c):
        rows_out(b, 0).start()
        rows_out(b, 0).wait()
        return c

    lax.fori_loop(n_used, n_blocks, fill, 0)


def _experts(xs, blk_expert, next_expert, n_used, wts, layer):
    n_rows, nlt, _ = xs.shape
    d = nlt * LANES
    f = wts['w_gate'].shape[-1]
    token = (nlt, LANES)
    any_spec = pl.BlockSpec(memory_space=pl.ANY)
    return pl.pallas_call(
        functools.partial(_experts_kernel, layer=layer),
        grid_spec=pltpu.PrefetchScalarGridSpec(
            num_scalar_prefetch=3,
            grid=(1,),
            in_specs=[any_spec] * 4,
            out_specs=any_spec,
            scratch_shapes=[pltpu.VMEM((EXPERT_ROW_RING, MOE_BLOCK) + token, xs.dtype),
                            pltpu.VMEM((2, MOE_BLOCK) + token, xs.dtype),
                            pltpu.VMEM((EXPERT_WEIGHT_RING, d, f), F32),
                            pltpu.VMEM((EXPERT_WEIGHT_RING, d, f), F32),
                            pltpu.VMEM((EXPERT_WEIGHT_RING, f, d), F32),
                            pltpu.VMEM((d, f), BF16), pltpu.VMEM((d, f), BF16),
                            pltpu.VMEM((f, d), BF16),
                            pltpu.VMEM((MOE_BLOCK * nlt, LANES), F32),
                            pltpu.SemaphoreType.DMA((EXPERT_ROW_RING,)),
                            pltpu.SemaphoreType.DMA((2,)),
                            pltpu.SemaphoreType.DMA((EXPERT_WEIGHT_RING,))]),
        out_shape=jax.ShapeDtypeStruct(xs.shape, xs.dtype),
        compiler_params=pltpu.CompilerParams(
            dimension_semantics=("arbitrary",), vmem_limit_bytes=VMEM_LIMIT),
        name="moe_experts",
    )(blk_expert, next_expert, n_used, xs, wts['w_gate'], wts['w_up'], wts['w_down'])


def _combine_kernel(lo_ref, cnt_ref, dst_ref, s1_ref, s2_ref, w1_ref, w2_ref, yp_ref, ys_ref,
                    nf_ref, rows_ref, op_ref, os_ref, loc, acc, sem,
                    *, unroll, final_norm, n_prompt_tiles):
    i = pl.program_id(0)
    tile = yp_ref.shape[0]
    nlt = loc.shape[2]
    slot = i % 2
    buf = loc.at[slot]
    fetch = lambda t, s: _start_runs(t, lo_ref, cnt_ref, dst_ref, loc.at[s], rows_ref, sem.at[s],
                                     False)

    @pl.when(i == 0)
    def _():
        fetch(0, 0)

    @pl.when(i + 1 < pl.num_programs(0))
    def _():
        fetch(i + 1, 1 - slot)

    _wait_runs(buf, sem.at[slot])

    def body(tt, c):
        for u in range(unroll):
            t = tt * unroll + u
            r1 = buf[s1_ref[t]].astype(F32)
            r2 = buf[s2_ref[t]].astype(F32)
            acc[pl.ds(pl.multiple_of(t * nlt, nlt), nlt), :] = w1_ref[t] * r1 + w2_ref[t] * r2
        return c

    lax.fori_loop(0, tile // unroll, body, 0)
    def finish(y_ref, o_ref):
        out = y_ref[...] + acc[...].reshape(tile, nlt, LANES).reshape(tile, nlt * LANES)
        if final_norm:
            out = _rms_norm(out, nf_ref[...])
        o_ref[...] = out

    is_prompt = i < n_prompt_tiles
    pl.when(is_prompt)(lambda: finish(yp_ref, op_ref))
    pl.when(jnp.logical_not(is_prompt))(lambda: finish(ys_ref, os_ref))


def _combine(yp, ys, rows, meta, tables, norm_f, final_norm, tile):
    d = yp.shape[1]
    token = rows.shape[1:]
    n_prompt_tiles = yp.shape[0] // tile
    n_tiles = n_prompt_tiles + ys.shape[0] // tile
    tok_p, tok_s = _token_specs(tile, d, n_prompt_tiles)
    return pl.pallas_call(
        functools.partial(_combine_kernel, unroll=16, final_norm=final_norm,
                          n_prompt_tiles=n_prompt_tiles),
        grid_spec=pltpu.PrefetchScalarGridSpec(
            num_scalar_prefetch=3,
            grid=(n_tiles,),
            in_specs=[_smem_tile(tile)] * 4 + [tok_p, tok_s, _const_spec((1, d)),
                                               pl.BlockSpec(memory_space=pl.ANY)],
            out_specs=(tok_p, tok_s),
            scratch_shapes=[pltpu.VMEM((2, 2 * tile) + token, rows.dtype),
                            pltpu.VMEM((tile * token[0], LANES), F32),
                            pltpu.SemaphoreType.DMA((2,))]),
        out_shape=(jax.ShapeDtypeStruct(yp.shape, F32), jax.ShapeDtypeStruct(ys.shape, F32)),
        compiler_params=pltpu.CompilerParams(
            dimension_semantics=("arbitrary",), vmem_limit_bytes=VMEM_LIMIT),
        name="moe_combine",
    )(*tables, *meta, yp, ys, norm_f, rows)


def _moe(yp, ys, wts, layer, norm_f, final_norm, tile):
    d = yp.shape[1]
    n = yp.shape[0] + ys.shape[0]
    assert yp.shape[0] % tile == 0 and ys.shape[0] % tile == 0 and d % LANES == 0
    xn_slab, s1, s2, w1, w2, cnt = _router(yp, ys, wts, layer, tile)
    cnt = cnt[:, SUBLANES:SUBLANES + N_EXPERTS, 0].astype(jnp.int32)
    lo = jnp.cumsum(cnt, axis=1) - cnt
    seg = jnp.sum(cnt, axis=0)
    seg_pad = (seg + MOE_BLOCK - 1) // MOE_BLOCK * MOE_BLOCK
    seg_end = jnp.cumsum(seg_pad)
    seg_start = seg_end - seg_pad
    dst = seg_start[None, :] + jnp.cumsum(cnt, axis=0) - cnt
    n_blocks = (2 * n + N_EXPERTS * (MOE_BLOCK - 1)) // MOE_BLOCK
    blk_ids = jnp.arange(n_blocks, dtype=jnp.int32)
    blk_expert = jnp.minimum(
        jnp.sum((seg_end[None, :] <= blk_ids[:, None] * MOE_BLOCK).astype(jnp.int32), axis=1),
        N_EXPERTS - 1)
    n_used = seg_end[-1:] // MOE_BLOCK
    pad_start = jnp.concatenate([seg_start + seg, blk_ids * MOE_BLOCK])
    pad_rows = jnp.concatenate([seg_pad - seg, jnp.where(blk_ids >= n_used[0], MOE_BLOCK, 0)])
    flat = lambda a: a.reshape(-1)
    runs = (flat(lo), flat(cnt), flat(dst))
    ids = jnp.arange(N_EXPERTS, dtype=jnp.int32)
    later = jnp.where((seg_pad[None, :] > 0) & (ids[None, :] > ids[:, None]), ids[None, :],
                      N_EXPERTS)
    next_expert = jnp.min(later, axis=1)
    next_expert = jnp.where(next_expert < N_EXPERTS, next_expert, -1)
    xs = _dispatch(xn_slab, s1, s2, runs + (pad_start, pad_rows), tile, n_blocks * MOE_BLOCK)
    rows = _experts(xs, blk_expert, next_expert, n_used, wts, layer)
    return _combine(yp, ys, rows, (s1, s2, w1, w2), runs, norm_f, final_norm, tile)


def _prep_weights(norm1, w_in, ln_g, ln_b, w_s, b_s, conv_w, conv_b, lru_lambda, w_rg_a, b_rg_a,
                  w_rg_x, b_rg_x, w_out, norm2, w_route_group, b_route_group, w_route_expert,
                  b_route_expert, w_gate, w_up, w_down, n_sample_t):
    depth, d = w_in.shape[0], w_in.shape[1]
    dh = d // H_A
    row = lambda a: a.reshape(depth, 1, -1).astype(F32)
    causal = jnp.tril(jnp.ones((CHUNK, CHUNK), dtype=bool))
    ws = jnp.where(causal, 0.5 * w_s, 0.0)
    bs = jnp.repeat(jnp.swapaxes(0.5 * b_s, 1, 2), dh, axis=2)
    in_scale = jnp.concatenate([jnp.ones((3 * d,), F32), jnp.full((2 * d,), 0.5, F32)])

    def pair_bd(w):
        w = w.reshape(depth, H_B // 2, 2, w.shape[-2], w.shape[-1])
        z = jnp.zeros_like(w[:, :, 0])
        return jnp.concatenate([jnp.concatenate([w[:, :, 0], z], axis=-1),
                                jnp.concatenate([z, w[:, :, 1]], axis=-1)], axis=-2)

    gap = SUBLANES - N_GROUPS
    tail = ROUTE_ROWS - SUBLANES - N_EXPERTS
    w_route = jnp.concatenate(
        [jnp.swapaxes(w_route_group, 1, 2), jnp.zeros((depth, gap, d), F32),
         jnp.swapaxes(w_route_expert, 1, 2), jnp.zeros((depth, tail, d), F32)], axis=1)
    b_route = jnp.concatenate(
        [b_route_group, jnp.zeros((depth, gap), F32), b_route_expert,
         jnp.zeros((depth, tail), F32)], axis=1)[..., None]
    nt = n_sample_t
    return dict(
        norm1=row(norm1), w_in=(w_in * in_scale).astype(BF16), ln_g=row(ln_g), ln_b=row(ln_b),
        ws=ws.astype(BF16), bs=bs.astype(F32), conv_w=conv_w.astype(F32), conv_b=row(conv_b),
        lam=row(lru_lambda),
        wbd=(0.5 * jnp.concatenate([pair_bd(w_rg_a), pair_bd(w_rg_x)], axis=-1)).astype(BF16),
        b_a=row(0.5 * b_rg_a), b_x=row(0.5 * b_rg_x), w_out=(0.5 * w_out).astype(BF16),
        norm2=row(norm2), w_route=w_route.astype(BF16), b_route=b_route.astype(F32),
        w_gate=w_gate, w_up=w_up, w_down=w_down,
        wsc=jnp.repeat(jnp.transpose(ws[:, :, :nt, :nt], (0, 2, 3, 1)), dh, axis=3).astype(F32),
        bsc=bs[:, :nt].astype(F32),
    )


def kernel(x_prompt, x_sample, state_lru_h, state_conv, norm1, w_in, ln_g, ln_b, w_s, b_s, conv_w, conv_b, lru_lambda, w_rg_a, b_rg_a, w_rg_x, b_rg_x, w_out, norm2, w_route_group, b_route_group, w_route_expert, b_route_expert, w_gate, w_up, w_down, norm_f):
    depth = w_in.shape[0]
    nb, seq, d = x_prompt.shape
    ns, nt, _ = x_sample.shape
    assert nt <= CHUNK
    nf = norm_f.reshape(1, d).astype(F32)
    wts = _prep_weights(norm1, w_in, ln_g, ln_b, w_s, b_s, conv_w, conv_b, lru_lambda, w_rg_a,
                        b_rg_a, w_rg_x, b_rg_x, w_out, norm2, w_route_group, b_route_group,
                        w_route_expert, b_route_expert, w_gate, w_up, w_down, nt)
    conv_state = jnp.transpose(state_conv, (0, 2, 1, 3))
    assert (nb * seq) % MOE_TILE == 0
    sample_pad = (-nt * ns) % MOE_TILE
    assert sample_pad % ns == 0
    pad_t = sample_pad // ns
    yp = x_prompt
    ys = jnp.transpose(x_sample, (1, 0, 2))
    hp, cp, vp, hs, cs, vs = [], [], [], [], [], []
    for l in range(depth):
        yp, h, c, v = _mixer_prompt(yp, wts, l)
        hp.append(h)
        cp.append(c)
        vp.append(v)
        ys, h, c, v = _mixer_sample(ys, state_lru_h, conv_state, wts, l, nt, pad_t)
        hs.append(h)
        cs.append(c)
        vs.append(v)
        yp, ys = _moe(yp.reshape(nb * seq, d), ys.reshape((nt + pad_t) * ns, d), wts, l, nf,
                      l == depth - 1, MOE_TILE)
        yp = yp.reshape(nb, seq, d)
        ys = ys.reshape(nt + pad_t, ns, d)
    to_seq_major = lambda a: jnp.transpose(jnp.stack(a), (0, 2, 1, 3))
    return (yp, jnp.transpose(ys[:nt], (1, 0, 2)), jnp.stack(hp), jnp.stack(cp), jnp.stack(vp),
            jnp.stack(hs), to_seq_major(cs), to_seq_major(vs))
```

```python
import functools

import jax
import jax.numpy as jnp
from jax import lax
from jax.experimental import pallas as pl
from jax.experimental.pallas import tpu as pltpu

F32 = jnp.float32
BF16 = jnp.bfloat16

CHUNK = 128
H_A = 4
H_B = 8
CONV_W = 4
LRU_C = 8.0
N_GROUPS = 4
E_PER_GROUP = 8
N_EXPERTS = N_GROUPS * E_PER_GROUP
EPS = 1e-6

LANES = 128
SUBLANES = 8
BF16_ROWS = 16
V7X_VMEM_BYTES = 64 * 1024 * 1024
VMEM_LIMIT = V7X_VMEM_BYTES - 4 * 1024 * 1024
SCAN_UNROLL = 4
SEQS_PER_TRIP = 8
MOE_TILE = 1024
ROUTE_SPAN = 256
MOE_BLOCK = 512
EXPERT_ROW_RING = 4
EXPERT_WEIGHT_RING = 3
assert E_PER_GROUP == SUBLANES and N_GROUPS <= SUBLANES
ROUTE_ROWS = -(-SUBLANES * (N_GROUPS + 1) // BF16_ROWS) * BF16_ROWS


def _dot(a, b):
    return jnp.dot(a, b, preferred_element_type=F32)


def _sigmoid(x):
    return 0.5 * (jnp.tanh(0.5 * x) + 1.0)


def _gelu2(x):
    c = 0.7978845608028654
    return x * (1.0 + jnp.tanh(x * (c + (c * 0.044715) * (x * x))))


def _rms_norm(x, g):
    return x * lax.rsqrt(jnp.mean(x * x, axis=-1, keepdims=True) + EPS) * g


def _layer_norm(x, g, b, eps):
    xc = x - jnp.mean(x, axis=-1, keepdims=True)
    var = jnp.mean(xc * xc, axis=-1, keepdims=True)
    return xc * lax.rsqrt(var + eps) * g + b


def _softplus(x):
    return jnp.maximum(x, 0.0) + jnp.log(1.0 + jnp.exp(-jnp.abs(x)))


def _lru_gates(xc, wbd_ref, b_a, b_x, lam_half):
    d = xc.shape[-1]
    xcb = xc.astype(BF16)
    r_parts, i_parts = [], []
    for q in range(d // 256):
        ri = _dot(xcb[:, q * 256:(q + 1) * 256], wbd_ref[q])
        r_parts.append(ri[:, :256])
        i_parts.append(ri[:, 256:])
    r2 = 1.0 + jnp.tanh(jnp.concatenate(r_parts, axis=-1) + b_a)
    i2 = 1.0 + jnp.tanh(jnp.concatenate(i_parts, axis=-1) + b_x)
    a = jnp.exp(lam_half * r2)
    mult_half = jnp.sqrt(0.25 - 0.25 * (a * a))
    return a, mult_half, i2 * xc


def _shift_rows(x, hist, j):
    sh = pltpu.roll(x, j, axis=0)
    top = jnp.where(lax.broadcasted_iota(jnp.int32, (SUBLANES, 1), 0) < j,
                    pltpu.roll(hist, j, axis=0), sh[:SUBLANES])
    return jnp.concatenate([top, sh[SUBLANES:]], axis=0)


def _mixer_prompt_kernel(x_ref, norm_ref, win_ref, lng_ref, lnb_ref, ws_ref, bs_ref,
                         cw_ref, cb_ref, lam_ref, wbd_ref, ba_ref, bx_ref, wout_ref,
                         y_ref, h_ref, conv_ref, v_ref,
                         hist_s, a_s, b_s, ya_s, gb_s, hc_s):
    i = pl.program_id(0)
    last = pl.num_programs(0) - 1
    nseq, _, d = x_ref.shape
    nlt = d // LANES
    spt = SEQS_PER_TRIP
    pair = spt * CHUNK

    @pl.when(i == 0)
    def _():
        hist_s[...] = jnp.zeros_like(hist_s)
        hc_s[...] = jnp.zeros_like(hc_s)

    lam_half = (-0.5 * LRU_C) * _softplus(-lam_ref[...])
    is_first = i == 0

    def branch_body(p, c):
        row = pl.multiple_of(p * pair, pair)
        x = jnp.concatenate([x_ref[spt * p + k] for k in range(spt)], axis=0)
        xnb = _rms_norm(x, norm_ref[...]).astype(BF16)
        v = _layer_norm(_gelu2(_dot(xnb, win_ref[:, d:2 * d])), lng_ref[...], lnb_ref[...],
                        4.0 * EPS)

        @pl.when(i == last)
        def _():
            for k in range(spt):
                v_ref[spt * p + k] = v[k * CHUNK:(k + 1) * CHUNK]

        vb = v.astype(BF16)
        dh = d // H_A
        z_rows = []
        for k in range(spt):
            zs = [_dot(ws_ref[h], vb[k * CHUNK:(k + 1) * CHUNK, h * dh:(h + 1) * dh])
                  for h in range(H_A)]
            z_rows.append(jnp.concatenate(zs, axis=-1) + bs_ref[...])
        z = jnp.concatenate(z_rows, axis=0)
        ya = _gelu2(_dot(xnb, win_ref[:, 0:d])) * z
        ga2 = 1.0 + jnp.tanh(_dot(xnb, win_ref[:, 3 * d:4 * d]))
        ya_s[pl.ds(row, pair), :] = (ga2 * ya).astype(BF16)
        gb_s[pl.ds(row, pair), :] = (1.0 + jnp.tanh(_dot(xnb, win_ref[:, 4 * d:5 * d]))).astype(BF16)
        xr = _dot(xnb, win_ref[:, 2 * d:3 * d])
        xcs = []
        for k in range(spt):
            s = spt * p + k
            xk = xr[k * CHUNK:(k + 1) * CHUNK]
            hist = hist_s[s]
            xc = cb_ref[...] + xk * cw_ref[CONV_W - 1:CONV_W, :]
            for j in range(1, CONV_W):
                xc = xc + _shift_rows(xk, hist, j) * cw_ref[CONV_W - 1 - j:CONV_W - j, :]
            hist_s[s] = xk[CHUNK - SUBLANES:]
            xcs.append(xc)
        xc = jnp.concatenate(xcs, axis=0)
        a, mult_half, ix2 = _lru_gates(xc, wbd_ref, ba_ref[...], bx_ref[...], lam_half)
        bt = mult_half * ix2
        for k in range(spt):
            s = spt * p + k
            for j in range(nlt):
                a_s[j, pl.ds(s, CHUNK, stride=nseq), :] = a[k * CHUNK:(k + 1) * CHUNK, j * LANES:(j + 1) * LANES]
                b_s[j, pl.ds(s, CHUNK, stride=nseq), :] = bt[k * CHUNK:(k + 1) * CHUNK, j * LANES:(j + 1) * LANES]

        @pl.when(is_first)
        def _():
            for k in range(spt):
                first = 0.5 * ix2[k * CHUNK:k * CHUNK + 1]
                for j in range(nlt):
                    b_s[j, pl.ds(spt * p + k, 1), :] = first[:, j * LANES:(j + 1) * LANES]

        return c

    lax.fori_loop(0, nseq // spt, branch_body, 0)

    def scan_body(tt, hs):
        for u in range(SCAN_UNROLL):
            row = pl.multiple_of((tt * SCAN_UNROLL + u) * nseq, nseq)
            out = []
            for j in range(nlt):
                h = a_s[j, pl.ds(row, nseq), :] * hs[j] + b_s[j, pl.ds(row, nseq), :]
                b_s[j, pl.ds(row, nseq), :] = h
                out.append(h)
            hs = tuple(out)
        return hs

    h0 = tuple(hc_s[:, j * LANES:(j + 1) * LANES] for j in range(nlt))
    hs = lax.fori_loop(0, CHUNK // SCAN_UNROLL, scan_body, h0)
    hfin = jnp.concatenate(hs, axis=-1)
    hc_s[...] = hfin
    h_ref[...] = hfin

    def out_body(p, c):
        row = pl.multiple_of(p * pair, pair)
        hrows = []
        for k in range(spt):
            hrows.append(jnp.concatenate(
                [b_s[j, pl.ds(spt * p + k, CHUNK, stride=nseq), :] for j in range(nlt)], axis=-1))
        h = jnp.concatenate(hrows, axis=0)
        merged = (ya_s[pl.ds(row, pair), :].astype(F32)
                  + gb_s[pl.ds(row, pair), :].astype(F32) * h).astype(BF16)
        o = _dot(merged, wout_ref[...])
        for k in range(spt):
            y_ref[spt * p + k] = x_ref[spt * p + k] + o[k * CHUNK:(k + 1) * CHUNK]
        return c

    lax.fori_loop(0, nseq // spt, out_body, 0)

    @pl.when(i == last)
    def _():
        conv_ref[...] = hist_s[:, SUBLANES - (CONV_W - 1):, :]


def _const_spec(shape):
    nd = len(shape)
    return pl.BlockSpec(shape, lambda i, *_, _n=nd: (0,) * _n, pipeline_mode=pl.Buffered(1))


def _layer_spec(stacked, layer):
    shape = stacked.shape[1:]
    nd = len(shape)
    return pl.BlockSpec((None,) + tuple(shape), lambda i, *_, _n=nd: (layer,) + (0,) * _n,
                        pipeline_mode=pl.Buffered(1))


MIXER_WEIGHTS = ('norm1', 'w_in', 'ln_g', 'ln_b', 'ws', 'bs', 'conv_w', 'conv_b', 'lam', 'wbd',
                 'b_a', 'b_x', 'w_out')


def _mixer_prompt(x, wts, layer):
    nseq, seq, d = x.shape
    assert seq % CHUNK == 0 and nseq == SUBLANES and d % 256 == 0
    n_chunks = seq // CHUNK
    nlt = d // LANES
    weights = tuple(wts[k] for k in MIXER_WEIGHTS)
    x_spec = pl.BlockSpec((nseq, CHUNK, d), lambda i: (0, i, 0))
    out_shape = (
        jax.ShapeDtypeStruct((nseq, seq, d), F32),
        jax.ShapeDtypeStruct((nseq, d), F32),
        jax.ShapeDtypeStruct((nseq, CONV_W - 1, d), F32),
        jax.ShapeDtypeStruct((nseq, CHUNK, d), F32),
    )
    out_specs = (
        x_spec,
        _const_spec((nseq, d)),
        _const_spec((nseq, CONV_W - 1, d)),
        _const_spec((nseq, CHUNK, d)),
    )
    scratch = [
        pltpu.VMEM((nseq, SUBLANES, d), F32),
        pltpu.VMEM((nlt, CHUNK * nseq, LANES), F32),
        pltpu.VMEM((nlt, CHUNK * nseq, LANES), F32),
        pltpu.VMEM((nseq * CHUNK, d), BF16),
        pltpu.VMEM((nseq * CHUNK, d), BF16),
        pltpu.VMEM((nseq, d), F32),
    ]
    return pl.pallas_call(
        _mixer_prompt_kernel,
        grid=(n_chunks,),
        in_specs=[x_spec] + [_layer_spec(w, layer) for w in weights],
        out_specs=out_specs,
        out_shape=out_shape,
        scratch_shapes=scratch,
        compiler_params=pltpu.CompilerParams(
            dimension_semantics=("arbitrary",), vmem_limit_bytes=VMEM_LIMIT),
        name="mixer_prompt",
    )(x, *weights)


def _mixer_sample_kernel(x_ref, h0_ref, cst_ref, norm_ref, win_ref, lng_ref, lnb_ref, wsc_ref,
                         bsc_ref, cw_ref, cb_ref, lam_ref, wbd_ref, ba_ref, bx_ref, wout_ref,
                         y_ref, h_ref, conv_ref, v_ref):
    nt, ns, d = x_ref.shape
    at = lambda a, t: a[t * ns:(t + 1) * ns]
    lam_half = (-0.5 * LRU_C) * _softplus(-lam_ref[...])
    x = x_ref[...].reshape(nt * ns, d)
    xnb = _rms_norm(x, norm_ref[...]).astype(BF16)
    v = _layer_norm(_gelu2(_dot(xnb, win_ref[:, d:2 * d])), lng_ref[...], lnb_ref[...], 4.0 * EPS)
    v_ref[...] = v.reshape(nt, ns, d)
    zs = []
    for t in range(nt):
        z = bsc_ref[t:t + 1, :]
        for s in range(t + 1):
            z = z + wsc_ref[t, s:s + 1, :] * at(v, s)
        zs.append(z)
    ya = _gelu2(_dot(xnb, win_ref[:, 0:d])) * jnp.concatenate(zs, axis=0)
    ga2 = 1.0 + jnp.tanh(_dot(xnb, win_ref[:, 3 * d:4 * d]))
    gb2 = 1.0 + jnp.tanh(_dot(xnb, win_ref[:, 4 * d:5 * d]))
    xr = _dot(xnb, win_ref[:, 2 * d:3 * d])
    xp = [cst_ref[k] for k in range(CONV_W - 1)] + [at(xr, t) for t in range(nt)]
    xcs = []
    for t in range(nt):
        xc = cb_ref[...]
        for k in range(CONV_W):
            xc = xc + xp[t + k] * cw_ref[k:k + 1, :]
        xcs.append(xc)
    a, mult_half, ix2 = _lru_gates(jnp.concatenate(xcs, axis=0), wbd_ref, ba_ref[...],
                                   bx_ref[...], lam_half)
    bt = mult_half * ix2
    h = h0_ref[...]
    hs = []
    for t in range(nt):
        h = at(a, t) * h + at(bt, t)
        hs.append(h)
    merged = (ga2 * ya + gb2 * jnp.concatenate(hs, axis=0)).astype(BF16)
    y_ref[0:nt] = (x + _dot(merged, wout_ref[...])).reshape(nt, ns, d)
    if y_ref.shape[0] > nt:
        y_ref[nt:] = jnp.zeros((y_ref.shape[0] - nt, ns, d), F32)
    h_ref[...] = h
    for k in range(CONV_W - 1):
        conv_ref[k] = xp[nt + k]


def _mixer_sample(x, h0, conv_state, wts, layer, nt, pad_t):
    _, ns, d = x.shape
    names = MIXER_WEIGHTS[:4] + ('wsc', 'bsc') + MIXER_WEIGHTS[6:]
    stacked = (h0, conv_state) + tuple(wts[k] for k in names)
    out_shape = (
        jax.ShapeDtypeStruct((nt + pad_t, ns, d), F32),
        jax.ShapeDtypeStruct((ns, d), F32),
        jax.ShapeDtypeStruct((CONV_W - 1, ns, d), F32),
        jax.ShapeDtypeStruct((nt, ns, d), F32),
    )
    return pl.pallas_call(
        _mixer_sample_kernel,
        grid=(1,),
        in_specs=[_const_spec((nt, ns, d))] + [_layer_spec(w, layer) for w in stacked],
        out_specs=tuple(_const_spec(s.shape) for s in out_shape),
        out_shape=out_shape,
        compiler_params=pltpu.CompilerParams(
            dimension_semantics=("arbitrary",), vmem_limit_bytes=VMEM_LIMIT),
        name="mixer_sample",
    )(x, *stacked)


def _to_slab(ref, val):
    t, d = val.shape
    nlt = d // LANES
    for j in range(nlt):
        ref[pl.ds(j, t, stride=nlt), :] = val[:, j * LANES:(j + 1) * LANES]


def _from_slab(ref, t, nlt):
    return jnp.concatenate([ref[pl.ds(j, t, stride=nlt), :] for j in range(nlt)], axis=-1)


def _tokens_from_slab(slab_ref, nlt):
    return slab_ref[...].reshape(slab_ref.shape[0] // nlt, nlt, LANES).astype(BF16)


def _tokens_to_slab(slab_ref, tokens):
    slab_ref[...] = tokens.astype(F32).reshape(slab_ref.shape)


def _router_kernel(yp_ref, ys_ref, norm_ref, wr_ref, br_ref, tri_ref, low_ref, xn_ref, s1_ref,
                   s2_ref, w1_ref, w2_ref, cnt_ref, slab_s, *, n_prompt_tiles):
    y = jnp.where(pl.program_id(0) < n_prompt_tiles, yp_ref[...], ys_ref[...])
    xn = _rms_norm(y, norm_ref[...])
    _to_slab(slab_s, xn)
    xn_ref[...] = _tokens_from_slab(slab_s, xn_ref.shape[1])
    lt = lax.dot_general(wr_ref[...], xn.astype(BF16), (((1,), (1,)), ((), ())),
                         preferred_element_type=F32) + br_ref[...]
    tile = lt.shape[1]
    row = lax.broadcasted_iota(jnp.int32, (SUBLANES, tile), 0).astype(F32)
    neg = jnp.float32(-jnp.inf)
    big = jnp.float32(SUBLANES)

    def argmax_first(vals):
        m = jnp.max(vals, axis=0, keepdims=True)
        return m, jnp.min(jnp.where(vals == m, row, big), axis=0, keepdims=True)

    def group_block(x, gi):
        out = x[SUBLANES:2 * SUBLANES]
        for g in range(1, N_GROUPS):
            out = jnp.where(gi == g, x[(g + 1) * SUBLANES:(g + 2) * SUBLANES], out)
        return out

    gl = jnp.where(row < N_GROUPS, lt[0:SUBLANES], neg)
    gm, gi = argmax_first(gl)
    pg_top = 1.0 / jnp.sum(jnp.exp(gl - gm), axis=0, keepdims=True)
    el = group_block(lt, gi)
    m1, i1 = argmax_first(el)
    m2, i2 = argmax_first(jnp.where(row == i1, neg, el))
    e2 = jnp.exp(m2 - m1)
    w1 = pg_top / (1.0 + e2)
    w2 = pg_top * e2 / (1.0 + e2)
    hot1 = row == i1
    hot2 = row == i2
    picked = jnp.where(jnp.logical_or(hot1, hot2), 1.0, 0.0)
    zero = jnp.zeros_like(picked)
    onehot = jnp.concatenate(
        [zero] + [jnp.where(gi == g, picked, 0.0) for g in range(N_GROUPS)] + [zero],
        axis=0).astype(BF16)
    span = tri_ref.shape[0]
    parts = []
    cnt = jnp.zeros((ROUTE_ROWS, 1), F32)
    for k in range(tile // span):
        blk = onehot[:, k * span:(k + 1) * span]
        parts.append(_dot(blk, tri_ref[...]) + cnt)
        cnt = cnt + jnp.sum(blk.astype(F32), axis=1, keepdims=True)
    before = jnp.concatenate(parts, axis=1)
    lower = jnp.sum(_dot(low_ref[...], onehot), axis=1, keepdims=True)
    slot = group_block(before + lower, gi)
    p1 = jnp.sum(jnp.where(hot1, slot, 0.0), axis=0, keepdims=True)
    p2 = jnp.sum(jnp.where(hot2, slot, 0.0), axis=0, keepdims=True)
    s1_ref[0] = p1.astype(jnp.int32)
    s2_ref[0] = p2.astype(jnp.int32)
    w1_ref[0] = w1
    w2_ref[0] = w2
    cnt_ref[0] = jnp.broadcast_to(cnt, cnt_ref.shape[1:])


def _token_specs(tile, d, n_prompt_tiles):
    last = n_prompt_tiles - 1
    return (pl.BlockSpec((tile, d), lambda i, *_: (jnp.minimum(i, last), 0)),
            pl.BlockSpec((tile, d), lambda i, *_: (jnp.maximum(i - n_prompt_tiles, 0), 0)))


def _router(yp, ys, wts, layer, tile):
    d = yp.shape[1]
    nlt = d // LANES
    n_prompt_tiles = yp.shape[0] // tile
    n_tiles = n_prompt_tiles + ys.shape[0] // tile
    span = min(tile, ROUTE_SPAN)
    assert tile % span == 0
    tri = jnp.triu(jnp.ones((span, span), BF16), 1)
    low = jnp.tril(jnp.ones((ROUTE_ROWS, ROUTE_ROWS), BF16), -1)
    per_token = pl.BlockSpec((1, 1, tile), lambda i: (i, 0, 0))
    return pl.pallas_call(
        functools.partial(_router_kernel, n_prompt_tiles=n_prompt_tiles),
        grid=(n_tiles,),
        in_specs=list(_token_specs(tile, d, n_prompt_tiles)) + [
            _layer_spec(wts['norm2'], layer), _layer_spec(wts['w_route'], layer),
            _layer_spec(wts['b_route'], layer), _const_spec((span, span)),
            _const_spec((ROUTE_ROWS, ROUTE_ROWS))],
        out_specs=(pl.BlockSpec((tile, nlt, LANES), lambda i: (i, 0, 0)),
                   per_token, per_token, per_token, per_token,
                   pl.BlockSpec((1, ROUTE_ROWS, LANES), lambda i: (i, 0, 0))),
        out_shape=(jax.ShapeDtypeStruct((n_tiles * tile, nlt, LANES), BF16),
                   jax.ShapeDtypeStruct((n_tiles, 1, tile), jnp.int32),
                   jax.ShapeDtypeStruct((n_tiles, 1, tile), jnp.int32),
                   jax.ShapeDtypeStruct((n_tiles, 1, tile), F32),
                   jax.ShapeDtypeStruct((n_tiles, 1, tile), F32),
                   jax.ShapeDtypeStruct((n_tiles, ROUTE_ROWS, LANES), F32)),
        scratch_shapes=[pltpu.VMEM((tile * nlt, LANES), F32)],
        compiler_params=pltpu.CompilerParams(
            dimension_semantics=("arbitrary",), vmem_limit_bytes=VMEM_LIMIT),
        name="moe_router",
    )(yp, ys, wts['norm2'], wts['w_route'], wts['b_route'], tri, low)


def _run_copy(tile_idx, e, lo_ref, cnt_ref, dst_ref, local, remote, sem, to_remote):
    k = tile_idx * N_EXPERTS + e
    loc = local.at[pl.ds(lo_ref[k], cnt_ref[k])]
    rem = remote.at[pl.ds(dst_ref[k], cnt_ref[k])]
    return pltpu.make_async_copy(loc, rem, sem) if to_remote else pltpu.make_async_copy(rem, loc, sem)


def _start_runs(tile_idx, lo_ref, cnt_ref, dst_ref, local, remote, sem, to_remote):
    for e in range(N_EXPERTS):
        @pl.when(cnt_ref[tile_idx * N_EXPERTS + e] > 0)
        def _():
            _run_copy(tile_idx, e, lo_ref, cnt_ref, dst_ref, local, remote, sem, to_remote).start()


def _wait_runs(local, sem):
    pltpu.make_async_copy(local, local, sem).wait()


def _dispatch_kernel(lo_ref, cnt_ref, dst_ref, padst_ref, padn_ref,
                     s1_ref, s2_ref, x_ref, xs_ref, loc, zeros, sem, *, unroll):
    i = pl.program_id(0)
    last = pl.num_programs(0) - 1
    tile = x_ref.shape[0]
    slot = i % 2
    buf = loc.at[slot]

    @pl.when(i >= 2)
    def _():
        _wait_runs(buf, sem.at[slot])

    def body(tt, c):
        for u in range(unroll):
            t = tt * unroll + u
            row = x_ref[t]
            buf[s1_ref[t]] = row
            buf[s2_ref[t]] = row
        return c

    lax.fori_loop(0, tile // unroll, body, 0)
    _start_runs(i, lo_ref, cnt_ref, dst_ref, buf, xs_ref, sem.at[slot], True)

    @pl.when(i == last)
    def _():
        zeros[...] = jnp.zeros_like(zeros)

        def pad_copy(e):
            n = padn_ref[e]
            dst = xs_ref.at[pl.ds(padst_ref[e], n)]
            return pltpu.make_async_copy(zeros.at[pl.ds(0, n)], dst, sem.at[2])

        def each_pad(fn):
            def b(e, c):
                @pl.when(padn_ref[e] > 0)
                def _():
                    fn(e)
                return c
            lax.fori_loop(0, padn_ref.shape[0], b, 0)

        each_pad(lambda e: pad_copy(e).start())
        _wait_runs(buf, sem.at[slot])

        @pl.when(i >= 1)
        def _():
            _wait_runs(loc.at[1 - slot], sem.at[1 - slot])

        each_pad(lambda e: pad_copy(e).wait())


def _smem_tile(tile):
    return pl.BlockSpec((None, None, tile), lambda i, *_: (i, 0, 0), memory_space=pltpu.SMEM)


def _dispatch(xn_slab, s1, s2, tables, tile, n_rows):
    token = xn_slab.shape[1:]
    return pl.pallas_call(
        functools.partial(_dispatch_kernel, unroll=16),
        grid_spec=pltpu.PrefetchScalarGridSpec(
            num_scalar_prefetch=5,
            grid=(xn_slab.shape[0] // tile,),
            in_specs=[_smem_tile(tile), _smem_tile(tile),
                      pl.BlockSpec((tile,) + token, lambda i, *_: (i, 0, 0))],
            out_specs=pl.BlockSpec(memory_space=pl.ANY),
            scratch_shapes=[pltpu.VMEM((2, 2 * tile) + token, xn_slab.dtype),
                            pltpu.VMEM((MOE_BLOCK,) + token, xn_slab.dtype),
                            pltpu.SemaphoreType.DMA((3,))]),
        out_shape=jax.ShapeDtypeStruct((n_rows,) + token, xn_slab.dtype),
        compiler_params=pltpu.CompilerParams(
            dimension_semantics=("arbitrary",), vmem_limit_bytes=VMEM_LIMIT),
        name="moe_dispatch",
    )(*tables, s1, s2, xn_slab)


def _experts_kernel(blk_ref, next_ref, nused_ref, xs_ref, wg_ref, wu_ref, wd_ref, ys_ref,
                    xin, yout, wg_f, wu_f, wd_f, wg_s, wu_s, wd_s, slab_in,
                    sem_in, sem_out, sem_w, *, layer):
    n_blocks = ys_ref.shape[0] // MOE_BLOCK
    nlt = xs_ref.shape[1]
    n_used = nused_ref[0]
    block = lambda ref, b: ref.at[pl.ds(pl.multiple_of(b * MOE_BLOCK, MOE_BLOCK), MOE_BLOCK)]
    ring = xin.shape[0]
    rows_in = lambda b: pltpu.make_async_copy(block(xs_ref, b), xin.at[b % ring],
                                              sem_in.at[b % ring])
    rows_out = lambda b, slot: pltpu.make_async_copy(yout.at[slot], block(ys_ref, b),
                                                     sem_out.at[slot])

    def weights(e, slot):
        pairs = ((wg_ref, wg_f), (wu_ref, wu_f), (wd_ref, wd_f))
        return [pltpu.make_async_copy(src.at[layer, e], dst.at[slot], sem_w.at[slot])
                for src, dst in pairs]

    for k in range(ring - 1):
        @pl.when(k < n_used)
        def _():
            rows_in(k).start()

    wring = wg_f.shape[0]

    def later_expert(e, hops):
        for _ in range(hops):
            e = jnp.where(e >= 0, next_ref[jnp.maximum(e, 0)], -1)
        return e

    @pl.when(n_used > 0)
    def _():
        for k in range(wring - 1):
            e_k = later_expert(blk_ref[0], k)

            @pl.when(e_k >= 0)
            def _():
                for c in weights(e_k, k):
                    c.start()

    def step(b, wcount):
        slot = b % 2
        e = blk_ref[b]
        rows_in(b).wait()

        @pl.when(b + ring - 1 < n_used)
        def _():
            rows_in(b + ring - 1).start()

        new_expert = jnp.logical_or(b == 0, e != blk_ref[jnp.maximum(b - 1, 0)])
        wcount = jnp.where(new_expert, wcount + 1, wcount)

        @pl.when(new_expert)
        def _():
            wslot = wcount % wring
            for c in weights(e, wslot):
                c.wait()
            wg_s[...] = wg_f[wslot].astype(BF16)
            wu_s[...] = wu_f[wslot].astype(BF16)
            wd_s[...] = wd_f[wslot].astype(BF16)
            e_far = later_expert(e, wring - 1)

            @pl.when(e_far >= 0)
            def _():
                for c in weights(e_far, (wcount + wring - 1) % wring):
                    c.start()

        _tokens_to_slab(slab_in, xin[b % ring])
        x = _from_slab(slab_in, MOE_BLOCK, nlt).astype(BF16)
        hg = _dot(x, wg_s[...])
        hu = _dot(x, wu_s[...])
        h = (hg * _sigmoid(hg) * hu).astype(BF16)
        o = _dot(h, wd_s[...]).reshape(MOE_BLOCK, nlt, LANES).astype(BF16)

        @pl.when(b >= 2)
        def _():
            rows_out(b - 2, slot).wait()

        yout[slot] = o
        rows_out(b, slot).start()
        return wcount

    lax.fori_loop(0, n_used, step, jnp.int32(-1))

    @pl.when(n_used >= 2)
    def _():
        rows_out(n_used - 2, n_used % 2).wait()

    @pl.when(n_used >= 1)
    def _():
        rows_out(n_used - 1, (n_used - 1) % 2).wait()

    yout[0] = jnp.zeros(yout.shape[1:], yout.dtype)

    def fill(b, c):
        rows_out(b, 0).start()
        rows_out(b, 0).wait()
        return c

    lax.fori_loop(n_used, n_blocks, fill, 0)


def _experts(xs, blk_expert, next_expert, n_used, wts, layer):
    n_rows, nlt, _ = xs.shape
    d = nlt * LANES
    f = wts['w_gate'].shape[-1]
    token = (nlt, LANES)
    any_spec = pl.BlockSpec(memory_space=pl.ANY)
    return pl.pallas_call(
        functools.partial(_experts_kernel, layer=layer),
        grid_spec=pltpu.PrefetchScalarGridSpec(
            num_scalar_prefetch=3,
            grid=(1,),
            in_specs=[any_spec] * 4,
            out_specs=any_spec,
            scratch_shapes=[pltpu.VMEM((EXPERT_ROW_RING, MOE_BLOCK) + token, xs.dtype),
                            pltpu.VMEM((2, MOE_BLOCK) + token, xs.dtype),
                            pltpu.VMEM((EXPERT_WEIGHT_RING, d, f), F32),
                            pltpu.VMEM((EXPERT_WEIGHT_RING, d, f), F32),
                            pltpu.VMEM((EXPERT_WEIGHT_RING, f, d), F32),
                            pltpu.VMEM((d, f), BF16), pltpu.VMEM((d, f), BF16),
                            pltpu.VMEM((f, d), BF16),
                            pltpu.VMEM((MOE_BLOCK * nlt, LANES), F32),
                            pltpu.SemaphoreType.DMA((EXPERT_ROW_RING,)),
                            pltpu.SemaphoreType.DMA((2,)),
                            pltpu.SemaphoreType.DMA((EXPERT_WEIGHT_RING,))]),
        out_shape=jax.ShapeDtypeStruct(xs.shape, xs.dtype),
        compiler_params=pltpu.CompilerParams(
            dimension_semantics=("arbitrary",), vmem_limit_bytes=VMEM_LIMIT),
        name="moe_experts",
    )(blk_expert, next_expert, n_used, xs, wts['w_gate'], wts['w_up'], wts['w_down'])


def _combine_kernel(lo_ref, cnt_ref, dst_ref, s1_ref, s2_ref, w1_ref, w2_ref, yp_ref, ys_ref,
                    nf_ref, rows_ref, op_ref, os_ref, loc, acc, sem,
                    *, unroll, final_norm, n_prompt_tiles):
    i = pl.program_id(0)
    tile = yp_ref.shape[0]
    nlt = loc.shape[2]
    slot = i % 2
    buf = loc.at[slot]
    fetch = lambda t, s: _start_runs(t, lo_ref, cnt_ref, dst_ref, loc.at[s], rows_ref, sem.at[s],
                                     False)

    @pl.when(i == 0)
    def _():
        fetch(0, 0)

    @pl.when(i + 1 < pl.num_programs(0))
    def _():
        fetch(i + 1, 1 - slot)

    _wait_runs(buf, sem.at[slot])

    def body(tt, c):
        for u in range(unroll):
            t = tt * unroll + u
            r1 = buf[s1_ref[t]].astype(F32)
            r2 = buf[s2_ref[t]].astype(F32)
            acc[pl.ds(pl.multiple_of(t * nlt, nlt), nlt), :] = w1_ref[t] * r1 + w2_ref[t] * r2
        return c

    lax.fori_loop(0, tile // unroll, body, 0)
    def finish(y_ref, o_ref):
        out = y_ref[...] + acc[...].reshape(tile, nlt, LANES).reshape(tile, nlt * LANES)
        if final_norm:
            out = _rms_norm(out, nf_ref[...])
        o_ref[...] = out

    is_prompt = i < n_prompt_tiles
    pl.when(is_prompt)(lambda: finish(yp_ref, op_ref))
    pl.when(jnp.logical_not(is_prompt))(lambda: finish(ys_ref, os_ref))


def _combine(yp, ys, rows, meta, tables, norm_f, final_norm, tile):
    d = yp.shape[1]
    token = rows.shape[1:]
    n_prompt_tiles = yp.shape[0] // tile
    n_tiles = n_prompt_tiles + ys.shape[0] // tile
    tok_p, tok_s = _token_specs(tile, d, n_prompt_tiles)
    return pl.pallas_call(
        functools.partial(_combine_kernel, unroll=32, final_norm=final_norm,
                          n_prompt_tiles=n_prompt_tiles),
        grid_spec=pltpu.PrefetchScalarGridSpec(
            num_scalar_prefetch=3,
            grid=(n_tiles,),
            in_specs=[_smem_tile(tile)] * 4 + [tok_p, tok_s, _const_spec((1, d)),
                                               pl.BlockSpec(memory_space=pl.ANY)],
            out_specs=(tok_p, tok_s),
            scratch_shapes=[pltpu.VMEM((2, 2 * tile) + token, rows.dtype),
                            pltpu.VMEM((tile * token[0], LANES), F32),
                            pltpu.SemaphoreType.DMA((2,))]),
        out_shape=(jax.ShapeDtypeStruct(yp.shape, F32), jax.ShapeDtypeStruct(ys.shape, F32)),
        compiler_params=pltpu.CompilerParams(
            dimension_semantics=("arbitrary",), vmem_limit_bytes=VMEM_LIMIT),
        name="moe_combine",
    )(*tables, *meta, yp, ys, norm_f, rows)


def _moe(yp, ys, wts, layer, norm_f, final_norm, tile):
    d = yp.shape[1]
    n = yp.shape[0] + ys.shape[0]
    assert yp.shape[0] % tile == 0 and ys.shape[0] % tile == 0 and d % LANES == 0
    xn_slab, s1, s2, w1, w2, cnt = _router(yp, ys, wts, layer, tile)
    cnt = cnt[:, SUBLANES:SUBLANES + N_EXPERTS, 0].astype(jnp.int32)
    lo = jnp.cumsum(cnt, axis=1) - cnt
    seg = jnp.sum(cnt, axis=0)
    seg_pad = (seg + MOE_BLOCK - 1) // MOE_BLOCK * MOE_BLOCK
    seg_end = jnp.cumsum(seg_pad)
    seg_start = seg_end - seg_pad
    dst = seg_start[None, :] + jnp.cumsum(cnt, axis=0) - cnt
    n_blocks = (2 * n + N_EXPERTS * (MOE_BLOCK - 1)) // MOE_BLOCK
    blk_ids = jnp.arange(n_blocks, dtype=jnp.int32)
    blk_expert = jnp.minimum(
        jnp.sum((seg_end[None, :] <= blk_ids[:, None] * MOE_BLOCK).astype(jnp.int32), axis=1),
        N_EXPERTS - 1)
    n_used = seg_end[-1:] // MOE_BLOCK
    pad_start = jnp.concatenate([seg_start + seg, blk_ids * MOE_BLOCK])
    pad_rows = jnp.concatenate([seg_pad - seg, jnp.where(blk_ids >= n_used[0], MOE_BLOCK, 0)])
    flat = lambda a: a.reshape(-1)
    runs = (flat(lo), flat(cnt), flat(dst))
    ids = jnp.arange(N_EXPERTS, dtype=jnp.int32)
    later = jnp.where((seg_pad[None, :] > 0) & (ids[None, :] > ids[:, None]), ids[None, :],
                      N_EXPERTS)
    next_expert = jnp.min(later, axis=1)
    next_expert = jnp.where(next_expert < N_EXPERTS, next_expert, -1)
    xs = _dispatch(xn_slab, s1, s2, runs + (pad_start, pad_rows), tile, n_blocks * MOE_BLOCK)
    rows = _experts(xs, blk_expert, next_expert, n_used, wts, layer)
    return _combine(yp, ys, rows, (s1, s2, w1, w2), runs, norm_f, final_norm, tile)


def _prep_weights(norm1, w_in, ln_g, ln_b, w_s, b_s, conv_w, conv_b, lru_lambda, w_rg_a, b_rg_a,
                  w_rg_x, b_rg_x, w_out, norm2, w_route_group, b_route_group, w_route_expert,
                  b_route_expert, w_gate, w_up, w_down, n_sample_t):
    depth, d = w_in.shape[0], w_in.shape[1]
    dh = d // H_A
    row = lambda a: a.reshape(depth, 1, -1).astype(F32)
    causal = jnp.tril(jnp.ones((CHUNK, CHUNK), dtype=bool))
    ws = jnp.where(causal, 0.5 * w_s, 0.0)
    bs = jnp.repeat(jnp.swapaxes(0.5 * b_s, 1, 2), dh, axis=2)
    in_scale = jnp.concatenate([jnp.ones((3 * d,), F32), jnp.full((2 * d,), 0.5, F32)])

    def pair_bd(w):
        w = w.reshape(depth, H_B // 2, 2, w.shape[-2], w.shape[-1])
        z = jnp.zeros_like(w[:, :, 0])
        return jnp.concatenate([jnp.concatenate([w[:, :, 0], z], axis=-1),
                                jnp.concatenate([z, w[:, :, 1]], axis=-1)], axis=-2)

    gap = SUBLANES - N_GROUPS
    tail = ROUTE_ROWS - SUBLANES - N_EXPERTS
    w_route = jnp.concatenate(
        [jnp.swapaxes(w_route_group, 1, 2), jnp.zeros((depth, gap, d), F32),
         jnp.swapaxes(w_route_expert, 1, 2), jnp.zeros((depth, tail, d), F32)], axis=1)
    b_route = jnp.concatenate(
        [b_route_group, jnp.zeros((depth, gap), F32), b_route_expert,
         jnp.zeros((depth, tail), F32)], axis=1)[..., None]
    nt = n_sample_t
    return dict(
        norm1=row(norm1), w_in=(w_in * in_scale).astype(BF16), ln_g=row(ln_g), ln_b=row(ln_b),
        ws=ws.astype(BF16), bs=bs.astype(F32), conv_w=conv_w.astype(F32), conv_b=row(conv_b),
        lam=row(lru_lambda),
        wbd=(0.5 * jnp.concatenate([pair_bd(w_rg_a), pair_bd(w_rg_x)], axis=-1)).astype(BF16),
        b_a=row(0.5 * b_rg_a), b_x=row(0.5 * b_rg_x), w_out=(0.5 * w_out).astype(BF16),
        norm2=row(norm2), w_route=w_route.astype(BF16), b_route=b_route.astype(F32),
        w_gate=w_gate, w_up=w_up, w_down=w_down,
        wsc=jnp.repeat(jnp.transpose(ws[:, :, :nt, :nt], (0, 2, 3, 1)), dh, axis=3).astype(F32),
        bsc=bs[:, :nt].astype(F32),
    )


def kernel(x_prompt, x_sample, state_lru_h, state_conv, norm1, w_in, ln_g, ln_b, w_s, b_s, conv_w, conv_b, lru_lambda, w_rg_a, b_rg_a, w_rg_x, b_rg_x, w_out, norm2, w_route_group, b_route_group, w_route_expert, b_route_expert, w_gate, w_up, w_down, norm_f):
    depth = w_in.shape[0]
    nb, seq, d = x_prompt.shape
    ns, nt, _ = x_sample.shape
    assert nt <= CHUNK
    nf = norm_f.reshape(1, d).astype(F32)
    wts = _prep_weights(norm1, w_in, ln_g, ln_b, w_s, b_s, conv_w, conv_b, lru_lambda, w_rg_a,
                        b_rg_a, w_rg_x, b_rg_x, w_out, norm2, w_route_group, b_route_group,
                        w_route_expert, b_route_expert, w_gate, w_up, w_down, nt)
    conv_state = jnp.transpose(state_conv, (0, 2, 1, 3))
    assert (nb * seq) % MOE_TILE == 0
    sample_pad = (-nt * ns) % MOE_TILE
    assert sample_pad % ns == 0
    pad_t = sample_pad // ns
    yp = x_prompt
    ys = jnp.transpose(x_sample, (1, 0, 2))
    hp, cp, vp, hs, cs, vs = [], [], [], [], [], []
    for l in range(depth):
        yp, h, c, v = _mixer_prompt(yp, wts, l)
        hp.append(h)
        cp.append(c)
        vp.append(v)
        ys, h, c, v = _mixer_sample(ys, state_lru_h, conv_state, wts, l, nt, pad_t)
        hs.append(h)
        cs.append(c)
        vs.append(v)
        yp, ys = _moe(yp.reshape(nb * seq, d), ys.reshape((nt + pad_t) * ns, d), wts, l, nf,
                      l == depth - 1, MOE_TILE)
        yp = yp.reshape(nb, seq, d)
        ys = ys.reshape(nt + pad_t, ns, d)
    to_seq_major = lambda a: jnp.transpose(jnp.stack(a), (0, 2, 1, 3))
    return (yp, jnp.transpose(ys[:nt], (1, 0, 2)), jnp.stack(hp), jnp.stack(cp), jnp.stack(vp),
            jnp.stack(hs), to_seq_major(cs), to_seq_major(vs))
```
